```python
import math
import jax
import jax.numpy as jnp
from jax import lax
import numpy as np

D_MODEL = 1024
BATCH = 2
SEQ = 8192
DEPTH = 2

CTX_LEN = 256
GRID_W = 64
N_BRANCH = 4
BRANCH_W = D_MODEL // N_BRANCH
NA_HEADS = 4
NA_DIM = BRANCH_W // NA_HEADS
NA_WIN_H = 8
NA_WIN_W = 16
NA_QB = 16
NA_KB = 32
DA_HEADS = 4
DA_QK_DIM = BRANCH_W // (2 * DA_HEADS)
DA_V_DIM = 2 * DA_QK_DIM
DA_QBLOCK = 128
ROPE_BASE = 10000.0
GDN_HEADS = 4
GDN_DIM = BRANCH_W // GDN_HEADS
GDN_CONV = 5
GDN_CHUNK = 64
FNET_GROUPS = 4
FNET_GROUP_W = BRANCH_W // FNET_GROUPS
N_EXPERTS = 16
N_GROUPS = 4
TOP_K = 2
D_FF = 512
MOE_BLOCK = 128
RMS_EPS = 1e-6
NEG_INF = -1e30
IN_SPLITS = (3 * BRANCH_W, 3 * BRANCH_W, 3 * BRANCH_W, BRANCH_W, 4 * GDN_HEADS, BRANCH_W, N_BRANCH * D_MODEL)
D_IN = sum(IN_SPLITS)

kernel_name = 'hybrid_diffusion_trunk'


def rmsnorm(x, g):
    xf = x.astype(jnp.float32)
    y = xf * lax.rsqrt(jnp.mean(xf * xf, axis=-1, keepdims=True) + RMS_EPS)
    return (y * g.astype(jnp.float32)).astype(x.dtype)


def l2norm(x):
    xf = x.astype(jnp.float32)
    return xf * lax.rsqrt(jnp.sum(xf * xf, axis=-1, keepdims=True) + RMS_EPS)


def modulate(h, shift, scale):
    return h * (1.0 + scale) + shift


def softmax_f32(s):
    return jax.nn.softmax(s.astype(jnp.float32), axis=-1)


def split_cols(p):
    outs, o = [], 0
    for w in IN_SPLITS:
        outs.append(p[..., o:o + w])
        o += w
    return outs


def rope_2d(x, rows, cols):
    nf = x.shape[-1] // 4
    freqs = ROPE_BASE ** (-jnp.arange(nf, dtype=jnp.float32) / nf)
    ang = jnp.stack([rows[:, None] * freqs, cols[:, None] * freqs], axis=1)
    cos = jnp.cos(ang)[None, :, None, None]
    sin = jnp.sin(ang)[None, :, None, None]
    xf = x.astype(jnp.float32).reshape(*x.shape[:-1], 2, 2, nf)
    x1, x2 = xf[..., 0, :], xf[..., 1, :]
    out = jnp.stack([x1 * cos - x2 * sin, x2 * cos + x1 * sin], axis=-2)
    return out.reshape(x.shape).astype(x.dtype)


def dense_ctx_attention(qc, kc, vc):
    p = softmax_f32(jnp.einsum('bqhd,bkhd->bhqk', qc, kc) * qc.shape[-1] ** -0.5)
    return jnp.einsum('bhqk,bkhd->bqhd', p.astype(vc.dtype), vc)


def neighbourhood_attention(q, k, v, qc, kc, vc, rpb, ctx_out):
    B, S, H, Dh = q.shape
    L = qc.shape[1]
    rows = S // GRID_W
    wh = min(NA_WIN_H, rows)
    ncb = GRID_W // NA_QB
    scale = Dh ** -0.5
    r = jnp.arange(rows)
    row_idx = jnp.clip(r - wh // 2, 0, rows - wh)[:, None] + jnp.arange(wh)[None, :]
    qcol = jnp.arange(GRID_W).reshape(ncb, NA_QB)
    col_idx = jnp.clip(qcol[:, 0] - NA_WIN_W // 2, 0, GRID_W - NA_KB)[:, None] + jnp.arange(NA_KB)[None, :]
    win_start = jnp.clip(qcol - NA_WIN_W // 2, 0, GRID_W - NA_WIN_W)
    rel = col_idx[:, None, :] - win_start[..., None]
    col_mask = (rel >= 0) & (rel < NA_WIN_W)
    rel_r = row_idx - r[:, None] + NA_WIN_H - 1
    rel_c = jnp.clip(col_idx[:, None, :] - qcol[..., None] + NA_WIN_W - 1, 0, 2 * NA_WIN_W - 2)
    bias = rpb.astype(jnp.float32)[:, rel_r[:, None, None, :, None], rel_c[None, :, :, None, :]]
    bias = jnp.where(col_mask[None, None, :, :, None, :], bias, NEG_INF)
    nk = wh * NA_KB
    bias = bias.reshape(H, rows, ncb, NA_QB, nk)
    kw = k.reshape(B, rows, GRID_W, H, Dh)[:, row_idx[:, None, :, None], col_idx[None, :, None, :]]
    vw = v.reshape(B, rows, GRID_W, H, Dh)[:, row_idx[:, None, :, None], col_idx[None, :, None, :]]
    kw = kw.reshape(B, rows, ncb, nk, H, Dh)
    vw = vw.reshape(B, rows, ncb, nk, H, Dh)
    qb = q.reshape(B, rows, ncb, NA_QB, H, Dh)
    s_loc = jnp.einsum('brjqhd,brjkhd->bhrjqk', qb, kw).astype(jnp.float32) * scale + bias[None]
    s_ctx = jnp.einsum('brjqhd,bchd->bhrjqc', qb, kc).astype(jnp.float32) * scale
    p = softmax_f32(jnp.concatenate([s_loc, s_ctx], axis=-1)).astype(v.dtype)
    o = (jnp.einsum('bhrjqk,brjkhd->brjqhd', p[..., :nk], vw)
         + jnp.einsum('bhrjqc,bchd->brjqhd', p[..., nk:], vc)).reshape(B, S, H * Dh)
    o_c = dense_ctx_attention(qc, kc, vc).reshape(B, L, H * Dh) if ctx_out else None
    return o, o_c


def diff_attention(q, k, v, qc, kc, vc, lam, ctx_out):
    B, S, H, _, dq = q.shape
    scale = dq ** -0.5
    k_all = jnp.concatenate([k, kc], axis=1)
    v_all = jnp.concatenate([v, vc], axis=1)
    nb = S // DA_QBLOCK
    qb = jnp.moveaxis(q.reshape(B, nb, DA_QBLOCK, H, 2, dq), 1, 0)

    def block(qblk):
        p = softmax_f32(jnp.einsum('bqhmd,bkhmd->bhmqk', qblk, k_all) * scale)
        a = p[:, :, 0] - lam * p[:, :, 1]
        return jnp.einsum('bhqk,bkhd->bqhd', a.astype(v_all.dtype), v_all)

    o = jnp.moveaxis(lax.map(block, qb), 0, 1).reshape(B, S, H, v.shape[-1])
    o_c = None
    if ctx_out:
        p = softmax_f32(jnp.einsum('bqhmd,bkhmd->bhmqk', qc, kc) * scale)
        a = p[:, :, 0] - lam * p[:, :, 1]
        o_c = jnp.einsum('bhqk,bkhd->bqhd', a.astype(vc.dtype), vc)
    return o, o_c


def short_conv(x, w):
    K = w.shape[0]
    pad = K // 2
    T = x.shape[1]
    xp = jnp.pad(x, ((0, 0), (pad, pad), (0, 0)))
    y = xp[:, 0:T] * w[0]
    for i in range(1, K):
        y = y + xp[:, i:i + T] * w[i]
    return jax.nn.silu(y)


def gdn_prepare(qkv, ab, conv_w, a_log, dt_bias):
    B, T, _ = qkv.shape
    q, k, v = jnp.split(short_conv(qkv, conv_w), 3, axis=-1)
    to_heads = lambda u: jnp.moveaxis(u.reshape(B, T, GDN_HEADS, GDN_DIM), 1, 2)
    q = l2norm(to_heads(q)) * GDN_DIM ** -0.5
    k = l2norm(to_heads(k))
    v = to_heads(v).astype(jnp.float32)
    abf = ab.astype(jnp.float32).reshape(B, T, 2, 2, GDN_HEADS)
    g = -jnp.exp(a_log.astype(jnp.float32)) * jax.nn.softplus(abf[:, :, 0] + dt_bias.astype(jnp.float32))
    beta = jax.nn.sigmoid(abf[:, :, 1])
    g = jnp.transpose(g, (2, 0, 3, 1))
    beta = jnp.transpose(beta, (2, 0, 3, 1))
    return q, k, v, g, beta


def gdn_chunked(q, k, v, g, beta, s0):
    B, H, T, dk = q.shape
    dv = v.shape[-1]
    C = GDN_CHUNK
    N = T // C
    q, k, v = (u.reshape(B, H, N, C, -1) for u in (q, k, v))
    g = g.reshape(B, H, N, C)
    beta = beta.reshape(B, H, N, C)
    G = jnp.cumsum(g, axis=-1)
    incl = jnp.tril(jnp.ones((C, C), dtype=bool))
    strict = jnp.tril(jnp.ones((C, C), dtype=bool), -1)
    decay = jnp.where(incl, jnp.exp(jnp.where(incl, G[..., :, None] - G[..., None, :], 0.0)), 0.0)
    kb = k * beta[..., None]
    a = jnp.where(strict, jnp.einsum('bhnid,bhnjd->bhnij', kb, k) * decay, 0.0)
    lhs = a + jnp.eye(C, dtype=a.dtype)
    u = lax.linalg.triangular_solve(lhs, v * beta[..., None], left_side=True, lower=True, unit_diagonal=True)
    w = lax.linalg.triangular_solve(lhs, kb * jnp.exp(G)[..., None], left_side=True, lower=True, unit_diagonal=True)
    qk = jnp.where(incl, jnp.einsum('bhnid,bhnjd->bhnij', q, k) * decay, 0.0)
    q_dec = q * jnp.exp(G)[..., None]
    k_dec = k * jnp.exp(G[..., -1:] - G)[..., None]
    g_last = jnp.exp(G[..., -1])

    def step(S, xs):
        qk_c, qd_c, kd_c, u_c, w_c, gl_c = xs
        v_new = u_c - jnp.einsum('bhcd,bhde->bhce', w_c, S)
        o = jnp.einsum('bhcd,bhde->bhce', qd_c, S) + jnp.einsum('bhij,bhje->bhie', qk_c, v_new)
        S = S * gl_c[..., None, None] + jnp.einsum('bhcd,bhce->bhde', kd_c, v_new)
        return S, o

    xs = tuple(jnp.moveaxis(t, 2, 0) for t in (qk, q_dec, k_dec, u, w, g_last))
    s_final, o = lax.scan(step, s0, xs)
    return jnp.moveaxis(o, 0, 2).reshape(B, H, T, dv), s_final


def gdn_mixer(qkv, ab, z, qkv_c, ab_c, z_c, conv_w, a_log, dt_bias, onorm_g, ctx_out):
    ql, kl, vl, gl, bl = gdn_prepare(qkv, ab, conv_w, a_log, dt_bias)
    qc, kc, vc, gc, bc = gdn_prepare(qkv_c, ab_c, conv_w, a_log, dt_bias)
    B = qkv.shape[0]
    s0 = jnp.zeros((B, GDN_HEADS, GDN_DIM, GDN_DIM), jnp.float32)
    o_lat, o_ctx = 0.0, 0.0
    for d in range(2):
        fl = (lambda u: jnp.flip(u, axis=2)) if d == 1 else (lambda u: u)
        oc, s_ctx = gdn_chunked(fl(qc), fl(kc), fl(vc), fl(gc[d]), fl(bc[d]), s0)
        ol, _ = gdn_chunked(fl(ql), fl(kl), fl(vl), fl(gl[d]), fl(bl[d]), s_ctx)
        o_lat = o_lat + fl(ol)
        if ctx_out:
            o_ctx = o_ctx + fl(oc)

    def gate_out(o, zz):
        Bq, H, T, dv = o.shape
        y = rmsnorm(jnp.moveaxis(o, 1, 2), onorm_g) * jax.nn.silu(zz.reshape(Bq, T, H, dv).astype(jnp.float32))
        return y.reshape(Bq, T, H * dv).astype(zz.dtype)

    return gate_out(o_lat, z), (gate_out(o_ctx, z_c) if ctx_out else None)


def fourier_mix(f):
    B, T, _ = f.shape
    fg = f.astype(jnp.float32).reshape(B, T, FNET_GROUPS, FNET_GROUP_W)
    y = jnp.real(jnp.fft.fft2(fg, axes=(1, 3), norm='ortho'))
    return y.reshape(B, T, BRANCH_W).astype(f.dtype)


def merge_branches(branches, gate_in, w_branch, w_out):
    br = jnp.stack(branches, axis=2)
    proj = jnp.einsum('btiw,iwd->btid', br, w_branch)
    gates = jax.nn.sigmoid(gate_in.reshape(*gate_in.shape[:2], N_BRANCH, D_MODEL))
    return jnp.sum(gates * proj, axis=2) @ w_out


def token_mixers(h, hc, w_in, na_qn_g, na_kn_g, na_rpb, da_qn_g, da_kn_g, da_lam_q1, da_lam_k1,
                 da_lam_q2, da_lam_k2, da_subln_g, gdn_conv_w, gdn_a_log, gdn_dt_bias, gdn_onorm_g,
                 w_branch, w_out, lam_init, ctx_out):
    B, S, _ = h.shape
    L = hc.shape[1]
    na_qkv, da_qkv, gdn_qkv, gdn_z, gdn_ab, fnet_in, gate_in = split_cols(h @ w_in)
    na_qkv_c, da_qkv_c, gdn_qkv_c, gdn_z_c, gdn_ab_c, fnet_in_c, gate_in_c = split_cols(hc @ w_in)

    def na_heads(u, n):
        q, k, v = (t.reshape(B, n, NA_HEADS, NA_DIM) for t in jnp.split(u, 3, axis=-1))
        return rmsnorm(q, na_qn_g), rmsnorm(k, na_kn_g), v
    ya, ya_c = neighbourhood_attention(*na_heads(na_qkv, S), *na_heads(na_qkv_c, L), na_rpb, ctx_out)

    t = jnp.arange(S)
    rows = (t // GRID_W).astype(jnp.float32)
    cols = (t % GRID_W).astype(jnp.float32)

    def da_heads(u, n):
        q, k, v = jnp.split(u, 3, axis=-1)
        q = rmsnorm(q.reshape(B, n, DA_HEADS, 2, DA_QK_DIM), da_qn_g)
        k = rmsnorm(k.reshape(B, n, DA_HEADS, 2, DA_QK_DIM), da_kn_g)
        return q, k, v.reshape(B, n, DA_HEADS, DA_V_DIM)

    qd, kd, vd = da_heads(da_qkv, S)
    qd, kd = rope_2d(qd, rows, cols), rope_2d(kd, rows, cols)
    lam = (jnp.exp(jnp.sum(da_lam_q1.astype(jnp.float32) * da_lam_k1.astype(jnp.float32)))
           - jnp.exp(jnp.sum(da_lam_q2.astype(jnp.float32) * da_lam_k2.astype(jnp.float32))) + lam_init)
    yd, yd_c = diff_attention(qd, kd, vd, *da_heads(da_qkv_c, L), lam, ctx_out)

    def da_out(o):
        return (rmsnorm(o, da_subln_g) * (1.0 - lam_init)).reshape(*o.shape[:2], BRANCH_W)

    yg, yg_c = gdn_mixer(gdn_qkv, gdn_ab, gdn_z, gdn_qkv_c, gdn_ab_c, gdn_z_c, gdn_conv_w, gdn_a_log,
                         gdn_dt_bias, gdn_onorm_g, ctx_out)

    y = merge_branches((ya, da_out(yd), yg, fourier_mix(fnet_in)), gate_in, w_branch, w_out)
    y_c = None
    if ctx_out:
        y_c = merge_branches((ya_c, da_out(yd_c), yg_c, fourier_mix(fnet_in_c)), gate_in_c, w_branch, w_out)
    return y, y_c


def moe_ffn(h, w_router, router_bias, w_g, w_u, w_d):
    N, D = h.shape
    per_group = N_EXPERTS // N_GROUPS
    scores = jax.nn.sigmoid((h @ w_router).astype(jnp.float32))
    sel = scores + router_bias.astype(jnp.float32)
    group_score = lax.top_k(sel.reshape(N, N_GROUPS, per_group), 2)[0].sum(-1)
    best = jnp.argmax(group_score, axis=-1)
    in_group = (jnp.arange(N_EXPERTS) // per_group)[None, :] == best[:, None]
    _, idx = lax.top_k(jnp.where(in_group, sel, NEG_INF), TOP_K)
    wts = jnp.take_along_axis(scores, idx, axis=-1)
    wts = wts / jnp.sum(wts, axis=-1, keepdims=True)
    A = N * TOP_K
    expert = idx.reshape(A)
    token = jnp.repeat(jnp.arange(N, dtype=jnp.int32), TOP_K)
    weight = wts.reshape(A)
    order = jnp.argsort(expert)
    e_sorted = expert[order]
    counts = jnp.bincount(expert, length=N_EXPERTS)
    padded = (counts + MOE_BLOCK - 1) // MOE_BLOCK * MOE_BLOCK
    pad_end = jnp.cumsum(padded)
    pad_start = pad_end - padded
    start = jnp.cumsum(counts) - counts
    dest = pad_start[e_sorted] + jnp.arange(A) - start[e_sorted]
    n_blocks = -(-A // MOE_BLOCK) + N_EXPERTS
    P = n_blocks * MOE_BLOCK
    row_token = jnp.full((P,), N, jnp.int32).at[dest].set(token[order])
    row_weight = jnp.zeros((P,), h.dtype).at[dest].set(weight[order].astype(h.dtype))
    block_expert = jnp.minimum(jnp.searchsorted(pad_end, jnp.arange(n_blocks) * MOE_BLOCK, side='right'), N_EXPERTS - 1)
    h_pad = jnp.concatenate([h, jnp.zeros((1, D), h.dtype)], axis=0)
    xs = h_pad[row_token].reshape(n_blocks, MOE_BLOCK, D)

    def expert_block(args):
        xb, e = args
        return (jax.nn.silu(xb @ w_g[e]) * (xb @ w_u[e])) @ w_d[e]

    y = lax.map(expert_block, (xs, block_expert)).reshape(P, D) * row_weight[:, None]
    return jnp.zeros((N + 1, D), h.dtype).at[row_token].add(y)[:N]


def setup_inputs(seed: int = 0) -> dict:
    key = jax.random.key(seed)
    ks = (jax.random.fold_in(key, i) for i in range(100))
    nrm = lambda shape, s: jax.random.normal(next(ks), shape, jnp.float32) * s
    gain = lambda shape: 1.0 + nrm(shape, 0.01)
    dt = jnp.exp(jax.random.uniform(next(ks), (DEPTH, 2, GDN_HEADS), jnp.float32,
                                    minval=math.log(1e-3), maxval=math.log(1e-1)))
    a_log = jnp.log(jax.random.uniform(next(ks), (DEPTH, 2, GDN_HEADS), jnp.float32, minval=1.0, maxval=16.0))
    return {
        'x': nrm((BATCH, SEQ, D_MODEL), 1.0),
        'c': nrm((BATCH, D_MODEL), 1.0),
        'ctx': nrm((BATCH, CTX_LEN, D_MODEL), 1.0),
        'c_ctx': nrm((D_MODEL,), 1.0),
        'w_mod': nrm((DEPTH, D_MODEL, 6 * D_MODEL), 0.5 * D_MODEL ** -0.5),
        'b_mod': nrm((DEPTH, 6 * D_MODEL), 0.02),
        'norm1_g': gain((DEPTH, D_MODEL)),
        'norm2_g': gain((DEPTH, D_MODEL)),
        'w_in': nrm((DEPTH, D_MODEL, D_IN), D_MODEL ** -0.5),
        'na_qn_g': gain((DEPTH, NA_DIM)),
        'na_kn_g': gain((DEPTH, NA_DIM)),
        'na_rpb': nrm((DEPTH, NA_HEADS, 2 * NA_WIN_H - 1, 2 * NA_WIN_W - 1), 0.1),
        'da_qn_g': gain((DEPTH, DA_QK_DIM)),
        'da_kn_g': gain((DEPTH, DA_QK_DIM)),
        'da_lam_q1': nrm((DEPTH, DA_QK_DIM), 0.1),
        'da_lam_k1': nrm((DEPTH, DA_QK_DIM), 0.1),
        'da_lam_q2': nrm((DEPTH, DA_QK_DIM), 0.1),
        'da_lam_k2': nrm((DEPTH, DA_QK_DIM), 0.1),
        'da_subln_g': gain((DEPTH, DA_V_DIM)),
        'gdn_conv_w': nrm((DEPTH, GDN_CONV, 3 * BRANCH_W), GDN_CONV ** -0.5),
        'gdn_a_log': a_log,
        'gdn_dt_bias': dt + jnp.log(-jnp.expm1(-dt)),
        'gdn_onorm_g': gain((DEPTH, GDN_DIM)),
        'w_branch': nrm((DEPTH, N_BRANCH, BRANCH_W, D_MODEL), BRANCH_W ** -0.5),
        'w_out': nrm((DEPTH, D_MODEL, D_MODEL), D_MODEL ** -0.5),
        'w_router': nrm((D_MODEL, N_EXPERTS), D_MODEL ** -0.5),
        'router_bias': nrm((N_EXPERTS,), 0.01),
        'moe_w_gate': nrm((DEPTH, N_EXPERTS, D_MODEL, D_FF), D_MODEL ** -0.5),
        'moe_w_up': nrm((DEPTH, N_EXPERTS, D_MODEL, D_FF), D_MODEL ** -0.5),
        'moe_w_down': nrm((DEPTH, N_EXPERTS, D_FF, D_MODEL), D_FF ** -0.5),
    }


def reference(x, c, ctx, c_ctx, w_mod, b_mod, norm1_g, norm2_g, w_in, na_qn_g, na_kn_g, na_rpb,
              da_qn_g, da_kn_g, da_lam_q1, da_lam_k1, da_lam_q2, da_lam_k2, da_subln_g, gdn_conv_w,
              gdn_a_log, gdn_dt_bias, gdn_onorm_g, w_branch, w_out, w_router, router_bias,
              moe_w_gate, moe_w_up, moe_w_down):
    B, S, _ = x.shape
    L = ctx.shape[1]
    for l in range(DEPTH):
        last = l == DEPTH - 1
        lam_init = 0.8 - 0.6 * math.exp(-0.3 * l)
        mod = jax.nn.silu(c) @ w_mod[l] + b_mod[l]
        mod_c = jax.nn.silu(c_ctx) @ w_mod[l] + b_mod[l]
        sh1, sc1, g1, sh2, sc2, g2 = jnp.split(mod[:, None, :], 6, axis=-1)
        sh1c, sc1c, g1c, sh2c, sc2c, g2c = jnp.split(mod_c, 6)
        h = modulate(rmsnorm(x, norm1_g[l]), sh1, sc1)
        hc = modulate(rmsnorm(ctx, norm1_g[l]), sh1c, sc1c)
        y, y_c = token_mixers(h, hc, w_in[l], na_qn_g[l], na_kn_g[l], na_rpb[l], da_qn_g[l], da_kn_g[l],
                              da_lam_q1[l], da_lam_k1[l], da_lam_q2[l], da_lam_k2[l], da_subln_g[l],
                              gdn_conv_w[l], gdn_a_log[l], gdn_dt_bias[l], gdn_onorm_g[l],
                              w_branch[l], w_out[l], lam_init, not last)
        x = x + g1 * y
        h2 = modulate(rmsnorm(x, norm2_g[l]), sh2, sc2).reshape(B * S, D_MODEL)
        if last:
            f = moe_ffn(h2, w_router, router_bias, moe_w_gate[l], moe_w_up[l], moe_w_down[l])
            x = x + g2 * f.reshape(B, S, D_MODEL)
        else:
            ctx = ctx + g1c * y_c
            h2c = modulate(rmsnorm(ctx, norm2_g[l]), sh2c, sc2c).reshape(B * L, D_MODEL)
            f = moe_ffn(jnp.concatenate([h2, h2c], axis=0), w_router, router_bias,
                        moe_w_gate[l], moe_w_up[l], moe_w_down[l])
            x = x + g2 * f[:B * S].reshape(B, S, D_MODEL)
            ctx = ctx + g2c * f[B * S:].reshape(B, L, D_MODEL)
    return x
```

```python
import math
import jax
import jax.numpy as jnp
from jax import lax
import numpy as np
from jax.experimental import pallas as pl
from jax.experimental.pallas import tpu as pltpu

D_MODEL = 1024
BATCH = 2
SEQ = 8192
DEPTH = 2

CTX_LEN = 256
GRID_W = 64
N_BRANCH = 4
BRANCH_W = D_MODEL // N_BRANCH
NA_HEADS = 4
NA_DIM = BRANCH_W // NA_HEADS
NA_WIN_H = 8
NA_WIN_W = 16
NA_QB = 16
NA_KB = 32
DA_HEADS = 4
DA_QK_DIM = BRANCH_W // (2 * DA_HEADS)
DA_V_DIM = 2 * DA_QK_DIM
DA_QBLOCK = 128
ROPE_BASE = 10000.0
GDN_HEADS = 4
GDN_DIM = BRANCH_W // GDN_HEADS
GDN_CONV = 5
GDN_CHUNK = 64
FNET_GROUPS = 4
FNET_GROUP_W = BRANCH_W // FNET_GROUPS
N_EXPERTS = 16
N_GROUPS = 4
TOP_K = 2
D_FF = 512
MOE_BLOCK = 128
RMS_EPS = 1e-6
NEG_INF = -1e30
IN_SPLITS = (3 * BRANCH_W, 3 * BRANCH_W, 3 * BRANCH_W, BRANCH_W, 4 * GDN_HEADS, BRANCH_W, N_BRANCH * D_MODEL)
D_IN = sum(IN_SPLITS)


def _mm_kernel(a_ref, b_ref, o_ref):
    o_ref[...] = jnp.dot(a_ref[...].astype(jnp.bfloat16), b_ref[...].astype(jnp.bfloat16),
                         preferred_element_type=jnp.float32)


def _pmm(a, b, tm=512, tn=512):
    M, K = a.shape
    N = b.shape[1]
    Np = -(-N // tn) * tn
    bp = jnp.pad(b, ((0, 0), (0, Np - N)))
    out = pl.pallas_call(
        _mm_kernel,
        grid=(M // tm, Np // tn),
        in_specs=[pl.BlockSpec((tm, K), lambda i, j: (i, 0)), pl.BlockSpec((K, tn), lambda i, j: (0, j))],
        out_specs=pl.BlockSpec((tm, tn), lambda i, j: (i, j)),
        out_shape=jax.ShapeDtypeStruct((M, Np), jnp.float32),
    )(a, bp)
    return out[:, :N]


def rmsnorm(x, g):
    xf = x.astype(jnp.float32)
    y = xf * lax.rsqrt(jnp.mean(xf * xf, axis=-1, keepdims=True) + RMS_EPS)
    return (y * g.astype(jnp.float32)).astype(x.dtype)


def l2norm(x):
    xf = x.astype(jnp.float32)
    return xf * lax.rsqrt(jnp.sum(xf * xf, axis=-1, keepdims=True) + RMS_EPS)


def modulate(h, shift, scale):
    return h * (1.0 + scale) + shift


def softmax_f32(s):
    return jax.nn.softmax(s.astype(jnp.float32), axis=-1)


def split_cols(p):
    outs, o = [], 0
    for w in IN_SPLITS:
        outs.append(p[..., o:o + w])
        o += w
    return outs


def rope_2d(x, rows, cols):
    nf = x.shape[-1] // 4
    freqs = ROPE_BASE ** (-jnp.arange(nf, dtype=jnp.float32) / nf)
    ang = jnp.stack([rows[:, None] * freqs, cols[:, None] * freqs], axis=1)
    cos = jnp.cos(ang)[None, :, None, None]
    sin = jnp.sin(ang)[None, :, None, None]
    xf = x.astype(jnp.float32).reshape(*x.shape[:-1], 2, 2, nf)
    x1, x2 = xf[..., 0, :], xf[..., 1, :]
    out = jnp.stack([x1 * cos - x2 * sin, x2 * cos + x1 * sin], axis=-2)
    return out.reshape(x.shape).astype(x.dtype)


def dense_ctx_attention(qc, kc, vc):
    p = softmax_f32(jnp.einsum('bqhd,bkhd->bhqk', qc, kc) * qc.shape[-1] ** -0.5)
    return jnp.einsum('bhqk,bkhd->bqhd', p.astype(vc.dtype), vc)


def neighbourhood_attention(q, k, v, qc, kc, vc, rpb, ctx_out):
    B, S, H, Dh = q.shape
    L = qc.shape[1]
    rows = S // GRID_W
    wh = min(NA_WIN_H, rows)
    ncb = GRID_W // NA_QB
    scale = Dh ** -0.5
    r = jnp.arange(rows)
    row_idx = jnp.clip(r - wh // 2, 0, rows - wh)[:, None] + jnp.arange(wh)[None, :]
    qcol = jnp.arange(GRID_W).reshape(ncb, NA_QB)
    col_idx = jnp.clip(qcol[:, 0] - NA_WIN_W // 2, 0, GRID_W - NA_KB)[:, None] + jnp.arange(NA_KB)[None, :]
    win_start = jnp.clip(qcol - NA_WIN_W // 2, 0, GRID_W - NA_WIN_W)
    rel = col_idx[:, None, :] - win_start[..., None]
    col_mask = (rel >= 0) & (rel < NA_WIN_W)
    rel_r = row_idx - r[:, None] + NA_WIN_H - 1
    rel_c = jnp.clip(col_idx[:, None, :] - qcol[..., None] + NA_WIN_W - 1, 0, 2 * NA_WIN_W - 2)
    bias = rpb.astype(jnp.float32)[:, rel_r[:, None, None, :, None], rel_c[None, :, :, None, :]]
    bias = jnp.where(col_mask[None, None, :, :, None, :], bias, NEG_INF)
    nk = wh * NA_KB
    bias = bias.reshape(H, rows, ncb, NA_QB, nk)
    kw = k.reshape(B, rows, GRID_W, H, Dh)[:, row_idx[:, None, :, None], col_idx[None, :, None, :]]
    vw = v.reshape(B, rows, GRID_W, H, Dh)[:, row_idx[:, None, :, None], col_idx[None, :, None, :]]
    kw = kw.reshape(B, rows, ncb, nk, H, Dh)
    vw = vw.reshape(B, rows, ncb, nk, H, Dh)
    qb = q.reshape(B, rows, ncb, NA_QB, H, Dh)
    s_loc = jnp.einsum('brjqhd,brjkhd->bhrjqk', qb, kw).astype(jnp.float32) * scale + bias[None]
    s_ctx = jnp.einsum('brjqhd,bchd->bhrjqc', qb, kc).astype(jnp.float32) * scale
    p = softmax_f32(jnp.concatenate([s_loc, s_ctx], axis=-1)).astype(v.dtype)
    o = (jnp.einsum('bhrjqk,brjkhd->brjqhd', p[..., :nk], vw)
         + jnp.einsum('bhrjqc,bchd->brjqhd', p[..., nk:], vc)).reshape(B, S, H * Dh)
    o_c = dense_ctx_attention(qc, kc, vc).reshape(B, L, H * Dh) if ctx_out else None
    return o, o_c


def diff_attention(q, k, v, qc, kc, vc, lam, ctx_out):
    B, S, H, _, dq = q.shape
    scale = dq ** -0.5
    k_all = jnp.concatenate([k, kc], axis=1)
    v_all = jnp.concatenate([v, vc], axis=1)
    nb = S // DA_QBLOCK
    qb = jnp.moveaxis(q.reshape(B, nb, DA_QBLOCK, H, 2, dq), 1, 0)

    def block(qblk):
        p = softmax_f32(jnp.einsum('bqhmd,bkhmd->bhmqk', qblk, k_all) * scale)
        a = p[:, :, 0] - lam * p[:, :, 1]
        return jnp.einsum('bhqk,bkhd->bqhd', a.astype(v_all.dtype), v_all)

    o = jnp.moveaxis(lax.map(block, qb), 0, 1).reshape(B, S, H, v.shape[-1])
    o_c = None
    if ctx_out:
        p = softmax_f32(jnp.einsum('bqhmd,bkhmd->bhmqk', qc, kc) * scale)
        a = p[:, :, 0] - lam * p[:, :, 1]
        o_c = jnp.einsum('bhqk,bkhd->bqhd', a.astype(vc.dtype), vc)
    return o, o_c


def short_conv(x, w):
    K = w.shape[0]
    pad = K // 2
    T = x.shape[1]
    xp = jnp.pad(x, ((0, 0), (pad, pad), (0, 0)))
    y = xp[:, 0:T] * w[0]
    for i in range(1, K):
        y = y + xp[:, i:i + T] * w[i]
    return jax.nn.silu(y)


def gdn_prepare(qkv, ab, conv_w, a_log, dt_bias):
    B, T, _ = qkv.shape
    q, k, v = jnp.split(short_conv(qkv, conv_w), 3, axis=-1)
    to_heads = lambda u: jnp.moveaxis(u.reshape(B, T, GDN_HEADS, GDN_DIM), 1, 2)
    q = l2norm(to_heads(q)) * GDN_DIM ** -0.5
    k = l2norm(to_heads(k))
    v = to_heads(v).astype(jnp.float32)
    abf = ab.astype(jnp.float32).reshape(B, T, 2, 2, GDN_HEADS)
    g = -jnp.exp(a_log.astype(jnp.float32)) * jax.nn.softplus(abf[:, :, 0] + dt_bias.astype(jnp.float32))
    beta = jax.nn.sigmoid(abf[:, :, 1])
    g = jnp.transpose(g, (2, 0, 3, 1))
    beta = jnp.transpose(beta, (2, 0, 3, 1))
    return q, k, v, g, beta


def gdn_chunked(q, k, v, g, beta, s0):
    B, H, T, dk = q.shape
    dv = v.shape[-1]
    C = GDN_CHUNK
    N = T // C
    q, k, v = (u.reshape(B, H, N, C, -1) for u in (q, k, v))
    g = g.reshape(B, H, N, C)
    beta = beta.reshape(B, H, N, C)
    G = jnp.cumsum(g, axis=-1)
    incl = jnp.tril(jnp.ones((C, C), dtype=bool))
    strict = jnp.tril(jnp.ones((C, C), dtype=bool), -1)
    decay = jnp.where(incl, jnp.exp(jnp.where(incl, G[..., :, None] - G[..., None, :], 0.0)), 0.0)
    kb = k * beta[..., None]
    a = jnp.where(strict, jnp.einsum('bhnid,bhnjd->bhnij', kb, k) * decay, 0.0)
    lhs = a + jnp.eye(C, dtype=a.dtype)
    u = lax.linalg.triangular_solve(lhs, v * beta[..., None], left_side=True, lower=True, unit_diagonal=True)
    w = lax.linalg.triangular_solve(lhs, kb * jnp.exp(G)[..., None], left_side=True, lower=True, unit_diagonal=True)
    qk = jnp.where(incl, jnp.einsum('bhnid,bhnjd->bhnij', q, k) * decay, 0.0)
    q_dec = q * jnp.exp(G)[..., None]
    k_dec = k * jnp.exp(G[..., -1:] - G)[..., None]
    g_last = jnp.exp(G[..., -1])

    def step(S, xs):
        qk_c, qd_c, kd_c, u_c, w_c, gl_c = xs
        v_new = u_c - jnp.einsum('bhcd,bhde->bhce', w_c, S)
        o = jnp.einsum('bhcd,bhde->bhce', qd_c, S) + jnp.einsum('bhij,bhje->bhie', qk_c, v_new)
        S = S * gl_c[..., None, None] + jnp.einsum('bhcd,bhce->bhde', kd_c, v_new)
        return S, o

    xs = tuple(jnp.moveaxis(t, 2, 0) for t in (qk, q_dec, k_dec, u, w, g_last))
    s_final, o = lax.scan(step, s0, xs)
    return jnp.moveaxis(o, 0, 2).reshape(B, H, T, dv), s_final


def gdn_mixer(qkv, ab, z, qkv_c, ab_c, z_c, conv_w, a_log, dt_bias, onorm_g, ctx_out):
    ql, kl, vl, gl, bl = gdn_prepare(qkv, ab, conv_w, a_log, dt_bias)
    qc, kc, vc, gc, bc = gdn_prepare(qkv_c, ab_c, conv_w, a_log, dt_bias)
    B = qkv.shape[0]
    s0 = jnp.zeros((B, GDN_HEADS, GDN_DIM, GDN_DIM), jnp.float32)
    o_lat, o_ctx = 0.0, 0.0
    for d in range(2):
        fl = (lambda u: jnp.flip(u, axis=2)) if d == 1 else (lambda u: u)
        oc, s_ctx = gdn_chunked(fl(qc), fl(kc), fl(vc), fl(gc[d]), fl(bc[d]), s0)
        ol, _ = gdn_chunked(fl(ql), fl(kl), fl(vl), fl(gl[d]), fl(bl[d]), s_ctx)
        o_lat = o_lat + fl(ol)
        if ctx_out:
            o_ctx = o_ctx + fl(oc)

    def gate_out(o, zz):
        Bq, H, T, dv = o.shape
        y = rmsnorm(jnp.moveaxis(o, 1, 2), onorm_g) * jax.nn.silu(zz.reshape(Bq, T, H, dv).astype(jnp.float32))
        return y.reshape(Bq, T, H * dv).astype(zz.dtype)

    return gate_out(o_lat, z), (gate_out(o_ctx, z_c) if ctx_out else None)


def fourier_mix(f):
    B, T, _ = f.shape
    fg = f.astype(jnp.float32).reshape(B, T, FNET_GROUPS, FNET_GROUP_W)
    y = jnp.real(jnp.fft.fft2(fg, axes=(1, 3), norm='ortho'))
    return y.reshape(B, T, BRANCH_W).astype(f.dtype)


def merge_branches(branches, gate_in, w_branch, w_out):
    br = jnp.stack(branches, axis=2)
    proj = jnp.einsum('btiw,iwd->btid', br, w_branch)
    gates = jax.nn.sigmoid(gate_in.reshape(*gate_in.shape[:2], N_BRANCH, D_MODEL))
    return jnp.sum(gates * proj, axis=2) @ w_out


def token_mixers(h, hc, w_in, na_qn_g, na_kn_g, na_rpb, da_qn_g, da_kn_g, da_lam_q1, da_lam_k1,
                 da_lam_q2, da_lam_k2, da_subln_g, gdn_conv_w, gdn_a_log, gdn_dt_bias, gdn_onorm_g,
                 w_branch, w_out, lam_init, ctx_out):
    B, S, _ = h.shape
    L = hc.shape[1]
    na_qkv, da_qkv, gdn_qkv, gdn_z, gdn_ab, fnet_in, gate_in = split_cols(
        _pmm(h.reshape(B * S, -1), w_in).reshape(B, S, -1))
    na_qkv_c, da_qkv_c, gdn_qkv_c, gdn_z_c, gdn_ab_c, fnet_in_c, gate_in_c = split_cols(hc @ w_in)

    def na_heads(u, n):
        q, k, v = (t.reshape(B, n, NA_HEADS, NA_DIM) for t in jnp.split(u, 3, axis=-1))
        return rmsnorm(q, na_qn_g), rmsnorm(k, na_kn_g), v
    ya, ya_c = neighbourhood_attention(*na_heads(na_qkv, S), *na_heads(na_qkv_c, L), na_rpb, ctx_out)

    t = jnp.arange(S)
    rows = (t // GRID_W).astype(jnp.float32)
    cols = (t % GRID_W).astype(jnp.float32)

    def da_heads(u, n):
        q, k, v = jnp.split(u, 3, axis=-1)
        q = rmsnorm(q.reshape(B, n, DA_HEADS, 2, DA_QK_DIM), da_qn_g)
        k = rmsnorm(k.reshape(B, n, DA_HEADS, 2, DA_QK_DIM), da_kn_g)
        return q, k, v.reshape(B, n, DA_HEADS, DA_V_DIM)

    qd, kd, vd = da_heads(da_qkv, S)
    qd, kd = rope_2d(qd, rows, cols), rope_2d(kd, rows, cols)
    lam = (jnp.exp(jnp.sum(da_lam_q1.astype(jnp.float32) * da_lam_k1.astype(jnp.float32)))
           - jnp.exp(jnp.sum(da_lam_q2.astype(jnp.float32) * da_lam_k2.astype(jnp.float32))) + lam_init)
    yd, yd_c = diff_attention(qd, kd, vd, *da_heads(da_qkv_c, L), lam, ctx_out)

    def da_out(o):
        return (rmsnorm(o, da_subln_g) * (1.0 - lam_init)).reshape(*o.shape[:2], BRANCH_W)

    yg, yg_c = gdn_mixer(gdn_qkv, gdn_ab, gdn_z, gdn_qkv_c, gdn_ab_c, gdn_z_c, gdn_conv_w, gdn_a_log,
                         gdn_dt_bias, gdn_onorm_g, ctx_out)

    y = merge_branches((ya, da_out(yd), yg, fourier_mix(fnet_in)), gate_in, w_branch, w_out)
    y_c = None
    if ctx_out:
        y_c = merge_branches((ya_c, da_out(yd_c), yg_c, fourier_mix(fnet_in_c)), gate_in_c, w_branch, w_out)
    return y, y_c


def moe_ffn(h, w_router, router_bias, w_g, w_u, w_d):
    N, D = h.shape
    per_group = N_EXPERTS // N_GROUPS
    scores = jax.nn.sigmoid((h @ w_router).astype(jnp.float32))
    sel = scores + router_bias.astype(jnp.float32)
    group_score = lax.top_k(sel.reshape(N, N_GROUPS, per_group), 2)[0].sum(-1)
    best = jnp.argmax(group_score, axis=-1)
    in_group = (jnp.arange(N_EXPERTS) // per_group)[None, :] == best[:, None]
    _, idx = lax.top_k(jnp.where(in_group, sel, NEG_INF), TOP_K)
    wts = jnp.take_along_axis(scores, idx, axis=-1)
    wts = wts / jnp.sum(wts, axis=-1, keepdims=True)
    A = N * TOP_K
    expert = idx.reshape(A)
    token = jnp.repeat(jnp.arange(N, dtype=jnp.int32), TOP_K)
    weight = wts.reshape(A)
    order = jnp.argsort(expert)
    e_sorted = expert[order]
    counts = jnp.bincount(expert, length=N_EXPERTS)
    padded = (counts + MOE_BLOCK - 1) // MOE_BLOCK * MOE_BLOCK
    pad_end = jnp.cumsum(padded)
    pad_start = pad_end - padded
    start = jnp.cumsum(counts) - counts
    dest = pad_start[e_sorted] + jnp.arange(A) - start[e_sorted]
    n_blocks = -(-A // MOE_BLOCK) + N_EXPERTS
    P = n_blocks * MOE_BLOCK
    row_token = jnp.full((P,), N, jnp.int32).at[dest].set(token[order])
    row_weight = jnp.zeros((P,), h.dtype).at[dest].set(weight[order].astype(h.dtype))
    block_expert = jnp.minimum(jnp.searchsorted(pad_end, jnp.arange(n_blocks) * MOE_BLOCK, side='right'), N_EXPERTS - 1)
    h_pad = jnp.concatenate([h, jnp.zeros((1, D), h.dtype)], axis=0)
    xs = h_pad[row_token].reshape(n_blocks, MOE_BLOCK, D)

    def expert_block(args):
        xb, e = args
        return (jax.nn.silu(xb @ w_g[e]) * (xb @ w_u[e])) @ w_d[e]

    y = lax.map(expert_block, (xs, block_expert)).reshape(P, D) * row_weight[:, None]
    return jnp.zeros((N + 1, D), h.dtype).at[row_token].add(y)[:N]


def kernel(x, c, ctx, c_ctx, w_mod, b_mod, norm1_g, norm2_g, w_in, na_qn_g, na_kn_g, na_rpb,
           da_qn_g, da_kn_g, da_lam_q1, da_lam_k1, da_lam_q2, da_lam_k2, da_subln_g, gdn_conv_w,
           gdn_a_log, gdn_dt_bias, gdn_onorm_g, w_branch, w_out, w_router, router_bias,
           moe_w_gate, moe_w_up, moe_w_down):
    B, S, _ = x.shape
    L = ctx.shape[1]
    for l in range(DEPTH):
        last = l == DEPTH - 1
        lam_init = 0.8 - 0.6 * math.exp(-0.3 * l)
        mod = jax.nn.silu(c) @ w_mod[l] + b_mod[l]
        mod_c = jax.nn.silu(c_ctx) @ w_mod[l] + b_mod[l]
        sh1, sc1, g1, sh2, sc2, g2 = jnp.split(mod[:, None, :], 6, axis=-1)
        sh1c, sc1c, g1c, sh2c, sc2c, g2c = jnp.split(mod_c, 6)
        h = modulate(rmsnorm(x, norm1_g[l]), sh1, sc1)
        hc = modulate(rmsnorm(ctx, norm1_g[l]), sh1c, sc1c)
        y, y_c = token_mixers(h, hc, w_in[l], na_qn_g[l], na_kn_g[l], na_rpb[l], da_qn_g[l], da_kn_g[l],
                              da_lam_q1[l], da_lam_k1[l], da_lam_q2[l], da_lam_k2[l], da_subln_g[l],
                              gdn_conv_w[l], gdn_a_log[l], gdn_dt_bias[l], gdn_onorm_g[l],
                              w_branch[l], w_out[l], lam_init, not last)
        x = x + g1 * y
        h2 = modulate(rmsnorm(x, norm2_g[l]), sh2, sc2).reshape(B * S, D_MODEL)
        if last:
            f = moe_ffn(h2, w_router, router_bias, moe_w_gate[l], moe_w_up[l], moe_w_down[l])
            x = x + g2 * f.reshape(B, S, D_MODEL)
        else:
            ctx = ctx + g1c * y_c
            h2c = modulate(rmsnorm(ctx, norm2_g[l]), sh2c, sc2c).reshape(B * L, D_MODEL)
            f = moe_ffn(jnp.concatenate([h2, h2c], axis=0), w_router, router_bias,
                        moe_w_gate[l], moe_w_up[l], moe_w_down[l])
            x = x + g2 * f[:B * S].reshape(B, S, D_MODEL)
            ctx = ctx + g2c * f[B * S:].reshape(B, L, D_MODEL)
    return x
```

```python
import functools
import math
import jax
import jax.numpy as jnp
from jax import lax
import numpy as np
from jax.experimental import pallas as pl
from jax.experimental.pallas import tpu as pltpu

D_MODEL = 1024
DEPTH = 2
GRID_W = 64
N_BRANCH = 4
BRANCH_W = D_MODEL // N_BRANCH
GDN_HEADS = 4
GDN_DIM = BRANCH_W // GDN_HEADS
GDN_CHUNK = 64
FNET_GROUPS = 4
FNET_GROUP_W = BRANCH_W // FNET_GROUPS
N_EXPERTS = 16
N_GROUPS = 4
TOP_K = 2
MOE_BLOCK = 128
RMS_EPS = 1e-6
NEG_INF = -1e30
ROPE_BASE = 10000.0
VMEM_LIMIT = 56 * 1024 * 1024
HI = lax.Precision.HIGHEST
NT_DIMS = (((1,), (1,)), ((), ()))


def _mod_kernel(c_ref, w_ref, b_ref, o_ref):
    c = c_ref[...]
    a = c * jax.nn.sigmoid(c)
    o_ref[0] = jnp.dot(a, w_ref[0], preferred_element_type=jnp.float32, precision=HI) + b_ref[0]


def mod_call(cvec, w_mod, b_mod, tn=512):
    depth, D, N = w_mod.shape
    return pl.pallas_call(
        _mod_kernel,
        grid=(depth, N // tn),
        in_specs=[pl.BlockSpec((8, D), lambda l, j: (0, 0)),
                  pl.BlockSpec((1, D, tn), lambda l, j: (l, 0, j)),
                  pl.BlockSpec((1, 1, tn), lambda l, j: (l, 0, j))],
        out_specs=pl.BlockSpec((1, 8, tn), lambda l, j: (l, 0, j)),
        out_shape=jax.ShapeDtypeStruct((depth, 8, N), jnp.float32),
        name="mod",
    )(cvec, w_mod, b_mod.reshape(depth, 1, N))


def _inproj_kernel(x_ref, m_ref, g_ref, w_ref, wab_ref, na_ref, da_ref, gdn_ref, z_ref, f_ref, gate_ref, ab_ref,
                   *, tiles_per_batch, n_lat_tiles, n_batch):
    i = pl.program_id(0)
    r = jnp.where(i < n_lat_tiles, i // tiles_per_batch, n_batch)
    x = x_ref[...]
    y = x * lax.rsqrt(jnp.mean(x * x, axis=-1, keepdims=True) + RMS_EPS) * g_ref[...]
    sh = m_ref[pl.ds(r, 1), 0:D_MODEL]
    sc = m_ref[pl.ds(r, 1), D_MODEL:2 * D_MODEL]
    h = (y * (1.0 + sc) + sh).astype(jnp.bfloat16)

    def seg(o_ref, a, b):
        o_ref[...] = jnp.dot(h, w_ref[:, a:b], preferred_element_type=jnp.float32).astype(o_ref.dtype)

    seg(na_ref, 0, 768)
    seg(da_ref, 768, 1536)
    seg(gdn_ref, 1536, 2304)
    seg(z_ref, 2304, 2560)
    seg(f_ref, 2560, 2816)
    for k in range(4):
        gate_ref[:, k * 1024:(k + 1) * 1024] = jnp.dot(
            h, w_ref[:, 2816 + k * 1024:2816 + (k + 1) * 1024],
            preferred_element_type=jnp.float32).astype(gate_ref.dtype)
    ab_ref[...] = jnp.dot(h, wab_ref[...], preferred_element_type=jnp.float32)


def inproj_call(X, mod_l, g, w_main, w_ab, *, n_batch, seq, tm=256):
    NT, D = X.shape
    n_lat_tiles = n_batch * seq // tm
    kern = functools.partial(_inproj_kernel, tiles_per_batch=seq // tm, n_lat_tiles=n_lat_tiles, n_batch=n_batch)
    widths = [(768, jnp.bfloat16), (768, jnp.bfloat16), (768, jnp.float32), (256, jnp.bfloat16),
              (256, jnp.bfloat16), (4096, jnp.bfloat16), (128, jnp.float32)]
    return pl.pallas_call(
        kern,
        grid=(NT // tm,),
        in_specs=[pl.BlockSpec((tm, D), lambda i: (i, 0)),
                  pl.BlockSpec(mod_l.shape, lambda i: (0, 0)),
                  pl.BlockSpec((1, D), lambda i: (0, 0)),
                  pl.BlockSpec(w_main.shape, lambda i: (0, 0)),
                  pl.BlockSpec(w_ab.shape, lambda i: (0, 0))],
        out_specs=[pl.BlockSpec((tm, w), lambda i: (i, 0)) for w, _ in widths],
        out_shape=[jax.ShapeDtypeStruct((NT, w), dt) for w, dt in widths],
        compiler_params=pltpu.CompilerParams(vmem_limit_bytes=VMEM_LIMIT),
        name="inproj",
    )(X, mod_l, g.reshape(1, D), w_main, w_ab)


def _group_ones(width, group):
    i = np.arange(width)
    return jnp.asarray((i[:, None] // group == i[None, :] // group).astype(np.float32))


def _group_rms(x, gmat, group, gain):
    ss = jnp.dot(x * x, gmat, preferred_element_type=jnp.float32, precision=HI)
    return x * lax.rsqrt(ss * (1.0 / group) + RMS_EPS) * gain


def rope_tables(seq, tm):
    nf = 8
    t = jnp.arange(seq)
    rows = (t // GRID_W).astype(jnp.float32)
    cols = (t % GRID_W).astype(jnp.float32)
    freqs = ROPE_BASE ** (-jnp.arange(nf, dtype=jnp.float32) / nf)
    d = np.arange(32)
    f_idx = d % 8
    use_col = (d // 16) == 1
    ang = jnp.where(use_col[None, :], cols[:, None], rows[:, None]) * freqs[f_idx][None, :]
    sign = np.where((d % 16) < 8, -1.0, 1.0).astype(np.float32)
    cos = jnp.tile(jnp.cos(ang), (1, 8))
    sin = jnp.tile(jnp.sin(ang) * sign[None, :], (1, 8))
    cos = jnp.concatenate([cos, jnp.ones((tm, 256), jnp.float32)], 0)
    sin = jnp.concatenate([sin, jnp.zeros((tm, 256), jnp.float32)], 0)
    return cos, sin


def _prep_kernel(na_ref, da_ref, cos_ref, sin_ref, g64_ref, g32_ref, gains_ref, naq_ref, nak_ref, daq_ref, dak_ref):
    g64 = g64_ref[...]
    g32 = g32_ref[...]
    na = na_ref[...].astype(jnp.float32)
    naq_ref[...] = (_group_rms(na[:, 0:256], g64, 64, gains_ref[0:1, :]) * (64 ** -0.5)).astype(naq_ref.dtype)
    nak_ref[...] = _group_rms(na[:, 256:512], g64, 64, gains_ref[1:2, :]).astype(nak_ref.dtype)
    da = da_ref[...].astype(jnp.float32)
    cos = cos_ref[...]
    sin = sin_ref[...]
    lane = lax.broadcasted_iota(jnp.int32, (1, 256), 1)
    first = (lane % 16) < 8

    def rope(x):
        swapped = jnp.where(first, pltpu.roll(x, 256 - 8, 1), pltpu.roll(x, 8, 1))
        return x * cos + swapped * sin

    q = rope(_group_rms(da[:, 0:256], g32, 32, gains_ref[2:3, :]))
    k = rope(_group_rms(da[:, 256:512], g32, 32, gains_ref[3:4, :]))
    daq_ref[...] = (q * (32 ** -0.5)).astype(daq_ref.dtype)
    dak_ref[...] = k.astype(dak_ref.dtype)


def prep_call(P_na, P_da, cos, sin, na_qg, na_kg, da_qg, da_kg, *, n_batch, seq, tm=256):
    NT = P_na.shape[0]
    tpb = seq // tm
    n_lat = n_batch * tpb
    gains = jnp.stack([jnp.tile(na_qg, 4), jnp.tile(na_kg, 4), jnp.tile(da_qg, 8), jnp.tile(da_kg, 8)]
                      + [jnp.zeros((256,), jnp.float32)] * 4, 0)
    tab_map = lambda i: (jnp.where(i < n_lat, i % tpb, tpb), 0)
    full = lambda a: pl.BlockSpec(a.shape, lambda i: (0,) * a.ndim)
    g64, g32 = _group_ones(256, 64), _group_ones(256, 32)
    return pl.pallas_call(
        _prep_kernel,
        grid=(NT // tm,),
        in_specs=[pl.BlockSpec((tm, 768), lambda i: (i, 0)), pl.BlockSpec((tm, 768), lambda i: (i, 0)),
                  pl.BlockSpec((tm, 256), tab_map), pl.BlockSpec((tm, 256), tab_map),
                  full(g64), full(g32), full(gains)],
        out_specs=[pl.BlockSpec((tm, 256), lambda i: (i, 0))] * 4,
        out_shape=[jax.ShapeDtypeStruct((NT, 256), jnp.bfloat16)] * 4,
        name="attn_prep",
    )(P_na, P_da, cos, sin, g64, g32, gains)


NA_RB = 4
NA_UR = 11


def na_bias_tables(rpb, rows):
    reps = [(0, 0), (4, 0), (rows - NA_RB, rows - NA_UR)]
    idx_r, idx_c, valid = [], [], []
    for r0, u0 in reps:
        rr = np.arange(NA_RB)[:, None, None, None]
        qc = np.arange(GRID_W)[None, :, None, None]
        ur = np.arange(NA_UR)[None, None, :, None]
        kc = np.arange(GRID_W)[None, None, None, :]
        r = r0 + rr
        start = np.clip(r - 4, 0, rows - 8)
        krow = u0 + ur
        vrow = (krow >= start) & (krow < start + 8)
        ws = np.clip(qc - 8, 0, GRID_W - 16)
        vcol = (kc >= ws) & (kc < ws + 16)
        v = np.broadcast_to(vrow & vcol, (NA_RB, GRID_W, NA_UR, GRID_W))
        rel_r = np.broadcast_to(np.clip(krow - r + 7, 0, 14), v.shape)
        rel_c = np.broadcast_to(np.clip(kc - qc + 15, 0, 30), v.shape)
        idx_r.append(rel_r.reshape(NA_RB * GRID_W, NA_UR * GRID_W))
        idx_c.append(rel_c.reshape(NA_RB * GRID_W, NA_UR * GRID_W))
        valid.append(v.reshape(NA_RB * GRID_W, NA_UR * GRID_W))
    idx_r, idx_c, valid = np.stack(idx_r), np.stack(idx_c), np.stack(valid)
    b = rpb.astype(jnp.float32)[:, idx_r, idx_c]
    b = jnp.where(valid[None], b, NEG_INF)
    return jnp.transpose(b, (1, 0, 2, 3))


def _na_kernel(q_ref, k_ref, v_ref, kc_ref, vc_ref, bias_ref, o_ref, *, rows):
    i = pl.program_id(1)
    u0 = jnp.clip(i * NA_RB - 4, 0, rows - NA_UR)
    off = pl.multiple_of(u0 * GRID_W, GRID_W)
    nkw = NA_UR * GRID_W
    kwin = k_ref[pl.ds(off, nkw), :]
    vwin = v_ref[pl.ds(off, nkw), :]
    q = q_ref[...]
    kc = kc_ref[...]
    vc = vc_ref[...]
    outs = []
    for h in range(4):
        sl = slice(h * 64, (h + 1) * 64)
        qh = q[:, sl]
        s_loc = lax.dot_general(qh, kwin[:, sl], NT_DIMS, preferred_element_type=jnp.float32) + bias_ref[0, h]
        s_ctx = lax.dot_general(qh, kc[:, sl], NT_DIMS, preferred_element_type=jnp.float32)
        m = jnp.maximum(jnp.max(s_loc, -1, keepdims=True), jnp.max(s_ctx, -1, keepdims=True))
        p_loc = jnp.exp(s_loc - m)
        p_ctx = jnp.exp(s_ctx - m)
        l = jnp.sum(p_loc, -1, keepdims=True) + jnp.sum(p_ctx, -1, keepdims=True)
        o = (jnp.dot(p_loc.astype(jnp.bfloat16), vwin[:, sl], preferred_element_type=jnp.float32)
             + jnp.dot(p_ctx.astype(jnp.bfloat16), vc[:, sl], preferred_element_type=jnp.float32))
        outs.append(o / l)
    o_ref[...] = jnp.concatenate(outs, -1).astype(o_ref.dtype)


def na_call(NAQ, NAK, P_na, bias, *, n_batch, seq, ctx_len):
    rows = seq // GRID_W
    nblk = rows // NA_RB
    tq = NA_RB * GRID_W
    cb = n_batch * seq // ctx_len
    kern = functools.partial(_na_kernel, rows=rows)
    tsel = lambda b, i: (jnp.where(i == 0, 0, jnp.where(i == nblk - 1, 2, 1)), 0, 0, 0)
    return pl.pallas_call(
        kern,
        grid=(n_batch, nblk),
        in_specs=[pl.BlockSpec((tq, 256), lambda b, i: (b * nblk + i, 0)),
                  pl.BlockSpec((seq, 256), lambda b, i: (b, 0)),
                  pl.BlockSpec((seq, 256), lambda b, i: (b, 2)),
                  pl.BlockSpec((ctx_len, 256), lambda b, i: (cb + b, 0)),
                  pl.BlockSpec((ctx_len, 256), lambda b, i: (cb + b, 2)),
                  pl.BlockSpec((1,) + bias.shape[1:], tsel)],
        out_specs=pl.BlockSpec((tq, 256), lambda b, i: (b * nblk + i, 0)),
        out_shape=jax.ShapeDtypeStruct((n_batch * seq, 256), jnp.bfloat16),
        compiler_params=pltpu.CompilerParams(vmem_limit_bytes=VMEM_LIMIT),
        name="na_attn",
    )(NAQ, NAK, P_na, NAK, P_na, bias)


def _stack_masked_q(q):
    lane = lax.broadcasted_iota(jnp.int32, (1, 256), 1)
    return jnp.concatenate([jnp.where((lane // 32) == hm, q, jnp.zeros_like(q)) for hm in range(8)], 0)


def _da_finish(acc, l, lam, gain, g64, post_scale):
    tq = acc.shape[0] // 8
    outs = []
    for h in range(4):
        a0 = acc[(2 * h) * tq:(2 * h + 1) * tq] / l[(2 * h) * tq:(2 * h + 1) * tq]
        a1 = acc[(2 * h + 1) * tq:(2 * h + 2) * tq] / l[(2 * h + 1) * tq:(2 * h + 2) * tq]
        outs.append(a0 - lam * a1)
    o = jnp.concatenate(outs, -1)
    return _group_rms(o, g64, 64, gain) * post_scale


def _da_tile(qs, k, v, m, l, acc, tq):
    s = lax.dot_general(qs, k, NT_DIMS, preferred_element_type=jnp.float32)
    m_new = jnp.maximum(m, jnp.max(s, -1, keepdims=True))
    alpha = jnp.exp(m - m_new)
    p = jnp.exp(s - m_new)
    l_new = alpha * l + jnp.sum(p, -1, keepdims=True)
    pb = p.astype(jnp.bfloat16)
    pv = jnp.concatenate(
        [jnp.dot(pb[2 * h * tq:(2 * h + 2) * tq], v[:, h * 64:(h + 1) * 64], preferred_element_type=jnp.float32)
         for h in range(4)], 0)
    return m_new, l_new, alpha * acc + pv


def _da_kernel(lam_ref, q_ref, k_ref, v_ref, kc_ref, vc_ref, gain_ref, g64_ref, o_ref, m_ref, l_ref, acc_ref,
               *, tk, n_kv, post_scale):
    tq = q_ref.shape[0]
    qs = _stack_masked_q(q_ref[...])
    m_ref[...] = jnp.full(m_ref.shape, -jnp.inf, jnp.float32)
    l_ref[...] = jnp.zeros(l_ref.shape, jnp.float32)
    acc_ref[...] = jnp.zeros(acc_ref.shape, jnp.float32)

    def body(j, carry):
        off = pl.multiple_of(j * tk, tk)
        m, l, acc = _da_tile(qs, k_ref[pl.ds(off, tk), :], v_ref[pl.ds(off, tk), :],
                             m_ref[...], l_ref[...], acc_ref[...], tq)
        m_ref[...] = m
        l_ref[...] = l
        acc_ref[...] = acc
        return carry

    lax.fori_loop(0, n_kv, body, 0)
    m, l, acc = _da_tile(qs, kc_ref[...], vc_ref[...], m_ref[...], l_ref[...], acc_ref[...], tq)
    o_ref[...] = _da_finish(acc, l, lam_ref[0], gain_ref[...], g64_ref[...], post_scale).astype(o_ref.dtype)


def da_call(lam, DAQ, DAK, P_da, subln_g, *, n_batch, seq, ctx_len, lam_init, tq=128, tk=512):
    nq = seq // tq
    cb = n_batch * seq // ctx_len
    g64 = _group_ones(256, 64)
    gain = jnp.tile(subln_g, 4).reshape(1, 256)
    kern = functools.partial(_da_kernel, tk=tk, n_kv=seq // tk, post_scale=1.0 - lam_init)
    return pl.pallas_call(
        kern,
        grid=(n_batch, nq),
        in_specs=[pl.BlockSpec(memory_space=pltpu.SMEM),
                  pl.BlockSpec((tq, 256), lambda b, i: (b * nq + i, 0)),
                  pl.BlockSpec((seq, 256), lambda b, i: (b, 0)),
                  pl.BlockSpec((seq, 256), lambda b, i: (b, 2)),
                  pl.BlockSpec((ctx_len, 256), lambda b, i: (cb + b, 0)),
                  pl.BlockSpec((ctx_len, 256), lambda b, i: (cb + b, 2)),
                  pl.BlockSpec((1, 256), lambda b, i: (0, 0)),
                  pl.BlockSpec((256, 256), lambda b, i: (0, 0))],
        out_specs=pl.BlockSpec((tq, 256), lambda b, i: (b * nq + i, 0)),
        out_shape=jax.ShapeDtypeStruct((n_batch * seq, 256), jnp.bfloat16),
        scratch_shapes=[pltpu.VMEM((8 * tq, 1), jnp.float32), pltpu.VMEM((8 * tq, 1), jnp.float32),
                        pltpu.VMEM((8 * tq, 64), jnp.float32)],
        compiler_params=pltpu.CompilerParams(vmem_limit_bytes=VMEM_LIMIT),
        name="da_attn",
    )(lam.reshape(1), DAQ, DAK, P_da, DAK, P_da, gain, g64)


def _ctx_kernel(lam_ref, naq_ref, nak_ref, nav_ref, daq_ref, dak_ref, dav_ref, gain_ref, g64_ref, ya_ref, yd_ref,
                *, post_scale):
    q = naq_ref[...]
    k = nak_ref[...]
    v = nav_ref[...]
    outs = []
    for h in range(4):
        sl = slice(h * 64, (h + 1) * 64)
        s = lax.dot_general(q[:, sl], k[:, sl], NT_DIMS, preferred_element_type=jnp.float32)
        p = jnp.exp(s - jnp.max(s, -1, keepdims=True))
        o = jnp.dot(p.astype(jnp.bfloat16), v[:, sl], preferred_element_type=jnp.float32)
        outs.append(o / jnp.sum(p, -1, keepdims=True))
    ya_ref[...] = jnp.concatenate(outs, -1).astype(ya_ref.dtype)
    tq = daq_ref.shape[0]
    qs = _stack_masked_q(daq_ref[...])
    m0 = jnp.full((8 * tq, 1), -jnp.inf, jnp.float32)
    m, l, acc = _da_tile(qs, dak_ref[...], dav_ref[...], m0, jnp.zeros((8 * tq, 1), jnp.float32),
                         jnp.zeros((8 * tq, 64), jnp.float32), tq)
    yd_ref[...] = _da_finish(acc, l, lam_ref[0], gain_ref[...], g64_ref[...], post_scale).astype(yd_ref.dtype)


def ctx_attn_call(lam, NAQ, NAK, P_na, DAQ, DAK, P_da, subln_g, *, n_batch, seq, ctx_len, lam_init):
    cb = n_batch * seq // ctx_len
    g64 = _group_ones(256, 64)
    gain = jnp.tile(subln_g, 4).reshape(1, 256)
    blk = lambda col: pl.BlockSpec((ctx_len, 256), lambda b: (cb + b, col))
    return pl.pallas_call(
        functools.partial(_ctx_kernel, post_scale=1.0 - lam_init),
        grid=(n_batch,),
        in_specs=[pl.BlockSpec(memory_space=pltpu.SMEM), blk(0), blk(0), blk(2), blk(0), blk(0), blk(2),
                  pl.BlockSpec((1, 256), lambda b: (0, 0)), pl.BlockSpec((256, 256), lambda b: (0, 0))],
        out_specs=[pl.BlockSpec((ctx_len, 256), lambda b: (b, 0))] * 2,
        out_shape=[jax.ShapeDtypeStruct((n_batch * ctx_len, 256), jnp.bfloat16)] * 2,
        name="ctx_attn",
    )(lam.reshape(1), NAQ, NAK, P_na, DAQ, DAK, P_da, gain, g64)


def rmsnorm(x, g):
    xf = x.astype(jnp.float32)
    y = xf * lax.rsqrt(jnp.mean(xf * xf, axis=-1, keepdims=True) + RMS_EPS)
    return (y * g.astype(jnp.float32)).astype(x.dtype)


def l2norm(x):
    xf = x.astype(jnp.float32)
    return xf * lax.rsqrt(jnp.sum(xf * xf, axis=-1, keepdims=True) + RMS_EPS)


def modulate(h, shift, scale):
    return h * (1.0 + scale) + shift


def short_conv(x, w):
    K = w.shape[0]
    pad = K // 2
    T = x.shape[1]
    xp = jnp.pad(x, ((0, 0), (pad, pad), (0, 0)))
    y = xp[:, 0:T] * w[0]
    for i in range(1, K):
        y = y + xp[:, i:i + T] * w[i]
    return jax.nn.silu(y)


def gdn_prepare(qkv, ab, conv_w, a_log, dt_bias):
    B, T, _ = qkv.shape
    q, k, v = jnp.split(short_conv(qkv, conv_w), 3, axis=-1)
    to_heads = lambda u: jnp.moveaxis(u.reshape(B, T, GDN_HEADS, GDN_DIM), 1, 2)
    q = l2norm(to_heads(q)) * GDN_DIM ** -0.5
    k = l2norm(to_heads(k))
    v = to_heads(v).astype(jnp.float32)
    abf = ab.astype(jnp.float32).reshape(B, T, 2, 2, GDN_HEADS)
    g = -jnp.exp(a_log.astype(jnp.float32)) * jax.nn.softplus(abf[:, :, 0] + dt_bias.astype(jnp.float32))
    beta = jax.nn.sigmoid(abf[:, :, 1])
    g = jnp.transpose(g, (2, 0, 3, 1))
    beta = jnp.transpose(beta, (2, 0, 3, 1))
    return q, k, v, g, beta


def gdn_chunked(q, k, v, g, beta, s0):
    B, H, T, dk = q.shape
    dv = v.shape[-1]
    C = GDN_CHUNK
    N = T // C
    q, k, v = (u.reshape(B, H, N, C, -1) for u in (q, k, v))
    g = g.reshape(B, H, N, C)
    beta = beta.reshape(B, H, N, C)
    G = jnp.cumsum(g, axis=-1)
    incl = jnp.tril(jnp.ones((C, C), dtype=bool))
    strict = jnp.tril(jnp.ones((C, C), dtype=bool), -1)
    decay = jnp.where(incl, jnp.exp(jnp.where(incl, G[..., :, None] - G[..., None, :], 0.0)), 0.0)
    kb = k * beta[..., None]
    a = jnp.where(strict, jnp.einsum('bhnid,bhnjd->bhnij', kb, k) * decay, 0.0)
    lhs = a + jnp.eye(C, dtype=a.dtype)
    u = lax.linalg.triangular_solve(lhs, v * beta[..., None], left_side=True, lower=True, unit_diagonal=True)
    w = lax.linalg.triangular_solve(lhs, kb * jnp.exp(G)[..., None], left_side=True, lower=True, unit_diagonal=True)
    qk = jnp.where(incl, jnp.einsum('bhnid,bhnjd->bhnij', q, k) * decay, 0.0)
    q_dec = q * jnp.exp(G)[..., None]
    k_dec = k * jnp.exp(G[..., -1:] - G)[..., None]
    g_last = jnp.exp(G[..., -1])

    def step(S, xs):
        qk_c, qd_c, kd_c, u_c, w_c, gl_c = xs
        v_new = u_c - jnp.einsum('bhcd,bhde->bhce', w_c, S)
        o = jnp.einsum('bhcd,bhde->bhce', qd_c, S) + jnp.einsum('bhij,bhje->bhie', qk_c, v_new)
        S = S * gl_c[..., None, None] + jnp.einsum('bhcd,bhce->bhde', kd_c, v_new)
        return S, o

    xs = tuple(jnp.moveaxis(t, 2, 0) for t in (qk, q_dec, k_dec, u, w, g_last))
    s_final, o = lax.scan(step, s0, xs)
    return jnp.moveaxis(o, 0, 2).reshape(B, H, T, dv), s_final


def gdn_mixer(qkv, ab, z, qkv_c, ab_c, z_c, conv_w, a_log, dt_bias, onorm_g, ctx_out):
    ql, kl, vl, gl, bl = gdn_prepare(qkv, ab, conv_w, a_log, dt_bias)
    qc, kc, vc, gc, bc = gdn_prepare(qkv_c, ab_c, conv_w, a_log, dt_bias)
    B = qkv.shape[0]
    s0 = jnp.zeros((B, GDN_HEADS, GDN_DIM, GDN_DIM), jnp.float32)
    o_lat, o_ctx = 0.0, 0.0
    for d in range(2):
        fl = (lambda u: jnp.flip(u, axis=2)) if d == 1 else (lambda u: u)
        oc, s_ctx = gdn_chunked(fl(qc), fl(kc), fl(vc), fl(gc[d]), fl(bc[d]), s0)
        ol, _ = gdn_chunked(fl(ql), fl(kl), fl(vl), fl(gl[d]), fl(bl[d]), s_ctx)
        o_lat = o_lat + fl(ol)
        if ctx_out:
            o_ctx = o_ctx + fl(oc)

    def gate_out(o, zz):
        Bq, H, T, dv = o.shape
        y = rmsnorm(jnp.moveaxis(o, 1, 2), onorm_g) * jax.nn.silu(zz.reshape(Bq, T, H, dv).astype(jnp.float32))
        return y.reshape(Bq, T, H * dv).astype(zz.dtype)

    return gate_out(o_lat, z), (gate_out(o_ctx, z_c) if ctx_out else None)


def fourier_mix(f):
    B, T, _ = f.shape
    fg = f.astype(jnp.float32).reshape(B, T, FNET_GROUPS, FNET_GROUP_W)
    y = jnp.real(jnp.fft.fft2(fg, axes=(1, 3), norm='ortho'))
    return y.reshape(B, T, BRANCH_W).astype(f.dtype)


def merge_branches(branches, gate_in, w_branch, w_out):
    br = jnp.stack(branches, axis=2)
    proj = jnp.einsum('btiw,iwd->btid', br, w_branch)
    gates = jax.nn.sigmoid(gate_in.reshape(*gate_in.shape[:2], N_BRANCH, D_MODEL))
    return jnp.sum(gates * proj, axis=2) @ w_out


def moe_ffn(h, w_router, router_bias, w_g, w_u, w_d):
    N, D = h.shape
    per_group = N_EXPERTS // N_GROUPS
    scores = jax.nn.sigmoid((h @ w_router).astype(jnp.float32))
    sel = scores + router_bias.astype(jnp.float32)
    group_score = lax.top_k(sel.reshape(N, N_GROUPS, per_group), 2)[0].sum(-1)
    best = jnp.argmax(group_score, axis=-1)
    in_group = (jnp.arange(N_EXPERTS) // per_group)[None, :] == best[:, None]
    _, idx = lax.top_k(jnp.where(in_group, sel, NEG_INF), TOP_K)
    wts = jnp.take_along_axis(scores, idx, axis=-1)
    wts = wts / jnp.sum(wts, axis=-1, keepdims=True)
    A = N * TOP_K
    expert = idx.reshape(A)
    token = jnp.repeat(jnp.arange(N, dtype=jnp.int32), TOP_K)
    weight = wts.reshape(A)
    order = jnp.argsort(expert)
    e_sorted = expert[order]
    counts = jnp.bincount(expert, length=N_EXPERTS)
    padded = (counts + MOE_BLOCK - 1) // MOE_BLOCK * MOE_BLOCK
    pad_end = jnp.cumsum(padded)
    pad_start = pad_end - padded
    start = jnp.cumsum(counts) - counts
    dest = pad_start[e_sorted] + jnp.arange(A) - start[e_sorted]
    n_blocks = -(-A // MOE_BLOCK) + N_EXPERTS
    P = n_blocks * MOE_BLOCK
    row_token = jnp.full((P,), N, jnp.int32).at[dest].set(token[order])
    row_weight = jnp.zeros((P,), h.dtype).at[dest].set(weight[order].astype(h.dtype))
    block_expert = jnp.minimum(jnp.searchsorted(pad_end, jnp.arange(n_blocks) * MOE_BLOCK, side='right'), N_EXPERTS - 1)
    h_pad = jnp.concatenate([h, jnp.zeros((1, D), h.dtype)], axis=0)
    xs = h_pad[row_token].reshape(n_blocks, MOE_BLOCK, D)

    def expert_block(args):
        xb, e = args
        return (jax.nn.silu(xb @ w_g[e]) * (xb @ w_u[e])) @ w_d[e]

    y = lax.map(expert_block, (xs, block_expert)).reshape(P, D) * row_weight[:, None]
    return jnp.zeros((N + 1, D), h.dtype).at[row_token].add(y)[:N]


def kernel(x, c, ctx, c_ctx, w_mod, b_mod, norm1_g, norm2_g, w_in, na_qn_g, na_kn_g, na_rpb,
           da_qn_g, da_kn_g, da_lam_q1, da_lam_k1, da_lam_q2, da_lam_k2, da_subln_g, gdn_conv_w,
           gdn_a_log, gdn_dt_bias, gdn_onorm_g, w_branch, w_out, w_router, router_bias,
           moe_w_gate, moe_w_up, moe_w_down):
    B, S, D = x.shape
    L = ctx.shape[1]
    NL = B * S
    f32 = jnp.float32
    cvec = jnp.zeros((8, D), f32).at[:B].set(c).at[B].set(c_ctx)
    mods = mod_call(cvec, w_mod, b_mod)
    cos, sin = rope_tables(S, 256)
    for l in range(DEPTH):
        last = l == DEPTH - 1
        lam_init = 0.8 - 0.6 * math.exp(-0.3 * l)
        mod = mods[l, :B]
        mod_c = mods[l, B]
        sh1, sc1, g1, sh2, sc2, g2 = jnp.split(mod[:, None, :], 6, axis=-1)
        sh1c, sc1c, g1c, sh2c, sc2c, g2c = jnp.split(mod_c, 6)
        X = jnp.concatenate([x.reshape(NL, D), ctx.reshape(B * L, D)], 0)
        w_main = jnp.concatenate([w_in[l][:, :2560], w_in[l][:, 2576:]], 1).astype(jnp.bfloat16)
        w_ab = jnp.pad(w_in[l][:, 2560:2576], ((0, 0), (0, 112))).astype(jnp.bfloat16)
        P_na, P_da, P_gdn, P_z, P_f, P_gate, P_ab = inproj_call(X, mods[l], norm1_g[l], w_main, w_ab,
                                                                n_batch=B, seq=S)
        NAQ, NAK, DAQ, DAK = prep_call(P_na, P_da, cos, sin, na_qn_g[l], na_kn_g[l], da_qn_g[l], da_kn_g[l],
                                       n_batch=B, seq=S)
        bias = na_bias_tables(na_rpb[l], S // GRID_W)
        lam = (jnp.exp(jnp.sum(da_lam_q1[l] * da_lam_k1[l])) - jnp.exp(jnp.sum(da_lam_q2[l] * da_lam_k2[l]))
               + lam_init).astype(f32)
        Ya = na_call(NAQ, NAK, P_na, bias, n_batch=B, seq=S, ctx_len=L)
        Yd = da_call(lam, DAQ, DAK, P_da, da_subln_g[l], n_batch=B, seq=S, ctx_len=L, lam_init=lam_init)
        lat = lambda a: a[:NL].astype(f32).reshape(B, S, -1)
        cx = lambda a: a[NL:].astype(f32).reshape(B, L, -1)
        yg, yg_c = gdn_mixer(lat(P_gdn), lat(P_ab)[..., :16], lat(P_z), cx(P_gdn), cx(P_ab)[..., :16], cx(P_z),
                             gdn_conv_w[l], gdn_a_log[l], gdn_dt_bias[l], gdn_onorm_g[l], not last)
        y = merge_branches((Ya.astype(f32).reshape(B, S, -1), Yd.astype(f32).reshape(B, S, -1), yg,
                            fourier_mix(lat(P_f))), lat(P_gate), w_branch[l], w_out[l])
        x = x + g1 * y
        h2 = modulate(rmsnorm(x, norm2_g[l]), sh2, sc2).reshape(B * S, D_MODEL)
        if last:
            f = moe_ffn(h2, w_router, router_bias, moe_w_gate[l], moe_w_up[l], moe_w_down[l])
            x = x + g2 * f.reshape(B, S, D_MODEL)
        else:
            Yac, Ydc = ctx_attn_call(lam, NAQ, NAK, P_na, DAQ, DAK, P_da, da_subln_g[l],
                                     n_batch=B, seq=S, ctx_len=L, lam_init=lam_init)
            y_c = merge_branches((Yac.astype(f32).reshape(B, L, -1), Ydc.astype(f32).reshape(B, L, -1), yg_c,
                                  fourier_mix(cx(P_f))), cx(P_gate), w_branch[l], w_out[l])
            ctx = ctx + g1c * y_c
            h2c = modulate(rmsnorm(ctx, norm2_g[l]), sh2c, sc2c).reshape(B * L, D_MODEL)
            f = moe_ffn(jnp.concatenate([h2, h2c], axis=0), w_router, router_bias,
                        moe_w_gate[l], moe_w_up[l], moe_w_down[l])
            x = x + g2 * f[:B * S].reshape(B, S, D_MODEL)
            ctx = ctx + g2c * f[B * S:].reshape(B, L, D_MODEL)
    return x
```

```python
import functools
import math
import jax
import jax.numpy as jnp
from jax import lax
import numpy as np
from jax.experimental import pallas as pl
from jax.experimental.pallas import tpu as pltpu

D_MODEL = 1024
DEPTH = 2
GRID_W = 64
N_BRANCH = 4
BRANCH_W = D_MODEL // N_BRANCH
GDN_HEADS = 4
GDN_DIM = BRANCH_W // GDN_HEADS
GDN_CHUNK = 64
FNET_GROUPS = 4
FNET_GROUP_W = BRANCH_W // FNET_GROUPS
N_EXPERTS = 16
N_GROUPS = 4
TOP_K = 2
MOE_BLOCK = 128
RMS_EPS = 1e-6
NEG_INF = -1e30
ROPE_BASE = 10000.0
VMEM_LIMIT = 56 * 1024 * 1024
HI = lax.Precision.HIGHEST
NT_DIMS = (((1,), (1,)), ((), ()))
LOG2E = 1.4426950408889634


def _mod_kernel(c_ref, w_ref, b_ref, o_ref):
    c = c_ref[...]
    a = c * jax.nn.sigmoid(c)
    o_ref[0] = jnp.dot(a, w_ref[0], preferred_element_type=jnp.float32, precision=HI) + b_ref[0]


def mod_call(cvec, w_mod, b_mod, tn=512):
    depth, D, N = w_mod.shape
    return pl.pallas_call(
        _mod_kernel,
        grid=(depth, N // tn),
        in_specs=[pl.BlockSpec((8, D), lambda l, j: (0, 0)),
                  pl.BlockSpec((1, D, tn), lambda l, j: (l, 0, j)),
                  pl.BlockSpec((1, 1, tn), lambda l, j: (l, 0, j))],
        out_specs=pl.BlockSpec((1, 8, tn), lambda l, j: (l, 0, j)),
        out_shape=jax.ShapeDtypeStruct((depth, 8, N), jnp.float32),
        name="mod",
    )(cvec, w_mod, b_mod.reshape(depth, 1, N))


def _inproj_kernel(x_ref, m_ref, g_ref, w_ref, wab_ref, na_ref, da_ref, gdn_ref, z_ref, f_ref, gate_ref, ab_ref,
                   *, tiles_per_batch, n_lat_tiles, n_batch):
    i = pl.program_id(0)
    r = jnp.where(i < n_lat_tiles, i // tiles_per_batch, n_batch)
    x = x_ref[...]
    y = x * lax.rsqrt(jnp.mean(x * x, axis=-1, keepdims=True) + RMS_EPS) * g_ref[...]
    sh = m_ref[pl.ds(r, 1), 0:D_MODEL]
    sc = m_ref[pl.ds(r, 1), D_MODEL:2 * D_MODEL]
    h = (y * (1.0 + sc) + sh).astype(jnp.bfloat16)

    def seg(o_ref, a, b):
        o_ref[...] = jnp.dot(h, w_ref[:, a:b], preferred_element_type=jnp.float32).astype(o_ref.dtype)

    seg(na_ref, 0, 768)
    seg(da_ref, 768, 1536)
    seg(gdn_ref, 1536, 2304)
    seg(z_ref, 2304, 2560)
    seg(f_ref, 2560, 2816)
    for k in range(4):
        gate_ref[:, k * 1024:(k + 1) * 1024] = jnp.dot(
            h, w_ref[:, 2816 + k * 1024:2816 + (k + 1) * 1024],
            preferred_element_type=jnp.float32).astype(gate_ref.dtype)
    ab_ref[...] = jnp.dot(h, wab_ref[...], preferred_element_type=jnp.float32)


def inproj_call(X, mod_l, g, w_main, w_ab, *, n_batch, seq, tm=256):
    NT, D = X.shape
    n_lat_tiles = n_batch * seq // tm
    kern = functools.partial(_inproj_kernel, tiles_per_batch=seq // tm, n_lat_tiles=n_lat_tiles, n_batch=n_batch)
    widths = [(768, jnp.bfloat16), (768, jnp.bfloat16), (768, jnp.float32), (256, jnp.bfloat16),
              (256, jnp.bfloat16), (4096, jnp.bfloat16), (128, jnp.float32)]
    return pl.pallas_call(
        kern,
        grid=(NT // tm,),
        in_specs=[pl.BlockSpec((tm, D), lambda i: (i, 0)),
                  pl.BlockSpec(mod_l.shape, lambda i: (0, 0)),
                  pl.BlockSpec((1, D), lambda i: (0, 0)),
                  pl.BlockSpec(w_main.shape, lambda i: (0, 0)),
                  pl.BlockSpec(w_ab.shape, lambda i: (0, 0))],
        out_specs=[pl.BlockSpec((tm, w), lambda i: (i, 0)) for w, _ in widths],
        out_shape=[jax.ShapeDtypeStruct((NT, w), dt) for w, dt in widths],
        compiler_params=pltpu.CompilerParams(vmem_limit_bytes=VMEM_LIMIT),
        name="inproj",
    )(X, mod_l, g.reshape(1, D), w_main, w_ab)


def _group_ones(width, group):
    i = np.arange(width)
    return jnp.asarray((i[:, None] // group == i[None, :] // group).astype(np.float32))


def _group_rms(x, gmat, group, gain):
    ss = jnp.dot(x * x, gmat, preferred_element_type=jnp.float32, precision=HI)
    return x * lax.rsqrt(ss * (1.0 / group) + RMS_EPS) * gain


def rope_tables(seq, tm):
    nf = 8
    t = jnp.arange(seq)
    rows = (t // GRID_W).astype(jnp.float32)
    cols = (t % GRID_W).astype(jnp.float32)
    freqs = ROPE_BASE ** (-jnp.arange(nf, dtype=jnp.float32) / nf)
    d = np.arange(32)
    f_idx = d % 8
    use_col = (d // 16) == 1
    ang = jnp.where(use_col[None, :], cols[:, None], rows[:, None]) * freqs[f_idx][None, :]
    sign = np.where((d % 16) < 8, -1.0, 1.0).astype(np.float32)
    cos = jnp.tile(jnp.cos(ang), (1, 8))
    sin = jnp.tile(jnp.sin(ang) * sign[None, :], (1, 8))
    cos = jnp.concatenate([cos, jnp.ones((tm, 256), jnp.float32)], 0)
    sin = jnp.concatenate([sin, jnp.zeros((tm, 256), jnp.float32)], 0)
    return cos, sin


def _prep_kernel(na_ref, da_ref, cos_ref, sin_ref, g64_ref, g32_ref, gains_ref, naq_ref, nak_ref, daq_ref, dak_ref,
                 davx_ref):
    g64 = g64_ref[...]
    g32 = g32_ref[...]
    na = na_ref[...].astype(jnp.float32)
    naq_ref[...] = (_group_rms(na[:, 0:256], g64, 64, gains_ref[0:1, :]) * (64 ** -0.5)).astype(naq_ref.dtype)
    nak_ref[...] = _group_rms(na[:, 256:512], g64, 64, gains_ref[1:2, :]).astype(nak_ref.dtype)
    da = da_ref[...].astype(jnp.float32)
    cos = cos_ref[...]
    sin = sin_ref[...]
    lane = lax.broadcasted_iota(jnp.int32, (1, 256), 1)
    first = (lane % 16) < 8

    def rope(x):
        swapped = jnp.where(first, pltpu.roll(x, 256 - 8, 1), pltpu.roll(x, 8, 1))
        return x * cos + swapped * sin

    q = rope(_group_rms(da[:, 0:256], g32, 32, gains_ref[2:3, :]))
    k = rope(_group_rms(da[:, 256:512], g32, 32, gains_ref[3:4, :]))
    daq_ref[...] = (q * (32 ** -0.5 * LOG2E)).astype(daq_ref.dtype)
    dak_ref[...] = k.astype(dak_ref.dtype)
    v = da_ref[:, 512:768]
    one_col = (lax.broadcasted_iota(jnp.int32, (v.shape[0], 64), 1) == 0).astype(v.dtype)
    davx_ref[...] = jnp.concatenate(
        [piece for h in range(4) for piece in (v[:, h * 64:(h + 1) * 64], one_col)], -1)


def prep_call(P_na, P_da, cos, sin, na_qg, na_kg, da_qg, da_kg, *, n_batch, seq, tm=256):
    NT = P_na.shape[0]
    tpb = seq // tm
    n_lat = n_batch * tpb
    gains = jnp.stack([jnp.tile(na_qg, 4), jnp.tile(na_kg, 4), jnp.tile(da_qg, 8), jnp.tile(da_kg, 8)]
                      + [jnp.zeros((256,), jnp.float32)] * 4, 0)
    tab_map = lambda i: (jnp.where(i < n_lat, i % tpb, tpb), 0)
    full = lambda a: pl.BlockSpec(a.shape, lambda i: (0,) * a.ndim)
    g64, g32 = _group_ones(256, 64), _group_ones(256, 32)
    return pl.pallas_call(
        _prep_kernel,
        grid=(NT // tm,),
        in_specs=[pl.BlockSpec((tm, 768), lambda i: (i, 0)), pl.BlockSpec((tm, 768), lambda i: (i, 0)),
                  pl.BlockSpec((tm, 256), tab_map), pl.BlockSpec((tm, 256), tab_map),
                  full(g64), full(g32), full(gains)],
        out_specs=[pl.BlockSpec((tm, 256), lambda i: (i, 0))] * 4 + [pl.BlockSpec((tm, 512), lambda i: (i, 0))],
        out_shape=[jax.ShapeDtypeStruct((NT, 256), jnp.bfloat16)] * 4 + [jax.ShapeDtypeStruct((NT, 512), jnp.bfloat16)],
        name="attn_prep",
    )(P_na, P_da, cos, sin, g64, g32, gains)


NA_RB = 4
NA_UR = 11


def na_bias_tables(rpb, rows):
    reps = [(0, 0), (4, 0), (rows - NA_RB, rows - NA_UR)]
    qc = np.arange(GRID_W)[:, None]
    kc = np.arange(GRID_W)[None, :]
    ws = np.clip(qc - 8, 0, GRID_W - 16)
    vcol = (kc >= ws) & (kc < ws + 16)
    rel_c = np.clip(kc - qc + 15, 0, 30)
    sel_c = (rel_c[..., None] == np.arange(31)).astype(np.float32)
    sel_r, vrow = [], []
    for r0, u0 in reps:
        r = r0 + np.arange(NA_RB)[:, None]
        krow = u0 + np.arange(NA_UR)[None, :]
        start = np.clip(r - 4, 0, rows - 8)
        vrow.append((krow >= start) & (krow < start + 8))
        sel_r.append((np.clip(krow - r + 7, 0, 14)[..., None] == np.arange(15)).astype(np.float32))
    sel_r, vrow = np.stack(sel_r), np.stack(vrow)
    b = jnp.einsum('taui,hij,qkj->thaquk', jnp.asarray(sel_r), rpb.astype(jnp.float32), jnp.asarray(sel_c),
                   precision=HI)
    valid = vrow[:, None, :, None, :, None] & vcol[None, None, None, :, None, :]
    b = jnp.where(valid, b, NEG_INF)
    return b.reshape(3, rpb.shape[0], NA_RB * GRID_W, NA_UR * GRID_W)


def _na_kernel(q_ref, k_ref, v_ref, kc_ref, vc_ref, bias_ref, o_ref, *, rows):
    i = pl.program_id(1)
    u0 = jnp.clip(i * NA_RB - 4, 0, rows - NA_UR)
    off = pl.multiple_of(u0 * GRID_W, GRID_W)
    nkw = NA_UR * GRID_W
    kwin = k_ref[pl.ds(off, nkw), :]
    vwin = v_ref[pl.ds(off, nkw), :]
    q = q_ref[...]
    kc = kc_ref[...]
    vc = vc_ref[...]
    outs = []
    for h in range(4):
        sl = slice(h * 64, (h + 1) * 64)
        qh = q[:, sl]
        s_loc = lax.dot_general(qh, kwin[:, sl], NT_DIMS, preferred_element_type=jnp.float32) + bias_ref[0, h]
        s_ctx = lax.dot_general(qh, kc[:, sl], NT_DIMS, preferred_element_type=jnp.float32)
        m = jnp.maximum(jnp.max(s_loc, -1, keepdims=True), jnp.max(s_ctx, -1, keepdims=True))
        p_loc = jnp.exp(s_loc - m)
        p_ctx = jnp.exp(s_ctx - m)
        l = jnp.sum(p_loc, -1, keepdims=True) + jnp.sum(p_ctx, -1, keepdims=True)
        o = (jnp.dot(p_loc.astype(jnp.bfloat16), vwin[:, sl], preferred_element_type=jnp.float32)
             + jnp.dot(p_ctx.astype(jnp.bfloat16), vc[:, sl], preferred_element_type=jnp.float32))
        outs.append(o / l)
    o_ref[...] = jnp.concatenate(outs, -1).astype(o_ref.dtype)


def na_call(NAQ, NAK, P_na, bias, *, n_batch, seq, ctx_len):
    rows = seq // GRID_W
    nblk = rows // NA_RB
    tq = NA_RB * GRID_W
    cb = n_batch * seq // ctx_len
    kern = functools.partial(_na_kernel, rows=rows)
    tsel = lambda b, i: (jnp.where(i == 0, 0, jnp.where(i == nblk - 1, 2, 1)), 0, 0, 0)
    return pl.pallas_call(
        kern,
        grid=(n_batch, nblk),
        in_specs=[pl.BlockSpec((tq, 256), lambda b, i: (b * nblk + i, 0)),
                  pl.BlockSpec((seq, 256), lambda b, i: (b, 0)),
                  pl.BlockSpec((seq, 256), lambda b, i: (b, 2)),
                  pl.BlockSpec((ctx_len, 256), lambda b, i: (cb + b, 0)),
                  pl.BlockSpec((ctx_len, 256), lambda b, i: (cb + b, 2)),
                  pl.BlockSpec((1,) + bias.shape[1:], tsel)],
        out_specs=pl.BlockSpec((tq, 256), lambda b, i: (b * nblk + i, 0)),
        out_shape=jax.ShapeDtypeStruct((n_batch * seq, 256), jnp.bfloat16),
        compiler_params=pltpu.CompilerParams(vmem_limit_bytes=VMEM_LIMIT),
        name="na_attn",
    )(NAQ, NAK, P_na, NAK, P_na, bias)


def _stack_masked_q(q):
    lane = lax.broadcasted_iota(jnp.int32, (1, 256), 1)
    return jnp.concatenate([jnp.where((lane // 32) == hm, q, jnp.zeros_like(q)) for hm in range(8)], 0)


def _da_finish(acc, lam, gain, g64, post_scale):
    tq = acc.shape[0] // 8
    outs = []
    for h in range(4):
        a0 = acc[(2 * h) * tq:(2 * h + 1) * tq]
        a1 = acc[(2 * h + 1) * tq:(2 * h + 2) * tq]
        outs.append(a0[:, 0:64] / a0[:, 64:65] - lam * (a1[:, 0:64] / a1[:, 64:65]))
    o = jnp.concatenate(outs, -1)
    return _group_rms(o, g64, 64, gain) * post_scale


def _da_tile(qs, k, vx, m, acc, tq):
    tk = k.shape[0]
    s = lax.dot_general(qs, k, NT_DIMS, preferred_element_type=jnp.float32)
    m_new = jnp.maximum(m, jnp.max(s, -1, keepdims=True))
    alpha = jnp.exp2(m - m_new)
    p = jnp.exp2(s - pltpu.repeat(m_new, tk // 128, axis=1)).astype(jnp.bfloat16)
    pv = jnp.concatenate(
        [jnp.dot(p[2 * h * tq:(2 * h + 2) * tq], vx[:, h * 128:(h + 1) * 128], preferred_element_type=jnp.float32)
         for h in range(4)], 0)
    return m_new, alpha * acc + pv


def _da_kernel(lam_ref, q_ref, k_ref, vx_ref, kc_ref, vxc_ref, gain_ref, g64_ref, o_ref, m_ref, acc_ref,
               *, tk, n_kv, post_scale):
    tq = q_ref.shape[0]
    qs = _stack_masked_q(q_ref[...])
    m_ref[...] = jnp.full(m_ref.shape, -jnp.inf, jnp.float32)
    acc_ref[...] = jnp.zeros(acc_ref.shape, jnp.float32)

    def body(j, carry):
        for u in range(2):
            off = pl.multiple_of((2 * j + u) * tk, tk)
            m, acc = _da_tile(qs, k_ref[pl.ds(off, tk), :], vx_ref[pl.ds(off, tk), :], m_ref[...], acc_ref[...], tq)
            m_ref[...] = m
            acc_ref[...] = acc
        return carry

    lax.fori_loop(0, n_kv // 2, body, 0)
    m, acc = _da_tile(qs, kc_ref[...], vxc_ref[...], m_ref[...], acc_ref[...], tq)
    o_ref[...] = _da_finish(acc, lam_ref[0], gain_ref[...], g64_ref[...], post_scale).astype(o_ref.dtype)


def da_call(lam, DAQ, DAK, DAVX, subln_g, *, n_batch, seq, ctx_len, lam_init, tq=128, tk=512):
    nq = seq // tq
    cb = n_batch * seq // ctx_len
    g64 = _group_ones(256, 64)
    gain = jnp.tile(subln_g, 4).reshape(1, 256)
    n_kv = seq // tk
    assert n_kv % 2 == 0
    kern = functools.partial(_da_kernel, tk=tk, n_kv=n_kv, post_scale=1.0 - lam_init)
    return pl.pallas_call(
        kern,
        grid=(n_batch, nq),
        in_specs=[pl.BlockSpec(memory_space=pltpu.SMEM),
                  pl.BlockSpec((tq, 256), lambda b, i: (b * nq + i, 0)),
                  pl.BlockSpec((seq, 256), lambda b, i: (b, 0)),
                  pl.BlockSpec((seq, 512), lambda b, i: (b, 0)),
                  pl.BlockSpec((ctx_len, 256), lambda b, i: (cb + b, 0)),
                  pl.BlockSpec((ctx_len, 512), lambda b, i: (cb + b, 0)),
                  pl.BlockSpec((1, 256), lambda b, i: (0, 0)),
                  pl.BlockSpec((256, 256), lambda b, i: (0, 0))],
        out_specs=pl.BlockSpec((tq, 256), lambda b, i: (b * nq + i, 0)),
        out_shape=jax.ShapeDtypeStruct((n_batch * seq, 256), jnp.bfloat16),
        scratch_shapes=[pltpu.VMEM((8 * tq, 128), jnp.float32), pltpu.VMEM((8 * tq, 128), jnp.float32)],
        compiler_params=pltpu.CompilerParams(vmem_limit_bytes=VMEM_LIMIT),
        name="da_attn",
    )(lam.reshape(1), DAQ, DAK, DAVX, DAK, DAVX, gain, g64)


def _ctx_kernel(lam_ref, naq_ref, nak_ref, nav_ref, daq_ref, dak_ref, dav_ref, gain_ref, g64_ref, ya_ref, yd_ref,
                *, post_scale):
    q = naq_ref[...]
    k = nak_ref[...]
    v = nav_ref[...]
    outs = []
    for h in range(4):
        sl = slice(h * 64, (h + 1) * 64)
        s = lax.dot_general(q[:, sl], k[:, sl], NT_DIMS, preferred_element_type=jnp.float32)
        p = jnp.exp(s - jnp.max(s, -1, keepdims=True))
        o = jnp.dot(p.astype(jnp.bfloat16), v[:, sl], preferred_element_type=jnp.float32)
        outs.append(o / jnp.sum(p, -1, keepdims=True))
    ya_ref[...] = jnp.concatenate(outs, -1).astype(ya_ref.dtype)
    tq = daq_ref.shape[0]
    qs = _stack_masked_q(daq_ref[...])
    m0 = jnp.full((8 * tq, 128), -jnp.inf, jnp.float32)
    m, acc = _da_tile(qs, dak_ref[...], dav_ref[...], m0, jnp.zeros((8 * tq, 128), jnp.float32), tq)
    yd_ref[...] = _da_finish(acc, lam_ref[0], gain_ref[...], g64_ref[...], post_scale).astype(yd_ref.dtype)


def ctx_attn_call(lam, NAQ, NAK, P_na, DAQ, DAK, DAVX, subln_g, *, n_batch, seq, ctx_len, lam_init):
    cb = n_batch * seq // ctx_len
    g64 = _group_ones(256, 64)
    gain = jnp.tile(subln_g, 4).reshape(1, 256)
    blk = lambda col: pl.BlockSpec((ctx_len, 256), lambda b: (cb + b, col))
    return pl.pallas_call(
        functools.partial(_ctx_kernel, post_scale=1.0 - lam_init),
        grid=(n_batch,),
        in_specs=[pl.BlockSpec(memory_space=pltpu.SMEM), blk(0), blk(0), blk(2), blk(0), blk(0),
                  pl.BlockSpec((ctx_len, 512), lambda b: (cb + b, 0)),
                  pl.BlockSpec((1, 256), lambda b: (0, 0)), pl.BlockSpec((256, 256), lambda b: (0, 0))],
        out_specs=[pl.BlockSpec((ctx_len, 256), lambda b: (b, 0))] * 2,
        out_shape=[jax.ShapeDtypeStruct((n_batch * ctx_len, 256), jnp.bfloat16)] * 2,
        name="ctx_attn",
    )(lam.reshape(1), NAQ, NAK, P_na, DAQ, DAK, DAVX, gain, g64)


def _split_bf16(a):
    hi = a.astype(jnp.bfloat16)
    return hi, (a - hi.astype(jnp.float32)).astype(jnp.bfloat16)


def _route(logits_t, bias_col):
    per_group = N_EXPERTS // N_GROUPS
    scores = jax.nn.sigmoid(logits_t)
    sel_all = scores + bias_col
    sel = [sel_all[e:e + 1, :] for e in range(N_EXPERTS)]
    top2 = []
    for e in range(N_EXPERTS):
        g0 = e // per_group * per_group
        rank = jnp.zeros_like(sel[e])
        for o in range(g0, g0 + per_group):
            if o != e:
                beats = (sel[o] > sel[e]) | (sel[o] == sel[e]) if o < e else (sel[o] > sel[e])
                rank = rank + beats.astype(jnp.float32)
        top2.append(rank < 1.5)
    gs = []
    for g in range(N_GROUPS):
        acc = jnp.zeros_like(sel[0])
        for e in range(g * per_group, (g + 1) * per_group):
            acc = acc + jnp.where(top2[e], sel[e], 0.0)
        gs.append(acc)
    rows = []
    for g in range(N_GROUPS):
        beaten = jnp.zeros_like(sel[0])
        for o in range(N_GROUPS):
            if o != g:
                b = (gs[o] >= gs[g]) if o < g else (gs[o] > gs[g])
                beaten = beaten + b.astype(jnp.float32)
        best = beaten < 0.5
        for e in range(g * per_group, (g + 1) * per_group):
            rows.append(jnp.where(best & top2[e], scores[e:e + 1, :], 0.0))
    w = jnp.concatenate(rows, 0)
    return w / jnp.sum(w, axis=0, keepdims=True)


def _merge_kernel(x_ref, ya_ref, yd_ref, yg_ref, yf_ref, gate_ref, m_ref, g2_ref, wb_ref, wo_ref, wr_ref, rb_ref,
                  xo_ref, h2_ref, rw_ref, *, tiles_per_batch, n_lat_tiles, n_batch):
    i = pl.program_id(0)
    r = jnp.where(i < n_lat_tiles, i // tiles_per_batch, n_batch)
    mix = None
    for b, y_ref in enumerate((ya_ref, yd_ref, yg_ref, yf_ref)):
        proj = jnp.dot(y_ref[...], wb_ref[b], preferred_element_type=jnp.float32)
        term = jax.nn.sigmoid(gate_ref[:, b * D_MODEL:(b + 1) * D_MODEL].astype(jnp.float32)) * proj
        mix = term if mix is None else mix + term
    y = jnp.dot(mix.astype(jnp.bfloat16), wo_ref[...], preferred_element_type=jnp.float32)
    mrow = lambda k: m_ref[pl.ds(r, 1), k * D_MODEL:(k + 1) * D_MODEL]
    x = x_ref[...] + mrow(2) * y
    xo_ref[...] = x
    h2 = (x * lax.rsqrt(jnp.mean(x * x, axis=-1, keepdims=True) + RMS_EPS) * g2_ref[...]) * (1.0 + mrow(4)) + mrow(3)
    h2_ref[...] = h2.astype(h2_ref.dtype)
    h_hi, h_lo = _split_bf16(h2)
    w_hi, w_lo = _split_bf16(wr_ref[...])
    nt = lambda a, b: lax.dot_general(a, b, NT_DIMS, preferred_element_type=jnp.float32)
    logits_t = nt(w_hi, h_hi) + nt(w_hi, h_lo) + nt(w_lo, h_hi)
    w_t = _route(logits_t, rb_ref[...])
    pad = jnp.zeros((128 - N_EXPERTS, w_t.shape[1]), jnp.float32)
    rw_ref[...] = jnp.transpose(jnp.concatenate([w_t, pad], 0))


def merge_call(X, Ya, Yd, Yg, Yf, P_gate, mod_l, g2n, w_branch, w_out, w_router, router_bias,
               *, n_tok, n_batch, seq, tm=256):
    D = X.shape[1]
    n_lat_tiles = n_batch * seq // tm
    kern = functools.partial(_merge_kernel, tiles_per_batch=seq // tm, n_lat_tiles=n_lat_tiles, n_batch=n_batch)
    row = lambda w: pl.BlockSpec((tm, w), lambda i: (i, 0))
    full = lambda a: pl.BlockSpec(a.shape, lambda i: (0,) * a.ndim)
    wr_t = jnp.transpose(w_router)
    rb = router_bias.reshape(N_EXPERTS, 1)
    g2 = g2n.reshape(1, D)
    return pl.pallas_call(
        kern,
        grid=(n_tok // tm,),
        in_specs=[row(D), row(256), row(256), row(256), row(256), row(4 * D), full(mod_l), full(g2),
                  full(w_branch), full(w_out), full(wr_t), full(rb)],
        out_specs=[row(D), row(D), row(128)],
        out_shape=[jax.ShapeDtypeStruct((n_tok, D), jnp.float32), jax.ShapeDtypeStruct((n_tok, D), jnp.bfloat16),
                   jax.ShapeDtypeStruct((n_tok, 128), jnp.float32)],
        compiler_params=pltpu.CompilerParams(vmem_limit_bytes=VMEM_LIMIT),
        name="merge",
    )(X, Ya, Yd, Yg, Yf, P_gate, mod_l, g2, w_branch, w_out, wr_t, rb)


def _moe_kernel(x_ref, h_ref, rw_ref, m_ref, wg_ref, wu_ref, wd_ref, o_ref, acc_ref,
                *, tiles_per_batch, n_lat_tiles, n_batch):
    i = pl.program_id(0)
    e = pl.program_id(1)

    @pl.when(e == 0)
    def _():
        acc_ref[...] = jnp.zeros(acc_ref.shape, jnp.float32)

    h = h_ref[...]
    a = jnp.dot(h, wg_ref[0], preferred_element_type=jnp.float32)
    u = jnp.dot(h, wu_ref[0], preferred_element_type=jnp.float32)
    lane = lax.broadcasted_iota(jnp.int32, (1, 128), 1)
    w_col = jnp.sum(jnp.where(lane == e, rw_ref[...], 0.0), axis=-1, keepdims=True)
    act = (a * jax.nn.sigmoid(a) * u * w_col).astype(jnp.bfloat16)
    acc_ref[...] += jnp.dot(act, wd_ref[0], preferred_element_type=jnp.float32)

    @pl.when(e == N_EXPERTS - 1)
    def _():
        r = jnp.where(i < n_lat_tiles, i // tiles_per_batch, n_batch)
        g2 = m_ref[pl.ds(r, 1), 5 * D_MODEL:6 * D_MODEL]
        o_ref[...] = x_ref[...] + g2 * acc_ref[...]


def moe_call(Xmid, H2, RW, mod_l, w_gate, w_up, w_down, *, n_tok, n_batch, seq, tm=512):
    D = Xmid.shape[1]
    F = w_gate.shape[-1]
    n_lat_tiles = n_batch * seq // tm
    kern = functools.partial(_moe_kernel, tiles_per_batch=seq // tm, n_lat_tiles=n_lat_tiles, n_batch=n_batch)
    return pl.pallas_call(
        kern,
        grid=(n_tok // tm, N_EXPERTS),
        in_specs=[pl.BlockSpec((tm, D), lambda i, e: (i, 0)),
                  pl.BlockSpec((tm, D), lambda i, e: (i, 0)),
                  pl.BlockSpec((tm, 128), lambda i, e: (i, 0)),
                  pl.BlockSpec(mod_l.shape, lambda i, e: (0, 0)),
                  pl.BlockSpec((1, D, F), lambda i, e: (e, 0, 0)),
                  pl.BlockSpec((1, D, F), lambda i, e: (e, 0, 0)),
                  pl.BlockSpec((1, F, D), lambda i, e: (e, 0, 0))],
        out_specs=pl.BlockSpec((tm, D), lambda i, e: (i, 0)),
        out_shape=jax.ShapeDtypeStruct((n_tok, D), jnp.float32),
        scratch_shapes=[pltpu.VMEM((tm, D), jnp.float32)],
        compiler_params=pltpu.CompilerParams(vmem_limit_bytes=VMEM_LIMIT),
        name="moe",
    )(Xmid, H2, RW, mod_l, w_gate, w_up, w_down)


def rmsnorm(x, g):
    xf = x.astype(jnp.float32)
    y = xf * lax.rsqrt(jnp.mean(xf * xf, axis=-1, keepdims=True) + RMS_EPS)
    return (y * g.astype(jnp.float32)).astype(x.dtype)


def l2norm(x):
    xf = x.astype(jnp.float32)
    return xf * lax.rsqrt(jnp.sum(xf * xf, axis=-1, keepdims=True) + RMS_EPS)


def modulate(h, shift, scale):
    return h * (1.0 + scale) + shift


def short_conv(x, w):
    K = w.shape[0]
    pad = K // 2
    T = x.shape[1]
    xp = jnp.pad(x, ((0, 0), (pad, pad), (0, 0)))
    y = xp[:, 0:T] * w[0]
    for i in range(1, K):
        y = y + xp[:, i:i + T] * w[i]
    return jax.nn.silu(y)


def gdn_prepare(qkv, ab, conv_w, a_log, dt_bias):
    B, T, _ = qkv.shape
    q, k, v = jnp.split(short_conv(qkv, conv_w), 3, axis=-1)
    to_heads = lambda u: jnp.moveaxis(u.reshape(B, T, GDN_HEADS, GDN_DIM), 1, 2)
    q = l2norm(to_heads(q)) * GDN_DIM ** -0.5
    k = l2norm(to_heads(k))
    v = to_heads(v).astype(jnp.float32)
    abf = ab.astype(jnp.float32).reshape(B, T, 2, 2, GDN_HEADS)
    g = -jnp.exp(a_log.astype(jnp.float32)) * jax.nn.softplus(abf[:, :, 0] + dt_bias.astype(jnp.float32))
    beta = jax.nn.sigmoid(abf[:, :, 1])
    g = jnp.transpose(g, (2, 0, 3, 1))
    beta = jnp.transpose(beta, (2, 0, 3, 1))
    return q, k, v, g, beta


def gdn_chunked(q, k, v, g, beta, s0):
    B, H, T, dk = q.shape
    dv = v.shape[-1]
    C = GDN_CHUNK
    N = T // C
    q, k, v = (u.reshape(B, H, N, C, -1) for u in (q, k, v))
    g = g.reshape(B, H, N, C)
    beta = beta.reshape(B, H, N, C)
    G = jnp.cumsum(g, axis=-1)
    incl = jnp.tril(jnp.ones((C, C), dtype=bool))
    strict = jnp.tril(jnp.ones((C, C), dtype=bool), -1)
    decay = jnp.where(incl, jnp.exp(jnp.where(incl, G[..., :, None] - G[..., None, :], 0.0)), 0.0)
    kb = k * beta[..., None]
    a = jnp.where(strict, jnp.einsum('bhnid,bhnjd->bhnij', kb, k) * decay, 0.0)
    lhs = a + jnp.eye(C, dtype=a.dtype)
    u = lax.linalg.triangular_solve(lhs, v * beta[..., None], left_side=True, lower=True, unit_diagonal=True)
    w = lax.linalg.triangular_solve(lhs, kb * jnp.exp(G)[..., None], left_side=True, lower=True, unit_diagonal=True)
    qk = jnp.where(incl, jnp.einsum('bhnid,bhnjd->bhnij', q, k) * decay, 0.0)
    q_dec = q * jnp.exp(G)[..., None]
    k_dec = k * jnp.exp(G[..., -1:] - G)[..., None]
    g_last = jnp.exp(G[..., -1])

    def step(S, xs):
        qk_c, qd_c, kd_c, u_c, w_c, gl_c = xs
        v_new = u_c - jnp.einsum('bhcd,bhde->bhce', w_c, S)
        o = jnp.einsum('bhcd,bhde->bhce', qd_c, S) + jnp.einsum('bhij,bhje->bhie', qk_c, v_new)
        S = S * gl_c[..., None, None] + jnp.einsum('bhcd,bhce->bhde', kd_c, v_new)
        return S, o

    xs = tuple(jnp.moveaxis(t, 2, 0) for t in (qk, q_dec, k_dec, u, w, g_last))
    s_final, o = lax.scan(step, s0, xs)
    return jnp.moveaxis(o, 0, 2).reshape(B, H, T, dv), s_final


def gdn_mixer(qkv, ab, z, qkv_c, ab_c, z_c, conv_w, a_log, dt_bias, onorm_g, ctx_out):
    ql, kl, vl, gl, bl = gdn_prepare(qkv, ab, conv_w, a_log, dt_bias)
    qc, kc, vc, gc, bc = gdn_prepare(qkv_c, ab_c, conv_w, a_log, dt_bias)
    B = qkv.shape[0]
    s0 = jnp.zeros((B, GDN_HEADS, GDN_DIM, GDN_DIM), jnp.float32)
    o_lat, o_ctx = 0.0, 0.0
    for d in range(2):
        fl = (lambda u: jnp.flip(u, axis=2)) if d == 1 else (lambda u: u)
        oc, s_ctx = gdn_chunked(fl(qc), fl(kc), fl(vc), fl(gc[d]), fl(bc[d]), s0)
        ol, _ = gdn_chunked(fl(ql), fl(kl), fl(vl), fl(gl[d]), fl(bl[d]), s_ctx)
        o_lat = o_lat + fl(ol)
        if ctx_out:
            o_ctx = o_ctx + fl(oc)

    def gate_out(o, zz):
        Bq, H, T, dv = o.shape
        y = rmsnorm(jnp.moveaxis(o, 1, 2), onorm_g) * jax.nn.silu(zz.reshape(Bq, T, H, dv).astype(jnp.float32))
        return y.reshape(Bq, T, H * dv).astype(zz.dtype)

    return gate_out(o_lat, z), (gate_out(o_ctx, z_c) if ctx_out else None)


def fourier_mix(f):
    B, T, _ = f.shape
    fg = f.astype(jnp.float32).reshape(B, T, FNET_GROUPS, FNET_GROUP_W)
    y = jnp.real(jnp.fft.fft2(fg, axes=(1, 3), norm='ortho'))
    return y.reshape(B, T, BRANCH_W).astype(f.dtype)


def merge_branches(branches, gate_in, w_branch, w_out):
    br = jnp.stack(branches, axis=2)
    proj = jnp.einsum('btiw,iwd->btid', br, w_branch)
    gates = jax.nn.sigmoid(gate_in.reshape(*gate_in.shape[:2], N_BRANCH, D_MODEL))
    return jnp.sum(gates * proj, axis=2) @ w_out


def moe_ffn(h, w_router, router_bias, w_g, w_u, w_d):
    N, D = h.shape
    per_group = N_EXPERTS // N_GROUPS
    scores = jax.nn.sigmoid((h @ w_router).astype(jnp.float32))
    sel = scores + router_bias.astype(jnp.float32)
    group_score = lax.top_k(sel.reshape(N, N_GROUPS, per_group), 2)[0].sum(-1)
    best = jnp.argmax(group_score, axis=-1)
    in_group = (jnp.arange(N_EXPERTS) // per_group)[None, :] == best[:, None]
    _, idx = lax.top_k(jnp.where(in_group, sel, NEG_INF), TOP_K)
    wts = jnp.take_along_axis(scores, idx, axis=-1)
    wts = wts / jnp.sum(wts, axis=-1, keepdims=True)
    A = N * TOP_K
    expert = idx.reshape(A)
    token = jnp.repeat(jnp.arange(N, dtype=jnp.int32), TOP_K)
    weight = wts.reshape(A)
    order = jnp.argsort(expert)
    e_sorted = expert[order]
    counts = jnp.bincount(expert, length=N_EXPERTS)
    padded = (counts + MOE_BLOCK - 1) // MOE_BLOCK * MOE_BLOCK
    pad_end = jnp.cumsum(padded)
    pad_start = pad_end - padded
    start = jnp.cumsum(counts) - counts
    dest = pad_start[e_sorted] + jnp.arange(A) - start[e_sorted]
    n_blocks = -(-A // MOE_BLOCK) + N_EXPERTS
    P = n_blocks * MOE_BLOCK
    row_token = jnp.full((P,), N, jnp.int32).at[dest].set(token[order])
    row_weight = jnp.zeros((P,), h.dtype).at[dest].set(weight[order].astype(h.dtype))
    block_expert = jnp.minimum(jnp.searchsorted(pad_end, jnp.arange(n_blocks) * MOE_BLOCK, side='right'), N_EXPERTS - 1)
    h_pad = jnp.concatenate([h, jnp.zeros((1, D), h.dtype)], axis=0)
    xs = h_pad[row_token].reshape(n_blocks, MOE_BLOCK, D)

    def expert_block(args):
        xb, e = args
        return (jax.nn.silu(xb @ w_g[e]) * (xb @ w_u[e])) @ w_d[e]

    y = lax.map(expert_block, (xs, block_expert)).reshape(P, D) * row_weight[:, None]
    return jnp.zeros((N + 1, D), h.dtype).at[row_token].add(y)[:N]


def kernel(x, c, ctx, c_ctx, w_mod, b_mod, norm1_g, norm2_g, w_in, na_qn_g, na_kn_g, na_rpb,
           da_qn_g, da_kn_g, da_lam_q1, da_lam_k1, da_lam_q2, da_lam_k2, da_subln_g, gdn_conv_w,
           gdn_a_log, gdn_dt_bias, gdn_onorm_g, w_branch, w_out, w_router, router_bias,
           moe_w_gate, moe_w_up, moe_w_down):
    B, S, D = x.shape
    L = ctx.shape[1]
    NL = B * S
    f32 = jnp.float32
    cvec = jnp.zeros((8, D), f32).at[:B].set(c).at[B].set(c_ctx)
    mods = mod_call(cvec, w_mod, b_mod)
    cos, sin = rope_tables(S, 256)
    bf16 = jnp.bfloat16
    X = jnp.concatenate([x.reshape(NL, D), ctx.reshape(B * L, D)], 0)
    for l in range(DEPTH):
        last = l == DEPTH - 1
        lam_init = 0.8 - 0.6 * math.exp(-0.3 * l)
        w_main = jnp.concatenate([w_in[l][:, :2560], w_in[l][:, 2576:]], 1).astype(jnp.bfloat16)
        w_ab = jnp.pad(w_in[l][:, 2560:2576], ((0, 0), (0, 112))).astype(jnp.bfloat16)
        P_na, P_da, P_gdn, P_z, P_f, P_gate, P_ab = inproj_call(X, mods[l], norm1_g[l], w_main, w_ab,
                                                                n_batch=B, seq=S)
        NAQ, NAK, DAQ, DAK, DAVX = prep_call(P_na, P_da, cos, sin, na_qn_g[l], na_kn_g[l], da_qn_g[l], da_kn_g[l],
                                             n_batch=B, seq=S)
        bias = na_bias_tables(na_rpb[l], S // GRID_W)
        lam = (jnp.exp(jnp.sum(da_lam_q1[l] * da_lam_k1[l])) - jnp.exp(jnp.sum(da_lam_q2[l] * da_lam_k2[l]))
               + lam_init).astype(f32)
        Ya = na_call(NAQ, NAK, P_na, bias, n_batch=B, seq=S, ctx_len=L)
        Yd = da_call(lam, DAQ, DAK, DAVX, da_subln_g[l], n_batch=B, seq=S, ctx_len=L, lam_init=lam_init)
        lat = lambda a: a[:NL].astype(f32).reshape(B, S, -1)
        cx = lambda a: a[NL:].astype(f32).reshape(B, L, -1)
        yg, yg_c = gdn_mixer(lat(P_gdn), lat(P_ab)[..., :16], lat(P_z), cx(P_gdn), cx(P_ab)[..., :16], cx(P_z),
                             gdn_conv_w[l], gdn_a_log[l], gdn_dt_bias[l], gdn_onorm_g[l], not last)
        Yg = yg.reshape(NL, -1).astype(bf16)
        Yf = fourier_mix(lat(P_f)).reshape(NL, -1).astype(bf16)
        n_tok = NL
        if not last:
            Yac, Ydc = ctx_attn_call(lam, NAQ, NAK, P_na, DAQ, DAK, DAVX, da_subln_g[l],
                                     n_batch=B, seq=S, ctx_len=L, lam_init=lam_init)
            Ya = jnp.concatenate([Ya, Yac], 0)
            Yd = jnp.concatenate([Yd, Ydc], 0)
            Yg = jnp.concatenate([Yg, yg_c.reshape(B * L, -1).astype(bf16)], 0)
            Yf = jnp.concatenate([Yf, fourier_mix(cx(P_f)).reshape(B * L, -1).astype(bf16)], 0)
            n_tok = NL + B * L
        Xmid, H2, RW = merge_call(X, Ya, Yd, Yg, Yf, P_gate, mods[l], norm2_g[l], w_branch[l].astype(bf16),
                                  w_out[l].astype(bf16), w_router, router_bias, n_tok=n_tok, n_batch=B, seq=S)
        X = moe_call(Xmid, H2, RW, mods[l], moe_w_gate[l].astype(bf16), moe_w_up[l].astype(bf16),
                     moe_w_down[l].astype(bf16), n_tok=n_tok, n_batch=B, seq=S)
    return X[:NL].reshape(B, S, D)
```

```python
import functools
import math
import jax
import jax.numpy as jnp
from jax import lax
import numpy as np
from jax.experimental import pallas as pl
from jax.experimental.pallas import tpu as pltpu

D_MODEL = 1024
DEPTH = 2
GRID_W = 64
N_BRANCH = 4
BRANCH_W = D_MODEL // N_BRANCH
GDN_HEADS = 4
GDN_DIM = BRANCH_W // GDN_HEADS
GDN_CHUNK = 64
FNET_GROUPS = 4
FNET_GROUP_W = BRANCH_W // FNET_GROUPS
N_EXPERTS = 16
N_GROUPS = 4
RMS_EPS = 1e-6
NEG_INF = -1e30
ROPE_BASE = 10000.0
VMEM_LIMIT = 56 * 1024 * 1024
HI = lax.Precision.HIGHEST
NT_DIMS = (((1,), (1,)), ((), ()))
LOG2E = 1.4426950408889634


def _mod_kernel(c_ref, w_ref, b_ref, o_ref):
    c = c_ref[...]
    a = c * jax.nn.sigmoid(c)
    o_ref[0] = jnp.dot(a, w_ref[0], preferred_element_type=jnp.float32, precision=HI) + b_ref[0]


def mod_call(cvec, w_mod, b_mod, tn=512):
    depth, D, N = w_mod.shape
    return pl.pallas_call(
        _mod_kernel,
        grid=(depth, N // tn),
        in_specs=[pl.BlockSpec((8, D), lambda l, j: (0, 0)),
                  pl.BlockSpec((1, D, tn), lambda l, j: (l, 0, j)),
                  pl.BlockSpec((1, 1, tn), lambda l, j: (l, 0, j))],
        out_specs=pl.BlockSpec((1, 8, tn), lambda l, j: (l, 0, j)),
        out_shape=jax.ShapeDtypeStruct((depth, 8, N), jnp.float32),
        name="mod",
    )(cvec, w_mod, b_mod.reshape(depth, 1, N))


def _inproj_kernel(x_ref, m_ref, g_ref, w_ref, wab_ref, na_ref, da_ref, gdn_ref, z_ref, f_ref, gate_ref, ab_ref,
                   *, tiles_per_batch, n_lat_tiles, n_batch):
    i = pl.program_id(0)
    r = jnp.where(i < n_lat_tiles, i // tiles_per_batch, n_batch)
    x = x_ref[...]
    y = x * lax.rsqrt(jnp.mean(x * x, axis=-1, keepdims=True) + RMS_EPS) * g_ref[...]
    sh = m_ref[pl.ds(r, 1), 0:D_MODEL]
    sc = m_ref[pl.ds(r, 1), D_MODEL:2 * D_MODEL]
    h = (y * (1.0 + sc) + sh).astype(jnp.bfloat16)

    def seg(o_ref, a, b):
        o_ref[...] = jnp.dot(h, w_ref[:, a:b], preferred_element_type=jnp.float32).astype(o_ref.dtype)

    seg(na_ref, 0, 768)
    seg(da_ref, 768, 1536)
    seg(gdn_ref, 1536, 2304)
    seg(z_ref, 2304, 2560)
    seg(f_ref, 2560, 2816)
    for k in range(4):
        gate_ref[:, k * 1024:(k + 1) * 1024] = jnp.dot(
            h, w_ref[:, 2816 + k * 1024:2816 + (k + 1) * 1024],
            preferred_element_type=jnp.float32).astype(gate_ref.dtype)
    ab_ref[...] = jnp.dot(h, wab_ref[...], preferred_element_type=jnp.float32)


def inproj_call(X, mod_l, g, w_main, w_ab, *, n_batch, seq, tm=256):
    NT, D = X.shape
    n_lat_tiles = n_batch * seq // tm
    kern = functools.partial(_inproj_kernel, tiles_per_batch=seq // tm, n_lat_tiles=n_lat_tiles, n_batch=n_batch)
    widths = [(768, jnp.bfloat16), (768, jnp.bfloat16), (768, jnp.float32), (256, jnp.bfloat16),
              (256, jnp.bfloat16), (4096, jnp.bfloat16), (128, jnp.float32)]
    return pl.pallas_call(
        kern,
        grid=(NT // tm,),
        in_specs=[pl.BlockSpec((tm, D), lambda i: (i, 0)),
                  pl.BlockSpec(mod_l.shape, lambda i: (0, 0)),
                  pl.BlockSpec((1, D), lambda i: (0, 0)),
                  pl.BlockSpec(w_main.shape, lambda i: (0, 0)),
                  pl.BlockSpec(w_ab.shape, lambda i: (0, 0))],
        out_specs=[pl.BlockSpec((tm, w), lambda i: (i, 0)) for w, _ in widths],
        out_shape=[jax.ShapeDtypeStruct((NT, w), dt) for w, dt in widths],
        compiler_params=pltpu.CompilerParams(vmem_limit_bytes=VMEM_LIMIT),
        name="inproj",
    )(X, mod_l, g.reshape(1, D), w_main, w_ab)


def _group_ones(width, group):
    i = np.arange(width)
    return jnp.asarray((i[:, None] // group == i[None, :] // group).astype(np.float32))


def _group_rms(x, gmat, group, gain):
    ss = jnp.dot(x * x, gmat, preferred_element_type=jnp.float32, precision=HI)
    return x * lax.rsqrt(ss * (1.0 / group) + RMS_EPS) * gain


def rope_tables(seq, tm):
    nf = 8
    t = jnp.arange(seq)
    rows = (t // GRID_W).astype(jnp.float32)
    cols = (t % GRID_W).astype(jnp.float32)
    freqs = ROPE_BASE ** (-jnp.arange(nf, dtype=jnp.float32) / nf)
    d = np.arange(32)
    f_idx = d % 8
    use_col = (d // 16) == 1
    ang = jnp.where(use_col[None, :], cols[:, None], rows[:, None]) * freqs[f_idx][None, :]
    sign = np.where((d % 16) < 8, -1.0, 1.0).astype(np.float32)
    cos = jnp.tile(jnp.cos(ang), (1, 8))
    sin = jnp.tile(jnp.sin(ang) * sign[None, :], (1, 8))
    cos = jnp.concatenate([cos, jnp.ones((tm, 256), jnp.float32)], 0)
    sin = jnp.concatenate([sin, jnp.zeros((tm, 256), jnp.float32)], 0)
    return cos, sin


def _prep_kernel(na_ref, da_ref, cos_ref, sin_ref, g64_ref, g32_ref, gains_ref, naq_ref, nak_ref, daq_ref, dak_ref,
                 davx_ref):
    g64 = g64_ref[...]
    g32 = g32_ref[...]
    na = na_ref[...].astype(jnp.float32)
    naq_ref[...] = (_group_rms(na[:, 0:256], g64, 64, gains_ref[0:1, :]) * (64 ** -0.5)).astype(naq_ref.dtype)
    nak_ref[...] = _group_rms(na[:, 256:512], g64, 64, gains_ref[1:2, :]).astype(nak_ref.dtype)
    da = da_ref[...].astype(jnp.float32)
    cos = cos_ref[...]
    sin = sin_ref[...]
    lane = lax.broadcasted_iota(jnp.int32, (1, 256), 1)
    first = (lane % 16) < 8

    def rope(x):
        swapped = jnp.where(first, pltpu.roll(x, 256 - 8, 1), pltpu.roll(x, 8, 1))
        return x * cos + swapped * sin

    q = rope(_group_rms(da[:, 0:256], g32, 32, gains_ref[2:3, :]))
    k = rope(_group_rms(da[:, 256:512], g32, 32, gains_ref[3:4, :]))
    daq_ref[...] = (q * (32 ** -0.5 * LOG2E)).astype(daq_ref.dtype)
    dak_ref[...] = k.astype(dak_ref.dtype)
    v = da_ref[:, 512:768]
    one_col = (lax.broadcasted_iota(jnp.int32, (v.shape[0], 64), 1) == 0).astype(v.dtype)
    davx_ref[...] = jnp.concatenate(
        [piece for h in range(4) for piece in (v[:, h * 64:(h + 1) * 64], one_col)], -1)


def prep_call(P_na, P_da, cos, sin, na_qg, na_kg, da_qg, da_kg, *, n_batch, seq, tm=256):
    NT = P_na.shape[0]
    tpb = seq // tm
    n_lat = n_batch * tpb
    gains = jnp.stack([jnp.tile(na_qg, 4), jnp.tile(na_kg, 4), jnp.tile(da_qg, 8), jnp.tile(da_kg, 8)]
                      + [jnp.zeros((256,), jnp.float32)] * 4, 0)
    tab_map = lambda i: (jnp.where(i < n_lat, i % tpb, tpb), 0)
    full = lambda a: pl.BlockSpec(a.shape, lambda i: (0,) * a.ndim)
    g64, g32 = _group_ones(256, 64), _group_ones(256, 32)
    return pl.pallas_call(
        _prep_kernel,
        grid=(NT // tm,),
        in_specs=[pl.BlockSpec((tm, 768), lambda i: (i, 0)), pl.BlockSpec((tm, 768), lambda i: (i, 0)),
                  pl.BlockSpec((tm, 256), tab_map), pl.BlockSpec((tm, 256), tab_map),
                  full(g64), full(g32), full(gains)],
        out_specs=[pl.BlockSpec((tm, 256), lambda i: (i, 0))] * 4 + [pl.BlockSpec((tm, 512), lambda i: (i, 0))],
        out_shape=[jax.ShapeDtypeStruct((NT, 256), jnp.bfloat16)] * 4 + [jax.ShapeDtypeStruct((NT, 512), jnp.bfloat16)],
        name="attn_prep",
    )(P_na, P_da, cos, sin, g64, g32, gains)


NA_RB = 4
NA_UR = 11


def na_bias_tables(rpb, rows):
    reps = [(0, 0), (4, 0), (rows - NA_RB, rows - NA_UR)]
    qc = np.arange(GRID_W)[:, None]
    kc = np.arange(GRID_W)[None, :]
    ws = np.clip(qc - 8, 0, GRID_W - 16)
    vcol = (kc >= ws) & (kc < ws + 16)
    rel_c = np.clip(kc - qc + 15, 0, 30)
    sel_c = (rel_c[..., None] == np.arange(31)).astype(np.float32)
    sel_r, vrow = [], []
    for r0, u0 in reps:
        r = r0 + np.arange(NA_RB)[:, None]
        krow = u0 + np.arange(NA_UR)[None, :]
        start = np.clip(r - 4, 0, rows - 8)
        vrow.append((krow >= start) & (krow < start + 8))
        sel_r.append((np.clip(krow - r + 7, 0, 14)[..., None] == np.arange(15)).astype(np.float32))
    sel_r, vrow = np.stack(sel_r), np.stack(vrow)
    b = jnp.einsum('taui,hij,qkj->thaquk', jnp.asarray(sel_r), rpb.astype(jnp.float32), jnp.asarray(sel_c),
                   precision=HI)
    valid = vrow[:, None, :, None, :, None] & vcol[None, None, None, :, None, :]
    b = jnp.where(valid, b, NEG_INF)
    return b.reshape(3, rpb.shape[0], NA_RB * GRID_W, NA_UR * GRID_W)


def _na_kernel(q_ref, k_ref, v_ref, kc_ref, vc_ref, bias_ref, o_ref, *, rows):
    i = pl.program_id(1)
    u0 = jnp.clip(i * NA_RB - 4, 0, rows - NA_UR)
    off = pl.multiple_of(u0 * GRID_W, GRID_W)
    nkw = NA_UR * GRID_W
    kwin = k_ref[pl.ds(off, nkw), :]
    vwin = v_ref[pl.ds(off, nkw), :]
    q = q_ref[...]
    kc = kc_ref[...]
    vc = vc_ref[...]
    outs = []
    for h in range(4):
        sl = slice(h * 64, (h + 1) * 64)
        qh = q[:, sl]
        s_loc = lax.dot_general(qh, kwin[:, sl], NT_DIMS, preferred_element_type=jnp.float32) + bias_ref[0, h]
        s_ctx = lax.dot_general(qh, kc[:, sl], NT_DIMS, preferred_element_type=jnp.float32)
        m = jnp.maximum(jnp.max(s_loc, -1, keepdims=True), jnp.max(s_ctx, -1, keepdims=True))
        p_loc = jnp.exp(s_loc - m)
        p_ctx = jnp.exp(s_ctx - m)
        l = jnp.sum(p_loc, -1, keepdims=True) + jnp.sum(p_ctx, -1, keepdims=True)
        o = (jnp.dot(p_loc.astype(jnp.bfloat16), vwin[:, sl], preferred_element_type=jnp.float32)
             + jnp.dot(p_ctx.astype(jnp.bfloat16), vc[:, sl], preferred_element_type=jnp.float32))
        outs.append(o / l)
    o_ref[...] = jnp.concatenate(outs, -1).astype(o_ref.dtype)


def na_call(NAQ, NAK, P_na, bias, *, n_batch, seq, ctx_len):
    rows = seq // GRID_W
    nblk = rows // NA_RB
    tq = NA_RB * GRID_W
    cb = n_batch * seq // ctx_len
    kern = functools.partial(_na_kernel, rows=rows)
    tsel = lambda b, i: (jnp.where(i == 0, 0, jnp.where(i == nblk - 1, 2, 1)), 0, 0, 0)
    return pl.pallas_call(
        kern,
        grid=(n_batch, nblk),
        in_specs=[pl.BlockSpec((tq, 256), lambda b, i: (b * nblk + i, 0)),
                  pl.BlockSpec((seq, 256), lambda b, i: (b, 0)),
                  pl.BlockSpec((seq, 256), lambda b, i: (b, 2)),
                  pl.BlockSpec((ctx_len, 256), lambda b, i: (cb + b, 0)),
                  pl.BlockSpec((ctx_len, 256), lambda b, i: (cb + b, 2)),
                  pl.BlockSpec((1,) + bias.shape[1:], tsel)],
        out_specs=pl.BlockSpec((tq, 256), lambda b, i: (b * nblk + i, 0)),
        out_shape=jax.ShapeDtypeStruct((n_batch * seq, 256), jnp.bfloat16),
        compiler_params=pltpu.CompilerParams(vmem_limit_bytes=VMEM_LIMIT),
        name="na_attn",
    )(NAQ, NAK, P_na, NAK, P_na, bias)


def _stack_masked_q(q):
    lane = lax.broadcasted_iota(jnp.int32, (1, 256), 1)
    return jnp.concatenate([jnp.where((lane // 32) == hm, q, jnp.zeros_like(q)) for hm in range(8)], 0)


def _da_finish(acc, lam, gain, g64, post_scale):
    tq = acc.shape[0] // 8
    outs = []
    for h in range(4):
        a0 = acc[(2 * h) * tq:(2 * h + 1) * tq]
        a1 = acc[(2 * h + 1) * tq:(2 * h + 2) * tq]
        outs.append(a0[:, 0:64] / a0[:, 64:65] - lam * (a1[:, 0:64] / a1[:, 64:65]))
    o = jnp.concatenate(outs, -1)
    return _group_rms(o, g64, 64, gain) * post_scale


def _da_tile(qs, k, vx, m, acc, tq):
    tk = k.shape[0]
    s = lax.dot_general(qs, k, NT_DIMS, preferred_element_type=jnp.float32)
    m_new = jnp.maximum(m, jnp.max(s, -1, keepdims=True))
    alpha = jnp.exp2(m - m_new)
    p = jnp.exp2(s - pltpu.repeat(m_new, tk // 128, axis=1)).astype(jnp.bfloat16)
    pv = jnp.concatenate(
        [jnp.dot(p[2 * h * tq:(2 * h + 2) * tq], vx[:, h * 128:(h + 1) * 128], preferred_element_type=jnp.float32)
         for h in range(4)], 0)
    return m_new, alpha * acc + pv


def _da_kernel(lam_ref, q_ref, k_ref, vx_ref, kc_ref, vxc_ref, gain_ref, g64_ref, o_ref, m_ref, acc_ref,
               *, tk, n_kv, post_scale):
    tq = q_ref.shape[0]
    qs = _stack_masked_q(q_ref[...])
    m_ref[...] = jnp.full(m_ref.shape, -jnp.inf, jnp.float32)
    acc_ref[...] = jnp.zeros(acc_ref.shape, jnp.float32)

    def body(j, carry):
        for u in range(2):
            off = pl.multiple_of((2 * j + u) * tk, tk)
            m, acc = _da_tile(qs, k_ref[pl.ds(off, tk), :], vx_ref[pl.ds(off, tk), :], m_ref[...], acc_ref[...], tq)
            m_ref[...] = m
            acc_ref[...] = acc
        return carry

    lax.fori_loop(0, n_kv // 2, body, 0)
    m, acc = _da_tile(qs, kc_ref[...], vxc_ref[...], m_ref[...], acc_ref[...], tq)
    o_ref[...] = _da_finish(acc, lam_ref[0], gain_ref[...], g64_ref[...], post_scale).astype(o_ref.dtype)


def da_call(lam, DAQ, DAK, DAVX, subln_g, *, n_batch, seq, ctx_len, lam_init, tq=128, tk=512):
    nq = seq // tq
    cb = n_batch * seq // ctx_len
    g64 = _group_ones(256, 64)
    gain = jnp.tile(subln_g, 4).reshape(1, 256)
    n_kv = seq // tk
    assert n_kv % 2 == 0
    kern = functools.partial(_da_kernel, tk=tk, n_kv=n_kv, post_scale=1.0 - lam_init)
    return pl.pallas_call(
        kern,
        grid=(n_batch, nq),
        in_specs=[pl.BlockSpec(memory_space=pltpu.SMEM),
                  pl.BlockSpec((tq, 256), lambda b, i: (b * nq + i, 0)),
                  pl.BlockSpec((seq, 256), lambda b, i: (b, 0)),
                  pl.BlockSpec((seq, 512), lambda b, i: (b, 0)),
                  pl.BlockSpec((ctx_len, 256), lambda b, i: (cb + b, 0)),
                  pl.BlockSpec((ctx_len, 512), lambda b, i: (cb + b, 0)),
                  pl.BlockSpec((1, 256), lambda b, i: (0, 0)),
                  pl.BlockSpec((256, 256), lambda b, i: (0, 0))],
        out_specs=pl.BlockSpec((tq, 256), lambda b, i: (b * nq + i, 0)),
        out_shape=jax.ShapeDtypeStruct((n_batch * seq, 256), jnp.bfloat16),
        scratch_shapes=[pltpu.VMEM((8 * tq, 128), jnp.float32), pltpu.VMEM((8 * tq, 128), jnp.float32)],
        compiler_params=pltpu.CompilerParams(vmem_limit_bytes=VMEM_LIMIT),
        name="da_attn",
    )(lam.reshape(1), DAQ, DAK, DAVX, DAK, DAVX, gain, g64)


def _ctx_kernel(lam_ref, naq_ref, nak_ref, nav_ref, daq_ref, dak_ref, dav_ref, gain_ref, g64_ref, ya_ref, yd_ref,
                *, post_scale):
    q = naq_ref[...]
    k = nak_ref[...]
    v = nav_ref[...]
    outs = []
    for h in range(4):
        sl = slice(h * 64, (h + 1) * 64)
        s = lax.dot_general(q[:, sl], k[:, sl], NT_DIMS, preferred_element_type=jnp.float32)
        p = jnp.exp(s - jnp.max(s, -1, keepdims=True))
        o = jnp.dot(p.astype(jnp.bfloat16), v[:, sl], preferred_element_type=jnp.float32)
        outs.append(o / jnp.sum(p, -1, keepdims=True))
    ya_ref[...] = jnp.concatenate(outs, -1).astype(ya_ref.dtype)
    tq = daq_ref.shape[0]
    qs = _stack_masked_q(daq_ref[...])
    m0 = jnp.full((8 * tq, 128), -jnp.inf, jnp.float32)
    m, acc = _da_tile(qs, dak_ref[...], dav_ref[...], m0, jnp.zeros((8 * tq, 128), jnp.float32), tq)
    yd_ref[...] = _da_finish(acc, lam_ref[0], gain_ref[...], g64_ref[...], post_scale).astype(yd_ref.dtype)


def ctx_attn_call(lam, NAQ, NAK, P_na, DAQ, DAK, DAVX, subln_g, *, n_batch, seq, ctx_len, lam_init):
    cb = n_batch * seq // ctx_len
    g64 = _group_ones(256, 64)
    gain = jnp.tile(subln_g, 4).reshape(1, 256)
    blk = lambda col: pl.BlockSpec((ctx_len, 256), lambda b: (cb + b, col))
    return pl.pallas_call(
        functools.partial(_ctx_kernel, post_scale=1.0 - lam_init),
        grid=(n_batch,),
        in_specs=[pl.BlockSpec(memory_space=pltpu.SMEM), blk(0), blk(0), blk(2), blk(0), blk(0),
                  pl.BlockSpec((ctx_len, 512), lambda b: (cb + b, 0)),
                  pl.BlockSpec((1, 256), lambda b: (0, 0)), pl.BlockSpec((256, 256), lambda b: (0, 0))],
        out_specs=[pl.BlockSpec((ctx_len, 256), lambda b: (b, 0))] * 2,
        out_shape=[jax.ShapeDtypeStruct((n_batch * ctx_len, 256), jnp.bfloat16)] * 2,
        name="ctx_attn",
    )(lam.reshape(1), NAQ, NAK, P_na, DAQ, DAK, DAVX, gain, g64)


def _split_bf16(a):
    hi = a.astype(jnp.bfloat16)
    return hi, (a - hi.astype(jnp.float32)).astype(jnp.bfloat16)


def _route(logits_t, bias_col):
    per_group = N_EXPERTS // N_GROUPS
    scores = jax.nn.sigmoid(logits_t)
    sel_all = scores + bias_col
    sel = [sel_all[e:e + 1, :] for e in range(N_EXPERTS)]
    top2 = []
    for e in range(N_EXPERTS):
        g0 = e // per_group * per_group
        rank = jnp.zeros_like(sel[e])
        for o in range(g0, g0 + per_group):
            if o != e:
                beats = (sel[o] > sel[e]) | (sel[o] == sel[e]) if o < e else (sel[o] > sel[e])
                rank = rank + beats.astype(jnp.float32)
        top2.append(rank < 1.5)
    gs = []
    for g in range(N_GROUPS):
        acc = jnp.zeros_like(sel[0])
        for e in range(g * per_group, (g + 1) * per_group):
            acc = acc + jnp.where(top2[e], sel[e], 0.0)
        gs.append(acc)
    rows = []
    for g in range(N_GROUPS):
        beaten = jnp.zeros_like(sel[0])
        for o in range(N_GROUPS):
            if o != g:
                b = (gs[o] >= gs[g]) if o < g else (gs[o] > gs[g])
                beaten = beaten + b.astype(jnp.float32)
        best = beaten < 0.5
        for e in range(g * per_group, (g + 1) * per_group):
            rows.append(jnp.where(best & top2[e], scores[e:e + 1, :], 0.0))
    w = jnp.concatenate(rows, 0)
    return w / jnp.sum(w, axis=0, keepdims=True)


def _merge_kernel(x_ref, ya_ref, yd_ref, of_ref, ob_ref, z_ref, yf_ref, gate_ref, m_ref, g2_ref, og_ref, g64_ref,
                  wb_ref, wo_ref, wr_ref, rb_ref, xo_ref, h2_ref, rw_ref, *, tiles_per_batch, n_lat_tiles, n_batch):
    i = pl.program_id(0)
    r = jnp.where(i < n_lat_tiles, i // tiles_per_batch, n_batch)
    z = z_ref[...].astype(jnp.float32)
    yg = (_group_rms(of_ref[...] + ob_ref[...], g64_ref[...], GDN_DIM, og_ref[...]) * (z * jax.nn.sigmoid(z)))
    branches = (ya_ref[...], yd_ref[...], yg.astype(jnp.bfloat16), yf_ref[...])
    mix = None
    for b, yb in enumerate(branches):
        proj = jnp.dot(yb, wb_ref[b], preferred_element_type=jnp.float32)
        term = jax.nn.sigmoid(gate_ref[:, b * D_MODEL:(b + 1) * D_MODEL].astype(jnp.float32)) * proj
        mix = term if mix is None else mix + term
    y = jnp.dot(mix.astype(jnp.bfloat16), wo_ref[...], preferred_element_type=jnp.float32)
    mrow = lambda k: m_ref[pl.ds(r, 1), k * D_MODEL:(k + 1) * D_MODEL]
    x = x_ref[...] + mrow(2) * y
    xo_ref[...] = x
    h2 = (x * lax.rsqrt(jnp.mean(x * x, axis=-1, keepdims=True) + RMS_EPS) * g2_ref[...]) * (1.0 + mrow(4)) + mrow(3)
    h2_ref[...] = h2.astype(h2_ref.dtype)
    h_hi, h_lo = _split_bf16(h2)
    w_hi, w_lo = _split_bf16(wr_ref[...])
    nt = lambda a, b: lax.dot_general(a, b, NT_DIMS, preferred_element_type=jnp.float32)
    logits_t = nt(w_hi, h_hi) + nt(w_hi, h_lo) + nt(w_lo, h_hi)
    w_t = _route(logits_t, rb_ref[...])
    pad = jnp.zeros((128 - N_EXPERTS, w_t.shape[1]), jnp.float32)
    rw_ref[...] = jnp.transpose(jnp.concatenate([w_t, pad], 0))


def merge_call(X, Ya, Yd, Of, Ob, P_z, Yf, P_gate, mod_l, g2n, onorm_g, w_branch, w_out, w_router, router_bias,
               *, n_tok, n_batch, seq, tm=256):
    D = X.shape[1]
    n_lat_tiles = n_batch * seq // tm
    kern = functools.partial(_merge_kernel, tiles_per_batch=seq // tm, n_lat_tiles=n_lat_tiles, n_batch=n_batch)
    row = lambda w: pl.BlockSpec((tm, w), lambda i: (i, 0))
    full = lambda a: pl.BlockSpec(a.shape, lambda i: (0,) * a.ndim)
    wr_t = jnp.transpose(w_router)
    rb = router_bias.reshape(N_EXPERTS, 1)
    g2 = g2n.reshape(1, D)
    og = jnp.tile(onorm_g, GDN_HEADS).reshape(1, BRANCH_W)
    g64 = _group_ones(BRANCH_W, GDN_DIM)
    return pl.pallas_call(
        kern,
        grid=(n_tok // tm,),
        in_specs=[row(D), row(256), row(256), row(256), row(256), row(256), row(256), row(4 * D), full(mod_l),
                  full(g2), full(og), full(g64), full(w_branch), full(w_out), full(wr_t), full(rb)],
        out_specs=[row(D), row(D), row(128)],
        out_shape=[jax.ShapeDtypeStruct((n_tok, D), jnp.float32), jax.ShapeDtypeStruct((n_tok, D), jnp.bfloat16),
                   jax.ShapeDtypeStruct((n_tok, 128), jnp.float32)],
        compiler_params=pltpu.CompilerParams(vmem_limit_bytes=VMEM_LIMIT),
        name="merge",
    )(X, Ya, Yd, Of, Ob, P_z, Yf, P_gate, mod_l, g2, og, g64, w_branch, w_out, wr_t, rb)


def _moe_kernel(x_ref, h_ref, rw_ref, m_ref, wg_ref, wu_ref, wd_ref, o_ref, acc_ref,
                *, tiles_per_batch, n_lat_tiles, n_batch):
    i = pl.program_id(0)
    e = pl.program_id(1)

    @pl.when(e == 0)
    def _():
        acc_ref[...] = jnp.zeros(acc_ref.shape, jnp.float32)

    h = h_ref[...]
    a = jnp.dot(h, wg_ref[0], preferred_element_type=jnp.float32)
    u = jnp.dot(h, wu_ref[0], preferred_element_type=jnp.float32)
    lane = lax.broadcasted_iota(jnp.int32, (1, 128), 1)
    w_col = jnp.sum(jnp.where(lane == e, rw_ref[...], 0.0), axis=-1, keepdims=True)
    act = (a * jax.nn.sigmoid(a) * u * w_col).astype(jnp.bfloat16)
    acc_ref[...] += jnp.dot(act, wd_ref[0], preferred_element_type=jnp.float32)

    @pl.when(e == N_EXPERTS - 1)
    def _():
        r = jnp.where(i < n_lat_tiles, i // tiles_per_batch, n_batch)
        g2 = m_ref[pl.ds(r, 1), 5 * D_MODEL:6 * D_MODEL]
        o_ref[...] = x_ref[...] + g2 * acc_ref[...]


def moe_call(Xmid, H2, RW, mod_l, w_gate, w_up, w_down, *, n_tok, n_batch, seq, tm=512):
    D = Xmid.shape[1]
    F = w_gate.shape[-1]
    n_lat_tiles = n_batch * seq // tm
    kern = functools.partial(_moe_kernel, tiles_per_batch=seq // tm, n_lat_tiles=n_lat_tiles, n_batch=n_batch)
    return pl.pallas_call(
        kern,
        grid=(n_tok // tm, N_EXPERTS),
        in_specs=[pl.BlockSpec((tm, D), lambda i, e: (i, 0)),
                  pl.BlockSpec((tm, D), lambda i, e: (i, 0)),
                  pl.BlockSpec((tm, 128), lambda i, e: (i, 0)),
                  pl.BlockSpec(mod_l.shape, lambda i, e: (0, 0)),
                  pl.BlockSpec((1, D, F), lambda i, e: (e, 0, 0)),
                  pl.BlockSpec((1, D, F), lambda i, e: (e, 0, 0)),
                  pl.BlockSpec((1, F, D), lambda i, e: (e, 0, 0))],
        out_specs=pl.BlockSpec((tm, D), lambda i, e: (i, 0)),
        out_shape=jax.ShapeDtypeStruct((n_tok, D), jnp.float32),
        scratch_shapes=[pltpu.VMEM((tm, D), jnp.float32)],
        compiler_params=pltpu.CompilerParams(vmem_limit_bytes=VMEM_LIMIT),
        name="moe",
    )(Xmid, H2, RW, mod_l, w_gate, w_up, w_down)


GDN_TILE = 256
GDN_CPT = GDN_TILE // GDN_CHUNK


def _mm(a, b, passes=1, dims=None):
    if dims is None:
        dot = lambda x, y: jnp.dot(x, y, preferred_element_type=jnp.float32)
    else:
        dot = lambda x, y: lax.dot_general(x, y, dims, preferred_element_type=jnp.float32)
    if passes == 1:
        return dot(a.astype(jnp.bfloat16), b.astype(jnp.bfloat16))
    a_hi, a_lo = _split_bf16(a)
    b_hi, b_lo = _split_bf16(b)
    return dot(a_hi, b_hi) + dot(a_hi, b_lo) + dot(a_lo, b_hi)


def _stack_heads(x):
    lane = lax.broadcasted_iota(jnp.int32, (1, 256), 1)
    return jnp.concatenate([jnp.where((lane // GDN_DIM) == h, x, 0.0) for h in range(GDN_HEADS)], 0)


def _slab(x):
    return x[0:64] + x[64:128] + x[128:192] + x[192:256]


def _unit_tri_inverse(a):
    eye = (lax.broadcasted_iota(jnp.int32, a.shape, 0) == lax.broadcasted_iota(jnp.int32, a.shape, 1)).astype(a.dtype)
    p = eye - a
    pw = a
    for _ in range(5):
        pw = _mm(pw, pw, 3)
        p = p + _mm(p, pw, 3)
    return p


def _gdn_chunk_kernel(x_ref, xp_ref, xn_ref, ab_ref, cw_ref, par_ref, g64_ref, qe_ref, ou_ref, mm_ref, nn_ref,
                      *, tiles_per_batch, n_lat_tiles):
    i = pl.program_id(0)
    is_lat = i < n_lat_tiles
    first = jnp.where(is_lat, (i % tiles_per_batch) == 0, True)
    last = jnp.where(is_lat, (i % tiles_per_batch) == tiles_per_batch - 1, True)
    xp = jnp.concatenate([jnp.where(first, 0.0, xp_ref[...]), x_ref[...], jnp.where(last, 0.0, xn_ref[...])], 0)
    y = xp[6:6 + GDN_TILE] * cw_ref[0:1, :]
    for t in range(1, 5):
        y = y + xp[6 + t:6 + t + GDN_TILE] * cw_ref[t:t + 1, :]
    y = y * jax.nn.sigmoid(y)
    g64 = g64_ref[...]
    q = y[:, 0:256]
    k = y[:, 256:512]
    v = y[:, 512:768]
    q = q * lax.rsqrt(jnp.dot(q * q, g64, preferred_element_type=jnp.float32, precision=HI) + RMS_EPS) * (GDN_DIM ** -0.5)
    k = k * lax.rsqrt(jnp.dot(k * k, g64, preferred_element_type=jnp.float32, precision=HI) + RMS_EPS)
    ab = ab_ref[...]
    lane128 = lax.broadcasted_iota(jnp.int32, (1, 128), 1)
    g_all = jnp.where(lane128 < 8, par_ref[0:1, :] * jax.nn.softplus(ab + par_ref[1:2, :]), 0.0)
    beta_all = jax.nn.sigmoid(ab)
    r64 = lax.broadcasted_iota(jnp.int32, (64, 64), 0)
    c64 = lax.broadcasted_iota(jnp.int32, (64, 64), 1)
    low = (r64 >= c64).astype(jnp.float32)
    upp = (r64 <= c64).astype(jnp.float32)
    rr = lax.broadcasted_iota(jnp.int32, (256, 256), 0)
    cc = lax.broadcasted_iota(jnp.int32, (256, 256), 1)
    same = (rr // 64) == (cc // 64)
    eye = (rr == cc).astype(jnp.float32)
    for c in range(GDN_CPT):
        sl = slice(c * GDN_CHUNK, (c + 1) * GDN_CHUNK)
        g_c = g_all[sl]
        gsum = jnp.where(lane128 < 4, jnp.dot(low, g_c, preferred_element_type=jnp.float32, precision=HI),
                         jnp.dot(upp, g_c, preferred_element_type=jnp.float32, precision=HI))
        gsum_t = jnp.transpose(jnp.concatenate([gsum, jnp.zeros_like(gsum)], 0))[:, 0:64]
        tot = jnp.sum(g_c, axis=0, keepdims=True)
        beta_c = beta_all[sl]
        q_sm = _stack_heads(q[sl])
        k_sm = _stack_heads(k[sl])
        v_sm = _stack_heads(v[sl])
        kk = _mm(k_sm, k_sm, 1, NT_DIMS)
        qk = _mm(q_sm, k_sm, 1, NT_DIMS)
        for d in range(2):
            col = lambda h: 4 * d + h
            cmat = jnp.concatenate([jnp.broadcast_to(gsum[:, col(h):col(h) + 1], (64, 256)) for h in range(4)], 0)
            rrow = jnp.concatenate([gsum_t[col(h):col(h) + 1, :] for h in range(4)], 1)
            bmat = jnp.concatenate([jnp.broadcast_to(beta_c[:, 8 + col(h):9 + col(h)], (64, 256)) for h in range(4)], 0)
            tmat = jnp.concatenate([jnp.broadcast_to(tot[:, col(h):col(h) + 1], (64, 256)) for h in range(4)], 0)
            tri = (rr % 64 >= cc % 64) if d == 0 else (rr % 64 <= cc % 64)
            incl = same & tri
            strict = incl & (rr != cc)
            dec = jnp.where(incl, jnp.exp(jnp.where(incl, cmat - rrow, 0.0)), 0.0)
            eg = jnp.exp(cmat)
            a = jnp.where(strict, kk * bmat * dec, 0.0)
            t_inv = _unit_tri_inverse(a)
            kb = k_sm * bmat
            u_sm = _mm(t_inv, v_sm * bmat, 1)
            w_sm = _mm(t_inv, kb * eg, 1)
            qkm = dec * qk
            qe_sm = q_sm * eg - _mm(qkm, w_sm, 1)
            ou_sm = _mm(qkm, u_sm, 1)
            kd_t = jnp.transpose(k_sm * jnp.exp(tmat - cmat))
            m_bd = eye * jnp.exp(tmat) - _mm(kd_t, w_sm, 1)
            n_bd = _mm(kd_t, u_sm, 1)
            qe_ref[d, sl, :] = _slab(qe_sm)
            ou_ref[d, sl, :] = _slab(ou_sm)
            mm_ref[d, sl, :] = _slab(m_bd)
            nn_ref[d, sl, :] = _slab(n_bd)


def gdn_chunk_call(P_gdn, P_ab, conv_w, a_log, dt_bias, *, n_batch, seq):
    NT = P_gdn.shape[0]
    tpb = seq // GDN_TILE
    n_lat = n_batch * tpb
    n_tiles = NT // GDN_TILE
    hb = GDN_TILE // 8
    cw = jnp.zeros((8, 768), jnp.float32).at[:5].set(conv_w)
    par = jnp.zeros((8, 128), jnp.float32)
    par = par.at[0, :8].set(-jnp.exp(a_log.reshape(8))).at[1, :8].set(dt_bias.reshape(8))
    g64 = _group_ones(256, 64)
    kern = functools.partial(_gdn_chunk_kernel, tiles_per_batch=tpb, n_lat_tiles=n_lat)
    full = lambda a: pl.BlockSpec(a.shape, lambda i: (0,) * a.ndim)
    out = pl.BlockSpec((2, GDN_TILE, 256), lambda i: (0, i, 0))
    return pl.pallas_call(
        kern,
        grid=(n_tiles,),
        in_specs=[pl.BlockSpec((GDN_TILE, 768), lambda i: (i, 0)),
                  pl.BlockSpec((8, 768), lambda i: (jnp.maximum(i * hb - 1, 0), 0)),
                  pl.BlockSpec((8, 768), lambda i: (jnp.minimum((i + 1) * hb, n_tiles * hb - 1), 0)),
                  pl.BlockSpec((GDN_TILE, 128), lambda i: (i, 0)),
                  full(cw), full(par), full(g64)],
        out_specs=[out] * 4,
        out_shape=[jax.ShapeDtypeStruct((2, NT, 256), jnp.float32)] * 4,
        compiler_params=pltpu.CompilerParams(vmem_limit_bytes=VMEM_LIMIT),
        name="gdn_chunk",
    )(P_gdn, P_gdn, P_gdn, P_ab, cw, par, g64)


def _gdn_scan_kernel(qef_ref, ouf_ref, mf_ref, nf_ref, qeb_ref, oub_ref, mb_ref, nb_ref, of_ref, ob_ref, s_ref):
    t = pl.program_id(1)

    @pl.when(t == 0)
    def _():
        s_ref[...] = jnp.zeros(s_ref.shape, jnp.float32)

    def step(d, c, qe_ref, ou_ref, m_ref, n_ref, o_ref):
        sl = slice(c * GDN_CHUNK, (c + 1) * GDN_CHUNK)
        s = s_ref[d]
        o_ref[sl, :] = _slab(_mm(_stack_heads(qe_ref[0, sl, :]), s, 3)) + ou_ref[0, sl, :]
        s_ref[d] = _mm(_stack_heads(m_ref[0, sl, :]), s, 3) + _stack_heads(n_ref[0, sl, :])

    for c in range(GDN_CPT):
        step(0, c, qef_ref, ouf_ref, mf_ref, nf_ref, of_ref)
        step(1, GDN_CPT - 1 - c, qeb_ref, oub_ref, mb_ref, nb_ref, ob_ref)


def gdn_scan_call(QE, OU, MM, NN, *, n_batch, seq):
    NT = QE.shape[1]
    tpb = seq // GDN_TILE
    cblk = n_batch * tpb
    fwd = lambda b, t: jnp.where(t == 0, cblk + b, b * tpb + t - 1)
    bwd = lambda b, t: jnp.where(t == 0, cblk + b, b * tpb + tpb - t)
    spec = lambda d, f: pl.BlockSpec((1, GDN_TILE, 256), lambda b, t: (d, f(b, t), 0))
    ospec = lambda f: pl.BlockSpec((GDN_TILE, 256), lambda b, t: (f(b, t), 0))
    return pl.pallas_call(
        _gdn_scan_kernel,
        grid=(n_batch, tpb + 1),
        in_specs=[spec(0, fwd)] * 4 + [spec(1, bwd)] * 4,
        out_specs=[ospec(fwd), ospec(bwd)],
        out_shape=[jax.ShapeDtypeStruct((NT, 256), jnp.float32)] * 2,
        scratch_shapes=[pltpu.VMEM((2, 256, 256), jnp.float32)],
        compiler_params=pltpu.CompilerParams(vmem_limit_bytes=VMEM_LIMIT),
        name="gdn_scan",
    )(QE, OU, MM, NN, QE, OU, MM, NN)


def _dft_cs(n):
    a = 2.0 * np.pi * np.outer(np.arange(n), np.arange(n)) / n
    return np.cos(a), np.sin(a)


def _channel_dft():
    c, s = _dft_cs(FNET_GROUP_W)
    eye = np.eye(FNET_GROUPS)
    return np.concatenate([np.kron(eye, c), np.kron(eye, s)], 1)


def _fnet_a_kernel(x_ref, cs_ref, m_ref, cphi_ref, sphi_ref, br_ref, bi_ref, *, n_rows):
    for j in range(x_ref.shape[1] // BRANCH_W):
        sl = slice(j * BRANCH_W, (j + 1) * BRANCH_W)
        u = jnp.dot(x_ref[:, sl], cs_ref[...], preferred_element_type=jnp.float32)
        st = jnp.concatenate([u[:, :BRANCH_W], u[:, BRANCH_W:]], 0).astype(jnp.bfloat16)
        a = jnp.dot(m_ref[...], st, preferred_element_type=jnp.float32)
        ar, ai = a[:n_rows], a[n_rows:]
        cp, sp = cphi_ref[:, sl], sphi_ref[:, sl]
        br_ref[j] = (ar * cp - ai * sp).astype(br_ref.dtype)
        bi_ref[j] = (ar * sp + ai * cp).astype(bi_ref.dtype)


def _fnet_c_kernel(br_ref, bi_ref, m_ref, o_ref):
    st = jnp.concatenate([br_ref[...], bi_ref[...]], 0)
    o_ref[...] = jnp.dot(m_ref[...], st, preferred_element_type=jnp.float32).astype(o_ref.dtype)


def fourier_latent_call(P_f, *, n_batch, seq):
    rows = seq // GRID_W
    bf16 = jnp.bfloat16
    c1, s1 = _dft_cs(rows)
    norm = 1.0 / math.sqrt(seq * FNET_GROUP_W)
    m_a = jnp.asarray(np.block([[c1, -s1], [s1, c1]]) * norm, bf16)
    c2, s2 = _dft_cs(GRID_W)
    m_c = jnp.asarray(np.concatenate([c2, -s2], 1), bf16)
    cs = jnp.asarray(_channel_dft(), bf16)
    phi = 2.0 * np.pi * np.outer(np.arange(rows), np.arange(GRID_W)) / seq
    cphi = jnp.repeat(jnp.asarray(np.cos(phi), jnp.float32), BRANCH_W, axis=1)
    sphi = jnp.repeat(jnp.asarray(np.sin(phi), jnp.float32), BRANCH_W, axis=1)
    xv = P_f[:n_batch * seq].reshape(n_batch * rows, GRID_W * BRANCH_W)
    t2b = 4
    nj = GRID_W // t2b
    full = lambda a: pl.BlockSpec(a.shape, lambda b, j: (0,) * a.ndim)
    br, bi = pl.pallas_call(
        functools.partial(_fnet_a_kernel, n_rows=rows),
        grid=(n_batch, nj),
        in_specs=[pl.BlockSpec((rows, t2b * BRANCH_W), lambda b, j: (b, j)), full(cs), full(m_a),
                  pl.BlockSpec((rows, t2b * BRANCH_W), lambda b, j: (0, j)),
                  pl.BlockSpec((rows, t2b * BRANCH_W), lambda b, j: (0, j))],
        out_specs=[pl.BlockSpec((t2b, rows, BRANCH_W), lambda b, j: (b * nj + j, 0, 0))] * 2,
        out_shape=[jax.ShapeDtypeStruct((n_batch * GRID_W, rows, BRANCH_W), bf16)] * 2,
        name="fnet_rows",
    )(xv, cs, m_a, cphi, sphi)
    width = rows * BRANCH_W
    tc = 2048
    y = pl.pallas_call(
        _fnet_c_kernel,
        grid=(n_batch, width // tc),
        in_specs=[pl.BlockSpec((GRID_W, tc), lambda b, j: (b, j)), pl.BlockSpec((GRID_W, tc), lambda b, j: (b, j)),
                  pl.BlockSpec(m_c.shape, lambda b, j: (0, 0))],
        out_specs=pl.BlockSpec((GRID_W, tc), lambda b, j: (b, j)),
        out_shape=jax.ShapeDtypeStruct((n_batch * GRID_W, width), bf16),
        name="fnet_cols",
    )(br.reshape(n_batch * GRID_W, width), bi.reshape(n_batch * GRID_W, width), m_c)
    return y.reshape(n_batch * seq, BRANCH_W)


def _fnet_ctx_kernel(x_ref, cs_ref, m_ref, o_ref):
    u = jnp.dot(x_ref[...], cs_ref[...], preferred_element_type=jnp.float32)
    st = jnp.concatenate([u[:, :BRANCH_W], u[:, BRANCH_W:]], 0).astype(jnp.bfloat16)
    o_ref[...] = jnp.dot(m_ref[...], st, preferred_element_type=jnp.float32).astype(o_ref.dtype)


def fourier_ctx_call(P_f, *, n_batch, seq, ctx_len):
    bf16 = jnp.bfloat16
    c, s = _dft_cs(ctx_len)
    m = jnp.asarray(np.concatenate([c, -s], 1) / math.sqrt(ctx_len * FNET_GROUP_W), bf16)
    cs = jnp.asarray(_channel_dft(), bf16)
    cb = n_batch * seq // ctx_len
    return pl.pallas_call(
        _fnet_ctx_kernel,
        grid=(n_batch,),
        in_specs=[pl.BlockSpec((ctx_len, BRANCH_W), lambda b: (cb + b, 0)),
                  pl.BlockSpec(cs.shape, lambda b: (0, 0)), pl.BlockSpec(m.shape, lambda b: (0, 0))],
        out_specs=pl.BlockSpec((ctx_len, BRANCH_W), lambda b: (b, 0)),
        out_shape=jax.ShapeDtypeStruct((n_batch * ctx_len, BRANCH_W), bf16),
        name="fnet_ctx",
    )(P_f, cs, m)


def kernel(x, c, ctx, c_ctx, w_mod, b_mod, norm1_g, norm2_g, w_in, na_qn_g, na_kn_g, na_rpb,
           da_qn_g, da_kn_g, da_lam_q1, da_lam_k1, da_lam_q2, da_lam_k2, da_subln_g, gdn_conv_w,
           gdn_a_log, gdn_dt_bias, gdn_onorm_g, w_branch, w_out, w_router, router_bias,
           moe_w_gate, moe_w_up, moe_w_down):
    B, S, D = x.shape
    L = ctx.shape[1]
    NL = B * S
    f32 = jnp.float32
    cvec = jnp.zeros((8, D), f32).at[:B].set(c).at[B].set(c_ctx)
    mods = mod_call(cvec, w_mod, b_mod)
    cos, sin = rope_tables(S, 256)
    bf16 = jnp.bfloat16
    X = jnp.concatenate([x.reshape(NL, D), ctx.reshape(B * L, D)], 0)
    for l in range(DEPTH):
        last = l == DEPTH - 1
        lam_init = 0.8 - 0.6 * math.exp(-0.3 * l)
        w_main = jnp.concatenate([w_in[l][:, :2560], w_in[l][:, 2576:]], 1).astype(jnp.bfloat16)
        w_ab = jnp.pad(w_in[l][:, 2560:2576], ((0, 0), (0, 112))).astype(jnp.bfloat16)
        P_na, P_da, P_gdn, P_z, P_f, P_gate, P_ab = inproj_call(X, mods[l], norm1_g[l], w_main, w_ab,
                                                                n_batch=B, seq=S)
        NAQ, NAK, DAQ, DAK, DAVX = prep_call(P_na, P_da, cos, sin, na_qn_g[l], na_kn_g[l], da_qn_g[l], da_kn_g[l],
                                             n_batch=B, seq=S)
        bias = na_bias_tables(na_rpb[l], S // GRID_W)
        lam = (jnp.exp(jnp.sum(da_lam_q1[l] * da_lam_k1[l])) - jnp.exp(jnp.sum(da_lam_q2[l] * da_lam_k2[l]))
               + lam_init).astype(f32)
        Ya = na_call(NAQ, NAK, P_na, bias, n_batch=B, seq=S, ctx_len=L)
        Yd = da_call(lam, DAQ, DAK, DAVX, da_subln_g[l], n_batch=B, seq=S, ctx_len=L, lam_init=lam_init)
        QE, OU, MM, NN = gdn_chunk_call(P_gdn, P_ab, gdn_conv_w[l], gdn_a_log[l], gdn_dt_bias[l], n_batch=B, seq=S)
        Of, Ob = gdn_scan_call(QE, OU, MM, NN, n_batch=B, seq=S)
        Yf = fourier_latent_call(P_f, n_batch=B, seq=S)
        n_tok = NL
        if not last:
            Yac, Ydc = ctx_attn_call(lam, NAQ, NAK, P_na, DAQ, DAK, DAVX, da_subln_g[l],
                                     n_batch=B, seq=S, ctx_len=L, lam_init=lam_init)
            Ya = jnp.concatenate([Ya, Yac], 0)
            Yd = jnp.concatenate([Yd, Ydc], 0)
            Yf = jnp.concatenate([Yf, fourier_ctx_call(P_f, n_batch=B, seq=S, ctx_len=L)], 0)
            n_tok = NL + B * L
        Xmid, H2, RW = merge_call(X, Ya, Yd, Of, Ob, P_z, Yf, P_gate, mods[l], norm2_g[l], gdn_onorm_g[l],
                                  w_branch[l].astype(bf16), w_out[l].astype(bf16), w_router, router_bias,
                                  n_tok=n_tok, n_batch=B, seq=S)
        X = moe_call(Xmid, H2, RW, mods[l], moe_w_gate[l].astype(bf16), moe_w_up[l].astype(bf16),
                     moe_w_down[l].astype(bf16), n_tok=n_tok, n_batch=B, seq=S)
    return X[:NL].reshape(B, S, D)
```

```python
import functools
import math
import jax
import jax.numpy as jnp
from jax import lax
import numpy as np
from jax.experimental import pallas as pl
from jax.experimental.pallas import tpu as pltpu

D_MODEL = 1024
DEPTH = 2
GRID_W = 64
N_BRANCH = 4
BRANCH_W = D_MODEL // N_BRANCH
GDN_HEADS = 4
GDN_DIM = BRANCH_W // GDN_HEADS
GDN_CHUNK = 64
FNET_GROUPS = 4
FNET_GROUP_W = BRANCH_W // FNET_GROUPS
N_EXPERTS = 16
N_GROUPS = 4
RMS_EPS = 1e-6
NEG_INF = -1e30
ROPE_BASE = 10000.0
VMEM_LIMIT = 56 * 1024 * 1024
HI = lax.Precision.HIGHEST
NT_DIMS = (((1,), (1,)), ((), ()))
LOG2E = 1.4426950408889634


def _mod_kernel(c_ref, w_ref, b_ref, o_ref):
    c = c_ref[...]
    a = c * jax.nn.sigmoid(c)
    o_ref[0] = jnp.dot(a, w_ref[0], preferred_element_type=jnp.float32, precision=HI) + b_ref[0]


def mod_call(cvec, w_mod, b_mod, tn=512):
    depth, D, N = w_mod.shape
    return pl.pallas_call(
        _mod_kernel,
        grid=(depth, N // tn),
        in_specs=[pl.BlockSpec((8, D), lambda l, j: (0, 0)),
                  pl.BlockSpec((1, D, tn), lambda l, j: (l, 0, j)),
                  pl.BlockSpec((1, 1, tn), lambda l, j: (l, 0, j))],
        out_specs=pl.BlockSpec((1, 8, tn), lambda l, j: (l, 0, j)),
        out_shape=jax.ShapeDtypeStruct((depth, 8, N), jnp.float32),
        name="mod",
    )(cvec, w_mod, b_mod.reshape(depth, 1, N))


def _inproj_kernel(x_ref, m_ref, g_ref, w_ref, wab_ref, na_ref, da_ref, gdn_ref, z_ref, f_ref, gate_ref, ab_ref,
                   *, tiles_per_batch, n_lat_tiles, n_batch):
    i = pl.program_id(0)
    r = jnp.where(i < n_lat_tiles, i // tiles_per_batch, n_batch)
    x = x_ref[...]
    y = x * lax.rsqrt(jnp.mean(x * x, axis=-1, keepdims=True) + RMS_EPS) * g_ref[...]
    sh = m_ref[pl.ds(r, 1), 0:D_MODEL]
    sc = m_ref[pl.ds(r, 1), D_MODEL:2 * D_MODEL]
    h = (y * (1.0 + sc) + sh).astype(jnp.bfloat16)

    def seg(o_ref, a, b):
        o_ref[...] = jnp.dot(h, w_ref[:, a:b], preferred_element_type=jnp.float32).astype(o_ref.dtype)

    seg(na_ref, 0, 768)
    seg(da_ref, 768, 1536)
    seg(gdn_ref, 1536, 2304)
    seg(z_ref, 2304, 2560)
    seg(f_ref, 2560, 2816)
    for k in range(4):
        gate_ref[:, k * 1024:(k + 1) * 1024] = jnp.dot(
            h, w_ref[:, 2816 + k * 1024:2816 + (k + 1) * 1024],
            preferred_element_type=jnp.float32).astype(gate_ref.dtype)
    ab_ref[...] = jnp.dot(h, wab_ref[...], preferred_element_type=jnp.float32)


def inproj_call(X, mod_l, g, w_main, w_ab, *, n_batch, seq, tm=256):
    NT, D = X.shape
    n_lat_tiles = n_batch * seq // tm
    kern = functools.partial(_inproj_kernel, tiles_per_batch=seq // tm, n_lat_tiles=n_lat_tiles, n_batch=n_batch)
    widths = [(768, jnp.bfloat16), (768, jnp.bfloat16), (768, jnp.float32), (256, jnp.bfloat16),
              (256, jnp.bfloat16), (4096, jnp.bfloat16), (128, jnp.float32)]
    return pl.pallas_call(
        kern,
        grid=(NT // tm,),
        in_specs=[pl.BlockSpec((tm, D), lambda i: (i, 0)),
                  pl.BlockSpec(mod_l.shape, lambda i: (0, 0)),
                  pl.BlockSpec((1, D), lambda i: (0, 0)),
                  pl.BlockSpec(w_main.shape, lambda i: (0, 0)),
                  pl.BlockSpec(w_ab.shape, lambda i: (0, 0))],
        out_specs=[pl.BlockSpec((tm, w), lambda i: (i, 0)) for w, _ in widths],
        out_shape=[jax.ShapeDtypeStruct((NT, w), dt) for w, dt in widths],
        compiler_params=pltpu.CompilerParams(vmem_limit_bytes=VMEM_LIMIT),
        name="inproj",
    )(X, mod_l, g.reshape(1, D), w_main, w_ab)


def _group_ones(width, group):
    i = np.arange(width)
    return jnp.asarray((i[:, None] // group == i[None, :] // group).astype(np.float32))


def _group_rms(x, gmat, group, gain):
    ss = jnp.dot(x * x, gmat, preferred_element_type=jnp.float32, precision=HI)
    return x * lax.rsqrt(ss * (1.0 / group) + RMS_EPS) * gain


def rope_tables(seq, tm):
    nf = 8
    t = jnp.arange(seq)
    rows = (t // GRID_W).astype(jnp.float32)
    cols = (t % GRID_W).astype(jnp.float32)
    freqs = ROPE_BASE ** (-jnp.arange(nf, dtype=jnp.float32) / nf)
    d = np.arange(32)
    f_idx = d % 8
    use_col = (d // 16) == 1
    ang = jnp.where(use_col[None, :], cols[:, None], rows[:, None]) * freqs[f_idx][None, :]
    sign = np.where((d % 16) < 8, -1.0, 1.0).astype(np.float32)
    cos = jnp.tile(jnp.cos(ang), (1, 8))
    sin = jnp.tile(jnp.sin(ang) * sign[None, :], (1, 8))
    cos = jnp.concatenate([cos, jnp.ones((tm, 256), jnp.float32)], 0)
    sin = jnp.concatenate([sin, jnp.zeros((tm, 256), jnp.float32)], 0)
    return cos, sin


def _prep_kernel(na_ref, da_ref, cos_ref, sin_ref, g64_ref, g32_ref, gains_ref, naq_ref, nak_ref, daq_ref, dak_ref,
                 davx_ref):
    g64 = g64_ref[...]
    g32 = g32_ref[...]
    na = na_ref[...].astype(jnp.float32)
    naq_ref[...] = (_group_rms(na[:, 0:256], g64, 64, gains_ref[0:1, :]) * (64 ** -0.5)).astype(naq_ref.dtype)
    nak_ref[...] = _group_rms(na[:, 256:512], g64, 64, gains_ref[1:2, :]).astype(nak_ref.dtype)
    da = da_ref[...].astype(jnp.float32)
    cos = cos_ref[...]
    sin = sin_ref[...]
    lane = lax.broadcasted_iota(jnp.int32, (1, 256), 1)
    first = (lane % 16) < 8

    def rope(x):
        swapped = jnp.where(first, pltpu.roll(x, 256 - 8, 1), pltpu.roll(x, 8, 1))
        return x * cos + swapped * sin

    q = rope(_group_rms(da[:, 0:256], g32, 32, gains_ref[2:3, :]))
    k = rope(_group_rms(da[:, 256:512], g32, 32, gains_ref[3:4, :]))
    daq_ref[...] = (q * (32 ** -0.5 * LOG2E)).astype(daq_ref.dtype)
    dak_ref[...] = k.astype(dak_ref.dtype)
    v = da_ref[:, 512:768]
    one_col = (lax.broadcasted_iota(jnp.int32, (v.shape[0], 64), 1) == 0).astype(v.dtype)
    davx_ref[...] = jnp.concatenate(
        [piece for h in range(4) for piece in (v[:, h * 64:(h + 1) * 64], one_col)], -1)


def prep_call(P_na, P_da, cos, sin, na_qg, na_kg, da_qg, da_kg, *, n_batch, seq, tm=256):
    NT = P_na.shape[0]
    tpb = seq // tm
    n_lat = n_batch * tpb
    gains = jnp.stack([jnp.tile(na_qg, 4), jnp.tile(na_kg, 4), jnp.tile(da_qg, 8), jnp.tile(da_kg, 8)]
                      + [jnp.zeros((256,), jnp.float32)] * 4, 0)
    tab_map = lambda i: (jnp.where(i < n_lat, i % tpb, tpb), 0)
    full = lambda a: pl.BlockSpec(a.shape, lambda i: (0,) * a.ndim)
    g64, g32 = _group_ones(256, 64), _group_ones(256, 32)
    return pl.pallas_call(
        _prep_kernel,
        grid=(NT // tm,),
        in_specs=[pl.BlockSpec((tm, 768), lambda i: (i, 0)), pl.BlockSpec((tm, 768), lambda i: (i, 0)),
                  pl.BlockSpec((tm, 256), tab_map), pl.BlockSpec((tm, 256), tab_map),
                  full(g64), full(g32), full(gains)],
        out_specs=[pl.BlockSpec((tm, 256), lambda i: (i, 0))] * 4 + [pl.BlockSpec((tm, 512), lambda i: (i, 0))],
        out_shape=[jax.ShapeDtypeStruct((NT, 256), jnp.bfloat16)] * 4 + [jax.ShapeDtypeStruct((NT, 512), jnp.bfloat16)],
        name="attn_prep",
    )(P_na, P_da, cos, sin, g64, g32, gains)


NA_RB = 4
NA_UR = 11


def na_bias_tables(rpb, rows):
    reps = [(0, 0), (4, 0), (rows - NA_RB, rows - NA_UR)]
    qc = np.arange(GRID_W)[:, None]
    kc = np.arange(GRID_W)[None, :]
    ws = np.clip(qc - 8, 0, GRID_W - 16)
    vcol = (kc >= ws) & (kc < ws + 16)
    rel_c = np.clip(kc - qc + 15, 0, 30)
    sel_c = (rel_c[..., None] == np.arange(31)).astype(np.float32)
    sel_r, vrow = [], []
    for r0, u0 in reps:
        r = r0 + np.arange(NA_RB)[:, None]
        krow = u0 + np.arange(NA_UR)[None, :]
        start = np.clip(r - 4, 0, rows - 8)
        vrow.append((krow >= start) & (krow < start + 8))
        sel_r.append((np.clip(krow - r + 7, 0, 14)[..., None] == np.arange(15)).astype(np.float32))
    sel_r, vrow = np.stack(sel_r), np.stack(vrow)
    b = jnp.einsum('taui,hij,qkj->thaquk', jnp.asarray(sel_r), rpb.astype(jnp.float32), jnp.asarray(sel_c),
                   precision=HI)
    valid = vrow[:, None, :, None, :, None] & vcol[None, None, None, :, None, :]
    b = jnp.where(valid, b, NEG_INF)
    return b.reshape(3, rpb.shape[0], NA_RB * GRID_W, NA_UR * GRID_W)


def _na_kernel(q_ref, k_ref, v_ref, kc_ref, vc_ref, bias_ref, o_ref, *, rows):
    i = pl.program_id(1)
    u0 = jnp.clip(i * NA_RB - 4, 0, rows - NA_UR)
    off = pl.multiple_of(u0 * GRID_W, GRID_W)
    nkw = NA_UR * GRID_W
    kwin = k_ref[pl.ds(off, nkw), :]
    vwin = v_ref[pl.ds(off, nkw), :]
    q = q_ref[...]
    kc = kc_ref[...]
    vc = vc_ref[...]
    outs = []
    for h in range(4):
        sl = slice(h * 64, (h + 1) * 64)
        qh = q[:, sl]
        s_loc = lax.dot_general(qh, kwin[:, sl], NT_DIMS, preferred_element_type=jnp.float32) + bias_ref[0, h]
        s_ctx = lax.dot_general(qh, kc[:, sl], NT_DIMS, preferred_element_type=jnp.float32)
        m = jnp.maximum(jnp.max(s_loc, -1, keepdims=True), jnp.max(s_ctx, -1, keepdims=True))
        p_loc = jnp.exp(s_loc - m)
        p_ctx = jnp.exp(s_ctx - m)
        l = jnp.sum(p_loc, -1, keepdims=True) + jnp.sum(p_ctx, -1, keepdims=True)
        o = (jnp.dot(p_loc.astype(jnp.bfloat16), vwin[:, sl], preferred_element_type=jnp.float32)
             + jnp.dot(p_ctx.astype(jnp.bfloat16), vc[:, sl], preferred_element_type=jnp.float32))
        outs.append(o / l)
    o_ref[...] = jnp.concatenate(outs, -1).astype(o_ref.dtype)


def na_call(NAQ, NAK, P_na, bias, *, n_batch, seq, ctx_len):
    rows = seq // GRID_W
    nblk = rows // NA_RB
    tq = NA_RB * GRID_W
    cb = n_batch * seq // ctx_len
    kern = functools.partial(_na_kernel, rows=rows)
    tsel = lambda b, i: (jnp.where(i == 0, 0, jnp.where(i == nblk - 1, 2, 1)), 0, 0, 0)
    return pl.pallas_call(
        kern,
        grid=(n_batch, nblk),
        in_specs=[pl.BlockSpec((tq, 256), lambda b, i: (b * nblk + i, 0)),
                  pl.BlockSpec((seq, 256), lambda b, i: (b, 0)),
                  pl.BlockSpec((seq, 256), lambda b, i: (b, 2)),
                  pl.BlockSpec((ctx_len, 256), lambda b, i: (cb + b, 0)),
                  pl.BlockSpec((ctx_len, 256), lambda b, i: (cb + b, 2)),
                  pl.BlockSpec((1,) + bias.shape[1:], tsel)],
        out_specs=pl.BlockSpec((tq, 256), lambda b, i: (b * nblk + i, 0)),
        out_shape=jax.ShapeDtypeStruct((n_batch * seq, 256), jnp.bfloat16),
        compiler_params=pltpu.CompilerParams(vmem_limit_bytes=VMEM_LIMIT),
        name="na_attn",
    )(NAQ, NAK, P_na, NAK, P_na, bias)


def _stack_masked_q(q):
    lane = lax.broadcasted_iota(jnp.int32, (1, 256), 1)
    return jnp.concatenate([jnp.where((lane // 32) == hm, q, jnp.zeros_like(q)) for hm in range(8)], 0)


def _da_finish(acc, lam, gain, g64, post_scale):
    tq = acc.shape[0] // 8
    outs = []
    for h in range(4):
        a0 = acc[(2 * h) * tq:(2 * h + 1) * tq]
        a1 = acc[(2 * h + 1) * tq:(2 * h + 2) * tq]
        outs.append(a0[:, 0:64] / a0[:, 64:65] - lam * (a1[:, 0:64] / a1[:, 64:65]))
    o = jnp.concatenate(outs, -1)
    return _group_rms(o, g64, 64, gain) * post_scale


def _da_tile(qs, k, vx, m, acc, tq):
    tk = k.shape[0]
    s = lax.dot_general(qs, k, NT_DIMS, preferred_element_type=jnp.float32)
    m_new = jnp.maximum(m, jnp.max(s, -1, keepdims=True))
    alpha = jnp.exp2(m - m_new)
    p = jnp.exp2(s - pltpu.repeat(m_new, tk // 128, axis=1)).astype(jnp.bfloat16)
    pv = jnp.concatenate(
        [jnp.dot(p[2 * h * tq:(2 * h + 2) * tq], vx[:, h * 128:(h + 1) * 128], preferred_element_type=jnp.float32)
         for h in range(4)], 0)
    return m_new, alpha * acc + pv


def _da_kernel(lam_ref, q_ref, k_ref, vx_ref, kc_ref, vxc_ref, gain_ref, g64_ref, o_ref, m_ref, acc_ref,
               *, tk, n_kv, post_scale):
    tq = q_ref.shape[0]
    qs = _stack_masked_q(q_ref[...])
    m_ref[...] = jnp.full(m_ref.shape, -jnp.inf, jnp.float32)
    acc_ref[...] = jnp.zeros(acc_ref.shape, jnp.float32)

    def body(j, carry):
        for u in range(2):
            off = pl.multiple_of((2 * j + u) * tk, tk)
            m, acc = _da_tile(qs, k_ref[pl.ds(off, tk), :], vx_ref[pl.ds(off, tk), :], m_ref[...], acc_ref[...], tq)
            m_ref[...] = m
            acc_ref[...] = acc
        return carry

    lax.fori_loop(0, n_kv // 2, body, 0)
    m, acc = _da_tile(qs, kc_ref[...], vxc_ref[...], m_ref[...], acc_ref[...], tq)
    o_ref[...] = _da_finish(acc, lam_ref[0], gain_ref[...], g64_ref[...], post_scale).astype(o_ref.dtype)


def da_call(lam, DAQ, DAK, DAVX, subln_g, *, n_batch, seq, ctx_len, lam_init, tq=256, tk=1024):
    tk = min(tk, seq // 2)
    nq = seq // tq
    cb = n_batch * seq // ctx_len
    g64 = _group_ones(256, 64)
    gain = jnp.tile(subln_g, 4).reshape(1, 256)
    n_kv = seq // tk
    assert n_kv % 2 == 0
    kern = functools.partial(_da_kernel, tk=tk, n_kv=n_kv, post_scale=1.0 - lam_init)
    return pl.pallas_call(
        kern,
        grid=(n_batch, nq),
        in_specs=[pl.BlockSpec(memory_space=pltpu.SMEM),
                  pl.BlockSpec((tq, 256), lambda b, i: (b * nq + i, 0)),
                  pl.BlockSpec((seq, 256), lambda b, i: (b, 0)),
                  pl.BlockSpec((seq, 512), lambda b, i: (b, 0)),
                  pl.BlockSpec((ctx_len, 256), lambda b, i: (cb + b, 0)),
                  pl.BlockSpec((ctx_len, 512), lambda b, i: (cb + b, 0)),
                  pl.BlockSpec((1, 256), lambda b, i: (0, 0)),
                  pl.BlockSpec((256, 256), lambda b, i: (0, 0))],
        out_specs=pl.BlockSpec((tq, 256), lambda b, i: (b * nq + i, 0)),
        out_shape=jax.ShapeDtypeStruct((n_batch * seq, 256), jnp.bfloat16),
        scratch_shapes=[pltpu.VMEM((8 * tq, 128), jnp.float32), pltpu.VMEM((8 * tq, 128), jnp.float32)],
        compiler_params=pltpu.CompilerParams(vmem_limit_bytes=VMEM_LIMIT),
        name="da_attn",
    )(lam.reshape(1), DAQ, DAK, DAVX, DAK, DAVX, gain, g64)


def _ctx_kernel(lam_ref, naq_ref, nak_ref, nav_ref, daq_ref, dak_ref, dav_ref, gain_ref, g64_ref, ya_ref, yd_ref,
                *, post_scale):
    q = naq_ref[...]
    k = nak_ref[...]
    v = nav_ref[...]
    outs = []
    for h in range(4):
        sl = slice(h * 64, (h + 1) * 64)
        s = lax.dot_general(q[:, sl], k[:, sl], NT_DIMS, preferred_element_type=jnp.float32)
        p = jnp.exp(s - jnp.max(s, -1, keepdims=True))
        o = jnp.dot(p.astype(jnp.bfloat16), v[:, sl], preferred_element_type=jnp.float32)
        outs.append(o / jnp.sum(p, -1, keepdims=True))
    ya_ref[...] = jnp.concatenate(outs, -1).astype(ya_ref.dtype)
    tq = daq_ref.shape[0]
    qs = _stack_masked_q(daq_ref[...])
    m0 = jnp.full((8 * tq, 128), -jnp.inf, jnp.float32)
    m, acc = _da_tile(qs, dak_ref[...], dav_ref[...], m0, jnp.zeros((8 * tq, 128), jnp.float32), tq)
    yd_ref[...] = _da_finish(acc, lam_ref[0], gain_ref[...], g64_ref[...], post_scale).astype(yd_ref.dtype)


def ctx_attn_call(lam, NAQ, NAK, P_na, DAQ, DAK, DAVX, subln_g, *, n_batch, seq, ctx_len, lam_init):
    cb = n_batch * seq // ctx_len
    g64 = _group_ones(256, 64)
    gain = jnp.tile(subln_g, 4).reshape(1, 256)
    blk = lambda col: pl.BlockSpec((ctx_len, 256), lambda b: (cb + b, col))
    return pl.pallas_call(
        functools.partial(_ctx_kernel, post_scale=1.0 - lam_init),
        grid=(n_batch,),
        in_specs=[pl.BlockSpec(memory_space=pltpu.SMEM), blk(0), blk(0), blk(2), blk(0), blk(0),
                  pl.BlockSpec((ctx_len, 512), lambda b: (cb + b, 0)),
                  pl.BlockSpec((1, 256), lambda b: (0, 0)), pl.BlockSpec((256, 256), lambda b: (0, 0))],
        out_specs=[pl.BlockSpec((ctx_len, 256), lambda b: (b, 0))] * 2,
        out_shape=[jax.ShapeDtypeStruct((n_batch * ctx_len, 256), jnp.bfloat16)] * 2,
        name="ctx_attn",
    )(lam.reshape(1), NAQ, NAK, P_na, DAQ, DAK, DAVX, gain, g64)


def _split_bf16(a):
    hi = a.astype(jnp.bfloat16)
    return hi, (a - hi.astype(jnp.float32)).astype(jnp.bfloat16)


def _route(logits_t, bias_col):
    per_group = N_EXPERTS // N_GROUPS
    scores = jax.nn.sigmoid(logits_t)
    sel_all = scores + bias_col
    sel = [sel_all[e:e + 1, :] for e in range(N_EXPERTS)]
    top2 = []
    for e in range(N_EXPERTS):
        g0 = e // per_group * per_group
        rank = jnp.zeros_like(sel[e])
        for o in range(g0, g0 + per_group):
            if o != e:
                beats = (sel[o] > sel[e]) | (sel[o] == sel[e]) if o < e else (sel[o] > sel[e])
                rank = rank + beats.astype(jnp.float32)
        top2.append(rank < 1.5)
    gs = []
    for g in range(N_GROUPS):
        acc = jnp.zeros_like(sel[0])
        for e in range(g * per_group, (g + 1) * per_group):
            acc = acc + jnp.where(top2[e], sel[e], 0.0)
        gs.append(acc)
    rows = []
    for g in range(N_GROUPS):
        beaten = jnp.zeros_like(sel[0])
        for o in range(N_GROUPS):
            if o != g:
                b = (gs[o] >= gs[g]) if o < g else (gs[o] > gs[g])
                beaten = beaten + b.astype(jnp.float32)
        best = beaten < 0.5
        for e in range(g * per_group, (g + 1) * per_group):
            rows.append(jnp.where(best & top2[e], scores[e:e + 1, :], 0.0))
    w = jnp.concatenate(rows, 0)
    return w / jnp.sum(w, axis=0, keepdims=True)


def _merge_kernel(x_ref, ya_ref, yd_ref, of_ref, ob_ref, z_ref, yf_ref, gate_ref, m_ref, g2_ref, og_ref, g64_ref,
                  wb_ref, wo_ref, wr_ref, rb_ref, xo_ref, h2_ref, rw_ref, *, tiles_per_batch, n_lat_tiles, n_batch):
    i = pl.program_id(0)
    r = jnp.where(i < n_lat_tiles, i // tiles_per_batch, n_batch)
    z = z_ref[...].astype(jnp.float32)
    yg = (_group_rms(of_ref[...] + ob_ref[...], g64_ref[...], GDN_DIM, og_ref[...]) * (z * jax.nn.sigmoid(z)))
    branches = (ya_ref[...], yd_ref[...], yg.astype(jnp.bfloat16), yf_ref[...])
    mix = None
    for b, yb in enumerate(branches):
        proj = jnp.dot(yb, wb_ref[b], preferred_element_type=jnp.float32)
        term = jax.nn.sigmoid(gate_ref[:, b * D_MODEL:(b + 1) * D_MODEL].astype(jnp.float32)) * proj
        mix = term if mix is None else mix + term
    y = jnp.dot(mix.astype(jnp.bfloat16), wo_ref[...], preferred_element_type=jnp.float32)
    mrow = lambda k: m_ref[pl.ds(r, 1), k * D_MODEL:(k + 1) * D_MODEL]
    x = x_ref[...] + mrow(2) * y
    xo_ref[...] = x
    h2 = (x * lax.rsqrt(jnp.mean(x * x, axis=-1, keepdims=True) + RMS_EPS) * g2_ref[...]) * (1.0 + mrow(4)) + mrow(3)
    h2_ref[...] = h2.astype(h2_ref.dtype)
    h_hi, h_lo = _split_bf16(h2)
    w_hi, w_lo = _split_bf16(wr_ref[...])
    nt = lambda a, b: lax.dot_general(a, b, NT_DIMS, preferred_element_type=jnp.float32)
    logits_t = nt(w_hi, h_hi) + nt(w_hi, h_lo) + nt(w_lo, h_hi)
    w_t = _route(logits_t, rb_ref[...])
    pad = jnp.zeros((128 - N_EXPERTS, w_t.shape[1]), jnp.float32)
    rw_ref[...] = jnp.transpose(jnp.concatenate([w_t, pad], 0))


def merge_call(X, Ya, Yd, Of, Ob, P_z, Yf, P_gate, mod_l, g2n, onorm_g, w_branch, w_out, w_router, router_bias,
               *, n_tok, n_batch, seq, tm=256):
    D = X.shape[1]
    n_lat_tiles = n_batch * seq // tm
    kern = functools.partial(_merge_kernel, tiles_per_batch=seq // tm, n_lat_tiles=n_lat_tiles, n_batch=n_batch)
    row = lambda w: pl.BlockSpec((tm, w), lambda i: (i, 0))
    full = lambda a: pl.BlockSpec(a.shape, lambda i: (0,) * a.ndim)
    wr_t = jnp.transpose(w_router)
    rb = router_bias.reshape(N_EXPERTS, 1)
    g2 = g2n.reshape(1, D)
    og = jnp.tile(onorm_g, GDN_HEADS).reshape(1, BRANCH_W)
    g64 = _group_ones(BRANCH_W, GDN_DIM)
    return pl.pallas_call(
        kern,
        grid=(n_tok // tm,),
        in_specs=[row(D), row(256), row(256), row(256), row(256), row(256), row(256), row(4 * D), full(mod_l),
                  full(g2), full(og), full(g64), full(w_branch), full(w_out), full(wr_t), full(rb)],
        out_specs=[row(D), row(D), row(128)],
        out_shape=[jax.ShapeDtypeStruct((n_tok, D), jnp.float32), jax.ShapeDtypeStruct((n_tok, D), jnp.bfloat16),
                   jax.ShapeDtypeStruct((n_tok, 128), jnp.float32)],
        compiler_params=pltpu.CompilerParams(vmem_limit_bytes=VMEM_LIMIT),
        name="merge",
    )(X, Ya, Yd, Of, Ob, P_z, Yf, P_gate, mod_l, g2, og, g64, w_branch, w_out, wr_t, rb)


MOE_TILE = 1024
MOE_WIN = 320
MOE_WIN_CTX = 176


def _split3_bf16(a):
    hi = a.astype(jnp.bfloat16)
    r = a - hi.astype(jnp.float32)
    mid = r.astype(jnp.bfloat16)
    return hi, mid, (r - mid.astype(jnp.float32)).astype(jnp.bfloat16)


def _moe_sorted_kernel(x_ref, h_ref, rw_ref, m_ref, wg_ref, wu_ref, wd_ref, o_ref, xs_ref, ws_ref, ys_ref, dest_ref,
                       seg_ref, *, win, mod_row_fn):
    T = h_ref.shape[0]
    per_group = N_EXPERTS // N_GROUPS
    i = pl.program_id(0)
    e = pl.program_id(1)
    lane = lax.broadcasted_iota(jnp.int32, (1, 128), 1)
    f32, bf16 = jnp.float32, jnp.bfloat16

    @pl.when(e == 0)
    def _():
        rw = rw_ref[...]
        r128 = lax.broadcasted_iota(jnp.int32, (128, 128), 0)
        c128 = lax.broadcasted_iota(jnp.int32, (128, 128), 1)
        e2g = ((r128 // per_group == c128) & (r128 < N_EXPERTS)).astype(bf16)
        og = (jnp.dot((rw > 0.0).astype(bf16), e2g, preferred_element_type=f32) > 0.5).astype(f32)
        rt = lax.broadcasted_iota(jnp.int32, (T, T), 0)
        ct = lax.broadcasted_iota(jnp.int32, (T, T), 1)
        earlier = jnp.dot((rt > ct).astype(bf16), og.astype(bf16), preferred_element_type=f32)
        cnt_row = jnp.sum(og, axis=0, keepdims=True)
        start_row = jnp.dot(jnp.broadcast_to(cnt_row, (8, 128)), (r128 < c128).astype(f32),
                            preferred_element_type=f32, precision=HI)[0:1]
        dest = jnp.sum(og * (earlier + start_row), axis=-1, keepdims=True)
        dest_ref[...] = jnp.broadcast_to(dest, (T, 128))
        dest_row = jnp.transpose(dest_ref[...])[0:1, :]
        perm = (dest_row == rt.astype(f32)).astype(bf16)
        xs_ref[...] = jnp.dot(perm, h_ref[...], preferred_element_type=f32).astype(bf16)
        ws_ref[...] = sum(jnp.dot(perm, piece, preferred_element_type=f32) for piece in _split3_bf16(rw))
        ys_ref[...] = jnp.zeros(ys_ref.shape, f32)
        for g in range(N_GROUPS):
            seg_ref[g] = jnp.sum(jnp.where(lane == g, start_row, 0.0)).astype(jnp.int32)
            seg_ref[N_GROUPS + g] = jnp.sum(jnp.where(lane == g, cnt_row, 0.0)).astype(jnp.int32)

    g = e // per_group
    start = seg_ref[g]
    end = start + seg_ref[N_GROUPS + g]
    s0 = (start // 16) * 16
    n_win = (end - s0 + win - 1) // win

    def window(w, carry):
        lo = s0 + w * win
        off = pl.multiple_of(jnp.minimum(lo, T - win), 16)
        xw = xs_ref[pl.ds(off, win), :]
        w_col = jnp.sum(jnp.where(lane == e, ws_ref[pl.ds(off, win), :], 0.0), axis=-1, keepdims=True)
        row = off + lax.broadcasted_iota(jnp.int32, (win, 1), 0)
        w_col = jnp.where(row >= lo, w_col, 0.0)
        a = jnp.dot(xw, wg_ref[0], preferred_element_type=f32)
        u = jnp.dot(xw, wu_ref[0], preferred_element_type=f32)
        act = (a * jax.nn.sigmoid(a) * u * w_col).astype(bf16)
        ys_ref[pl.ds(off, win), :] += jnp.dot(act, wd_ref[0], preferred_element_type=f32)
        return carry

    lax.fori_loop(0, n_win, window, 0)

    @pl.when(e == N_EXPERTS - 1)
    def _():
        ct = lax.broadcasted_iota(jnp.int32, (T, T), 1)
        unperm = (pltpu.repeat(dest_ref[...], T // 128, axis=1) == ct.astype(f32)).astype(bf16)
        y_hi, y_lo = _split_bf16(ys_ref[...])
        f = (jnp.dot(unperm, y_hi, preferred_element_type=f32) + jnp.dot(unperm, y_lo, preferred_element_type=f32))
        g2 = m_ref[pl.ds(mod_row_fn(i), 1), 5 * D_MODEL:6 * D_MODEL]
        o_ref[...] = x_ref[...] + g2 * f


def moe_sorted_call(Xmid, H2, RW, mod_l, w_gate, w_up, w_down, *, row0, n_rows, tile, win, mod_row_fn):
    D = Xmid.shape[1]
    F = w_gate.shape[-1]
    b0 = row0 // tile
    tok = lambda w: pl.BlockSpec((tile, w), lambda i, e: (b0 + i, 0))
    kern = functools.partial(_moe_sorted_kernel, win=win, mod_row_fn=mod_row_fn)
    return pl.pallas_call(
        kern,
        grid=(n_rows // tile, N_EXPERTS),
        in_specs=[tok(D), tok(D), tok(128),
                  pl.BlockSpec(mod_l.shape, lambda i, e: (0, 0)),
                  pl.BlockSpec((1, D, F), lambda i, e: (e, 0, 0)),
                  pl.BlockSpec((1, D, F), lambda i, e: (e, 0, 0)),
                  pl.BlockSpec((1, F, D), lambda i, e: (e, 0, 0))],
        out_specs=pl.BlockSpec((tile, D), lambda i, e: (i, 0)),
        out_shape=jax.ShapeDtypeStruct((n_rows, D), jnp.float32),
        scratch_shapes=[pltpu.VMEM((tile, D), jnp.bfloat16), pltpu.VMEM((tile, 128), jnp.float32),
                        pltpu.VMEM((tile, D), jnp.float32), pltpu.VMEM((tile, 128), jnp.float32),
                        pltpu.SMEM((2 * N_GROUPS,), jnp.int32)],
        compiler_params=pltpu.CompilerParams(vmem_limit_bytes=VMEM_LIMIT),
        name="moe_sorted",
    )(Xmid, H2, RW, mod_l, w_gate, w_up, w_down)


GDN_TILE = 256
GDN_CPT = GDN_TILE // GDN_CHUNK
GDN_LOCKSTEP = 2


def _mm(a, b, passes=1, dims=None):
    if dims is None:
        dot = lambda x, y: jnp.dot(x, y, preferred_element_type=jnp.float32)
    else:
        dot = lambda x, y: lax.dot_general(x, y, dims, preferred_element_type=jnp.float32)
    if passes == 1:
        return dot(a.astype(jnp.bfloat16), b.astype(jnp.bfloat16))
    a_hi, a_lo = _split_bf16(a)
    b_hi, b_lo = _split_bf16(b)
    return dot(a_hi, b_hi) + dot(a_hi, b_lo) + dot(a_lo, b_hi)


def _stack_heads(x):
    lane = lax.broadcasted_iota(jnp.int32, (1, 256), 1)
    return jnp.concatenate([jnp.where((lane // GDN_DIM) == h, x, 0.0) for h in range(GDN_HEADS)], 0)


def _slab(x):
    return x[0:64] + x[64:128] + x[128:192] + x[192:256]


def _unit_tri_inverse(mats):
    shape = mats[0].shape
    eye = (lax.broadcasted_iota(jnp.int32, shape, 0) == lax.broadcasted_iota(jnp.int32, shape, 1)).astype(jnp.float32)
    ps = [eye - a for a in mats]
    pws = list(mats)
    for _ in range(5):
        pws = [_mm(pw, pw, 3) for pw in pws]
        ps = [p + _mm(p, pw, 3) for p, pw in zip(ps, pws)]
    return ps


def _gdn_chunk_kernel(x_ref, xp_ref, xn_ref, ab_ref, cw_ref, par_ref, g64_ref, qe_ref, ou_ref, mm_ref, nn_ref,
                      *, tiles_per_batch, n_lat_tiles):
    i = pl.program_id(0)
    is_lat = i < n_lat_tiles
    first = jnp.where(is_lat, (i % tiles_per_batch) == 0, True)
    last = jnp.where(is_lat, (i % tiles_per_batch) == tiles_per_batch - 1, True)
    xp = jnp.concatenate([jnp.where(first, 0.0, xp_ref[...]), x_ref[...], jnp.where(last, 0.0, xn_ref[...])], 0)
    y = xp[6:6 + GDN_TILE] * cw_ref[0:1, :]
    for t in range(1, 5):
        y = y + xp[6 + t:6 + t + GDN_TILE] * cw_ref[t:t + 1, :]
    y = y * jax.nn.sigmoid(y)
    g64 = g64_ref[...]
    q = y[:, 0:256]
    k = y[:, 256:512]
    v = y[:, 512:768]
    q = q * lax.rsqrt(jnp.dot(q * q, g64, preferred_element_type=jnp.float32, precision=HI) + RMS_EPS) * (GDN_DIM ** -0.5)
    k = k * lax.rsqrt(jnp.dot(k * k, g64, preferred_element_type=jnp.float32, precision=HI) + RMS_EPS)
    ab = ab_ref[...]
    lane128 = lax.broadcasted_iota(jnp.int32, (1, 128), 1)
    g_all = jnp.where(lane128 < 8, par_ref[0:1, :] * jax.nn.softplus(ab + par_ref[1:2, :]), 0.0)
    beta_all = jax.nn.sigmoid(ab)
    r64 = lax.broadcasted_iota(jnp.int32, (64, 64), 0)
    c64 = lax.broadcasted_iota(jnp.int32, (64, 64), 1)
    low = (r64 >= c64).astype(jnp.float32)
    upp = (r64 <= c64).astype(jnp.float32)
    rr = lax.broadcasted_iota(jnp.int32, (256, 256), 0)
    cc = lax.broadcasted_iota(jnp.int32, (256, 256), 1)
    same = (rr // 64) == (cc // 64)
    eye = (rr == cc).astype(jnp.float32)
    for c0 in range(0, GDN_CPT, GDN_LOCKSTEP):
        stage = []
        for c in range(c0, c0 + GDN_LOCKSTEP):
            sl = slice(c * GDN_CHUNK, (c + 1) * GDN_CHUNK)
            g_c = g_all[sl]
            gsum = jnp.where(lane128 < 4, jnp.dot(low, g_c, preferred_element_type=jnp.float32, precision=HI),
                             jnp.dot(upp, g_c, preferred_element_type=jnp.float32, precision=HI))
            gsum_t = jnp.transpose(jnp.concatenate([gsum, jnp.zeros_like(gsum)], 0))[:, 0:64]
            tot = jnp.sum(g_c, axis=0, keepdims=True)
            beta_c = beta_all[sl]
            q_sm = _stack_heads(q[sl])
            k_sm = _stack_heads(k[sl])
            v_sm = _stack_heads(v[sl])
            kk = _mm(k_sm, k_sm, 1, NT_DIMS)
            qk = _mm(q_sm, k_sm, 1, NT_DIMS)
            for d in range(2):
                col = lambda h: 4 * d + h
                cmat = jnp.concatenate([jnp.broadcast_to(gsum[:, col(h):col(h) + 1], (64, 256)) for h in range(4)], 0)
                rrow = jnp.concatenate([gsum_t[col(h):col(h) + 1, :] for h in range(4)], 1)
                bmat = jnp.concatenate([jnp.broadcast_to(beta_c[:, 8 + col(h):9 + col(h)], (64, 256))
                                        for h in range(4)], 0)
                tmat = jnp.concatenate([jnp.broadcast_to(tot[:, col(h):col(h) + 1], (64, 256)) for h in range(4)], 0)
                tri = (rr % 64 >= cc % 64) if d == 0 else (rr % 64 <= cc % 64)
                incl = same & tri
                strict = incl & (rr != cc)
                dec = jnp.where(incl, jnp.exp(jnp.where(incl, cmat - rrow, 0.0)), 0.0)
                stage.append((d, sl, q_sm, k_sm, v_sm, qk, cmat, bmat, tmat, dec,
                              jnp.where(strict, kk * bmat * dec, 0.0)))
        t_invs = _unit_tri_inverse([st[-1] for st in stage])
        for (d, sl, q_sm, k_sm, v_sm, qk, cmat, bmat, tmat, dec, _), t_inv in zip(stage, t_invs):
            eg = jnp.exp(cmat)
            kb = k_sm * bmat
            u_sm = _mm(t_inv, v_sm * bmat, 1)
            w_sm = _mm(t_inv, kb * eg, 1)
            qkm = dec * qk
            qe_sm = q_sm * eg - _mm(qkm, w_sm, 1)
            ou_sm = _mm(qkm, u_sm, 1)
            kd_t = jnp.transpose(k_sm * jnp.exp(tmat - cmat))
            m_bd = eye * jnp.exp(tmat) - _mm(kd_t, w_sm, 1)
            n_bd = _mm(kd_t, u_sm, 1)
            qe_ref[d, sl, :] = _slab(qe_sm)
            ou_ref[d, sl, :] = _slab(ou_sm)
            mm_ref[d, sl, :] = _slab(m_bd)
            nn_ref[d, sl, :] = _slab(n_bd)


def gdn_chunk_call(P_gdn, P_ab, conv_w, a_log, dt_bias, *, n_batch, seq):
    NT = P_gdn.shape[0]
    tpb = seq // GDN_TILE
    n_lat = n_batch * tpb
    n_tiles = NT // GDN_TILE
    hb = GDN_TILE // 8
    cw = jnp.zeros((8, 768), jnp.float32).at[:5].set(conv_w)
    par = jnp.zeros((8, 128), jnp.float32)
    par = par.at[0, :8].set(-jnp.exp(a_log.reshape(8))).at[1, :8].set(dt_bias.reshape(8))
    g64 = _group_ones(256, 64)
    kern = functools.partial(_gdn_chunk_kernel, tiles_per_batch=tpb, n_lat_tiles=n_lat)
    full = lambda a: pl.BlockSpec(a.shape, lambda i: (0,) * a.ndim)
    out = pl.BlockSpec((2, GDN_TILE, 256), lambda i: (0, i, 0))
    return pl.pallas_call(
        kern,
        grid=(n_tiles,),
        in_specs=[pl.BlockSpec((GDN_TILE, 768), lambda i: (i, 0)),
                  pl.BlockSpec((8, 768), lambda i: (jnp.maximum(i * hb - 1, 0), 0)),
                  pl.BlockSpec((8, 768), lambda i: (jnp.minimum((i + 1) * hb, n_tiles * hb - 1), 0)),
                  pl.BlockSpec((GDN_TILE, 128), lambda i: (i, 0)),
                  full(cw), full(par), full(g64)],
        out_specs=[out] * 4,
        out_shape=[jax.ShapeDtypeStruct((2, NT, 256), jnp.float32)] * 4,
        compiler_params=pltpu.CompilerParams(vmem_limit_bytes=VMEM_LIMIT),
        name="gdn_chunk",
    )(P_gdn, P_gdn, P_gdn, P_ab, cw, par, g64)


def _gdn_scan_kernel(qef_ref, ouf_ref, mf_ref, nf_ref, qeb_ref, oub_ref, mb_ref, nb_ref, of_ref, ob_ref, s_ref):
    t = pl.program_id(1)

    @pl.when(t == 0)
    def _():
        s_ref[...] = jnp.zeros(s_ref.shape, jnp.float32)

    def step(d, c, qe_ref, ou_ref, m_ref, n_ref, o_ref):
        sl = slice(c * GDN_CHUNK, (c + 1) * GDN_CHUNK)
        s = s_ref[d]
        o_ref[sl, :] = _slab(_mm(_stack_heads(qe_ref[0, sl, :]), s, 3)) + ou_ref[0, sl, :]
        s_ref[d] = _mm(_stack_heads(m_ref[0, sl, :]), s, 3) + _stack_heads(n_ref[0, sl, :])

    for c in range(GDN_CPT):
        step(0, c, qef_ref, ouf_ref, mf_ref, nf_ref, of_ref)
        step(1, GDN_CPT - 1 - c, qeb_ref, oub_ref, mb_ref, nb_ref, ob_ref)


def gdn_scan_call(QE, OU, MM, NN, *, n_batch, seq):
    NT = QE.shape[1]
    tpb = seq // GDN_TILE
    cblk = n_batch * tpb
    fwd = lambda b, t: jnp.where(t == 0, cblk + b, b * tpb + t - 1)
    bwd = lambda b, t: jnp.where(t == 0, cblk + b, b * tpb + tpb - t)
    spec = lambda d, f: pl.BlockSpec((1, GDN_TILE, 256), lambda b, t: (d, f(b, t), 0))
    ospec = lambda f: pl.BlockSpec((GDN_TILE, 256), lambda b, t: (f(b, t), 0))
    return pl.pallas_call(
        _gdn_scan_kernel,
        grid=(n_batch, tpb + 1),
        in_specs=[spec(0, fwd)] * 4 + [spec(1, bwd)] * 4,
        out_specs=[ospec(fwd), ospec(bwd)],
        out_shape=[jax.ShapeDtypeStruct((NT, 256), jnp.float32)] * 2,
        scratch_shapes=[pltpu.VMEM((2, 256, 256), jnp.float32)],
        compiler_params=pltpu.CompilerParams(vmem_limit_bytes=VMEM_LIMIT),
        name="gdn_scan",
    )(QE, OU, MM, NN, QE, OU, MM, NN)


def _dft_cs(n):
    a = 2.0 * np.pi * np.outer(np.arange(n), np.arange(n)) / n
    return np.cos(a), np.sin(a)


def _channel_dft():
    c, s = _dft_cs(FNET_GROUP_W)
    eye = np.eye(FNET_GROUPS)
    return np.concatenate([np.kron(eye, c), np.kron(eye, s)], 1)


def _fnet_a_kernel(x_ref, cs_ref, m_ref, cphi_ref, sphi_ref, br_ref, bi_ref, *, n_rows):
    for j in range(x_ref.shape[1] // BRANCH_W):
        sl = slice(j * BRANCH_W, (j + 1) * BRANCH_W)
        u = jnp.dot(x_ref[:, sl], cs_ref[...], preferred_element_type=jnp.float32)
        st = jnp.concatenate([u[:, :BRANCH_W], u[:, BRANCH_W:]], 0).astype(jnp.bfloat16)
        a = jnp.dot(m_ref[...], st, preferred_element_type=jnp.float32)
        ar, ai = a[:n_rows], a[n_rows:]
        cp, sp = cphi_ref[:, sl], sphi_ref[:, sl]
        br_ref[j] = (ar * cp - ai * sp).astype(br_ref.dtype)
        bi_ref[j] = (ar * sp + ai * cp).astype(bi_ref.dtype)


def _fnet_c_kernel(br_ref, bi_ref, m_ref, o_ref):
    st = jnp.concatenate([br_ref[...], bi_ref[...]], 0)
    o_ref[...] = jnp.dot(m_ref[...], st, preferred_element_type=jnp.float32).astype(o_ref.dtype)


def fourier_latent_call(P_f, *, n_batch, seq):
    rows = seq // GRID_W
    bf16 = jnp.bfloat16
    c1, s1 = _dft_cs(rows)
    norm = 1.0 / math.sqrt(seq * FNET_GROUP_W)
    m_a = jnp.asarray(np.block([[c1, -s1], [s1, c1]]) * norm, bf16)
    c2, s2 = _dft_cs(GRID_W)
    m_c = jnp.asarray(np.concatenate([c2, -s2], 1), bf16)
    cs = jnp.asarray(_channel_dft(), bf16)
    phi = 2.0 * np.pi * np.outer(np.arange(rows), np.arange(GRID_W)) / seq
    cphi = jnp.repeat(jnp.asarray(np.cos(phi), jnp.float32), BRANCH_W, axis=1)
    sphi = jnp.repeat(jnp.asarray(np.sin(phi), jnp.float32), BRANCH_W, axis=1)
    xv = P_f[:n_batch * seq].reshape(n_batch * rows, GRID_W * BRANCH_W)
    t2b = 4
    nj = GRID_W // t2b
    full = lambda a: pl.BlockSpec(a.shape, lambda b, j: (0,) * a.ndim)
    br, bi = pl.pallas_call(
        functools.partial(_fnet_a_kernel, n_rows=rows),
        grid=(n_batch, nj),
        in_specs=[pl.BlockSpec((rows, t2b * BRANCH_W), lambda b, j: (b, j)), full(cs), full(m_a),
                  pl.BlockSpec((rows, t2b * BRANCH_W), lambda b, j: (0, j)),
                  pl.BlockSpec((rows, t2b * BRANCH_W), lambda b, j: (0, j))],
        out_specs=[pl.BlockSpec((t2b, rows, BRANCH_W), lambda b, j: (b * nj + j, 0, 0))] * 2,
        out_shape=[jax.ShapeDtypeStruct((n_batch * GRID_W, rows, BRANCH_W), bf16)] * 2,
        name="fnet_rows",
    )(xv, cs, m_a, cphi, sphi)
    width = rows * BRANCH_W
    tc = 2048
    y = pl.pallas_call(
        _fnet_c_kernel,
        grid=(n_batch, width // tc),
        in_specs=[pl.BlockSpec((GRID_W, tc), lambda b, j: (b, j)), pl.BlockSpec((GRID_W, tc), lambda b, j: (b, j)),
                  pl.BlockSpec(m_c.shape, lambda b, j: (0, 0))],
        out_specs=pl.BlockSpec((GRID_W, tc), lambda b, j: (b, j)),
        out_shape=jax.ShapeDtypeStruct((n_batch * GRID_W, width), bf16),
        name="fnet_cols",
    )(br.reshape(n_batch * GRID_W, width), bi.reshape(n_batch * GRID_W, width), m_c)
    return y.reshape(n_batch * seq, BRANCH_W)


def _fnet_ctx_kernel(x_ref, cs_ref, m_ref, o_ref):
    u = jnp.dot(x_ref[...], cs_ref[...], preferred_element_type=jnp.float32)
    st = jnp.concatenate([u[:, :BRANCH_W], u[:, BRANCH_W:]], 0).astype(jnp.bfloat16)
    o_ref[...] = jnp.dot(m_ref[...], st, preferred_element_type=jnp.float32).astype(o_ref.dtype)


def fourier_ctx_call(P_f, *, n_batch, seq, ctx_len):
    bf16 = jnp.bfloat16
    c, s = _dft_cs(ctx_len)
    m = jnp.asarray(np.concatenate([c, -s], 1) / math.sqrt(ctx_len * FNET_GROUP_W), bf16)
    cs = jnp.asarray(_channel_dft(), bf16)
    cb = n_batch * seq // ctx_len
    return pl.pallas_call(
        _fnet_ctx_kernel,
        grid=(n_batch,),
        in_specs=[pl.BlockSpec((ctx_len, BRANCH_W), lambda b: (cb + b, 0)),
                  pl.BlockSpec(cs.shape, lambda b: (0, 0)), pl.BlockSpec(m.shape, lambda b: (0, 0))],
        out_specs=pl.BlockSpec((ctx_len, BRANCH_W), lambda b: (b, 0)),
        out_shape=jax.ShapeDtypeStruct((n_batch * ctx_len, BRANCH_W), bf16),
        name="fnet_ctx",
    )(P_f, cs, m)


def kernel(x, c, ctx, c_ctx, w_mod, b_mod, norm1_g, norm2_g, w_in, na_qn_g, na_kn_g, na_rpb,
           da_qn_g, da_kn_g, da_lam_q1, da_lam_k1, da_lam_q2, da_lam_k2, da_subln_g, gdn_conv_w,
           gdn_a_log, gdn_dt_bias, gdn_onorm_g, w_branch, w_out, w_router, router_bias,
           moe_w_gate, moe_w_up, moe_w_down):
    B, S, D = x.shape
    L = ctx.shape[1]
    NL = B * S
    f32 = jnp.float32
    cvec = jnp.zeros((8, D), f32).at[:B].set(c).at[B].set(c_ctx)
    mods = mod_call(cvec, w_mod, b_mod)
    cos, sin = rope_tables(S, 256)
    bf16 = jnp.bfloat16
    X = jnp.concatenate([x.reshape(NL, D), ctx.reshape(B * L, D)], 0)
    for l in range(DEPTH):
        last = l == DEPTH - 1
        lam_init = 0.8 - 0.6 * math.exp(-0.3 * l)
        w_main = jnp.concatenate([w_in[l][:, :2560], w_in[l][:, 2576:]], 1).astype(jnp.bfloat16)
        w_ab = jnp.pad(w_in[l][:, 2560:2576], ((0, 0), (0, 112))).astype(jnp.bfloat16)
        P_na, P_da, P_gdn, P_z, P_f, P_gate, P_ab = inproj_call(X, mods[l], norm1_g[l], w_main, w_ab,
                                                                n_batch=B, seq=S)
        NAQ, NAK, DAQ, DAK, DAVX = prep_call(P_na, P_da, cos, sin, na_qn_g[l], na_kn_g[l], da_qn_g[l], da_kn_g[l],
                                             n_batch=B, seq=S)
        bias = na_bias_tables(na_rpb[l], S // GRID_W)
        lam = (jnp.exp(jnp.sum(da_lam_q1[l] * da_lam_k1[l])) - jnp.exp(jnp.sum(da_lam_q2[l] * da_lam_k2[l]))
               + lam_init).astype(f32)
        Ya = na_call(NAQ, NAK, P_na, bias, n_batch=B, seq=S, ctx_len=L)
        Yd = da_call(lam, DAQ, DAK, DAVX, da_subln_g[l], n_batch=B, seq=S, ctx_len=L, lam_init=lam_init)
        QE, OU, MM, NN = gdn_chunk_call(P_gdn, P_ab, gdn_conv_w[l], gdn_a_log[l], gdn_dt_bias[l], n_batch=B, seq=S)
        Of, Ob = gdn_scan_call(QE, OU, MM, NN, n_batch=B, seq=S)
        Yf = fourier_latent_call(P_f, n_batch=B, seq=S)
        n_tok = NL
        if not last:
            Yac, Ydc = ctx_attn_call(lam, NAQ, NAK, P_na, DAQ, DAK, DAVX, da_subln_g[l],
                                     n_batch=B, seq=S, ctx_len=L, lam_init=lam_init)
            Ya = jnp.concatenate([Ya, Yac], 0)
            Yd = jnp.concatenate([Yd, Ydc], 0)
            Yf = jnp.concatenate([Yf, fourier_ctx_call(P_f, n_batch=B, seq=S, ctx_len=L)], 0)
            n_tok = NL + B * L
        Xmid, H2, RW = merge_call(X, Ya, Yd, Of, Ob, P_z, Yf, P_gate, mods[l], norm2_g[l], gdn_onorm_g[l],
                                  w_branch[l].astype(bf16), w_out[l].astype(bf16), w_router, router_bias,
                                  n_tok=n_tok, n_batch=B, seq=S)
        experts = (moe_w_gate[l].astype(bf16), moe_w_up[l].astype(bf16), moe_w_down[l].astype(bf16))
        X = moe_sorted_call(Xmid, H2, RW, mods[l], *experts, row0=0, n_rows=NL, tile=MOE_TILE, win=MOE_WIN,
                            mod_row_fn=lambda i: i // (S // MOE_TILE))
        if not last:
            Xc = moe_sorted_call(Xmid, H2, RW, mods[l], *experts, row0=NL, n_rows=B * L, tile=B * L,
                                 win=MOE_WIN_CTX, mod_row_fn=lambda i: B)
            X = jnp.concatenate([X, Xc], 0)
    return X.reshape(B, S, D)
```

```python
import functools
import math
import jax
import jax.numpy as jnp
from jax import lax
import numpy as np
from jax.experimental import pallas as pl
from jax.experimental.pallas import tpu as pltpu

D_MODEL = 1024
DEPTH = 2
GRID_W = 64
N_BRANCH = 4
BRANCH_W = D_MODEL // N_BRANCH
GDN_HEADS = 4
GDN_DIM = BRANCH_W // GDN_HEADS
GDN_CHUNK = 64
FNET_GROUPS = 4
FNET_GROUP_W = BRANCH_W // FNET_GROUPS
N_EXPERTS = 16
N_GROUPS = 4
RMS_EPS = 1e-6
NEG_INF = -1e30
ROPE_BASE = 10000.0
VMEM_LIMIT = 56 * 1024 * 1024
HI = lax.Precision.HIGHEST
NT_DIMS = (((1,), (1,)), ((), ()))
LOG2E = 1.4426950408889634


def _mod_kernel(c_ref, w_ref, b_ref, o_ref):
    c = c_ref[...]
    a = c * jax.nn.sigmoid(c)
    o_ref[0] = jnp.dot(a, w_ref[0], preferred_element_type=jnp.float32, precision=HI) + b_ref[0]


def mod_call(cvec, w_mod, b_mod, tn=512):
    depth, D, N = w_mod.shape
    return pl.pallas_call(
        _mod_kernel,
        grid=(depth, N // tn),
        in_specs=[pl.BlockSpec((8, D), lambda l, j: (0, 0)),
                  pl.BlockSpec((1, D, tn), lambda l, j: (l, 0, j)),
                  pl.BlockSpec((1, 1, tn), lambda l, j: (l, 0, j))],
        out_specs=pl.BlockSpec((1, 8, tn), lambda l, j: (l, 0, j)),
        out_shape=jax.ShapeDtypeStruct((depth, 8, N), jnp.float32),
        name="mod",
    )(cvec, w_mod, b_mod.reshape(depth, 1, N))


def _inproj_kernel(x_ref, m_ref, g_ref, w_ref, wab_ref, na_ref, da_ref, gdn_ref, z_ref, f_ref, gate_ref, ab_ref,
                   *, tiles_per_batch, n_lat_tiles, n_batch):
    i = pl.program_id(0)
    r = jnp.where(i < n_lat_tiles, i // tiles_per_batch, n_batch)
    x = x_ref[...]
    y = x * lax.rsqrt(jnp.mean(x * x, axis=-1, keepdims=True) + RMS_EPS) * g_ref[...]
    sh = m_ref[pl.ds(r, 1), 0:D_MODEL]
    sc = m_ref[pl.ds(r, 1), D_MODEL:2 * D_MODEL]
    h = (y * (1.0 + sc) + sh).astype(jnp.bfloat16)

    def seg(o_ref, a, b):
        o_ref[...] = jnp.dot(h, w_ref[:, a:b], preferred_element_type=jnp.float32).astype(o_ref.dtype)

    seg(na_ref, 0, 768)
    seg(da_ref, 768, 1536)
    seg(gdn_ref, 1536, 2304)
    seg(z_ref, 2304, 2560)
    seg(f_ref, 2560, 2816)
    for k in range(4):
        gate_ref[:, k * 1024:(k + 1) * 1024] = jnp.dot(
            h, w_ref[:, 2816 + k * 1024:2816 + (k + 1) * 1024],
            preferred_element_type=jnp.float32).astype(gate_ref.dtype)
    ab_ref[...] = jnp.dot(h, wab_ref[...], preferred_element_type=jnp.float32)


def inproj_call(X, mod_l, g, w_main, w_ab, *, n_batch, seq, tm=256):
    NT, D = X.shape
    n_lat_tiles = n_batch * seq // tm
    kern = functools.partial(_inproj_kernel, tiles_per_batch=seq // tm, n_lat_tiles=n_lat_tiles, n_batch=n_batch)
    widths = [(768, jnp.bfloat16), (768, jnp.bfloat16), (768, jnp.float32), (256, jnp.bfloat16),
              (256, jnp.bfloat16), (4096, jnp.bfloat16), (128, jnp.float32)]
    return pl.pallas_call(
        kern,
        grid=(NT // tm,),
        in_specs=[pl.BlockSpec((tm, D), lambda i: (i, 0)),
                  pl.BlockSpec(mod_l.shape, lambda i: (0, 0)),
                  pl.BlockSpec((1, D), lambda i: (0, 0)),
                  pl.BlockSpec(w_main.shape, lambda i: (0, 0)),
                  pl.BlockSpec(w_ab.shape, lambda i: (0, 0))],
        out_specs=[pl.BlockSpec((tm, w), lambda i: (i, 0)) for w, _ in widths],
        out_shape=[jax.ShapeDtypeStruct((NT, w), dt) for w, dt in widths],
        compiler_params=pltpu.CompilerParams(vmem_limit_bytes=VMEM_LIMIT),
        name="inproj",
    )(X, mod_l, g.reshape(1, D), w_main, w_ab)


def _group_ones(width, group):
    i = np.arange(width)
    return jnp.asarray((i[:, None] // group == i[None, :] // group).astype(np.float32), jnp.bfloat16)


def _group_sum(xx, gmat):
    hi, lo = _split_bf16(xx)
    return (jnp.dot(hi, gmat, preferred_element_type=jnp.float32)
            + jnp.dot(lo, gmat, preferred_element_type=jnp.float32))


def _group_rms(x, gmat, group, gain):
    return x * lax.rsqrt(_group_sum(x * x, gmat) * (1.0 / group) + RMS_EPS) * gain


def rope_tables(seq, tm):
    nf = 8
    t = jnp.arange(seq)
    rows = (t // GRID_W).astype(jnp.float32)
    cols = (t % GRID_W).astype(jnp.float32)
    freqs = ROPE_BASE ** (-jnp.arange(nf, dtype=jnp.float32) / nf)
    d = np.arange(32)
    f_idx = d % 8
    use_col = (d // 16) == 1
    ang = jnp.where(use_col[None, :], cols[:, None], rows[:, None]) * freqs[f_idx][None, :]
    sign = np.where((d % 16) < 8, -1.0, 1.0).astype(np.float32)
    cos = jnp.tile(jnp.cos(ang), (1, 8))
    sin = jnp.tile(jnp.sin(ang) * sign[None, :], (1, 8))
    cos = jnp.concatenate([cos, jnp.ones((tm, 256), jnp.float32)], 0)
    sin = jnp.concatenate([sin, jnp.zeros((tm, 256), jnp.float32)], 0)
    return cos, sin


def _prep_kernel(na_ref, da_ref, cos_ref, sin_ref, g64_ref, g32_ref, gains_ref, naq_ref, nak_ref, daq_ref, dak_ref,
                 davx_ref):
    g64 = g64_ref[...]
    g32 = g32_ref[...]
    na = na_ref[...].astype(jnp.float32)
    naq_ref[...] = (_group_rms(na[:, 0:256], g64, 64, gains_ref[0:1, :]) * (64 ** -0.5)).astype(naq_ref.dtype)
    nak_ref[...] = _group_rms(na[:, 256:512], g64, 64, gains_ref[1:2, :]).astype(nak_ref.dtype)
    da = da_ref[...].astype(jnp.float32)
    cos = cos_ref[...]
    sin = sin_ref[...]
    lane = lax.broadcasted_iota(jnp.int32, (1, 256), 1)
    first = (lane % 16) < 8

    def rope(x):
        swapped = jnp.where(first, pltpu.roll(x, 256 - 8, 1), pltpu.roll(x, 8, 1))
        return x * cos + swapped * sin

    q = rope(_group_rms(da[:, 0:256], g32, 32, gains_ref[2:3, :]))
    k = rope(_group_rms(da[:, 256:512], g32, 32, gains_ref[3:4, :]))
    daq_ref[...] = (q * (32 ** -0.5 * LOG2E)).astype(daq_ref.dtype)
    dak_ref[...] = k.astype(dak_ref.dtype)
    v = da_ref[:, 512:768]
    one_col = (lax.broadcasted_iota(jnp.int32, (v.shape[0], 64), 1) == 0).astype(v.dtype)
    davx_ref[...] = jnp.concatenate(
        [piece for h in range(4) for piece in (v[:, h * 64:(h + 1) * 64], one_col)], -1)


def prep_call(P_na, P_da, cos, sin, na_qg, na_kg, da_qg, da_kg, *, n_batch, seq, tm=256):
    NT = P_na.shape[0]
    tpb = seq // tm
    n_lat = n_batch * tpb
    gains = jnp.stack([jnp.tile(na_qg, 4), jnp.tile(na_kg, 4), jnp.tile(da_qg, 8), jnp.tile(da_kg, 8)]
                      + [jnp.zeros((256,), jnp.float32)] * 4, 0)
    tab_map = lambda i: (jnp.where(i < n_lat, i % tpb, tpb), 0)
    full = lambda a: pl.BlockSpec(a.shape, lambda i: (0,) * a.ndim)
    g64, g32 = _group_ones(256, 64), _group_ones(256, 32)
    return pl.pallas_call(
        _prep_kernel,
        grid=(NT // tm,),
        in_specs=[pl.BlockSpec((tm, 768), lambda i: (i, 0)), pl.BlockSpec((tm, 768), lambda i: (i, 0)),
                  pl.BlockSpec((tm, 256), tab_map), pl.BlockSpec((tm, 256), tab_map),
                  full(g64), full(g32), full(gains)],
        out_specs=[pl.BlockSpec((tm, 256), lambda i: (i, 0))] * 4 + [pl.BlockSpec((tm, 512), lambda i: (i, 0))],
        out_shape=[jax.ShapeDtypeStruct((NT, 256), jnp.bfloat16)] * 4 + [jax.ShapeDtypeStruct((NT, 512), jnp.bfloat16)],
        name="attn_prep",
    )(P_na, P_da, cos, sin, g64, g32, gains)


NA_RB = 4
NA_UR = 11


def na_bias_tables(rpb, rows):
    reps = [(0, 0), (4, 0), (rows - NA_RB, rows - NA_UR)]
    qc = np.arange(GRID_W)[:, None]
    kc = np.arange(GRID_W)[None, :]
    ws = np.clip(qc - 8, 0, GRID_W - 16)
    vcol = (kc >= ws) & (kc < ws + 16)
    rel_c = np.clip(kc - qc + 15, 0, 30)
    sel_c = (rel_c[..., None] == np.arange(31)).astype(np.float32)
    sel_r, vrow = [], []
    for r0, u0 in reps:
        r = r0 + np.arange(NA_RB)[:, None]
        krow = u0 + np.arange(NA_UR)[None, :]
        start = np.clip(r - 4, 0, rows - 8)
        vrow.append((krow >= start) & (krow < start + 8))
        sel_r.append((np.clip(krow - r + 7, 0, 14)[..., None] == np.arange(15)).astype(np.float32))
    sel_r, vrow = np.stack(sel_r), np.stack(vrow)
    b = jnp.einsum('taui,hij,qkj->thaquk', jnp.asarray(sel_r), rpb.astype(jnp.float32), jnp.asarray(sel_c),
                   precision=HI)
    valid = vrow[:, None, :, None, :, None] & vcol[None, None, None, :, None, :]
    b = jnp.where(valid, b, NEG_INF)
    return b.reshape(3, rpb.shape[0], NA_RB * GRID_W, NA_UR * GRID_W).astype(jnp.bfloat16)


def _na_kernel(q_ref, k_ref, v_ref, kc_ref, vc_ref, bias_ref, o_ref, *, rows):
    i = pl.program_id(1)
    u0 = jnp.clip(i * NA_RB - 4, 0, rows - NA_UR)
    off = pl.multiple_of(u0 * GRID_W, GRID_W)
    nkw = NA_UR * GRID_W
    kwin = k_ref[pl.ds(off, nkw), :]
    vwin = v_ref[pl.ds(off, nkw), :]
    q = q_ref[...]
    kc = kc_ref[...]
    vc = vc_ref[...]
    outs = []
    for h in range(4):
        sl = slice(h * 64, (h + 1) * 64)
        qh = q[:, sl]
        s_loc = lax.dot_general(qh, kwin[:, sl], NT_DIMS, preferred_element_type=jnp.float32) + bias_ref[0, h].astype(jnp.float32)
        s_ctx = lax.dot_general(qh, kc[:, sl], NT_DIMS, preferred_element_type=jnp.float32)
        m = jnp.maximum(jnp.max(s_loc, -1, keepdims=True), jnp.max(s_ctx, -1, keepdims=True))
        p_loc = jnp.exp(s_loc - m)
        p_ctx = jnp.exp(s_ctx - m)
        l = jnp.sum(p_loc, -1, keepdims=True) + jnp.sum(p_ctx, -1, keepdims=True)
        o = (jnp.dot(p_loc.astype(jnp.bfloat16), vwin[:, sl], preferred_element_type=jnp.float32)
             + jnp.dot(p_ctx.astype(jnp.bfloat16), vc[:, sl], preferred_element_type=jnp.float32))
        outs.append(o / l)
    o_ref[...] = jnp.concatenate(outs, -1).astype(o_ref.dtype)


def na_call(NAQ, NAK, P_na, bias, *, n_batch, seq, ctx_len):
    rows = seq // GRID_W
    nblk = rows // NA_RB
    tq = NA_RB * GRID_W
    cb = n_batch * seq // ctx_len
    kern = functools.partial(_na_kernel, rows=rows)
    tsel = lambda b, i: (jnp.where(i == 0, 0, jnp.where(i == nblk - 1, 2, 1)), 0, 0, 0)
    return pl.pallas_call(
        kern,
        grid=(n_batch, nblk),
        in_specs=[pl.BlockSpec((tq, 256), lambda b, i: (b * nblk + i, 0)),
                  pl.BlockSpec((seq, 256), lambda b, i: (b, 0)),
                  pl.BlockSpec((seq, 256), lambda b, i: (b, 2)),
                  pl.BlockSpec((ctx_len, 256), lambda b, i: (cb + b, 0)),
                  pl.BlockSpec((ctx_len, 256), lambda b, i: (cb + b, 2)),
                  pl.BlockSpec((1,) + bias.shape[1:], tsel)],
        out_specs=pl.BlockSpec((tq, 256), lambda b, i: (b * nblk + i, 0)),
        out_shape=jax.ShapeDtypeStruct((n_batch * seq, 256), jnp.bfloat16),
        compiler_params=pltpu.CompilerParams(vmem_limit_bytes=VMEM_LIMIT),
        name="na_attn",
    )(NAQ, NAK, P_na, NAK, P_na, bias)


def _stack_masked_q(q):
    lane = lax.broadcasted_iota(jnp.int32, (1, 256), 1)
    return jnp.concatenate([jnp.where((lane // 32) == hm, q, jnp.zeros_like(q)) for hm in range(8)], 0)


def _da_finish(acc, lam, gain, g64, post_scale):
    tq = acc.shape[0] // 8
    outs = []
    for h in range(4):
        a0 = acc[(2 * h) * tq:(2 * h + 1) * tq]
        a1 = acc[(2 * h + 1) * tq:(2 * h + 2) * tq]
        outs.append(a0[:, 0:64] / a0[:, 64:65] - lam * (a1[:, 0:64] / a1[:, 64:65]))
    o = jnp.concatenate(outs, -1)
    return _group_rms(o, g64, 64, gain) * post_scale


def _da_tile(qs, k, vx, m, acc, tq):
    tk = k.shape[0]
    s = lax.dot_general(qs, k, NT_DIMS, preferred_element_type=jnp.float32)
    m_new = jnp.maximum(m, jnp.max(s, -1, keepdims=True))
    alpha = jnp.exp2(m - m_new)
    p = jnp.exp2(s - pltpu.repeat(m_new, tk // 128, axis=1)).astype(jnp.bfloat16)
    pv = jnp.concatenate(
        [jnp.dot(p[2 * h * tq:(2 * h + 2) * tq], vx[:, h * 128:(h + 1) * 128], preferred_element_type=jnp.float32)
         for h in range(4)], 0)
    return m_new, alpha * acc + pv


def _da_kernel(lam_ref, q_ref, k_ref, vx_ref, kc_ref, vxc_ref, gain_ref, g64_ref, o_ref, m_ref, acc_ref,
               *, tk, n_kv, post_scale):
    tq = q_ref.shape[0]
    qs = _stack_masked_q(q_ref[...])
    m_ref[...] = jnp.full(m_ref.shape, -jnp.inf, jnp.float32)
    acc_ref[...] = jnp.zeros(acc_ref.shape, jnp.float32)

    def body(j, carry):
        for u in range(2):
            off = pl.multiple_of((2 * j + u) * tk, tk)
            m, acc = _da_tile(qs, k_ref[pl.ds(off, tk), :], vx_ref[pl.ds(off, tk), :], m_ref[...], acc_ref[...], tq)
            m_ref[...] = m
            acc_ref[...] = acc
        return carry

    lax.fori_loop(0, n_kv // 2, body, 0)
    m, acc = _da_tile(qs, kc_ref[...], vxc_ref[...], m_ref[...], acc_ref[...], tq)
    o_ref[...] = _da_finish(acc, lam_ref[0], gain_ref[...], g64_ref[...], post_scale).astype(o_ref.dtype)


def da_call(lam, DAQ, DAK, DAVX, subln_g, *, n_batch, seq, ctx_len, lam_init, tq=256, tk=1024):
    tk = min(tk, seq // 2)
    nq = seq // tq
    cb = n_batch * seq // ctx_len
    g64 = _group_ones(256, 64)
    gain = jnp.tile(subln_g, 4).reshape(1, 256)
    n_kv = seq // tk
    assert n_kv % 2 == 0
    kern = functools.partial(_da_kernel, tk=tk, n_kv=n_kv, post_scale=1.0 - lam_init)
    return pl.pallas_call(
        kern,
        grid=(n_batch, nq),
        in_specs=[pl.BlockSpec(memory_space=pltpu.SMEM),
                  pl.BlockSpec((tq, 256), lambda b, i: (b * nq + i, 0)),
                  pl.BlockSpec((seq, 256), lambda b, i: (b, 0)),
                  pl.BlockSpec((seq, 512), lambda b, i: (b, 0)),
                  pl.BlockSpec((ctx_len, 256), lambda b, i: (cb + b, 0)),
                  pl.BlockSpec((ctx_len, 512), lambda b, i: (cb + b, 0)),
                  pl.BlockSpec((1, 256), lambda b, i: (0, 0)),
                  pl.BlockSpec((256, 256), lambda b, i: (0, 0))],
        out_specs=pl.BlockSpec((tq, 256), lambda b, i: (b * nq + i, 0)),
        out_shape=jax.ShapeDtypeStruct((n_batch * seq, 256), jnp.bfloat16),
        scratch_shapes=[pltpu.VMEM((8 * tq, 128), jnp.float32), pltpu.VMEM((8 * tq, 128), jnp.float32)],
        compiler_params=pltpu.CompilerParams(vmem_limit_bytes=VMEM_LIMIT),
        name="da_attn",
    )(lam.reshape(1), DAQ, DAK, DAVX, DAK, DAVX, gain, g64)


def _ctx_kernel(lam_ref, naq_ref, nak_ref, nav_ref, daq_ref, dak_ref, dav_ref, gain_ref, g64_ref, ya_ref, yd_ref,
                *, post_scale):
    q = naq_ref[...]
    k = nak_ref[...]
    v = nav_ref[...]
    outs = []
    for h in range(4):
        sl = slice(h * 64, (h + 1) * 64)
        s = lax.dot_general(q[:, sl], k[:, sl], NT_DIMS, preferred_element_type=jnp.float32)
        p = jnp.exp(s - jnp.max(s, -1, keepdims=True))
        o = jnp.dot(p.astype(jnp.bfloat16), v[:, sl], preferred_element_type=jnp.float32)
        outs.append(o / jnp.sum(p, -1, keepdims=True))
    ya_ref[...] = jnp.concatenate(outs, -1).astype(ya_ref.dtype)
    tq = daq_ref.shape[0]
    qs = _stack_masked_q(daq_ref[...])
    m0 = jnp.full((8 * tq, 128), -jnp.inf, jnp.float32)
    m, acc = _da_tile(qs, dak_ref[...], dav_ref[...], m0, jnp.zeros((8 * tq, 128), jnp.float32), tq)
    yd_ref[...] = _da_finish(acc, lam_ref[0], gain_ref[...], g64_ref[...], post_scale).astype(yd_ref.dtype)


def ctx_attn_call(lam, NAQ, NAK, P_na, DAQ, DAK, DAVX, subln_g, *, n_batch, seq, ctx_len, lam_init):
    cb = n_batch * seq // ctx_len
    g64 = _group_ones(256, 64)
    gain = jnp.tile(subln_g, 4).reshape(1, 256)
    blk = lambda col: pl.BlockSpec((ctx_len, 256), lambda b: (cb + b, col))
    return pl.pallas_call(
        functools.partial(_ctx_kernel, post_scale=1.0 - lam_init),
        grid=(n_batch,),
        in_specs=[pl.BlockSpec(memory_space=pltpu.SMEM), blk(0), blk(0), blk(2), blk(0), blk(0),
                  pl.BlockSpec((ctx_len, 512), lambda b: (cb + b, 0)),
                  pl.BlockSpec((1, 256), lambda b: (0, 0)), pl.BlockSpec((256, 256), lambda b: (0, 0))],
        out_specs=[pl.BlockSpec((ctx_len, 256), lambda b: (b, 0))] * 2,
        out_shape=[jax.ShapeDtypeStruct((n_batch * ctx_len, 256), jnp.bfloat16)] * 2,
        name="ctx_attn",
    )(lam.reshape(1), NAQ, NAK, P_na, DAQ, DAK, DAVX, gain, g64)


def _split_bf16(a):
    hi = a.astype(jnp.bfloat16)
    return hi, (a - hi.astype(jnp.float32)).astype(jnp.bfloat16)


def _route(logits_t, bias_col):
    per_group = N_EXPERTS // N_GROUPS
    scores = jax.nn.sigmoid(logits_t)
    sel_all = scores + bias_col
    sel = [sel_all[e:e + 1, :] for e in range(N_EXPERTS)]
    top2 = []
    for e in range(N_EXPERTS):
        g0 = e // per_group * per_group
        rank = jnp.zeros_like(sel[e])
        for o in range(g0, g0 + per_group):
            if o != e:
                beats = (sel[o] > sel[e]) | (sel[o] == sel[e]) if o < e else (sel[o] > sel[e])
                rank = rank + beats.astype(jnp.float32)
        top2.append(rank < 1.5)
    gs = []
    for g in range(N_GROUPS):
        acc = jnp.zeros_like(sel[0])
        for e in range(g * per_group, (g + 1) * per_group):
            acc = acc + jnp.where(top2[e], sel[e], 0.0)
        gs.append(acc)
    rows = []
    for g in range(N_GROUPS):
        beaten = jnp.zeros_like(sel[0])
        for o in range(N_GROUPS):
            if o != g:
                b = (gs[o] >= gs[g]) if o < g else (gs[o] > gs[g])
                beaten = beaten + b.astype(jnp.float32)
        best = beaten < 0.5
        for e in range(g * per_group, (g + 1) * per_group):
            rows.append(jnp.where(best & top2[e], scores[e:e + 1, :], 0.0))
    w = jnp.concatenate(rows, 0)
    return w / jnp.sum(w, axis=0, keepdims=True)


def _merge_kernel(x_ref, ya_ref, yd_ref, of_ref, ob_ref, z_ref, yf_ref, gate_ref, m_ref, g2_ref, og_ref, g64_ref,
                  wb_ref, wo_ref, wr_ref, rb_ref, xo_ref, h2_ref, rw_ref, *, tiles_per_batch, n_lat_tiles, n_batch):
    i = pl.program_id(0)
    r = jnp.where(i < n_lat_tiles, i // tiles_per_batch, n_batch)
    z = z_ref[...].astype(jnp.float32)
    yg = (_group_rms(of_ref[...] + ob_ref[...], g64_ref[...], GDN_DIM, og_ref[...]) * (z * jax.nn.sigmoid(z)))
    branches = (ya_ref[...], yd_ref[...], yg.astype(jnp.bfloat16), yf_ref[...].astype(jnp.bfloat16))
    mix = None
    for b, yb in enumerate(branches):
        proj = jnp.dot(yb, wb_ref[b], preferred_element_type=jnp.float32)
        term = jax.nn.sigmoid(gate_ref[:, b * D_MODEL:(b + 1) * D_MODEL].astype(jnp.float32)) * proj
        mix = term if mix is None else mix + term
    y = jnp.dot(mix.astype(jnp.bfloat16), wo_ref[...], preferred_element_type=jnp.float32)
    mrow = lambda k: m_ref[pl.ds(r, 1), k * D_MODEL:(k + 1) * D_MODEL]
    x = x_ref[...] + mrow(2) * y
    xo_ref[...] = x
    h2 = (x * lax.rsqrt(jnp.mean(x * x, axis=-1, keepdims=True) + RMS_EPS) * g2_ref[...]) * (1.0 + mrow(4)) + mrow(3)
    h2_ref[...] = h2.astype(h2_ref.dtype)
    h_hi, h_lo = _split_bf16(h2)
    w_hi, w_lo = _split_bf16(wr_ref[...])
    nt = lambda a, b: lax.dot_general(a, b, NT_DIMS, preferred_element_type=jnp.float32)
    logits_t = nt(w_hi, h_hi) + nt(w_hi, h_lo) + nt(w_lo, h_hi)
    w_t = _route(logits_t, rb_ref[...])
    pad = jnp.zeros((128 - N_EXPERTS, w_t.shape[1]), jnp.float32)
    rw_ref[...] = jnp.transpose(jnp.concatenate([w_t, pad], 0))


def merge_call(X, Ya, Yd, Of, Ob, P_z, Yf, P_gate, mod_l, g2n, onorm_g, w_branch, w_out, w_router, router_bias,
               *, n_tok, n_batch, seq, tm=256):
    D = X.shape[1]
    n_lat_tiles = n_batch * seq // tm
    kern = functools.partial(_merge_kernel, tiles_per_batch=seq // tm, n_lat_tiles=n_lat_tiles, n_batch=n_batch)
    row = lambda w: pl.BlockSpec((tm, w), lambda i: (i, 0))
    full = lambda a: pl.BlockSpec(a.shape, lambda i: (0,) * a.ndim)
    wr_t = jnp.transpose(w_router)
    rb = router_bias.reshape(N_EXPERTS, 1)
    g2 = g2n.reshape(1, D)
    og = jnp.tile(onorm_g, GDN_HEADS).reshape(1, BRANCH_W)
    g64 = _group_ones(BRANCH_W, GDN_DIM)
    return pl.pallas_call(
        kern,
        grid=(n_tok // tm,),
        in_specs=[row(D), row(256), row(256), row(256), row(256), row(256), row(256), row(4 * D), full(mod_l),
                  full(g2), full(og), full(g64), full(w_branch), full(w_out), full(wr_t), full(rb)],
        out_specs=[row(D), row(D), row(128)],
        out_shape=[jax.ShapeDtypeStruct((n_tok, D), jnp.float32), jax.ShapeDtypeStruct((n_tok, D), jnp.bfloat16),
                   jax.ShapeDtypeStruct((n_tok, 128), jnp.float32)],
        compiler_params=pltpu.CompilerParams(vmem_limit_bytes=VMEM_LIMIT),
        name="merge",
    )(X, Ya, Yd, Of, Ob, P_z, Yf, P_gate, mod_l, g2, og, g64, w_branch, w_out, wr_t, rb)


MOE_TILE = 1024
MOE_WIN = 320
MOE_WIN_CTX = 176


def _split3_bf16(a):
    hi = a.astype(jnp.bfloat16)
    r = a - hi.astype(jnp.float32)
    mid = r.astype(jnp.bfloat16)
    return hi, mid, (r - mid.astype(jnp.float32)).astype(jnp.bfloat16)


def _moe_sorted_kernel(x_ref, h_ref, rw_ref, m_ref, wg_ref, wu_ref, wd_ref, o_ref, xs_ref, ws_ref, ys_ref, dest_ref,
                       seg_ref, *, win, mod_row_fn):
    T = h_ref.shape[0]
    per_group = N_EXPERTS // N_GROUPS
    i = pl.program_id(0)
    e = pl.program_id(1)
    lane = lax.broadcasted_iota(jnp.int32, (1, 128), 1)
    f32, bf16 = jnp.float32, jnp.bfloat16

    @pl.when(e == 0)
    def _():
        rw = rw_ref[...]
        r128 = lax.broadcasted_iota(jnp.int32, (128, 128), 0)
        c128 = lax.broadcasted_iota(jnp.int32, (128, 128), 1)
        e2g = ((r128 // per_group == c128) & (r128 < N_EXPERTS)).astype(bf16)
        og = (jnp.dot((rw > 0.0).astype(bf16), e2g, preferred_element_type=f32) > 0.5).astype(f32)
        rt = lax.broadcasted_iota(jnp.int32, (T, T), 0)
        ct = lax.broadcasted_iota(jnp.int32, (T, T), 1)
        earlier = jnp.dot((rt > ct).astype(bf16), og.astype(bf16), preferred_element_type=f32)
        cnt_row = jnp.sum(og, axis=0, keepdims=True)
        start_row = jnp.dot(jnp.broadcast_to(cnt_row, (8, 128)), (r128 < c128).astype(f32),
                            preferred_element_type=f32, precision=HI)[0:1]
        dest = jnp.sum(og * (earlier + start_row), axis=-1, keepdims=True)
        dest_ref[...] = jnp.broadcast_to(dest, (T, 128))
        dest_row = jnp.transpose(dest_ref[...])[0:1, :]
        perm = (dest_row == rt.astype(f32)).astype(bf16)
        xs_ref[...] = jnp.dot(perm, h_ref[...], preferred_element_type=f32).astype(bf16)
        ws_ref[...] = sum(jnp.dot(perm, piece, preferred_element_type=f32) for piece in _split3_bf16(rw))
        ys_ref[...] = jnp.zeros(ys_ref.shape, f32)
        for g in range(N_GROUPS):
            seg_ref[g] = jnp.sum(jnp.where(lane == g, start_row, 0.0)).astype(jnp.int32)
            seg_ref[N_GROUPS + g] = jnp.sum(jnp.where(lane == g, cnt_row, 0.0)).astype(jnp.int32)

    g = e // per_group
    start = seg_ref[g]
    end = start + seg_ref[N_GROUPS + g]
    s0 = (start // 16) * 16
    n_win = (end - s0 + win - 1) // win

    def window(w, carry):
        lo = s0 + w * win
        off = pl.multiple_of(jnp.minimum(lo, T - win), 16)
        xw = xs_ref[pl.ds(off, win), :]
        w_col = jnp.sum(jnp.where(lane == e, ws_ref[pl.ds(off, win), :], 0.0), axis=-1, keepdims=True)
        row = off + lax.broadcasted_iota(jnp.int32, (win, 1), 0)
        w_col = jnp.where(row >= lo, w_col, 0.0)
        a = jnp.dot(xw, wg_ref[0], preferred_element_type=f32)
        u = jnp.dot(xw, wu_ref[0], preferred_element_type=f32)
        act = (a * jax.nn.sigmoid(a) * u * w_col).astype(bf16)
        ys_ref[pl.ds(off, win), :] += jnp.dot(act, wd_ref[0], preferred_element_type=f32)
        return carry

    lax.fori_loop(0, n_win, window, 0)

    @pl.when(e == N_EXPERTS - 1)
    def _():
        ct = lax.broadcasted_iota(jnp.int32, (T, T), 1)
        unperm = (pltpu.repeat(dest_ref[...], T // 128, axis=1) == ct.astype(f32)).astype(bf16)
        y_hi, y_lo = _split_bf16(ys_ref[...])
        f = (jnp.dot(unperm, y_hi, preferred_element_type=f32) + jnp.dot(unperm, y_lo, preferred_element_type=f32))
        g2 = m_ref[pl.ds(mod_row_fn(i), 1), 5 * D_MODEL:6 * D_MODEL]
        o_ref[...] = x_ref[...] + g2 * f


def moe_sorted_call(Xmid, H2, RW, mod_l, w_gate, w_up, w_down, *, row0, n_rows, tile, win, mod_row_fn):
    D = Xmid.shape[1]
    F = w_gate.shape[-1]
    b0 = row0 // tile
    tok = lambda w: pl.BlockSpec((tile, w), lambda i, e: (b0 + i, 0))
    kern = functools.partial(_moe_sorted_kernel, win=win, mod_row_fn=mod_row_fn)
    return pl.pallas_call(
        kern,
        grid=(n_rows // tile, N_EXPERTS),
        in_specs=[tok(D), tok(D), tok(128),
                  pl.BlockSpec(mod_l.shape, lambda i, e: (0, 0)),
                  pl.BlockSpec((1, D, F), lambda i, e: (e, 0, 0)),
                  pl.BlockSpec((1, D, F), lambda i, e: (e, 0, 0)),
                  pl.BlockSpec((1, F, D), lambda i, e: (e, 0, 0))],
        out_specs=pl.BlockSpec((tile, D), lambda i, e: (i, 0)),
        out_shape=jax.ShapeDtypeStruct((n_rows, D), jnp.float32),
        scratch_shapes=[pltpu.VMEM((tile, D), jnp.bfloat16), pltpu.VMEM((tile, 128), jnp.float32),
                        pltpu.VMEM((tile, D), jnp.float32), pltpu.VMEM((tile, 128), jnp.float32),
                        pltpu.SMEM((2 * N_GROUPS,), jnp.int32)],
        compiler_params=pltpu.CompilerParams(vmem_limit_bytes=VMEM_LIMIT),
        name="moe_sorted",
    )(Xmid, H2, RW, mod_l, w_gate, w_up, w_down)


GDN_TILE = 256
GDN_CPT = GDN_TILE // GDN_CHUNK
GDN_LOCKSTEP = 2


def _mm(a, b, passes=1, dims=None):
    if dims is None:
        dot = lambda x, y: jnp.dot(x, y, preferred_element_type=jnp.float32)
    else:
        dot = lambda x, y: lax.dot_general(x, y, dims, preferred_element_type=jnp.float32)
    if passes == 1:
        return dot(a.astype(jnp.bfloat16), b.astype(jnp.bfloat16))
    a_hi, a_lo = _split_bf16(a)
    b_hi, b_lo = _split_bf16(b)
    return dot(a_hi, b_hi) + dot(a_hi, b_lo) + dot(a_lo, b_hi)


def _stack_heads(x):
    lane = lax.broadcasted_iota(jnp.int32, (1, 256), 1)
    return jnp.concatenate([jnp.where((lane // GDN_DIM) == h, x, 0.0) for h in range(GDN_HEADS)], 0)


def _slab(x):
    return x[0:64] + x[64:128] + x[128:192] + x[192:256]


def _unit_tri_inverse(mats):
    shape = mats[0].shape
    eye = (lax.broadcasted_iota(jnp.int32, shape, 0) == lax.broadcasted_iota(jnp.int32, shape, 1)).astype(jnp.float32)
    ps = [eye - a for a in mats]
    pws = list(mats)
    for _ in range(5):
        pws = [_mm(pw, pw, 3) for pw in pws]
        ps = [p + _mm(p, pw, 3) for p, pw in zip(ps, pws)]
    return ps


def _gdn_chunk_kernel(x_ref, xp_ref, xn_ref, ab_ref, cw_ref, par_ref, g64_ref, qe_ref, ou_ref, mm_ref, nn_ref,
                      *, tiles_per_batch, n_lat_tiles):
    i = pl.program_id(0)
    is_lat = i < n_lat_tiles
    first = jnp.where(is_lat, (i % tiles_per_batch) == 0, True)
    last = jnp.where(is_lat, (i % tiles_per_batch) == tiles_per_batch - 1, True)
    xp = jnp.concatenate([jnp.where(first, 0.0, xp_ref[...]), x_ref[...], jnp.where(last, 0.0, xn_ref[...])], 0)
    y = xp[6:6 + GDN_TILE] * cw_ref[0:1, :]
    for t in range(1, 5):
        y = y + xp[6 + t:6 + t + GDN_TILE] * cw_ref[t:t + 1, :]
    y = y * jax.nn.sigmoid(y)
    g64 = g64_ref[...]
    q = y[:, 0:256]
    k = y[:, 256:512]
    v = y[:, 512:768]
    q = q * lax.rsqrt(_group_sum(q * q, g64) + RMS_EPS) * (GDN_DIM ** -0.5)
    k = k * lax.rsqrt(_group_sum(k * k, g64) + RMS_EPS)
    ab = ab_ref[...]
    lane128 = lax.broadcasted_iota(jnp.int32, (1, 128), 1)
    g_all = jnp.where(lane128 < 8, par_ref[0:1, :] * jax.nn.softplus(ab + par_ref[1:2, :]), 0.0)
    beta_all = jax.nn.sigmoid(ab)
    r64 = lax.broadcasted_iota(jnp.int32, (64, 64), 0)
    c64 = lax.broadcasted_iota(jnp.int32, (64, 64), 1)
    low = (r64 >= c64).astype(jnp.float32)
    upp = (r64 <= c64).astype(jnp.float32)
    rr = lax.broadcasted_iota(jnp.int32, (256, 256), 0)
    cc = lax.broadcasted_iota(jnp.int32, (256, 256), 1)
    same = (rr // 64) == (cc // 64)
    eye = (rr == cc).astype(jnp.float32)
    for c0 in range(0, GDN_CPT, GDN_LOCKSTEP):
        stage = []
        for c in range(c0, c0 + GDN_LOCKSTEP):
            sl = slice(c * GDN_CHUNK, (c + 1) * GDN_CHUNK)
            g_c = g_all[sl]
            gsum = jnp.where(lane128 < 4, jnp.dot(low, g_c, preferred_element_type=jnp.float32, precision=HI),
                             jnp.dot(upp, g_c, preferred_element_type=jnp.float32, precision=HI))
            gsum_t = jnp.transpose(jnp.concatenate([gsum, jnp.zeros_like(gsum)], 0))[:, 0:64]
            tot = jnp.sum(g_c, axis=0, keepdims=True)
            beta_c = beta_all[sl]
            q_sm = _stack_heads(q[sl])
            k_sm = _stack_heads(k[sl])
            v_sm = _stack_heads(v[sl])
            kk = _mm(k_sm, k_sm, 1, NT_DIMS)
            qk = _mm(q_sm, k_sm, 1, NT_DIMS)
            for d in range(2):
                col = lambda h: 4 * d + h
                cmat = jnp.concatenate([jnp.broadcast_to(gsum[:, col(h):col(h) + 1], (64, 256)) for h in range(4)], 0)
                rrow = jnp.concatenate([gsum_t[col(h):col(h) + 1, :] for h in range(4)], 1)
                bmat = jnp.concatenate([jnp.broadcast_to(beta_c[:, 8 + col(h):9 + col(h)], (64, 256))
                                        for h in range(4)], 0)
                tmat = jnp.concatenate([jnp.broadcast_to(tot[:, col(h):col(h) + 1], (64, 256)) for h in range(4)], 0)
                tri = (rr % 64 >= cc % 64) if d == 0 else (rr % 64 <= cc % 64)
                incl = same & tri
                strict = incl & (rr != cc)
                dec = jnp.where(incl, jnp.exp(jnp.where(incl, cmat - rrow, 0.0)), 0.0)
                stage.append((d, sl, q_sm, k_sm, v_sm, qk, cmat, bmat, tmat, dec,
                              jnp.where(strict, kk * bmat * dec, 0.0)))
        t_invs = _unit_tri_inverse([st[-1] for st in stage])
        for (d, sl, q_sm, k_sm, v_sm, qk, cmat, bmat, tmat, dec, _), t_inv in zip(stage, t_invs):
            eg = jnp.exp(cmat)
            kb = k_sm * bmat
            u_sm = _mm(t_inv, v_sm * bmat, 1)
            w_sm = _mm(t_inv, kb * eg, 1)
            qkm = dec * qk
            qe_sm = q_sm * eg - _mm(qkm, w_sm, 1)
            ou_sm = _mm(qkm, u_sm, 1)
            kd_t = jnp.transpose(k_sm * jnp.exp(tmat - cmat))
            m_bd = eye * jnp.exp(tmat) - _mm(kd_t, w_sm, 1)
            n_bd = _mm(kd_t, u_sm, 1)
            qe_ref[d, sl, :] = _slab(qe_sm)
            ou_ref[d, sl, :] = _slab(ou_sm)
            mm_ref[d, sl, :] = _slab(m_bd)
            nn_ref[d, sl, :] = _slab(n_bd)


def gdn_chunk_call(P_gdn, P_ab, conv_w, a_log, dt_bias, *, n_batch, seq):
    NT = P_gdn.shape[0]
    tpb = seq // GDN_TILE
    n_lat = n_batch * tpb
    n_tiles = NT // GDN_TILE
    hb = GDN_TILE // 8
    cw = jnp.zeros((8, 768), jnp.float32).at[:5].set(conv_w)
    par = jnp.zeros((8, 128), jnp.float32)
    par = par.at[0, :8].set(-jnp.exp(a_log.reshape(8))).at[1, :8].set(dt_bias.reshape(8))
    g64 = _group_ones(256, 64)
    kern = functools.partial(_gdn_chunk_kernel, tiles_per_batch=tpb, n_lat_tiles=n_lat)
    full = lambda a: pl.BlockSpec(a.shape, lambda i: (0,) * a.ndim)
    out = pl.BlockSpec((2, GDN_TILE, 256), lambda i: (0, i, 0))
    return pl.pallas_call(
        kern,
        grid=(n_tiles,),
        in_specs=[pl.BlockSpec((GDN_TILE, 768), lambda i: (i, 0)),
                  pl.BlockSpec((8, 768), lambda i: (jnp.maximum(i * hb - 1, 0), 0)),
                  pl.BlockSpec((8, 768), lambda i: (jnp.minimum((i + 1) * hb, n_tiles * hb - 1), 0)),
                  pl.BlockSpec((GDN_TILE, 128), lambda i: (i, 0)),
                  full(cw), full(par), full(g64)],
        out_specs=[out] * 4,
        out_shape=[jax.ShapeDtypeStruct((2, NT, 256), jnp.float32)] * 4,
        compiler_params=pltpu.CompilerParams(vmem_limit_bytes=VMEM_LIMIT),
        name="gdn_chunk",
    )(P_gdn, P_gdn, P_gdn, P_ab, cw, par, g64)


def _gdn_scan_kernel(qef_ref, ouf_ref, mf_ref, nf_ref, qeb_ref, oub_ref, mb_ref, nb_ref, of_ref, ob_ref, s_ref):
    t = pl.program_id(1)

    @pl.when(t == 0)
    def _():
        s_ref[...] = jnp.zeros(s_ref.shape, jnp.float32)

    def step(d, c, qe_ref, ou_ref, m_ref, n_ref, o_ref):
        sl = slice(c * GDN_CHUNK, (c + 1) * GDN_CHUNK)
        s = s_ref[d]
        o_ref[sl, :] = _slab(_mm(_stack_heads(qe_ref[0, sl, :]), s, 3)) + ou_ref[0, sl, :]
        s_ref[d] = _mm(_stack_heads(m_ref[0, sl, :]), s, 3) + _stack_heads(n_ref[0, sl, :])

    for c in range(GDN_CPT):
        step(0, c, qef_ref, ouf_ref, mf_ref, nf_ref, of_ref)
        step(1, GDN_CPT - 1 - c, qeb_ref, oub_ref, mb_ref, nb_ref, ob_ref)


def gdn_scan_call(QE, OU, MM, NN, *, n_batch, seq):
    NT = QE.shape[1]
    tpb = seq // GDN_TILE
    cblk = n_batch * tpb
    fwd = lambda b, t: jnp.where(t == 0, cblk + b, b * tpb + t - 1)
    bwd = lambda b, t: jnp.where(t == 0, cblk + b, b * tpb + tpb - t)
    spec = lambda d, f: pl.BlockSpec((1, GDN_TILE, 256), lambda b, t: (d, f(b, t), 0))
    ospec = lambda f: pl.BlockSpec((GDN_TILE, 256), lambda b, t: (f(b, t), 0))
    return pl.pallas_call(
        _gdn_scan_kernel,
        grid=(n_batch, tpb + 1),
        in_specs=[spec(0, fwd)] * 4 + [spec(1, bwd)] * 4,
        out_specs=[ospec(fwd), ospec(bwd)],
        out_shape=[jax.ShapeDtypeStruct((NT, 256), jnp.float32)] * 2,
        scratch_shapes=[pltpu.VMEM((2, 256, 256), jnp.float32)],
        compiler_params=pltpu.CompilerParams(vmem_limit_bytes=VMEM_LIMIT),
        name="gdn_scan",
    )(QE, OU, MM, NN, QE, OU, MM, NN)


def _dft_cs(n):
    a = 2.0 * np.pi * np.outer(np.arange(n), np.arange(n)) / n
    return np.cos(a), np.sin(a)


def _channel_dft():
    c, s = _dft_cs(FNET_GROUP_W)
    eye = np.eye(FNET_GROUPS)
    return np.concatenate([np.kron(eye, c), np.kron(eye, s)], 1)


FNET_SUB = 16


def _fnet_a_kernel(x_ref, cs_ref, m_ref, cphi_ref, sphi_ref, br_ref, bi_ref, *, n_rows):
    for j in range(FNET_SUB):
        sl = slice(j * BRANCH_W, (j + 1) * BRANCH_W)
        u = jnp.dot(x_ref[:, j, :], cs_ref[...], preferred_element_type=jnp.float32)
        st = jnp.concatenate([u[:, :BRANCH_W], u[:, BRANCH_W:]], 0).astype(jnp.bfloat16)
        a = jnp.dot(m_ref[...], st, preferred_element_type=jnp.float32)
        ar, ai = a[:n_rows], a[n_rows:]
        cp, sp = cphi_ref[:, sl], sphi_ref[:, sl]
        br_ref[j] = ar * cp - ai * sp
        bi_ref[j] = ar * sp + ai * cp


def _fnet_c_kernel(br_ref, bi_ref, m_ref, o_ref):
    for j in range(FNET_SUB):
        st = jnp.concatenate([br_ref[:, j, :], bi_ref[:, j, :]], 0).astype(jnp.bfloat16)
        o_ref[:, j, :] = jnp.dot(m_ref[...], st, preferred_element_type=jnp.float32)


def fourier_latent_call(P_f, *, n_batch, seq):
    rows = seq // GRID_W
    bf16, f32 = jnp.bfloat16, jnp.float32
    c1, s1 = _dft_cs(rows)
    norm = 1.0 / math.sqrt(seq * FNET_GROUP_W)
    m_a = jnp.asarray(np.block([[c1, -s1], [s1, c1]]) * norm, bf16)
    c2, s2 = _dft_cs(GRID_W)
    m_c = jnp.asarray(np.concatenate([c2, -s2], 1), bf16)
    cs = jnp.asarray(_channel_dft(), bf16)
    phi = 2.0 * np.pi * np.outer(np.arange(rows), np.arange(GRID_W)) / seq
    cphi = jnp.repeat(jnp.asarray(np.cos(phi), f32), BRANCH_W, axis=1)
    sphi = jnp.repeat(jnp.asarray(np.sin(phi), f32), BRANCH_W, axis=1)
    xv = P_f[:n_batch * seq].reshape(n_batch * rows, GRID_W, BRANCH_W)
    nj = GRID_W // FNET_SUB
    full = lambda a: pl.BlockSpec(a.shape, lambda b, j: (0,) * a.ndim)
    tw = pl.BlockSpec((rows, FNET_SUB * BRANCH_W), lambda b, j: (0, j))
    mid = jax.ShapeDtypeStruct((n_batch * GRID_W, rows, BRANCH_W), f32)
    br, bi = pl.pallas_call(
        functools.partial(_fnet_a_kernel, n_rows=rows),
        grid=(n_batch, nj),
        in_specs=[pl.BlockSpec((rows, FNET_SUB, BRANCH_W), lambda b, j: (b, j, 0)), full(cs), full(m_a), tw, tw],
        out_specs=[pl.BlockSpec((FNET_SUB, rows, BRANCH_W), lambda b, j: (b * nj + j, 0, 0))] * 2,
        out_shape=[mid, mid],
        name="fnet_rows",
    )(xv, cs, m_a, cphi, sphi)
    blk = pl.BlockSpec((GRID_W, FNET_SUB, BRANCH_W), lambda b, i: (b, i, 0))
    y = pl.pallas_call(
        _fnet_c_kernel,
        grid=(n_batch, rows // FNET_SUB),
        in_specs=[blk, blk, pl.BlockSpec(m_c.shape, lambda b, i: (0, 0))],
        out_specs=blk,
        out_shape=mid,
        name="fnet_cols",
    )(br, bi, m_c)
    return y.reshape(n_batch * seq, BRANCH_W)


def _fnet_ctx_kernel(x_ref, cs_ref, m_ref, o_ref):
    u = jnp.dot(x_ref[...], cs_ref[...], preferred_element_type=jnp.float32)
    st = jnp.concatenate([u[:, :BRANCH_W], u[:, BRANCH_W:]], 0).astype(jnp.bfloat16)
    o_ref[...] = jnp.dot(m_ref[...], st, preferred_element_type=jnp.float32).astype(o_ref.dtype)


def fourier_ctx_call(P_f, *, n_batch, seq, ctx_len):
    bf16 = jnp.bfloat16
    c, s = _dft_cs(ctx_len)
    m = jnp.asarray(np.concatenate([c, -s], 1) / math.sqrt(ctx_len * FNET_GROUP_W), bf16)
    cs = jnp.asarray(_channel_dft(), bf16)
    cb = n_batch * seq // ctx_len
    return pl.pallas_call(
        _fnet_ctx_kernel,
        grid=(n_batch,),
        in_specs=[pl.BlockSpec((ctx_len, BRANCH_W), lambda b: (cb + b, 0)),
                  pl.BlockSpec(cs.shape, lambda b: (0, 0)), pl.BlockSpec(m.shape, lambda b: (0, 0))],
        out_specs=pl.BlockSpec((ctx_len, BRANCH_W), lambda b: (b, 0)),
        out_shape=jax.ShapeDtypeStruct((n_batch * ctx_len, BRANCH_W), jnp.float32),
        name="fnet_ctx",
    )(P_f, cs, m)


def kernel(x, c, ctx, c_ctx, w_mod, b_mod, norm1_g, norm2_g, w_in, na_qn_g, na_kn_g, na_rpb,
           da_qn_g, da_kn_g, da_lam_q1, da_lam_k1, da_lam_q2, da_lam_k2, da_subln_g, gdn_conv_w,
           gdn_a_log, gdn_dt_bias, gdn_onorm_g, w_branch, w_out, w_router, router_bias,
           moe_w_gate, moe_w_up, moe_w_down):
    B, S, D = x.shape
    L = ctx.shape[1]
    NL = B * S
    f32 = jnp.float32
    cvec = jnp.zeros((8, D), f32).at[:B].set(c).at[B].set(c_ctx)
    mods = mod_call(cvec, w_mod, b_mod)
    cos, sin = rope_tables(S, 256)
    bf16 = jnp.bfloat16
    X = jnp.concatenate([x.reshape(NL, D), ctx.reshape(B * L, D)], 0)
    for l in range(DEPTH):
        last = l == DEPTH - 1
        lam_init = 0.8 - 0.6 * math.exp(-0.3 * l)
        w_main = jnp.concatenate([w_in[l][:, :2560], w_in[l][:, 2576:]], 1).astype(jnp.bfloat16)
        w_ab = jnp.pad(w_in[l][:, 2560:2576], ((0, 0), (0, 112))).astype(jnp.bfloat16)
        P_na, P_da, P_gdn, P_z, P_f, P_gate, P_ab = inproj_call(X, mods[l], norm1_g[l], w_main, w_ab,
                                                                n_batch=B, seq=S)
        NAQ, NAK, DAQ, DAK, DAVX = prep_call(P_na, P_da, cos, sin, na_qn_g[l], na_kn_g[l], da_qn_g[l], da_kn_g[l],
                                             n_batch=B, seq=S)
        bias = na_bias_tables(na_rpb[l], S // GRID_W)
        lam = (jnp.exp(jnp.sum(da_lam_q1[l] * da_lam_k1[l])) - jnp.exp(jnp.sum(da_lam_q2[l] * da_lam_k2[l]))
               + lam_init).astype(f32)
        Ya = na_call(NAQ, NAK, P_na, bias, n_batch=B, seq=S, ctx_len=L)
        Yd = da_call(lam, DAQ, DAK, DAVX, da_subln_g[l], n_batch=B, seq=S, ctx_len=L, lam_init=lam_init)
        QE, OU, MM, NN = gdn_chunk_call(P_gdn, P_ab, gdn_conv_w[l], gdn_a_log[l], gdn_dt_bias[l], n_batch=B, seq=S)
        Of, Ob = gdn_scan_call(QE, OU, MM, NN, n_batch=B, seq=S)
        Yf = fourier_latent_call(P_f, n_batch=B, seq=S)
        n_tok = NL
        if not last:
            Yac, Ydc = ctx_attn_call(lam, NAQ, NAK, P_na, DAQ, DAK, DAVX, da_subln_g[l],
                                     n_batch=B, seq=S, ctx_len=L, lam_init=lam_init)
            Ya = jnp.concatenate([Ya, Yac], 0)
            Yd = jnp.concatenate([Yd, Ydc], 0)
            Yf = jnp.concatenate([Yf, fourier_ctx_call(P_f, n_batch=B, seq=S, ctx_len=L)], 0)
            n_tok = NL + B * L
        Xmid, H2, RW = merge_call(X, Ya, Yd, Of, Ob, P_z, Yf, P_gate, mods[l], norm2_g[l], gdn_onorm_g[l],
                                  w_branch[l].astype(bf16), w_out[l].astype(bf16), w_router, router_bias,
                                  n_tok=n_tok, n_batch=B, seq=S)
        experts = (moe_w_gate[l].astype(bf16), moe_w_up[l].astype(bf16), moe_w_down[l].astype(bf16))
        X = moe_sorted_call(Xmid, H2, RW, mods[l], *experts, row0=0, n_rows=NL, tile=MOE_TILE, win=MOE_WIN,
                            mod_row_fn=lambda i: i // (S // MOE_TILE))
        if not last:
            Xc = moe_sorted_call(Xmid, H2, RW, mods[l], *experts, row0=NL, n_rows=B * L, tile=B * L,
                                 win=MOE_WIN_CTX, mod_row_fn=lambda i: B)
            X = jnp.concatenate([X, Xc], 0)
    return X.reshape(B, S, D)
```

```python
import functools
import math
import jax
import jax.numpy as jnp
from jax import lax
import numpy as np
from jax.experimental import pallas as pl
from jax.experimental.pallas import tpu as pltpu

D_MODEL = 1024
DEPTH = 2
GRID_W = 64
N_BRANCH = 4
BRANCH_W = D_MODEL // N_BRANCH
GDN_HEADS = 4
GDN_DIM = BRANCH_W // GDN_HEADS
GDN_CHUNK = 64
FNET_GROUPS = 4
FNET_GROUP_W = BRANCH_W // FNET_GROUPS
N_EXPERTS = 16
N_GROUPS = 4
RMS_EPS = 1e-6
NEG_INF = -1e30
ROPE_BASE = 10000.0
VMEM_LIMIT = 56 * 1024 * 1024
HI = lax.Precision.HIGHEST
NT_DIMS = (((1,), (1,)), ((), ()))
LOG2E = 1.4426950408889634


def _mod_kernel(c_ref, w_ref, b_ref, o_ref):
    c = c_ref[...]
    a = c * jax.nn.sigmoid(c)
    o_ref[0] = jnp.dot(a, w_ref[0], preferred_element_type=jnp.float32, precision=HI) + b_ref[0]


def mod_call(cvec, w_mod, b_mod, tn=512):
    depth, D, N = w_mod.shape
    return pl.pallas_call(
        _mod_kernel,
        grid=(depth, N // tn),
        in_specs=[pl.BlockSpec((8, D), lambda l, j: (0, 0)),
                  pl.BlockSpec((1, D, tn), lambda l, j: (l, 0, j)),
                  pl.BlockSpec((1, 1, tn), lambda l, j: (l, 0, j))],
        out_specs=pl.BlockSpec((1, 8, tn), lambda l, j: (l, 0, j)),
        out_shape=jax.ShapeDtypeStruct((depth, 8, N), jnp.float32),
        name="mod",
    )(cvec, w_mod, b_mod.reshape(depth, 1, N))


def _stream_specs(n_lat_tiles, ctx_tile0, tm, width):
    return [pl.BlockSpec((tm, width), lambda i: (jnp.minimum(i, n_lat_tiles - 1), 0)),
            pl.BlockSpec((tm, width), lambda i: (ctx_tile0 + jnp.maximum(i - n_lat_tiles, 0), 0))]


def _inproj_kernel(xl_ref, xc_ref, m_ref, g_ref, w_ref, wab_ref, na_ref, da_ref, gdn_ref, z_ref, f_ref, gate_ref,
                   ab_ref, *, tiles_per_batch, n_lat_tiles, n_batch):
    i = pl.program_id(0)
    r = jnp.where(i < n_lat_tiles, i // tiles_per_batch, n_batch)
    x = jnp.where(i < n_lat_tiles, xl_ref[...], xc_ref[...])
    y = x * lax.rsqrt(jnp.mean(x * x, axis=-1, keepdims=True) + RMS_EPS) * g_ref[...]
    sh = m_ref[pl.ds(r, 1), 0:D_MODEL]
    sc = m_ref[pl.ds(r, 1), D_MODEL:2 * D_MODEL]
    h = (y * (1.0 + sc) + sh).astype(jnp.bfloat16)

    def seg(o_ref, a, b):
        o_ref[...] = jnp.dot(h, w_ref[:, a:b], preferred_element_type=jnp.float32).astype(o_ref.dtype)

    seg(na_ref, 0, 768)
    seg(da_ref, 768, 1536)
    seg(gdn_ref, 1536, 2304)
    seg(z_ref, 2304, 2560)
    seg(f_ref, 2560, 2816)
    for k in range(4):
        gate_ref[:, k * 1024:(k + 1) * 1024] = jnp.dot(
            h, w_ref[:, 2816 + k * 1024:2816 + (k + 1) * 1024],
            preferred_element_type=jnp.float32).astype(gate_ref.dtype)
    ab_ref[...] = jnp.dot(h, wab_ref[...], preferred_element_type=jnp.float32)


def inproj_call(Xl, Xc, ctx_row0, mod_l, g, w_main, w_ab, *, n_batch, seq, ctx_len, tm=256):
    D = Xl.shape[1]
    NT = n_batch * (seq + ctx_len)
    n_lat_tiles = n_batch * seq // tm
    kern = functools.partial(_inproj_kernel, tiles_per_batch=seq // tm, n_lat_tiles=n_lat_tiles, n_batch=n_batch)
    widths = [(768, jnp.bfloat16), (768, jnp.bfloat16), (768, jnp.float32), (256, jnp.bfloat16),
              (256, jnp.bfloat16), (4096, jnp.bfloat16), (128, jnp.float32)]
    return pl.pallas_call(
        kern,
        grid=(NT // tm,),
        in_specs=_stream_specs(n_lat_tiles, ctx_row0 // tm, tm, D) + [
            pl.BlockSpec(mod_l.shape, lambda i: (0, 0)),
            pl.BlockSpec((1, D), lambda i: (0, 0)),
            pl.BlockSpec(w_main.shape, lambda i: (0, 0)),
            pl.BlockSpec(w_ab.shape, lambda i: (0, 0))],
        out_specs=[pl.BlockSpec((tm, w), lambda i: (i, 0)) for w, _ in widths],
        out_shape=[jax.ShapeDtypeStruct((NT, w), dt) for w, dt in widths],
        compiler_params=pltpu.CompilerParams(vmem_limit_bytes=VMEM_LIMIT),
        name="inproj",
    )(Xl, Xc, mod_l, g.reshape(1, D), w_main, w_ab)


def _group_ones(width, group):
    i = np.arange(width)
    return jnp.asarray((i[:, None] // group == i[None, :] // group).astype(np.float32), jnp.bfloat16)


def _group_sum(xx, gmat):
    hi, lo = _split_bf16(xx)
    return (jnp.dot(hi, gmat, preferred_element_type=jnp.float32)
            + jnp.dot(lo, gmat, preferred_element_type=jnp.float32))


def _group_rms(x, gmat, group, gain):
    return x * lax.rsqrt(_group_sum(x * x, gmat) * (1.0 / group) + RMS_EPS) * gain


def rope_tables(seq, tm):
    nf = 8
    t = jnp.arange(seq)
    rows = (t // GRID_W).astype(jnp.float32)
    cols = (t % GRID_W).astype(jnp.float32)
    freqs = ROPE_BASE ** (-jnp.arange(nf, dtype=jnp.float32) / nf)
    d = np.arange(32)
    f_idx = d % 8
    use_col = (d // 16) == 1
    ang = jnp.where(use_col[None, :], cols[:, None], rows[:, None]) * freqs[f_idx][None, :]
    sign = np.where((d % 16) < 8, -1.0, 1.0).astype(np.float32)
    cos = jnp.tile(jnp.cos(ang), (1, 8))
    sin = jnp.tile(jnp.sin(ang) * sign[None, :], (1, 8))
    cos = jnp.concatenate([cos, jnp.ones((tm, 256), jnp.float32)], 0)
    sin = jnp.concatenate([sin, jnp.zeros((tm, 256), jnp.float32)], 0)
    return cos, sin


def _prep_kernel(na_ref, da_ref, cos_ref, sin_ref, g64_ref, g32_ref, gains_ref, naq_ref, nak_ref, daq_ref, dak_ref,
                 davx_ref):
    g64 = g64_ref[...]
    g32 = g32_ref[...]
    na = na_ref[...].astype(jnp.float32)
    naq_ref[...] = (_group_rms(na[:, 0:256], g64, 64, gains_ref[0:1, :]) * (64 ** -0.5)).astype(naq_ref.dtype)
    nak_ref[...] = _group_rms(na[:, 256:512], g64, 64, gains_ref[1:2, :]).astype(nak_ref.dtype)
    da = da_ref[...].astype(jnp.float32)
    cos = cos_ref[...]
    sin = sin_ref[...]
    lane = lax.broadcasted_iota(jnp.int32, (1, 256), 1)
    first = (lane % 16) < 8

    def rope(x):
        swapped = jnp.where(first, pltpu.roll(x, 256 - 8, 1), pltpu.roll(x, 8, 1))
        return x * cos + swapped * sin

    q = rope(_group_rms(da[:, 0:256], g32, 32, gains_ref[2:3, :]))
    k = rope(_group_rms(da[:, 256:512], g32, 32, gains_ref[3:4, :]))
    daq_ref[...] = (q * (32 ** -0.5 * LOG2E)).astype(daq_ref.dtype)
    dak_ref[...] = k.astype(dak_ref.dtype)
    v = da_ref[:, 512:768]
    one_col = (lax.broadcasted_iota(jnp.int32, (v.shape[0], 64), 1) == 0).astype(v.dtype)
    davx_ref[...] = jnp.concatenate(
        [piece for h in range(4) for piece in (v[:, h * 64:(h + 1) * 64], one_col)], -1)


def prep_call(P_na, P_da, cos, sin, na_qg, na_kg, da_qg, da_kg, *, n_batch, seq, tm=256):
    NT = P_na.shape[0]
    tpb = seq // tm
    n_lat = n_batch * tpb
    gains = jnp.stack([jnp.tile(na_qg, 4), jnp.tile(na_kg, 4), jnp.tile(da_qg, 8), jnp.tile(da_kg, 8)]
                      + [jnp.zeros((256,), jnp.float32)] * 4, 0)
    tab_map = lambda i: (jnp.where(i < n_lat, i % tpb, tpb), 0)
    full = lambda a: pl.BlockSpec(a.shape, lambda i: (0,) * a.ndim)
    g64, g32 = _group_ones(256, 64), _group_ones(256, 32)
    return pl.pallas_call(
        _prep_kernel,
        grid=(NT // tm,),
        in_specs=[pl.BlockSpec((tm, 768), lambda i: (i, 0)), pl.BlockSpec((tm, 768), lambda i: (i, 0)),
                  pl.BlockSpec((tm, 256), tab_map), pl.BlockSpec((tm, 256), tab_map),
                  full(g64), full(g32), full(gains)],
        out_specs=[pl.BlockSpec((tm, 256), lambda i: (i, 0))] * 4 + [pl.BlockSpec((tm, 512), lambda i: (i, 0))],
        out_shape=[jax.ShapeDtypeStruct((NT, 256), jnp.bfloat16)] * 4 + [jax.ShapeDtypeStruct((NT, 512), jnp.bfloat16)],
        name="attn_prep",
    )(P_na, P_da, cos, sin, g64, g32, gains)


NA_RB = 4
NA_UR = 11


def na_bias_tables(rpb, rows):
    reps = [(0, 0), (4, 0), (rows - NA_RB, rows - NA_UR)]
    qc = np.arange(GRID_W)[:, None]
    kc = np.arange(GRID_W)[None, :]
    ws = np.clip(qc - 8, 0, GRID_W - 16)
    vcol = (kc >= ws) & (kc < ws + 16)
    rel_c = np.clip(kc - qc + 15, 0, 30)
    sel_c = (rel_c[..., None] == np.arange(31)).astype(np.float32)
    sel_r, vrow = [], []
    for r0, u0 in reps:
        r = r0 + np.arange(NA_RB)[:, None]
        krow = u0 + np.arange(NA_UR)[None, :]
        start = np.clip(r - 4, 0, rows - 8)
        vrow.append((krow >= start) & (krow < start + 8))
        sel_r.append((np.clip(krow - r + 7, 0, 14)[..., None] == np.arange(15)).astype(np.float32))
    sel_r, vrow = np.stack(sel_r), np.stack(vrow)
    b = jnp.einsum('taui,hij,qkj->thaquk', jnp.asarray(sel_r), rpb.astype(jnp.float32), jnp.asarray(sel_c),
                   precision=HI)
    valid = vrow[:, None, :, None, :, None] & vcol[None, None, None, :, None, :]
    b = jnp.where(valid, b, NEG_INF)
    return b.reshape(3, rpb.shape[0], NA_RB * GRID_W, NA_UR * GRID_W).astype(jnp.bfloat16)


def _na_kernel(q_ref, k_ref, v_ref, kc_ref, vc_ref, bias_ref, o_ref, *, rows):
    i = pl.program_id(1)
    u0 = jnp.clip(i * NA_RB - 4, 0, rows - NA_UR)
    off = pl.multiple_of(u0 * GRID_W, GRID_W)
    nkw = NA_UR * GRID_W
    kwin = k_ref[pl.ds(off, nkw), :]
    vwin = v_ref[pl.ds(off, nkw), :]
    q = q_ref[...]
    kc = kc_ref[...]
    vc = vc_ref[...]
    outs = []
    for h in range(4):
        sl = slice(h * 64, (h + 1) * 64)
        qh = q[:, sl]
        s_loc = lax.dot_general(qh, kwin[:, sl], NT_DIMS, preferred_element_type=jnp.float32) + bias_ref[0, h].astype(jnp.float32)
        s_ctx = lax.dot_general(qh, kc[:, sl], NT_DIMS, preferred_element_type=jnp.float32)
        m = jnp.maximum(jnp.max(s_loc, -1, keepdims=True), jnp.max(s_ctx, -1, keepdims=True))
        p_loc = jnp.exp(s_loc - m)
        p_ctx = jnp.exp(s_ctx - m)
        l = jnp.sum(p_loc, -1, keepdims=True) + jnp.sum(p_ctx, -1, keepdims=True)
        o = (jnp.dot(p_loc.astype(jnp.bfloat16), vwin[:, sl], preferred_element_type=jnp.float32)
             + jnp.dot(p_ctx.astype(jnp.bfloat16), vc[:, sl], preferred_element_type=jnp.float32))
        outs.append(o / l)
    o_ref[...] = jnp.concatenate(outs, -1).astype(o_ref.dtype)


def na_call(NAQ, NAK, P_na, bias, *, n_batch, seq, ctx_len):
    rows = seq // GRID_W
    nblk = rows // NA_RB
    tq = NA_RB * GRID_W
    cb = n_batch * seq // ctx_len
    kern = functools.partial(_na_kernel, rows=rows)
    tsel = lambda b, i: (jnp.where(i == 0, 0, jnp.where(i == nblk - 1, 2, 1)), 0, 0, 0)
    return pl.pallas_call(
        kern,
        grid=(n_batch, nblk),
        in_specs=[pl.BlockSpec((tq, 256), lambda b, i: (b * nblk + i, 0)),
                  pl.BlockSpec((seq, 256), lambda b, i: (b, 0)),
                  pl.BlockSpec((seq, 256), lambda b, i: (b, 2)),
                  pl.BlockSpec((ctx_len, 256), lambda b, i: (cb + b, 0)),
                  pl.BlockSpec((ctx_len, 256), lambda b, i: (cb + b, 2)),
                  pl.BlockSpec((1,) + bias.shape[1:], tsel)],
        out_specs=pl.BlockSpec((tq, 256), lambda b, i: (b * nblk + i, 0)),
        out_shape=jax.ShapeDtypeStruct((n_batch * seq, 256), jnp.bfloat16),
        compiler_params=pltpu.CompilerParams(vmem_limit_bytes=VMEM_LIMIT),
        name="na_attn",
    )(NAQ, NAK, P_na, NAK, P_na, bias)


def _stack_masked_q(q):
    lane = lax.broadcasted_iota(jnp.int32, (1, 256), 1)
    return jnp.concatenate([jnp.where((lane // 32) == hm, q, jnp.zeros_like(q)) for hm in range(8)], 0)


def _da_finish(acc, lam, gain, g64, post_scale):
    tq = acc.shape[0] // 8
    norm = acc / jnp.broadcast_to(acc[:, 64:65], acc.shape)
    lane = lax.broadcasted_iota(jnp.int32, (1, 128), 1)
    halves = []
    for hp in range(2):
        d = [norm[(4 * hp + 2 * j) * tq:(4 * hp + 2 * j + 1) * tq]
             - lam * norm[(4 * hp + 2 * j + 1) * tq:(4 * hp + 2 * j + 2) * tq] for j in range(2)]
        halves.append(jnp.where(lane < 64, d[0], pltpu.roll(d[1], 64, 1)))
    o = jnp.concatenate(halves, -1)
    return _group_rms(o, g64, 64, gain) * post_scale


def _da_tile(qs, k, vx, m, acc, tq):
    tk = k.shape[0]
    s = lax.dot_general(qs, k, NT_DIMS, preferred_element_type=jnp.float32)
    m_new = jnp.maximum(m, jnp.max(s, -1, keepdims=True))
    alpha = jnp.exp2(m - m_new)
    p = jnp.exp2(s - pltpu.repeat(m_new, tk // 128, axis=1)).astype(jnp.bfloat16)
    pv = jnp.concatenate(
        [jnp.dot(p[2 * h * tq:(2 * h + 2) * tq], vx[:, h * 128:(h + 1) * 128], preferred_element_type=jnp.float32)
         for h in range(4)], 0)
    return m_new, alpha * acc + pv


def _da_kernel(lam_ref, q_ref, k_ref, vx_ref, kc_ref, vxc_ref, gain_ref, g64_ref, o_ref, m_ref, acc_ref,
               *, tk, n_kv, post_scale):
    tq = q_ref.shape[0]
    qs = _stack_masked_q(q_ref[...])
    m_ref[...] = jnp.full(m_ref.shape, -jnp.inf, jnp.float32)
    acc_ref[...] = jnp.zeros(acc_ref.shape, jnp.float32)

    def body(j, carry):
        for u in range(2):
            off = pl.multiple_of((2 * j + u) * tk, tk)
            m, acc = _da_tile(qs, k_ref[pl.ds(off, tk), :], vx_ref[pl.ds(off, tk), :], m_ref[...], acc_ref[...], tq)
            m_ref[...] = m
            acc_ref[...] = acc
        return carry

    lax.fori_loop(0, n_kv // 2 - 1, body, 0, unroll=True)
    off = (n_kv - 2) * tk
    m, acc = _da_tile(qs, k_ref[off:off + tk, :], vx_ref[off:off + tk, :], m_ref[...], acc_ref[...], tq)
    k_last = jnp.concatenate([k_ref[off + tk:off + 2 * tk, :], kc_ref[...]], 0)
    vx_last = jnp.concatenate([vx_ref[off + tk:off + 2 * tk, :], vxc_ref[...]], 0)
    m, acc = _da_tile(qs, k_last, vx_last, m, acc, tq)
    o_ref[...] = _da_finish(acc, lam_ref[0], gain_ref[...], g64_ref[...], post_scale).astype(o_ref.dtype)


def da_call(lam, DAQ, DAK, DAVX, subln_g, *, n_batch, seq, ctx_len, lam_init, tq=256, tk=1024):
    tk = min(tk, seq // 2)
    nq = seq // tq
    cb = n_batch * seq // ctx_len
    g64 = _group_ones(256, 64)
    gain = jnp.tile(subln_g, 4).reshape(1, 256)
    n_kv = seq // tk
    assert n_kv % 2 == 0
    kern = functools.partial(_da_kernel, tk=tk, n_kv=n_kv, post_scale=1.0 - lam_init)
    return pl.pallas_call(
        kern,
        grid=(n_batch, nq),
        in_specs=[pl.BlockSpec(memory_space=pltpu.SMEM),
                  pl.BlockSpec((tq, 256), lambda b, i: (b * nq + i, 0)),
                  pl.BlockSpec((seq, 256), lambda b, i: (b, 0)),
                  pl.BlockSpec((seq, 512), lambda b, i: (b, 0)),
                  pl.BlockSpec((ctx_len, 256), lambda b, i: (cb + b, 0)),
                  pl.BlockSpec((ctx_len, 512), lambda b, i: (cb + b, 0)),
                  pl.BlockSpec((1, 256), lambda b, i: (0, 0)),
                  pl.BlockSpec((256, 256), lambda b, i: (0, 0))],
        out_specs=pl.BlockSpec((tq, 256), lambda b, i: (b * nq + i, 0)),
        out_shape=jax.ShapeDtypeStruct((n_batch * seq, 256), jnp.bfloat16),
        scratch_shapes=[pltpu.VMEM((8 * tq, 128), jnp.float32), pltpu.VMEM((8 * tq, 128), jnp.float32)],
        compiler_params=pltpu.CompilerParams(vmem_limit_bytes=VMEM_LIMIT),
        name="da_attn",
    )(lam.reshape(1), DAQ, DAK, DAVX, DAK, DAVX, gain, g64)


def _ctx_kernel(lam_ref, naq_ref, nak_ref, nav_ref, daq_ref, dak_ref, dav_ref, gain_ref, g64_ref, ya_ref, yd_ref,
                *, post_scale):
    q = naq_ref[...]
    k = nak_ref[...]
    v = nav_ref[...]
    outs = []
    for h in range(4):
        sl = slice(h * 64, (h + 1) * 64)
        s = lax.dot_general(q[:, sl], k[:, sl], NT_DIMS, preferred_element_type=jnp.float32)
        p = jnp.exp(s - jnp.max(s, -1, keepdims=True))
        o = jnp.dot(p.astype(jnp.bfloat16), v[:, sl], preferred_element_type=jnp.float32)
        outs.append(o / jnp.sum(p, -1, keepdims=True))
    ya_ref[...] = jnp.concatenate(outs, -1).astype(ya_ref.dtype)
    tq = daq_ref.shape[0]
    qs = _stack_masked_q(daq_ref[...])
    m0 = jnp.full((8 * tq, 128), -jnp.inf, jnp.float32)
    m, acc = _da_tile(qs, dak_ref[...], dav_ref[...], m0, jnp.zeros((8 * tq, 128), jnp.float32), tq)
    yd_ref[...] = _da_finish(acc, lam_ref[0], gain_ref[...], g64_ref[...], post_scale).astype(yd_ref.dtype)


def ctx_attn_call(lam, NAQ, NAK, P_na, DAQ, DAK, DAVX, subln_g, *, n_batch, seq, ctx_len, lam_init):
    cb = n_batch * seq // ctx_len
    g64 = _group_ones(256, 64)
    gain = jnp.tile(subln_g, 4).reshape(1, 256)
    blk = lambda col: pl.BlockSpec((ctx_len, 256), lambda b: (cb + b, col))
    return pl.pallas_call(
        functools.partial(_ctx_kernel, post_scale=1.0 - lam_init),
        grid=(n_batch,),
        in_specs=[pl.BlockSpec(memory_space=pltpu.SMEM), blk(0), blk(0), blk(2), blk(0), blk(0),
                  pl.BlockSpec((ctx_len, 512), lambda b: (cb + b, 0)),
                  pl.BlockSpec((1, 256), lambda b: (0, 0)), pl.BlockSpec((256, 256), lambda b: (0, 0))],
        out_specs=[pl.BlockSpec((ctx_len, 256), lambda b: (b, 0))] * 2,
        out_shape=[jax.ShapeDtypeStruct((n_batch * ctx_len, 256), jnp.bfloat16)] * 2,
        name="ctx_attn",
    )(lam.reshape(1), NAQ, NAK, P_na, DAQ, DAK, DAVX, gain, g64)


def _split_bf16(a):
    hi = a.astype(jnp.bfloat16)
    return hi, (a - hi.astype(jnp.float32)).astype(jnp.bfloat16)


def _route(logits_t, bias_col):
    per_group = N_EXPERTS // N_GROUPS
    scores = jax.nn.sigmoid(logits_t)
    sel_all = scores + bias_col
    sel = [sel_all[e:e + 1, :] for e in range(N_EXPERTS)]
    top2 = []
    for e in range(N_EXPERTS):
        g0 = e // per_group * per_group
        rank = jnp.zeros_like(sel[e])
        for o in range(g0, g0 + per_group):
            if o != e:
                beats = (sel[o] > sel[e]) | (sel[o] == sel[e]) if o < e else (sel[o] > sel[e])
                rank = rank + beats.astype(jnp.float32)
        top2.append(rank < 1.5)
    gs = []
    for g in range(N_GROUPS):
        acc = jnp.zeros_like(sel[0])
        for e in range(g * per_group, (g + 1) * per_group):
            acc = acc + jnp.where(top2[e], sel[e], 0.0)
        gs.append(acc)
    rows = []
    for g in range(N_GROUPS):
        beaten = jnp.zeros_like(sel[0])
        for o in range(N_GROUPS):
            if o != g:
                b = (gs[o] >= gs[g]) if o < g else (gs[o] > gs[g])
                beaten = beaten + b.astype(jnp.float32)
        best = beaten < 0.5
        for e in range(g * per_group, (g + 1) * per_group):
            rows.append(jnp.where(best & top2[e], scores[e:e + 1, :], 0.0))
    w = jnp.concatenate(rows, 0)
    return w / jnp.sum(w, axis=0, keepdims=True)


def _merge_kernel(xl_ref, xc_ref, ya_ref, yd_ref, of_ref, ob_ref, z_ref, yf_ref, gate_ref, m_ref, g2_ref, og_ref, g64_ref,
                  wb_ref, wo_ref, wr_ref, rb_ref, xo_ref, h2_ref, rw_ref, *, tiles_per_batch, n_lat_tiles, n_batch):
    i = pl.program_id(0)
    r = jnp.where(i < n_lat_tiles, i // tiles_per_batch, n_batch)
    z = z_ref[...].astype(jnp.float32)
    yg = (_group_rms(of_ref[...] + ob_ref[...], g64_ref[...], GDN_DIM, og_ref[...]) * (z * jax.nn.sigmoid(z)))
    branches = (ya_ref[...], yd_ref[...], yg.astype(jnp.bfloat16), yf_ref[...].astype(jnp.bfloat16))
    mix = None
    for b, yb in enumerate(branches):
        proj = jnp.dot(yb, wb_ref[b], preferred_element_type=jnp.float32)
        term = jax.nn.sigmoid(gate_ref[:, b * D_MODEL:(b + 1) * D_MODEL].astype(jnp.float32)) * proj
        mix = term if mix is None else mix + term
    y = jnp.dot(mix.astype(jnp.bfloat16), wo_ref[...], preferred_element_type=jnp.float32)
    mrow = lambda k: m_ref[pl.ds(r, 1), k * D_MODEL:(k + 1) * D_MODEL]
    x = jnp.where(i < n_lat_tiles, xl_ref[...], xc_ref[...]) + mrow(2) * y
    xo_ref[...] = x
    h2 = (x * lax.rsqrt(jnp.mean(x * x, axis=-1, keepdims=True) + RMS_EPS) * g2_ref[...]) * (1.0 + mrow(4)) + mrow(3)
    h2_ref[...] = h2.astype(h2_ref.dtype)
    h_hi, h_lo = _split_bf16(h2)
    w_hi, w_lo = _split_bf16(wr_ref[...])
    nt = lambda a, b: lax.dot_general(a, b, NT_DIMS, preferred_element_type=jnp.float32)
    logits_t = nt(w_hi, h_hi) + nt(w_hi, h_lo) + nt(w_lo, h_hi)
    w_t = _route(logits_t, rb_ref[...])
    pad = jnp.zeros((128 - N_EXPERTS, w_t.shape[1]), jnp.float32)
    rw_ref[...] = jnp.transpose(jnp.concatenate([w_t, pad], 0))


def merge_call(Xl, Xc, ctx_row0, Ya, Yd, Of, Ob, P_z, Yf, P_gate, mod_l, g2n, onorm_g, w_branch, w_out, w_router,
               router_bias, *, n_tok, n_batch, seq, tm=256):
    D = Xl.shape[1]
    n_lat_tiles = n_batch * seq // tm
    kern = functools.partial(_merge_kernel, tiles_per_batch=seq // tm, n_lat_tiles=n_lat_tiles, n_batch=n_batch)
    row = lambda w: pl.BlockSpec((tm, w), lambda i: (i, 0))
    full = lambda a: pl.BlockSpec(a.shape, lambda i: (0,) * a.ndim)
    wr_t = jnp.transpose(w_router)
    rb = router_bias.reshape(N_EXPERTS, 1)
    g2 = g2n.reshape(1, D)
    og = jnp.tile(onorm_g, GDN_HEADS).reshape(1, BRANCH_W)
    g64 = _group_ones(BRANCH_W, GDN_DIM)
    return pl.pallas_call(
        kern,
        grid=(n_tok // tm,),
        in_specs=_stream_specs(n_lat_tiles, ctx_row0 // tm, tm, D) + [
                  row(256), row(256), row(256), row(256), row(256), row(256), row(4 * D), full(mod_l),
                  full(g2), full(og), full(g64), full(w_branch), full(w_out), full(wr_t), full(rb)],
        out_specs=[row(D), row(D), row(128)],
        out_shape=[jax.ShapeDtypeStruct((n_tok, D), jnp.float32), jax.ShapeDtypeStruct((n_tok, D), jnp.bfloat16),
                   jax.ShapeDtypeStruct((n_tok, 128), jnp.float32)],
        compiler_params=pltpu.CompilerParams(vmem_limit_bytes=VMEM_LIMIT),
        name="merge",
    )(Xl, Xc, Ya, Yd, Of, Ob, P_z, Yf, P_gate, mod_l, g2, og, g64, w_branch, w_out, wr_t, rb)


MOE_TILE = 1024
MOE_WIN = 320
MOE_WIN_CTX = 176


def _split3_bf16(a):
    hi = a.astype(jnp.bfloat16)
    r = a - hi.astype(jnp.float32)
    mid = r.astype(jnp.bfloat16)
    return hi, mid, (r - mid.astype(jnp.float32)).astype(jnp.bfloat16)


def _moe_sorted_kernel(x_ref, h_ref, rw_ref, m_ref, wg_ref, wu_ref, wd_ref, o_ref, xs_ref, ws_ref, ys_ref, dest_ref,
                       seg_ref, *, win, mod_row_fn):
    T = h_ref.shape[0]
    per_group = N_EXPERTS // N_GROUPS
    i = pl.program_id(0)
    e = pl.program_id(1)
    lane = lax.broadcasted_iota(jnp.int32, (1, 128), 1)
    f32, bf16 = jnp.float32, jnp.bfloat16

    @pl.when(e == 0)
    def _():
        rw = rw_ref[...]
        r128 = lax.broadcasted_iota(jnp.int32, (128, 128), 0)
        c128 = lax.broadcasted_iota(jnp.int32, (128, 128), 1)
        e2g = ((r128 // per_group == c128) & (r128 < N_EXPERTS)).astype(bf16)
        og = (jnp.dot((rw > 0.0).astype(bf16), e2g, preferred_element_type=f32) > 0.5).astype(f32)
        rt = lax.broadcasted_iota(jnp.int32, (T, T), 0)
        ct = lax.broadcasted_iota(jnp.int32, (T, T), 1)
        earlier = jnp.dot((rt > ct).astype(bf16), og.astype(bf16), preferred_element_type=f32)
        cnt_row = jnp.sum(og, axis=0, keepdims=True)
        start_row = jnp.dot(jnp.broadcast_to(cnt_row, (8, 128)), (r128 < c128).astype(f32),
                            preferred_element_type=f32, precision=HI)[0:1]
        dest = jnp.sum(og * (earlier + start_row), axis=-1, keepdims=True)
        dest_ref[...] = jnp.broadcast_to(dest, (T, 128))
        dest_row = jnp.transpose(dest_ref[...])[0:1, :]
        perm = (dest_row == rt.astype(f32)).astype(bf16)
        xs_ref[...] = jnp.dot(perm, h_ref[...], preferred_element_type=f32).astype(bf16)
        ws_ref[...] = sum(jnp.dot(perm, piece, preferred_element_type=f32) for piece in _split3_bf16(rw))
        ys_ref[...] = jnp.zeros(ys_ref.shape, f32)
        for g in range(N_GROUPS):
            seg_ref[g] = jnp.sum(jnp.where(lane == g, start_row, 0.0)).astype(jnp.int32)
            seg_ref[N_GROUPS + g] = jnp.sum(jnp.where(lane == g, cnt_row, 0.0)).astype(jnp.int32)

    g = e // per_group
    start = seg_ref[g]
    end = start + seg_ref[N_GROUPS + g]
    s0 = (start // 16) * 16
    n_win = (end - s0 + win - 1) // win

    def window(w, carry):
        lo = s0 + w * win
        off = pl.multiple_of(jnp.minimum(lo, T - win), 16)
        xw = xs_ref[pl.ds(off, win), :]
        w_col = jnp.sum(jnp.where(lane == e, ws_ref[pl.ds(off, win), :], 0.0), axis=-1, keepdims=True)
        row = off + lax.broadcasted_iota(jnp.int32, (win, 1), 0)
        w_col = jnp.where(row >= lo, w_col, 0.0)
        a = jnp.dot(xw, wg_ref[0], preferred_element_type=f32)
        u = jnp.dot(xw, wu_ref[0], preferred_element_type=f32)
        act = (a * jax.nn.sigmoid(a) * u * w_col).astype(bf16)
        ys_ref[pl.ds(off, win), :] += jnp.dot(act, wd_ref[0], preferred_element_type=f32)
        return carry

    lax.fori_loop(0, n_win, window, 0)

    @pl.when(e == N_EXPERTS - 1)
    def _():
        ct = lax.broadcasted_iota(jnp.int32, (T, T), 1)
        unperm = (pltpu.repeat(dest_ref[...], T // 128, axis=1) == ct.astype(f32)).astype(bf16)
        y_hi, y_lo = _split_bf16(ys_ref[...])
        f = (jnp.dot(unperm, y_hi, preferred_element_type=f32) + jnp.dot(unperm, y_lo, preferred_element_type=f32))
        g2 = m_ref[pl.ds(mod_row_fn(i), 1), 5 * D_MODEL:6 * D_MODEL]
        o_ref[...] = x_ref[...] + g2 * f


def moe_sorted_call(Xmid, H2, RW, mod_l, w_gate, w_up, w_down, *, row0, n_rows, tile, win, mod_row_fn):
    D = Xmid.shape[1]
    F = w_gate.shape[-1]
    b0 = row0 // tile
    tok = lambda w: pl.BlockSpec((tile, w), lambda i, e: (b0 + i, 0))
    kern = functools.partial(_moe_sorted_kernel, win=win, mod_row_fn=mod_row_fn)
    return pl.pallas_call(
        kern,
        grid=(n_rows // tile, N_EXPERTS),
        in_specs=[tok(D), tok(D), tok(128),
                  pl.BlockSpec(mod_l.shape, lambda i, e: (0, 0)),
                  pl.BlockSpec((1, D, F), lambda i, e: (e, 0, 0)),
                  pl.BlockSpec((1, D, F), lambda i, e: (e, 0, 0)),
                  pl.BlockSpec((1, F, D), lambda i, e: (e, 0, 0))],
        out_specs=pl.BlockSpec((tile, D), lambda i, e: (i, 0)),
        out_shape=jax.ShapeDtypeStruct((n_rows, D), jnp.float32),
        scratch_shapes=[pltpu.VMEM((tile, D), jnp.bfloat16), pltpu.VMEM((tile, 128), jnp.float32),
                        pltpu.VMEM((tile, D), jnp.float32), pltpu.VMEM((tile, 128), jnp.float32),
                        pltpu.SMEM((2 * N_GROUPS,), jnp.int32)],
        compiler_params=pltpu.CompilerParams(vmem_limit_bytes=VMEM_LIMIT),
        name="moe_sorted",
    )(Xmid, H2, RW, mod_l, w_gate, w_up, w_down)


GDN_TILE = 256
GDN_CPT = GDN_TILE // GDN_CHUNK
GDN_LOCKSTEP = 2


def _mm(a, b, passes=1, dims=None):
    if dims is None:
        dot = lambda x, y: jnp.dot(x, y, preferred_element_type=jnp.float32)
    else:
        dot = lambda x, y: lax.dot_general(x, y, dims, preferred_element_type=jnp.float32)
    if passes == 1:
        return dot(a.astype(jnp.bfloat16), b.astype(jnp.bfloat16))
    a_hi, a_lo = _split_bf16(a)
    b_hi, b_lo = _split_bf16(b)
    return dot(a_hi, b_hi) + dot(a_hi, b_lo) + dot(a_lo, b_hi)


def _stack_heads(x):
    lane = lax.broadcasted_iota(jnp.int32, (1, 256), 1)
    return jnp.concatenate([jnp.where((lane // GDN_DIM) == h, x, 0.0) for h in range(GDN_HEADS)], 0)


def _slab(x):
    return x[0:64] + x[64:128] + x[128:192] + x[192:256]


def _unit_tri_inverse(mats):
    shape = mats[0].shape
    eye = (lax.broadcasted_iota(jnp.int32, shape, 0) == lax.broadcasted_iota(jnp.int32, shape, 1)).astype(jnp.float32)
    ps = [eye - a for a in mats]
    pws = list(mats)
    for _ in range(5):
        pws = [_mm(pw, pw, 3) for pw in pws]
        ps = [p + _mm(p, pw, 3) for p, pw in zip(ps, pws)]
    return ps


def _gdn_chunk_kernel(x_ref, xp_ref, xn_ref, ab_ref, cw_ref, par_ref, g64_ref, qe_ref, ou_ref, mm_ref, nn_ref,
                      *, tiles_per_batch, n_lat_tiles):
    i = pl.program_id(0)
    is_lat = i < n_lat_tiles
    first = jnp.where(is_lat, (i % tiles_per_batch) == 0, True)
    last = jnp.where(is_lat, (i % tiles_per_batch) == tiles_per_batch - 1, True)
    xp = jnp.concatenate([jnp.where(first, 0.0, xp_ref[...]), x_ref[...], jnp.where(last, 0.0, xn_ref[...])], 0)
    y = xp[6:6 + GDN_TILE] * cw_ref[0:1, :]
    for t in range(1, 5):
        y = y + xp[6 + t:6 + t + GDN_TILE] * cw_ref[t:t + 1, :]
    y = y * jax.nn.sigmoid(y)
    g64 = g64_ref[...]
    q = y[:, 0:256]
    k = y[:, 256:512]
    v = y[:, 512:768]
    q = q * lax.rsqrt(_group_sum(q * q, g64) + RMS_EPS) * (GDN_DIM ** -0.5)
    k = k * lax.rsqrt(_group_sum(k * k, g64) + RMS_EPS)
    ab = ab_ref[...]
    lane128 = lax.broadcasted_iota(jnp.int32, (1, 128), 1)
    g_all = jnp.where(lane128 < 8, par_ref[0:1, :] * jax.nn.softplus(ab + par_ref[1:2, :]), 0.0)
    beta_all = jax.nn.sigmoid(ab)
    r64 = lax.broadcasted_iota(jnp.int32, (64, 64), 0)
    c64 = lax.broadcasted_iota(jnp.int32, (64, 64), 1)
    low = (r64 >= c64).astype(jnp.float32)
    upp = (r64 <= c64).astype(jnp.float32)
    rr = lax.broadcasted_iota(jnp.int32, (256, 256), 0)
    cc = lax.broadcasted_iota(jnp.int32, (256, 256), 1)
    same = (rr // 64) == (cc // 64)
    eye = (rr == cc).astype(jnp.float32)
    for c0 in range(0, GDN_CPT, GDN_LOCKSTEP):
        stage = []
        for c in range(c0, c0 + GDN_LOCKSTEP):
            sl = slice(c * GDN_CHUNK, (c + 1) * GDN_CHUNK)
            g_c = g_all[sl]
            gsum = jnp.where(lane128 < 4, jnp.dot(low, g_c, preferred_element_type=jnp.float32, precision=HI),
                             jnp.dot(upp, g_c, preferred_element_type=jnp.float32, precision=HI))
            gsum_t = jnp.transpose(jnp.concatenate([gsum, jnp.zeros_like(gsum)], 0))[:, 0:64]
            tot = jnp.sum(g_c, axis=0, keepdims=True)
            beta_c = beta_all[sl]
            q_sm = _stack_heads(q[sl])
            k_sm = _stack_heads(k[sl])
            v_sm = _stack_heads(v[sl])
            kk = _mm(k_sm, k_sm, 1, NT_DIMS)
            qk = _mm(q_sm, k_sm, 1, NT_DIMS)
            for d in range(2):
                col = lambda h: 4 * d + h
                cmat = jnp.concatenate([jnp.broadcast_to(gsum[:, col(h):col(h) + 1], (64, 256)) for h in range(4)], 0)
                rrow = jnp.concatenate([gsum_t[col(h):col(h) + 1, :] for h in range(4)], 1)
                bmat = jnp.concatenate([jnp.broadcast_to(beta_c[:, 8 + col(h):9 + col(h)], (64, 256))
                                        for h in range(4)], 0)
                tmat = jnp.concatenate([jnp.broadcast_to(tot[:, col(h):col(h) + 1], (64, 256)) for h in range(4)], 0)
                tri = (rr % 64 >= cc % 64) if d == 0 else (rr % 64 <= cc % 64)
                incl = same & tri
                strict = incl & (rr != cc)
                dec = jnp.where(incl, jnp.exp(jnp.where(incl, cmat - rrow, 0.0)), 0.0)
                stage.append((d, sl, q_sm, k_sm, v_sm, qk, cmat, bmat, tmat, dec,
                              jnp.where(strict, kk * bmat * dec, 0.0)))
        t_invs = _unit_tri_inverse([st[-1] for st in stage])
        n = len(stage)
        q_sms, k_sms, v_sms, qks, cmats, bmats, tmats, decs = (
            [st[f] for st in stage] for f in range(2, 10))
        egs = [jnp.exp(cm) for cm in cmats]
        us = [_mm(t_invs[j], v_sms[j] * bmats[j], 1) for j in range(n)]
        ws = [_mm(t_invs[j], k_sms[j] * bmats[j] * egs[j], 1) for j in range(n)]
        qkms = [decs[j] * qks[j] for j in range(n)]
        qes = [q_sms[j] * egs[j] - _mm(qkms[j], ws[j], 1) for j in range(n)]
        ous = [_mm(qkms[j], us[j], 1) for j in range(n)]
        kd_ts = [jnp.transpose(k_sms[j] * jnp.exp(tmats[j] - cmats[j])) for j in range(n)]
        m_bds = [eye * jnp.exp(tmats[j]) - _mm(kd_ts[j], ws[j], 1) for j in range(n)]
        n_bds = [_mm(kd_ts[j], us[j], 1) for j in range(n)]
        for j in range(n):
            d, sl = stage[j][0], stage[j][1]
            qe_ref[d, sl, :] = _slab(qes[j])
            ou_ref[d, sl, :] = _slab(ous[j])
            mm_ref[d, sl, :] = _slab(m_bds[j])
            nn_ref[d, sl, :] = _slab(n_bds[j])


def gdn_chunk_call(P_gdn, P_ab, conv_w, a_log, dt_bias, *, n_batch, seq):
    NT = P_gdn.shape[0]
    tpb = seq // GDN_TILE
    n_lat = n_batch * tpb
    n_tiles = NT // GDN_TILE
    hb = GDN_TILE // 8
    cw = jnp.zeros((8, 768), jnp.float32).at[:5].set(conv_w)
    par = jnp.zeros((8, 128), jnp.float32)
    par = par.at[0, :8].set(-jnp.exp(a_log.reshape(8))).at[1, :8].set(dt_bias.reshape(8))
    g64 = _group_ones(256, 64)
    kern = functools.partial(_gdn_chunk_kernel, tiles_per_batch=tpb, n_lat_tiles=n_lat)
    full = lambda a: pl.BlockSpec(a.shape, lambda i: (0,) * a.ndim)
    out = pl.BlockSpec((2, GDN_TILE, 256), lambda i: (0, i, 0))
    return pl.pallas_call(
        kern,
        grid=(n_tiles,),
        in_specs=[pl.BlockSpec((GDN_TILE, 768), lambda i: (i, 0)),
                  pl.BlockSpec((8, 768), lambda i: (jnp.maximum(i * hb - 1, 0), 0)),
                  pl.BlockSpec((8, 768), lambda i: (jnp.minimum((i + 1) * hb, n_tiles * hb - 1), 0)),
                  pl.BlockSpec((GDN_TILE, 128), lambda i: (i, 0)),
                  full(cw), full(par), full(g64)],
        out_specs=[out] * 4,
        out_shape=[jax.ShapeDtypeStruct((2, NT, 256), jnp.float32)] * 4,
        compiler_params=pltpu.CompilerParams(vmem_limit_bytes=VMEM_LIMIT),
        name="gdn_chunk",
    )(P_gdn, P_gdn, P_gdn, P_ab, cw, par, g64)


def _gdn_scan_kernel(qef_ref, ouf_ref, mf_ref, nf_ref, qeb_ref, oub_ref, mb_ref, nb_ref, of_ref, ob_ref, s_ref):
    t = pl.program_id(1)

    @pl.when(t == 0)
    def _():
        s_ref[...] = jnp.zeros(s_ref.shape, jnp.float32)

    def step(d, c, qe_ref, ou_ref, m_ref, n_ref, o_ref):
        sl = slice(c * GDN_CHUNK, (c + 1) * GDN_CHUNK)
        s = s_ref[d]
        o_ref[sl, :] = _slab(_mm(_stack_heads(qe_ref[0, sl, :]), s, 3)) + ou_ref[0, sl, :]
        s_ref[d] = _mm(_stack_heads(m_ref[0, sl, :]), s, 3) + _stack_heads(n_ref[0, sl, :])

    for c in range(GDN_CPT):
        step(0, c, qef_ref, ouf_ref, mf_ref, nf_ref, of_ref)
        step(1, GDN_CPT - 1 - c, qeb_ref, oub_ref, mb_ref, nb_ref, ob_ref)


def gdn_scan_call(QE, OU, MM, NN, *, n_batch, seq):
    NT = QE.shape[1]
    tpb = seq // GDN_TILE
    cblk = n_batch * tpb
    fwd = lambda b, t: jnp.where(t == 0, cblk + b, b * tpb + t - 1)
    bwd = lambda b, t: jnp.where(t == 0, cblk + b, b * tpb + tpb - t)
    spec = lambda d, f: pl.BlockSpec((1, GDN_TILE, 256), lambda b, t: (d, f(b, t), 0))
    ospec = lambda f: pl.BlockSpec((GDN_TILE, 256), lambda b, t: (f(b, t), 0))
    return pl.pallas_call(
        _gdn_scan_kernel,
        grid=(n_batch, tpb + 1),
        in_specs=[spec(0, fwd)] * 4 + [spec(1, bwd)] * 4,
        out_specs=[ospec(fwd), ospec(bwd)],
        out_shape=[jax.ShapeDtypeStruct((NT, 256), jnp.float32)] * 2,
        scratch_shapes=[pltpu.VMEM((2, 256, 256), jnp.float32)],
        compiler_params=pltpu.CompilerParams(vmem_limit_bytes=VMEM_LIMIT),
        name="gdn_scan",
    )(QE, OU, MM, NN, QE, OU, MM, NN)


def _dft_cs(n):
    a = 2.0 * np.pi * np.outer(np.arange(n), np.arange(n)) / n
    return np.cos(a), np.sin(a)


def _channel_dft():
    c, s = _dft_cs(FNET_GROUP_W)
    eye = np.eye(FNET_GROUPS)
    return np.concatenate([np.kron(eye, c), np.kron(eye, s)], 1)


FNET_SUB = 16


def _fnet_a_kernel(x_ref, cs_ref, m_ref, cphi_ref, sphi_ref, br_ref, bi_ref, *, n_rows):
    for j in range(FNET_SUB):
        sl = slice(j * BRANCH_W, (j + 1) * BRANCH_W)
        u = jnp.dot(x_ref[:, j, :], cs_ref[...], preferred_element_type=jnp.float32)
        st = jnp.concatenate([u[:, :BRANCH_W], u[:, BRANCH_W:]], 0).astype(jnp.bfloat16)
        a = jnp.dot(m_ref[...], st, preferred_element_type=jnp.float32)
        ar, ai = a[:n_rows], a[n_rows:]
        cp, sp = cphi_ref[:, sl], sphi_ref[:, sl]
        br_ref[j] = ar * cp - ai * sp
        bi_ref[j] = ar * sp + ai * cp


def _fnet_c_kernel(br_ref, bi_ref, m_ref, o_ref):
    for j in range(FNET_SUB):
        st = jnp.concatenate([br_ref[:, j, :], bi_ref[:, j, :]], 0).astype(jnp.bfloat16)
        o_ref[:, j, :] = jnp.dot(m_ref[...], st, preferred_element_type=jnp.float32)


def fourier_latent_call(P_f, *, n_batch, seq):
    rows = seq // GRID_W
    bf16, f32 = jnp.bfloat16, jnp.float32
    c1, s1 = _dft_cs(rows)
    norm = 1.0 / math.sqrt(seq * FNET_GROUP_W)
    m_a = jnp.asarray(np.block([[c1, -s1], [s1, c1]]) * norm, bf16)
    c2, s2 = _dft_cs(GRID_W)
    m_c = jnp.asarray(np.concatenate([c2, -s2], 1), bf16)
    cs = jnp.asarray(_channel_dft(), bf16)
    phi = 2.0 * np.pi * np.outer(np.arange(rows), np.arange(GRID_W)) / seq
    cphi = jnp.repeat(jnp.asarray(np.cos(phi), f32), BRANCH_W, axis=1)
    sphi = jnp.repeat(jnp.asarray(np.sin(phi), f32), BRANCH_W, axis=1)
    xv = P_f[:n_batch * seq].reshape(n_batch * rows, GRID_W, BRANCH_W)
    nj = GRID_W // FNET_SUB
    full = lambda a: pl.BlockSpec(a.shape, lambda b, j: (0,) * a.ndim)
    tw = pl.BlockSpec((rows, FNET_SUB * BRANCH_W), lambda b, j: (0, j))
    mid = jax.ShapeDtypeStruct((n_batch * GRID_W, rows, BRANCH_W), f32)
    br, bi = pl.pallas_call(
        functools.partial(_fnet_a_kernel, n_rows=rows),
        grid=(n_batch, nj),
        in_specs=[pl.BlockSpec((rows, FNET_SUB, BRANCH_W), lambda b, j: (b, j, 0)), full(cs), full(m_a), tw, tw],
        out_specs=[pl.BlockSpec((FNET_SUB, rows, BRANCH_W), lambda b, j: (b * nj + j, 0, 0))] * 2,
        out_shape=[mid, mid],
        name="fnet_rows",
    )(xv, cs, m_a, cphi, sphi)
    blk = pl.BlockSpec((GRID_W, FNET_SUB, BRANCH_W), lambda b, i: (b, i, 0))
    y = pl.pallas_call(
        _fnet_c_kernel,
        grid=(n_batch, rows // FNET_SUB),
        in_specs=[blk, blk, pl.BlockSpec(m_c.shape, lambda b, i: (0, 0))],
        out_specs=blk,
        out_shape=mid,
        name="fnet_cols",
    )(br, bi, m_c)
    return y.reshape(n_batch * seq, BRANCH_W)


def _fnet_ctx_kernel(x_ref, cs_ref, m_ref, o_ref):
    u = jnp.dot(x_ref[...], cs_ref[...], preferred_element_type=jnp.float32)
    st = jnp.concatenate([u[:, :BRANCH_W], u[:, BRANCH_W:]], 0).astype(jnp.bfloat16)
    o_ref[...] = jnp.dot(m_ref[...], st, preferred_element_type=jnp.float32).astype(o_ref.dtype)


def fourier_ctx_call(P_f, *, n_batch, seq, ctx_len):
    bf16 = jnp.bfloat16
    c, s = _dft_cs(ctx_len)
    m = jnp.asarray(np.concatenate([c, -s], 1) / math.sqrt(ctx_len * FNET_GROUP_W), bf16)
    cs = jnp.asarray(_channel_dft(), bf16)
    cb = n_batch * seq // ctx_len
    return pl.pallas_call(
        _fnet_ctx_kernel,
        grid=(n_batch,),
        in_specs=[pl.BlockSpec((ctx_len, BRANCH_W), lambda b: (cb + b, 0)),
                  pl.BlockSpec(cs.shape, lambda b: (0, 0)), pl.BlockSpec(m.shape, lambda b: (0, 0))],
        out_specs=pl.BlockSpec((ctx_len, BRANCH_W), lambda b: (b, 0)),
        out_shape=jax.ShapeDtypeStruct((n_batch * ctx_len, BRANCH_W), jnp.float32),
        name="fnet_ctx",
    )(P_f, cs, m)


def kernel(x, c, ctx, c_ctx, w_mod, b_mod, norm1_g, norm2_g, w_in, na_qn_g, na_kn_g, na_rpb,
           da_qn_g, da_kn_g, da_lam_q1, da_lam_k1, da_lam_q2, da_lam_k2, da_subln_g, gdn_conv_w,
           gdn_a_log, gdn_dt_bias, gdn_onorm_g, w_branch, w_out, w_router, router_bias,
           moe_w_gate, moe_w_up, moe_w_down):
    B, S, D = x.shape
    L = ctx.shape[1]
    NL = B * S
    f32 = jnp.float32
    cvec = jnp.zeros((8, D), f32).at[:B].set(c).at[B].set(c_ctx)
    mods = mod_call(cvec, w_mod, b_mod)
    cos, sin = rope_tables(S, 256)
    bf16 = jnp.bfloat16
    Xl, Xc = x.reshape(NL, D), ctx.reshape(B * L, D)
    for l in range(DEPTH):
        last = l == DEPTH - 1
        lam_init = 0.8 - 0.6 * math.exp(-0.3 * l)
        w_main = jnp.concatenate([w_in[l][:, :2560], w_in[l][:, 2576:]], 1).astype(jnp.bfloat16)
        w_ab = jnp.pad(w_in[l][:, 2560:2576], ((0, 0), (0, 112))).astype(jnp.bfloat16)
        P_na, P_da, P_gdn, P_z, P_f, P_gate, P_ab = inproj_call(Xl, Xc, 0, mods[l], norm1_g[l], w_main, w_ab,
                                                                n_batch=B, seq=S, ctx_len=L)
        NAQ, NAK, DAQ, DAK, DAVX = prep_call(P_na, P_da, cos, sin, na_qn_g[l], na_kn_g[l], da_qn_g[l], da_kn_g[l],
                                             n_batch=B, seq=S)
        bias = na_bias_tables(na_rpb[l], S // GRID_W)
        lam = (jnp.exp(jnp.sum(da_lam_q1[l] * da_lam_k1[l])) - jnp.exp(jnp.sum(da_lam_q2[l] * da_lam_k2[l]))
               + lam_init).astype(f32)
        Ya = na_call(NAQ, NAK, P_na, bias, n_batch=B, seq=S, ctx_len=L)
        Yd = da_call(lam, DAQ, DAK, DAVX, da_subln_g[l], n_batch=B, seq=S, ctx_len=L, lam_init=lam_init)
        QE, OU, MM, NN = gdn_chunk_call(P_gdn, P_ab, gdn_conv_w[l], gdn_a_log[l], gdn_dt_bias[l], n_batch=B, seq=S)
        Of, Ob = gdn_scan_call(QE, OU, MM, NN, n_batch=B, seq=S)
        Yf = fourier_latent_call(P_f, n_batch=B, seq=S)
        n_tok = NL
        if not last:
            Yac, Ydc = ctx_attn_call(lam, NAQ, NAK, P_na, DAQ, DAK, DAVX, da_subln_g[l],
                                     n_batch=B, seq=S, ctx_len=L, lam_init=lam_init)
            Ya = jnp.concatenate([Ya, Yac], 0)
            Yd = jnp.concatenate([Yd, Ydc], 0)
            Yf = jnp.concatenate([Yf, fourier_ctx_call(P_f, n_batch=B, seq=S, ctx_len=L)], 0)
            n_tok = NL + B * L
        Xmid, H2, RW = merge_call(Xl, Xc, 0, Ya, Yd, Of, Ob, P_z, Yf, P_gate, mods[l], norm2_g[l], gdn_onorm_g[l],
                                  w_branch[l].astype(bf16), w_out[l].astype(bf16), w_router, router_bias,
                                  n_tok=n_tok, n_batch=B, seq=S)
        experts = (moe_w_gate[l].astype(bf16), moe_w_up[l].astype(bf16), moe_w_down[l].astype(bf16))
        Xl = moe_sorted_call(Xmid, H2, RW, mods[l], *experts, row0=0, n_rows=NL, tile=MOE_TILE, win=MOE_WIN,
                             mod_row_fn=lambda i: i // (S // MOE_TILE))
        if not last:
            Xc = moe_sorted_call(Xmid, H2, RW, mods[l], *experts, row0=NL, n_rows=B * L, tile=B * L,
                                 win=MOE_WIN_CTX, mod_row_fn=lambda i: B)
    return Xl.reshape(B, S, D)
```

```python
import functools
import math
import jax
import jax.numpy as jnp
from jax import lax
import numpy as np
from jax.experimental import pallas as pl
from jax.experimental.pallas import tpu as pltpu

D_MODEL = 1024
DEPTH = 2
GRID_W = 64
N_BRANCH = 4
BRANCH_W = D_MODEL // N_BRANCH
GDN_HEADS = 4
GDN_DIM = BRANCH_W // GDN_HEADS
GDN_CHUNK = 64
FNET_GROUPS = 4
FNET_GROUP_W = BRANCH_W // FNET_GROUPS
N_EXPERTS = 16
N_GROUPS = 4
RMS_EPS = 1e-6
NEG_INF = -1e30
ROPE_BASE = 10000.0
VMEM_LIMIT = 56 * 1024 * 1024
HI = lax.Precision.HIGHEST
NT_DIMS = (((1,), (1,)), ((), ()))
LOG2E = 1.4426950408889634


def _mod_kernel(c_ref, w_ref, b_ref, o_ref):
    c = c_ref[...]
    a = c * jax.nn.sigmoid(c)
    o_ref[0] = jnp.dot(a, w_ref[0], preferred_element_type=jnp.float32, precision=HI) + b_ref[0]


def mod_call(cvec, w_mod, b_mod, tn=512):
    depth, D, N = w_mod.shape
    return pl.pallas_call(
        _mod_kernel,
        grid=(depth, N // tn),
        in_specs=[pl.BlockSpec((8, D), lambda l, j: (0, 0)),
                  pl.BlockSpec((1, D, tn), lambda l, j: (l, 0, j)),
                  pl.BlockSpec((1, 1, tn), lambda l, j: (l, 0, j))],
        out_specs=pl.BlockSpec((1, 8, tn), lambda l, j: (l, 0, j)),
        out_shape=jax.ShapeDtypeStruct((depth, 8, N), jnp.float32),
        name="mod",
    )(cvec, w_mod, b_mod.reshape(depth, 1, N))


def _stream_specs(n_lat_tiles, ctx_tile0, tm, width):
    return [pl.BlockSpec((tm, width), lambda i: (jnp.minimum(i, n_lat_tiles - 1), 0)),
            pl.BlockSpec((tm, width), lambda i: (ctx_tile0 + jnp.maximum(i - n_lat_tiles, 0), 0))]


def _inproj_kernel(xl_ref, xc_ref, m_ref, g_ref, w_ref, wab_ref, na_ref, da_ref, gdn_ref, z_ref, f_ref, gate_ref,
                   ab_ref, *, tiles_per_batch, n_lat_tiles, n_batch):
    i = pl.program_id(0)
    r = jnp.where(i < n_lat_tiles, i // tiles_per_batch, n_batch)
    x = jnp.where(i < n_lat_tiles, xl_ref[...], xc_ref[...])
    y = x * lax.rsqrt(jnp.mean(x * x, axis=-1, keepdims=True) + RMS_EPS) * g_ref[...]
    sh = m_ref[pl.ds(r, 1), 0:D_MODEL]
    sc = m_ref[pl.ds(r, 1), D_MODEL:2 * D_MODEL]
    h = (y * (1.0 + sc) + sh).astype(jnp.bfloat16)

    def seg(o_ref, a, b):
        o_ref[...] = jnp.dot(h, w_ref[:, a:b], preferred_element_type=jnp.float32).astype(o_ref.dtype)

    seg(na_ref, 0, 768)
    seg(da_ref, 768, 1536)
    seg(gdn_ref, 1536, 2304)
    seg(z_ref, 2304, 2560)
    seg(f_ref, 2560, 2816)
    for k in range(4):
        gate_ref[:, k * 1024:(k + 1) * 1024] = jnp.dot(
            h, w_ref[:, 2816 + k * 1024:2816 + (k + 1) * 1024],
            preferred_element_type=jnp.float32).astype(gate_ref.dtype)
    ab_ref[...] = jnp.dot(h, wab_ref[...], preferred_element_type=jnp.float32)


def inproj_call(Xl, Xc, ctx_row0, mod_l, g, w_main, w_ab, *, n_batch, seq, ctx_len, tm=256):
    D = Xl.shape[1]
    NT = n_batch * (seq + ctx_len)
    n_lat_tiles = n_batch * seq // tm
    kern = functools.partial(_inproj_kernel, tiles_per_batch=seq // tm, n_lat_tiles=n_lat_tiles, n_batch=n_batch)
    widths = [(768, jnp.bfloat16), (768, jnp.bfloat16), (768, jnp.float32), (256, jnp.bfloat16),
              (256, jnp.bfloat16), (4096, jnp.bfloat16), (128, jnp.float32)]
    return pl.pallas_call(
        kern,
        grid=(NT // tm,),
        in_specs=_stream_specs(n_lat_tiles, ctx_row0 // tm, tm, D) + [
            pl.BlockSpec(mod_l.shape, lambda i: (0, 0)),
            pl.BlockSpec((1, D), lambda i: (0, 0)),
            pl.BlockSpec(w_main.shape, lambda i: (0, 0)),
            pl.BlockSpec(w_ab.shape, lambda i: (0, 0))],
        out_specs=[pl.BlockSpec((tm, w), lambda i: (i, 0)) for w, _ in widths],
        out_shape=[jax.ShapeDtypeStruct((NT, w), dt) for w, dt in widths],
        compiler_params=pltpu.CompilerParams(vmem_limit_bytes=VMEM_LIMIT),
        name="inproj",
    )(Xl, Xc, mod_l, g.reshape(1, D), w_main, w_ab)


def _group_ones(width, group):
    i = np.arange(width)
    return jnp.asarray((i[:, None] // group == i[None, :] // group).astype(np.float32), jnp.bfloat16)


def _group_sum(xx, gmat):
    hi, lo = _split_bf16(xx)
    return (jnp.dot(hi, gmat, preferred_element_type=jnp.float32)
            + jnp.dot(lo, gmat, preferred_element_type=jnp.float32))


def _group_rms(x, gmat, group, gain):
    return x * lax.rsqrt(_group_sum(x * x, gmat) * (1.0 / group) + RMS_EPS) * gain


def rope_tables(seq, tm):
    nf = 8
    t = jnp.arange(seq)
    rows = (t // GRID_W).astype(jnp.float32)
    cols = (t % GRID_W).astype(jnp.float32)
    freqs = ROPE_BASE ** (-jnp.arange(nf, dtype=jnp.float32) / nf)
    d = np.arange(32)
    f_idx = d % 8
    use_col = (d // 16) == 1
    ang = jnp.where(use_col[None, :], cols[:, None], rows[:, None]) * freqs[f_idx][None, :]
    sign = np.where((d % 16) < 8, -1.0, 1.0).astype(np.float32)
    cos = jnp.tile(jnp.cos(ang), (1, 8))
    sin = jnp.tile(jnp.sin(ang) * sign[None, :], (1, 8))
    cos = jnp.concatenate([cos, jnp.ones((tm, 256), jnp.float32)], 0)
    sin = jnp.concatenate([sin, jnp.zeros((tm, 256), jnp.float32)], 0)
    return cos, sin


def _prep_kernel(na_ref, da_ref, cos_ref, sin_ref, g64_ref, g32_ref, gains_ref, naq_ref, nak_ref, daq_ref, dak_ref,
                 davx_ref):
    g64 = g64_ref[...]
    g32 = g32_ref[...]
    na = na_ref[...].astype(jnp.float32)
    naq_ref[...] = (_group_rms(na[:, 0:256], g64, 64, gains_ref[0:1, :]) * (64 ** -0.5)).astype(naq_ref.dtype)
    nak_ref[...] = _group_rms(na[:, 256:512], g64, 64, gains_ref[1:2, :]).astype(nak_ref.dtype)
    da = da_ref[...].astype(jnp.float32)
    cos = cos_ref[...]
    sin = sin_ref[...]
    lane = lax.broadcasted_iota(jnp.int32, (1, 256), 1)
    first = (lane % 16) < 8

    def rope(x):
        swapped = jnp.where(first, pltpu.roll(x, 256 - 8, 1), pltpu.roll(x, 8, 1))
        return x * cos + swapped * sin

    q = rope(_group_rms(da[:, 0:256], g32, 32, gains_ref[2:3, :]))
    k = rope(_group_rms(da[:, 256:512], g32, 32, gains_ref[3:4, :]))
    daq_ref[...] = (q * (32 ** -0.5 * LOG2E)).astype(daq_ref.dtype)
    dak_ref[...] = k.astype(dak_ref.dtype)
    v = da_ref[:, 512:768]
    one_col = (lax.broadcasted_iota(jnp.int32, (v.shape[0], 64), 1) == 0).astype(v.dtype)
    davx_ref[...] = jnp.concatenate(
        [piece for h in range(4) for piece in (v[:, h * 64:(h + 1) * 64], one_col)], -1)


def prep_call(P_na, P_da, cos, sin, na_qg, na_kg, da_qg, da_kg, *, n_batch, seq, tm=256):
    NT = P_na.shape[0]
    tpb = seq // tm
    n_lat = n_batch * tpb
    gains = jnp.stack([jnp.tile(na_qg, 4), jnp.tile(na_kg, 4), jnp.tile(da_qg, 8), jnp.tile(da_kg, 8)]
                      + [jnp.zeros((256,), jnp.float32)] * 4, 0)
    tab_map = lambda i: (jnp.where(i < n_lat, i % tpb, tpb), 0)
    full = lambda a: pl.BlockSpec(a.shape, lambda i: (0,) * a.ndim)
    g64, g32 = _group_ones(256, 64), _group_ones(256, 32)
    return pl.pallas_call(
        _prep_kernel,
        grid=(NT // tm,),
        in_specs=[pl.BlockSpec((tm, 768), lambda i: (i, 0)), pl.BlockSpec((tm, 768), lambda i: (i, 0)),
                  pl.BlockSpec((tm, 256), tab_map), pl.BlockSpec((tm, 256), tab_map),
                  full(g64), full(g32), full(gains)],
        out_specs=[pl.BlockSpec((tm, 256), lambda i: (i, 0))] * 4 + [pl.BlockSpec((tm, 512), lambda i: (i, 0))],
        out_shape=[jax.ShapeDtypeStruct((NT, 256), jnp.bfloat16)] * 4 + [jax.ShapeDtypeStruct((NT, 512), jnp.bfloat16)],
        name="attn_prep",
    )(P_na, P_da, cos, sin, g64, g32, gains)


NA_RB = 4
NA_UR = 11


def na_bias_tables(rpb, rows):
    reps = [(0, 0), (4, 0), (rows - NA_RB, rows - NA_UR)]
    qc = np.arange(GRID_W)[:, None]
    kc = np.arange(GRID_W)[None, :]
    ws = np.clip(qc - 8, 0, GRID_W - 16)
    vcol = (kc >= ws) & (kc < ws + 16)
    rel_c = np.clip(kc - qc + 15, 0, 30)
    sel_c = (rel_c[..., None] == np.arange(31)).astype(np.float32)
    sel_r, vrow = [], []
    for r0, u0 in reps:
        r = r0 + np.arange(NA_RB)[:, None]
        krow = u0 + np.arange(NA_UR)[None, :]
        start = np.clip(r - 4, 0, rows - 8)
        vrow.append((krow >= start) & (krow < start + 8))
        sel_r.append((np.clip(krow - r + 7, 0, 14)[..., None] == np.arange(15)).astype(np.float32))
    sel_r, vrow = np.stack(sel_r), np.stack(vrow)
    b = jnp.einsum('taui,hij,qkj->thaquk', jnp.asarray(sel_r), rpb.astype(jnp.float32), jnp.asarray(sel_c),
                   precision=HI)
    valid = vrow[:, None, :, None, :, None] & vcol[None, None, None, :, None, :]
    b = jnp.where(valid, b, NEG_INF)
    return b.reshape(3, rpb.shape[0], NA_RB * GRID_W, NA_UR * GRID_W).astype(jnp.bfloat16)


def _na_kernel(q_ref, k_ref, v_ref, kc_ref, vc_ref, bias_ref, o_ref, *, rows):
    i = pl.program_id(1)
    u0 = jnp.clip(i * NA_RB - 4, 0, rows - NA_UR)
    off = pl.multiple_of(u0 * GRID_W, GRID_W)
    nkw = NA_UR * GRID_W
    kwin = k_ref[pl.ds(off, nkw), :]
    vwin = v_ref[pl.ds(off, nkw), :]
    q = q_ref[...]
    kc = kc_ref[...]
    vc = vc_ref[...]
    outs = []
    for h in range(4):
        sl = slice(h * 64, (h + 1) * 64)
        qh = q[:, sl]
        s_loc = lax.dot_general(qh, kwin[:, sl], NT_DIMS, preferred_element_type=jnp.float32) + bias_ref[0, h].astype(jnp.float32)
        s_ctx = lax.dot_general(qh, kc[:, sl], NT_DIMS, preferred_element_type=jnp.float32)
        m = jnp.maximum(jnp.max(s_loc, -1, keepdims=True), jnp.max(s_ctx, -1, keepdims=True))
        p_loc = jnp.exp(s_loc - m)
        p_ctx = jnp.exp(s_ctx - m)
        l = jnp.sum(p_loc, -1, keepdims=True) + jnp.sum(p_ctx, -1, keepdims=True)
        o = (jnp.dot(p_loc.astype(jnp.bfloat16), vwin[:, sl], preferred_element_type=jnp.float32)
             + jnp.dot(p_ctx.astype(jnp.bfloat16), vc[:, sl], preferred_element_type=jnp.float32))
        outs.append(o / l)
    o_ref[...] = jnp.concatenate(outs, -1).astype(o_ref.dtype)


def na_call(NAQ, NAK, P_na, bias, *, n_batch, seq, ctx_len):
    rows = seq // GRID_W
    nblk = rows // NA_RB
    tq = NA_RB * GRID_W
    cb = n_batch * seq // ctx_len
    kern = functools.partial(_na_kernel, rows=rows)
    tsel = lambda b, i: (jnp.where(i == 0, 0, jnp.where(i == nblk - 1, 2, 1)), 0, 0, 0)
    return pl.pallas_call(
        kern,
        grid=(n_batch, nblk),
        in_specs=[pl.BlockSpec((tq, 256), lambda b, i: (b * nblk + i, 0)),
                  pl.BlockSpec((seq, 256), lambda b, i: (b, 0)),
                  pl.BlockSpec((seq, 256), lambda b, i: (b, 2)),
                  pl.BlockSpec((ctx_len, 256), lambda b, i: (cb + b, 0)),
                  pl.BlockSpec((ctx_len, 256), lambda b, i: (cb + b, 2)),
                  pl.BlockSpec((1,) + bias.shape[1:], tsel)],
        out_specs=pl.BlockSpec((tq, 256), lambda b, i: (b * nblk + i, 0)),
        out_shape=jax.ShapeDtypeStruct((n_batch * seq, 256), jnp.bfloat16),
        compiler_params=pltpu.CompilerParams(vmem_limit_bytes=VMEM_LIMIT),
        name="na_attn",
    )(NAQ, NAK, P_na, NAK, P_na, bias)


def _stack_masked_q(q):
    lane = lax.broadcasted_iota(jnp.int32, (1, 256), 1)
    return jnp.concatenate([jnp.where((lane // 32) == hm, q, jnp.zeros_like(q)) for hm in range(8)], 0)


def _da_finish(acc, lam, gain, g64, post_scale):
    tq = acc.shape[0] // 8
    norm = acc / jnp.broadcast_to(acc[:, 64:65], acc.shape)
    lane = lax.broadcasted_iota(jnp.int32, (1, 128), 1)
    halves = []
    for hp in range(2):
        d = [norm[(4 * hp + 2 * j) * tq:(4 * hp + 2 * j + 1) * tq]
             - lam * norm[(4 * hp + 2 * j + 1) * tq:(4 * hp + 2 * j + 2) * tq] for j in range(2)]
        halves.append(jnp.where(lane < 64, d[0], pltpu.roll(d[1], 64, 1)))
    o = jnp.concatenate(halves, -1)
    return _group_rms(o, g64, 64, gain) * post_scale


def _da_tile(qs, k, vx, m, acc, tq):
    tk = k.shape[0]
    s = lax.dot_general(qs, k, NT_DIMS, preferred_element_type=jnp.float32)
    m_new = jnp.maximum(m, jnp.max(s, -1, keepdims=True))
    alpha = jnp.exp2(m - m_new)
    p = jnp.exp2(s - pltpu.repeat(m_new, tk // 128, axis=1)).astype(jnp.bfloat16)
    pv = jnp.concatenate(
        [jnp.dot(p[2 * h * tq:(2 * h + 2) * tq], vx[:, h * 128:(h + 1) * 128], preferred_element_type=jnp.float32)
         for h in range(4)], 0)
    return m_new, alpha * acc + pv


def _da_kernel(lam_ref, q_ref, k_ref, vx_ref, kc_ref, vxc_ref, gain_ref, g64_ref, o_ref, m_ref, acc_ref,
               *, tk, n_kv, post_scale):
    tq = q_ref.shape[0]
    qs = _stack_masked_q(q_ref[...])
    m_ref[...] = jnp.full(m_ref.shape, -jnp.inf, jnp.float32)
    acc_ref[...] = jnp.zeros(acc_ref.shape, jnp.float32)

    def body(j, carry):
        for u in range(2):
            off = pl.multiple_of((2 * j + u) * tk, tk)
            m, acc = _da_tile(qs, k_ref[pl.ds(off, tk), :], vx_ref[pl.ds(off, tk), :], m_ref[...], acc_ref[...], tq)
            m_ref[...] = m
            acc_ref[...] = acc
        return carry

    lax.fori_loop(0, n_kv // 2 - 1, body, 0, unroll=True)
    off = (n_kv - 2) * tk
    m, acc = _da_tile(qs, k_ref[off:off + tk, :], vx_ref[off:off + tk, :], m_ref[...], acc_ref[...], tq)
    k_last = jnp.concatenate([k_ref[off + tk:off + 2 * tk, :], kc_ref[...]], 0)
    vx_last = jnp.concatenate([vx_ref[off + tk:off + 2 * tk, :], vxc_ref[...]], 0)
    m, acc = _da_tile(qs, k_last, vx_last, m, acc, tq)
    o_ref[...] = _da_finish(acc, lam_ref[0], gain_ref[...], g64_ref[...], post_scale).astype(o_ref.dtype)


def da_call(lam, DAQ, DAK, DAVX, subln_g, *, n_batch, seq, ctx_len, lam_init, tq=256, tk=1024):
    tk = min(tk, seq // 2)
    nq = seq // tq
    cb = n_batch * seq // ctx_len
    g64 = _group_ones(256, 64)
    gain = jnp.tile(subln_g, 4).reshape(1, 256)
    n_kv = seq // tk
    assert n_kv % 2 == 0
    kern = functools.partial(_da_kernel, tk=tk, n_kv=n_kv, post_scale=1.0 - lam_init)
    return pl.pallas_call(
        kern,
        grid=(n_batch, nq),
        in_specs=[pl.BlockSpec(memory_space=pltpu.SMEM),
                  pl.BlockSpec((tq, 256), lambda b, i: (b * nq + i, 0)),
                  pl.BlockSpec((seq, 256), lambda b, i: (b, 0)),
                  pl.BlockSpec((seq, 512), lambda b, i: (b, 0)),
                  pl.BlockSpec((ctx_len, 256), lambda b, i: (cb + b, 0)),
                  pl.BlockSpec((ctx_len, 512), lambda b, i: (cb + b, 0)),
                  pl.BlockSpec((1, 256), lambda b, i: (0, 0)),
                  pl.BlockSpec((256, 256), lambda b, i: (0, 0))],
        out_specs=pl.BlockSpec((tq, 256), lambda b, i: (b * nq + i, 0)),
        out_shape=jax.ShapeDtypeStruct((n_batch * seq, 256), jnp.bfloat16),
        scratch_shapes=[pltpu.VMEM((8 * tq, 128), jnp.float32), pltpu.VMEM((8 * tq, 128), jnp.float32)],
        compiler_params=pltpu.CompilerParams(vmem_limit_bytes=VMEM_LIMIT),
        name="da_attn",
    )(lam.reshape(1), DAQ, DAK, DAVX, DAK, DAVX, gain, g64)


def _ctx_kernel(lam_ref, naq_ref, nak_ref, nav_ref, daq_ref, dak_ref, dav_ref, gain_ref, g64_ref, ya_ref, yd_ref,
                *, post_scale):
    q = naq_ref[...]
    k = nak_ref[...]
    v = nav_ref[...]
    outs = []
    for h in range(4):
        sl = slice(h * 64, (h + 1) * 64)
        s = lax.dot_general(q[:, sl], k[:, sl], NT_DIMS, preferred_element_type=jnp.float32)
        p = jnp.exp(s - jnp.max(s, -1, keepdims=True))
        o = jnp.dot(p.astype(jnp.bfloat16), v[:, sl], preferred_element_type=jnp.float32)
        outs.append(o / jnp.sum(p, -1, keepdims=True))
    ya_ref[...] = jnp.concatenate(outs, -1).astype(ya_ref.dtype)
    tq = daq_ref.shape[0]
    qs = _stack_masked_q(daq_ref[...])
    m0 = jnp.full((8 * tq, 128), -jnp.inf, jnp.float32)
    m, acc = _da_tile(qs, dak_ref[...], dav_ref[...], m0, jnp.zeros((8 * tq, 128), jnp.float32), tq)
    yd_ref[...] = _da_finish(acc, lam_ref[0], gain_ref[...], g64_ref[...], post_scale).astype(yd_ref.dtype)


def ctx_attn_call(lam, NAQ, NAK, P_na, DAQ, DAK, DAVX, subln_g, *, n_batch, seq, ctx_len, lam_init):
    cb = n_batch * seq // ctx_len
    g64 = _group_ones(256, 64)
    gain = jnp.tile(subln_g, 4).reshape(1, 256)
    blk = lambda col: pl.BlockSpec((ctx_len, 256), lambda b: (cb + b, col))
    return pl.pallas_call(
        functools.partial(_ctx_kernel, post_scale=1.0 - lam_init),
        grid=(n_batch,),
        in_specs=[pl.BlockSpec(memory_space=pltpu.SMEM), blk(0), blk(0), blk(2), blk(0), blk(0),
                  pl.BlockSpec((ctx_len, 512), lambda b: (cb + b, 0)),
                  pl.BlockSpec((1, 256), lambda b: (0, 0)), pl.BlockSpec((256, 256), lambda b: (0, 0))],
        out_specs=[pl.BlockSpec((ctx_len, 256), lambda b: (b, 0))] * 2,
        out_shape=[jax.ShapeDtypeStruct((n_batch * ctx_len, 256), jnp.bfloat16)] * 2,
        name="ctx_attn",
    )(lam.reshape(1), NAQ, NAK, P_na, DAQ, DAK, DAVX, gain, g64)


def _split_bf16(a):
    hi = a.astype(jnp.bfloat16)
    return hi, (a - hi.astype(jnp.float32)).astype(jnp.bfloat16)


def _route(logits_t, bias_col):
    per_group = N_EXPERTS // N_GROUPS
    scores = jax.nn.sigmoid(logits_t)
    sel_all = scores + bias_col
    sel = [sel_all[e:e + 1, :] for e in range(N_EXPERTS)]
    top2 = []
    for e in range(N_EXPERTS):
        g0 = e // per_group * per_group
        rank = jnp.zeros_like(sel[e])
        for o in range(g0, g0 + per_group):
            if o != e:
                beats = (sel[o] > sel[e]) | (sel[o] == sel[e]) if o < e else (sel[o] > sel[e])
                rank = rank + beats.astype(jnp.float32)
        top2.append(rank < 1.5)
    gs = []
    for g in range(N_GROUPS):
        acc = jnp.zeros_like(sel[0])
        for e in range(g * per_group, (g + 1) * per_group):
            acc = acc + jnp.where(top2[e], sel[e], 0.0)
        gs.append(acc)
    rows = []
    for g in range(N_GROUPS):
        beaten = jnp.zeros_like(sel[0])
        for o in range(N_GROUPS):
            if o != g:
                b = (gs[o] >= gs[g]) if o < g else (gs[o] > gs[g])
                beaten = beaten + b.astype(jnp.float32)
        best = beaten < 0.5
        for e in range(g * per_group, (g + 1) * per_group):
            rows.append(jnp.where(best & top2[e], scores[e:e + 1, :], 0.0))
    w = jnp.concatenate(rows, 0)
    return w / jnp.sum(w, axis=0, keepdims=True)


def _merge_kernel(xl_ref, xc_ref, ya_ref, yd_ref, of_ref, ob_ref, z_ref, yf_ref, gate_ref, m_ref, g2_ref, og_ref, g64_ref,
                  wb_ref, wo_ref, wr_ref, rb_ref, xo_ref, h2_ref, rw_ref, *, tiles_per_batch, n_lat_tiles, n_batch):
    i = pl.program_id(0)
    r = jnp.where(i < n_lat_tiles, i // tiles_per_batch, n_batch)
    z = z_ref[...].astype(jnp.float32)
    yg = (_group_rms(of_ref[...] + ob_ref[...], g64_ref[...], GDN_DIM, og_ref[...]) * (z * jax.nn.sigmoid(z)))
    branches = (ya_ref[...], yd_ref[...], yg.astype(jnp.bfloat16), yf_ref[...].astype(jnp.bfloat16))
    mix = None
    for b, yb in enumerate(branches):
        proj = jnp.dot(yb, wb_ref[b], preferred_element_type=jnp.float32)
        term = jax.nn.sigmoid(gate_ref[:, b * D_MODEL:(b + 1) * D_MODEL].astype(jnp.float32)) * proj
        mix = term if mix is None else mix + term
    y = jnp.dot(mix.astype(jnp.bfloat16), wo_ref[...], preferred_element_type=jnp.float32)
    mrow = lambda k: m_ref[pl.ds(r, 1), k * D_MODEL:(k + 1) * D_MODEL]
    x = jnp.where(i < n_lat_tiles, xl_ref[...], xc_ref[...]) + mrow(2) * y
    xo_ref[...] = x
    h2 = (x * lax.rsqrt(jnp.mean(x * x, axis=-1, keepdims=True) + RMS_EPS) * g2_ref[...]) * (1.0 + mrow(4)) + mrow(3)
    h2_ref[...] = h2.astype(h2_ref.dtype)
    h_hi, h_lo = _split_bf16(h2)
    w_hi, w_lo = _split_bf16(wr_ref[...])
    nt = lambda a, b: lax.dot_general(a, b, NT_DIMS, preferred_element_type=jnp.float32)
    logits_t = nt(w_hi, h_hi) + nt(w_hi, h_lo) + nt(w_lo, h_hi)
    w_t = _route(logits_t, rb_ref[...])
    pad = jnp.zeros((128 - N_EXPERTS, w_t.shape[1]), jnp.float32)
    rw_ref[...] = jnp.transpose(jnp.concatenate([w_t, pad], 0))


def merge_call(Xl, Xc, ctx_row0, Ya, Yd, Of, Ob, P_z, Yf, P_gate, mod_l, g2n, onorm_g, w_branch, w_out, w_router,
               router_bias, *, n_tok, n_batch, seq, tm=256):
    D = Xl.shape[1]
    n_lat_tiles = n_batch * seq // tm
    kern = functools.partial(_merge_kernel, tiles_per_batch=seq // tm, n_lat_tiles=n_lat_tiles, n_batch=n_batch)
    row = lambda w: pl.BlockSpec((tm, w), lambda i: (i, 0))
    full = lambda a: pl.BlockSpec(a.shape, lambda i: (0,) * a.ndim)
    wr_t = jnp.transpose(w_router)
    rb = router_bias.reshape(N_EXPERTS, 1)
    g2 = g2n.reshape(1, D)
    og = jnp.tile(onorm_g, GDN_HEADS).reshape(1, BRANCH_W)
    g64 = _group_ones(BRANCH_W, GDN_DIM)
    return pl.pallas_call(
        kern,
        grid=(n_tok // tm,),
        in_specs=_stream_specs(n_lat_tiles, ctx_row0 // tm, tm, D) + [
                  row(256), row(256), row(256), row(256), row(256), row(256), row(4 * D), full(mod_l),
                  full(g2), full(og), full(g64), full(w_branch), full(w_out), full(wr_t), full(rb)],
        out_specs=[row(D), row(D), row(128)],
        out_shape=[jax.ShapeDtypeStruct((n_tok, D), jnp.float32), jax.ShapeDtypeStruct((n_tok, D), jnp.bfloat16),
                   jax.ShapeDtypeStruct((n_tok, 128), jnp.float32)],
        compiler_params=pltpu.CompilerParams(vmem_limit_bytes=VMEM_LIMIT),
        name="merge",
    )(Xl, Xc, Ya, Yd, Of, Ob, P_z, Yf, P_gate, mod_l, g2, og, g64, w_branch, w_out, wr_t, rb)


MOE_TILE = 1024
MOE_WIN = 320
MOE_WIN_CTX = 176
MOE_EPS = 2


def _moe_sorted_kernel(x_ref, h_ref, rw_ref, m_ref, wg_ref, wu_ref, wd_ref, o_ref, xs_ref, ws_ref, ys_ref, dest_ref,
                       seg_ref, *, win, mod_row_fn):
    T = h_ref.shape[0]
    per_group = N_EXPERTS // N_GROUPS
    i = pl.program_id(0)
    step = pl.program_id(1)
    lane = lax.broadcasted_iota(jnp.int32, (1, 128), 1)
    f32, bf16 = jnp.float32, jnp.bfloat16

    @pl.when(step == 0)
    def _():
        rw = rw_ref[...]
        r128 = lax.broadcasted_iota(jnp.int32, (128, 128), 0)
        c128 = lax.broadcasted_iota(jnp.int32, (128, 128), 1)
        e2g = ((r128 // per_group == c128) & (r128 < N_EXPERTS)).astype(bf16)
        og = (jnp.dot((rw > 0.0).astype(bf16), e2g, preferred_element_type=f32) > 0.5).astype(f32)
        rt = lax.broadcasted_iota(jnp.int32, (T, T), 0)
        ct = lax.broadcasted_iota(jnp.int32, (T, T), 1)
        earlier = jnp.dot((rt > ct).astype(bf16), og.astype(bf16), preferred_element_type=f32)
        cnt_row = jnp.sum(og, axis=0, keepdims=True)
        start_row = jnp.dot(jnp.broadcast_to(cnt_row, (8, 128)), (r128 < c128).astype(f32),
                            preferred_element_type=f32, precision=HI)[0:1]
        dest = jnp.sum(og * (earlier + start_row), axis=-1, keepdims=True)
        dest_ref[...] = jnp.broadcast_to(dest, (T, 128))
        dest_row = jnp.transpose(dest_ref[...])[0:1, :]
        perm = (dest_row == rt.astype(f32)).astype(bf16)
        xs_ref[...] = jnp.dot(perm, h_ref[...], preferred_element_type=f32).astype(bf16)
        ws_ref[...] = sum(jnp.dot(perm, piece, preferred_element_type=f32) for piece in _split_bf16(rw))
        ys_ref[...] = jnp.zeros(ys_ref.shape, f32)
        for g in range(N_GROUPS):
            seg_ref[g] = jnp.sum(jnp.where(lane == g, start_row, 0.0)).astype(jnp.int32)
            seg_ref[N_GROUPS + g] = jnp.sum(jnp.where(lane == g, cnt_row, 0.0)).astype(jnp.int32)

    g = (step * MOE_EPS) // per_group
    start = seg_ref[g]
    end = start + seg_ref[N_GROUPS + g]
    s0 = (start // 16) * 16
    n_win = (end - s0 + win - 1) // win

    def window(w, carry):
        lo = s0 + w * win
        off = pl.multiple_of(jnp.minimum(lo, T - win), 16)
        xw = xs_ref[pl.ds(off, win), :]
        ww = ws_ref[pl.ds(off, win), :]
        row = off + lax.broadcasted_iota(jnp.int32, (win, 1), 0)
        y = None
        for j in range(MOE_EPS):
            w_col = jnp.sum(jnp.where(lane == step * MOE_EPS + j, ww, 0.0), axis=-1, keepdims=True)
            w_col = jnp.where(row >= lo, w_col, 0.0)
            a = jnp.dot(xw, wg_ref[j], preferred_element_type=f32)
            u = jnp.dot(xw, wu_ref[j], preferred_element_type=f32)
            act = (a * jax.nn.sigmoid(a) * u * w_col).astype(bf16)
            yj = jnp.dot(act, wd_ref[j], preferred_element_type=f32)
            y = yj if y is None else y + yj
        ys_ref[pl.ds(off, win), :] += y
        return carry

    lax.fori_loop(0, n_win, window, 0)

    @pl.when(step == N_EXPERTS // MOE_EPS - 1)
    def _():
        ct = lax.broadcasted_iota(jnp.int32, (T, T), 1)
        unperm = (pltpu.repeat(dest_ref[...], T // 128, axis=1) == ct.astype(f32)).astype(bf16)
        f = jnp.dot(unperm, ys_ref[...].astype(bf16), preferred_element_type=f32)
        g2 = m_ref[pl.ds(mod_row_fn(i), 1), 5 * D_MODEL:6 * D_MODEL]
        o_ref[...] = x_ref[...] + g2 * f


def moe_sorted_call(Xmid, H2, RW, mod_l, w_gate, w_up, w_down, *, row0, n_rows, tile, win, mod_row_fn):
    D = Xmid.shape[1]
    F = w_gate.shape[-1]
    b0 = row0 // tile
    tok = lambda w: pl.BlockSpec((tile, w), lambda i, e: (b0 + i, 0))
    kern = functools.partial(_moe_sorted_kernel, win=win, mod_row_fn=mod_row_fn)
    return pl.pallas_call(
        kern,
        grid=(n_rows // tile, N_EXPERTS // MOE_EPS),
        in_specs=[tok(D), tok(D), tok(128),
                  pl.BlockSpec(mod_l.shape, lambda i, e: (0, 0)),
                  pl.BlockSpec((MOE_EPS, D, F), lambda i, e: (e, 0, 0)),
                  pl.BlockSpec((MOE_EPS, D, F), lambda i, e: (e, 0, 0)),
                  pl.BlockSpec((MOE_EPS, F, D), lambda i, e: (e, 0, 0))],
        out_specs=pl.BlockSpec((tile, D), lambda i, e: (i, 0)),
        out_shape=jax.ShapeDtypeStruct((n_rows, D), jnp.float32),
        scratch_shapes=[pltpu.VMEM((tile, D), jnp.bfloat16), pltpu.VMEM((tile, 128), jnp.float32),
                        pltpu.VMEM((tile, D), jnp.float32), pltpu.VMEM((tile, 128), jnp.float32),
                        pltpu.SMEM((2 * N_GROUPS,), jnp.int32)],
        compiler_params=pltpu.CompilerParams(vmem_limit_bytes=VMEM_LIMIT),
        name="moe_sorted",
    )(Xmid, H2, RW, mod_l, w_gate, w_up, w_down)


GDN_TILE = 256
GDN_CPT = GDN_TILE // GDN_CHUNK
GDN_LOCKSTEP = 2


def _mm(a, b, passes=1, dims=None):
    if dims is None:
        dot = lambda x, y: jnp.dot(x, y, preferred_element_type=jnp.float32)
    else:
        dot = lambda x, y: lax.dot_general(x, y, dims, preferred_element_type=jnp.float32)
    if passes == 1:
        return dot(a.astype(jnp.bfloat16), b.astype(jnp.bfloat16))
    a_hi, a_lo = _split_bf16(a)
    b_hi, b_lo = _split_bf16(b)
    return dot(a_hi, b_hi) + dot(a_hi, b_lo) + dot(a_lo, b_hi)


def _stack_heads(x):
    lane = lax.broadcasted_iota(jnp.int32, (1, 256), 1)
    return jnp.concatenate([jnp.where((lane // GDN_DIM) == h, x, 0.0) for h in range(GDN_HEADS)], 0)


def _slab(x):
    return x[0:64] + x[64:128] + x[128:192] + x[192:256]


def _unit_tri_inverse(mats):
    shape = mats[0].shape
    eye = (lax.broadcasted_iota(jnp.int32, shape, 0) == lax.broadcasted_iota(jnp.int32, shape, 1)).astype(jnp.float32)
    ps = [eye - a for a in mats]
    pws = list(mats)
    for _ in range(5):
        pws = [_mm(pw, pw, 3) for pw in pws]
        ps = [p + _mm(p, pw, 3) for p, pw in zip(ps, pws)]
    return ps


def _gdn_chunk_kernel(x_ref, xp_ref, xn_ref, ab_ref, cw_ref, par_ref, g64_ref, qe_ref, ou_ref, mm_ref, nn_ref,
                      *, tiles_per_batch, n_lat_tiles):
    i = pl.program_id(0)
    is_lat = i < n_lat_tiles
    first = jnp.where(is_lat, (i % tiles_per_batch) == 0, True)
    last = jnp.where(is_lat, (i % tiles_per_batch) == tiles_per_batch - 1, True)
    xp = jnp.concatenate([jnp.where(first, 0.0, xp_ref[...]), x_ref[...], jnp.where(last, 0.0, xn_ref[...])], 0)
    y = xp[6:6 + GDN_TILE] * cw_ref[0:1, :]
    for t in range(1, 5):
        y = y + xp[6 + t:6 + t + GDN_TILE] * cw_ref[t:t + 1, :]
    y = y * jax.nn.sigmoid(y)
    g64 = g64_ref[...]
    q = y[:, 0:256]
    k = y[:, 256:512]
    v = y[:, 512:768]
    q = q * lax.rsqrt(_group_sum(q * q, g64) + RMS_EPS) * (GDN_DIM ** -0.5)
    k = k * lax.rsqrt(_group_sum(k * k, g64) + RMS_EPS)
    ab = ab_ref[...]
    lane128 = lax.broadcasted_iota(jnp.int32, (1, 128), 1)
    g_all = jnp.where(lane128 < 8, par_ref[0:1, :] * jax.nn.softplus(ab + par_ref[1:2, :]), 0.0)
    beta_all = jax.nn.sigmoid(ab)
    r64 = lax.broadcasted_iota(jnp.int32, (64, 64), 0)
    c64 = lax.broadcasted_iota(jnp.int32, (64, 64), 1)
    low = (r64 >= c64).astype(jnp.float32)
    upp = (r64 <= c64).astype(jnp.float32)
    rr = lax.broadcasted_iota(jnp.int32, (256, 256), 0)
    cc = lax.broadcasted_iota(jnp.int32, (256, 256), 1)
    same = (rr // 64) == (cc // 64)
    eye = (rr == cc).astype(jnp.float32)
    for c0 in range(0, GDN_CPT, GDN_LOCKSTEP):
        stage = []
        for c in range(c0, c0 + GDN_LOCKSTEP):
            sl = slice(c * GDN_CHUNK, (c + 1) * GDN_CHUNK)
            g_c = g_all[sl]
            gsum = jnp.where(lane128 < 4, jnp.dot(low, g_c, preferred_element_type=jnp.float32, precision=HI),
                             jnp.dot(upp, g_c, preferred_element_type=jnp.float32, precision=HI))
            gsum_t = jnp.transpose(jnp.concatenate([gsum, jnp.zeros_like(gsum)], 0))[:, 0:64]
            tot = jnp.sum(g_c, axis=0, keepdims=True)
            beta_c = beta_all[sl]
            q_sm = _stack_heads(q[sl])
            k_sm = _stack_heads(k[sl])
            v_sm = _stack_heads(v[sl])
            kk = _mm(k_sm, k_sm, 1, NT_DIMS)
            qk = _mm(q_sm, k_sm, 1, NT_DIMS)
            for d in range(2):
                col = lambda h: 4 * d + h
                cmat = jnp.concatenate([jnp.broadcast_to(gsum[:, col(h):col(h) + 1], (64, 256)) for h in range(4)], 0)
                rrow = jnp.concatenate([gsum_t[col(h):col(h) + 1, :] for h in range(4)], 1)
                bmat = jnp.concatenate([jnp.broadcast_to(beta_c[:, 8 + col(h):9 + col(h)], (64, 256))
                                        for h in range(4)], 0)
                tmat = jnp.concatenate([jnp.broadcast_to(tot[:, col(h):col(h) + 1], (64, 256)) for h in range(4)], 0)
                tri = (rr % 64 >= cc % 64) if d == 0 else (rr % 64 <= cc % 64)
                incl = same & tri
                strict = incl & (rr != cc)
                dec = jnp.where(incl, jnp.exp(jnp.where(incl, cmat - rrow, 0.0)), 0.0)
                stage.append((d, sl, q_sm, k_sm, v_sm, qk, cmat, bmat, tmat, dec,
                              jnp.where(strict, kk * bmat * dec, 0.0)))
        t_invs = _unit_tri_inverse([st[-1] for st in stage])
        n = len(stage)
        q_sms, k_sms, v_sms, qks, cmats, bmats, tmats, decs = (
            [st[f] for st in stage] for f in range(2, 10))
        egs = [jnp.exp(cm) for cm in cmats]
        us = [_mm(t_invs[j], v_sms[j] * bmats[j], 1) for j in range(n)]
        ws = [_mm(t_invs[j], k_sms[j] * bmats[j] * egs[j], 1) for j in range(n)]
        qkms = [decs[j] * qks[j] for j in range(n)]
        qes = [q_sms[j] * egs[j] - _mm(qkms[j], ws[j], 1) for j in range(n)]
        ous = [_mm(qkms[j], us[j], 1) for j in range(n)]
        kd_ts = [jnp.transpose(k_sms[j] * jnp.exp(tmats[j] - cmats[j])) for j in range(n)]
        m_bds = [eye * jnp.exp(tmats[j]) - _mm(kd_ts[j], ws[j], 1) for j in range(n)]
        n_bds = [_mm(kd_ts[j], us[j], 1) for j in range(n)]
        for j in range(n):
            d, sl = stage[j][0], stage[j][1]
            qe_ref[d, sl, :] = _slab(qes[j])
            ou_ref[d, sl, :] = _slab(ous[j])
            mm_ref[d, sl, :] = _slab(m_bds[j])
            nn_ref[d, sl, :] = _slab(n_bds[j])


def gdn_chunk_call(P_gdn, P_ab, conv_w, a_log, dt_bias, *, n_batch, seq):
    NT = P_gdn.shape[0]
    tpb = seq // GDN_TILE
    n_lat = n_batch * tpb
    n_tiles = NT // GDN_TILE
    hb = GDN_TILE // 8
    cw = jnp.zeros((8, 768), jnp.float32).at[:5].set(conv_w)
    par = jnp.zeros((8, 128), jnp.float32)
    par = par.at[0, :8].set(-jnp.exp(a_log.reshape(8))).at[1, :8].set(dt_bias.reshape(8))
    g64 = _group_ones(256, 64)
    kern = functools.partial(_gdn_chunk_kernel, tiles_per_batch=tpb, n_lat_tiles=n_lat)
    full = lambda a: pl.BlockSpec(a.shape, lambda i: (0,) * a.ndim)
    out = pl.BlockSpec((2, GDN_TILE, 256), lambda i: (0, i, 0))
    return pl.pallas_call(
        kern,
        grid=(n_tiles,),
        in_specs=[pl.BlockSpec((GDN_TILE, 768), lambda i: (i, 0)),
                  pl.BlockSpec((8, 768), lambda i: (jnp.maximum(i * hb - 1, 0), 0)),
                  pl.BlockSpec((8, 768), lambda i: (jnp.minimum((i + 1) * hb, n_tiles * hb - 1), 0)),
                  pl.BlockSpec((GDN_TILE, 128), lambda i: (i, 0)),
                  full(cw), full(par), full(g64)],
        out_specs=[out] * 4,
        out_shape=[jax.ShapeDtypeStruct((2, NT, 256), jnp.float32)] * 4,
        compiler_params=pltpu.CompilerParams(vmem_limit_bytes=VMEM_LIMIT),
        name="gdn_chunk",
    )(P_gdn, P_gdn, P_gdn, P_ab, cw, par, g64)


def _gdn_scan_kernel(qef_ref, ouf_ref, mf_ref, nf_ref, qeb_ref, oub_ref, mb_ref, nb_ref, of_ref, ob_ref, s_ref):
    t = pl.program_id(1)

    @pl.when(t == 0)
    def _():
        s_ref[...] = jnp.zeros(s_ref.shape, jnp.float32)

    def step(d, c, qe_ref, ou_ref, m_ref, n_ref, o_ref):
        sl = slice(c * GDN_CHUNK, (c + 1) * GDN_CHUNK)
        s = s_ref[d]
        o_ref[sl, :] = _slab(_mm(_stack_heads(qe_ref[0, sl, :]), s, 1)) + ou_ref[0, sl, :]
        s_ref[d] = _mm(_stack_heads(m_ref[0, sl, :]), s, 3) + _stack_heads(n_ref[0, sl, :])

    for c in range(GDN_CPT):
        step(0, c, qef_ref, ouf_ref, mf_ref, nf_ref, of_ref)
        step(1, GDN_CPT - 1 - c, qeb_ref, oub_ref, mb_ref, nb_ref, ob_ref)


def gdn_scan_call(QE, OU, MM, NN, *, n_batch, seq):
    NT = QE.shape[1]
    tpb = seq // GDN_TILE
    cblk = n_batch * tpb
    fwd = lambda b, t: jnp.where(t == 0, cblk + b, b * tpb + t - 1)
    bwd = lambda b, t: jnp.where(t == 0, cblk + b, b * tpb + tpb - t)
    spec = lambda d, f: pl.BlockSpec((1, GDN_TILE, 256), lambda b, t: (d, f(b, t), 0))
    ospec = lambda f: pl.BlockSpec((GDN_TILE, 256), lambda b, t: (f(b, t), 0))
    return pl.pallas_call(
        _gdn_scan_kernel,
        grid=(n_batch, tpb + 1),
        in_specs=[spec(0, fwd)] * 4 + [spec(1, bwd)] * 4,
        out_specs=[ospec(fwd), ospec(bwd)],
        out_shape=[jax.ShapeDtypeStruct((NT, 256), jnp.float32)] * 2,
        scratch_shapes=[pltpu.VMEM((2, 256, 256), jnp.float32)],
        compiler_params=pltpu.CompilerParams(vmem_limit_bytes=VMEM_LIMIT),
        name="gdn_scan",
    )(QE, OU, MM, NN, QE, OU, MM, NN)


def _dft_cs(n):
    a = 2.0 * np.pi * np.outer(np.arange(n), np.arange(n)) / n
    return np.cos(a), np.sin(a)


def _channel_dft():
    c, s = _dft_cs(FNET_GROUP_W)
    eye = np.eye(FNET_GROUPS)
    return np.concatenate([np.kron(eye, c), np.kron(eye, s)], 1)


FNET_SUB = 16


def _fnet_a_kernel(x_ref, cs_ref, m_ref, cphi_ref, sphi_ref, br_ref, bi_ref, *, n_rows):
    for j in range(FNET_SUB):
        sl = slice(j * BRANCH_W, (j + 1) * BRANCH_W)
        u = jnp.dot(x_ref[:, j, :], cs_ref[...], preferred_element_type=jnp.float32)
        st = jnp.concatenate([u[:, :BRANCH_W], u[:, BRANCH_W:]], 0).astype(jnp.bfloat16)
        a = jnp.dot(m_ref[...], st, preferred_element_type=jnp.float32)
        ar, ai = a[:n_rows], a[n_rows:]
        cp, sp = cphi_ref[:, sl], sphi_ref[:, sl]
        br_ref[j] = ar * cp - ai * sp
        bi_ref[j] = ar * sp + ai * cp


def _fnet_c_kernel(br_ref, bi_ref, m_ref, o_ref):
    for j in range(FNET_SUB):
        st = jnp.concatenate([br_ref[:, j, :], bi_ref[:, j, :]], 0).astype(jnp.bfloat16)
        o_ref[:, j, :] = jnp.dot(m_ref[...], st, preferred_element_type=jnp.float32)


def fourier_latent_call(P_f, *, n_batch, seq):
    rows = seq // GRID_W
    bf16, f32 = jnp.bfloat16, jnp.float32
    c1, s1 = _dft_cs(rows)
    norm = 1.0 / math.sqrt(seq * FNET_GROUP_W)
    m_a = jnp.asarray(np.block([[c1, -s1], [s1, c1]]) * norm, bf16)
    c2, s2 = _dft_cs(GRID_W)
    m_c = jnp.asarray(np.concatenate([c2, -s2], 1), bf16)
    cs = jnp.asarray(_channel_dft(), bf16)
    phi = 2.0 * np.pi * np.outer(np.arange(rows), np.arange(GRID_W)) / seq
    cphi = jnp.repeat(jnp.asarray(np.cos(phi), f32), BRANCH_W, axis=1)
    sphi = jnp.repeat(jnp.asarray(np.sin(phi), f32), BRANCH_W, axis=1)
    xv = P_f[:n_batch * seq].reshape(n_batch * rows, GRID_W, BRANCH_W)
    nj = GRID_W // FNET_SUB
    full = lambda a: pl.BlockSpec(a.shape, lambda b, j: (0,) * a.ndim)
    tw = pl.BlockSpec((rows, FNET_SUB * BRANCH_W), lambda b, j: (0, j))
    mid = jax.ShapeDtypeStruct((n_batch * GRID_W, rows, BRANCH_W), f32)
    br, bi = pl.pallas_call(
        functools.partial(_fnet_a_kernel, n_rows=rows),
        grid=(n_batch, nj),
        in_specs=[pl.BlockSpec((rows, FNET_SUB, BRANCH_W), lambda b, j: (b, j, 0)), full(cs), full(m_a), tw, tw],
        out_specs=[pl.BlockSpec((FNET_SUB, rows, BRANCH_W), lambda b, j: (b * nj + j, 0, 0))] * 2,
        out_shape=[mid, mid],
        name="fnet_rows",
    )(xv, cs, m_a, cphi, sphi)
    blk = pl.BlockSpec((GRID_W, FNET_SUB, BRANCH_W), lambda b, i: (b, i, 0))
    y = pl.pallas_call(
        _fnet_c_kernel,
        grid=(n_batch, rows // FNET_SUB),
        in_specs=[blk, blk, pl.BlockSpec(m_c.shape, lambda b, i: (0, 0))],
        out_specs=blk,
        out_shape=mid,
        name="fnet_cols",
    )(br, bi, m_c)
    return y.reshape(n_batch * seq, BRANCH_W)


def _fnet_ctx_kernel(x_ref, cs_ref, m_ref, o_ref):
    u = jnp.dot(x_ref[...], cs_ref[...], preferred_element_type=jnp.float32)
    st = jnp.concatenate([u[:, :BRANCH_W], u[:, BRANCH_W:]], 0).astype(jnp.bfloat16)
    o_ref[...] = jnp.dot(m_ref[...], st, preferred_element_type=jnp.float32).astype(o_ref.dtype)


def fourier_ctx_call(P_f, *, n_batch, seq, ctx_len):
    bf16 = jnp.bfloat16
    c, s = _dft_cs(ctx_len)
    m = jnp.asarray(np.concatenate([c, -s], 1) / math.sqrt(ctx_len * FNET_GROUP_W), bf16)
    cs = jnp.asarray(_channel_dft(), bf16)
    cb = n_batch * seq // ctx_len
    return pl.pallas_call(
        _fnet_ctx_kernel,
        grid=(n_batch,),
        in_specs=[pl.BlockSpec((ctx_len, BRANCH_W), lambda b: (cb + b, 0)),
                  pl.BlockSpec(cs.shape, lambda b: (0, 0)), pl.BlockSpec(m.shape, lambda b: (0, 0))],
        out_specs=pl.BlockSpec((ctx_len, BRANCH_W), lambda b: (b, 0)),
        out_shape=jax.ShapeDtypeStruct((n_batch * ctx_len, BRANCH_W), jnp.float32),
        name="fnet_ctx",
    )(P_f, cs, m)


def kernel(x, c, ctx, c_ctx, w_mod, b_mod, norm1_g, norm2_g, w_in, na_qn_g, na_kn_g, na_rpb,
           da_qn_g, da_kn_g, da_lam_q1, da_lam_k1, da_lam_q2, da_lam_k2, da_subln_g, gdn_conv_w,
           gdn_a_log, gdn_dt_bias, gdn_onorm_g, w_branch, w_out, w_router, router_bias,
           moe_w_gate, moe_w_up, moe_w_down):
    B, S, D = x.shape
    L = ctx.shape[1]
    NL = B * S
    f32 = jnp.float32
    cvec = jnp.zeros((8, D), f32).at[:B].set(c).at[B].set(c_ctx)
    mods = mod_call(cvec, w_mod, b_mod)
    cos, sin = rope_tables(S, 256)
    bf16 = jnp.bfloat16
    Xl, Xc = x.reshape(NL, D), ctx.reshape(B * L, D)
    for l in range(DEPTH):
        last = l == DEPTH - 1
        lam_init = 0.8 - 0.6 * math.exp(-0.3 * l)
        w_main = jnp.concatenate([w_in[l][:, :2560], w_in[l][:, 2576:]], 1).astype(jnp.bfloat16)
        w_ab = jnp.pad(w_in[l][:, 2560:2576], ((0, 0), (0, 112))).astype(jnp.bfloat16)
        P_na, P_da, P_gdn, P_z, P_f, P_gate, P_ab = inproj_call(Xl, Xc, 0, mods[l], norm1_g[l], w_main, w_ab,
                                                                n_batch=B, seq=S, ctx_len=L)
        NAQ, NAK, DAQ, DAK, DAVX = prep_call(P_na, P_da, cos, sin, na_qn_g[l], na_kn_g[l], da_qn_g[l], da_kn_g[l],
                                             n_batch=B, seq=S)
        bias = na_bias_tables(na_rpb[l], S // GRID_W)
        lam = (jnp.exp(jnp.sum(da_lam_q1[l] * da_lam_k1[l])) - jnp.exp(jnp.sum(da_lam_q2[l] * da_lam_k2[l]))
               + lam_init).astype(f32)
        Ya = na_call(NAQ, NAK, P_na, bias, n_batch=B, seq=S, ctx_len=L)
        Yd = da_call(lam, DAQ, DAK, DAVX, da_subln_g[l], n_batch=B, seq=S, ctx_len=L, lam_init=lam_init)
        QE, OU, MM, NN = gdn_chunk_call(P_gdn, P_ab, gdn_conv_w[l], gdn_a_log[l], gdn_dt_bias[l], n_batch=B, seq=S)
        Of, Ob = gdn_scan_call(QE, OU, MM, NN, n_batch=B, seq=S)
        Yf = fourier_latent_call(P_f, n_batch=B, seq=S)
        n_tok = NL
        if not last:
            Yac, Ydc = ctx_attn_call(lam, NAQ, NAK, P_na, DAQ, DAK, DAVX, da_subln_g[l],
                                     n_batch=B, seq=S, ctx_len=L, lam_init=lam_init)
            Ya = jnp.concatenate([Ya, Yac], 0)
            Yd = jnp.concatenate([Yd, Ydc], 0)
            Yf = jnp.concatenate([Yf, fourier_ctx_call(P_f, n_batch=B, seq=S, ctx_len=L)], 0)
            n_tok = NL + B * L
        Xmid, H2, RW = merge_call(Xl, Xc, 0, Ya, Yd, Of, Ob, P_z, Yf, P_gate, mods[l], norm2_g[l], gdn_onorm_g[l],
                                  w_branch[l].astype(bf16), w_out[l].astype(bf16), w_router, router_bias,
                                  n_tok=n_tok, n_batch=B, seq=S)
        experts = (moe_w_gate[l].astype(bf16), moe_w_up[l].astype(bf16), moe_w_down[l].astype(bf16))
        Xl = moe_sorted_call(Xmid, H2, RW, mods[l], *experts, row0=0, n_rows=NL, tile=MOE_TILE, win=MOE_WIN,
                             mod_row_fn=lambda i: i // (S // MOE_TILE))
        if not last:
            Xc = moe_sorted_call(Xmid, H2, RW, mods[l], *experts, row0=NL, n_rows=B * L, tile=B * L,
                                 win=MOE_WIN_CTX, mod_row_fn=lambda i: B)
    return Xl.reshape(B, S, D)
```

```python
import functools
import math
import jax
import jax.numpy as jnp
from jax import lax
import numpy as np
from jax.experimental import pallas as pl
from jax.experimental.pallas import tpu as pltpu

D_MODEL = 1024
DEPTH = 2
GRID_W = 64
N_BRANCH = 4
BRANCH_W = D_MODEL // N_BRANCH
GDN_HEADS = 4
GDN_DIM = BRANCH_W // GDN_HEADS
GDN_CHUNK = 64
FNET_GROUPS = 4
FNET_GROUP_W = BRANCH_W // FNET_GROUPS
N_EXPERTS = 16
N_GROUPS = 4
RMS_EPS = 1e-6
NEG_INF = -1e30
ROPE_BASE = 10000.0
VMEM_LIMIT = 56 * 1024 * 1024
HI = lax.Precision.HIGHEST
NT_DIMS = (((1,), (1,)), ((), ()))
LOG2E = 1.4426950408889634


def _mod_kernel(c_ref, w_ref, b_ref, o_ref):
    c = c_ref[...]
    a = c * jax.nn.sigmoid(c)
    o_ref[0] = jnp.dot(a, w_ref[0], preferred_element_type=jnp.float32, precision=HI) + b_ref[0]


def mod_call(cvec, w_mod, b_mod, tn=512):
    depth, D, N = w_mod.shape
    return pl.pallas_call(
        _mod_kernel,
        grid=(depth, N // tn),
        in_specs=[pl.BlockSpec((8, D), lambda l, j: (0, 0)),
                  pl.BlockSpec((1, D, tn), lambda l, j: (l, 0, j)),
                  pl.BlockSpec((1, 1, tn), lambda l, j: (l, 0, j))],
        out_specs=pl.BlockSpec((1, 8, tn), lambda l, j: (l, 0, j)),
        out_shape=jax.ShapeDtypeStruct((depth, 8, N), jnp.float32),
        name="mod",
    )(cvec, w_mod, b_mod.reshape(depth, 1, N))


def _stream_specs(n_lat_tiles, ctx_tile0, tm, width):
    return [pl.BlockSpec((tm, width), lambda i: (jnp.minimum(i, n_lat_tiles - 1), 0)),
            pl.BlockSpec((tm, width), lambda i: (ctx_tile0 + jnp.maximum(i - n_lat_tiles, 0), 0))]


def _inproj_kernel(xl_ref, xc_ref, m_ref, g_ref, w_ref, wab_ref, cos_ref, sin_ref, g64_ref, g32_ref, gains_ref,
                   naq_ref, nak_ref, nav_ref, daq_ref, dak_ref, davx_ref, gdn_ref, z_ref, f_ref, gate_ref, ab_ref,
                   *, tiles_per_batch, n_lat_tiles, n_batch):
    i = pl.program_id(0)
    r = jnp.where(i < n_lat_tiles, i // tiles_per_batch, n_batch)
    x = jnp.where(i < n_lat_tiles, xl_ref[...], xc_ref[...])
    y = x * lax.rsqrt(jnp.mean(x * x, axis=-1, keepdims=True) + RMS_EPS) * g_ref[...]
    sh = m_ref[pl.ds(r, 1), 0:D_MODEL]
    sc = m_ref[pl.ds(r, 1), D_MODEL:2 * D_MODEL]
    h = (y * (1.0 + sc) + sh).astype(jnp.bfloat16)
    proj = lambda a, b: jnp.dot(h, w_ref[:, a:b], preferred_element_type=jnp.float32)
    bf16 = jnp.bfloat16

    g64 = g64_ref[...]
    naq_ref[...] = (_group_rms(proj(0, 256), g64, 64, gains_ref[0:1, :]) * (64 ** -0.5)).astype(bf16)
    nak_ref[...] = _group_rms(proj(256, 512), g64, 64, gains_ref[1:2, :]).astype(bf16)
    nav_ref[...] = proj(512, 768).astype(bf16)

    g32 = g32_ref[...]
    cos = cos_ref[...]
    sin = sin_ref[...]
    lane = lax.broadcasted_iota(jnp.int32, (1, 256), 1)
    first = (lane % 16) < 8

    def rope(t):
        swapped = jnp.where(first, pltpu.roll(t, 256 - 8, 1), pltpu.roll(t, 8, 1))
        return t * cos + swapped * sin

    q = rope(_group_rms(proj(768, 1024), g32, 32, gains_ref[2:3, :]))
    k = rope(_group_rms(proj(1024, 1280), g32, 32, gains_ref[3:4, :]))
    daq_ref[...] = (q * (32 ** -0.5 * LOG2E)).astype(bf16)
    dak_ref[...] = k.astype(bf16)
    v = proj(1280, 1536).astype(bf16)
    one_col = (lax.broadcasted_iota(jnp.int32, (v.shape[0], 64), 1) == 0).astype(bf16)
    davx_ref[...] = jnp.concatenate(
        [piece for hd in range(4) for piece in (v[:, hd * 64:(hd + 1) * 64], one_col)], -1)

    gdn_ref[...] = proj(1536, 2304)
    z_ref[...] = proj(2304, 2560).astype(bf16)
    f_ref[...] = proj(2560, 2816).astype(bf16)
    for kk in range(4):
        gate_ref[:, kk * 1024:(kk + 1) * 1024] = proj(2816 + kk * 1024, 2816 + (kk + 1) * 1024).astype(bf16)
    ab_ref[...] = jnp.dot(h, wab_ref[...], preferred_element_type=jnp.float32)


def inproj_call(Xl, Xc, ctx_row0, mod_l, g, w_main, w_ab, cos, sin, na_qg, na_kg, da_qg, da_kg,
                *, n_batch, seq, ctx_len, tm=256):
    D = Xl.shape[1]
    NT = n_batch * (seq + ctx_len)
    tpb = seq // tm
    n_lat_tiles = n_batch * tpb
    kern = functools.partial(_inproj_kernel, tiles_per_batch=tpb, n_lat_tiles=n_lat_tiles, n_batch=n_batch)
    bf16, f32 = jnp.bfloat16, jnp.float32
    widths = [(256, bf16)] * 5 + [(512, bf16), (768, f32), (256, bf16), (256, bf16), (4096, bf16), (128, f32)]
    gains = jnp.stack([jnp.tile(na_qg, 4), jnp.tile(na_kg, 4), jnp.tile(da_qg, 8), jnp.tile(da_kg, 8)]
                      + [jnp.zeros((256,), f32)] * 4, 0)
    g64, g32 = _group_ones(256, 64), _group_ones(256, 32)
    tab = pl.BlockSpec((tm, 256), lambda i: (jnp.where(i < n_lat_tiles, i % tpb, tpb), 0))
    full = lambda a: pl.BlockSpec(a.shape, lambda i: (0,) * a.ndim)
    return pl.pallas_call(
        kern,
        grid=(NT // tm,),
        in_specs=_stream_specs(n_lat_tiles, ctx_row0 // tm, tm, D) + [
            full(mod_l), pl.BlockSpec((1, D), lambda i: (0, 0)), full(w_main), full(w_ab), tab, tab,
            full(g64), full(g32), full(gains)],
        out_specs=[pl.BlockSpec((tm, w), lambda i: (i, 0)) for w, _ in widths],
        out_shape=[jax.ShapeDtypeStruct((NT, w), dt) for w, dt in widths],
        compiler_params=pltpu.CompilerParams(vmem_limit_bytes=VMEM_LIMIT),
        name="inproj",
    )(Xl, Xc, mod_l, g.reshape(1, D), w_main, w_ab, cos, sin, g64, g32, gains)


def _group_ones(width, group):
    i = np.arange(width)
    return jnp.asarray((i[:, None] // group == i[None, :] // group).astype(np.float32), jnp.bfloat16)


def _group_sum(xx, gmat):
    hi, lo = _split_bf16(xx)
    return (jnp.dot(hi, gmat, preferred_element_type=jnp.float32)
            + jnp.dot(lo, gmat, preferred_element_type=jnp.float32))


def _group_rms(x, gmat, group, gain):
    return x * lax.rsqrt(_group_sum(x * x, gmat) * (1.0 / group) + RMS_EPS) * gain


def rope_tables(seq, tm):
    nf = 8
    t = jnp.arange(seq)
    rows = (t // GRID_W).astype(jnp.float32)
    cols = (t % GRID_W).astype(jnp.float32)
    freqs = ROPE_BASE ** (-jnp.arange(nf, dtype=jnp.float32) / nf)
    d = np.arange(32)
    f_idx = d % 8
    use_col = (d // 16) == 1
    ang = jnp.where(use_col[None, :], cols[:, None], rows[:, None]) * freqs[f_idx][None, :]
    sign = np.where((d % 16) < 8, -1.0, 1.0).astype(np.float32)
    cos = jnp.tile(jnp.cos(ang), (1, 8))
    sin = jnp.tile(jnp.sin(ang) * sign[None, :], (1, 8))
    cos = jnp.concatenate([cos, jnp.ones((tm, 256), jnp.float32)], 0)
    sin = jnp.concatenate([sin, jnp.zeros((tm, 256), jnp.float32)], 0)
    return cos, sin


NA_RB = 4
NA_UR = 11


def na_bias_tables(rpb, rows):
    reps = [(0, 0), (4, 0), (rows - NA_RB, rows - NA_UR)]
    qc = np.arange(GRID_W)[:, None]
    kc = np.arange(GRID_W)[None, :]
    ws = np.clip(qc - 8, 0, GRID_W - 16)
    vcol = (kc >= ws) & (kc < ws + 16)
    rel_c = np.clip(kc - qc + 15, 0, 30)
    sel_c = (rel_c[..., None] == np.arange(31)).astype(np.float32)
    sel_r, vrow = [], []
    for r0, u0 in reps:
        r = r0 + np.arange(NA_RB)[:, None]
        krow = u0 + np.arange(NA_UR)[None, :]
        start = np.clip(r - 4, 0, rows - 8)
        vrow.append((krow >= start) & (krow < start + 8))
        sel_r.append((np.clip(krow - r + 7, 0, 14)[..., None] == np.arange(15)).astype(np.float32))
    sel_r, vrow = np.stack(sel_r), np.stack(vrow)
    b = jnp.einsum('taui,hij,qkj->thaquk', jnp.asarray(sel_r), rpb.astype(jnp.float32), jnp.asarray(sel_c),
                   precision=HI)
    valid = vrow[:, None, :, None, :, None] & vcol[None, None, None, :, None, :]
    b = jnp.where(valid, b, NEG_INF)
    return b.reshape(3, rpb.shape[0], NA_RB * GRID_W, NA_UR * GRID_W).astype(jnp.bfloat16)


def _na_kernel(q_ref, k_ref, v_ref, kc_ref, vc_ref, bias_ref, o_ref, *, rows):
    i = pl.program_id(1)
    u0 = jnp.clip(i * NA_RB - 4, 0, rows - NA_UR)
    off = pl.multiple_of(u0 * GRID_W, GRID_W)
    nkw = NA_UR * GRID_W
    kwin = k_ref[pl.ds(off, nkw), :]
    vwin = v_ref[pl.ds(off, nkw), :]
    q = q_ref[...]
    kc = kc_ref[...]
    vc = vc_ref[...]
    outs = []
    for h in range(4):
        sl = slice(h * 64, (h + 1) * 64)
        qh = q[:, sl]
        s_loc = lax.dot_general(qh, kwin[:, sl], NT_DIMS, preferred_element_type=jnp.float32) + bias_ref[0, h].astype(jnp.float32)
        s_ctx = lax.dot_general(qh, kc[:, sl], NT_DIMS, preferred_element_type=jnp.float32)
        m = jnp.maximum(jnp.max(s_loc, -1, keepdims=True), jnp.max(s_ctx, -1, keepdims=True))
        p_loc = jnp.exp(s_loc - m)
        p_ctx = jnp.exp(s_ctx - m)
        l = jnp.sum(p_loc, -1, keepdims=True) + jnp.sum(p_ctx, -1, keepdims=True)
        o = (jnp.dot(p_loc.astype(jnp.bfloat16), vwin[:, sl], preferred_element_type=jnp.float32)
             + jnp.dot(p_ctx.astype(jnp.bfloat16), vc[:, sl], preferred_element_type=jnp.float32))
        outs.append(o / l)
    o_ref[...] = jnp.concatenate(outs, -1).astype(o_ref.dtype)


def na_call(NAQ, NAK, NAV, bias, *, n_batch, seq, ctx_len):
    rows = seq // GRID_W
    nblk = rows // NA_RB
    tq = NA_RB * GRID_W
    cb = n_batch * seq // ctx_len
    kern = functools.partial(_na_kernel, rows=rows)
    tsel = lambda b, i: (jnp.where(i == 0, 0, jnp.where(i == nblk - 1, 2, 1)), 0, 0, 0)
    return pl.pallas_call(
        kern,
        grid=(n_batch, nblk),
        in_specs=[pl.BlockSpec((tq, 256), lambda b, i: (b * nblk + i, 0)),
                  pl.BlockSpec((seq, 256), lambda b, i: (b, 0)),
                  pl.BlockSpec((seq, 256), lambda b, i: (b, 0)),
                  pl.BlockSpec((ctx_len, 256), lambda b, i: (cb + b, 0)),
                  pl.BlockSpec((ctx_len, 256), lambda b, i: (cb + b, 0)),
                  pl.BlockSpec((1,) + bias.shape[1:], tsel)],
        out_specs=pl.BlockSpec((tq, 256), lambda b, i: (b * nblk + i, 0)),
        out_shape=jax.ShapeDtypeStruct((n_batch * seq, 256), jnp.bfloat16),
        compiler_params=pltpu.CompilerParams(vmem_limit_bytes=VMEM_LIMIT),
        name="na_attn",
    )(NAQ, NAK, NAV, NAK, NAV, bias)


def _stack_masked_q(q):
    lane = lax.broadcasted_iota(jnp.int32, (1, 256), 1)
    return jnp.concatenate([jnp.where((lane // 32) == hm, q, jnp.zeros_like(q)) for hm in range(8)], 0)


def _da_finish(acc, lam, gain, g64, post_scale):
    tq = acc.shape[0] // 8
    norm = acc / jnp.broadcast_to(acc[:, 64:65], acc.shape)
    lane = lax.broadcasted_iota(jnp.int32, (1, 128), 1)
    halves = []
    for hp in range(2):
        d = [norm[(4 * hp + 2 * j) * tq:(4 * hp + 2 * j + 1) * tq]
             - lam * norm[(4 * hp + 2 * j + 1) * tq:(4 * hp + 2 * j + 2) * tq] for j in range(2)]
        halves.append(jnp.where(lane < 64, d[0], pltpu.roll(d[1], 64, 1)))
    o = jnp.concatenate(halves, -1)
    return _group_rms(o, g64, 64, gain) * post_scale


def _da_tile(qs, k, vx, m, acc, tq):
    tk = k.shape[0]
    s = lax.dot_general(qs, k, NT_DIMS, preferred_element_type=jnp.float32)
    m_new = jnp.maximum(m, jnp.max(s, -1, keepdims=True))
    alpha = jnp.exp2(m - m_new)
    p = jnp.exp2(s - pltpu.repeat(m_new, tk // 128, axis=1)).astype(jnp.bfloat16)
    pv = jnp.concatenate(
        [jnp.dot(p[2 * h * tq:(2 * h + 2) * tq], vx[:, h * 128:(h + 1) * 128], preferred_element_type=jnp.float32)
         for h in range(4)], 0)
    return m_new, alpha * acc + pv


def _da_kernel(lam_ref, q_ref, k_ref, vx_ref, kc_ref, vxc_ref, gain_ref, g64_ref, o_ref, m_ref, acc_ref,
               *, tk, n_kv, post_scale):
    tq = q_ref.shape[0]
    qs = _stack_masked_q(q_ref[...])
    m_ref[...] = jnp.full(m_ref.shape, -jnp.inf, jnp.float32)
    acc_ref[...] = jnp.zeros(acc_ref.shape, jnp.float32)

    def body(j, carry):
        for u in range(2):
            off = pl.multiple_of((2 * j + u) * tk, tk)
            m, acc = _da_tile(qs, k_ref[pl.ds(off, tk), :], vx_ref[pl.ds(off, tk), :], m_ref[...], acc_ref[...], tq)
            m_ref[...] = m
            acc_ref[...] = acc
        return carry

    lax.fori_loop(0, n_kv // 2 - 1, body, 0, unroll=True)
    off = (n_kv - 2) * tk
    m, acc = _da_tile(qs, k_ref[off:off + tk, :], vx_ref[off:off + tk, :], m_ref[...], acc_ref[...], tq)
    k_last = jnp.concatenate([k_ref[off + tk:off + 2 * tk, :], kc_ref[...]], 0)
    vx_last = jnp.concatenate([vx_ref[off + tk:off + 2 * tk, :], vxc_ref[...]], 0)
    m, acc = _da_tile(qs, k_last, vx_last, m, acc, tq)
    o_ref[...] = _da_finish(acc, lam_ref[0], gain_ref[...], g64_ref[...], post_scale).astype(o_ref.dtype)


def da_call(lam, DAQ, DAK, DAVX, subln_g, *, n_batch, seq, ctx_len, lam_init, tq=256, tk=1024):
    tk = min(tk, seq // 2)
    nq = seq // tq
    cb = n_batch * seq // ctx_len
    g64 = _group_ones(256, 64)
    gain = jnp.tile(subln_g, 4).reshape(1, 256)
    n_kv = seq // tk
    assert n_kv % 2 == 0
    kern = functools.partial(_da_kernel, tk=tk, n_kv=n_kv, post_scale=1.0 - lam_init)
    return pl.pallas_call(
        kern,
        grid=(n_batch, nq),
        in_specs=[pl.BlockSpec(memory_space=pltpu.SMEM),
                  pl.BlockSpec((tq, 256), lambda b, i: (b * nq + i, 0)),
                  pl.BlockSpec((seq, 256), lambda b, i: (b, 0)),
                  pl.BlockSpec((seq, 512), lambda b, i: (b, 0)),
                  pl.BlockSpec((ctx_len, 256), lambda b, i: (cb + b, 0)),
                  pl.BlockSpec((ctx_len, 512), lambda b, i: (cb + b, 0)),
                  pl.BlockSpec((1, 256), lambda b, i: (0, 0)),
                  pl.BlockSpec((256, 256), lambda b, i: (0, 0))],
        out_specs=pl.BlockSpec((tq, 256), lambda b, i: (b * nq + i, 0)),
        out_shape=jax.ShapeDtypeStruct((n_batch * seq, 256), jnp.bfloat16),
        scratch_shapes=[pltpu.VMEM((8 * tq, 128), jnp.float32), pltpu.VMEM((8 * tq, 128), jnp.float32)],
        compiler_params=pltpu.CompilerParams(vmem_limit_bytes=VMEM_LIMIT),
        name="da_attn",
    )(lam.reshape(1), DAQ, DAK, DAVX, DAK, DAVX, gain, g64)


def _ctx_kernel(lam_ref, naq_ref, nak_ref, nav_ref, daq_ref, dak_ref, dav_ref, gain_ref, g64_ref, ya_ref, yd_ref,
                *, post_scale):
    q = naq_ref[...]
    k = nak_ref[...]
    v = nav_ref[...]
    outs = []
    for h in range(4):
        sl = slice(h * 64, (h + 1) * 64)
        s = lax.dot_general(q[:, sl], k[:, sl], NT_DIMS, preferred_element_type=jnp.float32)
        p = jnp.exp(s - jnp.max(s, -1, keepdims=True))
        o = jnp.dot(p.astype(jnp.bfloat16), v[:, sl], preferred_element_type=jnp.float32)
        outs.append(o / jnp.sum(p, -1, keepdims=True))
    ya_ref[...] = jnp.concatenate(outs, -1).astype(ya_ref.dtype)
    tq = daq_ref.shape[0]
    qs = _stack_masked_q(daq_ref[...])
    m0 = jnp.full((8 * tq, 128), -jnp.inf, jnp.float32)
    m, acc = _da_tile(qs, dak_ref[...], dav_ref[...], m0, jnp.zeros((8 * tq, 128), jnp.float32), tq)
    yd_ref[...] = _da_finish(acc, lam_ref[0], gain_ref[...], g64_ref[...], post_scale).astype(yd_ref.dtype)


def ctx_attn_call(lam, NAQ, NAK, NAV, DAQ, DAK, DAVX, subln_g, *, n_batch, seq, ctx_len, lam_init):
    cb = n_batch * seq // ctx_len
    g64 = _group_ones(256, 64)
    gain = jnp.tile(subln_g, 4).reshape(1, 256)
    blk = lambda col: pl.BlockSpec((ctx_len, 256), lambda b: (cb + b, col))
    return pl.pallas_call(
        functools.partial(_ctx_kernel, post_scale=1.0 - lam_init),
        grid=(n_batch,),
        in_specs=[pl.BlockSpec(memory_space=pltpu.SMEM), blk(0), blk(0), blk(0), blk(0), blk(0),
                  pl.BlockSpec((ctx_len, 512), lambda b: (cb + b, 0)),
                  pl.BlockSpec((1, 256), lambda b: (0, 0)), pl.BlockSpec((256, 256), lambda b: (0, 0))],
        out_specs=[pl.BlockSpec((ctx_len, 256), lambda b: (b, 0))] * 2,
        out_shape=[jax.ShapeDtypeStruct((n_batch * ctx_len, 256), jnp.bfloat16)] * 2,
        name="ctx_attn",
    )(lam.reshape(1), NAQ, NAK, NAV, DAQ, DAK, DAVX, gain, g64)


def _split_bf16(a):
    hi = a.astype(jnp.bfloat16)
    return hi, (a - hi.astype(jnp.float32)).astype(jnp.bfloat16)


def _route(logits_t, bias_col):
    per_group = N_EXPERTS // N_GROUPS
    scores = jax.nn.sigmoid(logits_t)
    sel_all = scores + bias_col
    sel = [sel_all[e:e + 1, :] for e in range(N_EXPERTS)]
    top2 = []
    for e in range(N_EXPERTS):
        g0 = e // per_group * per_group
        rank = jnp.zeros_like(sel[e])
        for o in range(g0, g0 + per_group):
            if o != e:
                beats = (sel[o] > sel[e]) | (sel[o] == sel[e]) if o < e else (sel[o] > sel[e])
                rank = rank + beats.astype(jnp.float32)
        top2.append(rank < 1.5)
    gs = []
    for g in range(N_GROUPS):
        acc = jnp.zeros_like(sel[0])
        for e in range(g * per_group, (g + 1) * per_group):
            acc = acc + jnp.where(top2[e], sel[e], 0.0)
        gs.append(acc)
    rows = []
    for g in range(N_GROUPS):
        beaten = jnp.zeros_like(sel[0])
        for o in range(N_GROUPS):
            if o != g:
                b = (gs[o] >= gs[g]) if o < g else (gs[o] > gs[g])
                beaten = beaten + b.astype(jnp.float32)
        best = beaten < 0.5
        for e in range(g * per_group, (g + 1) * per_group):
            rows.append(jnp.where(best & top2[e], scores[e:e + 1, :], 0.0))
    w = jnp.concatenate(rows, 0)
    return w / jnp.sum(w, axis=0, keepdims=True)


def _merge_kernel(xl_ref, xc_ref, ya_ref, yd_ref, of_ref, ob_ref, z_ref, yf_ref, gate_ref, m_ref, g2_ref, og_ref, g64_ref,
                  wb_ref, wo_ref, wr_ref, rb_ref, xo_ref, h2_ref, rw_ref, *, tiles_per_batch, n_lat_tiles, n_batch):
    i = pl.program_id(0)
    r = jnp.where(i < n_lat_tiles, i // tiles_per_batch, n_batch)
    z = z_ref[...].astype(jnp.float32)
    yg = (_group_rms(of_ref[...] + ob_ref[...], g64_ref[...], GDN_DIM, og_ref[...]) * (z * jax.nn.sigmoid(z)))
    branches = (ya_ref[...], yd_ref[...], yg.astype(jnp.bfloat16), yf_ref[...].astype(jnp.bfloat16))
    mix = None
    for b, yb in enumerate(branches):
        proj = jnp.dot(yb, wb_ref[b], preferred_element_type=jnp.float32)
        term = jax.nn.sigmoid(gate_ref[:, b * D_MODEL:(b + 1) * D_MODEL].astype(jnp.float32)) * proj
        mix = term if mix is None else mix + term
    y = jnp.dot(mix.astype(jnp.bfloat16), wo_ref[...], preferred_element_type=jnp.float32)
    mrow = lambda k: m_ref[pl.ds(r, 1), k * D_MODEL:(k + 1) * D_MODEL]
    x = jnp.where(i < n_lat_tiles, xl_ref[...], xc_ref[...]) + mrow(2) * y
    xo_ref[...] = x
    h2 = (x * lax.rsqrt(jnp.mean(x * x, axis=-1, keepdims=True) + RMS_EPS) * g2_ref[...]) * (1.0 + mrow(4)) + mrow(3)
    h2_ref[...] = h2.astype(h2_ref.dtype)
    h_hi, h_lo = _split_bf16(h2)
    w_hi, w_lo = _split_bf16(wr_ref[...])
    nt = lambda a, b: lax.dot_general(a, b, NT_DIMS, preferred_element_type=jnp.float32)
    logits_t = nt(w_hi, h_hi) + nt(w_hi, h_lo) + nt(w_lo, h_hi)
    w_t = _route(logits_t, rb_ref[...])
    pad = jnp.zeros((128 - N_EXPERTS, w_t.shape[1]), jnp.float32)
    rw_ref[...] = jnp.transpose(jnp.concatenate([w_t, pad], 0))


def merge_call(Xl, Xc, ctx_row0, Ya, Yd, Of, Ob, P_z, Yf, P_gate, mod_l, g2n, onorm_g, w_branch, w_out, w_router,
               router_bias, *, n_tok, n_batch, seq, tm=256):
    D = Xl.shape[1]
    n_lat_tiles = n_batch * seq // tm
    kern = functools.partial(_merge_kernel, tiles_per_batch=seq // tm, n_lat_tiles=n_lat_tiles, n_batch=n_batch)
    row = lambda w: pl.BlockSpec((tm, w), lambda i: (i, 0))
    full = lambda a: pl.BlockSpec(a.shape, lambda i: (0,) * a.ndim)
    wr_t = jnp.transpose(w_router)
    rb = router_bias.reshape(N_EXPERTS, 1)
    g2 = g2n.reshape(1, D)
    og = jnp.tile(onorm_g, GDN_HEADS).reshape(1, BRANCH_W)
    g64 = _group_ones(BRANCH_W, GDN_DIM)
    return pl.pallas_call(
        kern,
        grid=(n_tok // tm,),
        in_specs=_stream_specs(n_lat_tiles, ctx_row0 // tm, tm, D) + [
                  row(256), row(256), row(256), row(256), row(256), row(256), row(4 * D), full(mod_l),
                  full(g2), full(og), full(g64), full(w_branch), full(w_out), full(wr_t), full(rb)],
        out_specs=[row(D), row(D), row(128)],
        out_shape=[jax.ShapeDtypeStruct((n_tok, D), jnp.float32), jax.ShapeDtypeStruct((n_tok, D), jnp.bfloat16),
                   jax.ShapeDtypeStruct((n_tok, 128), jnp.float32)],
        compiler_params=pltpu.CompilerParams(vmem_limit_bytes=VMEM_LIMIT),
        name="merge",
    )(Xl, Xc, Ya, Yd, Of, Ob, P_z, Yf, P_gate, mod_l, g2, og, g64, w_branch, w_out, wr_t, rb)


MOE_TILE = 1024
MOE_WIN = 320
MOE_WIN_CTX = 176
MOE_EPS = 4


def _moe_sorted_kernel(x_ref, h_ref, rw_ref, m_ref, wg_ref, wu_ref, wd_ref, o_ref, xs_ref, ws_ref, ys_ref, dest_ref,
                       seg_ref, *, win, mod_row_fn):
    T = h_ref.shape[0]
    per_group = N_EXPERTS // N_GROUPS
    i = pl.program_id(0)
    step = pl.program_id(1)
    lane = lax.broadcasted_iota(jnp.int32, (1, 128), 1)
    f32, bf16 = jnp.float32, jnp.bfloat16

    @pl.when(step == 0)
    def _():
        rw = rw_ref[...]
        r128 = lax.broadcasted_iota(jnp.int32, (128, 128), 0)
        c128 = lax.broadcasted_iota(jnp.int32, (128, 128), 1)
        e2g = ((r128 // per_group == c128) & (r128 < N_EXPERTS)).astype(bf16)
        og = (jnp.dot((rw > 0.0).astype(bf16), e2g, preferred_element_type=f32) > 0.5).astype(f32)
        rt = lax.broadcasted_iota(jnp.int32, (T, T), 0)
        ct = lax.broadcasted_iota(jnp.int32, (T, T), 1)
        earlier = jnp.dot((rt > ct).astype(bf16), og.astype(bf16), preferred_element_type=f32)
        cnt_row = jnp.sum(og, axis=0, keepdims=True)
        start_row = jnp.dot(jnp.broadcast_to(cnt_row, (8, 128)), (r128 < c128).astype(f32),
                            preferred_element_type=f32, precision=HI)[0:1]
        dest = jnp.sum(og * (earlier + start_row), axis=-1, keepdims=True)
        dest_ref[...] = jnp.broadcast_to(dest, (T, 128))
        dest_row = jnp.transpose(dest_ref[...])[0:1, :]
        perm = (dest_row == rt.astype(f32)).astype(bf16)
        xs_ref[...] = jnp.dot(perm, h_ref[...], preferred_element_type=f32).astype(bf16)
        ws_ref[...] = sum(jnp.dot(perm, piece, preferred_element_type=f32) for piece in _split_bf16(rw))
        ys_ref[...] = jnp.zeros(ys_ref.shape, f32)
        for g in range(N_GROUPS):
            seg_ref[g] = jnp.sum(jnp.where(lane == g, start_row, 0.0)).astype(jnp.int32)
            seg_ref[N_GROUPS + g] = jnp.sum(jnp.where(lane == g, cnt_row, 0.0)).astype(jnp.int32)

    g = (step * MOE_EPS) // per_group
    start = seg_ref[g]
    end = start + seg_ref[N_GROUPS + g]
    s0 = (start // 16) * 16
    n_win = (end - s0 + win - 1) // win

    def window(w, carry):
        lo = s0 + w * win
        off = pl.multiple_of(jnp.minimum(lo, T - win), 16)
        xw = xs_ref[pl.ds(off, win), :]
        ww = ws_ref[pl.ds(off, win), :]
        row = off + lax.broadcasted_iota(jnp.int32, (win, 1), 0)
        y = None
        for j in range(MOE_EPS):
            w_col = jnp.sum(jnp.where(lane == step * MOE_EPS + j, ww, 0.0), axis=-1, keepdims=True)
            w_col = jnp.where(row >= lo, w_col, 0.0)
            a = jnp.dot(xw, wg_ref[j], preferred_element_type=f32)
            u = jnp.dot(xw, wu_ref[j], preferred_element_type=f32)
            act = (a * jax.nn.sigmoid(a) * u * w_col).astype(bf16)
            yj = jnp.dot(act, wd_ref[j], preferred_element_type=f32)
            y = yj if y is None else y + yj
        ys_ref[pl.ds(off, win), :] += y
        return carry

    lax.fori_loop(0, n_win, window, 0)

    @pl.when(step == N_EXPERTS // MOE_EPS - 1)
    def _():
        ct = lax.broadcasted_iota(jnp.int32, (T, T), 1)
        unperm = (pltpu.repeat(dest_ref[...], T // 128, axis=1) == ct.astype(f32)).astype(bf16)
        f = jnp.dot(unperm, ys_ref[...].astype(bf16), preferred_element_type=f32)
        g2 = m_ref[pl.ds(mod_row_fn(i), 1), 5 * D_MODEL:6 * D_MODEL]
        o_ref[...] = x_ref[...] + g2 * f


def moe_sorted_call(Xmid, H2, RW, mod_l, w_gate, w_up, w_down, *, row0, n_rows, tile, win, mod_row_fn):
    D = Xmid.shape[1]
    F = w_gate.shape[-1]
    b0 = row0 // tile
    tok = lambda w: pl.BlockSpec((tile, w), lambda i, e: (b0 + i, 0))
    kern = functools.partial(_moe_sorted_kernel, win=win, mod_row_fn=mod_row_fn)
    return pl.pallas_call(
        kern,
        grid=(n_rows // tile, N_EXPERTS // MOE_EPS),
        in_specs=[tok(D), tok(D), tok(128),
                  pl.BlockSpec(mod_l.shape, lambda i, e: (0, 0)),
                  pl.BlockSpec((MOE_EPS, D, F), lambda i, e: (e, 0, 0)),
                  pl.BlockSpec((MOE_EPS, D, F), lambda i, e: (e, 0, 0)),
                  pl.BlockSpec((MOE_EPS, F, D), lambda i, e: (e, 0, 0))],
        out_specs=pl.BlockSpec((tile, D), lambda i, e: (i, 0)),
        out_shape=jax.ShapeDtypeStruct((n_rows, D), jnp.float32),
        scratch_shapes=[pltpu.VMEM((tile, D), jnp.bfloat16), pltpu.VMEM((tile, 128), jnp.float32),
                        pltpu.VMEM((tile, D), jnp.float32), pltpu.VMEM((tile, 128), jnp.float32),
                        pltpu.SMEM((2 * N_GROUPS,), jnp.int32)],
        compiler_params=pltpu.CompilerParams(vmem_limit_bytes=VMEM_LIMIT),
        name="moe_sorted",
    )(Xmid, H2, RW, mod_l, w_gate, w_up, w_down)


GDN_TILE = 256
GDN_CPT = GDN_TILE // GDN_CHUNK
GDN_LOCKSTEP = 2


def _mm(a, b, passes=1, dims=None):
    if dims is None:
        dot = lambda x, y: jnp.dot(x, y, preferred_element_type=jnp.float32)
    else:
        dot = lambda x, y: lax.dot_general(x, y, dims, preferred_element_type=jnp.float32)
    if passes == 1:
        return dot(a.astype(jnp.bfloat16), b.astype(jnp.bfloat16))
    a_hi, a_lo = _split_bf16(a)
    b_hi, b_lo = _split_bf16(b)
    return dot(a_hi, b_hi) + dot(a_hi, b_lo) + dot(a_lo, b_hi)


def _stack_heads(x):
    lane = lax.broadcasted_iota(jnp.int32, (1, 256), 1)
    return jnp.concatenate([jnp.where((lane // GDN_DIM) == h, x, 0.0) for h in range(GDN_HEADS)], 0)


def _slab(x):
    return x[0:64] + x[64:128] + x[128:192] + x[192:256]


def _unit_tri_inverse(mats):
    shape = mats[0].shape
    eye = (lax.broadcasted_iota(jnp.int32, shape, 0) == lax.broadcasted_iota(jnp.int32, shape, 1)).astype(jnp.float32)
    ps = [eye - a for a in mats]
    pws = list(mats)
    for _ in range(5):
        pws = [_mm(pw, pw, 3) for pw in pws]
        ps = [p + _mm(p, pw, 3) for p, pw in zip(ps, pws)]
    return ps


def _gdn_chunk_kernel(x_ref, xp_ref, xn_ref, ab_ref, cw_ref, par_ref, g64_ref, qe_ref, ou_ref, mm_ref, nn_ref,
                      *, tiles_per_batch, n_lat_tiles):
    i = pl.program_id(0)
    is_lat = i < n_lat_tiles
    first = jnp.where(is_lat, (i % tiles_per_batch) == 0, True)
    last = jnp.where(is_lat, (i % tiles_per_batch) == tiles_per_batch - 1, True)
    xp = jnp.concatenate([jnp.where(first, 0.0, xp_ref[...]), x_ref[...], jnp.where(last, 0.0, xn_ref[...])], 0)
    y = xp[6:6 + GDN_TILE] * cw_ref[0:1, :]
    for t in range(1, 5):
        y = y + xp[6 + t:6 + t + GDN_TILE] * cw_ref[t:t + 1, :]
    y = y * jax.nn.sigmoid(y)
    g64 = g64_ref[...]
    q = y[:, 0:256]
    k = y[:, 256:512]
    v = y[:, 512:768]
    q = q * lax.rsqrt(_group_sum(q * q, g64) + RMS_EPS) * (GDN_DIM ** -0.5)
    k = k * lax.rsqrt(_group_sum(k * k, g64) + RMS_EPS)
    ab = ab_ref[...]
    lane128 = lax.broadcasted_iota(jnp.int32, (1, 128), 1)
    g_all = jnp.where(lane128 < 8, par_ref[0:1, :] * jax.nn.softplus(ab + par_ref[1:2, :]), 0.0)
    beta_all = jax.nn.sigmoid(ab)
    r64 = lax.broadcasted_iota(jnp.int32, (64, 64), 0)
    c64 = lax.broadcasted_iota(jnp.int32, (64, 64), 1)
    low = (r64 >= c64).astype(jnp.float32)
    upp = (r64 <= c64).astype(jnp.float32)
    rr = lax.broadcasted_iota(jnp.int32, (256, 256), 0)
    cc = lax.broadcasted_iota(jnp.int32, (256, 256), 1)
    same = (rr // 64) == (cc // 64)
    eye = (rr == cc).astype(jnp.float32)
    for c0 in range(0, GDN_CPT, GDN_LOCKSTEP):
        stage = []
        for c in range(c0, c0 + GDN_LOCKSTEP):
            sl = slice(c * GDN_CHUNK, (c + 1) * GDN_CHUNK)
            g_c = g_all[sl]
            gsum = jnp.where(lane128 < 4, jnp.dot(low, g_c, preferred_element_type=jnp.float32, precision=HI),
                             jnp.dot(upp, g_c, preferred_element_type=jnp.float32, precision=HI))
            gsum_t = jnp.transpose(jnp.concatenate([gsum, jnp.zeros_like(gsum)], 0))[:, 0:64]
            tot = jnp.sum(g_c, axis=0, keepdims=True)
            beta_c = beta_all[sl]
            q_sm = _stack_heads(q[sl])
            k_sm = _stack_heads(k[sl])
            v_sm = _stack_heads(v[sl])
            kk = _mm(k_sm, k_sm, 1, NT_DIMS)
            qk = _mm(q_sm, k_sm, 1, NT_DIMS)
            for d in range(2):
                col = lambda h: 4 * d + h
                cmat = jnp.concatenate([jnp.broadcast_to(gsum[:, col(h):col(h) + 1], (64, 256)) for h in range(4)], 0)
                rrow = jnp.concatenate([gsum_t[col(h):col(h) + 1, :] for h in range(4)], 1)
                bmat = jnp.concatenate([jnp.broadcast_to(beta_c[:, 8 + col(h):9 + col(h)], (64, 256))
                                        for h in range(4)], 0)
                tmat = jnp.concatenate([jnp.broadcast_to(tot[:, col(h):col(h) + 1], (64, 256)) for h in range(4)], 0)
                tri = (rr % 64 >= cc % 64) if d == 0 else (rr % 64 <= cc % 64)
                incl = same & tri
                strict = incl & (rr != cc)
                dec = jnp.where(incl, jnp.exp(jnp.where(incl, cmat - rrow, 0.0)), 0.0)
                stage.append((d, sl, q_sm, k_sm, v_sm, qk, cmat, bmat, tmat, dec,
                              jnp.where(strict, kk * bmat * dec, 0.0)))
        t_invs = _unit_tri_inverse([st[-1] for st in stage])
        n = len(stage)
        q_sms, k_sms, v_sms, qks, cmats, bmats, tmats, decs = (
            [st[f] for st in stage] for f in range(2, 10))
        egs = [jnp.exp(cm) for cm in cmats]
        us = [_mm(t_invs[j], v_sms[j] * bmats[j], 1) for j in range(n)]
        ws = [_mm(t_invs[j], k_sms[j] * bmats[j] * egs[j], 1) for j in range(n)]
        qkms = [decs[j] * qks[j] for j in range(n)]
        qes = [q_sms[j] * egs[j] - _mm(qkms[j], ws[j], 1) for j in range(n)]
        ous = [_mm(qkms[j], us[j], 1) for j in range(n)]
        kd_ts = [jnp.transpose(k_sms[j] * jnp.exp(tmats[j] - cmats[j])) for j in range(n)]
        m_bds = [eye * jnp.exp(tmats[j]) - _mm(kd_ts[j], ws[j], 1) for j in range(n)]
        n_bds = [_mm(kd_ts[j], us[j], 1) for j in range(n)]
        for j in range(n):
            d, sl = stage[j][0], stage[j][1]
            qe_ref[d, sl, :] = _slab(qes[j])
            ou_ref[d, sl, :] = _slab(ous[j])
            mm_ref[d, sl, :] = _slab(m_bds[j])
            nn_ref[d, sl, :] = _slab(n_bds[j])


def gdn_chunk_call(P_gdn, P_ab, conv_w, a_log, dt_bias, *, n_batch, seq):
    NT = P_gdn.shape[0]
    tpb = seq // GDN_TILE
    n_lat = n_batch * tpb
    n_tiles = NT // GDN_TILE
    hb = GDN_TILE // 8
    cw = jnp.zeros((8, 768), jnp.float32).at[:5].set(conv_w)
    par = jnp.zeros((8, 128), jnp.float32)
    par = par.at[0, :8].set(-jnp.exp(a_log.reshape(8))).at[1, :8].set(dt_bias.reshape(8))
    g64 = _group_ones(256, 64)
    kern = functools.partial(_gdn_chunk_kernel, tiles_per_batch=tpb, n_lat_tiles=n_lat)
    full = lambda a: pl.BlockSpec(a.shape, lambda i: (0,) * a.ndim)
    out = pl.BlockSpec((2, GDN_TILE, 256), lambda i: (0, i, 0))
    return pl.pallas_call(
        kern,
        grid=(n_tiles,),
        in_specs=[pl.BlockSpec((GDN_TILE, 768), lambda i: (i, 0)),
                  pl.BlockSpec((8, 768), lambda i: (jnp.maximum(i * hb - 1, 0), 0)),
                  pl.BlockSpec((8, 768), lambda i: (jnp.minimum((i + 1) * hb, n_tiles * hb - 1), 0)),
                  pl.BlockSpec((GDN_TILE, 128), lambda i: (i, 0)),
                  full(cw), full(par), full(g64)],
        out_specs=[out] * 4,
        out_shape=[jax.ShapeDtypeStruct((2, NT, 256), jnp.float32)] * 4,
        compiler_params=pltpu.CompilerParams(vmem_limit_bytes=VMEM_LIMIT),
        name="gdn_chunk",
    )(P_gdn, P_gdn, P_gdn, P_ab, cw, par, g64)


def _gdn_scan_kernel(qef_ref, ouf_ref, mf_ref, nf_ref, qeb_ref, oub_ref, mb_ref, nb_ref, of_ref, ob_ref, s_ref):
    t = pl.program_id(1)

    @pl.when(t == 0)
    def _():
        s_ref[...] = jnp.zeros(s_ref.shape, jnp.float32)

    def step(d, c, qe_ref, ou_ref, m_ref, n_ref, o_ref):
        sl = slice(c * GDN_CHUNK, (c + 1) * GDN_CHUNK)
        s = s_ref[d]
        o_ref[sl, :] = _slab(_mm(_stack_heads(qe_ref[0, sl, :]), s, 1)) + ou_ref[0, sl, :]
        s_ref[d] = _mm(_stack_heads(m_ref[0, sl, :]), s, 3) + _stack_heads(n_ref[0, sl, :])

    for c in range(GDN_CPT):
        step(0, c, qef_ref, ouf_ref, mf_ref, nf_ref, of_ref)
        step(1, GDN_CPT - 1 - c, qeb_ref, oub_ref, mb_ref, nb_ref, ob_ref)


def gdn_scan_call(QE, OU, MM, NN, *, n_batch, seq):
    NT = QE.shape[1]
    tpb = seq // GDN_TILE
    cblk = n_batch * tpb
    fwd = lambda b, t: jnp.where(t == 0, cblk + b, b * tpb + t - 1)
    bwd = lambda b, t: jnp.where(t == 0, cblk + b, b * tpb + tpb - t)
    spec = lambda d, f: pl.BlockSpec((1, GDN_TILE, 256), lambda b, t: (d, f(b, t), 0))
    ospec = lambda f: pl.BlockSpec((GDN_TILE, 256), lambda b, t: (f(b, t), 0))
    return pl.pallas_call(
        _gdn_scan_kernel,
        grid=(n_batch, tpb + 1),
        in_specs=[spec(0, fwd)] * 4 + [spec(1, bwd)] * 4,
        out_specs=[ospec(fwd), ospec(bwd)],
        out_shape=[jax.ShapeDtypeStruct((NT, 256), jnp.float32)] * 2,
        scratch_shapes=[pltpu.VMEM((2, 256, 256), jnp.float32)],
        compiler_params=pltpu.CompilerParams(vmem_limit_bytes=VMEM_LIMIT),
        name="gdn_scan",
    )(QE, OU, MM, NN, QE, OU, MM, NN)


def _dft_cs(n):
    a = 2.0 * np.pi * np.outer(np.arange(n), np.arange(n)) / n
    return np.cos(a), np.sin(a)


def _channel_dft():
    c, s = _dft_cs(FNET_GROUP_W)
    eye = np.eye(FNET_GROUPS)
    return np.concatenate([np.kron(eye, c), np.kron(eye, s)], 1)


FNET_SUB = 16


def _fnet_a_kernel(x_ref, cs_ref, m_ref, cphi_ref, sphi_ref, br_ref, bi_ref, *, n_rows):
    for j in range(FNET_SUB):
        sl = slice(j * BRANCH_W, (j + 1) * BRANCH_W)
        u = jnp.dot(x_ref[:, j, :], cs_ref[...], preferred_element_type=jnp.float32)
        st = jnp.concatenate([u[:, :BRANCH_W], u[:, BRANCH_W:]], 0).astype(jnp.bfloat16)
        a = jnp.dot(m_ref[...], st, preferred_element_type=jnp.float32)
        ar, ai = a[:n_rows], a[n_rows:]
        cp, sp = cphi_ref[:, sl], sphi_ref[:, sl]
        br_ref[j] = ar * cp - ai * sp
        bi_ref[j] = ar * sp + ai * cp


def _fnet_c_kernel(br_ref, bi_ref, m_ref, o_ref):
    for j in range(FNET_SUB):
        st = jnp.concatenate([br_ref[:, j, :], bi_ref[:, j, :]], 0).astype(jnp.bfloat16)
        o_ref[:, j, :] = jnp.dot(m_ref[...], st, preferred_element_type=jnp.float32)


def fourier_latent_call(P_f, *, n_batch, seq):
    rows = seq // GRID_W
    bf16, f32 = jnp.bfloat16, jnp.float32
    c1, s1 = _dft_cs(rows)
    norm = 1.0 / math.sqrt(seq * FNET_GROUP_W)
    m_a = jnp.asarray(np.block([[c1, -s1], [s1, c1]]) * norm, bf16)
    c2, s2 = _dft_cs(GRID_W)
    m_c = jnp.asarray(np.concatenate([c2, -s2], 1), bf16)
    cs = jnp.asarray(_channel_dft(), bf16)
    phi = 2.0 * np.pi * np.outer(np.arange(rows), np.arange(GRID_W)) / seq
    cphi = jnp.repeat(jnp.asarray(np.cos(phi), f32), BRANCH_W, axis=1)
    sphi = jnp.repeat(jnp.asarray(np.sin(phi), f32), BRANCH_W, axis=1)
    xv = P_f[:n_batch * seq].reshape(n_batch * rows, GRID_W, BRANCH_W)
    nj = GRID_W // FNET_SUB
    full = lambda a: pl.BlockSpec(a.shape, lambda b, j: (0,) * a.ndim)
    tw = pl.BlockSpec((rows, FNET_SUB * BRANCH_W), lambda b, j: (0, j))
    mid = jax.ShapeDtypeStruct((n_batch * GRID_W, rows, BRANCH_W), f32)
    br, bi = pl.pallas_call(
        functools.partial(_fnet_a_kernel, n_rows=rows),
        grid=(n_batch, nj),
        in_specs=[pl.BlockSpec((rows, FNET_SUB, BRANCH_W), lambda b, j: (b, j, 0)), full(cs), full(m_a), tw, tw],
        out_specs=[pl.BlockSpec((FNET_SUB, rows, BRANCH_W), lambda b, j: (b * nj + j, 0, 0))] * 2,
        out_shape=[mid, mid],
        name="fnet_rows",
    )(xv, cs, m_a, cphi, sphi)
    blk = pl.BlockSpec((GRID_W, FNET_SUB, BRANCH_W), lambda b, i: (b, i, 0))
    y = pl.pallas_call(
        _fnet_c_kernel,
        grid=(n_batch, rows // FNET_SUB),
        in_specs=[blk, blk, pl.BlockSpec(m_c.shape, lambda b, i: (0, 0))],
        out_specs=blk,
        out_shape=mid,
        name="fnet_cols",
    )(br, bi, m_c)
    return y.reshape(n_batch * seq, BRANCH_W)


def _fnet_ctx_kernel(x_ref, cs_ref, m_ref, o_ref):
    u = jnp.dot(x_ref[...], cs_ref[...], preferred_element_type=jnp.float32)
    st = jnp.concatenate([u[:, :BRANCH_W], u[:, BRANCH_W:]], 0).astype(jnp.bfloat16)
    o_ref[...] = jnp.dot(m_ref[...], st, preferred_element_type=jnp.float32).astype(o_ref.dtype)


def fourier_ctx_call(P_f, *, n_batch, seq, ctx_len):
    bf16 = jnp.bfloat16
    c, s = _dft_cs(ctx_len)
    m = jnp.asarray(np.concatenate([c, -s], 1) / math.sqrt(ctx_len * FNET_GROUP_W), bf16)
    cs = jnp.asarray(_channel_dft(), bf16)
    cb = n_batch * seq // ctx_len
    return pl.pallas_call(
        _fnet_ctx_kernel,
        grid=(n_batch,),
        in_specs=[pl.BlockSpec((ctx_len, BRANCH_W), lambda b: (cb + b, 0)),
                  pl.BlockSpec(cs.shape, lambda b: (0, 0)), pl.BlockSpec(m.shape, lambda b: (0, 0))],
        out_specs=pl.BlockSpec((ctx_len, BRANCH_W), lambda b: (b, 0)),
        out_shape=jax.ShapeDtypeStruct((n_batch * ctx_len, BRANCH_W), jnp.float32),
        name="fnet_ctx",
    )(P_f, cs, m)


def kernel(x, c, ctx, c_ctx, w_mod, b_mod, norm1_g, norm2_g, w_in, na_qn_g, na_kn_g, na_rpb,
           da_qn_g, da_kn_g, da_lam_q1, da_lam_k1, da_lam_q2, da_lam_k2, da_subln_g, gdn_conv_w,
           gdn_a_log, gdn_dt_bias, gdn_onorm_g, w_branch, w_out, w_router, router_bias,
           moe_w_gate, moe_w_up, moe_w_down):
    B, S, D = x.shape
    L = ctx.shape[1]
    NL = B * S
    f32 = jnp.float32
    cvec = jnp.zeros((8, D), f32).at[:B].set(c).at[B].set(c_ctx)
    mods = mod_call(cvec, w_mod, b_mod)
    cos, sin = rope_tables(S, 256)
    bf16 = jnp.bfloat16
    Xl, Xc = x.reshape(NL, D), ctx.reshape(B * L, D)
    for l in range(DEPTH):
        last = l == DEPTH - 1
        lam_init = 0.8 - 0.6 * math.exp(-0.3 * l)
        w_main = jnp.concatenate([w_in[l][:, :2560], w_in[l][:, 2576:]], 1).astype(jnp.bfloat16)
        w_ab = jnp.pad(w_in[l][:, 2560:2576], ((0, 0), (0, 112))).astype(jnp.bfloat16)
        NAQ, NAK, NAV, DAQ, DAK, DAVX, P_gdn, P_z, P_f, P_gate, P_ab = inproj_call(
            Xl, Xc, 0, mods[l], norm1_g[l], w_main, w_ab, cos, sin, na_qn_g[l], na_kn_g[l], da_qn_g[l], da_kn_g[l],
            n_batch=B, seq=S, ctx_len=L)
        bias = na_bias_tables(na_rpb[l], S // GRID_W)
        lam = (jnp.exp(jnp.sum(da_lam_q1[l] * da_lam_k1[l])) - jnp.exp(jnp.sum(da_lam_q2[l] * da_lam_k2[l]))
               + lam_init).astype(f32)
        Ya = na_call(NAQ, NAK, NAV, bias, n_batch=B, seq=S, ctx_len=L)
        Yd = da_call(lam, DAQ, DAK, DAVX, da_subln_g[l], n_batch=B, seq=S, ctx_len=L, lam_init=lam_init)
        QE, OU, MM, NN = gdn_chunk_call(P_gdn, P_ab, gdn_conv_w[l], gdn_a_log[l], gdn_dt_bias[l], n_batch=B, seq=S)
        Of, Ob = gdn_scan_call(QE, OU, MM, NN, n_batch=B, seq=S)
        Yf = fourier_latent_call(P_f, n_batch=B, seq=S)
        n_tok = NL
        if not last:
            Yac, Ydc = ctx_attn_call(lam, NAQ, NAK, NAV, DAQ, DAK, DAVX, da_subln_g[l],
                                     n_batch=B, seq=S, ctx_len=L, lam_init=lam_init)
            Ya = jnp.concatenate([Ya, Yac], 0)
            Yd = jnp.concatenate([Yd, Ydc], 0)
            Yf = jnp.concatenate([Yf, fourier_ctx_call(P_f, n_batch=B, seq=S, ctx_len=L)], 0)
            n_tok = NL + B * L
        Xmid, H2, RW = merge_call(Xl, Xc, 0, Ya, Yd, Of, Ob, P_z, Yf, P_gate, mods[l], norm2_g[l], gdn_onorm_g[l],
                                  w_branch[l].astype(bf16), w_out[l].astype(bf16), w_router, router_bias,
                                  n_tok=n_tok, n_batch=B, seq=S)
        experts = (moe_w_gate[l].astype(bf16), moe_w_up[l].astype(bf16), moe_w_down[l].astype(bf16))
        Xl = moe_sorted_call(Xmid, H2, RW, mods[l], *experts, row0=0, n_rows=NL, tile=MOE_TILE, win=MOE_WIN,
                             mod_row_fn=lambda i: i // (S // MOE_TILE))
        if not last:
            Xc = moe_sorted_call(Xmid, H2, RW, mods[l], *experts, row0=NL, n_rows=B * L, tile=B * L,
                                 win=MOE_WIN_CTX, mod_row_fn=lambda i: B)
    return Xl.reshape(B, S, D)
```

```python
import functools
import math
import jax
import jax.numpy as jnp
from jax import lax
import numpy as np
from jax.experimental import pallas as pl
from jax.experimental.pallas import tpu as pltpu

D_MODEL = 1024
DEPTH = 2
GRID_W = 64
N_BRANCH = 4
BRANCH_W = D_MODEL // N_BRANCH
GDN_HEADS = 4
GDN_DIM = BRANCH_W // GDN_HEADS
GDN_CHUNK = 64
FNET_GROUPS = 4
FNET_GROUP_W = BRANCH_W // FNET_GROUPS
N_EXPERTS = 16
N_GROUPS = 4
RMS_EPS = 1e-6
NEG_INF = -1e30
ROPE_BASE = 10000.0
VMEM_LIMIT = 56 * 1024 * 1024
HI = lax.Precision.HIGHEST
NT_DIMS = (((1,), (1,)), ((), ()))
LOG2E = 1.4426950408889634


def _mod_kernel(c_ref, w_ref, b_ref, o_ref):
    c = c_ref[...]
    a = c * jax.nn.sigmoid(c)
    o_ref[0] = jnp.dot(a, w_ref[0], preferred_element_type=jnp.float32, precision=HI) + b_ref[0]


def mod_call(cvec, w_mod, b_mod, tn=512):
    depth, D, N = w_mod.shape
    return pl.pallas_call(
        _mod_kernel,
        grid=(depth, N // tn),
        in_specs=[pl.BlockSpec((8, D), lambda l, j: (0, 0)),
                  pl.BlockSpec((1, D, tn), lambda l, j: (l, 0, j)),
                  pl.BlockSpec((1, 1, tn), lambda l, j: (l, 0, j))],
        out_specs=pl.BlockSpec((1, 8, tn), lambda l, j: (l, 0, j)),
        out_shape=jax.ShapeDtypeStruct((depth, 8, N), jnp.float32),
        name="mod",
    )(cvec, w_mod, b_mod.reshape(depth, 1, N))


def _stream_specs(n_lat_tiles, ctx_tile0, tm, width):
    return [pl.BlockSpec((tm, width), lambda i: (jnp.minimum(i, n_lat_tiles - 1), 0)),
            pl.BlockSpec((tm, width), lambda i: (ctx_tile0 + jnp.maximum(i - n_lat_tiles, 0), 0))]


def _inproj_kernel(xl_ref, xc_ref, m_ref, g_ref, w_ref, wb_ref, wab_ref, cos_ref, sin_ref, g64_ref, g32_ref, gains_ref,
                   naq_ref, nak_ref, nav_ref, daq_ref, dak_ref, davx_ref, gdn_ref, z_ref, f_ref, gate_ref, ab_ref,
                   *, tiles_per_batch, n_lat_tiles, n_batch):
    i = pl.program_id(0)
    r = jnp.where(i < n_lat_tiles, i // tiles_per_batch, n_batch)
    x = jnp.where(i < n_lat_tiles, xl_ref[...], xc_ref[...])
    y = x * lax.rsqrt(jnp.mean(x * x, axis=-1, keepdims=True) + RMS_EPS) * g_ref[...]
    sh = m_ref[pl.ds(r, 1), 0:D_MODEL]
    sc = m_ref[pl.ds(r, 1), D_MODEL:2 * D_MODEL]
    h = (y * (1.0 + sc) + sh).astype(jnp.bfloat16)
    proj = lambda a, b: jnp.dot(h, w_ref[:, a:b], preferred_element_type=jnp.float32)
    proj_b = lambda a, b: jnp.dot(h, wb_ref[:, a:b], preferred_element_type=jnp.float32)
    bf16 = jnp.bfloat16

    g64 = g64_ref[...]
    naq_ref[...] = (_group_rms(proj(0, 256), g64, 64, gains_ref[0:1, :]) * (64 ** -0.5)).astype(bf16)
    nak_ref[...] = _group_rms(proj(256, 512), g64, 64, gains_ref[1:2, :]).astype(bf16)
    nav_ref[...] = proj(512, 768).astype(bf16)

    g32 = g32_ref[...]
    cos = cos_ref[...]
    sin = sin_ref[...]
    lane = lax.broadcasted_iota(jnp.int32, (1, 256), 1)
    first = (lane % 16) < 8

    def rope(t):
        swapped = jnp.where(first, pltpu.roll(t, 256 - 8, 1), pltpu.roll(t, 8, 1))
        return t * cos + swapped * sin

    q = rope(_group_rms(proj(768, 1024), g32, 32, gains_ref[2:3, :]))
    k = rope(_group_rms(proj(1024, 1280), g32, 32, gains_ref[3:4, :]))
    daq_ref[...] = (q * (32 ** -0.5 * LOG2E)).astype(bf16)
    dak_ref[...] = k.astype(bf16)
    v = proj(1280, 1536).astype(bf16)
    one_col = (lax.broadcasted_iota(jnp.int32, (v.shape[0], 64), 1) == 0).astype(bf16)
    davx_ref[...] = jnp.concatenate(
        [piece for hd in range(4) for piece in (v[:, hd * 64:(hd + 1) * 64], one_col)], -1)

    gdn_ref[...] = proj(1536, 2304)
    z_ref[...] = proj(2304, 2560).astype(bf16)
    f_ref[...] = proj_b(0, 256).astype(bf16)
    for kk in range(4):
        gate_ref[:, kk * 1024:(kk + 1) * 1024] = proj_b(256 + kk * 1024, 256 + (kk + 1) * 1024).astype(bf16)
    ab_ref[...] = jnp.dot(h, wab_ref[...], preferred_element_type=jnp.float32)


def inproj_call(Xl, Xc, ctx_row0, mod_l, g, w_a, w_b, w_ab, cos, sin, na_qg, na_kg, da_qg, da_kg,
                *, n_batch, seq, ctx_len, tm=256):
    D = Xl.shape[1]
    NT = n_batch * (seq + ctx_len)
    tpb = seq // tm
    n_lat_tiles = n_batch * tpb
    kern = functools.partial(_inproj_kernel, tiles_per_batch=tpb, n_lat_tiles=n_lat_tiles, n_batch=n_batch)
    bf16, f32 = jnp.bfloat16, jnp.float32
    widths = [(256, bf16)] * 5 + [(512, bf16), (768, f32), (256, bf16), (256, bf16), (4096, bf16), (128, f32)]
    gains = jnp.stack([jnp.tile(na_qg, 4), jnp.tile(na_kg, 4), jnp.tile(da_qg, 8), jnp.tile(da_kg, 8)]
                      + [jnp.zeros((256,), f32)] * 4, 0)
    g64, g32 = _group_ones(256, 64), _group_ones(256, 32)
    tab = pl.BlockSpec((tm, 256), lambda i: (jnp.where(i < n_lat_tiles, i % tpb, tpb), 0))
    full = lambda a: pl.BlockSpec(a.shape, lambda i: (0,) * a.ndim)
    return pl.pallas_call(
        kern,
        grid=(NT // tm,),
        in_specs=_stream_specs(n_lat_tiles, ctx_row0 // tm, tm, D) + [
            full(mod_l), pl.BlockSpec((1, D), lambda i: (0, 0)), full(w_a), full(w_b), full(w_ab), tab, tab,
            full(g64), full(g32), full(gains)],
        out_specs=[pl.BlockSpec((tm, w), lambda i: (i, 0)) for w, _ in widths],
        out_shape=[jax.ShapeDtypeStruct((NT, w), dt) for w, dt in widths],
        compiler_params=pltpu.CompilerParams(vmem_limit_bytes=VMEM_LIMIT),
        name="inproj",
    )(Xl, Xc, mod_l, g.reshape(1, D), w_a, w_b, w_ab, cos, sin, g64, g32, gains)


def _group_ones(width, group):
    i = np.arange(width)
    return jnp.asarray((i[:, None] // group == i[None, :] // group).astype(np.float32), jnp.bfloat16)


def _group_sum(xx, gmat):
    hi, lo = _split_bf16(xx)
    return (jnp.dot(hi, gmat, preferred_element_type=jnp.float32)
            + jnp.dot(lo, gmat, preferred_element_type=jnp.float32))


def _group_rms(x, gmat, group, gain):
    return x * lax.rsqrt(_group_sum(x * x, gmat) * (1.0 / group) + RMS_EPS) * gain


def rope_tables(seq, tm):
    nf = 8
    t = jnp.arange(seq)
    rows = (t // GRID_W).astype(jnp.float32)
    cols = (t % GRID_W).astype(jnp.float32)
    freqs = ROPE_BASE ** (-jnp.arange(nf, dtype=jnp.float32) / nf)
    d = np.arange(32)
    f_idx = d % 8
    use_col = (d // 16) == 1
    ang = jnp.where(use_col[None, :], cols[:, None], rows[:, None]) * freqs[f_idx][None, :]
    sign = np.where((d % 16) < 8, -1.0, 1.0).astype(np.float32)
    cos = jnp.tile(jnp.cos(ang), (1, 8))
    sin = jnp.tile(jnp.sin(ang) * sign[None, :], (1, 8))
    cos = jnp.concatenate([cos, jnp.ones((tm, 256), jnp.float32)], 0)
    sin = jnp.concatenate([sin, jnp.zeros((tm, 256), jnp.float32)], 0)
    return cos, sin


NA_RB = 4
NA_UR = 11


def na_bias_tables(rpb, rows):
    reps = [(0, 0), (4, 0), (rows - NA_RB, rows - NA_UR)]
    qc = np.arange(GRID_W)[:, None]
    kc = np.arange(GRID_W)[None, :]
    ws = np.clip(qc - 8, 0, GRID_W - 16)
    vcol = (kc >= ws) & (kc < ws + 16)
    rel_c = np.clip(kc - qc + 15, 0, 30)
    sel_c = (rel_c[..., None] == np.arange(31)).astype(np.float32)
    sel_r, vrow = [], []
    for r0, u0 in reps:
        r = r0 + np.arange(NA_RB)[:, None]
        krow = u0 + np.arange(NA_UR)[None, :]
        start = np.clip(r - 4, 0, rows - 8)
        vrow.append((krow >= start) & (krow < start + 8))
        sel_r.append((np.clip(krow - r + 7, 0, 14)[..., None] == np.arange(15)).astype(np.float32))
    sel_r, vrow = np.stack(sel_r), np.stack(vrow)
    b = jnp.einsum('taui,hij,qkj->thaquk', jnp.asarray(sel_r), rpb.astype(jnp.float32), jnp.asarray(sel_c),
                   precision=HI)
    valid = vrow[:, None, :, None, :, None] & vcol[None, None, None, :, None, :]
    b = jnp.where(valid, b, NEG_INF)
    return b.reshape(3, rpb.shape[0], NA_RB * GRID_W, NA_UR * GRID_W).astype(jnp.bfloat16)


def _na_kernel(q_ref, k_ref, v_ref, kc_ref, vc_ref, bias_ref, o_ref, *, rows):
    i = pl.program_id(1)
    u0 = jnp.clip(i * NA_RB - 4, 0, rows - NA_UR)
    off = pl.multiple_of(u0 * GRID_W, GRID_W)
    nkw = NA_UR * GRID_W
    kwin = k_ref[pl.ds(off, nkw), :]
    vwin = v_ref[pl.ds(off, nkw), :]
    q = q_ref[...]
    kc = kc_ref[...]
    vc = vc_ref[...]
    outs = []
    for h in range(4):
        sl = slice(h * 64, (h + 1) * 64)
        qh = q[:, sl]
        s_loc = lax.dot_general(qh, kwin[:, sl], NT_DIMS, preferred_element_type=jnp.float32) + bias_ref[0, h].astype(jnp.float32)
        s_ctx = lax.dot_general(qh, kc[:, sl], NT_DIMS, preferred_element_type=jnp.float32)
        m = jnp.maximum(jnp.max(s_loc, -1, keepdims=True), jnp.max(s_ctx, -1, keepdims=True))
        p_loc = jnp.exp(s_loc - m)
        p_ctx = jnp.exp(s_ctx - m)
        l = jnp.sum(p_loc, -1, keepdims=True) + jnp.sum(p_ctx, -1, keepdims=True)
        o = (jnp.dot(p_loc.astype(jnp.bfloat16), vwin[:, sl], preferred_element_type=jnp.float32)
             + jnp.dot(p_ctx.astype(jnp.bfloat16), vc[:, sl], preferred_element_type=jnp.float32))
        outs.append(o / l)
    o_ref[...] = jnp.concatenate(outs, -1).astype(o_ref.dtype)


def na_call(NAQ, NAK, NAV, bias, *, n_batch, seq, ctx_len):
    rows = seq // GRID_W
    nblk = rows // NA_RB
    tq = NA_RB * GRID_W
    cb = n_batch * seq // ctx_len
    kern = functools.partial(_na_kernel, rows=rows)
    tsel = lambda b, i: (jnp.where(i == 0, 0, jnp.where(i == nblk - 1, 2, 1)), 0, 0, 0)
    return pl.pallas_call(
        kern,
        grid=(n_batch, nblk),
        in_specs=[pl.BlockSpec((tq, 256), lambda b, i: (b * nblk + i, 0)),
                  pl.BlockSpec((seq, 256), lambda b, i: (b, 0)),
                  pl.BlockSpec((seq, 256), lambda b, i: (b, 0)),
                  pl.BlockSpec((ctx_len, 256), lambda b, i: (cb + b, 0)),
                  pl.BlockSpec((ctx_len, 256), lambda b, i: (cb + b, 0)),
                  pl.BlockSpec((1,) + bias.shape[1:], tsel)],
        out_specs=pl.BlockSpec((tq, 256), lambda b, i: (b * nblk + i, 0)),
        out_shape=jax.ShapeDtypeStruct((n_batch * seq, 256), jnp.bfloat16),
        compiler_params=pltpu.CompilerParams(vmem_limit_bytes=VMEM_LIMIT),
        name="na_attn",
    )(NAQ, NAK, NAV, NAK, NAV, bias)


def _stack_masked_q(q):
    lane = lax.broadcasted_iota(jnp.int32, (1, 256), 1)
    return jnp.concatenate([jnp.where((lane // 32) == hm, q, jnp.zeros_like(q)) for hm in range(8)], 0)


def _da_finish(acc, lam, gain, g64, post_scale):
    tq = acc.shape[0] // 8
    norm = acc / jnp.broadcast_to(acc[:, 64:65], acc.shape)
    lane = lax.broadcasted_iota(jnp.int32, (1, 128), 1)
    halves = []
    for hp in range(2):
        d = [norm[(4 * hp + 2 * j) * tq:(4 * hp + 2 * j + 1) * tq]
             - lam * norm[(4 * hp + 2 * j + 1) * tq:(4 * hp + 2 * j + 2) * tq] for j in range(2)]
        halves.append(jnp.where(lane < 64, d[0], pltpu.roll(d[1], 64, 1)))
    o = jnp.concatenate(halves, -1)
    return _group_rms(o, g64, 64, gain) * post_scale


def _da_tile(qs, k, vx, m, acc, tq):
    tk = k.shape[0]
    s = lax.dot_general(qs, k, NT_DIMS, preferred_element_type=jnp.float32)
    m_new = jnp.maximum(m, jnp.max(s, -1, keepdims=True))
    alpha = jnp.exp2(m - m_new)
    p = jnp.exp2(s - pltpu.repeat(m_new, tk // 128, axis=1)).astype(jnp.bfloat16)
    pv = jnp.concatenate(
        [jnp.dot(p[2 * h * tq:(2 * h + 2) * tq], vx[:, h * 128:(h + 1) * 128], preferred_element_type=jnp.float32)
         for h in range(4)], 0)
    return m_new, alpha * acc + pv


def _da_kernel(lam_ref, q_ref, k_ref, vx_ref, kc_ref, vxc_ref, gain_ref, g64_ref, o_ref, m_ref, acc_ref,
               *, tk, n_kv, post_scale):
    tq = q_ref.shape[0]
    qs = _stack_masked_q(q_ref[...])
    m_ref[...] = jnp.full(m_ref.shape, -jnp.inf, jnp.float32)
    acc_ref[...] = jnp.zeros(acc_ref.shape, jnp.float32)

    def body(j, carry):
        for u in range(2):
            off = pl.multiple_of((2 * j + u) * tk, tk)
            m, acc = _da_tile(qs, k_ref[pl.ds(off, tk), :], vx_ref[pl.ds(off, tk), :], m_ref[...], acc_ref[...], tq)
            m_ref[...] = m
            acc_ref[...] = acc
        return carry

    lax.fori_loop(0, n_kv // 2 - 1, body, 0, unroll=True)
    off = (n_kv - 2) * tk
    m, acc = _da_tile(qs, k_ref[off:off + tk, :], vx_ref[off:off + tk, :], m_ref[...], acc_ref[...], tq)
    k_last = jnp.concatenate([k_ref[off + tk:off + 2 * tk, :], kc_ref[...]], 0)
    vx_last = jnp.concatenate([vx_ref[off + tk:off + 2 * tk, :], vxc_ref[...]], 0)
    m, acc = _da_tile(qs, k_last, vx_last, m, acc, tq)
    o_ref[...] = _da_finish(acc, lam_ref[0], gain_ref[...], g64_ref[...], post_scale).astype(o_ref.dtype)


def da_call(lam, DAQ, DAK, DAVX, subln_g, *, n_batch, seq, ctx_len, lam_init, tq=256, tk=1024):
    tk = min(tk, seq // 2)
    nq = seq // tq
    cb = n_batch * seq // ctx_len
    g64 = _group_ones(256, 64)
    gain = jnp.tile(subln_g, 4).reshape(1, 256)
    n_kv = seq // tk
    assert n_kv % 2 == 0
    kern = functools.partial(_da_kernel, tk=tk, n_kv=n_kv, post_scale=1.0 - lam_init)
    return pl.pallas_call(
        kern,
        grid=(n_batch, nq),
        in_specs=[pl.BlockSpec(memory_space=pltpu.SMEM),
                  pl.BlockSpec((tq, 256), lambda b, i: (b * nq + i, 0)),
                  pl.BlockSpec((seq, 256), lambda b, i: (b, 0)),
                  pl.BlockSpec((seq, 512), lambda b, i: (b, 0)),
                  pl.BlockSpec((ctx_len, 256), lambda b, i: (cb + b, 0)),
                  pl.BlockSpec((ctx_len, 512), lambda b, i: (cb + b, 0)),
                  pl.BlockSpec((1, 256), lambda b, i: (0, 0)),
                  pl.BlockSpec((256, 256), lambda b, i: (0, 0))],
        out_specs=pl.BlockSpec((tq, 256), lambda b, i: (b * nq + i, 0)),
        out_shape=jax.ShapeDtypeStruct((n_batch * seq, 256), jnp.bfloat16),
        scratch_shapes=[pltpu.VMEM((8 * tq, 128), jnp.float32), pltpu.VMEM((8 * tq, 128), jnp.float32)],
        compiler_params=pltpu.CompilerParams(vmem_limit_bytes=VMEM_LIMIT),
        name="da_attn",
    )(lam.reshape(1), DAQ, DAK, DAVX, DAK, DAVX, gain, g64)


def _ctx_kernel(lam_ref, naq_ref, nak_ref, nav_ref, daq_ref, dak_ref, dav_ref, gain_ref, g64_ref, ya_ref, yd_ref,
                *, post_scale):
    q = naq_ref[...]
    k = nak_ref[...]
    v = nav_ref[...]
    outs = []
    for h in range(4):
        sl = slice(h * 64, (h + 1) * 64)
        s = lax.dot_general(q[:, sl], k[:, sl], NT_DIMS, preferred_element_type=jnp.float32)
        p = jnp.exp(s - jnp.max(s, -1, keepdims=True))
        o = jnp.dot(p.astype(jnp.bfloat16), v[:, sl], preferred_element_type=jnp.float32)
        outs.append(o / jnp.sum(p, -1, keepdims=True))
    ya_ref[...] = jnp.concatenate(outs, -1).astype(ya_ref.dtype)
    tq = daq_ref.shape[0]
    qs = _stack_masked_q(daq_ref[...])
    m0 = jnp.full((8 * tq, 128), -jnp.inf, jnp.float32)
    m, acc = _da_tile(qs, dak_ref[...], dav_ref[...], m0, jnp.zeros((8 * tq, 128), jnp.float32), tq)
    yd_ref[...] = _da_finish(acc, lam_ref[0], gain_ref[...], g64_ref[...], post_scale).astype(yd_ref.dtype)


def ctx_attn_call(lam, NAQ, NAK, NAV, DAQ, DAK, DAVX, subln_g, *, n_batch, seq, ctx_len, lam_init):
    cb = n_batch * seq // ctx_len
    g64 = _group_ones(256, 64)
    gain = jnp.tile(subln_g, 4).reshape(1, 256)
    blk = lambda col: pl.BlockSpec((ctx_len, 256), lambda b: (cb + b, col))
    return pl.pallas_call(
        functools.partial(_ctx_kernel, post_scale=1.0 - lam_init),
        grid=(n_batch,),
        in_specs=[pl.BlockSpec(memory_space=pltpu.SMEM), blk(0), blk(0), blk(0), blk(0), blk(0),
                  pl.BlockSpec((ctx_len, 512), lambda b: (cb + b, 0)),
                  pl.BlockSpec((1, 256), lambda b: (0, 0)), pl.BlockSpec((256, 256), lambda b: (0, 0))],
        out_specs=[pl.BlockSpec((ctx_len, 256), lambda b: (b, 0))] * 2,
        out_shape=[jax.ShapeDtypeStruct((n_batch * ctx_len, 256), jnp.bfloat16)] * 2,
        name="ctx_attn",
    )(lam.reshape(1), NAQ, NAK, NAV, DAQ, DAK, DAVX, gain, g64)


def _split_bf16(a):
    hi = a.astype(jnp.bfloat16)
    return hi, (a - hi.astype(jnp.float32)).astype(jnp.bfloat16)


def _route(logits_t, bias_col):
    per_group = N_EXPERTS // N_GROUPS
    scores = jax.nn.sigmoid(logits_t)
    sel_all = scores + bias_col
    sel = [sel_all[e:e + 1, :] for e in range(N_EXPERTS)]
    top2 = []
    for e in range(N_EXPERTS):
        g0 = e // per_group * per_group
        rank = jnp.zeros_like(sel[e])
        for o in range(g0, g0 + per_group):
            if o != e:
                beats = (sel[o] > sel[e]) | (sel[o] == sel[e]) if o < e else (sel[o] > sel[e])
                rank = rank + beats.astype(jnp.float32)
        top2.append(rank < 1.5)
    gs = []
    for g in range(N_GROUPS):
        acc = jnp.zeros_like(sel[0])
        for e in range(g * per_group, (g + 1) * per_group):
            acc = acc + jnp.where(top2[e], sel[e], 0.0)
        gs.append(acc)
    rows = []
    for g in range(N_GROUPS):
        beaten = jnp.zeros_like(sel[0])
        for o in range(N_GROUPS):
            if o != g:
                b = (gs[o] >= gs[g]) if o < g else (gs[o] > gs[g])
                beaten = beaten + b.astype(jnp.float32)
        best = beaten < 0.5
        for e in range(g * per_group, (g + 1) * per_group):
            rows.append(jnp.where(best & top2[e], scores[e:e + 1, :], 0.0))
    w = jnp.concatenate(rows, 0)
    return w / jnp.sum(w, axis=0, keepdims=True)


def _merge_kernel(xl_ref, xc_ref, yal_ref, yac_ref, ydl_ref, ydc_ref, yfl_ref, yfc_ref, of_ref, ob_ref, z_ref, gate_ref,
                  m_ref, g2_ref, og_ref, g64_ref, wb_ref, wo_ref, wr_ref, rb_ref, xo_ref, h2_ref, rw_ref, *, tiles_per_batch, n_lat_tiles, n_batch):
    i = pl.program_id(0)
    r = jnp.where(i < n_lat_tiles, i // tiles_per_batch, n_batch)
    z = z_ref[...].astype(jnp.float32)
    yg = (_group_rms(of_ref[...] + ob_ref[...], g64_ref[...], GDN_DIM, og_ref[...]) * (z * jax.nn.sigmoid(z)))
    is_lat = i < n_lat_tiles
    pick = lambda lat_ref, ctx_ref: jnp.where(is_lat, lat_ref[...], ctx_ref[...])
    branches = (pick(yal_ref, yac_ref), pick(ydl_ref, ydc_ref), yg.astype(jnp.bfloat16),
                pick(yfl_ref, yfc_ref).astype(jnp.bfloat16))
    mix = None
    for b, yb in enumerate(branches):
        proj = jnp.dot(yb, wb_ref[b], preferred_element_type=jnp.float32)
        term = jax.nn.sigmoid(gate_ref[:, b * D_MODEL:(b + 1) * D_MODEL].astype(jnp.float32)) * proj
        mix = term if mix is None else mix + term
    y = jnp.dot(mix.astype(jnp.bfloat16), wo_ref[...], preferred_element_type=jnp.float32)
    mrow = lambda k: m_ref[pl.ds(r, 1), k * D_MODEL:(k + 1) * D_MODEL]
    x = jnp.where(i < n_lat_tiles, xl_ref[...], xc_ref[...]) + mrow(2) * y
    xo_ref[...] = x
    h2 = (x * lax.rsqrt(jnp.mean(x * x, axis=-1, keepdims=True) + RMS_EPS) * g2_ref[...]) * (1.0 + mrow(4)) + mrow(3)
    h2_ref[...] = h2.astype(h2_ref.dtype)
    h_hi, h_lo = _split_bf16(h2)
    w_hi, w_lo = _split_bf16(wr_ref[...])
    nt = lambda a, b: lax.dot_general(a, b, NT_DIMS, preferred_element_type=jnp.float32)
    logits_t = nt(w_hi, h_hi) + nt(w_hi, h_lo) + nt(w_lo, h_hi)
    w_t = _route(logits_t, rb_ref[...])
    pad = jnp.zeros((128 - N_EXPERTS, w_t.shape[1]), jnp.float32)
    rw_ref[...] = jnp.transpose(jnp.concatenate([w_t, pad], 0))


def merge_call(Xl, Xc, ctx_row0, Ya, Yd, Yf, Of, Ob, P_z, P_gate, mod_l, g2n, onorm_g, w_branch, w_out, w_router,
               router_bias, *, n_tok, n_batch, seq, tm=256):
    D = Xl.shape[1]
    n_lat_tiles = n_batch * seq // tm
    kern = functools.partial(_merge_kernel, tiles_per_batch=seq // tm, n_lat_tiles=n_lat_tiles, n_batch=n_batch)
    row = lambda w: pl.BlockSpec((tm, w), lambda i: (i, 0))
    full = lambda a: pl.BlockSpec(a.shape, lambda i: (0,) * a.ndim)
    wr_t = jnp.transpose(w_router)
    rb = router_bias.reshape(N_EXPERTS, 1)
    g2 = g2n.reshape(1, D)
    og = jnp.tile(onorm_g, GDN_HEADS).reshape(1, BRANCH_W)
    g64 = _group_ones(BRANCH_W, GDN_DIM)
    return pl.pallas_call(
        kern,
        grid=(n_tok // tm,),
        in_specs=_stream_specs(n_lat_tiles, ctx_row0 // tm, tm, D) + 3 * _stream_specs(n_lat_tiles, 0, tm, 256) + [
                  row(256), row(256), row(256), row(4 * D), full(mod_l),
                  full(g2), full(og), full(g64), full(w_branch), full(w_out), full(wr_t), full(rb)],
        out_specs=[row(D), row(D), row(128)],
        out_shape=[jax.ShapeDtypeStruct((n_tok, D), jnp.float32), jax.ShapeDtypeStruct((n_tok, D), jnp.bfloat16),
                   jax.ShapeDtypeStruct((n_tok, 128), jnp.float32)],
        compiler_params=pltpu.CompilerParams(vmem_limit_bytes=VMEM_LIMIT),
        name="merge",
    )(Xl, Xc, *Ya, *Yd, *Yf, Of, Ob, P_z, P_gate, mod_l, g2, og, g64, w_branch, w_out, wr_t, rb)


MOE_TILE = 1024
MOE_WIN = 320
MOE_WIN_CTX = 176
MOE_EPS = 4


def _moe_sorted_kernel(x_ref, h_ref, rw_ref, m_ref, wg_ref, wu_ref, wd_ref, o_ref, xs_ref, ws_ref, ys_ref, dest_ref,
                       seg_ref, *, win, mod_row_fn):
    T = h_ref.shape[0]
    per_group = N_EXPERTS // N_GROUPS
    i = pl.program_id(0)
    step = pl.program_id(1)
    lane = lax.broadcasted_iota(jnp.int32, (1, 128), 1)
    f32, bf16 = jnp.float32, jnp.bfloat16

    @pl.when(step == 0)
    def _():
        rw = rw_ref[...]
        r128 = lax.broadcasted_iota(jnp.int32, (128, 128), 0)
        c128 = lax.broadcasted_iota(jnp.int32, (128, 128), 1)
        e2g = ((r128 // per_group == c128) & (r128 < N_EXPERTS)).astype(bf16)
        og = (jnp.dot((rw > 0.0).astype(bf16), e2g, preferred_element_type=f32) > 0.5).astype(f32)
        rt = lax.broadcasted_iota(jnp.int32, (T, T), 0)
        ct = lax.broadcasted_iota(jnp.int32, (T, T), 1)
        earlier = jnp.dot((rt > ct).astype(bf16), og.astype(bf16), preferred_element_type=f32)
        cnt_row = jnp.sum(og, axis=0, keepdims=True)
        start_row = jnp.dot(jnp.broadcast_to(cnt_row, (8, 128)), (r128 < c128).astype(f32),
                            preferred_element_type=f32, precision=HI)[0:1]
        dest = jnp.sum(og * (earlier + start_row), axis=-1, keepdims=True)
        dest_ref[...] = jnp.broadcast_to(dest, (T, 128))
        dest_row = jnp.transpose(dest_ref[...])[0:1, :]
        perm = (dest_row == rt.astype(f32)).astype(bf16)
        xs_ref[...] = jnp.dot(perm, h_ref[...], preferred_element_type=f32).astype(bf16)
        ws_ref[...] = sum(jnp.dot(perm, piece, preferred_element_type=f32) for piece in _split_bf16(rw))
        ys_ref[...] = jnp.zeros(ys_ref.shape, f32)
        for g in range(N_GROUPS):
            seg_ref[g] = jnp.sum(jnp.where(lane == g, start_row, 0.0)).astype(jnp.int32)
            seg_ref[N_GROUPS + g] = jnp.sum(jnp.where(lane == g, cnt_row, 0.0)).astype(jnp.int32)

    g = (step * MOE_EPS) // per_group
    start = seg_ref[g]
    end = start + seg_ref[N_GROUPS + g]
    s0 = (start // 16) * 16
    n_win = (end - s0 + win - 1) // win

    def window(w, carry):
        lo = s0 + w * win
        off = pl.multiple_of(jnp.minimum(lo, T - win), 16)
        xw = xs_ref[pl.ds(off, win), :]
        ww = ws_ref[pl.ds(off, win), :]
        row = off + lax.broadcasted_iota(jnp.int32, (win, 1), 0)
        y = None
        for j in range(MOE_EPS):
            w_col = jnp.sum(jnp.where(lane == step * MOE_EPS + j, ww, 0.0), axis=-1, keepdims=True)
            w_col = jnp.where(row >= lo, w_col, 0.0)
            a = jnp.dot(xw, wg_ref[j], preferred_element_type=f32)
            u = jnp.dot(xw, wu_ref[j], preferred_element_type=f32)
            act = (a * jax.nn.sigmoid(a) * u * w_col).astype(bf16)
            yj = jnp.dot(act, wd_ref[j], preferred_element_type=f32)
            y = yj if y is None else y + yj
        ys_ref[pl.ds(off, win), :] += y
        return carry

    lax.fori_loop(0, n_win, window, 0)

    @pl.when(step == N_EXPERTS // MOE_EPS - 1)
    def _():
        ct = lax.broadcasted_iota(jnp.int32, (T, T), 1)
        unperm = (pltpu.repeat(dest_ref[...], T // 128, axis=1) == ct.astype(f32)).astype(bf16)
        f = jnp.dot(unperm, ys_ref[...].astype(bf16), preferred_element_type=f32)
        g2 = m_ref[pl.ds(mod_row_fn(i), 1), 5 * D_MODEL:6 * D_MODEL]
        o_ref[...] = x_ref[...] + g2 * f


def moe_sorted_call(Xmid, H2, RW, mod_l, w_gate, w_up, w_down, *, row0, n_rows, tile, win, mod_row_fn):
    D = Xmid.shape[1]
    F = w_gate.shape[-1]
    b0 = row0 // tile
    tok = lambda w: pl.BlockSpec((tile, w), lambda i, e: (b0 + i, 0))
    kern = functools.partial(_moe_sorted_kernel, win=win, mod_row_fn=mod_row_fn)
    return pl.pallas_call(
        kern,
        grid=(n_rows // tile, N_EXPERTS // MOE_EPS),
        in_specs=[tok(D), tok(D), tok(128),
                  pl.BlockSpec(mod_l.shape, lambda i, e: (0, 0)),
                  pl.BlockSpec((MOE_EPS, D, F), lambda i, e: (e, 0, 0)),
                  pl.BlockSpec((MOE_EPS, D, F), lambda i, e: (e, 0, 0)),
                  pl.BlockSpec((MOE_EPS, F, D), lambda i, e: (e, 0, 0))],
        out_specs=pl.BlockSpec((tile, D), lambda i, e: (i, 0)),
        out_shape=jax.ShapeDtypeStruct((n_rows, D), jnp.float32),
        scratch_shapes=[pltpu.VMEM((tile, D), jnp.bfloat16), pltpu.VMEM((tile, 128), jnp.float32),
                        pltpu.VMEM((tile, D), jnp.float32), pltpu.VMEM((tile, 128), jnp.float32),
                        pltpu.SMEM((2 * N_GROUPS,), jnp.int32)],
        compiler_params=pltpu.CompilerParams(vmem_limit_bytes=VMEM_LIMIT),
        name="moe_sorted",
    )(Xmid, H2, RW, mod_l, w_gate, w_up, w_down)


GDN_TILE = 256
GDN_CPT = GDN_TILE // GDN_CHUNK
GDN_LOCKSTEP = 2


def _mm(a, b, passes=1, dims=None):
    if dims is None:
        dot = lambda x, y: jnp.dot(x, y, preferred_element_type=jnp.float32)
    else:
        dot = lambda x, y: lax.dot_general(x, y, dims, preferred_element_type=jnp.float32)
    if passes == 1:
        return dot(a.astype(jnp.bfloat16), b.astype(jnp.bfloat16))
    a_hi, a_lo = _split_bf16(a)
    b_hi, b_lo = _split_bf16(b)
    return dot(a_hi, b_hi) + dot(a_hi, b_lo) + dot(a_lo, b_hi)


def _stack_heads(x):
    lane = lax.broadcasted_iota(jnp.int32, (1, 256), 1)
    return jnp.concatenate([jnp.where((lane // GDN_DIM) == h, x, 0.0) for h in range(GDN_HEADS)], 0)


def _slab(x):
    return x[0:64] + x[64:128] + x[128:192] + x[192:256]


def _unit_tri_inverse(mats):
    shape = mats[0].shape
    eye = (lax.broadcasted_iota(jnp.int32, shape, 0) == lax.broadcasted_iota(jnp.int32, shape, 1)).astype(jnp.float32)
    ps = [eye - a for a in mats]
    pws = list(mats)
    for _ in range(5):
        pws = [_mm(pw, pw, 3) for pw in pws]
        ps = [p + _mm(p, pw, 3) for p, pw in zip(ps, pws)]
    return ps


def _gdn_chunk_kernel(x_ref, xp_ref, xn_ref, ab_ref, cw_ref, par_ref, g64_ref, qe_ref, ou_ref, mm_ref, nn_ref,
                      *, tiles_per_batch, n_lat_tiles):
    i = pl.program_id(0)
    is_lat = i < n_lat_tiles
    first = jnp.where(is_lat, (i % tiles_per_batch) == 0, True)
    last = jnp.where(is_lat, (i % tiles_per_batch) == tiles_per_batch - 1, True)
    xp = jnp.concatenate([jnp.where(first, 0.0, xp_ref[...]), x_ref[...], jnp.where(last, 0.0, xn_ref[...])], 0)
    y = xp[6:6 + GDN_TILE] * cw_ref[0:1, :]
    for t in range(1, 5):
        y = y + xp[6 + t:6 + t + GDN_TILE] * cw_ref[t:t + 1, :]
    y = y * jax.nn.sigmoid(y)
    g64 = g64_ref[...]
    q = y[:, 0:256]
    k = y[:, 256:512]
    v = y[:, 512:768]
    q = q * lax.rsqrt(_group_sum(q * q, g64) + RMS_EPS) * (GDN_DIM ** -0.5)
    k = k * lax.rsqrt(_group_sum(k * k, g64) + RMS_EPS)
    ab = ab_ref[...]
    lane128 = lax.broadcasted_iota(jnp.int32, (1, 128), 1)
    g_all = jnp.where(lane128 < 8, par_ref[0:1, :] * jax.nn.softplus(ab + par_ref[1:2, :]), 0.0)
    beta_all = jax.nn.sigmoid(ab)
    r64 = lax.broadcasted_iota(jnp.int32, (64, 64), 0)
    c64 = lax.broadcasted_iota(jnp.int32, (64, 64), 1)
    low = (r64 >= c64).astype(jnp.float32)
    upp = (r64 <= c64).astype(jnp.float32)
    rr = lax.broadcasted_iota(jnp.int32, (256, 256), 0)
    cc = lax.broadcasted_iota(jnp.int32, (256, 256), 1)
    same = (rr // 64) == (cc // 64)
    eye = (rr == cc).astype(jnp.float32)
    for c0 in range(0, GDN_CPT, GDN_LOCKSTEP):
        stage = []
        for c in range(c0, c0 + GDN_LOCKSTEP):
            sl = slice(c * GDN_CHUNK, (c + 1) * GDN_CHUNK)
            g_c = g_all[sl]
            gsum = jnp.where(lane128 < 4, jnp.dot(low, g_c, preferred_element_type=jnp.float32, precision=HI),
                             jnp.dot(upp, g_c, preferred_element_type=jnp.float32, precision=HI))
            gsum_t = jnp.transpose(jnp.concatenate([gsum, jnp.zeros_like(gsum)], 0))[:, 0:64]
            tot = jnp.sum(g_c, axis=0, keepdims=True)
            beta_c = beta_all[sl]
            q_sm = _stack_heads(q[sl])
            k_sm = _stack_heads(k[sl])
            v_sm = _stack_heads(v[sl])
            kk = _mm(k_sm, k_sm, 1, NT_DIMS)
            qk = _mm(q_sm, k_sm, 1, NT_DIMS)
            for d in range(2):
                col = lambda h: 4 * d + h
                cmat = jnp.concatenate([jnp.broadcast_to(gsum[:, col(h):col(h) + 1], (64, 256)) for h in range(4)], 0)
                rrow = jnp.concatenate([gsum_t[col(h):col(h) + 1, :] for h in range(4)], 1)
                bmat = jnp.concatenate([jnp.broadcast_to(beta_c[:, 8 + col(h):9 + col(h)], (64, 256))
                                        for h in range(4)], 0)
                tmat = jnp.concatenate([jnp.broadcast_to(tot[:, col(h):col(h) + 1], (64, 256)) for h in range(4)], 0)
                tri = (rr % 64 >= cc % 64) if d == 0 else (rr % 64 <= cc % 64)
                incl = same & tri
                strict = incl & (rr != cc)
                dec = jnp.where(incl, jnp.exp(jnp.where(incl, cmat - rrow, 0.0)), 0.0)
                stage.append((d, sl, q_sm, k_sm, v_sm, qk, cmat, bmat, tmat, dec,
                              jnp.where(strict, kk * bmat * dec, 0.0)))
        t_invs = _unit_tri_inverse([st[-1] for st in stage])
        n = len(stage)
        q_sms, k_sms, v_sms, qks, cmats, bmats, tmats, decs = (
            [st[f] for st in stage] for f in range(2, 10))
        egs = [jnp.exp(cm) for cm in cmats]
        us = [_mm(t_invs[j], v_sms[j] * bmats[j], 1) for j in range(n)]
        ws = [_mm(t_invs[j], k_sms[j] * bmats[j] * egs[j], 1) for j in range(n)]
        qkms = [decs[j] * qks[j] for j in range(n)]
        qes = [q_sms[j] * egs[j] - _mm(qkms[j], ws[j], 1) for j in range(n)]
        ous = [_mm(qkms[j], us[j], 1) for j in range(n)]
        kd_ts = [jnp.transpose(k_sms[j] * jnp.exp(tmats[j] - cmats[j])) for j in range(n)]
        m_bds = [eye * jnp.exp(tmats[j]) - _mm(kd_ts[j], ws[j], 1) for j in range(n)]
        n_bds = [_mm(kd_ts[j], us[j], 1) for j in range(n)]
        for j in range(n):
            d, sl = stage[j][0], stage[j][1]
            qe_ref[d, sl, :] = _slab(qes[j])
            ou_ref[d, sl, :] = _slab(ous[j])
            mm_ref[d, sl, :] = _slab(m_bds[j])
            nn_ref[d, sl, :] = _slab(n_bds[j])


def gdn_chunk_call(P_gdn, P_ab, conv_w, a_log, dt_bias, *, n_batch, seq):
    NT = P_gdn.shape[0]
    tpb = seq // GDN_TILE
    n_lat = n_batch * tpb
    n_tiles = NT // GDN_TILE
    hb = GDN_TILE // 8
    cw = jnp.zeros((8, 768), jnp.float32).at[:5].set(conv_w)
    par = jnp.zeros((8, 128), jnp.float32)
    par = par.at[0, :8].set(-jnp.exp(a_log.reshape(8))).at[1, :8].set(dt_bias.reshape(8))
    g64 = _group_ones(256, 64)
    kern = functools.partial(_gdn_chunk_kernel, tiles_per_batch=tpb, n_lat_tiles=n_lat)
    full = lambda a: pl.BlockSpec(a.shape, lambda i: (0,) * a.ndim)
    out = pl.BlockSpec((2, GDN_TILE, 256), lambda i: (0, i, 0))
    return pl.pallas_call(
        kern,
        grid=(n_tiles,),
        in_specs=[pl.BlockSpec((GDN_TILE, 768), lambda i: (i, 0)),
                  pl.BlockSpec((8, 768), lambda i: (jnp.maximum(i * hb - 1, 0), 0)),
                  pl.BlockSpec((8, 768), lambda i: (jnp.minimum((i + 1) * hb, n_tiles * hb - 1), 0)),
                  pl.BlockSpec((GDN_TILE, 128), lambda i: (i, 0)),
                  full(cw), full(par), full(g64)],
        out_specs=[out] * 4,
        out_shape=[jax.ShapeDtypeStruct((2, NT, 256), jnp.float32)] * 4,
        compiler_params=pltpu.CompilerParams(vmem_limit_bytes=VMEM_LIMIT),
        name="gdn_chunk",
    )(P_gdn, P_gdn, P_gdn, P_ab, cw, par, g64)


def _gdn_scan_kernel(qef_ref, ouf_ref, mf_ref, nf_ref, qeb_ref, oub_ref, mb_ref, nb_ref, of_ref, ob_ref, s_ref):
    t = pl.program_id(1)

    @pl.when(t == 0)
    def _():
        s_ref[...] = jnp.zeros(s_ref.shape, jnp.float32)

    def step(d, c, qe_ref, ou_ref, m_ref, n_ref, o_ref):
        sl = slice(c * GDN_CHUNK, (c + 1) * GDN_CHUNK)
        s = s_ref[d]
        o_ref[sl, :] = _slab(_mm(_stack_heads(qe_ref[0, sl, :]), s, 1)) + ou_ref[0, sl, :]
        s_ref[d] = _mm(_stack_heads(m_ref[0, sl, :]), s, 3) + _stack_heads(n_ref[0, sl, :])

    for c in range(GDN_CPT):
        step(0, c, qef_ref, ouf_ref, mf_ref, nf_ref, of_ref)
        step(1, GDN_CPT - 1 - c, qeb_ref, oub_ref, mb_ref, nb_ref, ob_ref)


def gdn_scan_call(QE, OU, MM, NN, *, n_batch, seq):
    NT = QE.shape[1]
    tpb = seq // GDN_TILE
    cblk = n_batch * tpb
    fwd = lambda b, t: jnp.where(t == 0, cblk + b, b * tpb + t - 1)
    bwd = lambda b, t: jnp.where(t == 0, cblk + b, b * tpb + tpb - t)
    spec = lambda d, f: pl.BlockSpec((1, GDN_TILE, 256), lambda b, t: (d, f(b, t), 0))
    ospec = lambda f: pl.BlockSpec((GDN_TILE, 256), lambda b, t: (f(b, t), 0))
    return pl.pallas_call(
        _gdn_scan_kernel,
        grid=(n_batch, tpb + 1),
        in_specs=[spec(0, fwd)] * 4 + [spec(1, bwd)] * 4,
        out_specs=[ospec(fwd), ospec(bwd)],
        out_shape=[jax.ShapeDtypeStruct((NT, 256), jnp.float32)] * 2,
        scratch_shapes=[pltpu.VMEM((2, 256, 256), jnp.float32)],
        compiler_params=pltpu.CompilerParams(vmem_limit_bytes=VMEM_LIMIT),
        name="gdn_scan",
    )(QE, OU, MM, NN, QE, OU, MM, NN)


def _dft_cs(n):
    a = 2.0 * np.pi * np.outer(np.arange(n), np.arange(n)) / n
    return np.cos(a), np.sin(a)


def _channel_dft():
    c, s = _dft_cs(FNET_GROUP_W)
    eye = np.eye(FNET_GROUPS)
    return np.concatenate([np.kron(eye, c), np.kron(eye, s)], 1)


FNET_SUB = 16


def _fnet_a_kernel(x_ref, cs_ref, m_ref, cphi_ref, sphi_ref, br_ref, bi_ref, *, n_rows):
    for j in range(FNET_SUB):
        sl = slice(j * BRANCH_W, (j + 1) * BRANCH_W)
        u = jnp.dot(x_ref[:, j, :], cs_ref[...], preferred_element_type=jnp.float32)
        st = jnp.concatenate([u[:, :BRANCH_W], u[:, BRANCH_W:]], 0).astype(jnp.bfloat16)
        a = jnp.dot(m_ref[...], st, preferred_element_type=jnp.float32)
        ar, ai = a[:n_rows], a[n_rows:]
        cp, sp = cphi_ref[:, sl], sphi_ref[:, sl]
        br_ref[j] = ar * cp - ai * sp
        bi_ref[j] = ar * sp + ai * cp


def _fnet_c_kernel(br_ref, bi_ref, m_ref, o_ref):
    for j in range(FNET_SUB):
        st = jnp.concatenate([br_ref[:, j, :], bi_ref[:, j, :]], 0).astype(jnp.bfloat16)
        o_ref[:, j, :] = jnp.dot(m_ref[...], st, preferred_element_type=jnp.float32)


def fourier_latent_call(P_f, *, n_batch, seq):
    rows = seq // GRID_W
    bf16, f32 = jnp.bfloat16, jnp.float32
    c1, s1 = _dft_cs(rows)
    norm = 1.0 / math.sqrt(seq * FNET_GROUP_W)
    m_a = jnp.asarray(np.block([[c1, -s1], [s1, c1]]) * norm, bf16)
    c2, s2 = _dft_cs(GRID_W)
    m_c = jnp.asarray(np.concatenate([c2, -s2], 1), bf16)
    cs = jnp.asarray(_channel_dft(), bf16)
    phi = 2.0 * np.pi * np.outer(np.arange(rows), np.arange(GRID_W)) / seq
    cphi = jnp.repeat(jnp.asarray(np.cos(phi), f32), BRANCH_W, axis=1)
    sphi = jnp.repeat(jnp.asarray(np.sin(phi), f32), BRANCH_W, axis=1)
    xv = P_f.reshape(P_f.shape[0] // GRID_W, GRID_W, BRANCH_W)
    nj = GRID_W // FNET_SUB
    full = lambda a: pl.BlockSpec(a.shape, lambda b, j: (0,) * a.ndim)
    tw = pl.BlockSpec((rows, FNET_SUB * BRANCH_W), lambda b, j: (0, j))
    mid = jax.ShapeDtypeStruct((n_batch * GRID_W, rows, BRANCH_W), f32)
    br, bi = pl.pallas_call(
        functools.partial(_fnet_a_kernel, n_rows=rows),
        grid=(n_batch, nj),
        in_specs=[pl.BlockSpec((rows, FNET_SUB, BRANCH_W), lambda b, j: (b, j, 0)), full(cs), full(m_a), tw, tw],
        out_specs=[pl.BlockSpec((FNET_SUB, rows, BRANCH_W), lambda b, j: (b * nj + j, 0, 0))] * 2,
        out_shape=[mid, mid],
        name="fnet_rows",
    )(xv, cs, m_a, cphi, sphi)
    blk = pl.BlockSpec((GRID_W, FNET_SUB, BRANCH_W), lambda b, i: (b, i, 0))
    y = pl.pallas_call(
        _fnet_c_kernel,
        grid=(n_batch, rows // FNET_SUB),
        in_specs=[blk, blk, pl.BlockSpec(m_c.shape, lambda b, i: (0, 0))],
        out_specs=blk,
        out_shape=mid,
        name="fnet_cols",
    )(br, bi, m_c)
    return y.reshape(n_batch * seq, BRANCH_W)


def _fnet_ctx_kernel(x_ref, cs_ref, m_ref, o_ref):
    u = jnp.dot(x_ref[...], cs_ref[...], preferred_element_type=jnp.float32)
    st = jnp.concatenate([u[:, :BRANCH_W], u[:, BRANCH_W:]], 0).astype(jnp.bfloat16)
    o_ref[...] = jnp.dot(m_ref[...], st, preferred_element_type=jnp.float32).astype(o_ref.dtype)


def fourier_ctx_call(P_f, *, n_batch, seq, ctx_len):
    bf16 = jnp.bfloat16
    c, s = _dft_cs(ctx_len)
    m = jnp.asarray(np.concatenate([c, -s], 1) / math.sqrt(ctx_len * FNET_GROUP_W), bf16)
    cs = jnp.asarray(_channel_dft(), bf16)
    cb = n_batch * seq // ctx_len
    return pl.pallas_call(
        _fnet_ctx_kernel,
        grid=(n_batch,),
        in_specs=[pl.BlockSpec((ctx_len, BRANCH_W), lambda b: (cb + b, 0)),
                  pl.BlockSpec(cs.shape, lambda b: (0, 0)), pl.BlockSpec(m.shape, lambda b: (0, 0))],
        out_specs=pl.BlockSpec((ctx_len, BRANCH_W), lambda b: (b, 0)),
        out_shape=jax.ShapeDtypeStruct((n_batch * ctx_len, BRANCH_W), jnp.float32),
        name="fnet_ctx",
    )(P_f, cs, m)


def kernel(x, c, ctx, c_ctx, w_mod, b_mod, norm1_g, norm2_g, w_in, na_qn_g, na_kn_g, na_rpb,
           da_qn_g, da_kn_g, da_lam_q1, da_lam_k1, da_lam_q2, da_lam_k2, da_subln_g, gdn_conv_w,
           gdn_a_log, gdn_dt_bias, gdn_onorm_g, w_branch, w_out, w_router, router_bias,
           moe_w_gate, moe_w_up, moe_w_down):
    B, S, D = x.shape
    L = ctx.shape[1]
    NL = B * S
    f32 = jnp.float32
    cvec = jnp.zeros((8, D), f32).at[:B].set(c).at[B].set(c_ctx)
    mods = mod_call(cvec, w_mod, b_mod)
    cos, sin = rope_tables(S, 256)
    bf16 = jnp.bfloat16
    Xl, Xc = x.reshape(NL, D), ctx.reshape(B * L, D)
    for l in range(DEPTH):
        last = l == DEPTH - 1
        lam_init = 0.8 - 0.6 * math.exp(-0.3 * l)
        w_a = w_in[l, :, :2560].astype(bf16)
        w_b = w_in[l, :, 2576:].astype(bf16)
        w_ab = jnp.pad(w_in[l, :, 2560:2576], ((0, 0), (0, 112))).astype(bf16)
        NAQ, NAK, NAV, DAQ, DAK, DAVX, P_gdn, P_z, P_f, P_gate, P_ab = inproj_call(
            Xl, Xc, 0, mods[l], norm1_g[l], w_a, w_b, w_ab, cos, sin, na_qn_g[l], na_kn_g[l], da_qn_g[l], da_kn_g[l],
            n_batch=B, seq=S, ctx_len=L)
        bias = na_bias_tables(na_rpb[l], S // GRID_W)
        lam = (jnp.exp(jnp.sum(da_lam_q1[l] * da_lam_k1[l])) - jnp.exp(jnp.sum(da_lam_q2[l] * da_lam_k2[l]))
               + lam_init).astype(f32)
        Ya = na_call(NAQ, NAK, NAV, bias, n_batch=B, seq=S, ctx_len=L)
        Yd = da_call(lam, DAQ, DAK, DAVX, da_subln_g[l], n_batch=B, seq=S, ctx_len=L, lam_init=lam_init)
        QE, OU, MM, NN = gdn_chunk_call(P_gdn, P_ab, gdn_conv_w[l], gdn_a_log[l], gdn_dt_bias[l], n_batch=B, seq=S)
        Of, Ob = gdn_scan_call(QE, OU, MM, NN, n_batch=B, seq=S)
        Yf = fourier_latent_call(P_f, n_batch=B, seq=S)
        n_tok = NL
        Yac, Ydc, Yfc = Ya, Yd, Yf
        if not last:
            Yac, Ydc = ctx_attn_call(lam, NAQ, NAK, NAV, DAQ, DAK, DAVX, da_subln_g[l],
                                     n_batch=B, seq=S, ctx_len=L, lam_init=lam_init)
            Yfc = fourier_ctx_call(P_f, n_batch=B, seq=S, ctx_len=L)
            n_tok = NL + B * L
        Xmid, H2, RW = merge_call(Xl, Xc, 0, (Ya, Yac), (Yd, Ydc), (Yf, Yfc), Of, Ob, P_z, P_gate, mods[l], norm2_g[l], gdn_onorm_g[l],
                                  w_branch[l].astype(bf16), w_out[l].astype(bf16), w_router, router_bias,
                                  n_tok=n_tok, n_batch=B, seq=S)
        experts = (moe_w_gate[l].astype(bf16), moe_w_up[l].astype(bf16), moe_w_down[l].astype(bf16))
        Xl = moe_sorted_call(Xmid, H2, RW, mods[l], *experts, row0=0, n_rows=NL, tile=MOE_TILE, win=MOE_WIN,
                             mod_row_fn=lambda i: i // (S // MOE_TILE))
        if not last:
            Xc = moe_sorted_call(Xmid, H2, RW, mods[l], *experts, row0=NL, n_rows=B * L, tile=B * L,
                                 win=MOE_WIN_CTX, mod_row_fn=lambda i: B)
    return Xl.reshape(B, S, D)
```

```python
import functools
import math
import jax
import jax.numpy as jnp
from jax import lax
import numpy as np
from jax.experimental import pallas as pl
from jax.experimental.pallas import tpu as pltpu

D_MODEL = 1024
DEPTH = 2
GRID_W = 64
N_BRANCH = 4
BRANCH_W = D_MODEL // N_BRANCH
GDN_HEADS = 4
GDN_DIM = BRANCH_W // GDN_HEADS
GDN_CHUNK = 64
FNET_GROUPS = 4
FNET_GROUP_W = BRANCH_W // FNET_GROUPS
N_EXPERTS = 16
N_GROUPS = 4
RMS_EPS = 1e-6
NEG_INF = -1e30
ROPE_BASE = 10000.0
VMEM_LIMIT = 56 * 1024 * 1024
HI = lax.Precision.HIGHEST
NT_DIMS = (((1,), (1,)), ((), ()))
LOG2E = 1.4426950408889634


def _mod_kernel(c_ref, w_ref, b_ref, o_ref):
    c = c_ref[...]
    a = c * jax.nn.sigmoid(c)
    o_ref[0] = jnp.dot(a, w_ref[0], preferred_element_type=jnp.float32, precision=HI) + b_ref[0]


def mod_call(cvec, w_mod, b_mod, tn=512):
    depth, D, N = w_mod.shape
    return pl.pallas_call(
        _mod_kernel,
        grid=(depth, N // tn),
        in_specs=[pl.BlockSpec((8, D), lambda l, j: (0, 0)),
                  pl.BlockSpec((1, D, tn), lambda l, j: (l, 0, j)),
                  pl.BlockSpec((1, 1, tn), lambda l, j: (l, 0, j))],
        out_specs=pl.BlockSpec((1, 8, tn), lambda l, j: (l, 0, j)),
        out_shape=jax.ShapeDtypeStruct((depth, 8, N), jnp.float32),
        name="mod",
    )(cvec, w_mod, b_mod.reshape(depth, 1, N))


def _stream_specs(n_lat_tiles, ctx_tile0, tm, width):
    return [pl.BlockSpec((tm, width), lambda i: (jnp.minimum(i, n_lat_tiles - 1), 0)),
            pl.BlockSpec((tm, width), lambda i: (ctx_tile0 + jnp.maximum(i - n_lat_tiles, 0), 0))]


def _inproj_kernel(xl_ref, xc_ref, m_ref, g_ref, w_ref, wb_ref, wab_ref, cos_ref, sin_ref, g64_ref, g32_ref, gains_ref,
                   naq_ref, nak_ref, nav_ref, daq_ref, dak_ref, davx_ref, gdn_ref, z_ref, f_ref, gate_ref, ab_ref,
                   *, tiles_per_batch, n_lat_tiles, n_batch):
    i = pl.program_id(0)
    r = jnp.where(i < n_lat_tiles, i // tiles_per_batch, n_batch)
    x = jnp.where(i < n_lat_tiles, xl_ref[...], xc_ref[...])
    y = x * lax.rsqrt(jnp.mean(x * x, axis=-1, keepdims=True) + RMS_EPS) * g_ref[...]
    sh = m_ref[pl.ds(r, 1), 0:D_MODEL]
    sc = m_ref[pl.ds(r, 1), D_MODEL:2 * D_MODEL]
    h = (y * (1.0 + sc) + sh).astype(jnp.bfloat16)
    proj = lambda a, b: jnp.dot(h, w_ref[:, a:b], preferred_element_type=jnp.float32)
    proj_b = lambda a, b: jnp.dot(h, wb_ref[:, a:b], preferred_element_type=jnp.float32)
    bf16 = jnp.bfloat16

    g64 = g64_ref[...]
    naq_ref[...] = (_group_rms(proj(0, 256), g64, 64, gains_ref[0:1, :]) * (64 ** -0.5)).astype(bf16)
    nak_ref[...] = _group_rms(proj(256, 512), g64, 64, gains_ref[1:2, :]).astype(bf16)
    nav_ref[...] = proj(512, 768).astype(bf16)

    g32 = g32_ref[...]
    cos = cos_ref[...]
    sin = sin_ref[...]
    lane = lax.broadcasted_iota(jnp.int32, (1, 256), 1)
    first = (lane % 16) < 8

    def rope(t):
        swapped = jnp.where(first, pltpu.roll(t, 256 - 8, 1), pltpu.roll(t, 8, 1))
        return t * cos + swapped * sin

    q = rope(_group_rms(proj(768, 1024), g32, 32, gains_ref[2:3, :]))
    k = rope(_group_rms(proj(1024, 1280), g32, 32, gains_ref[3:4, :]))
    daq_ref[...] = (q * (32 ** -0.5 * LOG2E)).astype(bf16)
    dak_ref[...] = k.astype(bf16)
    v = proj(1280, 1536).astype(bf16)
    one_col = (lax.broadcasted_iota(jnp.int32, (v.shape[0], 64), 1) == 0).astype(bf16)
    davx_ref[...] = jnp.concatenate(
        [piece for hd in range(4) for piece in (v[:, hd * 64:(hd + 1) * 64], one_col)], -1)

    gdn_ref[...] = proj(1536, 2304)
    z_ref[...] = proj(2304, 2560).astype(bf16)
    f_ref[...] = proj_b(0, 256).astype(bf16)
    for kk in range(4):
        gate_ref[:, kk * 1024:(kk + 1) * 1024] = proj_b(256 + kk * 1024, 256 + (kk + 1) * 1024).astype(bf16)
    ab_ref[...] = jnp.dot(h, wab_ref[...], preferred_element_type=jnp.float32)


def inproj_call(Xl, Xc, ctx_row0, mod_l, g, w_a, w_b, w_ab, cos, sin, na_qg, na_kg, da_qg, da_kg,
                *, n_batch, seq, ctx_len, tm=256):
    D = Xl.shape[1]
    NT = n_batch * (seq + ctx_len)
    tpb = seq // tm
    n_lat_tiles = n_batch * tpb
    kern = functools.partial(_inproj_kernel, tiles_per_batch=tpb, n_lat_tiles=n_lat_tiles, n_batch=n_batch)
    bf16, f32 = jnp.bfloat16, jnp.float32
    widths = [(256, bf16)] * 5 + [(512, bf16), (768, f32), (256, bf16), (256, bf16), (4096, bf16), (128, f32)]
    gains = jnp.stack([jnp.tile(na_qg, 4), jnp.tile(na_kg, 4), jnp.tile(da_qg, 8), jnp.tile(da_kg, 8)]
                      + [jnp.zeros((256,), f32)] * 4, 0)
    g64, g32 = _group_ones(256, 64), _group_ones(256, 32)
    tab = pl.BlockSpec((tm, 256), lambda i: (jnp.where(i < n_lat_tiles, i % tpb, tpb), 0))
    full = lambda a: pl.BlockSpec(a.shape, lambda i: (0,) * a.ndim)
    return pl.pallas_call(
        kern,
        grid=(NT // tm,),
        in_specs=_stream_specs(n_lat_tiles, ctx_row0 // tm, tm, D) + [
            full(mod_l), pl.BlockSpec((1, D), lambda i: (0, 0)), full(w_a), full(w_b), full(w_ab), tab, tab,
            full(g64), full(g32), full(gains)],
        out_specs=[pl.BlockSpec((tm, w), lambda i: (i, 0)) for w, _ in widths],
        out_shape=[jax.ShapeDtypeStruct((NT, w), dt) for w, dt in widths],
        compiler_params=pltpu.CompilerParams(vmem_limit_bytes=VMEM_LIMIT),
        name="inproj",
    )(Xl, Xc, mod_l, g.reshape(1, D), w_a, w_b, w_ab, cos, sin, g64, g32, gains)


def _group_ones(width, group):
    i = np.arange(width)
    return jnp.asarray((i[:, None] // group == i[None, :] // group).astype(np.float32), jnp.bfloat16)


def _group_sum(xx, gmat):
    hi, lo = _split_bf16(xx)
    return (jnp.dot(hi, gmat, preferred_element_type=jnp.float32)
            + jnp.dot(lo, gmat, preferred_element_type=jnp.float32))


def _group_rms(x, gmat, group, gain):
    return x * lax.rsqrt(_group_sum(x * x, gmat) * (1.0 / group) + RMS_EPS) * gain


def rope_tables(seq, tm):
    nf = 8
    t = jnp.arange(seq)
    rows = (t // GRID_W).astype(jnp.float32)
    cols = (t % GRID_W).astype(jnp.float32)
    freqs = ROPE_BASE ** (-jnp.arange(nf, dtype=jnp.float32) / nf)
    d = np.arange(32)
    f_idx = d % 8
    use_col = (d // 16) == 1
    ang = jnp.where(use_col[None, :], cols[:, None], rows[:, None]) * freqs[f_idx][None, :]
    sign = np.where((d % 16) < 8, -1.0, 1.0).astype(np.float32)
    cos = jnp.tile(jnp.cos(ang), (1, 8))
    sin = jnp.tile(jnp.sin(ang) * sign[None, :], (1, 8))
    cos = jnp.concatenate([cos, jnp.ones((tm, 256), jnp.float32)], 0)
    sin = jnp.concatenate([sin, jnp.zeros((tm, 256), jnp.float32)], 0)
    return cos, sin


NA_RB = 4
NA_UR = 11


def na_bias_tables(rpb, rows):
    reps = [(0, 0), (4, 0), (rows - NA_RB, rows - NA_UR)]
    qc = np.arange(GRID_W)[:, None]
    kc = np.arange(GRID_W)[None, :]
    ws = np.clip(qc - 8, 0, GRID_W - 16)
    vcol = (kc >= ws) & (kc < ws + 16)
    rel_c = np.clip(kc - qc + 15, 0, 30)
    sel_c = (rel_c[..., None] == np.arange(31)).astype(np.float32)
    sel_r, vrow = [], []
    for r0, u0 in reps:
        r = r0 + np.arange(NA_RB)[:, None]
        krow = u0 + np.arange(NA_UR)[None, :]
        start = np.clip(r - 4, 0, rows - 8)
        vrow.append((krow >= start) & (krow < start + 8))
        sel_r.append((np.clip(krow - r + 7, 0, 14)[..., None] == np.arange(15)).astype(np.float32))
    sel_r, vrow = np.stack(sel_r), np.stack(vrow)
    b = jnp.einsum('taui,hij,qkj->thaquk', jnp.asarray(sel_r), rpb.astype(jnp.float32), jnp.asarray(sel_c),
                   precision=HI)
    valid = vrow[:, None, :, None, :, None] & vcol[None, None, None, :, None, :]
    b = jnp.where(valid, b, NEG_INF)
    return b.reshape(3, rpb.shape[0], NA_RB * GRID_W, NA_UR * GRID_W).astype(jnp.bfloat16)


def _na_kernel(q_ref, k_ref, v_ref, kc_ref, vc_ref, bias_ref, o_ref, *, rows):
    i = pl.program_id(1)
    u0 = jnp.clip(i * NA_RB - 4, 0, rows - NA_UR)
    off = pl.multiple_of(u0 * GRID_W, GRID_W)
    nkw = NA_UR * GRID_W
    kwin = k_ref[pl.ds(off, nkw), :]
    vwin = v_ref[pl.ds(off, nkw), :]
    q = q_ref[...]
    kc = kc_ref[...]
    vc = vc_ref[...]
    outs = []
    for h in range(4):
        sl = slice(h * 64, (h + 1) * 64)
        qh = q[:, sl]
        s_loc = lax.dot_general(qh, kwin[:, sl], NT_DIMS, preferred_element_type=jnp.float32) + bias_ref[0, h].astype(jnp.float32)
        s_ctx = lax.dot_general(qh, kc[:, sl], NT_DIMS, preferred_element_type=jnp.float32)
        m = jnp.maximum(jnp.max(s_loc, -1, keepdims=True), jnp.max(s_ctx, -1, keepdims=True))
        p_loc = jnp.exp(s_loc - m)
        p_ctx = jnp.exp(s_ctx - m)
        l = jnp.sum(p_loc, -1, keepdims=True) + jnp.sum(p_ctx, -1, keepdims=True)
        o = (jnp.dot(p_loc.astype(jnp.bfloat16), vwin[:, sl], preferred_element_type=jnp.float32)
             + jnp.dot(p_ctx.astype(jnp.bfloat16), vc[:, sl], preferred_element_type=jnp.float32))
        outs.append(o / l)
    o_ref[...] = jnp.concatenate(outs, -1).astype(o_ref.dtype)


def na_call(NAQ, NAK, NAV, bias, *, n_batch, seq, ctx_len):
    rows = seq // GRID_W
    nblk = rows // NA_RB
    tq = NA_RB * GRID_W
    cb = n_batch * seq // ctx_len
    kern = functools.partial(_na_kernel, rows=rows)
    tsel = lambda b, i: (jnp.where(i == 0, 0, jnp.where(i == nblk - 1, 2, 1)), 0, 0, 0)
    return pl.pallas_call(
        kern,
        grid=(n_batch, nblk),
        in_specs=[pl.BlockSpec((tq, 256), lambda b, i: (b * nblk + i, 0)),
                  pl.BlockSpec((seq, 256), lambda b, i: (b, 0)),
                  pl.BlockSpec((seq, 256), lambda b, i: (b, 0)),
                  pl.BlockSpec((ctx_len, 256), lambda b, i: (cb + b, 0)),
                  pl.BlockSpec((ctx_len, 256), lambda b, i: (cb + b, 0)),
                  pl.BlockSpec((1,) + bias.shape[1:], tsel)],
        out_specs=pl.BlockSpec((tq, 256), lambda b, i: (b * nblk + i, 0)),
        out_shape=jax.ShapeDtypeStruct((n_batch * seq, 256), jnp.bfloat16),
        compiler_params=pltpu.CompilerParams(vmem_limit_bytes=VMEM_LIMIT),
        name="na_attn",
    )(NAQ, NAK, NAV, NAK, NAV, bias)


def _stack_masked_q(q):
    lane = lax.broadcasted_iota(jnp.int32, (1, 256), 1)
    return jnp.concatenate([jnp.where((lane // 32) == hm, q, jnp.zeros_like(q)) for hm in range(8)], 0)


def _da_finish(acc, lam, gain, g64, post_scale):
    tq = acc.shape[0] // 8
    norm = acc / jnp.broadcast_to(acc[:, 64:65], acc.shape)
    lane = lax.broadcasted_iota(jnp.int32, (1, 128), 1)
    halves = []
    for hp in range(2):
        d = [norm[(4 * hp + 2 * j) * tq:(4 * hp + 2 * j + 1) * tq]
             - lam * norm[(4 * hp + 2 * j + 1) * tq:(4 * hp + 2 * j + 2) * tq] for j in range(2)]
        halves.append(jnp.where(lane < 64, d[0], pltpu.roll(d[1], 64, 1)))
    o = jnp.concatenate(halves, -1)
    return _group_rms(o, g64, 64, gain) * post_scale


def _da_tile(qs, k, vx, m, acc, tq):
    tk = k.shape[0]
    s = lax.dot_general(qs, k, NT_DIMS, preferred_element_type=jnp.float32)
    m_new = jnp.maximum(m, jnp.max(s, -1, keepdims=True))
    alpha = jnp.exp2(m - m_new)
    p = jnp.exp2(s - pltpu.repeat(m_new, tk // 128, axis=1)).astype(jnp.bfloat16)
    pv = jnp.concatenate(
        [jnp.dot(p[2 * h * tq:(2 * h + 2) * tq], vx[:, h * 128:(h + 1) * 128], preferred_element_type=jnp.float32)
         for h in range(4)], 0)
    return m_new, alpha * acc + pv


def _da_kernel(lam_ref, q_ref, k_ref, vx_ref, kc_ref, vxc_ref, gain_ref, g64_ref, o_ref, m_ref, acc_ref,
               *, tk, n_kv, post_scale):
    tq = q_ref.shape[0]
    qs = _stack_masked_q(q_ref[...])
    m_ref[...] = jnp.full(m_ref.shape, -jnp.inf, jnp.float32)
    acc_ref[...] = jnp.zeros(acc_ref.shape, jnp.float32)

    def body(j, carry):
        for u in range(2):
            off = pl.multiple_of((2 * j + u) * tk, tk)
            m, acc = _da_tile(qs, k_ref[pl.ds(off, tk), :], vx_ref[pl.ds(off, tk), :], m_ref[...], acc_ref[...], tq)
            m_ref[...] = m
            acc_ref[...] = acc
        return carry

    lax.fori_loop(0, n_kv // 2 - 1, body, 0, unroll=True)
    off = (n_kv - 2) * tk
    m, acc = _da_tile(qs, k_ref[off:off + tk, :], vx_ref[off:off + tk, :], m_ref[...], acc_ref[...], tq)
    k_last = jnp.concatenate([k_ref[off + tk:off + 2 * tk, :], kc_ref[...]], 0)
    vx_last = jnp.concatenate([vx_ref[off + tk:off + 2 * tk, :], vxc_ref[...]], 0)
    m, acc = _da_tile(qs, k_last, vx_last, m, acc, tq)
    o_ref[...] = _da_finish(acc, lam_ref[0], gain_ref[...], g64_ref[...], post_scale).astype(o_ref.dtype)


def da_call(lam, DAQ, DAK, DAVX, subln_g, *, n_batch, seq, ctx_len, lam_init, tq=256, tk=1024):
    tk = min(tk, seq // 2)
    nq = seq // tq
    cb = n_batch * seq // ctx_len
    g64 = _group_ones(256, 64)
    gain = jnp.tile(subln_g, 4).reshape(1, 256)
    n_kv = seq // tk
    assert n_kv % 2 == 0
    kern = functools.partial(_da_kernel, tk=tk, n_kv=n_kv, post_scale=1.0 - lam_init)
    return pl.pallas_call(
        kern,
        grid=(n_batch, nq),
        in_specs=[pl.BlockSpec(memory_space=pltpu.SMEM),
                  pl.BlockSpec((tq, 256), lambda b, i: (b * nq + i, 0)),
                  pl.BlockSpec((seq, 256), lambda b, i: (b, 0)),
                  pl.BlockSpec((seq, 512), lambda b, i: (b, 0)),
                  pl.BlockSpec((ctx_len, 256), lambda b, i: (cb + b, 0)),
                  pl.BlockSpec((ctx_len, 512), lambda b, i: (cb + b, 0)),
                  pl.BlockSpec((1, 256), lambda b, i: (0, 0)),
                  pl.BlockSpec((256, 256), lambda b, i: (0, 0))],
        out_specs=pl.BlockSpec((tq, 256), lambda b, i: (b * nq + i, 0)),
        out_shape=jax.ShapeDtypeStruct((n_batch * seq, 256), jnp.bfloat16),
        scratch_shapes=[pltpu.VMEM((8 * tq, 128), jnp.float32), pltpu.VMEM((8 * tq, 128), jnp.float32)],
        compiler_params=pltpu.CompilerParams(vmem_limit_bytes=VMEM_LIMIT),
        name="da_attn",
    )(lam.reshape(1), DAQ, DAK, DAVX, DAK, DAVX, gain, g64)


def _ctx_kernel(lam_ref, naq_ref, nak_ref, nav_ref, daq_ref, dak_ref, dav_ref, gain_ref, g64_ref, ya_ref, yd_ref,
                *, post_scale):
    q = naq_ref[...]
    k = nak_ref[...]
    v = nav_ref[...]
    outs = []
    for h in range(4):
        sl = slice(h * 64, (h + 1) * 64)
        s = lax.dot_general(q[:, sl], k[:, sl], NT_DIMS, preferred_element_type=jnp.float32)
        p = jnp.exp(s - jnp.max(s, -1, keepdims=True))
        o = jnp.dot(p.astype(jnp.bfloat16), v[:, sl], preferred_element_type=jnp.float32)
        outs.append(o / jnp.sum(p, -1, keepdims=True))
    ya_ref[...] = jnp.concatenate(outs, -1).astype(ya_ref.dtype)
    tq = daq_ref.shape[0]
    qs = _stack_masked_q(daq_ref[...])
    m0 = jnp.full((8 * tq, 128), -jnp.inf, jnp.float32)
    m, acc = _da_tile(qs, dak_ref[...], dav_ref[...], m0, jnp.zeros((8 * tq, 128), jnp.float32), tq)
    yd_ref[...] = _da_finish(acc, lam_ref[0], gain_ref[...], g64_ref[...], post_scale).astype(yd_ref.dtype)


def ctx_attn_call(lam, NAQ, NAK, NAV, DAQ, DAK, DAVX, subln_g, *, n_batch, seq, ctx_len, lam_init):
    cb = n_batch * seq // ctx_len
    g64 = _group_ones(256, 64)
    gain = jnp.tile(subln_g, 4).reshape(1, 256)
    blk = lambda col: pl.BlockSpec((ctx_len, 256), lambda b: (cb + b, col))
    return pl.pallas_call(
        functools.partial(_ctx_kernel, post_scale=1.0 - lam_init),
        grid=(n_batch,),
        in_specs=[pl.BlockSpec(memory_space=pltpu.SMEM), blk(0), blk(0), blk(0), blk(0), blk(0),
                  pl.BlockSpec((ctx_len, 512), lambda b: (cb + b, 0)),
                  pl.BlockSpec((1, 256), lambda b: (0, 0)), pl.BlockSpec((256, 256), lambda b: (0, 0))],
        out_specs=[pl.BlockSpec((ctx_len, 256), lambda b: (b, 0))] * 2,
        out_shape=[jax.ShapeDtypeStruct((n_batch * ctx_len, 256), jnp.bfloat16)] * 2,
        name="ctx_attn",
    )(lam.reshape(1), NAQ, NAK, NAV, DAQ, DAK, DAVX, gain, g64)


def _split_bf16(a):
    hi = a.astype(jnp.bfloat16)
    return hi, (a - hi.astype(jnp.float32)).astype(jnp.bfloat16)


def _route(logits_t, bias_col):
    per_group = N_EXPERTS // N_GROUPS
    scores = jax.nn.sigmoid(logits_t)
    sel_all = scores + bias_col
    sel = [sel_all[e:e + 1, :] for e in range(N_EXPERTS)]
    top2 = []
    for e in range(N_EXPERTS):
        g0 = e // per_group * per_group
        rank = jnp.zeros_like(sel[e])
        for o in range(g0, g0 + per_group):
            if o != e:
                beats = (sel[o] > sel[e]) | (sel[o] == sel[e]) if o < e else (sel[o] > sel[e])
                rank = rank + beats.astype(jnp.float32)
        top2.append(rank < 1.5)
    gs = []
    for g in range(N_GROUPS):
        acc = jnp.zeros_like(sel[0])
        for e in range(g * per_group, (g + 1) * per_group):
            acc = acc + jnp.where(top2[e], sel[e], 0.0)
        gs.append(acc)
    rows = []
    for g in range(N_GROUPS):
        beaten = jnp.zeros_like(sel[0])
        for o in range(N_GROUPS):
            if o != g:
                b = (gs[o] >= gs[g]) if o < g else (gs[o] > gs[g])
                beaten = beaten + b.astype(jnp.float32)
        best = beaten < 0.5
        for e in range(g * per_group, (g + 1) * per_group):
            rows.append(jnp.where(best & top2[e], scores[e:e + 1, :], 0.0))
    w = jnp.concatenate(rows, 0)
    return w / jnp.sum(w, axis=0, keepdims=True)


def _merge_kernel(xl_ref, xc_ref, yal_ref, yac_ref, ydl_ref, ydc_ref, yfl_ref, yfc_ref, of_ref, ob_ref, z_ref, gate_ref,
                  m_ref, g2_ref, og_ref, g64_ref, wb_ref, wo_ref, wr_ref, rb_ref, xo_ref, h2_ref, rw_ref, *, tiles_per_batch, n_lat_tiles, n_batch):
    i = pl.program_id(0)
    r = jnp.where(i < n_lat_tiles, i // tiles_per_batch, n_batch)
    z = z_ref[...].astype(jnp.float32)
    yg = (_group_rms(of_ref[...] + ob_ref[...], g64_ref[...], GDN_DIM, og_ref[...]) * (z * jax.nn.sigmoid(z)))
    is_lat = i < n_lat_tiles
    pick = lambda lat_ref, ctx_ref: jnp.where(is_lat, lat_ref[...], ctx_ref[...])
    branches = (pick(yal_ref, yac_ref), pick(ydl_ref, ydc_ref), yg.astype(jnp.bfloat16),
                pick(yfl_ref, yfc_ref).astype(jnp.bfloat16))
    mix = None
    for b, yb in enumerate(branches):
        proj = jnp.dot(yb, wb_ref[b], preferred_element_type=jnp.float32)
        term = jax.nn.sigmoid(gate_ref[:, b * D_MODEL:(b + 1) * D_MODEL].astype(jnp.float32)) * proj
        mix = term if mix is None else mix + term
    y = jnp.dot(mix.astype(jnp.bfloat16), wo_ref[...], preferred_element_type=jnp.float32)
    mrow = lambda k: m_ref[pl.ds(r, 1), k * D_MODEL:(k + 1) * D_MODEL]
    x = jnp.where(i < n_lat_tiles, xl_ref[...], xc_ref[...]) + mrow(2) * y
    xo_ref[...] = x
    h2 = (x * lax.rsqrt(jnp.mean(x * x, axis=-1, keepdims=True) + RMS_EPS) * g2_ref[...]) * (1.0 + mrow(4)) + mrow(3)
    h2_ref[...] = h2.astype(h2_ref.dtype)
    h_hi, h_lo = _split_bf16(h2)
    w_hi, w_lo = _split_bf16(wr_ref[...])
    nt = lambda a, b: lax.dot_general(a, b, NT_DIMS, preferred_element_type=jnp.float32)
    logits_t = nt(w_hi, h_hi) + nt(w_hi, h_lo) + nt(w_lo, h_hi)
    w_t = _route(logits_t, rb_ref[...])
    pad = jnp.zeros((128 - N_EXPERTS, w_t.shape[1]), jnp.float32)
    rw_ref[...] = jnp.transpose(jnp.concatenate([w_t, pad], 0))


def merge_call(Xl, Xc, ctx_row0, Ya, Yd, Yf, Of, Ob, P_z, P_gate, mod_l, g2n, onorm_g, w_branch, w_out, w_router,
               router_bias, *, n_tok, n_batch, seq, tm=256):
    D = Xl.shape[1]
    n_lat_tiles = n_batch * seq // tm
    kern = functools.partial(_merge_kernel, tiles_per_batch=seq // tm, n_lat_tiles=n_lat_tiles, n_batch=n_batch)
    row = lambda w: pl.BlockSpec((tm, w), lambda i: (i, 0))
    full = lambda a: pl.BlockSpec(a.shape, lambda i: (0,) * a.ndim)
    wr_t = jnp.transpose(w_router)
    rb = router_bias.reshape(N_EXPERTS, 1)
    g2 = g2n.reshape(1, D)
    og = jnp.tile(onorm_g, GDN_HEADS).reshape(1, BRANCH_W)
    g64 = _group_ones(BRANCH_W, GDN_DIM)
    return pl.pallas_call(
        kern,
        grid=(n_tok // tm,),
        in_specs=_stream_specs(n_lat_tiles, ctx_row0 // tm, tm, D) + 3 * _stream_specs(n_lat_tiles, 0, tm, 256) + [
                  row(256), row(256), row(256), row(4 * D), full(mod_l),
                  full(g2), full(og), full(g64), full(w_branch), full(w_out), full(wr_t), full(rb)],
        out_specs=[row(D), row(D), row(128)],
        out_shape=[jax.ShapeDtypeStruct((n_tok, D), jnp.float32), jax.ShapeDtypeStruct((n_tok, D), jnp.bfloat16),
                   jax.ShapeDtypeStruct((n_tok, 128), jnp.float32)],
        compiler_params=pltpu.CompilerParams(vmem_limit_bytes=VMEM_LIMIT),
        name="merge",
    )(Xl, Xc, *Ya, *Yd, *Yf, Of, Ob, P_z, P_gate, mod_l, g2, og, g64, w_branch, w_out, wr_t, rb)


MOE_TILE = 1024
MOE_WIN = 320
MOE_WIN_CTX = 176
MOE_EPS = 4


def _moe_sorted_kernel(x_ref, h_ref, rw_ref, m_ref, wg_ref, wu_ref, wd_ref, o_ref, xs_ref, ws_ref, ys_ref, dest_ref,
                       seg_ref, *, win, mod_row_fn):
    T = h_ref.shape[0]
    per_group = N_EXPERTS // N_GROUPS
    i = pl.program_id(0)
    step = pl.program_id(1)
    lane = lax.broadcasted_iota(jnp.int32, (1, 128), 1)
    f32, bf16 = jnp.float32, jnp.bfloat16

    @pl.when(step == 0)
    def _():
        rw = rw_ref[...]
        r128 = lax.broadcasted_iota(jnp.int32, (128, 128), 0)
        c128 = lax.broadcasted_iota(jnp.int32, (128, 128), 1)
        e2g = ((r128 // per_group == c128) & (r128 < N_EXPERTS)).astype(bf16)
        og = (jnp.dot((rw > 0.0).astype(bf16), e2g, preferred_element_type=f32) > 0.5).astype(f32)
        rt = lax.broadcasted_iota(jnp.int32, (T, T), 0)
        ct = lax.broadcasted_iota(jnp.int32, (T, T), 1)
        earlier = jnp.dot((rt > ct).astype(bf16), og.astype(bf16), preferred_element_type=f32)
        cnt_row = jnp.sum(og, axis=0, keepdims=True)
        start_row = jnp.dot(jnp.broadcast_to(cnt_row, (8, 128)), (r128 < c128).astype(f32),
                            preferred_element_type=f32, precision=HI)[0:1]
        dest = jnp.sum(og * (earlier + start_row), axis=-1, keepdims=True)
        dest_ref[...] = jnp.broadcast_to(dest, (T, 128))
        dest_row = jnp.transpose(dest_ref[...])[0:1, :]
        perm = (dest_row == rt.astype(f32)).astype(bf16)
        xs_ref[...] = jnp.dot(perm, h_ref[...], preferred_element_type=f32).astype(bf16)
        ws_ref[...] = sum(jnp.dot(perm, piece, preferred_element_type=f32) for piece in _split_bf16(rw))
        ys_ref[...] = jnp.zeros(ys_ref.shape, f32)
        for g in range(N_GROUPS):
            seg_ref[g] = jnp.sum(jnp.where(lane == g, start_row, 0.0)).astype(jnp.int32)
            seg_ref[N_GROUPS + g] = jnp.sum(jnp.where(lane == g, cnt_row, 0.0)).astype(jnp.int32)

    g = (step * MOE_EPS) // per_group
    start = seg_ref[g]
    end = start + seg_ref[N_GROUPS + g]
    s0 = (start // 16) * 16
    n_win = (end - s0 + win - 1) // win

    def window(w, carry):
        lo = s0 + w * win
        off = pl.multiple_of(jnp.minimum(lo, T - win), 16)
        xw = xs_ref[pl.ds(off, win), :]
        ww = ws_ref[pl.ds(off, win), :]
        row = off + lax.broadcasted_iota(jnp.int32, (win, 1), 0)
        y = None
        for j in range(MOE_EPS):
            w_col = jnp.sum(jnp.where(lane == step * MOE_EPS + j, ww, 0.0), axis=-1, keepdims=True)
            w_col = jnp.where(row >= lo, w_col, 0.0)
            a = jnp.dot(xw, wg_ref[0, j], preferred_element_type=f32)
            u = jnp.dot(xw, wu_ref[0, j], preferred_element_type=f32)
            act = (a * jax.nn.sigmoid(a) * u * w_col).astype(bf16)
            yj = jnp.dot(act, wd_ref[0, j], preferred_element_type=f32)
            y = yj if y is None else y + yj
        ys_ref[pl.ds(off, win), :] += y
        return carry

    lax.fori_loop(0, n_win, window, 0)

    @pl.when(step == N_EXPERTS // MOE_EPS - 1)
    def _():
        ct = lax.broadcasted_iota(jnp.int32, (T, T), 1)
        unperm = (pltpu.repeat(dest_ref[...], T // 128, axis=1) == ct.astype(f32)).astype(bf16)
        f = jnp.dot(unperm, ys_ref[...].astype(bf16), preferred_element_type=f32)
        g2 = m_ref[pl.ds(mod_row_fn(i), 1), 5 * D_MODEL:6 * D_MODEL]
        o_ref[...] = x_ref[...] + g2 * f


def moe_sorted_call(Xmid, H2, RW, mod_l, w_gate, w_up, w_down, *, layer, row0, n_rows, tile, win, mod_row_fn):
    D = Xmid.shape[1]
    F = w_gate.shape[-1]
    b0 = row0 // tile
    tok = lambda w: pl.BlockSpec((tile, w), lambda i, e: (b0 + i, 0))
    kern = functools.partial(_moe_sorted_kernel, win=win, mod_row_fn=mod_row_fn)
    return pl.pallas_call(
        kern,
        grid=(n_rows // tile, N_EXPERTS // MOE_EPS),
        in_specs=[tok(D), tok(D), tok(128),
                  pl.BlockSpec(mod_l.shape, lambda i, e: (0, 0)),
                  pl.BlockSpec((1, MOE_EPS, D, F), lambda i, e: (layer, e, 0, 0)),
                  pl.BlockSpec((1, MOE_EPS, D, F), lambda i, e: (layer, e, 0, 0)),
                  pl.BlockSpec((1, MOE_EPS, F, D), lambda i, e: (layer, e, 0, 0))],
        out_specs=pl.BlockSpec((tile, D), lambda i, e: (i, 0)),
        out_shape=jax.ShapeDtypeStruct((n_rows, D), jnp.float32),
        scratch_shapes=[pltpu.VMEM((tile, D), jnp.bfloat16), pltpu.VMEM((tile, 128), jnp.float32),
                        pltpu.VMEM((tile, D), jnp.float32), pltpu.VMEM((tile, 128), jnp.float32),
                        pltpu.SMEM((2 * N_GROUPS,), jnp.int32)],
        compiler_params=pltpu.CompilerParams(vmem_limit_bytes=VMEM_LIMIT),
        name="moe_sorted",
    )(Xmid, H2, RW, mod_l, w_gate, w_up, w_down)


GDN_TILE = 256
GDN_CPT = GDN_TILE // GDN_CHUNK
GDN_LOCKSTEP = 2


def _mm(a, b, passes=1, dims=None):
    if dims is None:
        dot = lambda x, y: jnp.dot(x, y, preferred_element_type=jnp.float32)
    else:
        dot = lambda x, y: lax.dot_general(x, y, dims, preferred_element_type=jnp.float32)
    if passes == 1:
        return dot(a.astype(jnp.bfloat16), b.astype(jnp.bfloat16))
    a_hi, a_lo = _split_bf16(a)
    b_hi, b_lo = _split_bf16(b)
    return dot(a_hi, b_hi) + dot(a_hi, b_lo) + dot(a_lo, b_hi)


def _stack_heads(x):
    lane = lax.broadcasted_iota(jnp.int32, (1, 256), 1)
    return jnp.concatenate([jnp.where((lane // GDN_DIM) == h, x, 0.0) for h in range(GDN_HEADS)], 0)


def _slab(x):
    return x[0:64] + x[64:128] + x[128:192] + x[192:256]


def _unit_tri_inverse(mats):
    shape = mats[0].shape
    eye = (lax.broadcasted_iota(jnp.int32, shape, 0) == lax.broadcasted_iota(jnp.int32, shape, 1)).astype(jnp.float32)
    ps = [eye - a for a in mats]
    pws = list(mats)
    for _ in range(5):
        pws = [_mm(pw, pw, 3) for pw in pws]
        ps = [p + _mm(p, pw, 3) for p, pw in zip(ps, pws)]
    return ps


def _gdn_chunk_kernel(x_ref, xp_ref, xn_ref, ab_ref, cw_ref, par_ref, g64_ref, qe_ref, ou_ref, mm_ref, nn_ref,
                      *, tiles_per_batch, n_lat_tiles):
    i = pl.program_id(0)
    is_lat = i < n_lat_tiles
    first = jnp.where(is_lat, (i % tiles_per_batch) == 0, True)
    last = jnp.where(is_lat, (i % tiles_per_batch) == tiles_per_batch - 1, True)
    xp = jnp.concatenate([jnp.where(first, 0.0, xp_ref[...]), x_ref[...], jnp.where(last, 0.0, xn_ref[...])], 0)
    y = xp[6:6 + GDN_TILE] * cw_ref[0:1, :]
    for t in range(1, 5):
        y = y + xp[6 + t:6 + t + GDN_TILE] * cw_ref[t:t + 1, :]
    y = y * jax.nn.sigmoid(y)
    g64 = g64_ref[...]
    q = y[:, 0:256]
    k = y[:, 256:512]
    v = y[:, 512:768]
    q = q * lax.rsqrt(_group_sum(q * q, g64) + RMS_EPS) * (GDN_DIM ** -0.5)
    k = k * lax.rsqrt(_group_sum(k * k, g64) + RMS_EPS)
    ab = ab_ref[...]
    lane128 = lax.broadcasted_iota(jnp.int32, (1, 128), 1)
    g_all = jnp.where(lane128 < 8, par_ref[0:1, :] * jax.nn.softplus(ab + par_ref[1:2, :]), 0.0)
    beta_all = jax.nn.sigmoid(ab)
    r64 = lax.broadcasted_iota(jnp.int32, (64, 64), 0)
    c64 = lax.broadcasted_iota(jnp.int32, (64, 64), 1)
    low = (r64 >= c64).astype(jnp.float32)
    upp = (r64 <= c64).astype(jnp.float32)
    rr = lax.broadcasted_iota(jnp.int32, (256, 256), 0)
    cc = lax.broadcasted_iota(jnp.int32, (256, 256), 1)
    same = (rr // 64) == (cc // 64)
    eye = (rr == cc).astype(jnp.float32)
    for c0 in range(0, GDN_CPT, GDN_LOCKSTEP):
        stage = []
        for c in range(c0, c0 + GDN_LOCKSTEP):
            sl = slice(c * GDN_CHUNK, (c + 1) * GDN_CHUNK)
            g_c = g_all[sl]
            gsum = jnp.where(lane128 < 4, jnp.dot(low, g_c, preferred_element_type=jnp.float32, precision=HI),
                             jnp.dot(upp, g_c, preferred_element_type=jnp.float32, precision=HI))
            gsum_t = jnp.transpose(jnp.concatenate([gsum, jnp.zeros_like(gsum)], 0))[:, 0:64]
            tot = jnp.sum(g_c, axis=0, keepdims=True)
            beta_c = beta_all[sl]
            q_sm = _stack_heads(q[sl])
            k_sm = _stack_heads(k[sl])
            v_sm = _stack_heads(v[sl])
            kk = _mm(k_sm, k_sm, 1, NT_DIMS)
            qk = _mm(q_sm, k_sm, 1, NT_DIMS)
            for d in range(2):
                col = lambda h: 4 * d + h
                cmat = jnp.concatenate([jnp.broadcast_to(gsum[:, col(h):col(h) + 1], (64, 256)) for h in range(4)], 0)
                rrow = jnp.concatenate([gsum_t[col(h):col(h) + 1, :] for h in range(4)], 1)
                bmat = jnp.concatenate([jnp.broadcast_to(beta_c[:, 8 + col(h):9 + col(h)], (64, 256))
                                        for h in range(4)], 0)
                tmat = jnp.concatenate([jnp.broadcast_to(tot[:, col(h):col(h) + 1], (64, 256)) for h in range(4)], 0)
                tri = (rr % 64 >= cc % 64) if d == 0 else (rr % 64 <= cc % 64)
                incl = same & tri
                strict = incl & (rr != cc)
                dec = jnp.where(incl, jnp.exp(jnp.where(incl, cmat - rrow, 0.0)), 0.0)
                stage.append((d, sl, q_sm, k_sm, v_sm, qk, cmat, bmat, tmat, dec,
                              jnp.where(strict, kk * bmat * dec, 0.0)))
        t_invs = _unit_tri_inverse([st[-1] for st in stage])
        n = len(stage)
        q_sms, k_sms, v_sms, qks, cmats, bmats, tmats, decs = (
            [st[f] for st in stage] for f in range(2, 10))
        egs = [jnp.exp(cm) for cm in cmats]
        us = [_mm(t_invs[j], v_sms[j] * bmats[j], 1) for j in range(n)]
        ws = [_mm(t_invs[j], k_sms[j] * bmats[j] * egs[j], 1) for j in range(n)]
        qkms = [decs[j] * qks[j] for j in range(n)]
        qes = [q_sms[j] * egs[j] - _mm(qkms[j], ws[j], 1) for j in range(n)]
        ous = [_mm(qkms[j], us[j], 1) for j in range(n)]
        kd_ts = [jnp.transpose(k_sms[j] * jnp.exp(tmats[j] - cmats[j])) for j in range(n)]
        m_bds = [eye * jnp.exp(tmats[j]) - _mm(kd_ts[j], ws[j], 1) for j in range(n)]
        n_bds = [_mm(kd_ts[j], us[j], 1) for j in range(n)]
        for j in range(n):
            d, sl = stage[j][0], stage[j][1]
            qe_ref[d, sl, :] = _slab(qes[j])
            ou_ref[d, sl, :] = _slab(ous[j])
            mm_ref[d, sl, :] = _slab(m_bds[j])
            nn_ref[d, sl, :] = _slab(n_bds[j])


def gdn_chunk_call(P_gdn, P_ab, conv_w, a_log, dt_bias, *, n_batch, seq):
    NT = P_gdn.shape[0]
    tpb = seq // GDN_TILE
    n_lat = n_batch * tpb
    n_tiles = NT // GDN_TILE
    hb = GDN_TILE // 8
    cw = jnp.zeros((8, 768), jnp.float32).at[:5].set(conv_w)
    par = jnp.zeros((8, 128), jnp.float32)
    par = par.at[0, :8].set(-jnp.exp(a_log.reshape(8))).at[1, :8].set(dt_bias.reshape(8))
    g64 = _group_ones(256, 64)
    kern = functools.partial(_gdn_chunk_kernel, tiles_per_batch=tpb, n_lat_tiles=n_lat)
    full = lambda a: pl.BlockSpec(a.shape, lambda i: (0,) * a.ndim)
    out = pl.BlockSpec((2, GDN_TILE, 256), lambda i: (0, i, 0))
    return pl.pallas_call(
        kern,
        grid=(n_tiles,),
        in_specs=[pl.BlockSpec((GDN_TILE, 768), lambda i: (i, 0)),
                  pl.BlockSpec((8, 768), lambda i: (jnp.maximum(i * hb - 1, 0), 0)),
                  pl.BlockSpec((8, 768), lambda i: (jnp.minimum((i + 1) * hb, n_tiles * hb - 1), 0)),
                  pl.BlockSpec((GDN_TILE, 128), lambda i: (i, 0)),
                  full(cw), full(par), full(g64)],
        out_specs=[out] * 4,
        out_shape=[jax.ShapeDtypeStruct((2, NT, 256), jnp.float32)] * 4,
        compiler_params=pltpu.CompilerParams(vmem_limit_bytes=VMEM_LIMIT),
        name="gdn_chunk",
    )(P_gdn, P_gdn, P_gdn, P_ab, cw, par, g64)


def _gdn_scan_kernel(qef_ref, ouf_ref, mf_ref, nf_ref, qeb_ref, oub_ref, mb_ref, nb_ref, of_ref, ob_ref, s_ref):
    t = pl.program_id(1)

    @pl.when(t == 0)
    def _():
        s_ref[...] = jnp.zeros(s_ref.shape, jnp.float32)

    def step(d, c, qe_ref, ou_ref, m_ref, n_ref, o_ref):
        sl = slice(c * GDN_CHUNK, (c + 1) * GDN_CHUNK)
        s = s_ref[d]
        o_ref[sl, :] = _slab(_mm(_stack_heads(qe_ref[0, sl, :]), s, 1)) + ou_ref[0, sl, :]
        s_ref[d] = _mm(_stack_heads(m_ref[0, sl, :]), s, 3) + _stack_heads(n_ref[0, sl, :])

    for c in range(GDN_CPT):
        step(0, c, qef_ref, ouf_ref, mf_ref, nf_ref, of_ref)
        step(1, GDN_CPT - 1 - c, qeb_ref, oub_ref, mb_ref, nb_ref, ob_ref)


def gdn_scan_call(QE, OU, MM, NN, *, n_batch, seq):
    NT = QE.shape[1]
    tpb = seq // GDN_TILE
    cblk = n_batch * tpb
    fwd = lambda b, t: jnp.where(t == 0, cblk + b, b * tpb + t - 1)
    bwd = lambda b, t: jnp.where(t == 0, cblk + b, b * tpb + tpb - t)
    spec = lambda d, f: pl.BlockSpec((1, GDN_TILE, 256), lambda b, t: (d, f(b, t), 0))
    ospec = lambda f: pl.BlockSpec((GDN_TILE, 256), lambda b, t: (f(b, t), 0))
    return pl.pallas_call(
        _gdn_scan_kernel,
        grid=(n_batch, tpb + 1),
        in_specs=[spec(0, fwd)] * 4 + [spec(1, bwd)] * 4,
        out_specs=[ospec(fwd), ospec(bwd)],
        out_shape=[jax.ShapeDtypeStruct((NT, 256), jnp.float32)] * 2,
        scratch_shapes=[pltpu.VMEM((2, 256, 256), jnp.float32)],
        compiler_params=pltpu.CompilerParams(vmem_limit_bytes=VMEM_LIMIT),
        name="gdn_scan",
    )(QE, OU, MM, NN, QE, OU, MM, NN)


def _dft_cs(n):
    a = 2.0 * np.pi * np.outer(np.arange(n), np.arange(n)) / n
    return np.cos(a), np.sin(a)


def _channel_dft():
    c, s = _dft_cs(FNET_GROUP_W)
    eye = np.eye(FNET_GROUPS)
    return np.concatenate([np.kron(eye, c), np.kron(eye, s)], 1)


FNET_SUB = 16


def _fnet_a_kernel(x_ref, cs_ref, m_ref, cphi_ref, sphi_ref, br_ref, bi_ref, *, n_rows):
    for j in range(FNET_SUB):
        sl = slice(j * BRANCH_W, (j + 1) * BRANCH_W)
        u = jnp.dot(x_ref[:, j, :], cs_ref[...], preferred_element_type=jnp.float32)
        st = jnp.concatenate([u[:, :BRANCH_W], u[:, BRANCH_W:]], 0).astype(jnp.bfloat16)
        a = jnp.dot(m_ref[...], st, preferred_element_type=jnp.float32)
        ar, ai = a[:n_rows], a[n_rows:]
        cp, sp = cphi_ref[:, sl], sphi_ref[:, sl]
        br_ref[j] = ar * cp - ai * sp
        bi_ref[j] = ar * sp + ai * cp


def _fnet_c_kernel(br_ref, bi_ref, m_ref, o_ref):
    for j in range(FNET_SUB):
        st = jnp.concatenate([br_ref[:, j, :], bi_ref[:, j, :]], 0).astype(jnp.bfloat16)
        o_ref[:, j, :] = jnp.dot(m_ref[...], st, preferred_element_type=jnp.float32)


def fourier_latent_call(P_f, *, n_batch, seq):
    rows = seq // GRID_W
    bf16, f32 = jnp.bfloat16, jnp.float32
    c1, s1 = _dft_cs(rows)
    norm = 1.0 / math.sqrt(seq * FNET_GROUP_W)
    m_a = jnp.asarray(np.block([[c1, -s1], [s1, c1]]) * norm, bf16)
    c2, s2 = _dft_cs(GRID_W)
    m_c = jnp.asarray(np.concatenate([c2, -s2], 1), bf16)
    cs = jnp.asarray(_channel_dft(), bf16)
    phi = 2.0 * np.pi * np.outer(np.arange(rows), np.arange(GRID_W)) / seq
    cphi = jnp.repeat(jnp.asarray(np.cos(phi), f32), BRANCH_W, axis=1)
    sphi = jnp.repeat(jnp.asarray(np.sin(phi), f32), BRANCH_W, axis=1)
    xv = P_f.reshape(P_f.shape[0] // GRID_W, GRID_W, BRANCH_W)
    nj = GRID_W // FNET_SUB
    full = lambda a: pl.BlockSpec(a.shape, lambda b, j: (0,) * a.ndim)
    tw = pl.BlockSpec((rows, FNET_SUB * BRANCH_W), lambda b, j: (0, j))
    mid = jax.ShapeDtypeStruct((n_batch * GRID_W, rows, BRANCH_W), f32)
    br, bi = pl.pallas_call(
        functools.partial(_fnet_a_kernel, n_rows=rows),
        grid=(n_batch, nj),
        in_specs=[pl.BlockSpec((rows, FNET_SUB, BRANCH_W), lambda b, j: (b, j, 0)), full(cs), full(m_a), tw, tw],
        out_specs=[pl.BlockSpec((FNET_SUB, rows, BRANCH_W), lambda b, j: (b * nj + j, 0, 0))] * 2,
        out_shape=[mid, mid],
        name="fnet_rows",
    )(xv, cs, m_a, cphi, sphi)
    blk = pl.BlockSpec((GRID_W, FNET_SUB, BRANCH_W), lambda b, i: (b, i, 0))
    y = pl.pallas_call(
        _fnet_c_kernel,
        grid=(n_batch, rows // FNET_SUB),
        in_specs=[blk, blk, pl.BlockSpec(m_c.shape, lambda b, i: (0, 0))],
        out_specs=blk,
        out_shape=mid,
        name="fnet_cols",
    )(br, bi, m_c)
    return y.reshape(n_batch * seq, BRANCH_W)


def _fnet_ctx_kernel(x_ref, cs_ref, m_ref, o_ref):
    u = jnp.dot(x_ref[...], cs_ref[...], preferred_element_type=jnp.float32)
    st = jnp.concatenate([u[:, :BRANCH_W], u[:, BRANCH_W:]], 0).astype(jnp.bfloat16)
    o_ref[...] = jnp.dot(m_ref[...], st, preferred_element_type=jnp.float32).astype(o_ref.dtype)


def fourier_ctx_call(P_f, *, n_batch, seq, ctx_len):
    bf16 = jnp.bfloat16
    c, s = _dft_cs(ctx_len)
    m = jnp.asarray(np.concatenate([c, -s], 1) / math.sqrt(ctx_len * FNET_GROUP_W), bf16)
    cs = jnp.asarray(_channel_dft(), bf16)
    cb = n_batch * seq // ctx_len
    return pl.pallas_call(
        _fnet_ctx_kernel,
        grid=(n_batch,),
        in_specs=[pl.BlockSpec((ctx_len, BRANCH_W), lambda b: (cb + b, 0)),
                  pl.BlockSpec(cs.shape, lambda b: (0, 0)), pl.BlockSpec(m.shape, lambda b: (0, 0))],
        out_specs=pl.BlockSpec((ctx_len, BRANCH_W), lambda b: (b, 0)),
        out_shape=jax.ShapeDtypeStruct((n_batch * ctx_len, BRANCH_W), jnp.float32),
        name="fnet_ctx",
    )(P_f, cs, m)


def kernel(x, c, ctx, c_ctx, w_mod, b_mod, norm1_g, norm2_g, w_in, na_qn_g, na_kn_g, na_rpb,
           da_qn_g, da_kn_g, da_lam_q1, da_lam_k1, da_lam_q2, da_lam_k2, da_subln_g, gdn_conv_w,
           gdn_a_log, gdn_dt_bias, gdn_onorm_g, w_branch, w_out, w_router, router_bias,
           moe_w_gate, moe_w_up, moe_w_down):
    B, S, D = x.shape
    L = ctx.shape[1]
    NL = B * S
    f32 = jnp.float32
    cvec = jnp.zeros((8, D), f32).at[:B].set(c).at[B].set(c_ctx)
    mods = mod_call(cvec, w_mod, b_mod)
    cos, sin = rope_tables(S, 256)
    bf16 = jnp.bfloat16
    Xl, Xc = x.reshape(NL, D), ctx.reshape(B * L, D)
    experts = (moe_w_gate.astype(bf16), moe_w_up.astype(bf16), moe_w_down.astype(bf16))
    for l in range(DEPTH):
        last = l == DEPTH - 1
        lam_init = 0.8 - 0.6 * math.exp(-0.3 * l)
        w_a = w_in[l, :, :2560].astype(bf16)
        w_b = w_in[l, :, 2576:].astype(bf16)
        w_ab = jnp.pad(w_in[l, :, 2560:2576], ((0, 0), (0, 112))).astype(bf16)
        NAQ, NAK, NAV, DAQ, DAK, DAVX, P_gdn, P_z, P_f, P_gate, P_ab = inproj_call(
            Xl, Xc, 0, mods[l], norm1_g[l], w_a, w_b, w_ab, cos, sin, na_qn_g[l], na_kn_g[l], da_qn_g[l], da_kn_g[l],
            n_batch=B, seq=S, ctx_len=L)
        bias = na_bias_tables(na_rpb[l], S // GRID_W)
        lam = (jnp.exp(jnp.sum(da_lam_q1[l] * da_lam_k1[l])) - jnp.exp(jnp.sum(da_lam_q2[l] * da_lam_k2[l]))
               + lam_init).astype(f32)
        Ya = na_call(NAQ, NAK, NAV, bias, n_batch=B, seq=S, ctx_len=L)
        Yd = da_call(lam, DAQ, DAK, DAVX, da_subln_g[l], n_batch=B, seq=S, ctx_len=L, lam_init=lam_init)
        QE, OU, MM, NN = gdn_chunk_call(P_gdn, P_ab, gdn_conv_w[l], gdn_a_log[l], gdn_dt_bias[l], n_batch=B, seq=S)
        Of, Ob = gdn_scan_call(QE, OU, MM, NN, n_batch=B, seq=S)
        Yf = fourier_latent_call(P_f, n_batch=B, seq=S)
        n_tok = NL
        Yac, Ydc, Yfc = Ya, Yd, Yf
        if not last:
            Yac, Ydc = ctx_attn_call(lam, NAQ, NAK, NAV, DAQ, DAK, DAVX, da_subln_g[l],
                                     n_batch=B, seq=S, ctx_len=L, lam_init=lam_init)
            Yfc = fourier_ctx_call(P_f, n_batch=B, seq=S, ctx_len=L)
            n_tok = NL + B * L
        Xmid, H2, RW = merge_call(Xl, Xc, 0, (Ya, Yac), (Yd, Ydc), (Yf, Yfc), Of, Ob, P_z, P_gate, mods[l], norm2_g[l], gdn_onorm_g[l],
                                  w_branch[l].astype(bf16), w_out[l].astype(bf16), w_router, router_bias,
                                  n_tok=n_tok, n_batch=B, seq=S)
        Xl = moe_sorted_call(Xmid, H2, RW, mods[l], *experts, layer=l, row0=0, n_rows=NL, tile=MOE_TILE, win=MOE_WIN,
                             mod_row_fn=lambda i: i // (S // MOE_TILE))
        if not last:
            Xc = moe_sorted_call(Xmid, H2, RW, mods[l], *experts, layer=l, row0=NL, n_rows=B * L, tile=B * L,
                                 win=MOE_WIN_CTX, mod_row_fn=lambda i: B)
    return Xl.reshape(B, S, D)
```

```python
import functools
import math
import jax
import jax.numpy as jnp
from jax import lax
import numpy as np
from jax.experimental import pallas as pl
from jax.experimental.pallas import tpu as pltpu

D_MODEL = 1024
DEPTH = 2
GRID_W = 64
N_BRANCH = 4
BRANCH_W = D_MODEL // N_BRANCH
GDN_HEADS = 4
GDN_DIM = BRANCH_W // GDN_HEADS
GDN_CHUNK = 64
FNET_GROUPS = 4
FNET_GROUP_W = BRANCH_W // FNET_GROUPS
N_EXPERTS = 16
N_GROUPS = 4
RMS_EPS = 1e-6
NEG_INF = -1e30
ROPE_BASE = 10000.0
VMEM_LIMIT = 56 * 1024 * 1024
HI = lax.Precision.HIGHEST
NT_DIMS = (((1,), (1,)), ((), ()))
LOG2E = 1.4426950408889634


def _mod_kernel(c_ref, w_ref, b_ref, o_ref):
    c = c_ref[...]
    a = c * jax.nn.sigmoid(c)
    o_ref[0] = jnp.dot(a, w_ref[0], preferred_element_type=jnp.float32, precision=HI) + b_ref[0]


def mod_call(cvec, w_mod, b_mod, tn=512):
    depth, D, N = w_mod.shape
    return pl.pallas_call(
        _mod_kernel,
        grid=(depth, N // tn),
        in_specs=[pl.BlockSpec((8, D), lambda l, j: (0, 0)),
                  pl.BlockSpec((1, D, tn), lambda l, j: (l, 0, j)),
                  pl.BlockSpec((1, 1, tn), lambda l, j: (l, 0, j))],
        out_specs=pl.BlockSpec((1, 8, tn), lambda l, j: (l, 0, j)),
        out_shape=jax.ShapeDtypeStruct((depth, 8, N), jnp.float32),
        name="mod",
    )(cvec, w_mod, b_mod.reshape(depth, 1, N))


def _stream_specs(n_lat_tiles, ctx_tile0, tm, width):
    return [pl.BlockSpec((tm, width), lambda i: (jnp.minimum(i, n_lat_tiles - 1), 0)),
            pl.BlockSpec((tm, width), lambda i: (ctx_tile0 + jnp.maximum(i - n_lat_tiles, 0), 0))]


def _inproj_kernel(xl_ref, xc_ref, m_ref, g_ref, w_ref, wb_ref, wab_ref, cos_ref, sin_ref, g64_ref, g32_ref, gains_ref,
                   naq_ref, nak_ref, nav_ref, daq_ref, dak_ref, davx_ref, gdn_ref, z_ref, f_ref, gate_ref, ab_ref,
                   *, tiles_per_batch, n_lat_tiles, n_batch):
    i = pl.program_id(0)
    r = jnp.where(i < n_lat_tiles, i // tiles_per_batch, n_batch)
    x = jnp.where(i < n_lat_tiles, xl_ref[...], xc_ref[...])
    y = x * lax.rsqrt(jnp.mean(x * x, axis=-1, keepdims=True) + RMS_EPS) * g_ref[...]
    sh = m_ref[pl.ds(r, 1), 0:D_MODEL]
    sc = m_ref[pl.ds(r, 1), D_MODEL:2 * D_MODEL]
    h = (y * (1.0 + sc) + sh).astype(jnp.bfloat16)
    proj = lambda a, b: jnp.dot(h, w_ref[:, a:b], preferred_element_type=jnp.float32)
    proj_b = lambda a, b: jnp.dot(h, wb_ref[:, a:b], preferred_element_type=jnp.float32)
    bf16 = jnp.bfloat16

    g64 = g64_ref[...]
    naq_ref[...] = (_group_rms(proj(0, 256), g64, 64, gains_ref[0:1, :]) * (64 ** -0.5)).astype(bf16)
    nak_ref[...] = _group_rms(proj(256, 512), g64, 64, gains_ref[1:2, :]).astype(bf16)
    nav_ref[...] = proj(512, 768).astype(bf16)

    g32 = g32_ref[...]
    cos = cos_ref[...]
    sin = sin_ref[...]
    lane = lax.broadcasted_iota(jnp.int32, (1, 256), 1)
    first = (lane % 16) < 8

    def rope(t):
        swapped = jnp.where(first, pltpu.roll(t, 256 - 8, 1), pltpu.roll(t, 8, 1))
        return t * cos + swapped * sin

    q = rope(_group_rms(proj(768, 1024), g32, 32, gains_ref[2:3, :]))
    k = rope(_group_rms(proj(1024, 1280), g32, 32, gains_ref[3:4, :]))
    daq_ref[...] = (q * (32 ** -0.5 * LOG2E)).astype(bf16)
    dak_ref[...] = k.astype(bf16)
    v = proj(1280, 1536).astype(bf16)
    one_col = (lax.broadcasted_iota(jnp.int32, (v.shape[0], 64), 1) == 0).astype(bf16)
    davx_ref[...] = jnp.concatenate(
        [piece for hd in range(4) for piece in (v[:, hd * 64:(hd + 1) * 64], one_col)], -1)

    gdn_ref[...] = proj(1536, 2304)
    z_ref[...] = proj(2304, 2560).astype(bf16)
    f_ref[...] = proj_b(0, 256).astype(bf16)
    for kk in range(4):
        gate_ref[:, kk * 1024:(kk + 1) * 1024] = jax.nn.sigmoid(
            proj_b(256 + kk * 1024, 256 + (kk + 1) * 1024)).astype(bf16)
    ab_ref[...] = jnp.dot(h, wab_ref[...], preferred_element_type=jnp.float32)


def inproj_call(Xl, Xc, ctx_row0, mod_l, g, w_a, w_b, w_ab, cos, sin, na_qg, na_kg, da_qg, da_kg,
                *, n_batch, seq, ctx_len, tm=256):
    D = Xl.shape[1]
    NT = n_batch * (seq + ctx_len)
    tpb = seq // tm
    n_lat_tiles = n_batch * tpb
    kern = functools.partial(_inproj_kernel, tiles_per_batch=tpb, n_lat_tiles=n_lat_tiles, n_batch=n_batch)
    bf16, f32 = jnp.bfloat16, jnp.float32
    widths = [(256, bf16)] * 5 + [(512, bf16), (768, f32), (256, bf16), (256, bf16), (4096, bf16), (128, f32)]
    gains = jnp.stack([jnp.tile(na_qg, 4), jnp.tile(na_kg, 4), jnp.tile(da_qg, 8), jnp.tile(da_kg, 8)]
                      + [jnp.zeros((256,), f32)] * 4, 0)
    g64, g32 = _group_ones(256, 64), _group_ones(256, 32)
    tab = pl.BlockSpec((tm, 256), lambda i: (jnp.where(i < n_lat_tiles, i % tpb, tpb), 0))
    full = lambda a: pl.BlockSpec(a.shape, lambda i: (0,) * a.ndim)
    return pl.pallas_call(
        kern,
        grid=(NT // tm,),
        in_specs=_stream_specs(n_lat_tiles, ctx_row0 // tm, tm, D) + [
            full(mod_l), pl.BlockSpec((1, D), lambda i: (0, 0)), full(w_a), full(w_b), full(w_ab), tab, tab,
            full(g64), full(g32), full(gains)],
        out_specs=[pl.BlockSpec((tm, w), lambda i: (i, 0)) for w, _ in widths],
        out_shape=[jax.ShapeDtypeStruct((NT, w), dt) for w, dt in widths],
        compiler_params=pltpu.CompilerParams(vmem_limit_bytes=VMEM_LIMIT),
        name="inproj",
    )(Xl, Xc, mod_l, g.reshape(1, D), w_a, w_b, w_ab, cos, sin, g64, g32, gains)


def _group_ones(width, group):
    i = np.arange(width)
    return jnp.asarray((i[:, None] // group == i[None, :] // group).astype(np.float32), jnp.bfloat16)


def _group_sum(xx, gmat):
    hi, lo = _split_bf16(xx)
    return (jnp.dot(hi, gmat, preferred_element_type=jnp.float32)
            + jnp.dot(lo, gmat, preferred_element_type=jnp.float32))


def _group_rms(x, gmat, group, gain):
    return x * lax.rsqrt(_group_sum(x * x, gmat) * (1.0 / group) + RMS_EPS) * gain


def rope_tables(seq, tm):
    nf = 8
    t = jnp.arange(seq)
    rows = (t // GRID_W).astype(jnp.float32)
    cols = (t % GRID_W).astype(jnp.float32)
    freqs = ROPE_BASE ** (-jnp.arange(nf, dtype=jnp.float32) / nf)
    d = np.arange(32)
    f_idx = d % 8
    use_col = (d // 16) == 1
    ang = jnp.where(use_col[None, :], cols[:, None], rows[:, None]) * freqs[f_idx][None, :]
    sign = np.where((d % 16) < 8, -1.0, 1.0).astype(np.float32)
    cos = jnp.tile(jnp.cos(ang), (1, 8))
    sin = jnp.tile(jnp.sin(ang) * sign[None, :], (1, 8))
    cos = jnp.concatenate([cos, jnp.ones((tm, 256), jnp.float32)], 0)
    sin = jnp.concatenate([sin, jnp.zeros((tm, 256), jnp.float32)], 0)
    return cos, sin


NA_RB = 4
NA_UR = 11


def na_bias_tables(rpb, rows):
    reps = [(0, 0), (4, 0), (rows - NA_RB, rows - NA_UR)]
    qc = np.arange(GRID_W)[:, None]
    kc = np.arange(GRID_W)[None, :]
    ws = np.clip(qc - 8, 0, GRID_W - 16)
    vcol = (kc >= ws) & (kc < ws + 16)
    rel_c = np.clip(kc - qc + 15, 0, 30)
    sel_c = (rel_c[..., None] == np.arange(31)).astype(np.float32)
    sel_r, vrow = [], []
    for r0, u0 in reps:
        r = r0 + np.arange(NA_RB)[:, None]
        krow = u0 + np.arange(NA_UR)[None, :]
        start = np.clip(r - 4, 0, rows - 8)
        vrow.append((krow >= start) & (krow < start + 8))
        sel_r.append((np.clip(krow - r + 7, 0, 14)[..., None] == np.arange(15)).astype(np.float32))
    sel_r, vrow = np.stack(sel_r), np.stack(vrow)
    b = jnp.einsum('taui,hij,qkj->thaquk', jnp.asarray(sel_r), rpb.astype(jnp.float32), jnp.asarray(sel_c),
                   precision=HI)
    valid = vrow[:, None, :, None, :, None] & vcol[None, None, None, :, None, :]
    b = jnp.where(valid, b, NEG_INF)
    return b.reshape(3, rpb.shape[0], NA_RB * GRID_W, NA_UR * GRID_W).astype(jnp.bfloat16)


def _na_kernel(q_ref, k_ref, v_ref, kc_ref, vc_ref, bias_ref, o_ref, *, rows):
    i = pl.program_id(1)
    u0 = jnp.clip(i * NA_RB - 4, 0, rows - NA_UR)
    off = pl.multiple_of(u0 * GRID_W, GRID_W)
    nkw = NA_UR * GRID_W
    kwin = k_ref[pl.ds(off, nkw), :]
    vwin = v_ref[pl.ds(off, nkw), :]
    q = q_ref[...]
    kc = kc_ref[...]
    vc = vc_ref[...]
    outs = []
    for h in range(4):
        sl = slice(h * 64, (h + 1) * 64)
        qh = q[:, sl]
        s_loc = lax.dot_general(qh, kwin[:, sl], NT_DIMS, preferred_element_type=jnp.float32) + bias_ref[0, h].astype(jnp.float32)
        s_ctx = lax.dot_general(qh, kc[:, sl], NT_DIMS, preferred_element_type=jnp.float32)
        m = jnp.maximum(jnp.max(s_loc, -1, keepdims=True), jnp.max(s_ctx, -1, keepdims=True))
        p_loc = jnp.exp(s_loc - m)
        p_ctx = jnp.exp(s_ctx - m)
        l = jnp.sum(p_loc, -1, keepdims=True) + jnp.sum(p_ctx, -1, keepdims=True)
        o = (jnp.dot(p_loc.astype(jnp.bfloat16), vwin[:, sl], preferred_element_type=jnp.float32)
             + jnp.dot(p_ctx.astype(jnp.bfloat16), vc[:, sl], preferred_element_type=jnp.float32))
        outs.append(o / l)
    o_ref[...] = jnp.concatenate(outs, -1).astype(o_ref.dtype)


def na_call(NAQ, NAK, NAV, bias, *, n_batch, seq, ctx_len):
    rows = seq // GRID_W
    nblk = rows // NA_RB
    tq = NA_RB * GRID_W
    cb = n_batch * seq // ctx_len
    kern = functools.partial(_na_kernel, rows=rows)
    tsel = lambda b, i: (jnp.where(i == 0, 0, jnp.where(i == nblk - 1, 2, 1)), 0, 0, 0)
    return pl.pallas_call(
        kern,
        grid=(n_batch, nblk),
        in_specs=[pl.BlockSpec((tq, 256), lambda b, i: (b * nblk + i, 0)),
                  pl.BlockSpec((seq, 256), lambda b, i: (b, 0)),
                  pl.BlockSpec((seq, 256), lambda b, i: (b, 0)),
                  pl.BlockSpec((ctx_len, 256), lambda b, i: (cb + b, 0)),
                  pl.BlockSpec((ctx_len, 256), lambda b, i: (cb + b, 0)),
                  pl.BlockSpec((1,) + bias.shape[1:], tsel)],
        out_specs=pl.BlockSpec((tq, 256), lambda b, i: (b * nblk + i, 0)),
        out_shape=jax.ShapeDtypeStruct((n_batch * seq, 256), jnp.bfloat16),
        compiler_params=pltpu.CompilerParams(vmem_limit_bytes=VMEM_LIMIT),
        name="na_attn",
    )(NAQ, NAK, NAV, NAK, NAV, bias)


def _stack_masked_q(q):
    lane = lax.broadcasted_iota(jnp.int32, (1, 256), 1)
    return jnp.concatenate([jnp.where((lane // 32) == hm, q, jnp.zeros_like(q)) for hm in range(8)], 0)


def _da_finish(acc, lam, gain, g64, post_scale):
    tq = acc.shape[0] // 8
    norm = acc / jnp.broadcast_to(acc[:, 64:65], acc.shape)
    lane = lax.broadcasted_iota(jnp.int32, (1, 128), 1)
    halves = []
    for hp in range(2):
        d = [norm[(4 * hp + 2 * j) * tq:(4 * hp + 2 * j + 1) * tq]
             - lam * norm[(4 * hp + 2 * j + 1) * tq:(4 * hp + 2 * j + 2) * tq] for j in range(2)]
        halves.append(jnp.where(lane < 64, d[0], pltpu.roll(d[1], 64, 1)))
    o = jnp.concatenate(halves, -1)
    return _group_rms(o, g64, 64, gain) * post_scale


def _da_tile(qs, k, vx, m, acc, tq):
    tk = k.shape[0]
    s = lax.dot_general(qs, k, NT_DIMS, preferred_element_type=jnp.float32)
    m_new = jnp.maximum(m, jnp.max(s, -1, keepdims=True))
    alpha = jnp.exp2(m - m_new)
    p = jnp.exp2(s - pltpu.repeat(m_new, tk // 128, axis=1)).astype(jnp.bfloat16)
    pv = jnp.concatenate(
        [jnp.dot(p[2 * h * tq:(2 * h + 2) * tq], vx[:, h * 128:(h + 1) * 128], preferred_element_type=jnp.float32)
         for h in range(4)], 0)
    return m_new, alpha * acc + pv


def _da_kernel(lam_ref, q_ref, k_ref, vx_ref, kc_ref, vxc_ref, gain_ref, g64_ref, o_ref, m_ref, acc_ref,
               *, tk, n_kv, post_scale):
    tq = q_ref.shape[0]
    qs = _stack_masked_q(q_ref[...])
    m_ref[...] = jnp.full(m_ref.shape, -jnp.inf, jnp.float32)
    acc_ref[...] = jnp.zeros(acc_ref.shape, jnp.float32)

    def body(j, carry):
        for u in range(2):
            off = pl.multiple_of((2 * j + u) * tk, tk)
            m, acc = _da_tile(qs, k_ref[pl.ds(off, tk), :], vx_ref[pl.ds(off, tk), :], m_ref[...], acc_ref[...], tq)
            m_ref[...] = m
            acc_ref[...] = acc
        return carry

    lax.fori_loop(0, n_kv // 2 - 1, body, 0, unroll=True)
    off = (n_kv - 2) * tk
    m, acc = _da_tile(qs, k_ref[off:off + tk, :], vx_ref[off:off + tk, :], m_ref[...], acc_ref[...], tq)
    k_last = jnp.concatenate([k_ref[off + tk:off + 2 * tk, :], kc_ref[...]], 0)
    vx_last = jnp.concatenate([vx_ref[off + tk:off + 2 * tk, :], vxc_ref[...]], 0)
    m, acc = _da_tile(qs, k_last, vx_last, m, acc, tq)
    o_ref[...] = _da_finish(acc, lam_ref[0], gain_ref[...], g64_ref[...], post_scale).astype(o_ref.dtype)


def da_call(lam, DAQ, DAK, DAVX, subln_g, *, n_batch, seq, ctx_len, lam_init, tq=256, tk=1024):
    tk = min(tk, seq // 2)
    nq = seq // tq
    cb = n_batch * seq // ctx_len
    g64 = _group_ones(256, 64)
    gain = jnp.tile(subln_g, 4).reshape(1, 256)
    n_kv = seq // tk
    assert n_kv % 2 == 0
    kern = functools.partial(_da_kernel, tk=tk, n_kv=n_kv, post_scale=1.0 - lam_init)
    return pl.pallas_call(
        kern,
        grid=(n_batch, nq),
        in_specs=[pl.BlockSpec(memory_space=pltpu.SMEM),
                  pl.BlockSpec((tq, 256), lambda b, i: (b * nq + i, 0)),
                  pl.BlockSpec((seq, 256), lambda b, i: (b, 0)),
                  pl.BlockSpec((seq, 512), lambda b, i: (b, 0)),
                  pl.BlockSpec((ctx_len, 256), lambda b, i: (cb + b, 0)),
                  pl.BlockSpec((ctx_len, 512), lambda b, i: (cb + b, 0)),
                  pl.BlockSpec((1, 256), lambda b, i: (0, 0)),
                  pl.BlockSpec((256, 256), lambda b, i: (0, 0))],
        out_specs=pl.BlockSpec((tq, 256), lambda b, i: (b * nq + i, 0)),
        out_shape=jax.ShapeDtypeStruct((n_batch * seq, 256), jnp.bfloat16),
        scratch_shapes=[pltpu.VMEM((8 * tq, 128), jnp.float32), pltpu.VMEM((8 * tq, 128), jnp.float32)],
        compiler_params=pltpu.CompilerParams(vmem_limit_bytes=VMEM_LIMIT),
        name="da_attn",
    )(lam.reshape(1), DAQ, DAK, DAVX, DAK, DAVX, gain, g64)


def _ctx_kernel(lam_ref, naq_ref, nak_ref, nav_ref, daq_ref, dak_ref, dav_ref, gain_ref, g64_ref, ya_ref, yd_ref,
                *, post_scale):
    q = naq_ref[...]
    k = nak_ref[...]
    v = nav_ref[...]
    outs = []
    for h in range(4):
        sl = slice(h * 64, (h + 1) * 64)
        s = lax.dot_general(q[:, sl], k[:, sl], NT_DIMS, preferred_element_type=jnp.float32)
        p = jnp.exp(s - jnp.max(s, -1, keepdims=True))
        o = jnp.dot(p.astype(jnp.bfloat16), v[:, sl], preferred_element_type=jnp.float32)
        outs.append(o / jnp.sum(p, -1, keepdims=True))
    ya_ref[...] = jnp.concatenate(outs, -1).astype(ya_ref.dtype)
    tq = daq_ref.shape[0]
    qs = _stack_masked_q(daq_ref[...])
    m0 = jnp.full((8 * tq, 128), -jnp.inf, jnp.float32)
    m, acc = _da_tile(qs, dak_ref[...], dav_ref[...], m0, jnp.zeros((8 * tq, 128), jnp.float32), tq)
    yd_ref[...] = _da_finish(acc, lam_ref[0], gain_ref[...], g64_ref[...], post_scale).astype(yd_ref.dtype)


def ctx_attn_call(lam, NAQ, NAK, NAV, DAQ, DAK, DAVX, subln_g, *, n_batch, seq, ctx_len, lam_init):
    cb = n_batch * seq // ctx_len
    g64 = _group_ones(256, 64)
    gain = jnp.tile(subln_g, 4).reshape(1, 256)
    blk = lambda col: pl.BlockSpec((ctx_len, 256), lambda b: (cb + b, col))
    return pl.pallas_call(
        functools.partial(_ctx_kernel, post_scale=1.0 - lam_init),
        grid=(n_batch,),
        in_specs=[pl.BlockSpec(memory_space=pltpu.SMEM), blk(0), blk(0), blk(0), blk(0), blk(0),
                  pl.BlockSpec((ctx_len, 512), lambda b: (cb + b, 0)),
                  pl.BlockSpec((1, 256), lambda b: (0, 0)), pl.BlockSpec((256, 256), lambda b: (0, 0))],
        out_specs=[pl.BlockSpec((ctx_len, 256), lambda b: (b, 0))] * 2,
        out_shape=[jax.ShapeDtypeStruct((n_batch * ctx_len, 256), jnp.bfloat16)] * 2,
        name="ctx_attn",
    )(lam.reshape(1), NAQ, NAK, NAV, DAQ, DAK, DAVX, gain, g64)


def _split_bf16(a):
    hi = a.astype(jnp.bfloat16)
    return hi, (a - hi.astype(jnp.float32)).astype(jnp.bfloat16)


def _route(logits_t, bias_col):
    per_group = N_EXPERTS // N_GROUPS
    scores = jax.nn.sigmoid(logits_t)
    sel_all = scores + bias_col
    sel = [sel_all[e:e + 1, :] for e in range(N_EXPERTS)]
    top2 = []
    for e in range(N_EXPERTS):
        g0 = e // per_group * per_group
        rank = jnp.zeros_like(sel[e])
        for o in range(g0, g0 + per_group):
            if o != e:
                beats = (sel[o] > sel[e]) | (sel[o] == sel[e]) if o < e else (sel[o] > sel[e])
                rank = rank + beats.astype(jnp.float32)
        top2.append(rank < 1.5)
    gs = []
    for g in range(N_GROUPS):
        acc = jnp.zeros_like(sel[0])
        for e in range(g * per_group, (g + 1) * per_group):
            acc = acc + jnp.where(top2[e], sel[e], 0.0)
        gs.append(acc)
    rows = []
    for g in range(N_GROUPS):
        beaten = jnp.zeros_like(sel[0])
        for o in range(N_GROUPS):
            if o != g:
                b = (gs[o] >= gs[g]) if o < g else (gs[o] > gs[g])
                beaten = beaten + b.astype(jnp.float32)
        best = beaten < 0.5
        for e in range(g * per_group, (g + 1) * per_group):
            rows.append(jnp.where(best & top2[e], scores[e:e + 1, :], 0.0))
    w = jnp.concatenate(rows, 0)
    return w / jnp.sum(w, axis=0, keepdims=True)


def _merge_kernel(xl_ref, xc_ref, yal_ref, yac_ref, ydl_ref, ydc_ref, yfl_ref, yfc_ref, of_ref, ob_ref, z_ref, gate_ref,
                  m_ref, g2_ref, og_ref, g64_ref, wb_ref, wo_ref, wr_ref, rb_ref, xo_ref, h2_ref, rw_ref, *, tiles_per_batch, n_lat_tiles, n_batch):
    i = pl.program_id(0)
    r = jnp.where(i < n_lat_tiles, i // tiles_per_batch, n_batch)
    z = z_ref[...].astype(jnp.float32)
    yg = (_group_rms(of_ref[...] + ob_ref[...], g64_ref[...], GDN_DIM, og_ref[...]) * (z * jax.nn.sigmoid(z)))
    is_lat = i < n_lat_tiles
    pick = lambda lat_ref, ctx_ref: jnp.where(is_lat, lat_ref[...], ctx_ref[...])
    branches = (pick(yal_ref, yac_ref), pick(ydl_ref, ydc_ref), yg.astype(jnp.bfloat16),
                pick(yfl_ref, yfc_ref).astype(jnp.bfloat16))
    mix = None
    for b, yb in enumerate(branches):
        proj = jnp.dot(yb, wb_ref[b], preferred_element_type=jnp.float32)
        term = gate_ref[:, b * D_MODEL:(b + 1) * D_MODEL].astype(jnp.float32) * proj
        mix = term if mix is None else mix + term
    y = jnp.dot(mix.astype(jnp.bfloat16), wo_ref[...], preferred_element_type=jnp.float32)
    mrow = lambda k: m_ref[pl.ds(r, 1), k * D_MODEL:(k + 1) * D_MODEL]
    x = jnp.where(i < n_lat_tiles, xl_ref[...], xc_ref[...]) + mrow(2) * y
    xo_ref[...] = x
    h2 = (x * lax.rsqrt(jnp.mean(x * x, axis=-1, keepdims=True) + RMS_EPS) * g2_ref[...]) * (1.0 + mrow(4)) + mrow(3)
    h2_ref[...] = h2.astype(h2_ref.dtype)
    h_hi, h_lo = _split_bf16(h2)
    w_hi, w_lo = _split_bf16(wr_ref[...])
    nt = lambda a, b: lax.dot_general(a, b, NT_DIMS, preferred_element_type=jnp.float32)
    logits_t = nt(w_hi, h_hi) + nt(w_hi, h_lo) + nt(w_lo, h_hi)
    w_t = _route(logits_t, rb_ref[...])
    pad = jnp.zeros((128 - N_EXPERTS, w_t.shape[1]), jnp.float32)
    rw_ref[...] = jnp.transpose(jnp.concatenate([w_t, pad], 0))


def merge_call(Xl, Xc, ctx_row0, Ya, Yd, Yf, Of, Ob, P_z, P_gate, mod_l, g2n, onorm_g, w_branch, w_out, w_router,
               router_bias, *, n_tok, n_batch, seq, tm=256):
    D = Xl.shape[1]
    n_lat_tiles = n_batch * seq // tm
    kern = functools.partial(_merge_kernel, tiles_per_batch=seq // tm, n_lat_tiles=n_lat_tiles, n_batch=n_batch)
    row = lambda w: pl.BlockSpec((tm, w), lambda i: (i, 0))
    full = lambda a: pl.BlockSpec(a.shape, lambda i: (0,) * a.ndim)
    wr_t = jnp.transpose(w_router)
    rb = router_bias.reshape(N_EXPERTS, 1)
    g2 = g2n.reshape(1, D)
    og = jnp.tile(onorm_g, GDN_HEADS).reshape(1, BRANCH_W)
    g64 = _group_ones(BRANCH_W, GDN_DIM)
    return pl.pallas_call(
        kern,
        grid=(n_tok // tm,),
        in_specs=_stream_specs(n_lat_tiles, ctx_row0 // tm, tm, D) + 3 * _stream_specs(n_lat_tiles, 0, tm, 256) + [
                  row(256), row(256), row(256), row(4 * D), full(mod_l),
                  full(g2), full(og), full(g64), full(w_branch), full(w_out), full(wr_t), full(rb)],
        out_specs=[row(D), row(D), row(128)],
        out_shape=[jax.ShapeDtypeStruct((n_tok, D), jnp.float32), jax.ShapeDtypeStruct((n_tok, D), jnp.bfloat16),
                   jax.ShapeDtypeStruct((n_tok, 128), jnp.float32)],
        compiler_params=pltpu.CompilerParams(vmem_limit_bytes=VMEM_LIMIT),
        name="merge",
    )(Xl, Xc, *Ya, *Yd, *Yf, Of, Ob, P_z, P_gate, mod_l, g2, og, g64, w_branch, w_out, wr_t, rb)


MOE_TILE = 1024
MOE_WIN = 320
MOE_WIN_CTX = 176
MOE_EPS = 4


def _moe_sorted_kernel(x_ref, h_ref, rw_ref, m_ref, wg_ref, wu_ref, wd_ref, o_ref, xs_ref, ws_ref, ys_ref, dest_ref,
                       seg_ref, *, win, mod_row_fn):
    T = h_ref.shape[0]
    per_group = N_EXPERTS // N_GROUPS
    i = pl.program_id(0)
    step = pl.program_id(1)
    lane = lax.broadcasted_iota(jnp.int32, (1, 128), 1)
    f32, bf16 = jnp.float32, jnp.bfloat16

    @pl.when(step == 0)
    def _():
        rw = rw_ref[...]
        r128 = lax.broadcasted_iota(jnp.int32, (128, 128), 0)
        c128 = lax.broadcasted_iota(jnp.int32, (128, 128), 1)
        e2g = ((r128 // per_group == c128) & (r128 < N_EXPERTS)).astype(bf16)
        og = (jnp.dot((rw > 0.0).astype(bf16), e2g, preferred_element_type=f32) > 0.5).astype(f32)
        rt = lax.broadcasted_iota(jnp.int32, (T, T), 0)
        ct = lax.broadcasted_iota(jnp.int32, (T, T), 1)
        earlier = jnp.dot((rt > ct).astype(bf16), og.astype(bf16), preferred_element_type=f32)
        cnt_row = jnp.sum(og, axis=0, keepdims=True)
        start_row = jnp.dot(jnp.broadcast_to(cnt_row, (8, 128)), (r128 < c128).astype(f32),
                            preferred_element_type=f32, precision=HI)[0:1]
        dest = jnp.sum(og * (earlier + start_row), axis=-1, keepdims=True)
        dest_ref[...] = jnp.broadcast_to(dest, (T, 128))
        dest_row = jnp.transpose(dest_ref[...])[0:1, :]
        perm = (dest_row == rt.astype(f32)).astype(bf16)
        xs_ref[...] = jnp.dot(perm, h_ref[...], preferred_element_type=f32).astype(bf16)
        ws_ref[...] = sum(jnp.dot(perm, piece, preferred_element_type=f32) for piece in _split_bf16(rw))
        ys_ref[...] = jnp.zeros(ys_ref.shape, f32)
        for g in range(N_GROUPS):
            seg_ref[g] = jnp.sum(jnp.where(lane == g, start_row, 0.0)).astype(jnp.int32)
            seg_ref[N_GROUPS + g] = jnp.sum(jnp.where(lane == g, cnt_row, 0.0)).astype(jnp.int32)

    g = (step * MOE_EPS) // per_group
    start = seg_ref[g]
    end = start + seg_ref[N_GROUPS + g]
    s0 = (start // 16) * 16
    n_win = (end - s0 + win - 1) // win

    def window(w, carry):
        lo = s0 + w * win
        off = pl.multiple_of(jnp.minimum(lo, T - win), 16)
        xw = xs_ref[pl.ds(off, win), :]
        ww = ws_ref[pl.ds(off, win), :]
        row = off + lax.broadcasted_iota(jnp.int32, (win, 1), 0)
        y = None
        for j in range(MOE_EPS):
            w_col = jnp.sum(jnp.where(lane == step * MOE_EPS + j, ww, 0.0), axis=-1, keepdims=True)
            w_col = jnp.where(row >= lo, w_col, 0.0)
            a = jnp.dot(xw, wg_ref[0, j], preferred_element_type=f32)
            u = jnp.dot(xw, wu_ref[0, j], preferred_element_type=f32)
            act = (a * jax.nn.sigmoid(a) * u * w_col).astype(bf16)
            yj = jnp.dot(act, wd_ref[0, j], preferred_element_type=f32)
            y = yj if y is None else y + yj
        ys_ref[pl.ds(off, win), :] += y
        return carry

    lax.fori_loop(0, n_win, window, 0)

    @pl.when(step == N_EXPERTS // MOE_EPS - 1)
    def _():
        ct = lax.broadcasted_iota(jnp.int32, (T, T), 1)
        unperm = (pltpu.repeat(dest_ref[...], T // 128, axis=1) == ct.astype(f32)).astype(bf16)
        f = jnp.dot(unperm, ys_ref[...].astype(bf16), preferred_element_type=f32)
        g2 = m_ref[pl.ds(mod_row_fn(i), 1), 5 * D_MODEL:6 * D_MODEL]
        o_ref[...] = x_ref[...] + g2 * f


def moe_sorted_call(Xmid, H2, RW, mod_l, w_gate, w_up, w_down, *, layer, row0, n_rows, tile, win, mod_row_fn):
    D = Xmid.shape[1]
    F = w_gate.shape[-1]
    b0 = row0 // tile
    tok = lambda w: pl.BlockSpec((tile, w), lambda i, e: (b0 + i, 0))
    kern = functools.partial(_moe_sorted_kernel, win=win, mod_row_fn=mod_row_fn)
    return pl.pallas_call(
        kern,
        grid=(n_rows // tile, N_EXPERTS // MOE_EPS),
        in_specs=[tok(D), tok(D), tok(128),
                  pl.BlockSpec(mod_l.shape, lambda i, e: (0, 0)),
                  pl.BlockSpec((1, MOE_EPS, D, F), lambda i, e: (layer, e, 0, 0)),
                  pl.BlockSpec((1, MOE_EPS, D, F), lambda i, e: (layer, e, 0, 0)),
                  pl.BlockSpec((1, MOE_EPS, F, D), lambda i, e: (layer, e, 0, 0))],
        out_specs=pl.BlockSpec((tile, D), lambda i, e: (i, 0)),
        out_shape=jax.ShapeDtypeStruct((n_rows, D), jnp.float32),
        scratch_shapes=[pltpu.VMEM((tile, D), jnp.bfloat16), pltpu.VMEM((tile, 128), jnp.float32),
                        pltpu.VMEM((tile, D), jnp.float32), pltpu.VMEM((tile, 128), jnp.float32),
                        pltpu.SMEM((2 * N_GROUPS,), jnp.int32)],
        compiler_params=pltpu.CompilerParams(vmem_limit_bytes=VMEM_LIMIT),
        name="moe_sorted",
    )(Xmid, H2, RW, mod_l, w_gate, w_up, w_down)


GDN_TILE = 256
GDN_CPT = GDN_TILE // GDN_CHUNK
GDN_LOCKSTEP = 2


def _mm(a, b, passes=1, dims=None):
    if dims is None:
        dot = lambda x, y: jnp.dot(x, y, preferred_element_type=jnp.float32)
    else:
        dot = lambda x, y: lax.dot_general(x, y, dims, preferred_element_type=jnp.float32)
    if passes == 1:
        return dot(a.astype(jnp.bfloat16), b.astype(jnp.bfloat16))
    a_hi, a_lo = _split_bf16(a)
    b_hi, b_lo = _split_bf16(b)
    return dot(a_hi, b_hi) + dot(a_hi, b_lo) + dot(a_lo, b_hi)


def _stack_heads(x):
    lane = lax.broadcasted_iota(jnp.int32, (1, 256), 1)
    return jnp.concatenate([jnp.where((lane // GDN_DIM) == h, x, 0.0) for h in range(GDN_HEADS)], 0)


def _slab(x):
    return x[0:64] + x[64:128] + x[128:192] + x[192:256]


def _unit_tri_inverse(mats):
    shape = mats[0].shape
    eye = (lax.broadcasted_iota(jnp.int32, shape, 0) == lax.broadcasted_iota(jnp.int32, shape, 1)).astype(jnp.float32)
    ps = [eye - a for a in mats]
    pws = list(mats)
    for _ in range(5):
        pws = [_mm(pw, pw, 3) for pw in pws]
        ps = [p + _mm(p, pw, 3) for p, pw in zip(ps, pws)]
    return ps


def _gdn_chunk_kernel(x_ref, xp_ref, xn_ref, ab_ref, cw_ref, par_ref, g64_ref, qe_ref, ou_ref, mm_ref, nn_ref,
                      *, tiles_per_batch, n_lat_tiles):
    i = pl.program_id(0)
    is_lat = i < n_lat_tiles
    first = jnp.where(is_lat, (i % tiles_per_batch) == 0, True)
    last = jnp.where(is_lat, (i % tiles_per_batch) == tiles_per_batch - 1, True)
    xp = jnp.concatenate([jnp.where(first, 0.0, xp_ref[...]), x_ref[...], jnp.where(last, 0.0, xn_ref[...])], 0)
    y = xp[6:6 + GDN_TILE] * cw_ref[0:1, :]
    for t in range(1, 5):
        y = y + xp[6 + t:6 + t + GDN_TILE] * cw_ref[t:t + 1, :]
    y = y * jax.nn.sigmoid(y)
    g64 = g64_ref[...]
    q = y[:, 0:256]
    k = y[:, 256:512]
    v = y[:, 512:768]
    q = q * lax.rsqrt(_group_sum(q * q, g64) + RMS_EPS) * (GDN_DIM ** -0.5)
    k = k * lax.rsqrt(_group_sum(k * k, g64) + RMS_EPS)
    ab = ab_ref[...]
    lane128 = lax.broadcasted_iota(jnp.int32, (1, 128), 1)
    g_all = jnp.where(lane128 < 8, par_ref[0:1, :] * jax.nn.softplus(ab + par_ref[1:2, :]), 0.0)
    beta_all = jax.nn.sigmoid(ab)
    r64 = lax.broadcasted_iota(jnp.int32, (64, 64), 0)
    c64 = lax.broadcasted_iota(jnp.int32, (64, 64), 1)
    low = (r64 >= c64).astype(jnp.float32)
    upp = (r64 <= c64).astype(jnp.float32)
    rr = lax.broadcasted_iota(jnp.int32, (256, 256), 0)
    cc = lax.broadcasted_iota(jnp.int32, (256, 256), 1)
    same = (rr // 64) == (cc // 64)
    eye = (rr == cc).astype(jnp.float32)
    for c0 in range(0, GDN_CPT, GDN_LOCKSTEP):
        stage = []
        for c in range(c0, c0 + GDN_LOCKSTEP):
            sl = slice(c * GDN_CHUNK, (c + 1) * GDN_CHUNK)
            g_c = g_all[sl]
            gsum = jnp.where(lane128 < 4, jnp.dot(low, g_c, preferred_element_type=jnp.float32, precision=HI),
                             jnp.dot(upp, g_c, preferred_element_type=jnp.float32, precision=HI))
            gsum_t = jnp.transpose(jnp.concatenate([gsum, jnp.zeros_like(gsum)], 0))[:, 0:64]
            tot = jnp.sum(g_c, axis=0, keepdims=True)
            beta_c = beta_all[sl]
            q_sm = _stack_heads(q[sl])
            k_sm = _stack_heads(k[sl])
            v_sm = _stack_heads(v[sl])
            kk = _mm(k_sm, k_sm, 1, NT_DIMS)
            qk = _mm(q_sm, k_sm, 1, NT_DIMS)
            for d in range(2):
                col = lambda h: 4 * d + h
                cmat = jnp.concatenate([jnp.broadcast_to(gsum[:, col(h):col(h) + 1], (64, 256)) for h in range(4)], 0)
                rrow = jnp.concatenate([gsum_t[col(h):col(h) + 1, :] for h in range(4)], 1)
                bmat = jnp.concatenate([jnp.broadcast_to(beta_c[:, 8 + col(h):9 + col(h)], (64, 256))
                                        for h in range(4)], 0)
                tmat = jnp.concatenate([jnp.broadcast_to(tot[:, col(h):col(h) + 1], (64, 256)) for h in range(4)], 0)
                tri = (rr % 64 >= cc % 64) if d == 0 else (rr % 64 <= cc % 64)
                incl = same & tri
                strict = incl & (rr != cc)
                dec = jnp.where(incl, jnp.exp(jnp.where(incl, cmat - rrow, 0.0)), 0.0)
                stage.append((d, sl, q_sm, k_sm, v_sm, qk, cmat, bmat, tmat, dec,
                              jnp.where(strict, kk * bmat * dec, 0.0)))
        t_invs = _unit_tri_inverse([st[-1] for st in stage])
        n = len(stage)
        q_sms, k_sms, v_sms, qks, cmats, bmats, tmats, decs = (
            [st[f] for st in stage] for f in range(2, 10))
        egs = [jnp.exp(cm) for cm in cmats]
        us = [_mm(t_invs[j], v_sms[j] * bmats[j], 1) for j in range(n)]
        ws = [_mm(t_invs[j], k_sms[j] * bmats[j] * egs[j], 1) for j in range(n)]
        qkms = [decs[j] * qks[j] for j in range(n)]
        qes = [q_sms[j] * egs[j] - _mm(qkms[j], ws[j], 1) for j in range(n)]
        ous = [_mm(qkms[j], us[j], 1) for j in range(n)]
        kd_ts = [jnp.transpose(k_sms[j] * jnp.exp(tmats[j] - cmats[j])) for j in range(n)]
        m_bds = [eye * jnp.exp(tmats[j]) - _mm(kd_ts[j], ws[j], 1) for j in range(n)]
        n_bds = [_mm(kd_ts[j], us[j], 1) for j in range(n)]
        for j in range(n):
            d, sl = stage[j][0], stage[j][1]
            qe_ref[d, sl, :] = _slab(qes[j])
            ou_ref[d, sl, :] = _slab(ous[j])
            mm_ref[d, sl, :] = _slab(m_bds[j])
            nn_ref[d, sl, :] = _slab(n_bds[j])


def gdn_chunk_call(P_gdn, P_ab, conv_w, a_log, dt_bias, *, n_batch, seq):
    NT = P_gdn.shape[0]
    tpb = seq // GDN_TILE
    n_lat = n_batch * tpb
    n_tiles = NT // GDN_TILE
    hb = GDN_TILE // 8
    cw = jnp.zeros((8, 768), jnp.float32).at[:5].set(conv_w)
    par = jnp.zeros((8, 128), jnp.float32)
    par = par.at[0, :8].set(-jnp.exp(a_log.reshape(8))).at[1, :8].set(dt_bias.reshape(8))
    g64 = _group_ones(256, 64)
    kern = functools.partial(_gdn_chunk_kernel, tiles_per_batch=tpb, n_lat_tiles=n_lat)
    full = lambda a: pl.BlockSpec(a.shape, lambda i: (0,) * a.ndim)
    out = pl.BlockSpec((2, GDN_TILE, 256), lambda i: (0, i, 0))
    return pl.pallas_call(
        kern,
        grid=(n_tiles,),
        in_specs=[pl.BlockSpec((GDN_TILE, 768), lambda i: (i, 0)),
                  pl.BlockSpec((8, 768), lambda i: (jnp.maximum(i * hb - 1, 0), 0)),
                  pl.BlockSpec((8, 768), lambda i: (jnp.minimum((i + 1) * hb, n_tiles * hb - 1), 0)),
                  pl.BlockSpec((GDN_TILE, 128), lambda i: (i, 0)),
                  full(cw), full(par), full(g64)],
        out_specs=[out] * 4,
        out_shape=[jax.ShapeDtypeStruct((2, NT, 256), jnp.float32)] * 4,
        compiler_params=pltpu.CompilerParams(vmem_limit_bytes=VMEM_LIMIT),
        name="gdn_chunk",
    )(P_gdn, P_gdn, P_gdn, P_ab, cw, par, g64)


def _gdn_scan_kernel(qef_ref, ouf_ref, mf_ref, nf_ref, qeb_ref, oub_ref, mb_ref, nb_ref, of_ref, ob_ref, s_ref):
    t = pl.program_id(1)

    @pl.when(t == 0)
    def _():
        s_ref[...] = jnp.zeros(s_ref.shape, jnp.float32)

    def step(d, c, qe_ref, ou_ref, m_ref, n_ref, o_ref):
        sl = slice(c * GDN_CHUNK, (c + 1) * GDN_CHUNK)
        s = s_ref[d]
        o_ref[sl, :] = _slab(_mm(_stack_heads(qe_ref[0, sl, :]), s, 1)) + ou_ref[0, sl, :]
        s_ref[d] = _mm(_stack_heads(m_ref[0, sl, :]), s, 3) + _stack_heads(n_ref[0, sl, :])

    for c in range(GDN_CPT):
        step(0, c, qef_ref, ouf_ref, mf_ref, nf_ref, of_ref)
        step(1, GDN_CPT - 1 - c, qeb_ref, oub_ref, mb_ref, nb_ref, ob_ref)


def gdn_scan_call(QE, OU, MM, NN, *, n_batch, seq):
    NT = QE.shape[1]
    tpb = seq // GDN_TILE
    cblk = n_batch * tpb
    fwd = lambda b, t: jnp.where(t == 0, cblk + b, b * tpb + t - 1)
    bwd = lambda b, t: jnp.where(t == 0, cblk + b, b * tpb + tpb - t)
    spec = lambda d, f: pl.BlockSpec((1, GDN_TILE, 256), lambda b, t: (d, f(b, t), 0))
    ospec = lambda f: pl.BlockSpec((GDN_TILE, 256), lambda b, t: (f(b, t), 0))
    return pl.pallas_call(
        _gdn_scan_kernel,
        grid=(n_batch, tpb + 1),
        in_specs=[spec(0, fwd)] * 4 + [spec(1, bwd)] * 4,
        out_specs=[ospec(fwd), ospec(bwd)],
        out_shape=[jax.ShapeDtypeStruct((NT, 256), jnp.float32)] * 2,
        scratch_shapes=[pltpu.VMEM((2, 256, 256), jnp.float32)],
        compiler_params=pltpu.CompilerParams(vmem_limit_bytes=VMEM_LIMIT),
        name="gdn_scan",
    )(QE, OU, MM, NN, QE, OU, MM, NN)


def _dft_cs(n):
    a = 2.0 * np.pi * np.outer(np.arange(n), np.arange(n)) / n
    return np.cos(a), np.sin(a)


def _channel_dft():
    c, s = _dft_cs(FNET_GROUP_W)
    eye = np.eye(FNET_GROUPS)
    return np.concatenate([np.kron(eye, c), np.kron(eye, s)], 1)


FNET_SUB = 16


def _fnet_a_kernel(x_ref, cs_ref, m_ref, cphi_ref, sphi_ref, br_ref, bi_ref, *, n_rows):
    for j in range(FNET_SUB):
        sl = slice(j * BRANCH_W, (j + 1) * BRANCH_W)
        u = jnp.dot(x_ref[:, j, :], cs_ref[...], preferred_element_type=jnp.float32)
        st = jnp.concatenate([u[:, :BRANCH_W], u[:, BRANCH_W:]], 0).astype(jnp.bfloat16)
        a = jnp.dot(m_ref[...], st, preferred_element_type=jnp.float32)
        ar, ai = a[:n_rows], a[n_rows:]
        cp, sp = cphi_ref[:, sl], sphi_ref[:, sl]
        br_ref[j] = ar * cp - ai * sp
        bi_ref[j] = ar * sp + ai * cp


def _fnet_c_kernel(br_ref, bi_ref, m_ref, o_ref):
    for j in range(FNET_SUB):
        st = jnp.concatenate([br_ref[:, j, :], bi_ref[:, j, :]], 0).astype(jnp.bfloat16)
        o_ref[:, j, :] = jnp.dot(m_ref[...], st, preferred_element_type=jnp.float32)


def fourier_latent_call(P_f, *, n_batch, seq):
    rows = seq // GRID_W
    bf16, f32 = jnp.bfloat16, jnp.float32
    c1, s1 = _dft_cs(rows)
    norm = 1.0 / math.sqrt(seq * FNET_GROUP_W)
    m_a = jnp.asarray(np.block([[c1, -s1], [s1, c1]]) * norm, bf16)
    c2, s2 = _dft_cs(GRID_W)
    m_c = jnp.asarray(np.concatenate([c2, -s2], 1), bf16)
    cs = jnp.asarray(_channel_dft(), bf16)
    phi = 2.0 * np.pi * np.outer(np.arange(rows), np.arange(GRID_W)) / seq
    cphi = jnp.repeat(jnp.asarray(np.cos(phi), f32), BRANCH_W, axis=1)
    sphi = jnp.repeat(jnp.asarray(np.sin(phi), f32), BRANCH_W, axis=1)
    xv = P_f.reshape(P_f.shape[0] // GRID_W, GRID_W, BRANCH_W)
    nj = GRID_W // FNET_SUB
    full = lambda a: pl.BlockSpec(a.shape, lambda b, j: (0,) * a.ndim)
    tw = pl.BlockSpec((rows, FNET_SUB * BRANCH_W), lambda b, j: (0, j))
    mid = jax.ShapeDtypeStruct((n_batch * GRID_W, rows, BRANCH_W), f32)
    br, bi = pl.pallas_call(
        functools.partial(_fnet_a_kernel, n_rows=rows),
        grid=(n_batch, nj),
        in_specs=[pl.BlockSpec((rows, FNET_SUB, BRANCH_W), lambda b, j: (b, j, 0)), full(cs), full(m_a), tw, tw],
        out_specs=[pl.BlockSpec((FNET_SUB, rows, BRANCH_W), lambda b, j: (b * nj + j, 0, 0))] * 2,
        out_shape=[mid, mid],
        name="fnet_rows",
    )(xv, cs, m_a, cphi, sphi)
    blk = pl.BlockSpec((GRID_W, FNET_SUB, BRANCH_W), lambda b, i: (b, i, 0))
    y = pl.pallas_call(
        _fnet_c_kernel,
        grid=(n_batch, rows // FNET_SUB),
        in_specs=[blk, blk, pl.BlockSpec(m_c.shape, lambda b, i: (0, 0))],
        out_specs=blk,
        out_shape=mid,
        name="fnet_cols",
    )(br, bi, m_c)
    return y.reshape(n_batch * seq, BRANCH_W)


def _fnet_ctx_kernel(x_ref, cs_ref, m_ref, o_ref):
    u = jnp.dot(x_ref[...], cs_ref[...], preferred_element_type=jnp.float32)
    st = jnp.concatenate([u[:, :BRANCH_W], u[:, BRANCH_W:]], 0).astype(jnp.bfloat16)
    o_ref[...] = jnp.dot(m_ref[...], st, preferred_element_type=jnp.float32).astype(o_ref.dtype)


def fourier_ctx_call(P_f, *, n_batch, seq, ctx_len):
    bf16 = jnp.bfloat16
    c, s = _dft_cs(ctx_len)
    m = jnp.asarray(np.concatenate([c, -s], 1) / math.sqrt(ctx_len * FNET_GROUP_W), bf16)
    cs = jnp.asarray(_channel_dft(), bf16)
    cb = n_batch * seq // ctx_len
    return pl.pallas_call(
        _fnet_ctx_kernel,
        grid=(n_batch,),
        in_specs=[pl.BlockSpec((ctx_len, BRANCH_W), lambda b: (cb + b, 0)),
                  pl.BlockSpec(cs.shape, lambda b: (0, 0)), pl.BlockSpec(m.shape, lambda b: (0, 0))],
        out_specs=pl.BlockSpec((ctx_len, BRANCH_W), lambda b: (b, 0)),
        out_shape=jax.ShapeDtypeStruct((n_batch * ctx_len, BRANCH_W), jnp.float32),
        name="fnet_ctx",
    )(P_f, cs, m)


def kernel(x, c, ctx, c_ctx, w_mod, b_mod, norm1_g, norm2_g, w_in, na_qn_g, na_kn_g, na_rpb,
           da_qn_g, da_kn_g, da_lam_q1, da_lam_k1, da_lam_q2, da_lam_k2, da_subln_g, gdn_conv_w,
           gdn_a_log, gdn_dt_bias, gdn_onorm_g, w_branch, w_out, w_router, router_bias,
           moe_w_gate, moe_w_up, moe_w_down):
    B, S, D = x.shape
    L = ctx.shape[1]
    NL = B * S
    f32 = jnp.float32
    cvec = jnp.zeros((8, D), f32).at[:B].set(c).at[B].set(c_ctx)
    mods = mod_call(cvec, w_mod, b_mod)
    cos, sin = rope_tables(S, 256)
    bf16 = jnp.bfloat16
    Xl, Xc = x.reshape(NL, D), ctx.reshape(B * L, D)
    experts = (moe_w_gate.astype(bf16), moe_w_up.astype(bf16), moe_w_down.astype(bf16))
    for l in range(DEPTH):
        last = l == DEPTH - 1
        lam_init = 0.8 - 0.6 * math.exp(-0.3 * l)
        w_a = w_in[l, :, :2560].astype(bf16)
        w_b = w_in[l, :, 2576:].astype(bf16)
        w_ab = jnp.pad(w_in[l, :, 2560:2576], ((0, 0), (0, 112))).astype(bf16)
        NAQ, NAK, NAV, DAQ, DAK, DAVX, P_gdn, P_z, P_f, P_gate, P_ab = inproj_call(
            Xl, Xc, 0, mods[l], norm1_g[l], w_a, w_b, w_ab, cos, sin, na_qn_g[l], na_kn_g[l], da_qn_g[l], da_kn_g[l],
            n_batch=B, seq=S, ctx_len=L)
        bias = na_bias_tables(na_rpb[l], S // GRID_W)
        lam = (jnp.exp(jnp.sum(da_lam_q1[l] * da_lam_k1[l])) - jnp.exp(jnp.sum(da_lam_q2[l] * da_lam_k2[l]))
               + lam_init).astype(f32)
        Ya = na_call(NAQ, NAK, NAV, bias, n_batch=B, seq=S, ctx_len=L)
        Yd = da_call(lam, DAQ, DAK, DAVX, da_subln_g[l], n_batch=B, seq=S, ctx_len=L, lam_init=lam_init)
        QE, OU, MM, NN = gdn_chunk_call(P_gdn, P_ab, gdn_conv_w[l], gdn_a_log[l], gdn_dt_bias[l], n_batch=B, seq=S)
        Of, Ob = gdn_scan_call(QE, OU, MM, NN, n_batch=B, seq=S)
        Yf = fourier_latent_call(P_f, n_batch=B, seq=S)
        n_tok = NL
        Yac, Ydc, Yfc = Ya, Yd, Yf
        if not last:
            Yac, Ydc = ctx_attn_call(lam, NAQ, NAK, NAV, DAQ, DAK, DAVX, da_subln_g[l],
                                     n_batch=B, seq=S, ctx_len=L, lam_init=lam_init)
            Yfc = fourier_ctx_call(P_f, n_batch=B, seq=S, ctx_len=L)
            n_tok = NL + B * L
        Xmid, H2, RW = merge_call(Xl, Xc, 0, (Ya, Yac), (Yd, Ydc), (Yf, Yfc), Of, Ob, P_z, P_gate, mods[l], norm2_g[l], gdn_onorm_g[l],
                                  w_branch[l].astype(bf16), w_out[l].astype(bf16), w_router, router_bias,
                                  n_tok=n_tok, n_batch=B, seq=S)
        Xl = moe_sorted_call(Xmid, H2, RW, mods[l], *experts, layer=l, row0=0, n_rows=NL, tile=MOE_TILE, win=MOE_WIN,
                             mod_row_fn=lambda i: i // (S // MOE_TILE))
        if not last:
            Xc = moe_sorted_call(Xmid, H2, RW, mods[l], *experts, layer=l, row0=NL, n_rows=B * L, tile=B * L,
                                 win=MOE_WIN_CTX, mod_row_fn=lambda i: B)
    return Xl.reshape(B, S, D)
```

```python
import functools
import math
import jax
import jax.numpy as jnp
from jax import lax
import numpy as np
from jax.experimental import pallas as pl
from jax.experimental.pallas import tpu as pltpu

D_MODEL = 1024
DEPTH = 2
GRID_W = 64
N_BRANCH = 4
BRANCH_W = D_MODEL // N_BRANCH
GDN_HEADS = 4
GDN_DIM = BRANCH_W // GDN_HEADS
GDN_CHUNK = 64
FNET_GROUPS = 4
FNET_GROUP_W = BRANCH_W // FNET_GROUPS
N_EXPERTS = 16
N_GROUPS = 4
RMS_EPS = 1e-6
NEG_INF = -1e30
ROPE_BASE = 10000.0
VMEM_LIMIT = 56 * 1024 * 1024
HI = lax.Precision.HIGHEST
NT_DIMS = (((1,), (1,)), ((), ()))
LOG2E = 1.4426950408889634


def _mod_kernel(c_ref, w_ref, b_ref, o_ref):
    c = c_ref[...]
    a = c * jax.nn.sigmoid(c)
    o_ref[0] = jnp.dot(a, w_ref[0], preferred_element_type=jnp.float32, precision=HI) + b_ref[0]


def mod_call(cvec, w_mod, b_mod, tn=512):
    depth, D, N = w_mod.shape
    return pl.pallas_call(
        _mod_kernel,
        grid=(depth, N // tn),
        in_specs=[pl.BlockSpec((8, D), lambda l, j: (0, 0)),
                  pl.BlockSpec((1, D, tn), lambda l, j: (l, 0, j)),
                  pl.BlockSpec((1, 1, tn), lambda l, j: (l, 0, j))],
        out_specs=pl.BlockSpec((1, 8, tn), lambda l, j: (l, 0, j)),
        out_shape=jax.ShapeDtypeStruct((depth, 8, N), jnp.float32),
        name="mod",
    )(cvec, w_mod, b_mod.reshape(depth, 1, N))


def _stream_specs(n_lat_tiles, ctx_tile0, tm, width):
    return [pl.BlockSpec((tm, width), lambda i: (jnp.minimum(i, n_lat_tiles - 1), 0)),
            pl.BlockSpec((tm, width), lambda i: (ctx_tile0 + jnp.maximum(i - n_lat_tiles, 0), 0))]


def _inproj_kernel(xl_ref, xc_ref, m_ref, g_ref, w_ref, wb_ref, wab_ref, cos_ref, sin_ref, g64_ref, g32_ref, gains_ref,
                   naq_ref, nak_ref, nav_ref, daq_ref, dak_ref, davx_ref, gdn_ref, z_ref, f_ref, gate_ref, ab_ref,
                   *, tiles_per_batch, n_lat_tiles, n_batch):
    i = pl.program_id(0)
    r = jnp.where(i < n_lat_tiles, i // tiles_per_batch, n_batch)
    x = jnp.where(i < n_lat_tiles, xl_ref[...], xc_ref[...])
    y = x * lax.rsqrt(jnp.mean(x * x, axis=-1, keepdims=True) + RMS_EPS) * g_ref[...]
    sh = m_ref[pl.ds(r, 1), 0:D_MODEL]
    sc = m_ref[pl.ds(r, 1), D_MODEL:2 * D_MODEL]
    h = (y * (1.0 + sc) + sh).astype(jnp.bfloat16)
    proj = lambda a, b: jnp.dot(h, w_ref[:, a:b], preferred_element_type=jnp.float32)
    proj_b = lambda a, b: jnp.dot(h, wb_ref[:, a:b], preferred_element_type=jnp.float32)
    bf16 = jnp.bfloat16

    g64 = g64_ref[...]
    naq_ref[...] = (_group_rms(proj(0, 256), g64, 64, gains_ref[0:1, :]) * (64 ** -0.5)).astype(bf16)
    nak_ref[...] = _group_rms(proj(256, 512), g64, 64, gains_ref[1:2, :]).astype(bf16)
    nav_ref[...] = proj(512, 768).astype(bf16)

    g32 = g32_ref[...]
    cos = cos_ref[...]
    sin = sin_ref[...]
    lane = lax.broadcasted_iota(jnp.int32, (1, 256), 1)
    first = (lane % 16) < 8

    def rope(t):
        swapped = jnp.where(first, pltpu.roll(t, 256 - 8, 1), pltpu.roll(t, 8, 1))
        return t * cos + swapped * sin

    q = rope(_group_rms(proj(768, 1024), g32, 32, gains_ref[2:3, :]))
    k = rope(_group_rms(proj(1024, 1280), g32, 32, gains_ref[3:4, :]))
    daq_ref[...] = (q * (32 ** -0.5 * LOG2E)).astype(bf16)
    dak_ref[...] = k.astype(bf16)
    v = proj(1280, 1536).astype(bf16)
    one_col = (lax.broadcasted_iota(jnp.int32, (v.shape[0], 64), 1) == 0).astype(bf16)
    davx_ref[...] = jnp.concatenate(
        [piece for hd in range(4) for piece in (v[:, hd * 64:(hd + 1) * 64], one_col)], -1)

    gdn_ref[...] = proj(1536, 2304)
    z_ref[...] = proj(2304, 2560).astype(bf16)
    f_ref[...] = proj_b(0, 256).astype(bf16)
    for kk in range(4):
        gate_ref[:, kk * 1024:(kk + 1) * 1024] = jax.nn.sigmoid(
            proj_b(256 + kk * 1024, 256 + (kk + 1) * 1024)).astype(bf16)
    ab_ref[...] = jnp.dot(h, wab_ref[...], preferred_element_type=jnp.float32)


def inproj_call(Xl, Xc, ctx_row0, mod_l, g, w_a, w_b, w_ab, cos, sin, na_qg, na_kg, da_qg, da_kg,
                *, n_batch, seq, ctx_len, tm=256):
    D = Xl.shape[1]
    NT = n_batch * (seq + ctx_len)
    tpb = seq // tm
    n_lat_tiles = n_batch * tpb
    kern = functools.partial(_inproj_kernel, tiles_per_batch=tpb, n_lat_tiles=n_lat_tiles, n_batch=n_batch)
    bf16, f32 = jnp.bfloat16, jnp.float32
    widths = [(256, bf16)] * 5 + [(512, bf16), (768, f32), (256, bf16), (256, bf16), (4096, bf16), (128, f32)]
    gains = jnp.stack([jnp.tile(na_qg, 4), jnp.tile(na_kg, 4), jnp.tile(da_qg, 8), jnp.tile(da_kg, 8)]
                      + [jnp.zeros((256,), f32)] * 4, 0)
    g64, g32 = _group_ones(256, 64), _group_ones(256, 32)
    tab = pl.BlockSpec((tm, 256), lambda i: (jnp.where(i < n_lat_tiles, i % tpb, tpb), 0))
    full = lambda a: pl.BlockSpec(a.shape, lambda i: (0,) * a.ndim)
    return pl.pallas_call(
        kern,
        grid=(NT // tm,),
        in_specs=_stream_specs(n_lat_tiles, ctx_row0 // tm, tm, D) + [
            full(mod_l), pl.BlockSpec((1, D), lambda i: (0, 0)), full(w_a), full(w_b), full(w_ab), tab, tab,
            full(g64), full(g32), full(gains)],
        out_specs=[pl.BlockSpec((tm, w), lambda i: (i, 0)) for w, _ in widths],
        out_shape=[jax.ShapeDtypeStruct((NT, w), dt) for w, dt in widths],
        compiler_params=pltpu.CompilerParams(vmem_limit_bytes=VMEM_LIMIT),
        name="inproj",
    )(Xl, Xc, mod_l, g.reshape(1, D), w_a, w_b, w_ab, cos, sin, g64, g32, gains)


def _group_ones(width, group):
    i = np.arange(width)
    return jnp.asarray((i[:, None] // group == i[None, :] // group).astype(np.float32), jnp.bfloat16)


def _group_sum(xx, gmat):
    hi, lo = _split_bf16(xx)
    return (jnp.dot(hi, gmat, preferred_element_type=jnp.float32)
            + jnp.dot(lo, gmat, preferred_element_type=jnp.float32))


def _group_rms(x, gmat, group, gain):
    return x * lax.rsqrt(_group_sum(x * x, gmat) * (1.0 / group) + RMS_EPS) * gain


def rope_tables(seq, tm):
    nf = 8
    t = jnp.arange(seq)
    rows = (t // GRID_W).astype(jnp.float32)
    cols = (t % GRID_W).astype(jnp.float32)
    freqs = ROPE_BASE ** (-jnp.arange(nf, dtype=jnp.float32) / nf)
    d = np.arange(32)
    f_idx = d % 8
    use_col = (d // 16) == 1
    ang = jnp.where(use_col[None, :], cols[:, None], rows[:, None]) * freqs[f_idx][None, :]
    sign = np.where((d % 16) < 8, -1.0, 1.0).astype(np.float32)
    cos = jnp.tile(jnp.cos(ang), (1, 8))
    sin = jnp.tile(jnp.sin(ang) * sign[None, :], (1, 8))
    cos = jnp.concatenate([cos, jnp.ones((tm, 256), jnp.float32)], 0)
    sin = jnp.concatenate([sin, jnp.zeros((tm, 256), jnp.float32)], 0)
    return cos, sin


NA_RB = 4
NA_UR = 11


def na_bias_tables(rpb, rows):
    reps = [(0, 0), (4, 0), (rows - NA_RB, rows - NA_UR)]
    qc = np.arange(GRID_W)[:, None]
    kc = np.arange(GRID_W)[None, :]
    ws = np.clip(qc - 8, 0, GRID_W - 16)
    vcol = (kc >= ws) & (kc < ws + 16)
    rel_c = np.clip(kc - qc + 15, 0, 30)
    sel_c = (rel_c[..., None] == np.arange(31)).astype(np.float32)
    sel_r, vrow = [], []
    for r0, u0 in reps:
        r = r0 + np.arange(NA_RB)[:, None]
        krow = u0 + np.arange(NA_UR)[None, :]
        start = np.clip(r - 4, 0, rows - 8)
        vrow.append((krow >= start) & (krow < start + 8))
        sel_r.append((np.clip(krow - r + 7, 0, 14)[..., None] == np.arange(15)).astype(np.float32))
    sel_r, vrow = np.stack(sel_r), np.stack(vrow)
    b = jnp.einsum('taui,hij,qkj->thauqk', jnp.asarray(sel_r), rpb.astype(jnp.float32), jnp.asarray(sel_c),
                   precision=HI)
    valid = vrow[:, None, :, :, None, None] & vcol[None, None, None, None, :, :]
    return jnp.where(valid, b, NEG_INF).astype(jnp.bfloat16)


def _na_kernel(q_ref, k_ref, v_ref, kc_ref, vc_ref, bias_ref, o_ref, bias_s, *, rows):
    i = pl.program_id(1)

    @pl.when((i <= 1) | (i == rows // NA_RB - 1))
    def _():
        for h in range(4):
            for a in range(NA_RB):
                for u in range(NA_UR):
                    bias_s[h, a * GRID_W:(a + 1) * GRID_W, u * GRID_W:(u + 1) * GRID_W] = (
                        bias_ref[0, h, a, u].astype(jnp.float32))

    u0 = jnp.clip(i * NA_RB - 4, 0, rows - NA_UR)
    off = pl.multiple_of(u0 * GRID_W, GRID_W)
    nkw = NA_UR * GRID_W
    kwin = k_ref[pl.ds(off, nkw), :]
    vwin = v_ref[pl.ds(off, nkw), :]
    q = q_ref[...]
    kc = kc_ref[...]
    vc = vc_ref[...]
    outs = []
    for h in range(4):
        sl = slice(h * 64, (h + 1) * 64)
        qh = q[:, sl]
        s_loc = lax.dot_general(qh, kwin[:, sl], NT_DIMS, preferred_element_type=jnp.float32) + bias_s[h]
        s_ctx = lax.dot_general(qh, kc[:, sl], NT_DIMS, preferred_element_type=jnp.float32)
        m = jnp.maximum(jnp.max(s_loc, -1, keepdims=True), jnp.max(s_ctx, -1, keepdims=True))
        p_loc = jnp.exp(s_loc - m)
        p_ctx = jnp.exp(s_ctx - m)
        l = jnp.sum(p_loc, -1, keepdims=True) + jnp.sum(p_ctx, -1, keepdims=True)
        o = (jnp.dot(p_loc.astype(jnp.bfloat16), vwin[:, sl], preferred_element_type=jnp.float32)
             + jnp.dot(p_ctx.astype(jnp.bfloat16), vc[:, sl], preferred_element_type=jnp.float32))
        outs.append(o / l)
    o_ref[...] = jnp.concatenate(outs, -1).astype(o_ref.dtype)


def na_call(NAQ, NAK, NAV, bias, *, n_batch, seq, ctx_len):
    rows = seq // GRID_W
    nblk = rows // NA_RB
    tq = NA_RB * GRID_W
    cb = n_batch * seq // ctx_len
    kern = functools.partial(_na_kernel, rows=rows)
    tsel = lambda b, i: (jnp.where(i == 0, 0, jnp.where(i == nblk - 1, 2, 1)), 0, 0, 0, 0, 0)
    return pl.pallas_call(
        kern,
        grid=(n_batch, nblk),
        in_specs=[pl.BlockSpec((tq, 256), lambda b, i: (b * nblk + i, 0)),
                  pl.BlockSpec((seq, 256), lambda b, i: (b, 0)),
                  pl.BlockSpec((seq, 256), lambda b, i: (b, 0)),
                  pl.BlockSpec((ctx_len, 256), lambda b, i: (cb + b, 0)),
                  pl.BlockSpec((ctx_len, 256), lambda b, i: (cb + b, 0)),
                  pl.BlockSpec((1,) + bias.shape[1:], tsel)],
        out_specs=pl.BlockSpec((tq, 256), lambda b, i: (b * nblk + i, 0)),
        out_shape=jax.ShapeDtypeStruct((n_batch * seq, 256), jnp.bfloat16),
        scratch_shapes=[pltpu.VMEM((4, tq, NA_UR * GRID_W), jnp.float32)],
        compiler_params=pltpu.CompilerParams(vmem_limit_bytes=VMEM_LIMIT),
        name="na_attn",
    )(NAQ, NAK, NAV, NAK, NAV, bias)


def _stack_masked_q(q):
    lane = lax.broadcasted_iota(jnp.int32, (1, 256), 1)
    return jnp.concatenate([jnp.where((lane // 32) == hm, q, jnp.zeros_like(q)) for hm in range(8)], 0)


def _da_finish(acc, lam, gain, g64, post_scale):
    tq = acc.shape[0] // 8
    norm = acc / jnp.broadcast_to(acc[:, 64:65], acc.shape)
    lane = lax.broadcasted_iota(jnp.int32, (1, 128), 1)
    halves = []
    for hp in range(2):
        d = [norm[(4 * hp + 2 * j) * tq:(4 * hp + 2 * j + 1) * tq]
             - lam * norm[(4 * hp + 2 * j + 1) * tq:(4 * hp + 2 * j + 2) * tq] for j in range(2)]
        halves.append(jnp.where(lane < 64, d[0], pltpu.roll(d[1], 64, 1)))
    o = jnp.concatenate(halves, -1)
    return _group_rms(o, g64, 64, gain) * post_scale


def _da_tile(qs, k, vx, m, acc, tq):
    tk = k.shape[0]
    s = lax.dot_general(qs, k, NT_DIMS, preferred_element_type=jnp.float32)
    m_new = jnp.maximum(m, jnp.max(s, -1, keepdims=True))
    alpha = jnp.exp2(m - m_new)
    p = jnp.exp2(s - pltpu.repeat(m_new, tk // 128, axis=1)).astype(jnp.bfloat16)
    pv = jnp.concatenate(
        [jnp.dot(p[2 * h * tq:(2 * h + 2) * tq], vx[:, h * 128:(h + 1) * 128], preferred_element_type=jnp.float32)
         for h in range(4)], 0)
    return m_new, alpha * acc + pv


def _da_kernel(lam_ref, q_ref, k_ref, vx_ref, kc_ref, vxc_ref, gain_ref, g64_ref, o_ref, m_ref, acc_ref,
               *, tk, n_kv, post_scale):
    tq = q_ref.shape[0]
    qs = _stack_masked_q(q_ref[...])
    m_ref[...] = jnp.full(m_ref.shape, -jnp.inf, jnp.float32)
    acc_ref[...] = jnp.zeros(acc_ref.shape, jnp.float32)

    def body(j, carry):
        for u in range(2):
            off = pl.multiple_of((2 * j + u) * tk, tk)
            m, acc = _da_tile(qs, k_ref[pl.ds(off, tk), :], vx_ref[pl.ds(off, tk), :], m_ref[...], acc_ref[...], tq)
            m_ref[...] = m
            acc_ref[...] = acc
        return carry

    lax.fori_loop(0, n_kv // 2 - 1, body, 0, unroll=True)
    off = (n_kv - 2) * tk
    m, acc = _da_tile(qs, k_ref[off:off + tk, :], vx_ref[off:off + tk, :], m_ref[...], acc_ref[...], tq)
    k_last = jnp.concatenate([k_ref[off + tk:off + 2 * tk, :], kc_ref[...]], 0)
    vx_last = jnp.concatenate([vx_ref[off + tk:off + 2 * tk, :], vxc_ref[...]], 0)
    m, acc = _da_tile(qs, k_last, vx_last, m, acc, tq)
    o_ref[...] = _da_finish(acc, lam_ref[0], gain_ref[...], g64_ref[...], post_scale).astype(o_ref.dtype)


def da_call(lam, DAQ, DAK, DAVX, subln_g, *, n_batch, seq, ctx_len, lam_init, tq=256, tk=1024):
    tk = min(tk, seq // 2)
    nq = seq // tq
    cb = n_batch * seq // ctx_len
    g64 = _group_ones(256, 64)
    gain = jnp.tile(subln_g, 4).reshape(1, 256)
    n_kv = seq // tk
    assert n_kv % 2 == 0
    kern = functools.partial(_da_kernel, tk=tk, n_kv=n_kv, post_scale=1.0 - lam_init)
    return pl.pallas_call(
        kern,
        grid=(n_batch, nq),
        in_specs=[pl.BlockSpec(memory_space=pltpu.SMEM),
                  pl.BlockSpec((tq, 256), lambda b, i: (b * nq + i, 0)),
                  pl.BlockSpec((seq, 256), lambda b, i: (b, 0)),
                  pl.BlockSpec((seq, 512), lambda b, i: (b, 0)),
                  pl.BlockSpec((ctx_len, 256), lambda b, i: (cb + b, 0)),
                  pl.BlockSpec((ctx_len, 512), lambda b, i: (cb + b, 0)),
                  pl.BlockSpec((1, 256), lambda b, i: (0, 0)),
                  pl.BlockSpec((256, 256), lambda b, i: (0, 0))],
        out_specs=pl.BlockSpec((tq, 256), lambda b, i: (b * nq + i, 0)),
        out_shape=jax.ShapeDtypeStruct((n_batch * seq, 256), jnp.bfloat16),
        scratch_shapes=[pltpu.VMEM((8 * tq, 128), jnp.float32), pltpu.VMEM((8 * tq, 128), jnp.float32)],
        compiler_params=pltpu.CompilerParams(vmem_limit_bytes=VMEM_LIMIT),
        name="da_attn",
    )(lam.reshape(1), DAQ, DAK, DAVX, DAK, DAVX, gain, g64)


def _ctx_kernel(lam_ref, naq_ref, nak_ref, nav_ref, daq_ref, dak_ref, dav_ref, gain_ref, g64_ref, ya_ref, yd_ref,
                *, post_scale):
    q = naq_ref[...]
    k = nak_ref[...]
    v = nav_ref[...]
    outs = []
    for h in range(4):
        sl = slice(h * 64, (h + 1) * 64)
        s = lax.dot_general(q[:, sl], k[:, sl], NT_DIMS, preferred_element_type=jnp.float32)
        p = jnp.exp(s - jnp.max(s, -1, keepdims=True))
        o = jnp.dot(p.astype(jnp.bfloat16), v[:, sl], preferred_element_type=jnp.float32)
        outs.append(o / jnp.sum(p, -1, keepdims=True))
    ya_ref[...] = jnp.concatenate(outs, -1).astype(ya_ref.dtype)
    tq = daq_ref.shape[0]
    qs = _stack_masked_q(daq_ref[...])
    m0 = jnp.full((8 * tq, 128), -jnp.inf, jnp.float32)
    m, acc = _da_tile(qs, dak_ref[...], dav_ref[...], m0, jnp.zeros((8 * tq, 128), jnp.float32), tq)
    yd_ref[...] = _da_finish(acc, lam_ref[0], gain_ref[...], g64_ref[...], post_scale).astype(yd_ref.dtype)


def ctx_attn_call(lam, NAQ, NAK, NAV, DAQ, DAK, DAVX, subln_g, *, n_batch, seq, ctx_len, lam_init):
    cb = n_batch * seq // ctx_len
    g64 = _group_ones(256, 64)
    gain = jnp.tile(subln_g, 4).reshape(1, 256)
    blk = lambda col: pl.BlockSpec((ctx_len, 256), lambda b: (cb + b, col))
    return pl.pallas_call(
        functools.partial(_ctx_kernel, post_scale=1.0 - lam_init),
        grid=(n_batch,),
        in_specs=[pl.BlockSpec(memory_space=pltpu.SMEM), blk(0), blk(0), blk(0), blk(0), blk(0),
                  pl.BlockSpec((ctx_len, 512), lambda b: (cb + b, 0)),
                  pl.BlockSpec((1, 256), lambda b: (0, 0)), pl.BlockSpec((256, 256), lambda b: (0, 0))],
        out_specs=[pl.BlockSpec((ctx_len, 256), lambda b: (b, 0))] * 2,
        out_shape=[jax.ShapeDtypeStruct((n_batch * ctx_len, 256), jnp.bfloat16)] * 2,
        name="ctx_attn",
    )(lam.reshape(1), NAQ, NAK, NAV, DAQ, DAK, DAVX, gain, g64)


def _split_bf16(a):
    hi = a.astype(jnp.bfloat16)
    return hi, (a - hi.astype(jnp.float32)).astype(jnp.bfloat16)


def _route(logits_t, bias_col):
    per_group = N_EXPERTS // N_GROUPS
    scores = jax.nn.sigmoid(logits_t)
    sel_all = scores + bias_col
    sel = [sel_all[e:e + 1, :] for e in range(N_EXPERTS)]
    top2 = []
    for e in range(N_EXPERTS):
        g0 = e // per_group * per_group
        rank = jnp.zeros_like(sel[e])
        for o in range(g0, g0 + per_group):
            if o != e:
                beats = (sel[o] > sel[e]) | (sel[o] == sel[e]) if o < e else (sel[o] > sel[e])
                rank = rank + beats.astype(jnp.float32)
        top2.append(rank < 1.5)
    gs = []
    for g in range(N_GROUPS):
        acc = jnp.zeros_like(sel[0])
        for e in range(g * per_group, (g + 1) * per_group):
            acc = acc + jnp.where(top2[e], sel[e], 0.0)
        gs.append(acc)
    rows = []
    for g in range(N_GROUPS):
        beaten = jnp.zeros_like(sel[0])
        for o in range(N_GROUPS):
            if o != g:
                b = (gs[o] >= gs[g]) if o < g else (gs[o] > gs[g])
                beaten = beaten + b.astype(jnp.float32)
        best = beaten < 0.5
        for e in range(g * per_group, (g + 1) * per_group):
            rows.append(jnp.where(best & top2[e], scores[e:e + 1, :], 0.0))
    w = jnp.concatenate(rows, 0)
    return w / jnp.sum(w, axis=0, keepdims=True)


def _merge_kernel(xl_ref, xc_ref, yal_ref, yac_ref, ydl_ref, ydc_ref, yfl_ref, yfc_ref, of_ref, ob_ref, z_ref, gate_ref,
                  m_ref, g2_ref, og_ref, g64_ref, wb_ref, wo_ref, wr_ref, rb_ref, xo_ref, h2_ref, rw_ref, *, tiles_per_batch, n_lat_tiles, n_batch):
    i = pl.program_id(0)
    r = jnp.where(i < n_lat_tiles, i // tiles_per_batch, n_batch)
    z = z_ref[...].astype(jnp.float32)
    yg = (_group_rms(of_ref[...] + ob_ref[...], g64_ref[...], GDN_DIM, og_ref[...]) * (z * jax.nn.sigmoid(z)))
    is_lat = i < n_lat_tiles
    pick = lambda lat_ref, ctx_ref: jnp.where(is_lat, lat_ref[...], ctx_ref[...])
    branches = (pick(yal_ref, yac_ref), pick(ydl_ref, ydc_ref), yg.astype(jnp.bfloat16),
                pick(yfl_ref, yfc_ref).astype(jnp.bfloat16))
    mix = None
    for b, yb in enumerate(branches):
        proj = jnp.dot(yb, wb_ref[b], preferred_element_type=jnp.float32)
        term = gate_ref[:, b * D_MODEL:(b + 1) * D_MODEL].astype(jnp.float32) * proj
        mix = term if mix is None else mix + term
    y = jnp.dot(mix.astype(jnp.bfloat16), wo_ref[...], preferred_element_type=jnp.float32)
    mrow = lambda k: m_ref[pl.ds(r, 1), k * D_MODEL:(k + 1) * D_MODEL]
    x = jnp.where(i < n_lat_tiles, xl_ref[...], xc_ref[...]) + mrow(2) * y
    xo_ref[...] = x
    h2 = (x * lax.rsqrt(jnp.mean(x * x, axis=-1, keepdims=True) + RMS_EPS) * g2_ref[...]) * (1.0 + mrow(4)) + mrow(3)
    h2_ref[...] = h2.astype(h2_ref.dtype)
    h_hi, h_lo = _split_bf16(h2)
    w_hi, w_lo = _split_bf16(wr_ref[...])
    nt = lambda a, b: lax.dot_general(a, b, NT_DIMS, preferred_element_type=jnp.float32)
    logits_t = nt(w_hi, h_hi) + nt(w_hi, h_lo) + nt(w_lo, h_hi)
    w_t = _route(logits_t, rb_ref[...])
    pad = jnp.zeros((128 - N_EXPERTS, w_t.shape[1]), jnp.float32)
    rw_ref[...] = jnp.transpose(jnp.concatenate([w_t, pad], 0))


def merge_call(Xl, Xc, ctx_row0, Ya, Yd, Yf, Of, Ob, P_z, P_gate, mod_l, g2n, onorm_g, w_branch, w_out, w_router,
               router_bias, *, n_tok, n_batch, seq, tm=256):
    D = Xl.shape[1]
    n_lat_tiles = n_batch * seq // tm
    kern = functools.partial(_merge_kernel, tiles_per_batch=seq // tm, n_lat_tiles=n_lat_tiles, n_batch=n_batch)
    row = lambda w: pl.BlockSpec((tm, w), lambda i: (i, 0))
    full = lambda a: pl.BlockSpec(a.shape, lambda i: (0,) * a.ndim)
    wr_t = jnp.transpose(w_router)
    rb = router_bias.reshape(N_EXPERTS, 1)
    g2 = g2n.reshape(1, D)
    og = jnp.tile(onorm_g, GDN_HEADS).reshape(1, BRANCH_W)
    g64 = _group_ones(BRANCH_W, GDN_DIM)
    return pl.pallas_call(
        kern,
        grid=(n_tok // tm,),
        in_specs=_stream_specs(n_lat_tiles, ctx_row0 // tm, tm, D) + 3 * _stream_specs(n_lat_tiles, 0, tm, 256) + [
                  row(256), row(256), row(256), row(4 * D), full(mod_l),
                  full(g2), full(og), full(g64), full(w_branch), full(w_out), full(wr_t), full(rb)],
        out_specs=[row(D), row(D), row(128)],
        out_shape=[jax.ShapeDtypeStruct((n_tok, D), jnp.float32), jax.ShapeDtypeStruct((n_tok, D), jnp.bfloat16),
                   jax.ShapeDtypeStruct((n_tok, 128), jnp.float32)],
        compiler_params=pltpu.CompilerParams(vmem_limit_bytes=VMEM_LIMIT),
        name="merge",
    )(Xl, Xc, *Ya, *Yd, *Yf, Of, Ob, P_z, P_gate, mod_l, g2, og, g64, w_branch, w_out, wr_t, rb)


MOE_TILE = 1024
MOE_WIN = 320
MOE_WIN_CTX = 176
MOE_EPS = 4


def _moe_sorted_kernel(x_ref, h_ref, rw_ref, m_ref, wg_ref, wu_ref, wd_ref, o_ref, xs_ref, ws_ref, ys_ref, dest_ref,
                       seg_ref, *, win, mod_row_fn):
    T = h_ref.shape[0]
    per_group = N_EXPERTS // N_GROUPS
    i = pl.program_id(0)
    step = pl.program_id(1)
    lane = lax.broadcasted_iota(jnp.int32, (1, 128), 1)
    f32, bf16 = jnp.float32, jnp.bfloat16

    @pl.when(step == 0)
    def _():
        rw = rw_ref[...]
        r128 = lax.broadcasted_iota(jnp.int32, (128, 128), 0)
        c128 = lax.broadcasted_iota(jnp.int32, (128, 128), 1)
        e2g = ((r128 // per_group == c128) & (r128 < N_EXPERTS)).astype(bf16)
        og = (jnp.dot((rw > 0.0).astype(bf16), e2g, preferred_element_type=f32) > 0.5).astype(f32)
        rt = lax.broadcasted_iota(jnp.int32, (T, T), 0)
        ct = lax.broadcasted_iota(jnp.int32, (T, T), 1)
        earlier = jnp.dot((rt > ct).astype(bf16), og.astype(bf16), preferred_element_type=f32)
        cnt_row = jnp.sum(og, axis=0, keepdims=True)
        start_row = jnp.dot(jnp.broadcast_to(cnt_row, (8, 128)), (r128 < c128).astype(f32),
                            preferred_element_type=f32, precision=HI)[0:1]
        dest = jnp.sum(og * (earlier + start_row), axis=-1, keepdims=True)
        dest_ref[...] = jnp.broadcast_to(dest, (T, 128))
        dest_row = jnp.transpose(dest_ref[...])[0:1, :]
        perm = (dest_row == rt.astype(f32)).astype(bf16)
        xs_ref[...] = jnp.dot(perm, h_ref[...], preferred_element_type=f32).astype(bf16)
        ws_ref[...] = sum(jnp.dot(perm, piece, preferred_element_type=f32) for piece in _split_bf16(rw))
        ys_ref[...] = jnp.zeros(ys_ref.shape, f32)
        for g in range(N_GROUPS):
            seg_ref[g] = jnp.sum(jnp.where(lane == g, start_row, 0.0)).astype(jnp.int32)
            seg_ref[N_GROUPS + g] = jnp.sum(jnp.where(lane == g, cnt_row, 0.0)).astype(jnp.int32)

    g = (step * MOE_EPS) // per_group
    start = seg_ref[g]
    end = start + seg_ref[N_GROUPS + g]
    s0 = (start // 16) * 16
    n_win = (end - s0 + win - 1) // win

    def window(w, carry):
        lo = s0 + w * win
        off = pl.multiple_of(jnp.minimum(lo, T - win), 16)
        xw = xs_ref[pl.ds(off, win), :]
        ww = ws_ref[pl.ds(off, win), :]
        row = off + lax.broadcasted_iota(jnp.int32, (win, 1), 0)
        y = None
        for j in range(MOE_EPS):
            w_col = jnp.sum(jnp.where(lane == step * MOE_EPS + j, ww, 0.0), axis=-1, keepdims=True)
            w_col = jnp.where(row >= lo, w_col, 0.0)
            a = jnp.dot(xw, wg_ref[0, j], preferred_element_type=f32)
            u = jnp.dot(xw, wu_ref[0, j], preferred_element_type=f32)
            act = (a * jax.nn.sigmoid(a) * u * w_col).astype(bf16)
            yj = jnp.dot(act, wd_ref[0, j], preferred_element_type=f32)
            y = yj if y is None else y + yj
        ys_ref[pl.ds(off, win), :] += y
        return carry

    lax.fori_loop(0, n_win, window, 0)

    @pl.when(step == N_EXPERTS // MOE_EPS - 1)
    def _():
        ct = lax.broadcasted_iota(jnp.int32, (T, T), 1)
        unperm = (pltpu.repeat(dest_ref[...], T // 128, axis=1) == ct.astype(f32)).astype(bf16)
        f = jnp.dot(unperm, ys_ref[...].astype(bf16), preferred_element_type=f32)
        g2 = m_ref[pl.ds(mod_row_fn(i), 1), 5 * D_MODEL:6 * D_MODEL]
        o_ref[...] = x_ref[...] + g2 * f


def moe_sorted_call(Xmid, H2, RW, mod_l, w_gate, w_up, w_down, *, layer, row0, n_rows, tile, win, mod_row_fn):
    D = Xmid.shape[1]
    F = w_gate.shape[-1]
    b0 = row0 // tile
    tok = lambda w: pl.BlockSpec((tile, w), lambda i, e: (b0 + i, 0))
    kern = functools.partial(_moe_sorted_kernel, win=win, mod_row_fn=mod_row_fn)
    return pl.pallas_call(
        kern,
        grid=(n_rows // tile, N_EXPERTS // MOE_EPS),
        in_specs=[tok(D), tok(D), tok(128),
                  pl.BlockSpec(mod_l.shape, lambda i, e: (0, 0)),
                  pl.BlockSpec((1, MOE_EPS, D, F), lambda i, e: (layer, e, 0, 0)),
                  pl.BlockSpec((1, MOE_EPS, D, F), lambda i, e: (layer, e, 0, 0)),
                  pl.BlockSpec((1, MOE_EPS, F, D), lambda i, e: (layer, e, 0, 0))],
        out_specs=pl.BlockSpec((tile, D), lambda i, e: (i, 0)),
        out_shape=jax.ShapeDtypeStruct((n_rows, D), jnp.float32),
        scratch_shapes=[pltpu.VMEM((tile, D), jnp.bfloat16), pltpu.VMEM((tile, 128), jnp.float32),
                        pltpu.VMEM((tile, D), jnp.float32), pltpu.VMEM((tile, 128), jnp.float32),
                        pltpu.SMEM((2 * N_GROUPS,), jnp.int32)],
        compiler_params=pltpu.CompilerParams(vmem_limit_bytes=VMEM_LIMIT),
        name="moe_sorted",
    )(Xmid, H2, RW, mod_l, w_gate, w_up, w_down)


GDN_TILE = 256
GDN_CPT = GDN_TILE // GDN_CHUNK
GDN_LOCKSTEP = 2


def _mm(a, b, passes=1, dims=None):
    if dims is None:
        dot = lambda x, y: jnp.dot(x, y, preferred_element_type=jnp.float32)
    else:
        dot = lambda x, y: lax.dot_general(x, y, dims, preferred_element_type=jnp.float32)
    if passes == 1:
        return dot(a.astype(jnp.bfloat16), b.astype(jnp.bfloat16))
    a_hi, a_lo = _split_bf16(a)
    b_hi, b_lo = _split_bf16(b)
    return dot(a_hi, b_hi) + dot(a_hi, b_lo) + dot(a_lo, b_hi)


def _stack_heads(x):
    lane = lax.broadcasted_iota(jnp.int32, (1, 256), 1)
    return jnp.concatenate([jnp.where((lane // GDN_DIM) == h, x, 0.0) for h in range(GDN_HEADS)], 0)


def _slab(x):
    return x[0:64] + x[64:128] + x[128:192] + x[192:256]


def _unit_tri_inverse(mats):
    shape = mats[0].shape
    eye = (lax.broadcasted_iota(jnp.int32, shape, 0) == lax.broadcasted_iota(jnp.int32, shape, 1)).astype(jnp.float32)
    ps = [eye - a for a in mats]
    pws = list(mats)
    for _ in range(5):
        pws = [_mm(pw, pw, 3) for pw in pws]
        ps = [p + _mm(p, pw, 3) for p, pw in zip(ps, pws)]
    return ps


def _gdn_chunk_kernel(x_ref, xp_ref, xn_ref, ab_ref, cw_ref, par_ref, g64_ref, qe_ref, ou_ref, mm_ref, nn_ref,
                      *, tiles_per_batch, n_lat_tiles):
    i = pl.program_id(0)
    is_lat = i < n_lat_tiles
    first = jnp.where(is_lat, (i % tiles_per_batch) == 0, True)
    last = jnp.where(is_lat, (i % tiles_per_batch) == tiles_per_batch - 1, True)
    xp = jnp.concatenate([jnp.where(first, 0.0, xp_ref[...]), x_ref[...], jnp.where(last, 0.0, xn_ref[...])], 0)
    y = xp[6:6 + GDN_TILE] * cw_ref[0:1, :]
    for t in range(1, 5):
        y = y + xp[6 + t:6 + t + GDN_TILE] * cw_ref[t:t + 1, :]
    y = y * jax.nn.sigmoid(y)
    g64 = g64_ref[...]
    q = y[:, 0:256]
    k = y[:, 256:512]
    v = y[:, 512:768]
    q = q * lax.rsqrt(_group_sum(q * q, g64) + RMS_EPS) * (GDN_DIM ** -0.5)
    k = k * lax.rsqrt(_group_sum(k * k, g64) + RMS_EPS)
    ab = ab_ref[...]
    lane128 = lax.broadcasted_iota(jnp.int32, (1, 128), 1)
    g_all = jnp.where(lane128 < 8, par_ref[0:1, :] * jax.nn.softplus(ab + par_ref[1:2, :]), 0.0)
    beta_all = jax.nn.sigmoid(ab)
    r64 = lax.broadcasted_iota(jnp.int32, (64, 64), 0)
    c64 = lax.broadcasted_iota(jnp.int32, (64, 64), 1)
    low = (r64 >= c64).astype(jnp.float32)
    upp = (r64 <= c64).astype(jnp.float32)
    rr = lax.broadcasted_iota(jnp.int32, (256, 256), 0)
    cc = lax.broadcasted_iota(jnp.int32, (256, 256), 1)
    same = (rr // 64) == (cc // 64)
    eye = (rr == cc).astype(jnp.float32)
    for c0 in range(0, GDN_CPT, GDN_LOCKSTEP):
        stage = []
        for c in range(c0, c0 + GDN_LOCKSTEP):
            sl = slice(c * GDN_CHUNK, (c + 1) * GDN_CHUNK)
            g_c = g_all[sl]
            gsum = jnp.where(lane128 < 4, jnp.dot(low, g_c, preferred_element_type=jnp.float32, precision=HI),
                             jnp.dot(upp, g_c, preferred_element_type=jnp.float32, precision=HI))
            gsum_t = jnp.transpose(jnp.concatenate([gsum, jnp.zeros_like(gsum)], 0))[:, 0:64]
            tot = jnp.sum(g_c, axis=0, keepdims=True)
            beta_c = beta_all[sl]
            q_sm = _stack_heads(q[sl])
            k_sm = _stack_heads(k[sl])
            v_sm = _stack_heads(v[sl])
            kk = _mm(k_sm, k_sm, 1, NT_DIMS)
            qk = _mm(q_sm, k_sm, 1, NT_DIMS)
            for d in range(2):
                col = lambda h: 4 * d + h
                cmat = jnp.concatenate([jnp.broadcast_to(gsum[:, col(h):col(h) + 1], (64, 256)) for h in range(4)], 0)
                rrow = jnp.concatenate([gsum_t[col(h):col(h) + 1, :] for h in range(4)], 1)
                bmat = jnp.concatenate([jnp.broadcast_to(beta_c[:, 8 + col(h):9 + col(h)], (64, 256))
                                        for h in range(4)], 0)
                tmat = jnp.concatenate([jnp.broadcast_to(tot[:, col(h):col(h) + 1], (64, 256)) for h in range(4)], 0)
                tri = (rr % 64 >= cc % 64) if d == 0 else (rr % 64 <= cc % 64)
                incl = same & tri
                strict = incl & (rr != cc)
                dec = jnp.where(incl, jnp.exp(jnp.where(incl, cmat - rrow, 0.0)), 0.0)
                stage.append((d, sl, q_sm, k_sm, v_sm, qk, cmat, bmat, tmat, dec,
                              jnp.where(strict, kk * bmat * dec, 0.0)))
        t_invs = _unit_tri_inverse([st[-1] for st in stage])
        n = len(stage)
        q_sms, k_sms, v_sms, qks, cmats, bmats, tmats, decs = (
            [st[f] for st in stage] for f in range(2, 10))
        egs = [jnp.exp(cm) for cm in cmats]
        us = [_mm(t_invs[j], v_sms[j] * bmats[j], 1) for j in range(n)]
        ws = [_mm(t_invs[j], k_sms[j] * bmats[j] * egs[j], 1) for j in range(n)]
        qkms = [decs[j] * qks[j] for j in range(n)]
        qes = [q_sms[j] * egs[j] - _mm(qkms[j], ws[j], 1) for j in range(n)]
        ous = [_mm(qkms[j], us[j], 1) for j in range(n)]
        kd_ts = [jnp.transpose(k_sms[j] * jnp.exp(tmats[j] - cmats[j])) for j in range(n)]
        m_bds = [eye * jnp.exp(tmats[j]) - _mm(kd_ts[j], ws[j], 1) for j in range(n)]
        n_bds = [_mm(kd_ts[j], us[j], 1) for j in range(n)]
        for j in range(n):
            d, sl = stage[j][0], stage[j][1]
            qe_ref[d, sl, :] = _slab(qes[j])
            ou_ref[d, sl, :] = _slab(ous[j])
            mm_ref[d, sl, :] = _slab(m_bds[j])
            nn_ref[d, sl, :] = _slab(n_bds[j])


def gdn_chunk_call(P_gdn, P_ab, conv_w, a_log, dt_bias, *, n_batch, seq):
    NT = P_gdn.shape[0]
    tpb = seq // GDN_TILE
    n_lat = n_batch * tpb
    n_tiles = NT // GDN_TILE
    hb = GDN_TILE // 8
    cw = jnp.zeros((8, 768), jnp.float32).at[:5].set(conv_w)
    par = jnp.zeros((8, 128), jnp.float32)
    par = par.at[0, :8].set(-jnp.exp(a_log.reshape(8))).at[1, :8].set(dt_bias.reshape(8))
    g64 = _group_ones(256, 64)
    kern = functools.partial(_gdn_chunk_kernel, tiles_per_batch=tpb, n_lat_tiles=n_lat)
    full = lambda a: pl.BlockSpec(a.shape, lambda i: (0,) * a.ndim)
    out = pl.BlockSpec((2, GDN_TILE, 256), lambda i: (0, i, 0))
    return pl.pallas_call(
        kern,
        grid=(n_tiles,),
        in_specs=[pl.BlockSpec((GDN_TILE, 768), lambda i: (i, 0)),
                  pl.BlockSpec((8, 768), lambda i: (jnp.maximum(i * hb - 1, 0), 0)),
                  pl.BlockSpec((8, 768), lambda i: (jnp.minimum((i + 1) * hb, n_tiles * hb - 1), 0)),
                  pl.BlockSpec((GDN_TILE, 128), lambda i: (i, 0)),
                  full(cw), full(par), full(g64)],
        out_specs=[out] * 4,
        out_shape=[jax.ShapeDtypeStruct((2, NT, 256), jnp.float32)] * 4,
        compiler_params=pltpu.CompilerParams(vmem_limit_bytes=VMEM_LIMIT),
        name="gdn_chunk",
    )(P_gdn, P_gdn, P_gdn, P_ab, cw, par, g64)


def _gdn_scan_kernel(qef_ref, ouf_ref, mf_ref, nf_ref, qeb_ref, oub_ref, mb_ref, nb_ref, of_ref, ob_ref, s_ref):
    t = pl.program_id(1)

    @pl.when(t == 0)
    def _():
        s_ref[...] = jnp.zeros(s_ref.shape, jnp.float32)

    def step(d, c, qe_ref, ou_ref, m_ref, n_ref, o_ref):
        sl = slice(c * GDN_CHUNK, (c + 1) * GDN_CHUNK)
        s = s_ref[d]
        o_ref[sl, :] = _slab(_mm(_stack_heads(qe_ref[0, sl, :]), s, 1)) + ou_ref[0, sl, :]
        s_ref[d] = _mm(_stack_heads(m_ref[0, sl, :]), s, 3) + _stack_heads(n_ref[0, sl, :])

    for c in range(GDN_CPT):
        step(0, c, qef_ref, ouf_ref, mf_ref, nf_ref, of_ref)
        step(1, GDN_CPT - 1 - c, qeb_ref, oub_ref, mb_ref, nb_ref, ob_ref)


def gdn_scan_call(QE, OU, MM, NN, *, n_batch, seq):
    NT = QE.shape[1]
    tpb = seq // GDN_TILE
    cblk = n_batch * tpb
    fwd = lambda b, t: jnp.where(t == 0, cblk + b, b * tpb + t - 1)
    bwd = lambda b, t: jnp.where(t == 0, cblk + b, b * tpb + tpb - t)
    spec = lambda d, f: pl.BlockSpec((1, GDN_TILE, 256), lambda b, t: (d, f(b, t), 0))
    ospec = lambda f: pl.BlockSpec((GDN_TILE, 256), lambda b, t: (f(b, t), 0))
    return pl.pallas_call(
        _gdn_scan_kernel,
        grid=(n_batch, tpb + 1),
        in_specs=[spec(0, fwd)] * 4 + [spec(1, bwd)] * 4,
        out_specs=[ospec(fwd), ospec(bwd)],
        out_shape=[jax.ShapeDtypeStruct((NT, 256), jnp.float32)] * 2,
        scratch_shapes=[pltpu.VMEM((2, 256, 256), jnp.float32)],
        compiler_params=pltpu.CompilerParams(vmem_limit_bytes=VMEM_LIMIT),
        name="gdn_scan",
    )(QE, OU, MM, NN, QE, OU, MM, NN)


def _dft_cs(n):
    a = 2.0 * np.pi * np.outer(np.arange(n), np.arange(n)) / n
    return np.cos(a), np.sin(a)


def _channel_dft():
    c, s = _dft_cs(FNET_GROUP_W)
    eye = np.eye(FNET_GROUPS)
    return np.concatenate([np.kron(eye, c), np.kron(eye, s)], 1)


FNET_SUB = 16


def _fnet_a_kernel(x_ref, cs_ref, m_ref, cphi_ref, sphi_ref, br_ref, bi_ref, *, n_rows):
    for j in range(FNET_SUB):
        sl = slice(j * BRANCH_W, (j + 1) * BRANCH_W)
        u = jnp.dot(x_ref[:, j, :], cs_ref[...], preferred_element_type=jnp.float32)
        st = jnp.concatenate([u[:, :BRANCH_W], u[:, BRANCH_W:]], 0).astype(jnp.bfloat16)
        a = jnp.dot(m_ref[...], st, preferred_element_type=jnp.float32)
        ar, ai = a[:n_rows], a[n_rows:]
        cp, sp = cphi_ref[:, sl], sphi_ref[:, sl]
        br_ref[j] = ar * cp - ai * sp
        bi_ref[j] = ar * sp + ai * cp


def _fnet_c_kernel(br_ref, bi_ref, m_ref, o_ref):
    for j in range(FNET_SUB):
        st = jnp.concatenate([br_ref[:, j, :], bi_ref[:, j, :]], 0).astype(jnp.bfloat16)
        o_ref[:, j, :] = jnp.dot(m_ref[...], st, preferred_element_type=jnp.float32)


def fourier_latent_call(P_f, *, n_batch, seq):
    rows = seq // GRID_W
    bf16, f32 = jnp.bfloat16, jnp.float32
    c1, s1 = _dft_cs(rows)
    norm = 1.0 / math.sqrt(seq * FNET_GROUP_W)
    m_a = jnp.asarray(np.block([[c1, -s1], [s1, c1]]) * norm, bf16)
    c2, s2 = _dft_cs(GRID_W)
    m_c = jnp.asarray(np.concatenate([c2, -s2], 1), bf16)
    cs = jnp.asarray(_channel_dft(), bf16)
    phi = 2.0 * np.pi * np.outer(np.arange(rows), np.arange(GRID_W)) / seq
    cphi = jnp.repeat(jnp.asarray(np.cos(phi), f32), BRANCH_W, axis=1)
    sphi = jnp.repeat(jnp.asarray(np.sin(phi), f32), BRANCH_W, axis=1)
    xv = P_f.reshape(P_f.shape[0] // GRID_W, GRID_W, BRANCH_W)
    nj = GRID_W // FNET_SUB
    full = lambda a: pl.BlockSpec(a.shape, lambda b, j: (0,) * a.ndim)
    tw = pl.BlockSpec((rows, FNET_SUB * BRANCH_W), lambda b, j: (0, j))
    mid = jax.ShapeDtypeStruct((n_batch * GRID_W, rows, BRANCH_W), f32)
    br, bi = pl.pallas_call(
        functools.partial(_fnet_a_kernel, n_rows=rows),
        grid=(n_batch, nj),
        in_specs=[pl.BlockSpec((rows, FNET_SUB, BRANCH_W), lambda b, j: (b, j, 0)), full(cs), full(m_a), tw, tw],
        out_specs=[pl.BlockSpec((FNET_SUB, rows, BRANCH_W), lambda b, j: (b * nj + j, 0, 0))] * 2,
        out_shape=[mid, mid],
        name="fnet_rows",
    )(xv, cs, m_a, cphi, sphi)
    blk = pl.BlockSpec((GRID_W, FNET_SUB, BRANCH_W), lambda b, i: (b, i, 0))
    y = pl.pallas_call(
        _fnet_c_kernel,
        grid=(n_batch, rows // FNET_SUB),
        in_specs=[blk, blk, pl.BlockSpec(m_c.shape, lambda b, i: (0, 0))],
        out_specs=blk,
        out_shape=mid,
        name="fnet_cols",
    )(br, bi, m_c)
    return y.reshape(n_batch * seq, BRANCH_W)


def _fnet_ctx_kernel(x_ref, cs_ref, m_ref, o_ref):
    u = jnp.dot(x_ref[...], cs_ref[...], preferred_element_type=jnp.float32)
    st = jnp.concatenate([u[:, :BRANCH_W], u[:, BRANCH_W:]], 0).astype(jnp.bfloat16)
    o_ref[...] = jnp.dot(m_ref[...], st, preferred_element_type=jnp.float32).astype(o_ref.dtype)


def fourier_ctx_call(P_f, *, n_batch, seq, ctx_len):
    bf16 = jnp.bfloat16
    c, s = _dft_cs(ctx_len)
    m = jnp.asarray(np.concatenate([c, -s], 1) / math.sqrt(ctx_len * FNET_GROUP_W), bf16)
    cs = jnp.asarray(_channel_dft(), bf16)
    cb = n_batch * seq // ctx_len
    return pl.pallas_call(
        _fnet_ctx_kernel,
        grid=(n_batch,),
        in_specs=[pl.BlockSpec((ctx_len, BRANCH_W), lambda b: (cb + b, 0)),
                  pl.BlockSpec(cs.shape, lambda b: (0, 0)), pl.BlockSpec(m.shape, lambda b: (0, 0))],
        out_specs=pl.BlockSpec((ctx_len, BRANCH_W), lambda b: (b, 0)),
        out_shape=jax.ShapeDtypeStruct((n_batch * ctx_len, BRANCH_W), jnp.float32),
        name="fnet_ctx",
    )(P_f, cs, m)


def kernel(x, c, ctx, c_ctx, w_mod, b_mod, norm1_g, norm2_g, w_in, na_qn_g, na_kn_g, na_rpb,
           da_qn_g, da_kn_g, da_lam_q1, da_lam_k1, da_lam_q2, da_lam_k2, da_subln_g, gdn_conv_w,
           gdn_a_log, gdn_dt_bias, gdn_onorm_g, w_branch, w_out, w_router, router_bias,
           moe_w_gate, moe_w_up, moe_w_down):
    B, S, D = x.shape
    L = ctx.shape[1]
    NL = B * S
    f32 = jnp.float32
    cvec = jnp.zeros((8, D), f32).at[:B].set(c).at[B].set(c_ctx)
    mods = mod_call(cvec, w_mod, b_mod)
    cos, sin = rope_tables(S, 256)
    bf16 = jnp.bfloat16
    Xl, Xc = x.reshape(NL, D), ctx.reshape(B * L, D)
    experts = (moe_w_gate.astype(bf16), moe_w_up.astype(bf16), moe_w_down.astype(bf16))
    for l in range(DEPTH):
        last = l == DEPTH - 1
        lam_init = 0.8 - 0.6 * math.exp(-0.3 * l)
        w_a = w_in[l, :, :2560].astype(bf16)
        w_b = w_in[l, :, 2576:].astype(bf16)
        w_ab = jnp.pad(w_in[l, :, 2560:2576], ((0, 0), (0, 112))).astype(bf16)
        NAQ, NAK, NAV, DAQ, DAK, DAVX, P_gdn, P_z, P_f, P_gate, P_ab = inproj_call(
            Xl, Xc, 0, mods[l], norm1_g[l], w_a, w_b, w_ab, cos, sin, na_qn_g[l], na_kn_g[l], da_qn_g[l], da_kn_g[l],
            n_batch=B, seq=S, ctx_len=L)
        bias = na_bias_tables(na_rpb[l], S // GRID_W)
        lam = (jnp.exp(jnp.sum(da_lam_q1[l] * da_lam_k1[l])) - jnp.exp(jnp.sum(da_lam_q2[l] * da_lam_k2[l]))
               + lam_init).astype(f32)
        Ya = na_call(NAQ, NAK, NAV, bias, n_batch=B, seq=S, ctx_len=L)
        Yd = da_call(lam, DAQ, DAK, DAVX, da_subln_g[l], n_batch=B, seq=S, ctx_len=L, lam_init=lam_init)
        QE, OU, MM, NN = gdn_chunk_call(P_gdn, P_ab, gdn_conv_w[l], gdn_a_log[l], gdn_dt_bias[l], n_batch=B, seq=S)
        Of, Ob = gdn_scan_call(QE, OU, MM, NN, n_batch=B, seq=S)
        Yf = fourier_latent_call(P_f, n_batch=B, seq=S)
        n_tok = NL
        Yac, Ydc, Yfc = Ya, Yd, Yf
        if not last:
            Yac, Ydc = ctx_attn_call(lam, NAQ, NAK, NAV, DAQ, DAK, DAVX, da_subln_g[l],
                                     n_batch=B, seq=S, ctx_len=L, lam_init=lam_init)
            Yfc = fourier_ctx_call(P_f, n_batch=B, seq=S, ctx_len=L)
            n_tok = NL + B * L
        Xmid, H2, RW = merge_call(Xl, Xc, 0, (Ya, Yac), (Yd, Ydc), (Yf, Yfc), Of, Ob, P_z, P_gate, mods[l], norm2_g[l], gdn_onorm_g[l],
                                  w_branch[l].astype(bf16), w_out[l].astype(bf16), w_router, router_bias,
                                  n_tok=n_tok, n_batch=B, seq=S)
        Xl = moe_sorted_call(Xmid, H2, RW, mods[l], *experts, layer=l, row0=0, n_rows=NL, tile=MOE_TILE, win=MOE_WIN,
                             mod_row_fn=lambda i: i // (S // MOE_TILE))
        if not last:
            Xc = moe_sorted_call(Xmid, H2, RW, mods[l], *experts, layer=l, row0=NL, n_rows=B * L, tile=B * L,
                                 win=MOE_WIN_CTX, mod_row_fn=lambda i: B)
    return Xl.reshape(B, S, D)
```

```python
import functools
import math
import jax
import jax.numpy as jnp
from jax import lax
import numpy as np
from jax.experimental import pallas as pl
from jax.experimental.pallas import tpu as pltpu

D_MODEL = 1024
DEPTH = 2
GRID_W = 64
N_BRANCH = 4
BRANCH_W = D_MODEL // N_BRANCH
GDN_HEADS = 4
GDN_DIM = BRANCH_W // GDN_HEADS
GDN_CHUNK = 64
FNET_GROUPS = 4
FNET_GROUP_W = BRANCH_W // FNET_GROUPS
N_EXPERTS = 16
N_GROUPS = 4
RMS_EPS = 1e-6
NEG_INF = -1e30
ROPE_BASE = 10000.0
VMEM_LIMIT = 56 * 1024 * 1024
HI = lax.Precision.HIGHEST
NT_DIMS = (((1,), (1,)), ((), ()))
LOG2E = 1.4426950408889634


def _mod_kernel(c_ref, w_ref, b_ref, o_ref):
    c = c_ref[...]
    a = c * jax.nn.sigmoid(c)
    o_ref[0] = _mm(a, w_ref[0], 3) + b_ref[0]


def mod_call(cvec, w_mod, b_mod, tn=1024):
    depth, D, N = w_mod.shape
    return pl.pallas_call(
        _mod_kernel,
        grid=(depth, N // tn),
        in_specs=[pl.BlockSpec((8, D), lambda l, j: (0, 0)),
                  pl.BlockSpec((1, D, tn), lambda l, j: (l, 0, j)),
                  pl.BlockSpec((1, 1, tn), lambda l, j: (l, 0, j))],
        out_specs=pl.BlockSpec((1, 8, tn), lambda l, j: (l, 0, j)),
        out_shape=jax.ShapeDtypeStruct((depth, 8, N), jnp.float32),
        name="mod",
    )(cvec, w_mod, b_mod.reshape(depth, 1, N))


W_AB0 = 2560


def _wpack_kernel(w_ref, wa_ref, wb_ref, wab_ref):
    w = w_ref[0]
    wa_ref[0] = w[:, :W_AB0].astype(wa_ref.dtype)
    wb_ref[0] = w[:, W_AB0 + 16:].astype(wb_ref.dtype)
    wab_ref[0] = jnp.concatenate([w[:, W_AB0:W_AB0 + 16], jnp.zeros((w.shape[0], 112), w.dtype)], 1).astype(wab_ref.dtype)


def wpack_call(w_in, tr=128):
    depth, D, n = w_in.shape
    nb = n - W_AB0 - 16
    bf16 = jnp.bfloat16
    return pl.pallas_call(
        _wpack_kernel,
        grid=(depth, D // tr),
        in_specs=[pl.BlockSpec((1, tr, n), lambda l, i: (l, i, 0))],
        out_specs=[pl.BlockSpec((1, tr, W_AB0), lambda l, i: (l, i, 0)),
                   pl.BlockSpec((1, tr, nb), lambda l, i: (l, i, 0)),
                   pl.BlockSpec((1, tr, 128), lambda l, i: (l, i, 0))],
        out_shape=[jax.ShapeDtypeStruct((depth, D, W_AB0), bf16), jax.ShapeDtypeStruct((depth, D, nb), bf16),
                   jax.ShapeDtypeStruct((depth, D, 128), bf16)],
        name="wpack",
    )(w_in)


def _stream_specs(n_lat_tiles, ctx_tile0, tm, width):
    return [pl.BlockSpec((tm, width), lambda i: (jnp.minimum(i, n_lat_tiles - 1), 0)),
            pl.BlockSpec((tm, width), lambda i: (ctx_tile0 + jnp.maximum(i - n_lat_tiles, 0), 0))]


def _inproj_kernel(xl_ref, xc_ref, m_ref, g_ref, w_ref, wb_ref, wab_ref, cos_ref, sin_ref, g64_ref, g32_ref, gains_ref,
                   naq_ref, nak_ref, nav_ref, daq_ref, dak_ref, davx_ref, gdn_ref, z_ref, f_ref, gate_ref, ab_ref,
                   *, tiles_per_batch, n_lat_tiles, n_batch):
    i = pl.program_id(0)
    r = jnp.where(i < n_lat_tiles, i // tiles_per_batch, n_batch)
    x = jnp.where(i < n_lat_tiles, xl_ref[...], xc_ref[...])
    y = x * lax.rsqrt(jnp.mean(x * x, axis=-1, keepdims=True) + RMS_EPS) * g_ref[...]
    sh = m_ref[pl.ds(r, 1), 0:D_MODEL]
    sc = m_ref[pl.ds(r, 1), D_MODEL:2 * D_MODEL]
    h = (y * (1.0 + sc) + sh).astype(jnp.bfloat16)
    proj = lambda a, b: jnp.dot(h, w_ref[0, :, a:b], preferred_element_type=jnp.float32)
    proj_b = lambda a, b: jnp.dot(h, wb_ref[0, :, a:b], preferred_element_type=jnp.float32)
    bf16 = jnp.bfloat16

    g64 = g64_ref[...]
    naq_ref[...] = (_group_rms(proj(0, 256), g64, 64, gains_ref[0:1, :]) * (64 ** -0.5)).astype(bf16)
    nak_ref[...] = _group_rms(proj(256, 512), g64, 64, gains_ref[1:2, :]).astype(bf16)
    nav_ref[...] = proj(512, 768).astype(bf16)

    g32 = g32_ref[...]
    cos = cos_ref[...]
    sin = sin_ref[...]
    lane = lax.broadcasted_iota(jnp.int32, (1, 256), 1)
    first = (lane % 16) < 8

    def rope(t):
        swapped = jnp.where(first, pltpu.roll(t, 256 - 8, 1), pltpu.roll(t, 8, 1))
        return t * cos + swapped * sin

    q = rope(_group_rms(proj(768, 1024), g32, 32, gains_ref[2:3, :]))
    k = rope(_group_rms(proj(1024, 1280), g32, 32, gains_ref[3:4, :]))
    daq_ref[...] = (q * (32 ** -0.5 * LOG2E)).astype(bf16)
    dak_ref[...] = k.astype(bf16)
    v = proj(1280, 1536).astype(bf16)
    one_col = (lax.broadcasted_iota(jnp.int32, (v.shape[0], 64), 1) == 0).astype(bf16)
    davx_ref[...] = jnp.concatenate(
        [piece for hd in range(4) for piece in (v[:, hd * 64:(hd + 1) * 64], one_col)], -1)

    gdn_ref[...] = proj(1536, 2304)
    z_ref[...] = proj(2304, 2560).astype(bf16)
    f_ref[...] = proj_b(0, 256).astype(bf16)
    for kk in range(4):
        gate_ref[:, kk * 1024:(kk + 1) * 1024] = jax.nn.sigmoid(
            proj_b(256 + kk * 1024, 256 + (kk + 1) * 1024)).astype(bf16)
    ab_ref[...] = jnp.dot(h, wab_ref[0], preferred_element_type=jnp.float32)


def inproj_call(Xl, Xc, ctx_row0, mod_l, g, w_a, w_b, w_ab, layer, cos, sin, na_qg, na_kg, da_qg, da_kg,
                *, n_batch, seq, ctx_len, tm=256):
    D = Xl.shape[1]
    NT = n_batch * (seq + ctx_len)
    tpb = seq // tm
    n_lat_tiles = n_batch * tpb
    kern = functools.partial(_inproj_kernel, tiles_per_batch=tpb, n_lat_tiles=n_lat_tiles, n_batch=n_batch)
    bf16, f32 = jnp.bfloat16, jnp.float32
    widths = [(256, bf16)] * 5 + [(512, bf16), (768, f32), (256, bf16), (256, bf16), (4096, bf16), (128, f32)]
    gains = jnp.stack([jnp.tile(na_qg, 4), jnp.tile(na_kg, 4), jnp.tile(da_qg, 8), jnp.tile(da_kg, 8)]
                      + [jnp.zeros((256,), f32)] * 4, 0)
    g64, g32 = _group_ones(256, 64), _group_ones(256, 32)
    tab = pl.BlockSpec((tm, 256), lambda i: (jnp.where(i < n_lat_tiles, i % tpb, tpb), 0))
    full = lambda a: pl.BlockSpec(a.shape, lambda i: (0,) * a.ndim)
    wblk = lambda a: pl.BlockSpec((1,) + a.shape[1:], lambda i: (layer, 0, 0))
    return pl.pallas_call(
        kern,
        grid=(NT // tm,),
        in_specs=_stream_specs(n_lat_tiles, ctx_row0 // tm, tm, D) + [
            full(mod_l), pl.BlockSpec((1, D), lambda i: (0, 0)), wblk(w_a), wblk(w_b), wblk(w_ab), tab, tab,
            full(g64), full(g32), full(gains)],
        out_specs=[pl.BlockSpec((tm, w), lambda i: (i, 0)) for w, _ in widths],
        out_shape=[jax.ShapeDtypeStruct((NT, w), dt) for w, dt in widths],
        compiler_params=pltpu.CompilerParams(vmem_limit_bytes=VMEM_LIMIT),
        name="inproj",
    )(Xl, Xc, mod_l, g.reshape(1, D), w_a, w_b, w_ab, cos, sin, g64, g32, gains)


def _group_ones(width, group):
    i = np.arange(width)
    return jnp.asarray((i[:, None] // group == i[None, :] // group).astype(np.float32), jnp.bfloat16)


def _group_sum(xx, gmat):
    hi, lo = _split_bf16(xx)
    return (jnp.dot(hi, gmat, preferred_element_type=jnp.float32)
            + jnp.dot(lo, gmat, preferred_element_type=jnp.float32))


def _group_rms(x, gmat, group, gain):
    return x * lax.rsqrt(_group_sum(x * x, gmat) * (1.0 / group) + RMS_EPS) * gain


def rope_tables(seq, tm):
    nf = 8
    t = jnp.arange(seq)
    rows = (t // GRID_W).astype(jnp.float32)
    cols = (t % GRID_W).astype(jnp.float32)
    freqs = ROPE_BASE ** (-jnp.arange(nf, dtype=jnp.float32) / nf)
    d = np.arange(32)
    f_idx = d % 8
    use_col = (d // 16) == 1
    ang = jnp.where(use_col[None, :], cols[:, None], rows[:, None]) * freqs[f_idx][None, :]
    sign = np.where((d % 16) < 8, -1.0, 1.0).astype(np.float32)
    cos = jnp.tile(jnp.cos(ang), (1, 8))
    sin = jnp.tile(jnp.sin(ang) * sign[None, :], (1, 8))
    cos = jnp.concatenate([cos, jnp.ones((tm, 256), jnp.float32)], 0)
    sin = jnp.concatenate([sin, jnp.zeros((tm, 256), jnp.float32)], 0)
    return cos, sin


NA_RB = 4
NA_UR = 11


def na_bias_tables(rpb, rows):
    reps = [(0, 0), (4, 0), (rows - NA_RB, rows - NA_UR)]
    qc = np.arange(GRID_W)[:, None]
    kc = np.arange(GRID_W)[None, :]
    ws = np.clip(qc - 8, 0, GRID_W - 16)
    vcol = (kc >= ws) & (kc < ws + 16)
    rel_c = np.clip(kc - qc + 15, 0, 30)
    sel_c = (rel_c[..., None] == np.arange(31)).astype(np.float32)
    sel_r, vrow = [], []
    for r0, u0 in reps:
        r = r0 + np.arange(NA_RB)[:, None]
        krow = u0 + np.arange(NA_UR)[None, :]
        start = np.clip(r - 4, 0, rows - 8)
        vrow.append((krow >= start) & (krow < start + 8))
        sel_r.append((np.clip(krow - r + 7, 0, 14)[..., None] == np.arange(15)).astype(np.float32))
    sel_r, vrow = np.stack(sel_r), np.stack(vrow)
    b = jnp.einsum('taui,hij,qkj->thauqk', jnp.asarray(sel_r), rpb.astype(jnp.float32), jnp.asarray(sel_c),
                   precision=HI)
    valid = vrow[:, None, :, :, None, None] & vcol[None, None, None, None, :, :]
    return jnp.where(valid, b, NEG_INF).astype(jnp.bfloat16)


def _na_kernel(q_ref, k_ref, v_ref, kc_ref, vc_ref, bias_ref, o_ref, bias_s, *, rows):
    i = pl.program_id(1)

    @pl.when((i <= 1) | (i == rows // NA_RB - 1))
    def _():
        for h in range(4):
            for a in range(NA_RB):
                for u in range(NA_UR):
                    bias_s[h, a * GRID_W:(a + 1) * GRID_W, u * GRID_W:(u + 1) * GRID_W] = (
                        bias_ref[0, h, a, u].astype(jnp.float32))

    u0 = jnp.clip(i * NA_RB - 4, 0, rows - NA_UR)
    off = pl.multiple_of(u0 * GRID_W, GRID_W)
    nkw = NA_UR * GRID_W
    kwin = k_ref[pl.ds(off, nkw), :]
    vwin = v_ref[pl.ds(off, nkw), :]
    q = q_ref[...]
    kc = kc_ref[...]
    vc = vc_ref[...]
    outs = []
    for h in range(4):
        sl = slice(h * 64, (h + 1) * 64)
        qh = q[:, sl]
        s_loc = lax.dot_general(qh, kwin[:, sl], NT_DIMS, preferred_element_type=jnp.float32) + bias_s[h]
        s_ctx = lax.dot_general(qh, kc[:, sl], NT_DIMS, preferred_element_type=jnp.float32)
        m = jnp.maximum(jnp.max(s_loc, -1, keepdims=True), jnp.max(s_ctx, -1, keepdims=True))
        p_loc = jnp.exp(s_loc - m)
        p_ctx = jnp.exp(s_ctx - m)
        l = jnp.sum(p_loc, -1, keepdims=True) + jnp.sum(p_ctx, -1, keepdims=True)
        o = (jnp.dot(p_loc.astype(jnp.bfloat16), vwin[:, sl], preferred_element_type=jnp.float32)
             + jnp.dot(p_ctx.astype(jnp.bfloat16), vc[:, sl], preferred_element_type=jnp.float32))
        outs.append(o / l)
    o_ref[...] = jnp.concatenate(outs, -1).astype(o_ref.dtype)


def na_call(NAQ, NAK, NAV, bias, *, n_batch, seq, ctx_len):
    rows = seq // GRID_W
    nblk = rows // NA_RB
    tq = NA_RB * GRID_W
    cb = n_batch * seq // ctx_len
    kern = functools.partial(_na_kernel, rows=rows)
    tsel = lambda b, i: (jnp.where(i == 0, 0, jnp.where(i == nblk - 1, 2, 1)), 0, 0, 0, 0, 0)
    return pl.pallas_call(
        kern,
        grid=(n_batch, nblk),
        in_specs=[pl.BlockSpec((tq, 256), lambda b, i: (b * nblk + i, 0)),
                  pl.BlockSpec((seq, 256), lambda b, i: (b, 0)),
                  pl.BlockSpec((seq, 256), lambda b, i: (b, 0)),
                  pl.BlockSpec((ctx_len, 256), lambda b, i: (cb + b, 0)),
                  pl.BlockSpec((ctx_len, 256), lambda b, i: (cb + b, 0)),
                  pl.BlockSpec((1,) + bias.shape[1:], tsel)],
        out_specs=pl.BlockSpec((tq, 256), lambda b, i: (b * nblk + i, 0)),
        out_shape=jax.ShapeDtypeStruct((n_batch * seq, 256), jnp.bfloat16),
        scratch_shapes=[pltpu.VMEM((4, tq, NA_UR * GRID_W), jnp.float32)],
        compiler_params=pltpu.CompilerParams(vmem_limit_bytes=VMEM_LIMIT),
        name="na_attn",
    )(NAQ, NAK, NAV, NAK, NAV, bias)


def _stack_masked_q(q):
    lane = lax.broadcasted_iota(jnp.int32, (1, 256), 1)
    return jnp.concatenate([jnp.where((lane // 32) == hm, q, jnp.zeros_like(q)) for hm in range(8)], 0)


def _da_finish(acc, lam, gain, g64, post_scale):
    tq = acc.shape[0] // 8
    norm = acc / jnp.broadcast_to(acc[:, 64:65], acc.shape)
    lane = lax.broadcasted_iota(jnp.int32, (1, 128), 1)
    halves = []
    for hp in range(2):
        d = [norm[(4 * hp + 2 * j) * tq:(4 * hp + 2 * j + 1) * tq]
             - lam * norm[(4 * hp + 2 * j + 1) * tq:(4 * hp + 2 * j + 2) * tq] for j in range(2)]
        halves.append(jnp.where(lane < 64, d[0], pltpu.roll(d[1], 64, 1)))
    o = jnp.concatenate(halves, -1)
    return _group_rms(o, g64, 64, gain) * post_scale


def _da_tile(qs, k, vx, m, acc, tq):
    tk = k.shape[0]
    s = lax.dot_general(qs, k, NT_DIMS, preferred_element_type=jnp.float32)
    m_new = jnp.maximum(m, jnp.max(s, -1, keepdims=True))
    alpha = jnp.exp2(m - m_new)
    p = jnp.exp2(s - pltpu.repeat(m_new, tk // 128, axis=1)).astype(jnp.bfloat16)
    pv = jnp.concatenate(
        [jnp.dot(p[2 * h * tq:(2 * h + 2) * tq], vx[:, h * 128:(h + 1) * 128], preferred_element_type=jnp.float32)
         for h in range(4)], 0)
    return m_new, alpha * acc + pv


def _da_kernel(lam_ref, q_ref, k_ref, vx_ref, kc_ref, vxc_ref, gain_ref, g64_ref, o_ref, m_ref, acc_ref,
               *, tk, n_kv, post_scale):
    tq = q_ref.shape[0]
    qs = _stack_masked_q(q_ref[...])
    m_ref[...] = jnp.full(m_ref.shape, -jnp.inf, jnp.float32)
    acc_ref[...] = jnp.zeros(acc_ref.shape, jnp.float32)

    def body(j, carry):
        for u in range(2):
            off = pl.multiple_of((2 * j + u) * tk, tk)
            m, acc = _da_tile(qs, k_ref[pl.ds(off, tk), :], vx_ref[pl.ds(off, tk), :], m_ref[...], acc_ref[...], tq)
            m_ref[...] = m
            acc_ref[...] = acc
        return carry

    lax.fori_loop(0, n_kv // 2 - 1, body, 0, unroll=True)
    off = (n_kv - 2) * tk
    m, acc = _da_tile(qs, k_ref[off:off + tk, :], vx_ref[off:off + tk, :], m_ref[...], acc_ref[...], tq)
    k_last = jnp.concatenate([k_ref[off + tk:off + 2 * tk, :], kc_ref[...]], 0)
    vx_last = jnp.concatenate([vx_ref[off + tk:off + 2 * tk, :], vxc_ref[...]], 0)
    m, acc = _da_tile(qs, k_last, vx_last, m, acc, tq)
    o_ref[...] = _da_finish(acc, lam_ref[0], gain_ref[...], g64_ref[...], post_scale).astype(o_ref.dtype)


def da_call(lam, DAQ, DAK, DAVX, subln_g, *, n_batch, seq, ctx_len, lam_init, tq=256, tk=1024):
    tk = min(tk, seq // 2)
    nq = seq // tq
    cb = n_batch * seq // ctx_len
    g64 = _group_ones(256, 64)
    gain = jnp.tile(subln_g, 4).reshape(1, 256)
    n_kv = seq // tk
    assert n_kv % 2 == 0
    kern = functools.partial(_da_kernel, tk=tk, n_kv=n_kv, post_scale=1.0 - lam_init)
    return pl.pallas_call(
        kern,
        grid=(n_batch, nq),
        in_specs=[pl.BlockSpec(memory_space=pltpu.SMEM),
                  pl.BlockSpec((tq, 256), lambda b, i: (b * nq + i, 0)),
                  pl.BlockSpec((seq, 256), lambda b, i: (b, 0)),
                  pl.BlockSpec((seq, 512), lambda b, i: (b, 0)),
                  pl.BlockSpec((ctx_len, 256), lambda b, i: (cb + b, 0)),
                  pl.BlockSpec((ctx_len, 512), lambda b, i: (cb + b, 0)),
                  pl.BlockSpec((1, 256), lambda b, i: (0, 0)),
                  pl.BlockSpec((256, 256), lambda b, i: (0, 0))],
        out_specs=pl.BlockSpec((tq, 256), lambda b, i: (b * nq + i, 0)),
        out_shape=jax.ShapeDtypeStruct((n_batch * seq, 256), jnp.bfloat16),
        scratch_shapes=[pltpu.VMEM((8 * tq, 128), jnp.float32), pltpu.VMEM((8 * tq, 128), jnp.float32)],
        compiler_params=pltpu.CompilerParams(vmem_limit_bytes=VMEM_LIMIT),
        name="da_attn",
    )(lam.reshape(1), DAQ, DAK, DAVX, DAK, DAVX, gain, g64)


def _ctx_kernel(lam_ref, naq_ref, nak_ref, nav_ref, daq_ref, dak_ref, dav_ref, gain_ref, g64_ref, ya_ref, yd_ref,
                *, post_scale):
    q = naq_ref[...]
    k = nak_ref[...]
    v = nav_ref[...]
    outs = []
    for h in range(4):
        sl = slice(h * 64, (h + 1) * 64)
        s = lax.dot_general(q[:, sl], k[:, sl], NT_DIMS, preferred_element_type=jnp.float32)
        p = jnp.exp(s - jnp.max(s, -1, keepdims=True))
        o = jnp.dot(p.astype(jnp.bfloat16), v[:, sl], preferred_element_type=jnp.float32)
        outs.append(o / jnp.sum(p, -1, keepdims=True))
    ya_ref[...] = jnp.concatenate(outs, -1).astype(ya_ref.dtype)
    tq = daq_ref.shape[0]
    qs = _stack_masked_q(daq_ref[...])
    m0 = jnp.full((8 * tq, 128), -jnp.inf, jnp.float32)
    m, acc = _da_tile(qs, dak_ref[...], dav_ref[...], m0, jnp.zeros((8 * tq, 128), jnp.float32), tq)
    yd_ref[...] = _da_finish(acc, lam_ref[0], gain_ref[...], g64_ref[...], post_scale).astype(yd_ref.dtype)


def ctx_attn_call(lam, NAQ, NAK, NAV, DAQ, DAK, DAVX, subln_g, *, n_batch, seq, ctx_len, lam_init):
    cb = n_batch * seq // ctx_len
    g64 = _group_ones(256, 64)
    gain = jnp.tile(subln_g, 4).reshape(1, 256)
    blk = lambda col: pl.BlockSpec((ctx_len, 256), lambda b: (cb + b, col))
    return pl.pallas_call(
        functools.partial(_ctx_kernel, post_scale=1.0 - lam_init),
        grid=(n_batch,),
        in_specs=[pl.BlockSpec(memory_space=pltpu.SMEM), blk(0), blk(0), blk(0), blk(0), blk(0),
                  pl.BlockSpec((ctx_len, 512), lambda b: (cb + b, 0)),
                  pl.BlockSpec((1, 256), lambda b: (0, 0)), pl.BlockSpec((256, 256), lambda b: (0, 0))],
        out_specs=[pl.BlockSpec((ctx_len, 256), lambda b: (b, 0))] * 2,
        out_shape=[jax.ShapeDtypeStruct((n_batch * ctx_len, 256), jnp.bfloat16)] * 2,
        name="ctx_attn",
    )(lam.reshape(1), NAQ, NAK, NAV, DAQ, DAK, DAVX, gain, g64)


def _split_bf16(a):
    hi = a.astype(jnp.bfloat16)
    return hi, (a - hi.astype(jnp.float32)).astype(jnp.bfloat16)


def _route(logits_t, bias_col):
    per_group = N_EXPERTS // N_GROUPS
    scores = jax.nn.sigmoid(logits_t)
    sel_all = scores + bias_col
    sel = [sel_all[e:e + 1, :] for e in range(N_EXPERTS)]
    top2 = []
    for e in range(N_EXPERTS):
        g0 = e // per_group * per_group
        rank = jnp.zeros_like(sel[e])
        for o in range(g0, g0 + per_group):
            if o != e:
                beats = (sel[o] > sel[e]) | (sel[o] == sel[e]) if o < e else (sel[o] > sel[e])
                rank = rank + beats.astype(jnp.float32)
        top2.append(rank < 1.5)
    gs = []
    for g in range(N_GROUPS):
        acc = jnp.zeros_like(sel[0])
        for e in range(g * per_group, (g + 1) * per_group):
            acc = acc + jnp.where(top2[e], sel[e], 0.0)
        gs.append(acc)
    rows = []
    for g in range(N_GROUPS):
        beaten = jnp.zeros_like(sel[0])
        for o in range(N_GROUPS):
            if o != g:
                b = (gs[o] >= gs[g]) if o < g else (gs[o] > gs[g])
                beaten = beaten + b.astype(jnp.float32)
        best = beaten < 0.5
        for e in range(g * per_group, (g + 1) * per_group):
            rows.append(jnp.where(best & top2[e], scores[e:e + 1, :], 0.0))
    w = jnp.concatenate(rows, 0)
    return w / jnp.sum(w, axis=0, keepdims=True)


def _merge_kernel(xl_ref, xc_ref, yal_ref, yac_ref, ydl_ref, ydc_ref, yfl_ref, yfc_ref, of_ref, ob_ref, z_ref, gate_ref,
                  m_ref, g2_ref, og_ref, g64_ref, wb_ref, wo_ref, wr_ref, rb_ref, xo_ref, h2_ref, rw_ref, *, tiles_per_batch, n_lat_tiles, n_batch):
    i = pl.program_id(0)
    r = jnp.where(i < n_lat_tiles, i // tiles_per_batch, n_batch)
    z = z_ref[...].astype(jnp.float32)
    yg = (_group_rms(of_ref[...] + ob_ref[...], g64_ref[...], GDN_DIM, og_ref[...]) * (z * jax.nn.sigmoid(z)))
    is_lat = i < n_lat_tiles
    pick = lambda lat_ref, ctx_ref: jnp.where(is_lat, lat_ref[...], ctx_ref[...])
    branches = (pick(yal_ref, yac_ref), pick(ydl_ref, ydc_ref), yg.astype(jnp.bfloat16),
                pick(yfl_ref, yfc_ref).astype(jnp.bfloat16))
    mix = None
    for b, yb in enumerate(branches):
        proj = jnp.dot(yb, wb_ref[b], preferred_element_type=jnp.float32)
        term = gate_ref[:, b * D_MODEL:(b + 1) * D_MODEL].astype(jnp.float32) * proj
        mix = term if mix is None else mix + term
    y = jnp.dot(mix.astype(jnp.bfloat16), wo_ref[...], preferred_element_type=jnp.float32)
    mrow = lambda k: m_ref[pl.ds(r, 1), k * D_MODEL:(k + 1) * D_MODEL]
    x = jnp.where(i < n_lat_tiles, xl_ref[...], xc_ref[...]) + mrow(2) * y
    xo_ref[...] = x
    h2 = (x * lax.rsqrt(jnp.mean(x * x, axis=-1, keepdims=True) + RMS_EPS) * g2_ref[...]) * (1.0 + mrow(4)) + mrow(3)
    h2_ref[...] = h2.astype(h2_ref.dtype)
    h_hi, h_lo = _split_bf16(h2)
    w_hi, w_lo = _split_bf16(wr_ref[...])
    nt = lambda a, b: lax.dot_general(a, b, NT_DIMS, preferred_element_type=jnp.float32)
    logits_t = nt(w_hi, h_hi) + nt(w_hi, h_lo) + nt(w_lo, h_hi)
    w_t = _route(logits_t, rb_ref[...])
    pad = jnp.zeros((128 - N_EXPERTS, w_t.shape[1]), jnp.float32)
    rw_ref[...] = jnp.transpose(jnp.concatenate([w_t, pad], 0))


def merge_call(Xl, Xc, ctx_row0, Ya, Yd, Yf, Of, Ob, P_z, P_gate, mod_l, g2n, onorm_g, w_branch, w_out, w_router,
               router_bias, *, n_tok, n_batch, seq, tm=256):
    D = Xl.shape[1]
    n_lat_tiles = n_batch * seq // tm
    kern = functools.partial(_merge_kernel, tiles_per_batch=seq // tm, n_lat_tiles=n_lat_tiles, n_batch=n_batch)
    row = lambda w: pl.BlockSpec((tm, w), lambda i: (i, 0))
    full = lambda a: pl.BlockSpec(a.shape, lambda i: (0,) * a.ndim)
    wr_t = jnp.transpose(w_router)
    rb = router_bias.reshape(N_EXPERTS, 1)
    g2 = g2n.reshape(1, D)
    og = jnp.tile(onorm_g, GDN_HEADS).reshape(1, BRANCH_W)
    g64 = _group_ones(BRANCH_W, GDN_DIM)
    return pl.pallas_call(
        kern,
        grid=(n_tok // tm,),
        in_specs=_stream_specs(n_lat_tiles, ctx_row0 // tm, tm, D) + 3 * _stream_specs(n_lat_tiles, 0, tm, 256) + [
                  row(256), row(256), row(256), row(4 * D), full(mod_l),
                  full(g2), full(og), full(g64), full(w_branch), full(w_out), full(wr_t), full(rb)],
        out_specs=[row(D), row(D), row(128)],
        out_shape=[jax.ShapeDtypeStruct((n_tok, D), jnp.float32), jax.ShapeDtypeStruct((n_tok, D), jnp.bfloat16),
                   jax.ShapeDtypeStruct((n_tok, 128), jnp.float32)],
        compiler_params=pltpu.CompilerParams(vmem_limit_bytes=VMEM_LIMIT),
        name="merge",
    )(Xl, Xc, *Ya, *Yd, *Yf, Of, Ob, P_z, P_gate, mod_l, g2, og, g64, w_branch, w_out, wr_t, rb)


MOE_TILE = 1024
MOE_WIN = 320
MOE_WIN_CTX = 176
MOE_EPS = 4


def _moe_sorted_kernel(x_ref, h_ref, rw_ref, m_ref, wg_ref, wu_ref, wd_ref, o_ref, xs_ref, ws_ref, ys_ref, dest_ref,
                       seg_ref, *, win, mod_row_fn):
    T = h_ref.shape[0]
    per_group = N_EXPERTS // N_GROUPS
    i = pl.program_id(0)
    step = pl.program_id(1)
    lane = lax.broadcasted_iota(jnp.int32, (1, 128), 1)
    f32, bf16 = jnp.float32, jnp.bfloat16

    @pl.when(step == 0)
    def _():
        rw = rw_ref[...]
        r128 = lax.broadcasted_iota(jnp.int32, (128, 128), 0)
        c128 = lax.broadcasted_iota(jnp.int32, (128, 128), 1)
        e2g = ((r128 // per_group == c128) & (r128 < N_EXPERTS)).astype(bf16)
        og = (jnp.dot((rw > 0.0).astype(bf16), e2g, preferred_element_type=f32) > 0.5).astype(f32)
        rt = lax.broadcasted_iota(jnp.int32, (T, T), 0)
        ct = lax.broadcasted_iota(jnp.int32, (T, T), 1)
        earlier = jnp.dot((rt > ct).astype(bf16), og.astype(bf16), preferred_element_type=f32)
        cnt_row = jnp.sum(og, axis=0, keepdims=True)
        start_row = jnp.dot(jnp.broadcast_to(cnt_row, (8, 128)), (r128 < c128).astype(f32),
                            preferred_element_type=f32, precision=HI)[0:1]
        dest = jnp.sum(og * (earlier + start_row), axis=-1, keepdims=True)
        dest_ref[...] = jnp.broadcast_to(dest, (T, 128))
        dest_row = jnp.transpose(dest_ref[...])[0:1, :]
        perm = (dest_row == rt.astype(f32)).astype(bf16)
        xs_ref[...] = jnp.dot(perm, h_ref[...], preferred_element_type=f32).astype(bf16)
        ws_ref[...] = sum(jnp.dot(perm, piece, preferred_element_type=f32) for piece in _split_bf16(rw))
        ys_ref[...] = jnp.zeros(ys_ref.shape, f32)
        for g in range(N_GROUPS):
            seg_ref[g] = jnp.sum(jnp.where(lane == g, start_row, 0.0)).astype(jnp.int32)
            seg_ref[N_GROUPS + g] = jnp.sum(jnp.where(lane == g, cnt_row, 0.0)).astype(jnp.int32)

    g = (step * MOE_EPS) // per_group
    start = seg_ref[g]
    end = start + seg_ref[N_GROUPS + g]
    s0 = (start // 16) * 16
    n_win = (end - s0 + win - 1) // win

    def window(w, carry):
        lo = s0 + w * win
        off = pl.multiple_of(jnp.minimum(lo, T - win), 16)
        xw = xs_ref[pl.ds(off, win), :]
        ww = ws_ref[pl.ds(off, win), :]
        row = off + lax.broadcasted_iota(jnp.int32, (win, 1), 0)
        y = None
        for j in range(MOE_EPS):
            w_col = jnp.sum(jnp.where(lane == step * MOE_EPS + j, ww, 0.0), axis=-1, keepdims=True)
            w_col = jnp.where(row >= lo, w_col, 0.0)
            a = jnp.dot(xw, wg_ref[0, j], preferred_element_type=f32)
            u = jnp.dot(xw, wu_ref[0, j], preferred_element_type=f32)
            act = (a * jax.nn.sigmoid(a) * u * w_col).astype(bf16)
            yj = jnp.dot(act, wd_ref[0, j], preferred_element_type=f32)
            y = yj if y is None else y + yj
        ys_ref[pl.ds(off, win), :] += y
        return carry

    lax.fori_loop(0, n_win, window, 0)

    @pl.when(step == N_EXPERTS // MOE_EPS - 1)
    def _():
        ct = lax.broadcasted_iota(jnp.int32, (T, T), 1)
        unperm = (pltpu.repeat(dest_ref[...], T // 128, axis=1) == ct.astype(f32)).astype(bf16)
        f = jnp.dot(unperm, ys_ref[...].astype(bf16), preferred_element_type=f32)
        g2 = m_ref[pl.ds(mod_row_fn(i), 1), 5 * D_MODEL:6 * D_MODEL]
        o_ref[...] = x_ref[...] + g2 * f


def moe_sorted_call(Xmid, H2, RW, mod_l, w_gate, w_up, w_down, *, layer, row0, n_rows, tile, win, mod_row_fn):
    D = Xmid.shape[1]
    F = w_gate.shape[-1]
    b0 = row0 // tile
    tok = lambda w: pl.BlockSpec((tile, w), lambda i, e: (b0 + i, 0))
    kern = functools.partial(_moe_sorted_kernel, win=win, mod_row_fn=mod_row_fn)
    return pl.pallas_call(
        kern,
        grid=(n_rows // tile, N_EXPERTS // MOE_EPS),
        in_specs=[tok(D), tok(D), tok(128),
                  pl.BlockSpec(mod_l.shape, lambda i, e: (0, 0)),
                  pl.BlockSpec((1, MOE_EPS, D, F), lambda i, e: (layer, e, 0, 0)),
                  pl.BlockSpec((1, MOE_EPS, D, F), lambda i, e: (layer, e, 0, 0)),
                  pl.BlockSpec((1, MOE_EPS, F, D), lambda i, e: (layer, e, 0, 0))],
        out_specs=pl.BlockSpec((tile, D), lambda i, e: (i, 0)),
        out_shape=jax.ShapeDtypeStruct((n_rows, D), jnp.float32),
        scratch_shapes=[pltpu.VMEM((tile, D), jnp.bfloat16), pltpu.VMEM((tile, 128), jnp.float32),
                        pltpu.VMEM((tile, D), jnp.float32), pltpu.VMEM((tile, 128), jnp.float32),
                        pltpu.SMEM((2 * N_GROUPS,), jnp.int32)],
        compiler_params=pltpu.CompilerParams(vmem_limit_bytes=VMEM_LIMIT),
        name="moe_sorted",
    )(Xmid, H2, RW, mod_l, w_gate, w_up, w_down)


GDN_TILE = 256
GDN_CPT = GDN_TILE // GDN_CHUNK
GDN_LOCKSTEP = 2


def _mm(a, b, passes=1, dims=None):
    if dims is None:
        dot = lambda x, y: jnp.dot(x, y, preferred_element_type=jnp.float32)
    else:
        dot = lambda x, y: lax.dot_general(x, y, dims, preferred_element_type=jnp.float32)
    if passes == 1:
        return dot(a.astype(jnp.bfloat16), b.astype(jnp.bfloat16))
    a_hi, a_lo = _split_bf16(a)
    b_hi, b_lo = _split_bf16(b)
    return dot(a_hi, b_hi) + dot(a_hi, b_lo) + dot(a_lo, b_hi)


def _stack_heads(x):
    lane = lax.broadcasted_iota(jnp.int32, (1, 256), 1)
    return jnp.concatenate([jnp.where((lane // GDN_DIM) == h, x, 0.0) for h in range(GDN_HEADS)], 0)


def _slab(x):
    return x[0:64] + x[64:128] + x[128:192] + x[192:256]


def _unit_tri_inverse(mats):
    shape = mats[0].shape
    eye = (lax.broadcasted_iota(jnp.int32, shape, 0) == lax.broadcasted_iota(jnp.int32, shape, 1)).astype(jnp.float32)
    ps = [eye - a for a in mats]
    pws = list(mats)
    for _ in range(5):
        pws = [_mm(pw, pw, 3) for pw in pws]
        ps = [p + _mm(p, pw, 3) for p, pw in zip(ps, pws)]
    return ps


def _gdn_chunk_kernel(x_ref, xp_ref, xn_ref, ab_ref, cw_ref, par_ref, g64_ref, qe_ref, ou_ref, mm_ref, nn_ref,
                      *, tiles_per_batch, n_lat_tiles):
    i = pl.program_id(0)
    is_lat = i < n_lat_tiles
    first = jnp.where(is_lat, (i % tiles_per_batch) == 0, True)
    last = jnp.where(is_lat, (i % tiles_per_batch) == tiles_per_batch - 1, True)
    xp = jnp.concatenate([jnp.where(first, 0.0, xp_ref[...]), x_ref[...], jnp.where(last, 0.0, xn_ref[...])], 0)
    y = xp[6:6 + GDN_TILE] * cw_ref[0:1, :]
    for t in range(1, 5):
        y = y + xp[6 + t:6 + t + GDN_TILE] * cw_ref[t:t + 1, :]
    y = y * jax.nn.sigmoid(y)
    g64 = g64_ref[...]
    q = y[:, 0:256]
    k = y[:, 256:512]
    v = y[:, 512:768]
    q = q * lax.rsqrt(_group_sum(q * q, g64) + RMS_EPS) * (GDN_DIM ** -0.5)
    k = k * lax.rsqrt(_group_sum(k * k, g64) + RMS_EPS)
    ab = ab_ref[...]
    lane128 = lax.broadcasted_iota(jnp.int32, (1, 128), 1)
    g_all = jnp.where(lane128 < 8, par_ref[0:1, :] * jax.nn.softplus(ab + par_ref[1:2, :]), 0.0)
    beta_all = jax.nn.sigmoid(ab)
    r64 = lax.broadcasted_iota(jnp.int32, (64, 64), 0)
    c64 = lax.broadcasted_iota(jnp.int32, (64, 64), 1)
    low = (r64 >= c64).astype(jnp.float32)
    upp = (r64 <= c64).astype(jnp.float32)
    rr = lax.broadcasted_iota(jnp.int32, (256, 256), 0)
    cc = lax.broadcasted_iota(jnp.int32, (256, 256), 1)
    same = (rr // 64) == (cc // 64)
    eye = (rr == cc).astype(jnp.float32)
    for c0 in range(0, GDN_CPT, GDN_LOCKSTEP):
        stage = []
        for c in range(c0, c0 + GDN_LOCKSTEP):
            sl = slice(c * GDN_CHUNK, (c + 1) * GDN_CHUNK)
            g_c = g_all[sl]
            gsum = jnp.where(lane128 < 4, jnp.dot(low, g_c, preferred_element_type=jnp.float32, precision=HI),
                             jnp.dot(upp, g_c, preferred_element_type=jnp.float32, precision=HI))
            gsum_t = jnp.transpose(jnp.concatenate([gsum, jnp.zeros_like(gsum)], 0))[:, 0:64]
            tot = jnp.sum(g_c, axis=0, keepdims=True)
            beta_c = beta_all[sl]
            q_sm = _stack_heads(q[sl])
            k_sm = _stack_heads(k[sl])
            v_sm = _stack_heads(v[sl])
            kk = _mm(k_sm, k_sm, 1, NT_DIMS)
            qk = _mm(q_sm, k_sm, 1, NT_DIMS)
            for d in range(2):
                col = lambda h: 4 * d + h
                cmat = jnp.concatenate([jnp.broadcast_to(gsum[:, col(h):col(h) + 1], (64, 256)) for h in range(4)], 0)
                rrow = jnp.concatenate([gsum_t[col(h):col(h) + 1, :] for h in range(4)], 1)
                bmat = jnp.concatenate([jnp.broadcast_to(beta_c[:, 8 + col(h):9 + col(h)], (64, 256))
                                        for h in range(4)], 0)
                tmat = jnp.concatenate([jnp.broadcast_to(tot[:, col(h):col(h) + 1], (64, 256)) for h in range(4)], 0)
                tri = (rr % 64 >= cc % 64) if d == 0 else (rr % 64 <= cc % 64)
                incl = same & tri
                strict = incl & (rr != cc)
                dec = jnp.where(incl, jnp.exp(jnp.where(incl, cmat - rrow, 0.0)), 0.0)
                stage.append((d, sl, q_sm, k_sm, v_sm, qk, cmat, bmat, tmat, dec,
                              jnp.where(strict, kk * bmat * dec, 0.0)))
        t_invs = _unit_tri_inverse([st[-1] for st in stage])
        n = len(stage)
        q_sms, k_sms, v_sms, qks, cmats, bmats, tmats, decs = (
            [st[f] for st in stage] for f in range(2, 10))
        egs = [jnp.exp(cm) for cm in cmats]
        us = [_mm(t_invs[j], v_sms[j] * bmats[j], 1) for j in range(n)]
        ws = [_mm(t_invs[j], k_sms[j] * bmats[j] * egs[j], 1) for j in range(n)]
        qkms = [decs[j] * qks[j] for j in range(n)]
        qes = [q_sms[j] * egs[j] - _mm(qkms[j], ws[j], 1) for j in range(n)]
        ous = [_mm(qkms[j], us[j], 1) for j in range(n)]
        kd_ts = [jnp.transpose(k_sms[j] * jnp.exp(tmats[j] - cmats[j])) for j in range(n)]
        m_bds = [eye * jnp.exp(tmats[j]) - _mm(kd_ts[j], ws[j], 1) for j in range(n)]
        n_bds = [_mm(kd_ts[j], us[j], 1) for j in range(n)]
        for j in range(n):
            d, sl = stage[j][0], stage[j][1]
            qe_ref[d, sl, :] = _slab(qes[j])
            ou_ref[d, sl, :] = _slab(ous[j])
            mm_ref[d, sl, :] = _slab(m_bds[j])
            nn_ref[d, sl, :] = _slab(n_bds[j])


def gdn_chunk_call(P_gdn, P_ab, conv_w, a_log, dt_bias, *, n_batch, seq):
    NT = P_gdn.shape[0]
    tpb = seq // GDN_TILE
    n_lat = n_batch * tpb
    n_tiles = NT // GDN_TILE
    hb = GDN_TILE // 8
    cw = jnp.zeros((8, 768), jnp.float32).at[:5].set(conv_w)
    par = jnp.zeros((8, 128), jnp.float32)
    par = par.at[0, :8].set(-jnp.exp(a_log.reshape(8))).at[1, :8].set(dt_bias.reshape(8))
    g64 = _group_ones(256, 64)
    kern = functools.partial(_gdn_chunk_kernel, tiles_per_batch=tpb, n_lat_tiles=n_lat)
    full = lambda a: pl.BlockSpec(a.shape, lambda i: (0,) * a.ndim)
    out = pl.BlockSpec((2, GDN_TILE, 256), lambda i: (0, i, 0))
    return pl.pallas_call(
        kern,
        grid=(n_tiles,),
        in_specs=[pl.BlockSpec((GDN_TILE, 768), lambda i: (i, 0)),
                  pl.BlockSpec((8, 768), lambda i: (jnp.maximum(i * hb - 1, 0), 0)),
                  pl.BlockSpec((8, 768), lambda i: (jnp.minimum((i + 1) * hb, n_tiles * hb - 1), 0)),
                  pl.BlockSpec((GDN_TILE, 128), lambda i: (i, 0)),
                  full(cw), full(par), full(g64)],
        out_specs=[out] * 4,
        out_shape=[jax.ShapeDtypeStruct((2, NT, 256), jnp.float32)] * 4,
        compiler_params=pltpu.CompilerParams(vmem_limit_bytes=VMEM_LIMIT),
        name="gdn_chunk",
    )(P_gdn, P_gdn, P_gdn, P_ab, cw, par, g64)


def _gdn_scan_kernel(qef_ref, ouf_ref, mf_ref, nf_ref, qeb_ref, oub_ref, mb_ref, nb_ref, of_ref, ob_ref, s_ref):
    t = pl.program_id(1)

    @pl.when(t == 0)
    def _():
        s_ref[...] = jnp.zeros(s_ref.shape, jnp.float32)

    def step(d, c, qe_ref, ou_ref, m_ref, n_ref, o_ref):
        sl = slice(c * GDN_CHUNK, (c + 1) * GDN_CHUNK)
        s = s_ref[d]
        o_ref[sl, :] = _slab(_mm(_stack_heads(qe_ref[0, sl, :]), s, 1)) + ou_ref[0, sl, :]
        s_ref[d] = _mm(_stack_heads(m_ref[0, sl, :]), s, 3) + _stack_heads(n_ref[0, sl, :])

    for c in range(GDN_CPT):
        step(0, c, qef_ref, ouf_ref, mf_ref, nf_ref, of_ref)
        step(1, GDN_CPT - 1 - c, qeb_ref, oub_ref, mb_ref, nb_ref, ob_ref)


def gdn_scan_call(QE, OU, MM, NN, *, n_batch, seq):
    NT = QE.shape[1]
    tpb = seq // GDN_TILE
    cblk = n_batch * tpb
    fwd = lambda b, t: jnp.where(t == 0, cblk + b, b * tpb + t - 1)
    bwd = lambda b, t: jnp.where(t == 0, cblk + b, b * tpb + tpb - t)
    spec = lambda d, f: pl.BlockSpec((1, GDN_TILE, 256), lambda b, t: (d, f(b, t), 0))
    ospec = lambda f: pl.BlockSpec((GDN_TILE, 256), lambda b, t: (f(b, t), 0))
    return pl.pallas_call(
        _gdn_scan_kernel,
        grid=(n_batch, tpb + 1),
        in_specs=[spec(0, fwd)] * 4 + [spec(1, bwd)] * 4,
        out_specs=[ospec(fwd), ospec(bwd)],
        out_shape=[jax.ShapeDtypeStruct((NT, 256), jnp.float32)] * 2,
        scratch_shapes=[pltpu.VMEM((2, 256, 256), jnp.float32)],
        compiler_params=pltpu.CompilerParams(vmem_limit_bytes=VMEM_LIMIT),
        name="gdn_scan",
    )(QE, OU, MM, NN, QE, OU, MM, NN)


def _dft_cs(n):
    a = 2.0 * np.pi * np.outer(np.arange(n), np.arange(n)) / n
    return np.cos(a), np.sin(a)


def _channel_dft():
    c, s = _dft_cs(FNET_GROUP_W)
    eye = np.eye(FNET_GROUPS)
    return np.concatenate([np.kron(eye, c), np.kron(eye, s)], 1)


FNET_SUB = 16


def _fnet_a_kernel(x_ref, cs_ref, m_ref, cphi_ref, sphi_ref, br_ref, bi_ref, *, n_rows):
    for j in range(FNET_SUB):
        sl = slice(j * BRANCH_W, (j + 1) * BRANCH_W)
        u = jnp.dot(x_ref[:, j, :], cs_ref[...], preferred_element_type=jnp.float32)
        st = jnp.concatenate([u[:, :BRANCH_W], u[:, BRANCH_W:]], 0).astype(jnp.bfloat16)
        a = jnp.dot(m_ref[...], st, preferred_element_type=jnp.float32)
        ar, ai = a[:n_rows], a[n_rows:]
        cp, sp = cphi_ref[:, sl], sphi_ref[:, sl]
        br_ref[j] = ar * cp - ai * sp
        bi_ref[j] = ar * sp + ai * cp


def _fnet_c_kernel(br_ref, bi_ref, m_ref, o_ref):
    for j in range(FNET_SUB):
        st = jnp.concatenate([br_ref[:, j, :], bi_ref[:, j, :]], 0).astype(jnp.bfloat16)
        o_ref[:, j, :] = jnp.dot(m_ref[...], st, preferred_element_type=jnp.float32)


def fourier_latent_call(P_f, *, n_batch, seq):
    rows = seq // GRID_W
    bf16, f32 = jnp.bfloat16, jnp.float32
    c1, s1 = _dft_cs(rows)
    norm = 1.0 / math.sqrt(seq * FNET_GROUP_W)
    m_a = jnp.asarray(np.block([[c1, -s1], [s1, c1]]) * norm, bf16)
    c2, s2 = _dft_cs(GRID_W)
    m_c = jnp.asarray(np.concatenate([c2, -s2], 1), bf16)
    cs = jnp.asarray(_channel_dft(), bf16)
    phi = 2.0 * np.pi * np.outer(np.arange(rows), np.arange(GRID_W)) / seq
    cphi = jnp.repeat(jnp.asarray(np.cos(phi), f32), BRANCH_W, axis=1)
    sphi = jnp.repeat(jnp.asarray(np.sin(phi), f32), BRANCH_W, axis=1)
    xv = P_f.reshape(P_f.shape[0] // GRID_W, GRID_W, BRANCH_W)
    nj = GRID_W // FNET_SUB
    full = lambda a: pl.BlockSpec(a.shape, lambda b, j: (0,) * a.ndim)
    tw = pl.BlockSpec((rows, FNET_SUB * BRANCH_W), lambda b, j: (0, j))
    mid = jax.ShapeDtypeStruct((n_batch * GRID_W, rows, BRANCH_W), f32)
    br, bi = pl.pallas_call(
        functools.partial(_fnet_a_kernel, n_rows=rows),
        grid=(n_batch, nj),
        in_specs=[pl.BlockSpec((rows, FNET_SUB, BRANCH_W), lambda b, j: (b, j, 0)), full(cs), full(m_a), tw, tw],
        out_specs=[pl.BlockSpec((FNET_SUB, rows, BRANCH_W), lambda b, j: (b * nj + j, 0, 0))] * 2,
        out_shape=[mid, mid],
        name="fnet_rows",
    )(xv, cs, m_a, cphi, sphi)
    blk = pl.BlockSpec((GRID_W, FNET_SUB, BRANCH_W), lambda b, i: (b, i, 0))
    y = pl.pallas_call(
        _fnet_c_kernel,
        grid=(n_batch, rows // FNET_SUB),
        in_specs=[blk, blk, pl.BlockSpec(m_c.shape, lambda b, i: (0, 0))],
        out_specs=blk,
        out_shape=mid,
        name="fnet_cols",
    )(br, bi, m_c)
    return y.reshape(n_batch * seq, BRANCH_W)


def _fnet_ctx_kernel(x_ref, cs_ref, m_ref, o_ref):
    u = jnp.dot(x_ref[...], cs_ref[...], preferred_element_type=jnp.float32)
    st = jnp.concatenate([u[:, :BRANCH_W], u[:, BRANCH_W:]], 0).astype(jnp.bfloat16)
    o_ref[...] = jnp.dot(m_ref[...], st, preferred_element_type=jnp.float32).astype(o_ref.dtype)


def fourier_ctx_call(P_f, *, n_batch, seq, ctx_len):
    bf16 = jnp.bfloat16
    c, s = _dft_cs(ctx_len)
    m = jnp.asarray(np.concatenate([c, -s], 1) / math.sqrt(ctx_len * FNET_GROUP_W), bf16)
    cs = jnp.asarray(_channel_dft(), bf16)
    cb = n_batch * seq // ctx_len
    return pl.pallas_call(
        _fnet_ctx_kernel,
        grid=(n_batch,),
        in_specs=[pl.BlockSpec((ctx_len, BRANCH_W), lambda b: (cb + b, 0)),
                  pl.BlockSpec(cs.shape, lambda b: (0, 0)), pl.BlockSpec(m.shape, lambda b: (0, 0))],
        out_specs=pl.BlockSpec((ctx_len, BRANCH_W), lambda b: (b, 0)),
        out_shape=jax.ShapeDtypeStruct((n_batch * ctx_len, BRANCH_W), jnp.float32),
        name="fnet_ctx",
    )(P_f, cs, m)


def kernel(x, c, ctx, c_ctx, w_mod, b_mod, norm1_g, norm2_g, w_in, na_qn_g, na_kn_g, na_rpb,
           da_qn_g, da_kn_g, da_lam_q1, da_lam_k1, da_lam_q2, da_lam_k2, da_subln_g, gdn_conv_w,
           gdn_a_log, gdn_dt_bias, gdn_onorm_g, w_branch, w_out, w_router, router_bias,
           moe_w_gate, moe_w_up, moe_w_down):
    B, S, D = x.shape
    L = ctx.shape[1]
    NL = B * S
    f32 = jnp.float32
    cvec = jnp.zeros((8, D), f32).at[:B].set(c).at[B].set(c_ctx)
    mods = mod_call(cvec, w_mod, b_mod)
    cos, sin = rope_tables(S, 256)
    bf16 = jnp.bfloat16
    Xl, Xc = x.reshape(NL, D), ctx.reshape(B * L, D)
    experts = (moe_w_gate.astype(bf16), moe_w_up.astype(bf16), moe_w_down.astype(bf16))
    w_a, w_b, w_ab = wpack_call(w_in)
    for l in range(DEPTH):
        last = l == DEPTH - 1
        lam_init = 0.8 - 0.6 * math.exp(-0.3 * l)
        NAQ, NAK, NAV, DAQ, DAK, DAVX, P_gdn, P_z, P_f, P_gate, P_ab = inproj_call(
            Xl, Xc, 0, mods[l], norm1_g[l], w_a, w_b, w_ab, l, cos, sin, na_qn_g[l], na_kn_g[l], da_qn_g[l], da_kn_g[l],
            n_batch=B, seq=S, ctx_len=L)
        bias = na_bias_tables(na_rpb[l], S // GRID_W)
        lam = (jnp.exp(jnp.sum(da_lam_q1[l] * da_lam_k1[l])) - jnp.exp(jnp.sum(da_lam_q2[l] * da_lam_k2[l]))
               + lam_init).astype(f32)
        Ya = na_call(NAQ, NAK, NAV, bias, n_batch=B, seq=S, ctx_len=L)
        Yd = da_call(lam, DAQ, DAK, DAVX, da_subln_g[l], n_batch=B, seq=S, ctx_len=L, lam_init=lam_init)
        QE, OU, MM, NN = gdn_chunk_call(P_gdn, P_ab, gdn_conv_w[l], gdn_a_log[l], gdn_dt_bias[l], n_batch=B, seq=S)
        Of, Ob = gdn_scan_call(QE, OU, MM, NN, n_batch=B, seq=S)
        Yf = fourier_latent_call(P_f, n_batch=B, seq=S)
        n_tok = NL
        Yac, Ydc, Yfc = Ya, Yd, Yf
        if not last:
            Yac, Ydc = ctx_attn_call(lam, NAQ, NAK, NAV, DAQ, DAK, DAVX, da_subln_g[l],
                                     n_batch=B, seq=S, ctx_len=L, lam_init=lam_init)
            Yfc = fourier_ctx_call(P_f, n_batch=B, seq=S, ctx_len=L)
            n_tok = NL + B * L
        Xmid, H2, RW = merge_call(Xl, Xc, 0, (Ya, Yac), (Yd, Ydc), (Yf, Yfc), Of, Ob, P_z, P_gate, mods[l], norm2_g[l], gdn_onorm_g[l],
                                  w_branch[l].astype(bf16), w_out[l].astype(bf16), w_router, router_bias,
                                  n_tok=n_tok, n_batch=B, seq=S)
        Xl = moe_sorted_call(Xmid, H2, RW, mods[l], *experts, layer=l, row0=0, n_rows=NL, tile=MOE_TILE, win=MOE_WIN,
                             mod_row_fn=lambda i: i // (S // MOE_TILE))
        if not last:
            Xc = moe_sorted_call(Xmid, H2, RW, mods[l], *experts, layer=l, row0=NL, n_rows=B * L, tile=B * L,
                                 win=MOE_WIN_CTX, mod_row_fn=lambda i: B)
    return Xl.reshape(B, S, D)
```

```python
import functools
import math
import jax
import jax.numpy as jnp
from jax import lax
import numpy as np
from jax.experimental import pallas as pl
from jax.experimental.pallas import tpu as pltpu

D_MODEL = 1024
DEPTH = 2
GRID_W = 64
N_BRANCH = 4
BRANCH_W = D_MODEL // N_BRANCH
GDN_HEADS = 4
GDN_DIM = BRANCH_W // GDN_HEADS
GDN_CHUNK = 64
FNET_GROUPS = 4
FNET_GROUP_W = BRANCH_W // FNET_GROUPS
N_EXPERTS = 16
N_GROUPS = 4
RMS_EPS = 1e-6
NEG_INF = -1e30
ROPE_BASE = 10000.0
VMEM_LIMIT = 56 * 1024 * 1024
W_AB0 = 2560
HI = lax.Precision.HIGHEST
NT_DIMS = (((1,), (1,)), ((), ()))
LOG2E = 1.4426950408889634


def _mod_kernel(c_ref, w_ref, b_ref, o_ref):
    c = c_ref[...]
    a = c * jax.nn.sigmoid(c)
    o_ref[0] = _mm(a, w_ref[0], 3) + b_ref[0]


def mod_call(cvec, w_mod, b_mod, tn=1024):
    depth, D, N = w_mod.shape
    return pl.pallas_call(
        _mod_kernel,
        grid=(depth, N // tn),
        in_specs=[pl.BlockSpec((8, D), lambda l, j: (0, 0)),
                  pl.BlockSpec((1, D, tn), lambda l, j: (l, 0, j)),
                  pl.BlockSpec((1, 1, tn), lambda l, j: (l, 0, j))],
        out_specs=pl.BlockSpec((1, 8, tn), lambda l, j: (l, 0, j)),
        out_shape=jax.ShapeDtypeStruct((depth, 8, N), jnp.float32),
        name="mod",
    )(cvec, w_mod, b_mod.reshape(depth, 1, N))


def _stream_specs(n_lat_tiles, ctx_tile0, tm, width):
    return [pl.BlockSpec((tm, width), lambda i: (jnp.minimum(i, n_lat_tiles - 1), 0)),
            pl.BlockSpec((tm, width), lambda i: (ctx_tile0 + jnp.maximum(i - n_lat_tiles, 0), 0))]


def _inproj_kernel(xl_ref, xc_ref, m_ref, g_ref, wt_ref, cos_ref, sin_ref, g64_ref, g32_ref, gains_ref,
                   naq_ref, nak_ref, nav_ref, daq_ref, dak_ref, davx_ref, gdn_ref, z_ref, f_ref, gate_ref, ab_ref,
                   *, tiles_per_batch, n_lat_tiles, n_batch):
    i = pl.program_id(0)
    r = jnp.where(i < n_lat_tiles, i // tiles_per_batch, n_batch)
    x = jnp.where(i < n_lat_tiles, xl_ref[...], xc_ref[...])
    y = x * lax.rsqrt(jnp.mean(x * x, axis=-1, keepdims=True) + RMS_EPS) * g_ref[...]
    sh = m_ref[pl.ds(r, 1), 0:D_MODEL]
    sc = m_ref[pl.ds(r, 1), D_MODEL:2 * D_MODEL]
    h = (y * (1.0 + sc) + sh).astype(jnp.bfloat16)
    proj = lambda a, b: lax.dot_general(h, wt_ref[0, a:b, :], NT_DIMS, preferred_element_type=jnp.float32)
    proj_b = lambda a, b: proj(W_AB0 + 16 + a, W_AB0 + 16 + b)
    bf16 = jnp.bfloat16

    g64 = g64_ref[...]
    naq_ref[...] = (_group_rms(proj(0, 256), g64, 64, gains_ref[0:1, :]) * (64 ** -0.5)).astype(bf16)
    nak_ref[...] = _group_rms(proj(256, 512), g64, 64, gains_ref[1:2, :]).astype(bf16)
    nav_ref[...] = proj(512, 768).astype(bf16)

    g32 = g32_ref[...]
    cos = cos_ref[...]
    sin = sin_ref[...]
    lane = lax.broadcasted_iota(jnp.int32, (1, 256), 1)
    first = (lane % 16) < 8

    def rope(t):
        swapped = jnp.where(first, pltpu.roll(t, 256 - 8, 1), pltpu.roll(t, 8, 1))
        return t * cos + swapped * sin

    q = rope(_group_rms(proj(768, 1024), g32, 32, gains_ref[2:3, :]))
    k = rope(_group_rms(proj(1024, 1280), g32, 32, gains_ref[3:4, :]))
    daq_ref[...] = (q * (32 ** -0.5 * LOG2E)).astype(bf16)
    dak_ref[...] = k.astype(bf16)
    v = proj(1280, 1536).astype(bf16)
    one_col = (lax.broadcasted_iota(jnp.int32, (v.shape[0], 64), 1) == 0).astype(bf16)
    davx_ref[...] = jnp.concatenate(
        [piece for hd in range(4) for piece in (v[:, hd * 64:(hd + 1) * 64], one_col)], -1)

    gdn_ref[...] = proj(1536, 2304)
    z_ref[...] = proj(2304, 2560).astype(bf16)
    f_ref[...] = proj_b(0, 256).astype(bf16)
    for kk in range(4):
        gate_ref[:, kk * 1024:(kk + 1) * 1024] = jax.nn.sigmoid(
            proj_b(256 + kk * 1024, 256 + (kk + 1) * 1024)).astype(bf16)
    ab_ref[...] = proj(W_AB0, W_AB0 + 128)


def inproj_call(Xl, Xc, ctx_row0, mod_l, g, w_t, layer, cos, sin, na_qg, na_kg, da_qg, da_kg,
                *, n_batch, seq, ctx_len, tm=256):
    D = Xl.shape[1]
    NT = n_batch * (seq + ctx_len)
    tpb = seq // tm
    n_lat_tiles = n_batch * tpb
    kern = functools.partial(_inproj_kernel, tiles_per_batch=tpb, n_lat_tiles=n_lat_tiles, n_batch=n_batch)
    bf16, f32 = jnp.bfloat16, jnp.float32
    widths = [(256, bf16)] * 5 + [(512, bf16), (768, f32), (256, bf16), (256, bf16), (4096, bf16), (128, f32)]
    gains = jnp.stack([jnp.tile(na_qg, 4), jnp.tile(na_kg, 4), jnp.tile(da_qg, 8), jnp.tile(da_kg, 8)]
                      + [jnp.zeros((256,), f32)] * 4, 0)
    g64, g32 = _group_ones(256, 64), _group_ones(256, 32)
    tab = pl.BlockSpec((tm, 256), lambda i: (jnp.where(i < n_lat_tiles, i % tpb, tpb), 0))
    full = lambda a: pl.BlockSpec(a.shape, lambda i: (0,) * a.ndim)
    return pl.pallas_call(
        kern,
        grid=(NT // tm,),
        in_specs=_stream_specs(n_lat_tiles, ctx_row0 // tm, tm, D) + [
            full(mod_l), pl.BlockSpec((1, D), lambda i: (0, 0)),
            pl.BlockSpec((1,) + w_t.shape[1:], lambda i: (layer, 0, 0)), tab, tab,
            full(g64), full(g32), full(gains)],
        out_specs=[pl.BlockSpec((tm, w), lambda i: (i, 0)) for w, _ in widths],
        out_shape=[jax.ShapeDtypeStruct((NT, w), dt) for w, dt in widths],
        compiler_params=pltpu.CompilerParams(vmem_limit_bytes=VMEM_LIMIT),
        name="inproj",
    )(Xl, Xc, mod_l, g.reshape(1, D), w_t, cos, sin, g64, g32, gains)


def _group_ones(width, group):
    i = np.arange(width)
    return jnp.asarray((i[:, None] // group == i[None, :] // group).astype(np.float32), jnp.bfloat16)


def _group_sum(xx, gmat):
    hi, lo = _split_bf16(xx)
    return (jnp.dot(hi, gmat, preferred_element_type=jnp.float32)
            + jnp.dot(lo, gmat, preferred_element_type=jnp.float32))


def _group_rms(x, gmat, group, gain):
    return x * lax.rsqrt(_group_sum(x * x, gmat) * (1.0 / group) + RMS_EPS) * gain


def rope_tables(seq, tm):
    nf = 8
    t = jnp.arange(seq)
    rows = (t // GRID_W).astype(jnp.float32)
    cols = (t % GRID_W).astype(jnp.float32)
    freqs = ROPE_BASE ** (-jnp.arange(nf, dtype=jnp.float32) / nf)
    d = np.arange(32)
    f_idx = d % 8
    use_col = (d // 16) == 1
    ang = jnp.where(use_col[None, :], cols[:, None], rows[:, None]) * freqs[f_idx][None, :]
    sign = np.where((d % 16) < 8, -1.0, 1.0).astype(np.float32)
    cos = jnp.tile(jnp.cos(ang), (1, 8))
    sin = jnp.tile(jnp.sin(ang) * sign[None, :], (1, 8))
    cos = jnp.concatenate([cos, jnp.ones((tm, 256), jnp.float32)], 0)
    sin = jnp.concatenate([sin, jnp.zeros((tm, 256), jnp.float32)], 0)
    return cos, sin


NA_RB = 4
NA_UR = 11


def na_bias_tables(rpb, rows):
    reps = [(0, 0), (4, 0), (rows - NA_RB, rows - NA_UR)]
    qc = np.arange(GRID_W)[:, None]
    kc = np.arange(GRID_W)[None, :]
    ws = np.clip(qc - 8, 0, GRID_W - 16)
    vcol = (kc >= ws) & (kc < ws + 16)
    rel_c = np.clip(kc - qc + 15, 0, 30)
    sel_c = (rel_c[..., None] == np.arange(31)).astype(np.float32)
    sel_r, vrow = [], []
    for r0, u0 in reps:
        r = r0 + np.arange(NA_RB)[:, None]
        krow = u0 + np.arange(NA_UR)[None, :]
        start = np.clip(r - 4, 0, rows - 8)
        vrow.append((krow >= start) & (krow < start + 8))
        sel_r.append((np.clip(krow - r + 7, 0, 14)[..., None] == np.arange(15)).astype(np.float32))
    sel_r, vrow = np.stack(sel_r), np.stack(vrow)
    b = jnp.einsum('taui,hij,qkj->thauqk', jnp.asarray(sel_r), rpb.astype(jnp.float32), jnp.asarray(sel_c),
                   precision=HI)
    valid = vrow[:, None, :, :, None, None] & vcol[None, None, None, None, :, :]
    return jnp.where(valid, b, NEG_INF).astype(jnp.bfloat16)


def _na_kernel(q_ref, k_ref, v_ref, kc_ref, vc_ref, bias_ref, o_ref, bias_s, *, rows):
    i = pl.program_id(1)

    @pl.when((i <= 1) | (i == rows // NA_RB - 1))
    def _():
        for h in range(4):
            for a in range(NA_RB):
                for u in range(NA_UR):
                    bias_s[h, a * GRID_W:(a + 1) * GRID_W, u * GRID_W:(u + 1) * GRID_W] = (
                        bias_ref[0, h, a, u].astype(jnp.float32))

    u0 = jnp.clip(i * NA_RB - 4, 0, rows - NA_UR)
    off = pl.multiple_of(u0 * GRID_W, GRID_W)
    nkw = NA_UR * GRID_W
    kwin = k_ref[pl.ds(off, nkw), :]
    vwin = v_ref[pl.ds(off, nkw), :]
    q = q_ref[...]
    kc = kc_ref[...]
    vc = vc_ref[...]
    outs = []
    for h in range(4):
        sl = slice(h * 64, (h + 1) * 64)
        qh = q[:, sl]
        s_loc = lax.dot_general(qh, kwin[:, sl], NT_DIMS, preferred_element_type=jnp.float32) + bias_s[h]
        s_ctx = lax.dot_general(qh, kc[:, sl], NT_DIMS, preferred_element_type=jnp.float32)
        m = jnp.maximum(jnp.max(s_loc, -1, keepdims=True), jnp.max(s_ctx, -1, keepdims=True))
        p_loc = jnp.exp(s_loc - m)
        p_ctx = jnp.exp(s_ctx - m)
        l = jnp.sum(p_loc, -1, keepdims=True) + jnp.sum(p_ctx, -1, keepdims=True)
        o = (jnp.dot(p_loc.astype(jnp.bfloat16), vwin[:, sl], preferred_element_type=jnp.float32)
             + jnp.dot(p_ctx.astype(jnp.bfloat16), vc[:, sl], preferred_element_type=jnp.float32))
        outs.append(o / l)
    o_ref[...] = jnp.concatenate(outs, -1).astype(o_ref.dtype)


def na_call(NAQ, NAK, NAV, bias, *, n_batch, seq, ctx_len):
    rows = seq // GRID_W
    nblk = rows // NA_RB
    tq = NA_RB * GRID_W
    cb = n_batch * seq // ctx_len
    kern = functools.partial(_na_kernel, rows=rows)
    tsel = lambda b, i: (jnp.where(i == 0, 0, jnp.where(i == nblk - 1, 2, 1)), 0, 0, 0, 0, 0)
    return pl.pallas_call(
        kern,
        grid=(n_batch, nblk),
        in_specs=[pl.BlockSpec((tq, 256), lambda b, i: (b * nblk + i, 0)),
                  pl.BlockSpec((seq, 256), lambda b, i: (b, 0)),
                  pl.BlockSpec((seq, 256), lambda b, i: (b, 0)),
                  pl.BlockSpec((ctx_len, 256), lambda b, i: (cb + b, 0)),
                  pl.BlockSpec((ctx_len, 256), lambda b, i: (cb + b, 0)),
                  pl.BlockSpec((1,) + bias.shape[1:], tsel)],
        out_specs=pl.BlockSpec((tq, 256), lambda b, i: (b * nblk + i, 0)),
        out_shape=jax.ShapeDtypeStruct((n_batch * seq, 256), jnp.bfloat16),
        scratch_shapes=[pltpu.VMEM((4, tq, NA_UR * GRID_W), jnp.float32)],
        compiler_params=pltpu.CompilerParams(vmem_limit_bytes=VMEM_LIMIT),
        name="na_attn",
    )(NAQ, NAK, NAV, NAK, NAV, bias)


def _stack_masked_q(q):
    lane = lax.broadcasted_iota(jnp.int32, (1, 256), 1)
    return jnp.concatenate([jnp.where((lane // 32) == hm, q, jnp.zeros_like(q)) for hm in range(8)], 0)


def _da_finish(acc, lam, gain, g64, post_scale):
    tq = acc.shape[0] // 8
    norm = acc / jnp.broadcast_to(acc[:, 64:65], acc.shape)
    lane = lax.broadcasted_iota(jnp.int32, (1, 128), 1)
    halves = []
    for hp in range(2):
        d = [norm[(4 * hp + 2 * j) * tq:(4 * hp + 2 * j + 1) * tq]
             - lam * norm[(4 * hp + 2 * j + 1) * tq:(4 * hp + 2 * j + 2) * tq] for j in range(2)]
        halves.append(jnp.where(lane < 64, d[0], pltpu.roll(d[1], 64, 1)))
    o = jnp.concatenate(halves, -1)
    return _group_rms(o, g64, 64, gain) * post_scale


def _da_tile(qs, k, vx, m, acc, tq):
    tk = k.shape[0]
    s = lax.dot_general(qs, k, NT_DIMS, preferred_element_type=jnp.float32)
    m_new = jnp.maximum(m, jnp.max(s, -1, keepdims=True))
    alpha = jnp.exp2(m - m_new)
    p = jnp.exp2(s - pltpu.repeat(m_new, tk // 128, axis=1)).astype(jnp.bfloat16)
    pv = jnp.concatenate(
        [jnp.dot(p[2 * h * tq:(2 * h + 2) * tq], vx[:, h * 128:(h + 1) * 128], preferred_element_type=jnp.float32)
         for h in range(4)], 0)
    return m_new, alpha * acc + pv


def _da_kernel(lam_ref, q_ref, k_ref, vx_ref, kc_ref, vxc_ref, gain_ref, g64_ref, o_ref, m_ref, acc_ref,
               *, tk, n_kv, post_scale):
    tq = q_ref.shape[0]
    qs = _stack_masked_q(q_ref[...])
    m_ref[...] = jnp.full(m_ref.shape, -jnp.inf, jnp.float32)
    acc_ref[...] = jnp.zeros(acc_ref.shape, jnp.float32)

    def body(j, carry):
        for u in range(2):
            off = pl.multiple_of((2 * j + u) * tk, tk)
            m, acc = _da_tile(qs, k_ref[pl.ds(off, tk), :], vx_ref[pl.ds(off, tk), :], m_ref[...], acc_ref[...], tq)
            m_ref[...] = m
            acc_ref[...] = acc
        return carry

    lax.fori_loop(0, n_kv // 2 - 1, body, 0, unroll=True)
    off = (n_kv - 2) * tk
    m, acc = _da_tile(qs, k_ref[off:off + tk, :], vx_ref[off:off + tk, :], m_ref[...], acc_ref[...], tq)
    k_last = jnp.concatenate([k_ref[off + tk:off + 2 * tk, :], kc_ref[...]], 0)
    vx_last = jnp.concatenate([vx_ref[off + tk:off + 2 * tk, :], vxc_ref[...]], 0)
    m, acc = _da_tile(qs, k_last, vx_last, m, acc, tq)
    o_ref[...] = _da_finish(acc, lam_ref[0], gain_ref[...], g64_ref[...], post_scale).astype(o_ref.dtype)


def da_call(lam, DAQ, DAK, DAVX, subln_g, *, n_batch, seq, ctx_len, lam_init, tq=256, tk=1024):
    tk = min(tk, seq // 2)
    nq = seq // tq
    cb = n_batch * seq // ctx_len
    g64 = _group_ones(256, 64)
    gain = jnp.tile(subln_g, 4).reshape(1, 256)
    n_kv = seq // tk
    assert n_kv % 2 == 0
    kern = functools.partial(_da_kernel, tk=tk, n_kv=n_kv, post_scale=1.0 - lam_init)
    return pl.pallas_call(
        kern,
        grid=(n_batch, nq),
        in_specs=[pl.BlockSpec(memory_space=pltpu.SMEM),
                  pl.BlockSpec((tq, 256), lambda b, i: (b * nq + i, 0)),
                  pl.BlockSpec((seq, 256), lambda b, i: (b, 0)),
                  pl.BlockSpec((seq, 512), lambda b, i: (b, 0)),
                  pl.BlockSpec((ctx_len, 256), lambda b, i: (cb + b, 0)),
                  pl.BlockSpec((ctx_len, 512), lambda b, i: (cb + b, 0)),
                  pl.BlockSpec((1, 256), lambda b, i: (0, 0)),
                  pl.BlockSpec((256, 256), lambda b, i: (0, 0))],
        out_specs=pl.BlockSpec((tq, 256), lambda b, i: (b * nq + i, 0)),
        out_shape=jax.ShapeDtypeStruct((n_batch * seq, 256), jnp.bfloat16),
        scratch_shapes=[pltpu.VMEM((8 * tq, 128), jnp.float32), pltpu.VMEM((8 * tq, 128), jnp.float32)],
        compiler_params=pltpu.CompilerParams(vmem_limit_bytes=VMEM_LIMIT),
        name="da_attn",
    )(lam.reshape(1), DAQ, DAK, DAVX, DAK, DAVX, gain, g64)


def _ctx_kernel(lam_ref, naq_ref, nak_ref, nav_ref, daq_ref, dak_ref, dav_ref, gain_ref, g64_ref, ya_ref, yd_ref,
                *, post_scale):
    q = naq_ref[...]
    k = nak_ref[...]
    v = nav_ref[...]
    outs = []
    for h in range(4):
        sl = slice(h * 64, (h + 1) * 64)
        s = lax.dot_general(q[:, sl], k[:, sl], NT_DIMS, preferred_element_type=jnp.float32)
        p = jnp.exp(s - jnp.max(s, -1, keepdims=True))
        o = jnp.dot(p.astype(jnp.bfloat16), v[:, sl], preferred_element_type=jnp.float32)
        outs.append(o / jnp.sum(p, -1, keepdims=True))
    ya_ref[...] = jnp.concatenate(outs, -1).astype(ya_ref.dtype)
    tq = daq_ref.shape[0]
    qs = _stack_masked_q(daq_ref[...])
    m0 = jnp.full((8 * tq, 128), -jnp.inf, jnp.float32)
    m, acc = _da_tile(qs, dak_ref[...], dav_ref[...], m0, jnp.zeros((8 * tq, 128), jnp.float32), tq)
    yd_ref[...] = _da_finish(acc, lam_ref[0], gain_ref[...], g64_ref[...], post_scale).astype(yd_ref.dtype)


def ctx_attn_call(lam, NAQ, NAK, NAV, DAQ, DAK, DAVX, subln_g, *, n_batch, seq, ctx_len, lam_init):
    cb = n_batch * seq // ctx_len
    g64 = _group_ones(256, 64)
    gain = jnp.tile(subln_g, 4).reshape(1, 256)
    blk = lambda col: pl.BlockSpec((ctx_len, 256), lambda b: (cb + b, col))
    return pl.pallas_call(
        functools.partial(_ctx_kernel, post_scale=1.0 - lam_init),
        grid=(n_batch,),
        in_specs=[pl.BlockSpec(memory_space=pltpu.SMEM), blk(0), blk(0), blk(0), blk(0), blk(0),
                  pl.BlockSpec((ctx_len, 512), lambda b: (cb + b, 0)),
                  pl.BlockSpec((1, 256), lambda b: (0, 0)), pl.BlockSpec((256, 256), lambda b: (0, 0))],
        out_specs=[pl.BlockSpec((ctx_len, 256), lambda b: (b, 0))] * 2,
        out_shape=[jax.ShapeDtypeStruct((n_batch * ctx_len, 256), jnp.bfloat16)] * 2,
        name="ctx_attn",
    )(lam.reshape(1), NAQ, NAK, NAV, DAQ, DAK, DAVX, gain, g64)


def _split_bf16(a):
    hi = a.astype(jnp.bfloat16)
    return hi, (a - hi.astype(jnp.float32)).astype(jnp.bfloat16)


def _route(logits_t, bias_col):
    per_group = N_EXPERTS // N_GROUPS
    scores = jax.nn.sigmoid(logits_t)
    sel_all = scores + bias_col
    sel = [sel_all[e:e + 1, :] for e in range(N_EXPERTS)]
    top2 = []
    for e in range(N_EXPERTS):
        g0 = e // per_group * per_group
        rank = jnp.zeros_like(sel[e])
        for o in range(g0, g0 + per_group):
            if o != e:
                beats = (sel[o] > sel[e]) | (sel[o] == sel[e]) if o < e else (sel[o] > sel[e])
                rank = rank + beats.astype(jnp.float32)
        top2.append(rank < 1.5)
    gs = []
    for g in range(N_GROUPS):
        acc = jnp.zeros_like(sel[0])
        for e in range(g * per_group, (g + 1) * per_group):
            acc = acc + jnp.where(top2[e], sel[e], 0.0)
        gs.append(acc)
    rows = []
    for g in range(N_GROUPS):
        beaten = jnp.zeros_like(sel[0])
        for o in range(N_GROUPS):
            if o != g:
                b = (gs[o] >= gs[g]) if o < g else (gs[o] > gs[g])
                beaten = beaten + b.astype(jnp.float32)
        best = beaten < 0.5
        for e in range(g * per_group, (g + 1) * per_group):
            rows.append(jnp.where(best & top2[e], scores[e:e + 1, :], 0.0))
    w = jnp.concatenate(rows, 0)
    return w / jnp.sum(w, axis=0, keepdims=True)


def _merge_kernel(xl_ref, xc_ref, yal_ref, yac_ref, ydl_ref, ydc_ref, yfl_ref, yfc_ref, of_ref, ob_ref, z_ref, gate_ref,
                  m_ref, g2_ref, og_ref, g64_ref, wb_ref, wo_ref, wr_ref, rb_ref, xo_ref, h2_ref, rw_ref, *, tiles_per_batch, n_lat_tiles, n_batch):
    i = pl.program_id(0)
    r = jnp.where(i < n_lat_tiles, i // tiles_per_batch, n_batch)
    z = z_ref[...].astype(jnp.float32)
    yg = (_group_rms(of_ref[...] + ob_ref[...], g64_ref[...], GDN_DIM, og_ref[...]) * (z * jax.nn.sigmoid(z)))
    is_lat = i < n_lat_tiles
    pick = lambda lat_ref, ctx_ref: jnp.where(is_lat, lat_ref[...], ctx_ref[...])
    branches = (pick(yal_ref, yac_ref), pick(ydl_ref, ydc_ref), yg.astype(jnp.bfloat16),
                pick(yfl_ref, yfc_ref).astype(jnp.bfloat16))
    mix = None
    for b, yb in enumerate(branches):
        proj = jnp.dot(yb, wb_ref[b], preferred_element_type=jnp.float32)
        term = gate_ref[:, b * D_MODEL:(b + 1) * D_MODEL].astype(jnp.float32) * proj
        mix = term if mix is None else mix + term
    y = jnp.dot(mix.astype(jnp.bfloat16), wo_ref[...], preferred_element_type=jnp.float32)
    mrow = lambda k: m_ref[pl.ds(r, 1), k * D_MODEL:(k + 1) * D_MODEL]
    x = jnp.where(i < n_lat_tiles, xl_ref[...], xc_ref[...]) + mrow(2) * y
    xo_ref[...] = x
    h2 = (x * lax.rsqrt(jnp.mean(x * x, axis=-1, keepdims=True) + RMS_EPS) * g2_ref[...]) * (1.0 + mrow(4)) + mrow(3)
    h2_ref[...] = h2.astype(h2_ref.dtype)
    h_hi, h_lo = _split_bf16(h2)
    w_hi, w_lo = _split_bf16(wr_ref[...])
    nt = lambda a, b: lax.dot_general(a, b, NT_DIMS, preferred_element_type=jnp.float32)
    logits_t = nt(w_hi, h_hi) + nt(w_hi, h_lo) + nt(w_lo, h_hi)
    w_t = _route(logits_t, rb_ref[...])
    pad = jnp.zeros((128 - N_EXPERTS, w_t.shape[1]), jnp.float32)
    rw_ref[...] = jnp.transpose(jnp.concatenate([w_t, pad], 0))


def merge_call(Xl, Xc, ctx_row0, Ya, Yd, Yf, Of, Ob, P_z, P_gate, mod_l, g2n, onorm_g, w_branch, w_out, w_router,
               router_bias, *, n_tok, n_batch, seq, tm=256):
    D = Xl.shape[1]
    n_lat_tiles = n_batch * seq // tm
    kern = functools.partial(_merge_kernel, tiles_per_batch=seq // tm, n_lat_tiles=n_lat_tiles, n_batch=n_batch)
    row = lambda w: pl.BlockSpec((tm, w), lambda i: (i, 0))
    full = lambda a: pl.BlockSpec(a.shape, lambda i: (0,) * a.ndim)
    wr_t = jnp.transpose(w_router)
    rb = router_bias.reshape(N_EXPERTS, 1)
    g2 = g2n.reshape(1, D)
    og = jnp.tile(onorm_g, GDN_HEADS).reshape(1, BRANCH_W)
    g64 = _group_ones(BRANCH_W, GDN_DIM)
    return pl.pallas_call(
        kern,
        grid=(n_tok // tm,),
        in_specs=_stream_specs(n_lat_tiles, ctx_row0 // tm, tm, D) + 3 * _stream_specs(n_lat_tiles, 0, tm, 256) + [
                  row(256), row(256), row(256), row(4 * D), full(mod_l),
                  full(g2), full(og), full(g64), full(w_branch), full(w_out), full(wr_t), full(rb)],
        out_specs=[row(D), row(D), row(128)],
        out_shape=[jax.ShapeDtypeStruct((n_tok, D), jnp.float32), jax.ShapeDtypeStruct((n_tok, D), jnp.bfloat16),
                   jax.ShapeDtypeStruct((n_tok, 128), jnp.float32)],
        compiler_params=pltpu.CompilerParams(vmem_limit_bytes=VMEM_LIMIT),
        name="merge",
    )(Xl, Xc, *Ya, *Yd, *Yf, Of, Ob, P_z, P_gate, mod_l, g2, og, g64, w_branch, w_out, wr_t, rb)


MOE_TILE = 1024
MOE_WIN = 320
MOE_WIN_CTX = 176
MOE_EPS = 4


def _moe_sorted_kernel(x_ref, h_ref, rw_ref, m_ref, wg_ref, wu_ref, wd_ref, o_ref, xs_ref, ws_ref, ys_ref, dest_ref,
                       seg_ref, *, win, mod_row_fn):
    T = h_ref.shape[0]
    per_group = N_EXPERTS // N_GROUPS
    i = pl.program_id(0)
    step = pl.program_id(1)
    lane = lax.broadcasted_iota(jnp.int32, (1, 128), 1)
    f32, bf16 = jnp.float32, jnp.bfloat16

    @pl.when(step == 0)
    def _():
        rw = rw_ref[...]
        r128 = lax.broadcasted_iota(jnp.int32, (128, 128), 0)
        c128 = lax.broadcasted_iota(jnp.int32, (128, 128), 1)
        e2g = ((r128 // per_group == c128) & (r128 < N_EXPERTS)).astype(bf16)
        og = (jnp.dot((rw > 0.0).astype(bf16), e2g, preferred_element_type=f32) > 0.5).astype(f32)
        rt = lax.broadcasted_iota(jnp.int32, (T, T), 0)
        ct = lax.broadcasted_iota(jnp.int32, (T, T), 1)
        earlier = jnp.dot((rt > ct).astype(bf16), og.astype(bf16), preferred_element_type=f32)
        cnt_row = jnp.sum(og, axis=0, keepdims=True)
        start_row = jnp.dot(jnp.broadcast_to(cnt_row, (8, 128)), (r128 < c128).astype(f32),
                            preferred_element_type=f32, precision=HI)[0:1]
        dest = jnp.sum(og * (earlier + start_row), axis=-1, keepdims=True)
        dest_ref[...] = jnp.broadcast_to(dest, (T, 128))
        dest_row = jnp.transpose(dest_ref[...])[0:1, :]
        perm = (dest_row == rt.astype(f32)).astype(bf16)
        xs_ref[...] = jnp.dot(perm, h_ref[...], preferred_element_type=f32).astype(bf16)
        ws_ref[...] = sum(jnp.dot(perm, piece, preferred_element_type=f32) for piece in _split_bf16(rw))
        ys_ref[...] = jnp.zeros(ys_ref.shape, f32)
        for g in range(N_GROUPS):
            seg_ref[g] = jnp.sum(jnp.where(lane == g, start_row, 0.0)).astype(jnp.int32)
            seg_ref[N_GROUPS + g] = jnp.sum(jnp.where(lane == g, cnt_row, 0.0)).astype(jnp.int32)

    g = (step * MOE_EPS) // per_group
    start = seg_ref[g]
    end = start + seg_ref[N_GROUPS + g]
    s0 = (start // 16) * 16
    n_win = (end - s0 + win - 1) // win

    def window(w, carry):
        lo = s0 + w * win
        off = pl.multiple_of(jnp.minimum(lo, T - win), 16)
        xw = xs_ref[pl.ds(off, win), :]
        ww = ws_ref[pl.ds(off, win), :]
        row = off + lax.broadcasted_iota(jnp.int32, (win, 1), 0)
        y = None
        for j in range(MOE_EPS):
            w_col = jnp.sum(jnp.where(lane == step * MOE_EPS + j, ww, 0.0), axis=-1, keepdims=True)
            w_col = jnp.where(row >= lo, w_col, 0.0)
            a = jnp.dot(xw, wg_ref[0, j], preferred_element_type=f32)
            u = jnp.dot(xw, wu_ref[0, j], preferred_element_type=f32)
            act = (a * jax.nn.sigmoid(a) * u * w_col).astype(bf16)
            yj = jnp.dot(act, wd_ref[0, j], preferred_element_type=f32)
            y = yj if y is None else y + yj
        ys_ref[pl.ds(off, win), :] += y
        return carry

    lax.fori_loop(0, n_win, window, 0)

    @pl.when(step == N_EXPERTS // MOE_EPS - 1)
    def _():
        ct = lax.broadcasted_iota(jnp.int32, (T, T), 1)
        unperm = (pltpu.repeat(dest_ref[...], T // 128, axis=1) == ct.astype(f32)).astype(bf16)
        f = jnp.dot(unperm, ys_ref[...].astype(bf16), preferred_element_type=f32)
        g2 = m_ref[pl.ds(mod_row_fn(i), 1), 5 * D_MODEL:6 * D_MODEL]
        o_ref[...] = x_ref[...] + g2 * f


def moe_sorted_call(Xmid, H2, RW, mod_l, w_gate, w_up, w_down, *, layer, row0, n_rows, tile, win, mod_row_fn):
    D = Xmid.shape[1]
    F = w_gate.shape[-1]
    b0 = row0 // tile
    tok = lambda w: pl.BlockSpec((tile, w), lambda i, e: (b0 + i, 0))
    kern = functools.partial(_moe_sorted_kernel, win=win, mod_row_fn=mod_row_fn)
    return pl.pallas_call(
        kern,
        grid=(n_rows // tile, N_EXPERTS // MOE_EPS),
        in_specs=[tok(D), tok(D), tok(128),
                  pl.BlockSpec(mod_l.shape, lambda i, e: (0, 0)),
                  pl.BlockSpec((1, MOE_EPS, D, F), lambda i, e: (layer, e, 0, 0)),
                  pl.BlockSpec((1, MOE_EPS, D, F), lambda i, e: (layer, e, 0, 0)),
                  pl.BlockSpec((1, MOE_EPS, F, D), lambda i, e: (layer, e, 0, 0))],
        out_specs=pl.BlockSpec((tile, D), lambda i, e: (i, 0)),
        out_shape=jax.ShapeDtypeStruct((n_rows, D), jnp.float32),
        scratch_shapes=[pltpu.VMEM((tile, D), jnp.bfloat16), pltpu.VMEM((tile, 128), jnp.float32),
                        pltpu.VMEM((tile, D), jnp.float32), pltpu.VMEM((tile, 128), jnp.float32),
                        pltpu.SMEM((2 * N_GROUPS,), jnp.int32)],
        compiler_params=pltpu.CompilerParams(vmem_limit_bytes=VMEM_LIMIT),
        name="moe_sorted",
    )(Xmid, H2, RW, mod_l, w_gate, w_up, w_down)


GDN_TILE = 256
GDN_CPT = GDN_TILE // GDN_CHUNK
GDN_LOCKSTEP = 2


def _mm(a, b, passes=1, dims=None):
    if dims is None:
        dot = lambda x, y: jnp.dot(x, y, preferred_element_type=jnp.float32)
    else:
        dot = lambda x, y: lax.dot_general(x, y, dims, preferred_element_type=jnp.float32)
    if passes == 1:
        return dot(a.astype(jnp.bfloat16), b.astype(jnp.bfloat16))
    a_hi, a_lo = _split_bf16(a)
    b_hi, b_lo = _split_bf16(b)
    return dot(a_hi, b_hi) + dot(a_hi, b_lo) + dot(a_lo, b_hi)


def _stack_heads(x):
    lane = lax.broadcasted_iota(jnp.int32, (1, 256), 1)
    return jnp.concatenate([jnp.where((lane // GDN_DIM) == h, x, 0.0) for h in range(GDN_HEADS)], 0)


def _slab(x):
    return x[0:64] + x[64:128] + x[128:192] + x[192:256]


def _unit_tri_inverse(mats):
    shape = mats[0].shape
    eye = (lax.broadcasted_iota(jnp.int32, shape, 0) == lax.broadcasted_iota(jnp.int32, shape, 1)).astype(jnp.float32)
    ps = [eye - a for a in mats]
    pws = list(mats)
    for _ in range(5):
        pws = [_mm(pw, pw, 3) for pw in pws]
        ps = [p + _mm(p, pw, 3) for p, pw in zip(ps, pws)]
    return ps


def _gdn_chunk_kernel(x_ref, xp_ref, xn_ref, ab_ref, cw_ref, par_ref, g64_ref, qe_ref, ou_ref, mm_ref, nn_ref,
                      *, tiles_per_batch, n_lat_tiles):
    i = pl.program_id(0)
    is_lat = i < n_lat_tiles
    first = jnp.where(is_lat, (i % tiles_per_batch) == 0, True)
    last = jnp.where(is_lat, (i % tiles_per_batch) == tiles_per_batch - 1, True)
    xp = jnp.concatenate([jnp.where(first, 0.0, xp_ref[...]), x_ref[...], jnp.where(last, 0.0, xn_ref[...])], 0)
    y = xp[6:6 + GDN_TILE] * cw_ref[0:1, :]
    for t in range(1, 5):
        y = y + xp[6 + t:6 + t + GDN_TILE] * cw_ref[t:t + 1, :]
    y = y * jax.nn.sigmoid(y)
    g64 = g64_ref[...]
    q = y[:, 0:256]
    k = y[:, 256:512]
    v = y[:, 512:768]
    q = q * lax.rsqrt(_group_sum(q * q, g64) + RMS_EPS) * (GDN_DIM ** -0.5)
    k = k * lax.rsqrt(_group_sum(k * k, g64) + RMS_EPS)
    ab = ab_ref[...]
    lane128 = lax.broadcasted_iota(jnp.int32, (1, 128), 1)
    g_all = jnp.where(lane128 < 8, par_ref[0:1, :] * jax.nn.softplus(ab + par_ref[1:2, :]), 0.0)
    beta_all = jax.nn.sigmoid(ab)
    r64 = lax.broadcasted_iota(jnp.int32, (64, 64), 0)
    c64 = lax.broadcasted_iota(jnp.int32, (64, 64), 1)
    low = (r64 >= c64).astype(jnp.float32)
    upp = (r64 <= c64).astype(jnp.float32)
    rr = lax.broadcasted_iota(jnp.int32, (256, 256), 0)
    cc = lax.broadcasted_iota(jnp.int32, (256, 256), 1)
    same = (rr // 64) == (cc // 64)
    eye = (rr == cc).astype(jnp.float32)
    for c0 in range(0, GDN_CPT, GDN_LOCKSTEP):
        stage = []
        for c in range(c0, c0 + GDN_LOCKSTEP):
            sl = slice(c * GDN_CHUNK, (c + 1) * GDN_CHUNK)
            g_c = g_all[sl]
            gsum = jnp.where(lane128 < 4, jnp.dot(low, g_c, preferred_element_type=jnp.float32, precision=HI),
                             jnp.dot(upp, g_c, preferred_element_type=jnp.float32, precision=HI))
            gsum_t = jnp.transpose(jnp.concatenate([gsum, jnp.zeros_like(gsum)], 0))[:, 0:64]
            tot = jnp.sum(g_c, axis=0, keepdims=True)
            beta_c = beta_all[sl]
            q_sm = _stack_heads(q[sl])
            k_sm = _stack_heads(k[sl])
            v_sm = _stack_heads(v[sl])
            kk = _mm(k_sm, k_sm, 1, NT_DIMS)
            qk = _mm(q_sm, k_sm, 1, NT_DIMS)
            for d in range(2):
                col = lambda h: 4 * d + h
                cmat = jnp.concatenate([jnp.broadcast_to(gsum[:, col(h):col(h) + 1], (64, 256)) for h in range(4)], 0)
                rrow = jnp.concatenate([gsum_t[col(h):col(h) + 1, :] for h in range(4)], 1)
                bmat = jnp.concatenate([jnp.broadcast_to(beta_c[:, 8 + col(h):9 + col(h)], (64, 256))
                                        for h in range(4)], 0)
                tmat = jnp.concatenate([jnp.broadcast_to(tot[:, col(h):col(h) + 1], (64, 256)) for h in range(4)], 0)
                tri = (rr % 64 >= cc % 64) if d == 0 else (rr % 64 <= cc % 64)
                incl = same & tri
                strict = incl & (rr != cc)
                dec = jnp.where(incl, jnp.exp(jnp.where(incl, cmat - rrow, 0.0)), 0.0)
                stage.append((d, sl, q_sm, k_sm, v_sm, qk, cmat, bmat, tmat, dec,
                              jnp.where(strict, kk * bmat * dec, 0.0)))
        t_invs = _unit_tri_inverse([st[-1] for st in stage])
        n = len(stage)
        q_sms, k_sms, v_sms, qks, cmats, bmats, tmats, decs = (
            [st[f] for st in stage] for f in range(2, 10))
        egs = [jnp.exp(cm) for cm in cmats]
        us = [_mm(t_invs[j], v_sms[j] * bmats[j], 1) for j in range(n)]
        ws = [_mm(t_invs[j], k_sms[j] * bmats[j] * egs[j], 1) for j in range(n)]
        qkms = [decs[j] * qks[j] for j in range(n)]
        qes = [q_sms[j] * egs[j] - _mm(qkms[j], ws[j], 1) for j in range(n)]
        ous = [_mm(qkms[j], us[j], 1) for j in range(n)]
        kd_ts = [jnp.transpose(k_sms[j] * jnp.exp(tmats[j] - cmats[j])) for j in range(n)]
        m_bds = [eye * jnp.exp(tmats[j]) - _mm(kd_ts[j], ws[j], 1) for j in range(n)]
        n_bds = [_mm(kd_ts[j], us[j], 1) for j in range(n)]
        for j in range(n):
            d, sl = stage[j][0], stage[j][1]
            qe_ref[d, sl, :] = _slab(qes[j])
            ou_ref[d, sl, :] = _slab(ous[j])
            mm_ref[d, sl, :] = _slab(m_bds[j])
            nn_ref[d, sl, :] = _slab(n_bds[j])


def gdn_chunk_call(P_gdn, P_ab, conv_w, a_log, dt_bias, *, n_batch, seq):
    NT = P_gdn.shape[0]
    tpb = seq // GDN_TILE
    n_lat = n_batch * tpb
    n_tiles = NT // GDN_TILE
    hb = GDN_TILE // 8
    cw = jnp.zeros((8, 768), jnp.float32).at[:5].set(conv_w)
    par = jnp.zeros((8, 128), jnp.float32)
    par = par.at[0, :8].set(-jnp.exp(a_log.reshape(8))).at[1, :8].set(dt_bias.reshape(8))
    g64 = _group_ones(256, 64)
    kern = functools.partial(_gdn_chunk_kernel, tiles_per_batch=tpb, n_lat_tiles=n_lat)
    full = lambda a: pl.BlockSpec(a.shape, lambda i: (0,) * a.ndim)
    out = pl.BlockSpec((2, GDN_TILE, 256), lambda i: (0, i, 0))
    return pl.pallas_call(
        kern,
        grid=(n_tiles,),
        in_specs=[pl.BlockSpec((GDN_TILE, 768), lambda i: (i, 0)),
                  pl.BlockSpec((8, 768), lambda i: (jnp.maximum(i * hb - 1, 0), 0)),
                  pl.BlockSpec((8, 768), lambda i: (jnp.minimum((i + 1) * hb, n_tiles * hb - 1), 0)),
                  pl.BlockSpec((GDN_TILE, 128), lambda i: (i, 0)),
                  full(cw), full(par), full(g64)],
        out_specs=[out] * 4,
        out_shape=[jax.ShapeDtypeStruct((2, NT, 256), jnp.float32)] * 4,
        compiler_params=pltpu.CompilerParams(vmem_limit_bytes=VMEM_LIMIT),
        name="gdn_chunk",
    )(P_gdn, P_gdn, P_gdn, P_ab, cw, par, g64)


def _gdn_scan_kernel(qef_ref, ouf_ref, mf_ref, nf_ref, qeb_ref, oub_ref, mb_ref, nb_ref, of_ref, ob_ref, s_ref):
    t = pl.program_id(1)

    @pl.when(t == 0)
    def _():
        s_ref[...] = jnp.zeros(s_ref.shape, jnp.float32)

    def step(d, c, qe_ref, ou_ref, m_ref, n_ref, o_ref):
        sl = slice(c * GDN_CHUNK, (c + 1) * GDN_CHUNK)
        s = s_ref[d]
        o_ref[sl, :] = _slab(_mm(_stack_heads(qe_ref[0, sl, :]), s, 1)) + ou_ref[0, sl, :]
        s_ref[d] = _mm(_stack_heads(m_ref[0, sl, :]), s, 3) + _stack_heads(n_ref[0, sl, :])

    for c in range(GDN_CPT):
        step(0, c, qef_ref, ouf_ref, mf_ref, nf_ref, of_ref)
        step(1, GDN_CPT - 1 - c, qeb_ref, oub_ref, mb_ref, nb_ref, ob_ref)


def gdn_scan_call(QE, OU, MM, NN, *, n_batch, seq):
    NT = QE.shape[1]
    tpb = seq // GDN_TILE
    cblk = n_batch * tpb
    fwd = lambda b, t: jnp.where(t == 0, cblk + b, b * tpb + t - 1)
    bwd = lambda b, t: jnp.where(t == 0, cblk + b, b * tpb + tpb - t)
    spec = lambda d, f: pl.BlockSpec((1, GDN_TILE, 256), lambda b, t: (d, f(b, t), 0))
    ospec = lambda f: pl.BlockSpec((GDN_TILE, 256), lambda b, t: (f(b, t), 0))
    return pl.pallas_call(
        _gdn_scan_kernel,
        grid=(n_batch, tpb + 1),
        in_specs=[spec(0, fwd)] * 4 + [spec(1, bwd)] * 4,
        out_specs=[ospec(fwd), ospec(bwd)],
        out_shape=[jax.ShapeDtypeStruct((NT, 256), jnp.float32)] * 2,
        scratch_shapes=[pltpu.VMEM((2, 256, 256), jnp.float32)],
        compiler_params=pltpu.CompilerParams(vmem_limit_bytes=VMEM_LIMIT),
        name="gdn_scan",
    )(QE, OU, MM, NN, QE, OU, MM, NN)


def _dft_cs(n):
    a = 2.0 * np.pi * np.outer(np.arange(n), np.arange(n)) / n
    return np.cos(a), np.sin(a)


def _channel_dft():
    c, s = _dft_cs(FNET_GROUP_W)
    eye = np.eye(FNET_GROUPS)
    return np.concatenate([np.kron(eye, c), np.kron(eye, s)], 1)


FNET_SUB = 16


def _fnet_a_kernel(x_ref, cs_ref, m_ref, cphi_ref, sphi_ref, br_ref, bi_ref, *, n_rows):
    for j in range(FNET_SUB):
        sl = slice(j * BRANCH_W, (j + 1) * BRANCH_W)
        u = jnp.dot(x_ref[:, j, :], cs_ref[...], preferred_element_type=jnp.float32)
        st = jnp.concatenate([u[:, :BRANCH_W], u[:, BRANCH_W:]], 0).astype(jnp.bfloat16)
        a = jnp.dot(m_ref[...], st, preferred_element_type=jnp.float32)
        ar, ai = a[:n_rows], a[n_rows:]
        cp, sp = cphi_ref[:, sl], sphi_ref[:, sl]
        br_ref[j] = ar * cp - ai * sp
        bi_ref[j] = ar * sp + ai * cp


def _fnet_c_kernel(br_ref, bi_ref, m_ref, o_ref):
    for j in range(FNET_SUB):
        st = jnp.concatenate([br_ref[:, j, :], bi_ref[:, j, :]], 0).astype(jnp.bfloat16)
        o_ref[:, j, :] = jnp.dot(m_ref[...], st, preferred_element_type=jnp.float32)


def fourier_latent_call(P_f, *, n_batch, seq):
    rows = seq // GRID_W
    bf16, f32 = jnp.bfloat16, jnp.float32
    c1, s1 = _dft_cs(rows)
    norm = 1.0 / math.sqrt(seq * FNET_GROUP_W)
    m_a = jnp.asarray(np.block([[c1, -s1], [s1, c1]]) * norm, bf16)
    c2, s2 = _dft_cs(GRID_W)
    m_c = jnp.asarray(np.concatenate([c2, -s2], 1), bf16)
    cs = jnp.asarray(_channel_dft(), bf16)
    phi = 2.0 * np.pi * np.outer(np.arange(rows), np.arange(GRID_W)) / seq
    cphi = jnp.repeat(jnp.asarray(np.cos(phi), f32), BRANCH_W, axis=1)
    sphi = jnp.repeat(jnp.asarray(np.sin(phi), f32), BRANCH_W, axis=1)
    xv = P_f.reshape(P_f.shape[0] // GRID_W, GRID_W, BRANCH_W)
    nj = GRID_W // FNET_SUB
    full = lambda a: pl.BlockSpec(a.shape, lambda b, j: (0,) * a.ndim)
    tw = pl.BlockSpec((rows, FNET_SUB * BRANCH_W), lambda b, j: (0, j))
    mid = jax.ShapeDtypeStruct((n_batch * GRID_W, rows, BRANCH_W), f32)
    br, bi = pl.pallas_call(
        functools.partial(_fnet_a_kernel, n_rows=rows),
        grid=(n_batch, nj),
        in_specs=[pl.BlockSpec((rows, FNET_SUB, BRANCH_W), lambda b, j: (b, j, 0)), full(cs), full(m_a), tw, tw],
        out_specs=[pl.BlockSpec((FNET_SUB, rows, BRANCH_W), lambda b, j: (b * nj + j, 0, 0))] * 2,
        out_shape=[mid, mid],
        name="fnet_rows",
    )(xv, cs, m_a, cphi, sphi)
    blk = pl.BlockSpec((GRID_W, FNET_SUB, BRANCH_W), lambda b, i: (b, i, 0))
    y = pl.pallas_call(
        _fnet_c_kernel,
        grid=(n_batch, rows // FNET_SUB),
        in_specs=[blk, blk, pl.BlockSpec(m_c.shape, lambda b, i: (0, 0))],
        out_specs=blk,
        out_shape=mid,
        name="fnet_cols",
    )(br, bi, m_c)
    return y.reshape(n_batch * seq, BRANCH_W)


def _fnet_ctx_kernel(x_ref, cs_ref, m_ref, o_ref):
    u = jnp.dot(x_ref[...], cs_ref[...], preferred_element_type=jnp.float32)
    st = jnp.concatenate([u[:, :BRANCH_W], u[:, BRANCH_W:]], 0).astype(jnp.bfloat16)
    o_ref[...] = jnp.dot(m_ref[...], st, preferred_element_type=jnp.float32).astype(o_ref.dtype)


def fourier_ctx_call(P_f, *, n_batch, seq, ctx_len):
    bf16 = jnp.bfloat16
    c, s = _dft_cs(ctx_len)
    m = jnp.asarray(np.concatenate([c, -s], 1) / math.sqrt(ctx_len * FNET_GROUP_W), bf16)
    cs = jnp.asarray(_channel_dft(), bf16)
    cb = n_batch * seq // ctx_len
    return pl.pallas_call(
        _fnet_ctx_kernel,
        grid=(n_batch,),
        in_specs=[pl.BlockSpec((ctx_len, BRANCH_W), lambda b: (cb + b, 0)),
                  pl.BlockSpec(cs.shape, lambda b: (0, 0)), pl.BlockSpec(m.shape, lambda b: (0, 0))],
        out_specs=pl.BlockSpec((ctx_len, BRANCH_W), lambda b: (b, 0)),
        out_shape=jax.ShapeDtypeStruct((n_batch * ctx_len, BRANCH_W), jnp.float32),
        name="fnet_ctx",
    )(P_f, cs, m)


def kernel(x, c, ctx, c_ctx, w_mod, b_mod, norm1_g, norm2_g, w_in, na_qn_g, na_kn_g, na_rpb,
           da_qn_g, da_kn_g, da_lam_q1, da_lam_k1, da_lam_q2, da_lam_k2, da_subln_g, gdn_conv_w,
           gdn_a_log, gdn_dt_bias, gdn_onorm_g, w_branch, w_out, w_router, router_bias,
           moe_w_gate, moe_w_up, moe_w_down):
    B, S, D = x.shape
    L = ctx.shape[1]
    NL = B * S
    f32 = jnp.float32
    cvec = jnp.zeros((8, D), f32).at[:B].set(c).at[B].set(c_ctx)
    mods = mod_call(cvec, w_mod, b_mod)
    cos, sin = rope_tables(S, 256)
    bf16 = jnp.bfloat16
    Xl, Xc = x.reshape(NL, D), ctx.reshape(B * L, D)
    experts = (moe_w_gate.astype(bf16), moe_w_up.astype(bf16), moe_w_down.astype(bf16))
    w_t = jnp.swapaxes(w_in, 1, 2).astype(bf16)
    for l in range(DEPTH):
        last = l == DEPTH - 1
        lam_init = 0.8 - 0.6 * math.exp(-0.3 * l)
        NAQ, NAK, NAV, DAQ, DAK, DAVX, P_gdn, P_z, P_f, P_gate, P_ab = inproj_call(
            Xl, Xc, 0, mods[l], norm1_g[l], w_t, l, cos, sin, na_qn_g[l], na_kn_g[l], da_qn_g[l], da_kn_g[l],
            n_batch=B, seq=S, ctx_len=L)
        bias = na_bias_tables(na_rpb[l], S // GRID_W)
        lam = (jnp.exp(jnp.sum(da_lam_q1[l] * da_lam_k1[l])) - jnp.exp(jnp.sum(da_lam_q2[l] * da_lam_k2[l]))
               + lam_init).astype(f32)
        Ya = na_call(NAQ, NAK, NAV, bias, n_batch=B, seq=S, ctx_len=L)
        Yd = da_call(lam, DAQ, DAK, DAVX, da_subln_g[l], n_batch=B, seq=S, ctx_len=L, lam_init=lam_init)
        QE, OU, MM, NN = gdn_chunk_call(P_gdn, P_ab, gdn_conv_w[l], gdn_a_log[l], gdn_dt_bias[l], n_batch=B, seq=S)
        Of, Ob = gdn_scan_call(QE, OU, MM, NN, n_batch=B, seq=S)
        Yf = fourier_latent_call(P_f, n_batch=B, seq=S)
        n_tok = NL
        Yac, Ydc, Yfc = Ya, Yd, Yf
        if not last:
            Yac, Ydc = ctx_attn_call(lam, NAQ, NAK, NAV, DAQ, DAK, DAVX, da_subln_g[l],
                                     n_batch=B, seq=S, ctx_len=L, lam_init=lam_init)
            Yfc = fourier_ctx_call(P_f, n_batch=B, seq=S, ctx_len=L)
            n_tok = NL + B * L
        Xmid, H2, RW = merge_call(Xl, Xc, 0, (Ya, Yac), (Yd, Ydc), (Yf, Yfc), Of, Ob, P_z, P_gate, mods[l], norm2_g[l], gdn_onorm_g[l],
                                  w_branch[l].astype(bf16), w_out[l].astype(bf16), w_router, router_bias,
                                  n_tok=n_tok, n_batch=B, seq=S)
        Xl = moe_sorted_call(Xmid, H2, RW, mods[l], *experts, layer=l, row0=0, n_rows=NL, tile=MOE_TILE, win=MOE_WIN,
                             mod_row_fn=lambda i: i // (S // MOE_TILE))
        if not last:
            Xc = moe_sorted_call(Xmid, H2, RW, mods[l], *experts, layer=l, row0=NL, n_rows=B * L, tile=B * L,
                                 win=MOE_WIN_CTX, mod_row_fn=lambda i: B)
    return Xl.reshape(B, S, D)
```

```python
import functools
import math
import jax
import jax.numpy as jnp
from jax import lax
import numpy as np
from jax.experimental import pallas as pl
from jax.experimental.pallas import tpu as pltpu

D_MODEL = 1024
DEPTH = 2
GRID_W = 64
N_BRANCH = 4
BRANCH_W = D_MODEL // N_BRANCH
GDN_HEADS = 4
GDN_DIM = BRANCH_W // GDN_HEADS
GDN_CHUNK = 64
FNET_GROUPS = 4
FNET_GROUP_W = BRANCH_W // FNET_GROUPS
N_EXPERTS = 16
N_GROUPS = 4
RMS_EPS = 1e-6
NEG_INF = -1e30
ROPE_BASE = 10000.0
VMEM_LIMIT = 56 * 1024 * 1024
W_AB0 = 2560
HI = lax.Precision.HIGHEST
NT_DIMS = (((1,), (1,)), ((), ()))
LOG2E = 1.4426950408889634


def _mod_kernel(c_ref, w_ref, b_ref, o_ref):
    c = c_ref[...]
    a = c * jax.nn.sigmoid(c)
    o_ref[0] = _mm(a, w_ref[0], 3) + b_ref[0]


def mod_call(cvec, w_mod, b_mod, tn=1024):
    depth, D, N = w_mod.shape
    return pl.pallas_call(
        _mod_kernel,
        grid=(depth, N // tn),
        in_specs=[pl.BlockSpec((8, D), lambda l, j: (0, 0)),
                  pl.BlockSpec((1, D, tn), lambda l, j: (l, 0, j)),
                  pl.BlockSpec((1, 1, tn), lambda l, j: (l, 0, j))],
        out_specs=pl.BlockSpec((1, 8, tn), lambda l, j: (l, 0, j)),
        out_shape=jax.ShapeDtypeStruct((depth, 8, N), jnp.float32),
        name="mod",
    )(cvec, w_mod, b_mod.reshape(depth, 1, N))


def _stream_specs(n_lat_tiles, ctx_tile0, tm, width):
    return [pl.BlockSpec((tm, width), lambda i: (jnp.minimum(i, n_lat_tiles - 1), 0)),
            pl.BlockSpec((tm, width), lambda i: (ctx_tile0 + jnp.maximum(i - n_lat_tiles, 0), 0))]


def _inproj_kernel(xl_ref, xc_ref, m_ref, g_ref, wt_ref, cos_ref, sin_ref, g64_ref, g32_ref, gains_ref,
                   naq_ref, nak_ref, nav_ref, daq_ref, dak_ref, davx_ref, gdn_ref, z_ref, f_ref, gate_ref, ab_ref,
                   *, tiles_per_batch, n_lat_tiles, n_batch):
    i = pl.program_id(0)
    r = jnp.where(i < n_lat_tiles, i // tiles_per_batch, n_batch)
    x = jnp.where(i < n_lat_tiles, xl_ref[...], xc_ref[...])
    y = x * lax.rsqrt(jnp.mean(x * x, axis=-1, keepdims=True) + RMS_EPS) * g_ref[...]
    sh = m_ref[pl.ds(r, 1), 0:D_MODEL]
    sc = m_ref[pl.ds(r, 1), D_MODEL:2 * D_MODEL]
    h = (y * (1.0 + sc) + sh).astype(jnp.bfloat16)
    proj = lambda a, b: lax.dot_general(h, wt_ref[0, a:b, :], NT_DIMS, preferred_element_type=jnp.float32)
    proj_b = lambda a, b: proj(W_AB0 + 16 + a, W_AB0 + 16 + b)
    bf16 = jnp.bfloat16

    g64 = g64_ref[...]
    naq_ref[...] = (_group_rms(proj(0, 256), g64, 64, gains_ref[0:1, :]) * (64 ** -0.5)).astype(bf16)
    nak_ref[...] = _group_rms(proj(256, 512), g64, 64, gains_ref[1:2, :]).astype(bf16)
    nav_ref[...] = proj(512, 768).astype(bf16)

    g32 = g32_ref[...]
    cos = cos_ref[...]
    sin = sin_ref[...]
    lane = lax.broadcasted_iota(jnp.int32, (1, 256), 1)
    first = (lane % 16) < 8

    def rope(t):
        swapped = jnp.where(first, pltpu.roll(t, 256 - 8, 1), pltpu.roll(t, 8, 1))
        return t * cos + swapped * sin

    q = rope(_group_rms(proj(768, 1024), g32, 32, gains_ref[2:3, :]))
    k = rope(_group_rms(proj(1024, 1280), g32, 32, gains_ref[3:4, :]))
    daq_ref[...] = (q * (32 ** -0.5 * LOG2E)).astype(bf16)
    dak_ref[...] = k.astype(bf16)
    v = proj(1280, 1536).astype(bf16)
    one_col = (lax.broadcasted_iota(jnp.int32, (v.shape[0], 64), 1) == 0).astype(bf16)
    davx_ref[...] = jnp.concatenate(
        [piece for hd in range(4) for piece in (v[:, hd * 64:(hd + 1) * 64], one_col)], -1)

    gdn_ref[...] = proj(1536, 2304)
    z_ref[...] = proj(2304, 2560).astype(bf16)
    f_ref[...] = proj_b(0, 256).astype(bf16)
    for kk in range(4):
        gate_ref[:, kk * 1024:(kk + 1) * 1024] = jax.nn.sigmoid(
            proj_b(256 + kk * 1024, 256 + (kk + 1) * 1024)).astype(bf16)
    ab_ref[...] = proj(W_AB0, W_AB0 + 128)


def inproj_call(Xl, Xc, ctx_row0, mod_l, g, w_t, layer, cos, sin, na_qg, na_kg, da_qg, da_kg,
                *, n_batch, seq, ctx_len, tm=256):
    D = Xl.shape[1]
    NT = n_batch * (seq + ctx_len)
    tpb = seq // tm
    n_lat_tiles = n_batch * tpb
    kern = functools.partial(_inproj_kernel, tiles_per_batch=tpb, n_lat_tiles=n_lat_tiles, n_batch=n_batch)
    bf16, f32 = jnp.bfloat16, jnp.float32
    widths = [(256, bf16)] * 5 + [(512, bf16), (768, f32), (256, bf16), (256, bf16), (4096, bf16), (128, f32)]
    gains = jnp.stack([jnp.tile(na_qg, 4), jnp.tile(na_kg, 4), jnp.tile(da_qg, 8), jnp.tile(da_kg, 8)]
                      + [jnp.zeros((256,), f32)] * 4, 0)
    g64, g32 = _group_ones(256, 64), _group_ones(256, 32)
    tab = pl.BlockSpec((tm, 256), lambda i: (jnp.where(i < n_lat_tiles, i % tpb, tpb), 0))
    full = lambda a: pl.BlockSpec(a.shape, lambda i: (0,) * a.ndim)
    return pl.pallas_call(
        kern,
        grid=(NT // tm,),
        in_specs=_stream_specs(n_lat_tiles, ctx_row0 // tm, tm, D) + [
            full(mod_l), pl.BlockSpec((1, D), lambda i: (0, 0)),
            pl.BlockSpec((1,) + w_t.shape[1:], lambda i: (layer, 0, 0)), tab, tab,
            full(g64), full(g32), full(gains)],
        out_specs=[pl.BlockSpec((tm, w), lambda i: (i, 0)) for w, _ in widths],
        out_shape=[jax.ShapeDtypeStruct((NT, w), dt) for w, dt in widths],
        compiler_params=pltpu.CompilerParams(vmem_limit_bytes=VMEM_LIMIT),
        name="inproj",
    )(Xl, Xc, mod_l, g.reshape(1, D), w_t, cos, sin, g64, g32, gains)


def _group_ones(width, group):
    i = np.arange(width)
    return jnp.asarray((i[:, None] // group == i[None, :] // group).astype(np.float32), jnp.bfloat16)


def _group_sum(xx, gmat):
    hi, lo = _split_bf16(xx)
    return (jnp.dot(hi, gmat, preferred_element_type=jnp.float32)
            + jnp.dot(lo, gmat, preferred_element_type=jnp.float32))


def _group_rms(x, gmat, group, gain):
    return x * lax.rsqrt(_group_sum(x * x, gmat) * (1.0 / group) + RMS_EPS) * gain


def rope_tables(seq, tm):
    nf = 8
    t = jnp.arange(seq)
    rows = (t // GRID_W).astype(jnp.float32)
    cols = (t % GRID_W).astype(jnp.float32)
    freqs = ROPE_BASE ** (-jnp.arange(nf, dtype=jnp.float32) / nf)
    d = np.arange(32)
    f_idx = d % 8
    use_col = (d // 16) == 1
    ang = jnp.where(use_col[None, :], cols[:, None], rows[:, None]) * freqs[f_idx][None, :]
    sign = np.where((d % 16) < 8, -1.0, 1.0).astype(np.float32)
    cos = jnp.tile(jnp.cos(ang), (1, 8))
    sin = jnp.tile(jnp.sin(ang) * sign[None, :], (1, 8))
    cos = jnp.concatenate([cos, jnp.ones((tm, 256), jnp.float32)], 0)
    sin = jnp.concatenate([sin, jnp.zeros((tm, 256), jnp.float32)], 0)
    return cos, sin


NA_RB = 4
NA_UR = 11


def na_bias_tables(rpb, rows):
    reps = [(0, 0), (4, 0), (rows - NA_RB, rows - NA_UR)]
    qc = np.arange(GRID_W)[:, None]
    kc = np.arange(GRID_W)[None, :]
    ws = np.clip(qc - 8, 0, GRID_W - 16)
    vcol = (kc >= ws) & (kc < ws + 16)
    rel_c = np.clip(kc - qc + 15, 0, 30)
    sel_c = (rel_c[..., None] == np.arange(31)).astype(np.float32)
    sel_r, vrow = [], []
    for r0, u0 in reps:
        r = r0 + np.arange(NA_RB)[:, None]
        krow = u0 + np.arange(NA_UR)[None, :]
        start = np.clip(r - 4, 0, rows - 8)
        vrow.append((krow >= start) & (krow < start + 8))
        sel_r.append((np.clip(krow - r + 7, 0, 14)[..., None] == np.arange(15)).astype(np.float32))
    sel_r, vrow = np.stack(sel_r), np.stack(vrow)
    b = jnp.einsum('taui,hij,qkj->thauqk', jnp.asarray(sel_r), rpb.astype(jnp.float32), jnp.asarray(sel_c),
                   precision=HI)
    valid = vrow[:, None, :, :, None, None] & vcol[None, None, None, None, :, :]
    return jnp.where(valid, b, NEG_INF).astype(jnp.bfloat16)


def _na_kernel(q_ref, k_ref, v_ref, kc_ref, vc_ref, bias_ref, o_ref, bias_s, *, rows):
    i = pl.program_id(1)

    @pl.when((i <= 1) | (i == rows // NA_RB - 1))
    def _():
        for h in range(4):
            for a in range(NA_RB):
                for u in range(NA_UR):
                    bias_s[h, a * GRID_W:(a + 1) * GRID_W, u * GRID_W:(u + 1) * GRID_W] = (
                        bias_ref[0, h, a, u].astype(jnp.float32))

    u0 = jnp.clip(i * NA_RB - 4, 0, rows - NA_UR)
    off = pl.multiple_of(u0 * GRID_W, GRID_W)
    nkw = NA_UR * GRID_W
    kwin = k_ref[pl.ds(off, nkw), :]
    vwin = v_ref[pl.ds(off, nkw), :]
    q = q_ref[...]
    kc = kc_ref[...]
    vc = vc_ref[...]
    sls = [slice(h * 64, (h + 1) * 64) for h in range(4)]
    dot_nt = lambda x, y: lax.dot_general(x, y, NT_DIMS, preferred_element_type=jnp.float32)
    s_locs = [dot_nt(q[:, sl], kwin[:, sl]) + bias_s[h] for h, sl in enumerate(sls)]
    s_ctxs = [dot_nt(q[:, sl], kc[:, sl]) for sl in sls]
    ms = [jnp.maximum(jnp.max(a, -1, keepdims=True), jnp.max(c, -1, keepdims=True)) for a, c in zip(s_locs, s_ctxs)]
    p_locs = [jnp.exp(a - m) for a, m in zip(s_locs, ms)]
    p_ctxs = [jnp.exp(c - m) for c, m in zip(s_ctxs, ms)]
    ls = [jnp.sum(a, -1, keepdims=True) + jnp.sum(c, -1, keepdims=True) for a, c in zip(p_locs, p_ctxs)]
    outs = [(jnp.dot(pa.astype(jnp.bfloat16), vwin[:, sl], preferred_element_type=jnp.float32)
             + jnp.dot(pc.astype(jnp.bfloat16), vc[:, sl], preferred_element_type=jnp.float32)) / l
            for pa, pc, l, sl in zip(p_locs, p_ctxs, ls, sls)]
    o_ref[...] = jnp.concatenate(outs, -1).astype(o_ref.dtype)


def na_call(NAQ, NAK, NAV, bias, *, n_batch, seq, ctx_len):
    rows = seq // GRID_W
    nblk = rows // NA_RB
    tq = NA_RB * GRID_W
    cb = n_batch * seq // ctx_len
    kern = functools.partial(_na_kernel, rows=rows)
    tsel = lambda b, i: (jnp.where(i == 0, 0, jnp.where(i == nblk - 1, 2, 1)), 0, 0, 0, 0, 0)
    return pl.pallas_call(
        kern,
        grid=(n_batch, nblk),
        in_specs=[pl.BlockSpec((tq, 256), lambda b, i: (b * nblk + i, 0)),
                  pl.BlockSpec((seq, 256), lambda b, i: (b, 0)),
                  pl.BlockSpec((seq, 256), lambda b, i: (b, 0)),
                  pl.BlockSpec((ctx_len, 256), lambda b, i: (cb + b, 0)),
                  pl.BlockSpec((ctx_len, 256), lambda b, i: (cb + b, 0)),
                  pl.BlockSpec((1,) + bias.shape[1:], tsel)],
        out_specs=pl.BlockSpec((tq, 256), lambda b, i: (b * nblk + i, 0)),
        out_shape=jax.ShapeDtypeStruct((n_batch * seq, 256), jnp.bfloat16),
        scratch_shapes=[pltpu.VMEM((4, tq, NA_UR * GRID_W), jnp.float32)],
        compiler_params=pltpu.CompilerParams(vmem_limit_bytes=VMEM_LIMIT),
        name="na_attn",
    )(NAQ, NAK, NAV, NAK, NAV, bias)


def _stack_masked_q(q):
    lane = lax.broadcasted_iota(jnp.int32, (1, 256), 1)
    return jnp.concatenate([jnp.where((lane // 32) == hm, q, jnp.zeros_like(q)) for hm in range(8)], 0)


def _da_finish(acc, lam, gain, g64, post_scale):
    tq = acc.shape[0] // 8
    norm = acc / jnp.broadcast_to(acc[:, 64:65], acc.shape)
    lane = lax.broadcasted_iota(jnp.int32, (1, 128), 1)
    halves = []
    for hp in range(2):
        d = [norm[(4 * hp + 2 * j) * tq:(4 * hp + 2 * j + 1) * tq]
             - lam * norm[(4 * hp + 2 * j + 1) * tq:(4 * hp + 2 * j + 2) * tq] for j in range(2)]
        halves.append(jnp.where(lane < 64, d[0], pltpu.roll(d[1], 64, 1)))
    o = jnp.concatenate(halves, -1)
    return _group_rms(o, g64, 64, gain) * post_scale


def _da_tile(qs, k, vx, m, acc, tq):
    tk = k.shape[0]
    s = lax.dot_general(qs, k, NT_DIMS, preferred_element_type=jnp.float32)
    m_new = jnp.maximum(m, jnp.max(s, -1, keepdims=True))
    alpha = jnp.exp2(m - m_new)
    p = jnp.exp2(s - pltpu.repeat(m_new, tk // 128, axis=1)).astype(jnp.bfloat16)
    pv = jnp.concatenate(
        [jnp.dot(p[2 * h * tq:(2 * h + 2) * tq], vx[:, h * 128:(h + 1) * 128], preferred_element_type=jnp.float32)
         for h in range(4)], 0)
    return m_new, alpha * acc + pv


def _da_kernel(lam_ref, q_ref, k_ref, vx_ref, kc_ref, vxc_ref, gain_ref, g64_ref, o_ref, m_ref, acc_ref,
               *, tk, n_kv, post_scale):
    tq = q_ref.shape[0]
    qs = _stack_masked_q(q_ref[...])
    m_ref[...] = jnp.full(m_ref.shape, -jnp.inf, jnp.float32)
    acc_ref[...] = jnp.zeros(acc_ref.shape, jnp.float32)

    def body(j, carry):
        for u in range(2):
            off = pl.multiple_of((2 * j + u) * tk, tk)
            m, acc = _da_tile(qs, k_ref[pl.ds(off, tk), :], vx_ref[pl.ds(off, tk), :], m_ref[...], acc_ref[...], tq)
            m_ref[...] = m
            acc_ref[...] = acc
        return carry

    lax.fori_loop(0, n_kv // 2 - 1, body, 0, unroll=True)
    off = (n_kv - 2) * tk
    m, acc = _da_tile(qs, k_ref[off:off + tk, :], vx_ref[off:off + tk, :], m_ref[...], acc_ref[...], tq)
    k_last = jnp.concatenate([k_ref[off + tk:off + 2 * tk, :], kc_ref[...]], 0)
    vx_last = jnp.concatenate([vx_ref[off + tk:off + 2 * tk, :], vxc_ref[...]], 0)
    m, acc = _da_tile(qs, k_last, vx_last, m, acc, tq)
    o_ref[...] = _da_finish(acc, lam_ref[0], gain_ref[...], g64_ref[...], post_scale).astype(o_ref.dtype)


def da_call(lam, DAQ, DAK, DAVX, subln_g, *, n_batch, seq, ctx_len, lam_init, tq=256, tk=1024):
    tk = min(tk, seq // 2)
    nq = seq // tq
    cb = n_batch * seq // ctx_len
    g64 = _group_ones(256, 64)
    gain = jnp.tile(subln_g, 4).reshape(1, 256)
    n_kv = seq // tk
    assert n_kv % 2 == 0
    kern = functools.partial(_da_kernel, tk=tk, n_kv=n_kv, post_scale=1.0 - lam_init)
    return pl.pallas_call(
        kern,
        grid=(n_batch, nq),
        in_specs=[pl.BlockSpec(memory_space=pltpu.SMEM),
                  pl.BlockSpec((tq, 256), lambda b, i: (b * nq + i, 0)),
                  pl.BlockSpec((seq, 256), lambda b, i: (b, 0)),
                  pl.BlockSpec((seq, 512), lambda b, i: (b, 0)),
                  pl.BlockSpec((ctx_len, 256), lambda b, i: (cb + b, 0)),
                  pl.BlockSpec((ctx_len, 512), lambda b, i: (cb + b, 0)),
                  pl.BlockSpec((1, 256), lambda b, i: (0, 0)),
                  pl.BlockSpec((256, 256), lambda b, i: (0, 0))],
        out_specs=pl.BlockSpec((tq, 256), lambda b, i: (b * nq + i, 0)),
        out_shape=jax.ShapeDtypeStruct((n_batch * seq, 256), jnp.bfloat16),
        scratch_shapes=[pltpu.VMEM((8 * tq, 128), jnp.float32), pltpu.VMEM((8 * tq, 128), jnp.float32)],
        compiler_params=pltpu.CompilerParams(vmem_limit_bytes=VMEM_LIMIT),
        name="da_attn",
    )(lam.reshape(1), DAQ, DAK, DAVX, DAK, DAVX, gain, g64)


def _ctx_kernel(lam_ref, naq_ref, nak_ref, nav_ref, daq_ref, dak_ref, dav_ref, gain_ref, g64_ref, ya_ref, yd_ref,
                *, post_scale):
    q = naq_ref[...]
    k = nak_ref[...]
    v = nav_ref[...]
    outs = []
    for h in range(4):
        sl = slice(h * 64, (h + 1) * 64)
        s = lax.dot_general(q[:, sl], k[:, sl], NT_DIMS, preferred_element_type=jnp.float32)
        p = jnp.exp(s - jnp.max(s, -1, keepdims=True))
        o = jnp.dot(p.astype(jnp.bfloat16), v[:, sl], preferred_element_type=jnp.float32)
        outs.append(o / jnp.sum(p, -1, keepdims=True))
    ya_ref[...] = jnp.concatenate(outs, -1).astype(ya_ref.dtype)
    tq = daq_ref.shape[0]
    qs = _stack_masked_q(daq_ref[...])
    m0 = jnp.full((8 * tq, 128), -jnp.inf, jnp.float32)
    m, acc = _da_tile(qs, dak_ref[...], dav_ref[...], m0, jnp.zeros((8 * tq, 128), jnp.float32), tq)
    yd_ref[...] = _da_finish(acc, lam_ref[0], gain_ref[...], g64_ref[...], post_scale).astype(yd_ref.dtype)


def ctx_attn_call(lam, NAQ, NAK, NAV, DAQ, DAK, DAVX, subln_g, *, n_batch, seq, ctx_len, lam_init):
    cb = n_batch * seq // ctx_len
    g64 = _group_ones(256, 64)
    gain = jnp.tile(subln_g, 4).reshape(1, 256)
    blk = lambda col: pl.BlockSpec((ctx_len, 256), lambda b: (cb + b, col))
    return pl.pallas_call(
        functools.partial(_ctx_kernel, post_scale=1.0 - lam_init),
        grid=(n_batch,),
        in_specs=[pl.BlockSpec(memory_space=pltpu.SMEM), blk(0), blk(0), blk(0), blk(0), blk(0),
                  pl.BlockSpec((ctx_len, 512), lambda b: (cb + b, 0)),
                  pl.BlockSpec((1, 256), lambda b: (0, 0)), pl.BlockSpec((256, 256), lambda b: (0, 0))],
        out_specs=[pl.BlockSpec((ctx_len, 256), lambda b: (b, 0))] * 2,
        out_shape=[jax.ShapeDtypeStruct((n_batch * ctx_len, 256), jnp.bfloat16)] * 2,
        name="ctx_attn",
    )(lam.reshape(1), NAQ, NAK, NAV, DAQ, DAK, DAVX, gain, g64)


def _split_bf16(a):
    hi = a.astype(jnp.bfloat16)
    return hi, (a - hi.astype(jnp.float32)).astype(jnp.bfloat16)


def _route(logits_t, bias_col):
    per_group = N_EXPERTS // N_GROUPS
    scores = jax.nn.sigmoid(logits_t)
    sel_all = scores + bias_col
    sel = [sel_all[e:e + 1, :] for e in range(N_EXPERTS)]
    top2 = []
    for e in range(N_EXPERTS):
        g0 = e // per_group * per_group
        rank = jnp.zeros_like(sel[e])
        for o in range(g0, g0 + per_group):
            if o != e:
                beats = (sel[o] > sel[e]) | (sel[o] == sel[e]) if o < e else (sel[o] > sel[e])
                rank = rank + beats.astype(jnp.float32)
        top2.append(rank < 1.5)
    gs = []
    for g in range(N_GROUPS):
        acc = jnp.zeros_like(sel[0])
        for e in range(g * per_group, (g + 1) * per_group):
            acc = acc + jnp.where(top2[e], sel[e], 0.0)
        gs.append(acc)
    rows = []
    for g in range(N_GROUPS):
        beaten = jnp.zeros_like(sel[0])
        for o in range(N_GROUPS):
            if o != g:
                b = (gs[o] >= gs[g]) if o < g else (gs[o] > gs[g])
                beaten = beaten + b.astype(jnp.float32)
        best = beaten < 0.5
        for e in range(g * per_group, (g + 1) * per_group):
            rows.append(jnp.where(best & top2[e], scores[e:e + 1, :], 0.0))
    w = jnp.concatenate(rows, 0)
    return w / jnp.sum(w, axis=0, keepdims=True)


def _merge_kernel(xl_ref, xc_ref, yal_ref, yac_ref, ydl_ref, ydc_ref, yfl_ref, yfc_ref, of_ref, ob_ref, z_ref, gate_ref,
                  m_ref, g2_ref, og_ref, g64_ref, wb_ref, wo_ref, wr_ref, rb_ref, xo_ref, h2_ref, rw_ref,
                  *, tiles_per_batch, n_lat_tiles, n_batch):
    i = pl.program_id(0)
    r = jnp.where(i < n_lat_tiles, i // tiles_per_batch, n_batch)
    z = z_ref[...].astype(jnp.float32)
    yg = (_group_rms(of_ref[...] + ob_ref[...], g64_ref[...], GDN_DIM, og_ref[...]) * (z * jax.nn.sigmoid(z)))
    is_lat = i < n_lat_tiles
    pick = lambda lat_ref, ctx_ref: jnp.where(is_lat, lat_ref[...], ctx_ref[...])
    branches = (pick(yal_ref, yac_ref), pick(ydl_ref, ydc_ref), yg.astype(jnp.bfloat16),
                pick(yfl_ref, yfc_ref).astype(jnp.bfloat16))
    mix = None
    for b, yb in enumerate(branches):
        proj = jnp.dot(yb, wb_ref[b], preferred_element_type=jnp.float32)
        term = gate_ref[:, b * D_MODEL:(b + 1) * D_MODEL].astype(jnp.float32) * proj
        mix = term if mix is None else mix + term
    y = jnp.dot(mix.astype(jnp.bfloat16), wo_ref[...], preferred_element_type=jnp.float32)
    mrow = lambda k: m_ref[pl.ds(r, 1), k * D_MODEL:(k + 1) * D_MODEL]
    x = jnp.where(i < n_lat_tiles, xl_ref[...], xc_ref[...]) + mrow(2) * y
    xo_ref[...] = x
    h2 = (x * lax.rsqrt(jnp.mean(x * x, axis=-1, keepdims=True) + RMS_EPS) * g2_ref[...]) * (1.0 + mrow(4)) + mrow(3)
    h2_ref[...] = h2.astype(h2_ref.dtype)
    h_hi, h_lo = _split_bf16(h2)
    w_hi, w_lo = _split_bf16(wr_ref[...])
    nt = lambda a, b: lax.dot_general(a, b, NT_DIMS, preferred_element_type=jnp.float32)
    logits_t = nt(w_hi, h_hi) + nt(w_hi, h_lo) + nt(w_lo, h_hi)
    w_t = _route(logits_t, rb_ref[...])
    pad = jnp.zeros((128 - N_EXPERTS, w_t.shape[1]), jnp.float32)
    rw_ref[...] = jnp.transpose(jnp.concatenate([w_t, pad], 0))


def merge_call(Xl, Xc, ctx_row0, Ya, Yd, Yf, Of, Ob, P_z, P_gate, mod_l, g2n, onorm_g, w_branch, w_out, w_router,
               router_bias, *, n_tok, n_batch, seq, tm=256):
    D = Xl.shape[1]
    n_lat_tiles = n_batch * seq // tm
    kern = functools.partial(_merge_kernel, tiles_per_batch=seq // tm, n_lat_tiles=n_lat_tiles, n_batch=n_batch)
    row = lambda w: pl.BlockSpec((tm, w), lambda i: (i, 0))
    full = lambda a: pl.BlockSpec(a.shape, lambda i: (0,) * a.ndim)
    wr_t = jnp.transpose(w_router)
    rb = router_bias.reshape(N_EXPERTS, 1)
    g2 = g2n.reshape(1, D)
    og = jnp.tile(onorm_g, GDN_HEADS).reshape(1, BRANCH_W)
    g64 = _group_ones(BRANCH_W, GDN_DIM)
    return pl.pallas_call(
        kern,
        grid=(n_tok // tm,),
        in_specs=_stream_specs(n_lat_tiles, ctx_row0 // tm, tm, D) + 3 * _stream_specs(n_lat_tiles, 0, tm, 256) + [
                  row(256), row(256), row(256), row(4 * D), full(mod_l),
                  full(g2), full(og), full(g64), full(w_branch), full(w_out), full(wr_t), full(rb)],
        out_specs=[row(D), row(D), row(128)],
        out_shape=[jax.ShapeDtypeStruct((n_tok, D), jnp.float32), jax.ShapeDtypeStruct((n_tok, D), jnp.bfloat16),
                   jax.ShapeDtypeStruct((n_tok, 128), jnp.float32)],
        compiler_params=pltpu.CompilerParams(vmem_limit_bytes=VMEM_LIMIT),
        name="merge",
    )(Xl, Xc, *Ya, *Yd, *Yf, Of, Ob, P_z, P_gate, mod_l, g2, og, g64, w_branch, w_out, wr_t, rb)


MOE_TILE = 1024
MOE_WIN = 320
MOE_WIN_CTX = 176
MOE_EPS = 4


def _moe_sorted_kernel(x_ref, h_ref, rw_ref, m_ref, wg_ref, wu_ref, wd_ref, o_ref, xs_ref, ws_ref, ys_ref, dest_ref,
                       seg_ref, *, win, mod_row_fn):
    T = h_ref.shape[0]
    per_group = N_EXPERTS // N_GROUPS
    i = pl.program_id(0)
    step = pl.program_id(1)
    lane = lax.broadcasted_iota(jnp.int32, (1, 128), 1)
    f32, bf16 = jnp.float32, jnp.bfloat16

    @pl.when(step == 0)
    def _():
        rw = rw_ref[...]
        r128 = lax.broadcasted_iota(jnp.int32, (128, 128), 0)
        c128 = lax.broadcasted_iota(jnp.int32, (128, 128), 1)
        e2g = ((r128 // per_group == c128) & (r128 < N_EXPERTS)).astype(bf16)
        og = (jnp.dot((rw > 0.0).astype(bf16), e2g, preferred_element_type=f32) > 0.5).astype(f32)
        rt = lax.broadcasted_iota(jnp.int32, (T, T), 0)
        ct = lax.broadcasted_iota(jnp.int32, (T, T), 1)
        earlier = jnp.dot((rt > ct).astype(bf16), og.astype(bf16), preferred_element_type=f32)
        cnt_row = jnp.sum(og, axis=0, keepdims=True)
        start_row = jnp.dot(jnp.broadcast_to(cnt_row, (8, 128)), (r128 < c128).astype(f32),
                            preferred_element_type=f32, precision=HI)[0:1]
        dest = jnp.sum(og * (earlier + start_row), axis=-1, keepdims=True)
        dest_ref[...] = jnp.broadcast_to(dest, (T, 128))
        dest_row = jnp.transpose(dest_ref[...])[0:1, :]
        perm = (dest_row == rt.astype(f32)).astype(bf16)
        xs_ref[...] = jnp.dot(perm, h_ref[...], preferred_element_type=f32).astype(bf16)
        ws_ref[...] = sum(jnp.dot(perm, piece, preferred_element_type=f32) for piece in _split_bf16(rw))
        ys_ref[...] = jnp.zeros(ys_ref.shape, f32)
        for g in range(N_GROUPS):
            seg_ref[g] = jnp.sum(jnp.where(lane == g, start_row, 0.0)).astype(jnp.int32)
            seg_ref[N_GROUPS + g] = jnp.sum(jnp.where(lane == g, cnt_row, 0.0)).astype(jnp.int32)

    g = (step * MOE_EPS) // per_group
    start = seg_ref[g]
    end = start + seg_ref[N_GROUPS + g]
    s0 = (start // 16) * 16
    n_win = (end - s0 + win - 1) // win

    def window(w, carry):
        lo = s0 + w * win
        off = pl.multiple_of(jnp.minimum(lo, T - win), 16)
        xw = xs_ref[pl.ds(off, win), :]
        ww = ws_ref[pl.ds(off, win), :]
        row = off + lax.broadcasted_iota(jnp.int32, (win, 1), 0)
        y = None
        for j in range(MOE_EPS):
            w_col = jnp.sum(jnp.where(lane == step * MOE_EPS + j, ww, 0.0), axis=-1, keepdims=True)
            w_col = jnp.where(row >= lo, w_col, 0.0)
            a = jnp.dot(xw, wg_ref[0, j], preferred_element_type=f32)
            u = jnp.dot(xw, wu_ref[0, j], preferred_element_type=f32)
            act = (a * jax.nn.sigmoid(a) * u * w_col).astype(bf16)
            yj = jnp.dot(act, wd_ref[0, j], preferred_element_type=f32)
            y = yj if y is None else y + yj
        ys_ref[pl.ds(off, win), :] += y
        return carry

    lax.fori_loop(0, n_win, window, 0)

    @pl.when(step == N_EXPERTS // MOE_EPS - 1)
    def _():
        ct = lax.broadcasted_iota(jnp.int32, (T, T), 1)
        unperm = (pltpu.repeat(dest_ref[...], T // 128, axis=1) == ct.astype(f32)).astype(bf16)
        f = jnp.dot(unperm, ys_ref[...].astype(bf16), preferred_element_type=f32)
        g2 = m_ref[pl.ds(mod_row_fn(i), 1), 5 * D_MODEL:6 * D_MODEL]
        o_ref[...] = x_ref[...] + g2 * f


def moe_sorted_call(Xmid, H2, RW, mod_l, w_gate, w_up, w_down, *, layer, row0, n_rows, tile, win, mod_row_fn):
    D = Xmid.shape[1]
    F = w_gate.shape[-1]
    b0 = row0 // tile
    tok = lambda w: pl.BlockSpec((tile, w), lambda i, e: (b0 + i, 0))
    kern = functools.partial(_moe_sorted_kernel, win=win, mod_row_fn=mod_row_fn)
    return pl.pallas_call(
        kern,
        grid=(n_rows // tile, N_EXPERTS // MOE_EPS),
        in_specs=[tok(D), tok(D), tok(128),
                  pl.BlockSpec(mod_l.shape, lambda i, e: (0, 0)),
                  pl.BlockSpec((1, MOE_EPS, D, F), lambda i, e: (layer, e, 0, 0)),
                  pl.BlockSpec((1, MOE_EPS, D, F), lambda i, e: (layer, e, 0, 0)),
                  pl.BlockSpec((1, MOE_EPS, F, D), lambda i, e: (layer, e, 0, 0))],
        out_specs=pl.BlockSpec((tile, D), lambda i, e: (i, 0)),
        out_shape=jax.ShapeDtypeStruct((n_rows, D), jnp.float32),
        scratch_shapes=[pltpu.VMEM((tile, D), jnp.bfloat16), pltpu.VMEM((tile, 128), jnp.float32),
                        pltpu.VMEM((tile, D), jnp.float32), pltpu.VMEM((tile, 128), jnp.float32),
                        pltpu.SMEM((2 * N_GROUPS,), jnp.int32)],
        compiler_params=pltpu.CompilerParams(vmem_limit_bytes=VMEM_LIMIT),
        name="moe_sorted",
    )(Xmid, H2, RW, mod_l, w_gate, w_up, w_down)


GDN_TILE = 256
GDN_CPT = GDN_TILE // GDN_CHUNK
GDN_LOCKSTEP = 2


def _mm(a, b, passes=1, dims=None):
    if dims is None:
        dot = lambda x, y: jnp.dot(x, y, preferred_element_type=jnp.float32)
    else:
        dot = lambda x, y: lax.dot_general(x, y, dims, preferred_element_type=jnp.float32)
    if passes == 1:
        return dot(a.astype(jnp.bfloat16), b.astype(jnp.bfloat16))
    a_hi, a_lo = _split_bf16(a)
    b_hi, b_lo = _split_bf16(b)
    return dot(a_hi, b_hi) + dot(a_hi, b_lo) + dot(a_lo, b_hi)


def _stack_heads(x):
    lane = lax.broadcasted_iota(jnp.int32, (1, 256), 1)
    return jnp.concatenate([jnp.where((lane // GDN_DIM) == h, x, 0.0) for h in range(GDN_HEADS)], 0)


def _slab(x):
    return x[0:64] + x[64:128] + x[128:192] + x[192:256]


def _unit_tri_inverse(mats):
    shape = mats[0].shape
    eye = (lax.broadcasted_iota(jnp.int32, shape, 0) == lax.broadcasted_iota(jnp.int32, shape, 1)).astype(jnp.float32)
    ps = [eye - a for a in mats]
    pws = list(mats)
    for _ in range(5):
        pws = [_mm(pw, pw, 3) for pw in pws]
        ps = [p + _mm(p, pw, 3) for p, pw in zip(ps, pws)]
    return ps


def _gdn_chunk_kernel(x_ref, xp_ref, xn_ref, ab_ref, cw_ref, par_ref, g64_ref, qe_ref, ou_ref, mm_ref, nn_ref,
                      *, tiles_per_batch, n_lat_tiles):
    i = pl.program_id(0)
    is_lat = i < n_lat_tiles
    first = jnp.where(is_lat, (i % tiles_per_batch) == 0, True)
    last = jnp.where(is_lat, (i % tiles_per_batch) == tiles_per_batch - 1, True)
    xp = jnp.concatenate([jnp.where(first, 0.0, xp_ref[...]), x_ref[...], jnp.where(last, 0.0, xn_ref[...])], 0)
    y = xp[6:6 + GDN_TILE] * cw_ref[0:1, :]
    for t in range(1, 5):
        y = y + xp[6 + t:6 + t + GDN_TILE] * cw_ref[t:t + 1, :]
    y = y * jax.nn.sigmoid(y)
    g64 = g64_ref[...]
    q = y[:, 0:256]
    k = y[:, 256:512]
    v = y[:, 512:768]
    q = q * lax.rsqrt(_group_sum(q * q, g64) + RMS_EPS) * (GDN_DIM ** -0.5)
    k = k * lax.rsqrt(_group_sum(k * k, g64) + RMS_EPS)
    ab = ab_ref[...]
    lane128 = lax.broadcasted_iota(jnp.int32, (1, 128), 1)
    g_all = jnp.where(lane128 < 8, par_ref[0:1, :] * jax.nn.softplus(ab + par_ref[1:2, :]), 0.0)
    beta_all = jax.nn.sigmoid(ab)
    r64 = lax.broadcasted_iota(jnp.int32, (64, 64), 0)
    c64 = lax.broadcasted_iota(jnp.int32, (64, 64), 1)
    low = (r64 >= c64).astype(jnp.float32)
    upp = (r64 <= c64).astype(jnp.float32)
    rr = lax.broadcasted_iota(jnp.int32, (256, 256), 0)
    cc = lax.broadcasted_iota(jnp.int32, (256, 256), 1)
    same = (rr // 64) == (cc // 64)
    eye = (rr == cc).astype(jnp.float32)
    for c0 in range(0, GDN_CPT, GDN_LOCKSTEP):
        stage = []
        for c in range(c0, c0 + GDN_LOCKSTEP):
            sl = slice(c * GDN_CHUNK, (c + 1) * GDN_CHUNK)
            g_c = g_all[sl]
            gsum = jnp.where(lane128 < 4, jnp.dot(low, g_c, preferred_element_type=jnp.float32, precision=HI),
                             jnp.dot(upp, g_c, preferred_element_type=jnp.float32, precision=HI))
            gsum_t = jnp.transpose(jnp.concatenate([gsum, jnp.zeros_like(gsum)], 0))[:, 0:64]
            tot = jnp.sum(g_c, axis=0, keepdims=True)
            beta_c = beta_all[sl]
            q_sm = _stack_heads(q[sl])
            k_sm = _stack_heads(k[sl])
            v_sm = _stack_heads(v[sl])
            kk = _mm(k_sm, k_sm, 1, NT_DIMS)
            qk = _mm(q_sm, k_sm, 1, NT_DIMS)
            for d in range(2):
                col = lambda h: 4 * d + h
                cmat = jnp.concatenate([jnp.broadcast_to(gsum[:, col(h):col(h) + 1], (64, 256)) for h in range(4)], 0)
                rrow = jnp.concatenate([gsum_t[col(h):col(h) + 1, :] for h in range(4)], 1)
                bmat = jnp.concatenate([jnp.broadcast_to(beta_c[:, 8 + col(h):9 + col(h)], (64, 256))
                                        for h in range(4)], 0)
                tmat = jnp.concatenate([jnp.broadcast_to(tot[:, col(h):col(h) + 1], (64, 256)) for h in range(4)], 0)
                tri = (rr % 64 >= cc % 64) if d == 0 else (rr % 64 <= cc % 64)
                incl = same & tri
                strict = incl & (rr != cc)
                dec = jnp.where(incl, jnp.exp(jnp.where(incl, cmat - rrow, 0.0)), 0.0)
                stage.append((d, sl, q_sm, k_sm, v_sm, qk, cmat, bmat, tmat, dec,
                              jnp.where(strict, kk * bmat * dec, 0.0)))
        t_invs = _unit_tri_inverse([st[-1] for st in stage])
        n = len(stage)
        q_sms, k_sms, v_sms, qks, cmats, bmats, tmats, decs = (
            [st[f] for st in stage] for f in range(2, 10))
        egs = [jnp.exp(cm) for cm in cmats]
        us = [_mm(t_invs[j], v_sms[j] * bmats[j], 1) for j in range(n)]
        ws = [_mm(t_invs[j], k_sms[j] * bmats[j] * egs[j], 1) for j in range(n)]
        qkms = [decs[j] * qks[j] for j in range(n)]
        qes = [q_sms[j] * egs[j] - _mm(qkms[j], ws[j], 1) for j in range(n)]
        ous = [_mm(qkms[j], us[j], 1) for j in range(n)]
        kd_ts = [jnp.transpose(k_sms[j] * jnp.exp(tmats[j] - cmats[j])) for j in range(n)]
        m_bds = [eye * jnp.exp(tmats[j]) - _mm(kd_ts[j], ws[j], 1) for j in range(n)]
        n_bds = [_mm(kd_ts[j], us[j], 1) for j in range(n)]
        for j in range(n):
            d, sl = stage[j][0], stage[j][1]
            qe_ref[d, sl, :] = _slab(qes[j])
            ou_ref[d, sl, :] = _slab(ous[j])
            mm_ref[d, sl, :] = _slab(m_bds[j])
            nn_ref[d, sl, :] = _slab(n_bds[j])


def gdn_chunk_call(P_gdn, P_ab, conv_w, a_log, dt_bias, *, n_batch, seq):
    NT = P_gdn.shape[0]
    tpb = seq // GDN_TILE
    n_lat = n_batch * tpb
    n_tiles = NT // GDN_TILE
    hb = GDN_TILE // 8
    cw = jnp.zeros((8, 768), jnp.float32).at[:5].set(conv_w)
    par = jnp.zeros((8, 128), jnp.float32)
    par = par.at[0, :8].set(-jnp.exp(a_log.reshape(8))).at[1, :8].set(dt_bias.reshape(8))
    g64 = _group_ones(256, 64)
    kern = functools.partial(_gdn_chunk_kernel, tiles_per_batch=tpb, n_lat_tiles=n_lat)
    full = lambda a: pl.BlockSpec(a.shape, lambda i: (0,) * a.ndim)
    out = pl.BlockSpec((2, GDN_TILE, 256), lambda i: (0, i, 0))
    return pl.pallas_call(
        kern,
        grid=(n_tiles,),
        in_specs=[pl.BlockSpec((GDN_TILE, 768), lambda i: (i, 0)),
                  pl.BlockSpec((8, 768), lambda i: (jnp.maximum(i * hb - 1, 0), 0)),
                  pl.BlockSpec((8, 768), lambda i: (jnp.minimum((i + 1) * hb, n_tiles * hb - 1), 0)),
                  pl.BlockSpec((GDN_TILE, 128), lambda i: (i, 0)),
                  full(cw), full(par), full(g64)],
        out_specs=[out] * 4,
        out_shape=[jax.ShapeDtypeStruct((2, NT, 256), jnp.float32)] * 4,
        compiler_params=pltpu.CompilerParams(vmem_limit_bytes=VMEM_LIMIT),
        name="gdn_chunk",
    )(P_gdn, P_gdn, P_gdn, P_ab, cw, par, g64)


def _gdn_scan_kernel(qef_ref, ouf_ref, mf_ref, nf_ref, qeb_ref, oub_ref, mb_ref, nb_ref, of_ref, ob_ref, s_ref):
    t = pl.program_id(1)

    @pl.when(t == 0)
    def _():
        s_ref[...] = jnp.zeros(s_ref.shape, jnp.float32)

    def step(d, c, qe_ref, ou_ref, m_ref, n_ref, o_ref):
        sl = slice(c * GDN_CHUNK, (c + 1) * GDN_CHUNK)
        s = s_ref[d]
        o_ref[sl, :] = _slab(_mm(_stack_heads(qe_ref[0, sl, :]), s, 1)) + ou_ref[0, sl, :]
        s_ref[d] = _mm(_stack_heads(m_ref[0, sl, :]), s, 3) + _stack_heads(n_ref[0, sl, :])

    for c in range(GDN_CPT):
        step(0, c, qef_ref, ouf_ref, mf_ref, nf_ref, of_ref)
        step(1, GDN_CPT - 1 - c, qeb_ref, oub_ref, mb_ref, nb_ref, ob_ref)


def gdn_scan_call(QE, OU, MM, NN, *, n_batch, seq):
    NT = QE.shape[1]
    tpb = seq // GDN_TILE
    cblk = n_batch * tpb
    fwd = lambda b, t: jnp.where(t == 0, cblk + b, b * tpb + t - 1)
    bwd = lambda b, t: jnp.where(t == 0, cblk + b, b * tpb + tpb - t)
    spec = lambda d, f: pl.BlockSpec((1, GDN_TILE, 256), lambda b, t: (d, f(b, t), 0))
    ospec = lambda f: pl.BlockSpec((GDN_TILE, 256), lambda b, t: (f(b, t), 0))
    return pl.pallas_call(
        _gdn_scan_kernel,
        grid=(n_batch, tpb + 1),
        in_specs=[spec(0, fwd)] * 4 + [spec(1, bwd)] * 4,
        out_specs=[ospec(fwd), ospec(bwd)],
        out_shape=[jax.ShapeDtypeStruct((NT, 256), jnp.float32)] * 2,
        scratch_shapes=[pltpu.VMEM((2, 256, 256), jnp.float32)],
        compiler_params=pltpu.CompilerParams(vmem_limit_bytes=VMEM_LIMIT),
        name="gdn_scan",
    )(QE, OU, MM, NN, QE, OU, MM, NN)


def _dft_cs(n):
    a = 2.0 * np.pi * np.outer(np.arange(n), np.arange(n)) / n
    return np.cos(a), np.sin(a)


def _channel_dft():
    c, s = _dft_cs(FNET_GROUP_W)
    eye = np.eye(FNET_GROUPS)
    return np.concatenate([np.kron(eye, c), np.kron(eye, s)], 1)


FNET_SUB = 16


def _fnet_a_kernel(x_ref, cs_ref, m_ref, cphi_ref, sphi_ref, br_ref, bi_ref, *, n_rows):
    for j in range(FNET_SUB):
        sl = slice(j * BRANCH_W, (j + 1) * BRANCH_W)
        u = jnp.dot(x_ref[:, j, :], cs_ref[...], preferred_element_type=jnp.float32)
        st = jnp.concatenate([u[:, :BRANCH_W], u[:, BRANCH_W:]], 0).astype(jnp.bfloat16)
        a = jnp.dot(m_ref[...], st, preferred_element_type=jnp.float32)
        ar, ai = a[:n_rows], a[n_rows:]
        cp, sp = cphi_ref[:, sl], sphi_ref[:, sl]
        br_ref[j] = ar * cp - ai * sp
        bi_ref[j] = ar * sp + ai * cp


def _fnet_c_kernel(br_ref, bi_ref, m_ref, o_ref):
    for j in range(FNET_SUB):
        st = jnp.concatenate([br_ref[:, j, :], bi_ref[:, j, :]], 0).astype(jnp.bfloat16)
        o_ref[:, j, :] = jnp.dot(m_ref[...], st, preferred_element_type=jnp.float32)


def fourier_latent_call(P_f, *, n_batch, seq):
    rows = seq // GRID_W
    bf16, f32 = jnp.bfloat16, jnp.float32
    c1, s1 = _dft_cs(rows)
    norm = 1.0 / math.sqrt(seq * FNET_GROUP_W)
    m_a = jnp.asarray(np.block([[c1, -s1], [s1, c1]]) * norm, bf16)
    c2, s2 = _dft_cs(GRID_W)
    m_c = jnp.asarray(np.concatenate([c2, -s2], 1), bf16)
    cs = jnp.asarray(_channel_dft(), bf16)
    phi = 2.0 * np.pi * np.outer(np.arange(rows), np.arange(GRID_W)) / seq
    cphi = jnp.repeat(jnp.asarray(np.cos(phi), f32), BRANCH_W, axis=1)
    sphi = jnp.repeat(jnp.asarray(np.sin(phi), f32), BRANCH_W, axis=1)
    xv = P_f.reshape(P_f.shape[0] // GRID_W, GRID_W, BRANCH_W)
    nj = GRID_W // FNET_SUB
    full = lambda a: pl.BlockSpec(a.shape, lambda b, j: (0,) * a.ndim)
    tw = pl.BlockSpec((rows, FNET_SUB * BRANCH_W), lambda b, j: (0, j))
    mid = jax.ShapeDtypeStruct((n_batch * GRID_W, rows, BRANCH_W), f32)
    br, bi = pl.pallas_call(
        functools.partial(_fnet_a_kernel, n_rows=rows),
        grid=(n_batch, nj),
        in_specs=[pl.BlockSpec((rows, FNET_SUB, BRANCH_W), lambda b, j: (b, j, 0)), full(cs), full(m_a), tw, tw],
        out_specs=[pl.BlockSpec((FNET_SUB, rows, BRANCH_W), lambda b, j: (b * nj + j, 0, 0))] * 2,
        out_shape=[mid, mid],
        name="fnet_rows",
    )(xv, cs, m_a, cphi, sphi)
    blk = pl.BlockSpec((GRID_W, FNET_SUB, BRANCH_W), lambda b, i: (b, i, 0))
    y = pl.pallas_call(
        _fnet_c_kernel,
        grid=(n_batch, rows // FNET_SUB),
        in_specs=[blk, blk, pl.BlockSpec(m_c.shape, lambda b, i: (0, 0))],
        out_specs=blk,
        out_shape=mid,
        name="fnet_cols",
    )(br, bi, m_c)
    return y.reshape(n_batch * seq, BRANCH_W)


def _fnet_ctx_kernel(x_ref, cs_ref, m_ref, o_ref):
    u = jnp.dot(x_ref[...], cs_ref[...], preferred_element_type=jnp.float32)
    st = jnp.concatenate([u[:, :BRANCH_W], u[:, BRANCH_W:]], 0).astype(jnp.bfloat16)
    o_ref[...] = jnp.dot(m_ref[...], st, preferred_element_type=jnp.float32).astype(o_ref.dtype)


def fourier_ctx_call(P_f, *, n_batch, seq, ctx_len):
    bf16 = jnp.bfloat16
    c, s = _dft_cs(ctx_len)
    m = jnp.asarray(np.concatenate([c, -s], 1) / math.sqrt(ctx_len * FNET_GROUP_W), bf16)
    cs = jnp.asarray(_channel_dft(), bf16)
    cb = n_batch * seq // ctx_len
    return pl.pallas_call(
        _fnet_ctx_kernel,
        grid=(n_batch,),
        in_specs=[pl.BlockSpec((ctx_len, BRANCH_W), lambda b: (cb + b, 0)),
                  pl.BlockSpec(cs.shape, lambda b: (0, 0)), pl.BlockSpec(m.shape, lambda b: (0, 0))],
        out_specs=pl.BlockSpec((ctx_len, BRANCH_W), lambda b: (b, 0)),
        out_shape=jax.ShapeDtypeStruct((n_batch * ctx_len, BRANCH_W), jnp.float32),
        name="fnet_ctx",
    )(P_f, cs, m)


def kernel(x, c, ctx, c_ctx, w_mod, b_mod, norm1_g, norm2_g, w_in, na_qn_g, na_kn_g, na_rpb,
           da_qn_g, da_kn_g, da_lam_q1, da_lam_k1, da_lam_q2, da_lam_k2, da_subln_g, gdn_conv_w,
           gdn_a_log, gdn_dt_bias, gdn_onorm_g, w_branch, w_out, w_router, router_bias,
           moe_w_gate, moe_w_up, moe_w_down):
    B, S, D = x.shape
    L = ctx.shape[1]
    NL = B * S
    f32 = jnp.float32
    cvec = jnp.zeros((8, D), f32).at[:B].set(c).at[B].set(c_ctx)
    mods = mod_call(cvec, w_mod, b_mod)
    cos, sin = rope_tables(S, 256)
    bf16 = jnp.bfloat16
    Xl, Xc = x.reshape(NL, D), ctx.reshape(B * L, D)
    experts = (moe_w_gate.astype(bf16), moe_w_up.astype(bf16), moe_w_down.astype(bf16))
    w_t = jnp.swapaxes(w_in, 1, 2).astype(bf16)
    for l in range(DEPTH):
        last = l == DEPTH - 1
        lam_init = 0.8 - 0.6 * math.exp(-0.3 * l)
        NAQ, NAK, NAV, DAQ, DAK, DAVX, P_gdn, P_z, P_f, P_gate, P_ab = inproj_call(
            Xl, Xc, 0, mods[l], norm1_g[l], w_t, l, cos, sin, na_qn_g[l], na_kn_g[l], da_qn_g[l], da_kn_g[l],
            n_batch=B, seq=S, ctx_len=L)
        bias = na_bias_tables(na_rpb[l], S // GRID_W)
        lam = (jnp.exp(jnp.sum(da_lam_q1[l] * da_lam_k1[l])) - jnp.exp(jnp.sum(da_lam_q2[l] * da_lam_k2[l]))
               + lam_init).astype(f32)
        Ya = na_call(NAQ, NAK, NAV, bias, n_batch=B, seq=S, ctx_len=L)
        Yd = da_call(lam, DAQ, DAK, DAVX, da_subln_g[l], n_batch=B, seq=S, ctx_len=L, lam_init=lam_init)
        QE, OU, MM, NN = gdn_chunk_call(P_gdn, P_ab, gdn_conv_w[l], gdn_a_log[l], gdn_dt_bias[l], n_batch=B, seq=S)
        Of, Ob = gdn_scan_call(QE, OU, MM, NN, n_batch=B, seq=S)
        Yf = fourier_latent_call(P_f, n_batch=B, seq=S)
        n_tok = NL
        Yac, Ydc, Yfc = Ya, Yd, Yf
        if not last:
            Yac, Ydc = ctx_attn_call(lam, NAQ, NAK, NAV, DAQ, DAK, DAVX, da_subln_g[l],
                                     n_batch=B, seq=S, ctx_len=L, lam_init=lam_init)
            Yfc = fourier_ctx_call(P_f, n_batch=B, seq=S, ctx_len=L)
            n_tok = NL + B * L
        Xmid, H2, RW = merge_call(Xl, Xc, 0, (Ya, Yac), (Yd, Ydc), (Yf, Yfc), Of, Ob, P_z, P_gate, mods[l],
                                  norm2_g[l], gdn_onorm_g[l], w_branch[l].astype(bf16), w_out[l].astype(bf16),
                                  w_router, router_bias, n_tok=n_tok, n_batch=B, seq=S)
        Xl = moe_sorted_call(Xmid, H2, RW, mods[l], *experts, layer=l, row0=0, n_rows=NL, tile=MOE_TILE, win=MOE_WIN,
                             mod_row_fn=lambda i: i // (S // MOE_TILE))
        if not last:
            Xc = moe_sorted_call(Xmid, H2, RW, mods[l], *experts, layer=l, row0=NL, n_rows=B * L, tile=B * L,
                                 win=MOE_WIN_CTX, mod_row_fn=lambda i: B)
    return Xl.reshape(B, S, D)
```

```python
import functools
import math
import jax
import jax.numpy as jnp
from jax import lax
import numpy as np
from jax.experimental import pallas as pl
from jax.experimental.pallas import tpu as pltpu

D_MODEL = 1024
DEPTH = 2
GRID_W = 64
N_BRANCH = 4
BRANCH_W = D_MODEL // N_BRANCH
GDN_HEADS = 4
GDN_DIM = BRANCH_W // GDN_HEADS
GDN_CHUNK = 64
FNET_GROUPS = 4
FNET_GROUP_W = BRANCH_W // FNET_GROUPS
N_EXPERTS = 16
N_GROUPS = 4
RMS_EPS = 1e-6
NEG_INF = -1e30
ROPE_BASE = 10000.0
VMEM_LIMIT = 56 * 1024 * 1024
W_AB0 = 2560
HI = lax.Precision.HIGHEST
NT_DIMS = (((1,), (1,)), ((), ()))
LOG2E = 1.4426950408889634


def _mod_kernel(c_ref, w_ref, b_ref, o_ref):
    c = c_ref[...]
    a = c * jax.nn.sigmoid(c)
    o_ref[0] = _mm(a, w_ref[0], 3) + b_ref[0]


def mod_call(cvec, w_mod, b_mod, tn=1024):
    depth, D, N = w_mod.shape
    return pl.pallas_call(
        _mod_kernel,
        grid=(depth, N // tn),
        in_specs=[pl.BlockSpec((8, D), lambda l, j: (0, 0)),
                  pl.BlockSpec((1, D, tn), lambda l, j: (l, 0, j)),
                  pl.BlockSpec((1, 1, tn), lambda l, j: (l, 0, j))],
        out_specs=pl.BlockSpec((1, 8, tn), lambda l, j: (l, 0, j)),
        out_shape=jax.ShapeDtypeStruct((depth, 8, N), jnp.float32),
        name="mod",
    )(cvec, w_mod, b_mod.reshape(depth, 1, N))


def _stream_specs(n_lat_tiles, ctx_tile0, tm, width):
    return [pl.BlockSpec((tm, width), lambda i: (jnp.minimum(i, n_lat_tiles - 1), 0)),
            pl.BlockSpec((tm, width), lambda i: (ctx_tile0 + jnp.maximum(i - n_lat_tiles, 0), 0))]


def _inproj_kernel(xl_ref, xc_ref, m_ref, g_ref, wt_ref, cos_ref, sin_ref, g64_ref, g32_ref, gains_ref,
                   naq_ref, nak_ref, nav_ref, daq_ref, dak_ref, davx_ref, gdn_ref, z_ref, f_ref, gate_ref, ab_ref,
                   *, tiles_per_batch, n_lat_tiles, n_batch):
    i = pl.program_id(0)
    r = jnp.where(i < n_lat_tiles, i // tiles_per_batch, n_batch)
    x = jnp.where(i < n_lat_tiles, xl_ref[...], xc_ref[...])
    y = x * lax.rsqrt(jnp.mean(x * x, axis=-1, keepdims=True) + RMS_EPS) * g_ref[...]
    sh = m_ref[pl.ds(r, 1), 0:D_MODEL]
    sc = m_ref[pl.ds(r, 1), D_MODEL:2 * D_MODEL]
    h = (y * (1.0 + sc) + sh).astype(jnp.bfloat16)
    proj = lambda a, b: lax.dot_general(h, wt_ref[0, a:b, :], NT_DIMS, preferred_element_type=jnp.float32)
    proj_b = lambda a, b: proj(W_AB0 + 16 + a, W_AB0 + 16 + b)
    bf16 = jnp.bfloat16

    g64 = g64_ref[...]
    naq_ref[...] = (_group_rms(proj(0, 256), g64, 64, gains_ref[0:1, :]) * (64 ** -0.5)).astype(bf16)
    nak_ref[...] = _group_rms(proj(256, 512), g64, 64, gains_ref[1:2, :]).astype(bf16)
    nav_ref[...] = proj(512, 768).astype(bf16)

    g32 = g32_ref[...]
    cos = cos_ref[...]
    sin = sin_ref[...]
    lane = lax.broadcasted_iota(jnp.int32, (1, 256), 1)
    first = (lane % 16) < 8

    def rope(t):
        swapped = jnp.where(first, pltpu.roll(t, 256 - 8, 1), pltpu.roll(t, 8, 1))
        return t * cos + swapped * sin

    q = rope(_group_rms(proj(768, 1024), g32, 32, gains_ref[2:3, :]))
    k = rope(_group_rms(proj(1024, 1280), g32, 32, gains_ref[3:4, :]))
    daq_ref[...] = (q * (32 ** -0.5 * LOG2E)).astype(bf16)
    dak_ref[...] = k.astype(bf16)
    v = proj(1280, 1536).astype(bf16)
    one_col = (lax.broadcasted_iota(jnp.int32, (v.shape[0], 64), 1) == 0).astype(bf16)
    davx_ref[...] = jnp.concatenate(
        [piece for hd in range(4) for piece in (v[:, hd * 64:(hd + 1) * 64], one_col)], -1)

    gdn_ref[...] = proj(1536, 2304)
    z_ref[...] = proj(2304, 2560).astype(bf16)
    f_ref[...] = proj_b(0, 256).astype(bf16)
    for kk in range(4):
        gate_ref[:, kk * 1024:(kk + 1) * 1024] = jax.nn.sigmoid(
            proj_b(256 + kk * 1024, 256 + (kk + 1) * 1024)).astype(bf16)
    ab_ref[...] = proj(W_AB0, W_AB0 + 128)


def inproj_call(Xl, Xc, ctx_row0, mod_l, g, w_t, layer, cos, sin, na_qg, na_kg, da_qg, da_kg,
                *, n_batch, seq, ctx_len, tm=256):
    D = Xl.shape[1]
    NT = n_batch * (seq + ctx_len)
    tpb = seq // tm
    n_lat_tiles = n_batch * tpb
    kern = functools.partial(_inproj_kernel, tiles_per_batch=tpb, n_lat_tiles=n_lat_tiles, n_batch=n_batch)
    bf16, f32 = jnp.bfloat16, jnp.float32
    widths = [(256, bf16)] * 5 + [(512, bf16), (768, f32), (256, bf16), (256, bf16), (4096, bf16), (128, f32)]
    gains = jnp.stack([jnp.tile(na_qg, 4), jnp.tile(na_kg, 4), jnp.tile(da_qg, 8), jnp.tile(da_kg, 8)]
                      + [jnp.zeros((256,), f32)] * 4, 0)
    g64, g32 = _group_ones(256, 64), _group_ones(256, 32)
    tab = pl.BlockSpec((tm, 256), lambda i: (jnp.where(i < n_lat_tiles, i % tpb, tpb), 0))
    full = lambda a: pl.BlockSpec(a.shape, lambda i: (0,) * a.ndim)
    return pl.pallas_call(
        kern,
        grid=(NT // tm,),
        in_specs=_stream_specs(n_lat_tiles, ctx_row0 // tm, tm, D) + [
            full(mod_l), pl.BlockSpec((1, D), lambda i: (0, 0)),
            pl.BlockSpec((1,) + w_t.shape[1:], lambda i: (layer, 0, 0)), tab, tab,
            full(g64), full(g32), full(gains)],
        out_specs=[pl.BlockSpec((tm, w), lambda i: (i, 0)) for w, _ in widths],
        out_shape=[jax.ShapeDtypeStruct((NT, w), dt) for w, dt in widths],
        compiler_params=pltpu.CompilerParams(vmem_limit_bytes=VMEM_LIMIT),
        name="inproj",
    )(Xl, Xc, mod_l, g.reshape(1, D), w_t, cos, sin, g64, g32, gains)


def _group_ones(width, group):
    i = np.arange(width)
    return jnp.asarray((i[:, None] // group == i[None, :] // group).astype(np.float32), jnp.bfloat16)


def _group_sum(xx, gmat):
    hi, lo = _split_bf16(xx)
    return (jnp.dot(hi, gmat, preferred_element_type=jnp.float32)
            + jnp.dot(lo, gmat, preferred_element_type=jnp.float32))


def _group_rms(x, gmat, group, gain):
    return x * lax.rsqrt(_group_sum(x * x, gmat) * (1.0 / group) + RMS_EPS) * gain


def rope_tables(seq, tm):
    nf = 8
    t = jnp.arange(seq)
    rows = (t // GRID_W).astype(jnp.float32)
    cols = (t % GRID_W).astype(jnp.float32)
    freqs = ROPE_BASE ** (-jnp.arange(nf, dtype=jnp.float32) / nf)
    d = np.arange(32)
    f_idx = d % 8
    use_col = (d // 16) == 1
    ang = jnp.where(use_col[None, :], cols[:, None], rows[:, None]) * freqs[f_idx][None, :]
    sign = np.where((d % 16) < 8, -1.0, 1.0).astype(np.float32)
    cos = jnp.tile(jnp.cos(ang), (1, 8))
    sin = jnp.tile(jnp.sin(ang) * sign[None, :], (1, 8))
    cos = jnp.concatenate([cos, jnp.ones((tm, 256), jnp.float32)], 0)
    sin = jnp.concatenate([sin, jnp.zeros((tm, 256), jnp.float32)], 0)
    return cos, sin


NA_RB = 4
NA_UR = 11


def na_bias_tables(rpb, rows):
    reps = [(0, 0), (4, 0), (rows - NA_RB, rows - NA_UR)]
    qc = np.arange(GRID_W)[:, None]
    kc = np.arange(GRID_W)[None, :]
    ws = np.clip(qc - 8, 0, GRID_W - 16)
    vcol = (kc >= ws) & (kc < ws + 16)
    rel_c = np.clip(kc - qc + 15, 0, 30)
    sel_c = (rel_c[..., None] == np.arange(31)).astype(np.float32)
    sel_r, vrow = [], []
    for r0, u0 in reps:
        r = r0 + np.arange(NA_RB)[:, None]
        krow = u0 + np.arange(NA_UR)[None, :]
        start = np.clip(r - 4, 0, rows - 8)
        vrow.append((krow >= start) & (krow < start + 8))
        sel_r.append((np.clip(krow - r + 7, 0, 14)[..., None] == np.arange(15)).astype(np.float32))
    sel_r, vrow = np.stack(sel_r), np.stack(vrow)
    b = jnp.einsum('taui,hij,qkj->thauqk', jnp.asarray(sel_r), rpb.astype(jnp.float32), jnp.asarray(sel_c),
                   precision=HI)
    valid = vrow[:, None, :, :, None, None] & vcol[None, None, None, None, :, :]
    return jnp.where(valid, b, NEG_INF).astype(jnp.bfloat16)


def _na_kernel(q_ref, k_ref, v_ref, kc_ref, vc_ref, bias_ref, o_ref, bias_s, *, rows):
    i = pl.program_id(1)

    @pl.when((i <= 1) | (i == rows // NA_RB - 1))
    def _():
        for h in range(4):
            for a in range(NA_RB):
                for u in range(NA_UR):
                    bias_s[h, a * GRID_W:(a + 1) * GRID_W, u * GRID_W:(u + 1) * GRID_W] = (
                        bias_ref[0, h, a, u].astype(jnp.float32))

    u0 = jnp.clip(i * NA_RB - 4, 0, rows - NA_UR)
    off = pl.multiple_of(u0 * GRID_W, GRID_W)
    nkw = NA_UR * GRID_W
    kwin = k_ref[pl.ds(off, nkw), :]
    vwin = v_ref[pl.ds(off, nkw), :]
    q = q_ref[...]
    kc = kc_ref[...]
    vc = vc_ref[...]
    sls = [slice(h * 64, (h + 1) * 64) for h in range(4)]
    dot_nt = lambda x, y: lax.dot_general(x, y, NT_DIMS, preferred_element_type=jnp.float32)
    s_locs = [dot_nt(q[:, sl], kwin[:, sl]) + bias_s[h] for h, sl in enumerate(sls)]
    s_ctxs = [dot_nt(q[:, sl], kc[:, sl]) for sl in sls]
    ms = [jnp.maximum(jnp.max(a, -1, keepdims=True), jnp.max(c, -1, keepdims=True)) for a, c in zip(s_locs, s_ctxs)]
    p_locs = [jnp.exp(a - m) for a, m in zip(s_locs, ms)]
    p_ctxs = [jnp.exp(c - m) for c, m in zip(s_ctxs, ms)]
    ls = [jnp.sum(a, -1, keepdims=True) + jnp.sum(c, -1, keepdims=True) for a, c in zip(p_locs, p_ctxs)]
    outs = [(jnp.dot(pa.astype(jnp.bfloat16), vwin[:, sl], preferred_element_type=jnp.float32)
             + jnp.dot(pc.astype(jnp.bfloat16), vc[:, sl], preferred_element_type=jnp.float32)) / l
            for pa, pc, l, sl in zip(p_locs, p_ctxs, ls, sls)]
    o_ref[...] = jnp.concatenate(outs, -1).astype(o_ref.dtype)


def na_call(NAQ, NAK, NAV, bias, *, n_batch, seq, ctx_len):
    rows = seq // GRID_W
    nblk = rows // NA_RB
    tq = NA_RB * GRID_W
    cb = n_batch * seq // ctx_len
    kern = functools.partial(_na_kernel, rows=rows)
    tsel = lambda b, i: (jnp.where(i == 0, 0, jnp.where(i == nblk - 1, 2, 1)), 0, 0, 0, 0, 0)
    return pl.pallas_call(
        kern,
        grid=(n_batch, nblk),
        in_specs=[pl.BlockSpec((tq, 256), lambda b, i: (b * nblk + i, 0)),
                  pl.BlockSpec((seq, 256), lambda b, i: (b, 0)),
                  pl.BlockSpec((seq, 256), lambda b, i: (b, 0)),
                  pl.BlockSpec((ctx_len, 256), lambda b, i: (cb + b, 0)),
                  pl.BlockSpec((ctx_len, 256), lambda b, i: (cb + b, 0)),
                  pl.BlockSpec((1,) + bias.shape[1:], tsel)],
        out_specs=pl.BlockSpec((tq, 256), lambda b, i: (b * nblk + i, 0)),
        out_shape=jax.ShapeDtypeStruct((n_batch * seq, 256), jnp.bfloat16),
        scratch_shapes=[pltpu.VMEM((4, tq, NA_UR * GRID_W), jnp.float32)],
        compiler_params=pltpu.CompilerParams(vmem_limit_bytes=VMEM_LIMIT),
        name="na_attn",
    )(NAQ, NAK, NAV, NAK, NAV, bias)


def _stack_masked_q(q):
    lane = lax.broadcasted_iota(jnp.int32, (1, 256), 1)
    return jnp.concatenate([jnp.where((lane // 32) == hm, q, jnp.zeros_like(q)) for hm in range(8)], 0)


def _da_finish(acc, lam, gain, g64, post_scale):
    tq = acc.shape[0] // 8
    norm = acc / jnp.broadcast_to(acc[:, 64:65], acc.shape)
    lane = lax.broadcasted_iota(jnp.int32, (1, 128), 1)
    halves = []
    for hp in range(2):
        d = [norm[(4 * hp + 2 * j) * tq:(4 * hp + 2 * j + 1) * tq]
             - lam * norm[(4 * hp + 2 * j + 1) * tq:(4 * hp + 2 * j + 2) * tq] for j in range(2)]
        halves.append(jnp.where(lane < 64, d[0], pltpu.roll(d[1], 64, 1)))
    o = jnp.concatenate(halves, -1)
    return _group_rms(o, g64, 64, gain) * post_scale


def _da_tile(qs, k, vx, m, acc, tq):
    tk = k.shape[0]
    s = lax.dot_general(qs, k, NT_DIMS, preferred_element_type=jnp.float32)
    m_new = jnp.maximum(m, jnp.max(s, -1, keepdims=True))
    alpha = jnp.exp2(m - m_new)
    p = jnp.exp2(s - jnp.concatenate([m_new] * (tk // 128), axis=1)).astype(jnp.bfloat16)
    pv = jnp.concatenate(
        [jnp.dot(p[2 * h * tq:(2 * h + 2) * tq], vx[:, h * 128:(h + 1) * 128], preferred_element_type=jnp.float32)
         for h in range(4)], 0)
    return m_new, alpha * acc + pv


def _da_kernel(lam_ref, q_ref, k_ref, vx_ref, kc_ref, vxc_ref, gain_ref, g64_ref, o_ref, m_ref, acc_ref,
               *, tk, n_kv, post_scale):
    tq = q_ref.shape[0]
    qs = _stack_masked_q(q_ref[...])
    m_ref[...] = jnp.full(m_ref.shape, -jnp.inf, jnp.float32)
    acc_ref[...] = jnp.zeros(acc_ref.shape, jnp.float32)

    def body(j, carry):
        for u in range(2):
            off = pl.multiple_of((2 * j + u) * tk, tk)
            m, acc = _da_tile(qs, k_ref[pl.ds(off, tk), :], vx_ref[pl.ds(off, tk), :], m_ref[...], acc_ref[...], tq)
            m_ref[...] = m
            acc_ref[...] = acc
        return carry

    lax.fori_loop(0, n_kv // 2 - 1, body, 0, unroll=True)
    off = (n_kv - 2) * tk
    m, acc = _da_tile(qs, k_ref[off:off + tk, :], vx_ref[off:off + tk, :], m_ref[...], acc_ref[...], tq)
    k_last = jnp.concatenate([k_ref[off + tk:off + 2 * tk, :], kc_ref[...]], 0)
    vx_last = jnp.concatenate([vx_ref[off + tk:off + 2 * tk, :], vxc_ref[...]], 0)
    m, acc = _da_tile(qs, k_last, vx_last, m, acc, tq)
    o_ref[...] = _da_finish(acc, lam_ref[0], gain_ref[...], g64_ref[...], post_scale).astype(o_ref.dtype)


def da_call(lam, DAQ, DAK, DAVX, subln_g, *, n_batch, seq, ctx_len, lam_init, tq=256, tk=1024):
    tk = min(tk, seq // 2)
    nq = seq // tq
    cb = n_batch * seq // ctx_len
    g64 = _group_ones(256, 64)
    gain = jnp.tile(subln_g, 4).reshape(1, 256)
    n_kv = seq // tk
    assert n_kv % 2 == 0
    kern = functools.partial(_da_kernel, tk=tk, n_kv=n_kv, post_scale=1.0 - lam_init)
    return pl.pallas_call(
        kern,
        grid=(n_batch, nq),
        in_specs=[pl.BlockSpec(memory_space=pltpu.SMEM),
                  pl.BlockSpec((tq, 256), lambda b, i: (b * nq + i, 0)),
                  pl.BlockSpec((seq, 256), lambda b, i: (b, 0)),
                  pl.BlockSpec((seq, 512), lambda b, i: (b, 0)),
                  pl.BlockSpec((ctx_len, 256), lambda b, i: (cb + b, 0)),
                  pl.BlockSpec((ctx_len, 512), lambda b, i: (cb + b, 0)),
                  pl.BlockSpec((1, 256), lambda b, i: (0, 0)),
                  pl.BlockSpec((256, 256), lambda b, i: (0, 0))],
        out_specs=pl.BlockSpec((tq, 256), lambda b, i: (b * nq + i, 0)),
        out_shape=jax.ShapeDtypeStruct((n_batch * seq, 256), jnp.bfloat16),
        scratch_shapes=[pltpu.VMEM((8 * tq, 128), jnp.float32), pltpu.VMEM((8 * tq, 128), jnp.float32)],
        compiler_params=pltpu.CompilerParams(vmem_limit_bytes=VMEM_LIMIT),
        name="da_attn",
    )(lam.reshape(1), DAQ, DAK, DAVX, DAK, DAVX, gain, g64)


def _ctx_kernel(lam_ref, naq_ref, nak_ref, nav_ref, daq_ref, dak_ref, dav_ref, gain_ref, g64_ref, ya_ref, yd_ref,
                *, post_scale):
    q = naq_ref[...]
    k = nak_ref[...]
    v = nav_ref[...]
    outs = []
    for h in range(4):
        sl = slice(h * 64, (h + 1) * 64)
        s = lax.dot_general(q[:, sl], k[:, sl], NT_DIMS, preferred_element_type=jnp.float32)
        p = jnp.exp(s - jnp.max(s, -1, keepdims=True))
        o = jnp.dot(p.astype(jnp.bfloat16), v[:, sl], preferred_element_type=jnp.float32)
        outs.append(o / jnp.sum(p, -1, keepdims=True))
    ya_ref[...] = jnp.concatenate(outs, -1).astype(ya_ref.dtype)
    tq = daq_ref.shape[0]
    qs = _stack_masked_q(daq_ref[...])
    m0 = jnp.full((8 * tq, 128), -jnp.inf, jnp.float32)
    m, acc = _da_tile(qs, dak_ref[...], dav_ref[...], m0, jnp.zeros((8 * tq, 128), jnp.float32), tq)
    yd_ref[...] = _da_finish(acc, lam_ref[0], gain_ref[...], g64_ref[...], post_scale).astype(yd_ref.dtype)


def ctx_attn_call(lam, NAQ, NAK, NAV, DAQ, DAK, DAVX, subln_g, *, n_batch, seq, ctx_len, lam_init):
    cb = n_batch * seq // ctx_len
    g64 = _group_ones(256, 64)
    gain = jnp.tile(subln_g, 4).reshape(1, 256)
    blk = lambda col: pl.BlockSpec((ctx_len, 256), lambda b: (cb + b, col))
    return pl.pallas_call(
        functools.partial(_ctx_kernel, post_scale=1.0 - lam_init),
        grid=(n_batch,),
        in_specs=[pl.BlockSpec(memory_space=pltpu.SMEM), blk(0), blk(0), blk(0), blk(0), blk(0),
                  pl.BlockSpec((ctx_len, 512), lambda b: (cb + b, 0)),
                  pl.BlockSpec((1, 256), lambda b: (0, 0)), pl.BlockSpec((256, 256), lambda b: (0, 0))],
        out_specs=[pl.BlockSpec((ctx_len, 256), lambda b: (b, 0))] * 2,
        out_shape=[jax.ShapeDtypeStruct((n_batch * ctx_len, 256), jnp.bfloat16)] * 2,
        name="ctx_attn",
    )(lam.reshape(1), NAQ, NAK, NAV, DAQ, DAK, DAVX, gain, g64)


def _split_bf16(a):
    hi = a.astype(jnp.bfloat16)
    return hi, (a - hi.astype(jnp.float32)).astype(jnp.bfloat16)


def _route(logits_t, bias_col):
    per_group = N_EXPERTS // N_GROUPS
    scores = jax.nn.sigmoid(logits_t)
    sel_all = scores + bias_col
    sel = [sel_all[e:e + 1, :] for e in range(N_EXPERTS)]
    top2 = []
    for e in range(N_EXPERTS):
        g0 = e // per_group * per_group
        rank = jnp.zeros_like(sel[e])
        for o in range(g0, g0 + per_group):
            if o != e:
                beats = (sel[o] > sel[e]) | (sel[o] == sel[e]) if o < e else (sel[o] > sel[e])
                rank = rank + beats.astype(jnp.float32)
        top2.append(rank < 1.5)
    gs = []
    for g in range(N_GROUPS):
        acc = jnp.zeros_like(sel[0])
        for e in range(g * per_group, (g + 1) * per_group):
            acc = acc + jnp.where(top2[e], sel[e], 0.0)
        gs.append(acc)
    rows = []
    for g in range(N_GROUPS):
        beaten = jnp.zeros_like(sel[0])
        for o in range(N_GROUPS):
            if o != g:
                b = (gs[o] >= gs[g]) if o < g else (gs[o] > gs[g])
                beaten = beaten + b.astype(jnp.float32)
        best = beaten < 0.5
        for e in range(g * per_group, (g + 1) * per_group):
            rows.append(jnp.where(best & top2[e], scores[e:e + 1, :], 0.0))
    w = jnp.concatenate(rows, 0)
    return w / jnp.sum(w, axis=0, keepdims=True)


def _merge_kernel(xl_ref, xc_ref, yal_ref, yac_ref, ydl_ref, ydc_ref, yfl_ref, yfc_ref, of_ref, ob_ref, z_ref, gate_ref,
                  m_ref, g2_ref, og_ref, g64_ref, wb_ref, wo_ref, wr_ref, rb_ref, xo_ref, h2_ref, rw_ref,
                  *, tiles_per_batch, n_lat_tiles, n_batch):
    i = pl.program_id(0)
    r = jnp.where(i < n_lat_tiles, i // tiles_per_batch, n_batch)
    z = z_ref[...].astype(jnp.float32)
    yg = (_group_rms(of_ref[...] + ob_ref[...], g64_ref[...], GDN_DIM, og_ref[...]) * (z * jax.nn.sigmoid(z)))
    is_lat = i < n_lat_tiles
    pick = lambda lat_ref, ctx_ref: jnp.where(is_lat, lat_ref[...], ctx_ref[...])
    branches = (pick(yal_ref, yac_ref), pick(ydl_ref, ydc_ref), yg.astype(jnp.bfloat16),
                pick(yfl_ref, yfc_ref).astype(jnp.bfloat16))
    mix = None
    for b, yb in enumerate(branches):
        proj = jnp.dot(yb, wb_ref[b], preferred_element_type=jnp.float32)
        term = gate_ref[:, b * D_MODEL:(b + 1) * D_MODEL].astype(jnp.float32) * proj
        mix = term if mix is None else mix + term
    y = jnp.dot(mix.astype(jnp.bfloat16), wo_ref[...], preferred_element_type=jnp.float32)
    mrow = lambda k: m_ref[pl.ds(r, 1), k * D_MODEL:(k + 1) * D_MODEL]
    x = jnp.where(i < n_lat_tiles, xl_ref[...], xc_ref[...]) + mrow(2) * y
    xo_ref[...] = x
    h2 = (x * lax.rsqrt(jnp.mean(x * x, axis=-1, keepdims=True) + RMS_EPS) * g2_ref[...]) * (1.0 + mrow(4)) + mrow(3)
    h2_ref[...] = h2.astype(h2_ref.dtype)
    h_hi, h_lo = _split_bf16(h2)
    w_hi, w_lo = _split_bf16(wr_ref[...])
    nt = lambda a, b: lax.dot_general(a, b, NT_DIMS, preferred_element_type=jnp.float32)
    logits_t = nt(w_hi, h_hi) + nt(w_hi, h_lo) + nt(w_lo, h_hi)
    w_t = _route(logits_t, rb_ref[...])
    pad = jnp.zeros((128 - N_EXPERTS, w_t.shape[1]), jnp.float32)
    rw_ref[...] = jnp.transpose(jnp.concatenate([w_t, pad], 0))


def merge_call(Xl, Xc, ctx_row0, Ya, Yd, Yf, Of, Ob, P_z, P_gate, mod_l, g2n, onorm_g, w_branch, w_out, w_router,
               router_bias, *, n_tok, n_batch, seq, tm=256):
    D = Xl.shape[1]
    n_lat_tiles = n_batch * seq // tm
    kern = functools.partial(_merge_kernel, tiles_per_batch=seq // tm, n_lat_tiles=n_lat_tiles, n_batch=n_batch)
    row = lambda w: pl.BlockSpec((tm, w), lambda i: (i, 0))
    full = lambda a: pl.BlockSpec(a.shape, lambda i: (0,) * a.ndim)
    wr_t = jnp.transpose(w_router)
    rb = router_bias.reshape(N_EXPERTS, 1)
    g2 = g2n.reshape(1, D)
    og = jnp.tile(onorm_g, GDN_HEADS).reshape(1, BRANCH_W)
    g64 = _group_ones(BRANCH_W, GDN_DIM)
    return pl.pallas_call(
        kern,
        grid=(n_tok // tm,),
        in_specs=_stream_specs(n_lat_tiles, ctx_row0 // tm, tm, D) + 3 * _stream_specs(n_lat_tiles, 0, tm, 256) + [
                  row(256), row(256), row(256), row(4 * D), full(mod_l),
                  full(g2), full(og), full(g64), full(w_branch), full(w_out), full(wr_t), full(rb)],
        out_specs=[row(D), row(D), row(128)],
        out_shape=[jax.ShapeDtypeStruct((n_tok, D), jnp.float32), jax.ShapeDtypeStruct((n_tok, D), jnp.bfloat16),
                   jax.ShapeDtypeStruct((n_tok, 128), jnp.float32)],
        compiler_params=pltpu.CompilerParams(vmem_limit_bytes=VMEM_LIMIT),
        name="merge",
    )(Xl, Xc, *Ya, *Yd, *Yf, Of, Ob, P_z, P_gate, mod_l, g2, og, g64, w_branch, w_out, wr_t, rb)


MOE_TILE = 1024
MOE_WIN = 320
MOE_WIN_CTX = 176
MOE_EPS = 4


def _moe_sorted_kernel(x_ref, h_ref, rw_ref, m_ref, wg_ref, wu_ref, wd_ref, o_ref, xs_ref, ws_ref, ys_ref, dest_ref,
                       seg_ref, *, win, mod_row_fn):
    T = h_ref.shape[0]
    per_group = N_EXPERTS // N_GROUPS
    i = pl.program_id(0)
    step = pl.program_id(1)
    lane = lax.broadcasted_iota(jnp.int32, (1, 128), 1)
    f32, bf16 = jnp.float32, jnp.bfloat16

    @pl.when(step == 0)
    def _():
        rw = rw_ref[...]
        r128 = lax.broadcasted_iota(jnp.int32, (128, 128), 0)
        c128 = lax.broadcasted_iota(jnp.int32, (128, 128), 1)
        e2g = ((r128 // per_group == c128) & (r128 < N_EXPERTS)).astype(bf16)
        og = (jnp.dot((rw > 0.0).astype(bf16), e2g, preferred_element_type=f32) > 0.5).astype(f32)
        rt = lax.broadcasted_iota(jnp.int32, (T, T), 0)
        ct = lax.broadcasted_iota(jnp.int32, (T, T), 1)
        earlier = jnp.dot((rt > ct).astype(bf16), og.astype(bf16), preferred_element_type=f32)
        cnt_row = jnp.sum(og, axis=0, keepdims=True)
        start_row = jnp.dot(jnp.broadcast_to(cnt_row, (8, 128)), (r128 < c128).astype(f32),
                            preferred_element_type=f32, precision=HI)[0:1]
        dest = jnp.sum(og * (earlier + start_row), axis=-1, keepdims=True)
        dest_ref[...] = jnp.broadcast_to(dest, (T, 128))
        dest_row = jnp.transpose(dest_ref[...])[0:1, :]
        perm = (dest_row == rt.astype(f32)).astype(bf16)
        xs_ref[...] = jnp.dot(perm, h_ref[...], preferred_element_type=f32).astype(bf16)
        ws_ref[...] = sum(jnp.dot(perm, piece, preferred_element_type=f32) for piece in _split_bf16(rw))
        ys_ref[...] = jnp.zeros(ys_ref.shape, f32)
        for g in range(N_GROUPS):
            seg_ref[g] = jnp.sum(jnp.where(lane == g, start_row, 0.0)).astype(jnp.int32)
            seg_ref[N_GROUPS + g] = jnp.sum(jnp.where(lane == g, cnt_row, 0.0)).astype(jnp.int32)

    g = (step * MOE_EPS) // per_group
    start = seg_ref[g]
    end = start + seg_ref[N_GROUPS + g]
    s0 = (start // 16) * 16
    n_win = (end - s0 + win - 1) // win

    def window(w, carry):
        lo = s0 + w * win
        off = pl.multiple_of(jnp.minimum(lo, T - win), 16)
        xw = xs_ref[pl.ds(off, win), :]
        ww = ws_ref[pl.ds(off, win), :]
        row = off + lax.broadcasted_iota(jnp.int32, (win, 1), 0)
        y = None
        for j in range(MOE_EPS):
            w_col = jnp.sum(jnp.where(lane == step * MOE_EPS + j, ww, 0.0), axis=-1, keepdims=True)
            w_col = jnp.where(row >= lo, w_col, 0.0)
            a = jnp.dot(xw, wg_ref[0, j], preferred_element_type=f32)
            u = jnp.dot(xw, wu_ref[0, j], preferred_element_type=f32)
            act = (a * jax.nn.sigmoid(a) * u * w_col).astype(bf16)
            yj = jnp.dot(act, wd_ref[0, j], preferred_element_type=f32)
            y = yj if y is None else y + yj
        ys_ref[pl.ds(off, win), :] += y
        return carry

    lax.fori_loop(0, n_win, window, 0)

    @pl.when(step == N_EXPERTS // MOE_EPS - 1)
    def _():
        ct = lax.broadcasted_iota(jnp.int32, (T, T), 1)
        dest_wide = jnp.concatenate([dest_ref[...]] * (T // 128), axis=1)
        unperm = (dest_wide == ct.astype(f32)).astype(bf16)
        f = jnp.dot(unperm, ys_ref[...].astype(bf16), preferred_element_type=f32)
        g2 = m_ref[pl.ds(mod_row_fn(i), 1), 5 * D_MODEL:6 * D_MODEL]
        o_ref[...] = x_ref[...] + g2 * f


def moe_sorted_call(Xmid, H2, RW, mod_l, w_gate, w_up, w_down, *, layer, row0, n_rows, tile, win, mod_row_fn):
    D = Xmid.shape[1]
    F = w_gate.shape[-1]
    b0 = row0 // tile
    tok = lambda w: pl.BlockSpec((tile, w), lambda i, e: (b0 + i, 0))
    kern = functools.partial(_moe_sorted_kernel, win=win, mod_row_fn=mod_row_fn)
    return pl.pallas_call(
        kern,
        grid=(n_rows // tile, N_EXPERTS // MOE_EPS),
        in_specs=[tok(D), tok(D), tok(128),
                  pl.BlockSpec(mod_l.shape, lambda i, e: (0, 0)),
                  pl.BlockSpec((1, MOE_EPS, D, F), lambda i, e: (layer, e, 0, 0)),
                  pl.BlockSpec((1, MOE_EPS, D, F), lambda i, e: (layer, e, 0, 0)),
                  pl.BlockSpec((1, MOE_EPS, F, D), lambda i, e: (layer, e, 0, 0))],
        out_specs=pl.BlockSpec((tile, D), lambda i, e: (i, 0)),
        out_shape=jax.ShapeDtypeStruct((n_rows, D), jnp.float32),
        scratch_shapes=[pltpu.VMEM((tile, D), jnp.bfloat16), pltpu.VMEM((tile, 128), jnp.float32),
                        pltpu.VMEM((tile, D), jnp.float32), pltpu.VMEM((tile, 128), jnp.float32),
                        pltpu.SMEM((2 * N_GROUPS,), jnp.int32)],
        compiler_params=pltpu.CompilerParams(vmem_limit_bytes=VMEM_LIMIT),
        name="moe_sorted",
    )(Xmid, H2, RW, mod_l, w_gate, w_up, w_down)


GDN_TILE = 256
GDN_CPT = GDN_TILE // GDN_CHUNK
GDN_LOCKSTEP = 2


def _mm(a, b, passes=1, dims=None):
    if dims is None:
        dot = lambda x, y: jnp.dot(x, y, preferred_element_type=jnp.float32)
    else:
        dot = lambda x, y: lax.dot_general(x, y, dims, preferred_element_type=jnp.float32)
    if passes == 1:
        return dot(a.astype(jnp.bfloat16), b.astype(jnp.bfloat16))
    a_hi, a_lo = _split_bf16(a)
    b_hi, b_lo = _split_bf16(b)
    return dot(a_hi, b_hi) + dot(a_hi, b_lo) + dot(a_lo, b_hi)


def _stack_heads(x):
    lane = lax.broadcasted_iota(jnp.int32, (1, 256), 1)
    return jnp.concatenate([jnp.where((lane // GDN_DIM) == h, x, 0.0) for h in range(GDN_HEADS)], 0)


def _slab(x):
    return x[0:64] + x[64:128] + x[128:192] + x[192:256]


def _unit_tri_inverse(mats):
    shape = mats[0].shape
    eye = (lax.broadcasted_iota(jnp.int32, shape, 0) == lax.broadcasted_iota(jnp.int32, shape, 1)).astype(jnp.float32)
    ps = [eye - a for a in mats]
    pws = list(mats)
    for _ in range(5):
        pws = [_mm(pw, pw, 3) for pw in pws]
        ps = [p + _mm(p, pw, 3) for p, pw in zip(ps, pws)]
    return ps


def _gdn_chunk_kernel(x_ref, xp_ref, xn_ref, ab_ref, cw_ref, par_ref, g64_ref, qe_ref, ou_ref, mm_ref, nn_ref,
                      *, tiles_per_batch, n_lat_tiles):
    i = pl.program_id(0)
    is_lat = i < n_lat_tiles
    first = jnp.where(is_lat, (i % tiles_per_batch) == 0, True)
    last = jnp.where(is_lat, (i % tiles_per_batch) == tiles_per_batch - 1, True)
    xp = jnp.concatenate([jnp.where(first, 0.0, xp_ref[...]), x_ref[...], jnp.where(last, 0.0, xn_ref[...])], 0)
    y = xp[6:6 + GDN_TILE] * cw_ref[0:1, :]
    for t in range(1, 5):
        y = y + xp[6 + t:6 + t + GDN_TILE] * cw_ref[t:t + 1, :]
    y = y * jax.nn.sigmoid(y)
    g64 = g64_ref[...]
    q = y[:, 0:256]
    k = y[:, 256:512]
    v = y[:, 512:768]
    q = q * lax.rsqrt(_group_sum(q * q, g64) + RMS_EPS) * (GDN_DIM ** -0.5)
    k = k * lax.rsqrt(_group_sum(k * k, g64) + RMS_EPS)
    ab = ab_ref[...]
    lane128 = lax.broadcasted_iota(jnp.int32, (1, 128), 1)
    g_all = jnp.where(lane128 < 8, par_ref[0:1, :] * jax.nn.softplus(ab + par_ref[1:2, :]), 0.0)
    beta_all = jax.nn.sigmoid(ab)
    r64 = lax.broadcasted_iota(jnp.int32, (64, 64), 0)
    c64 = lax.broadcasted_iota(jnp.int32, (64, 64), 1)
    low = (r64 >= c64).astype(jnp.float32)
    upp = (r64 <= c64).astype(jnp.float32)
    rr = lax.broadcasted_iota(jnp.int32, (256, 256), 0)
    cc = lax.broadcasted_iota(jnp.int32, (256, 256), 1)
    same = (rr // 64) == (cc // 64)
    eye = (rr == cc).astype(jnp.float32)
    for c0 in range(0, GDN_CPT, GDN_LOCKSTEP):
        stage = []
        for c in range(c0, c0 + GDN_LOCKSTEP):
            sl = slice(c * GDN_CHUNK, (c + 1) * GDN_CHUNK)
            g_c = g_all[sl]
            gsum = jnp.where(lane128 < 4, jnp.dot(low, g_c, preferred_element_type=jnp.float32, precision=HI),
                             jnp.dot(upp, g_c, preferred_element_type=jnp.float32, precision=HI))
            gsum_t = jnp.transpose(jnp.concatenate([gsum, jnp.zeros_like(gsum)], 0))[:, 0:64]
            tot = jnp.sum(g_c, axis=0, keepdims=True)
            beta_c = beta_all[sl]
            q_sm = _stack_heads(q[sl])
            k_sm = _stack_heads(k[sl])
            v_sm = _stack_heads(v[sl])
            kk = _mm(k_sm, k_sm, 1, NT_DIMS)
            qk = _mm(q_sm, k_sm, 1, NT_DIMS)
            for d in range(2):
                col = lambda h: 4 * d + h
                cmat = jnp.concatenate([jnp.broadcast_to(gsum[:, col(h):col(h) + 1], (64, 256)) for h in range(4)], 0)
                rrow = jnp.concatenate([gsum_t[col(h):col(h) + 1, :] for h in range(4)], 1)
                bmat = jnp.concatenate([jnp.broadcast_to(beta_c[:, 8 + col(h):9 + col(h)], (64, 256))
                                        for h in range(4)], 0)
                tmat = jnp.concatenate([jnp.broadcast_to(tot[:, col(h):col(h) + 1], (64, 256)) for h in range(4)], 0)
                tri = (rr % 64 >= cc % 64) if d == 0 else (rr % 64 <= cc % 64)
                incl = same & tri
                strict = incl & (rr != cc)
                dec = jnp.where(incl, jnp.exp(jnp.where(incl, cmat - rrow, 0.0)), 0.0)
                stage.append((d, sl, q_sm, k_sm, v_sm, qk, cmat, bmat, tmat, dec,
                              jnp.where(strict, kk * bmat * dec, 0.0)))
        t_invs = _unit_tri_inverse([st[-1] for st in stage])
        n = len(stage)
        q_sms, k_sms, v_sms, qks, cmats, bmats, tmats, decs = (
            [st[f] for st in stage] for f in range(2, 10))
        egs = [jnp.exp(cm) for cm in cmats]
        us = [_mm(t_invs[j], v_sms[j] * bmats[j], 1) for j in range(n)]
        ws = [_mm(t_invs[j], k_sms[j] * bmats[j] * egs[j], 1) for j in range(n)]
        qkms = [decs[j] * qks[j] for j in range(n)]
        qes = [q_sms[j] * egs[j] - _mm(qkms[j], ws[j], 1) for j in range(n)]
        ous = [_mm(qkms[j], us[j], 1) for j in range(n)]
        kd_ts = [jnp.transpose(k_sms[j] * jnp.exp(tmats[j] - cmats[j])) for j in range(n)]
        m_bds = [eye * jnp.exp(tmats[j]) - _mm(kd_ts[j], ws[j], 1) for j in range(n)]
        n_bds = [_mm(kd_ts[j], us[j], 1) for j in range(n)]
        for j in range(n):
            d, sl = stage[j][0], stage[j][1]
            qe_ref[d, sl, :] = _slab(qes[j])
            ou_ref[d, sl, :] = _slab(ous[j])
            mm_ref[d, sl, :] = _slab(m_bds[j])
            nn_ref[d, sl, :] = _slab(n_bds[j])


def gdn_chunk_call(P_gdn, P_ab, conv_w, a_log, dt_bias, *, n_batch, seq):
    NT = P_gdn.shape[0]
    tpb = seq // GDN_TILE
    n_lat = n_batch * tpb
    n_tiles = NT // GDN_TILE
    hb = GDN_TILE // 8
    cw = jnp.zeros((8, 768), jnp.float32).at[:5].set(conv_w)
    par = jnp.zeros((8, 128), jnp.float32)
    par = par.at[0, :8].set(-jnp.exp(a_log.reshape(8))).at[1, :8].set(dt_bias.reshape(8))
    g64 = _group_ones(256, 64)
    kern = functools.partial(_gdn_chunk_kernel, tiles_per_batch=tpb, n_lat_tiles=n_lat)
    full = lambda a: pl.BlockSpec(a.shape, lambda i: (0,) * a.ndim)
    out = pl.BlockSpec((2, GDN_TILE, 256), lambda i: (0, i, 0))
    return pl.pallas_call(
        kern,
        grid=(n_tiles,),
        in_specs=[pl.BlockSpec((GDN_TILE, 768), lambda i: (i, 0)),
                  pl.BlockSpec((8, 768), lambda i: (jnp.maximum(i * hb - 1, 0), 0)),
                  pl.BlockSpec((8, 768), lambda i: (jnp.minimum((i + 1) * hb, n_tiles * hb - 1), 0)),
                  pl.BlockSpec((GDN_TILE, 128), lambda i: (i, 0)),
                  full(cw), full(par), full(g64)],
        out_specs=[out] * 4,
        out_shape=[jax.ShapeDtypeStruct((2, NT, 256), jnp.float32)] * 4,
        compiler_params=pltpu.CompilerParams(vmem_limit_bytes=VMEM_LIMIT),
        name="gdn_chunk",
    )(P_gdn, P_gdn, P_gdn, P_ab, cw, par, g64)


def _gdn_scan_kernel(qef_ref, ouf_ref, mf_ref, nf_ref, qeb_ref, oub_ref, mb_ref, nb_ref, of_ref, ob_ref, s_ref):
    t = pl.program_id(1)

    @pl.when(t == 0)
    def _():
        s_ref[...] = jnp.zeros(s_ref.shape, jnp.float32)

    def step(d, c, qe_ref, ou_ref, m_ref, n_ref, o_ref):
        sl = slice(c * GDN_CHUNK, (c + 1) * GDN_CHUNK)
        s = s_ref[d]
        o_ref[sl, :] = _slab(_mm(_stack_heads(qe_ref[0, sl, :]), s, 1)) + ou_ref[0, sl, :]
        s_ref[d] = _mm(_stack_heads(m_ref[0, sl, :]), s, 3) + _stack_heads(n_ref[0, sl, :])

    for c in range(GDN_CPT):
        step(0, c, qef_ref, ouf_ref, mf_ref, nf_ref, of_ref)
        step(1, GDN_CPT - 1 - c, qeb_ref, oub_ref, mb_ref, nb_ref, ob_ref)


def gdn_scan_call(QE, OU, MM, NN, *, n_batch, seq):
    NT = QE.shape[1]
    tpb = seq // GDN_TILE
    cblk = n_batch * tpb
    fwd = lambda b, t: jnp.where(t == 0, cblk + b, b * tpb + t - 1)
    bwd = lambda b, t: jnp.where(t == 0, cblk + b, b * tpb + tpb - t)
    spec = lambda d, f: pl.BlockSpec((1, GDN_TILE, 256), lambda b, t: (d, f(b, t), 0))
    ospec = lambda f: pl.BlockSpec((GDN_TILE, 256), lambda b, t: (f(b, t), 0))
    return pl.pallas_call(
        _gdn_scan_kernel,
        grid=(n_batch, tpb + 1),
        in_specs=[spec(0, fwd)] * 4 + [spec(1, bwd)] * 4,
        out_specs=[ospec(fwd), ospec(bwd)],
        out_shape=[jax.ShapeDtypeStruct((NT, 256), jnp.float32)] * 2,
        scratch_shapes=[pltpu.VMEM((2, 256, 256), jnp.float32)],
        compiler_params=pltpu.CompilerParams(vmem_limit_bytes=VMEM_LIMIT),
        name="gdn_scan",
    )(QE, OU, MM, NN, QE, OU, MM, NN)


def _dft_cs(n):
    a = 2.0 * np.pi * np.outer(np.arange(n), np.arange(n)) / n
    return np.cos(a), np.sin(a)


def _channel_dft():
    c, s = _dft_cs(FNET_GROUP_W)
    eye = np.eye(FNET_GROUPS)
    return np.concatenate([np.kron(eye, c), np.kron(eye, s)], 1)


FNET_SUB = 16


def _fnet_a_kernel(x_ref, cs_ref, m_ref, cphi_ref, sphi_ref, br_ref, bi_ref, *, n_rows):
    for j in range(FNET_SUB):
        sl = slice(j * BRANCH_W, (j + 1) * BRANCH_W)
        u = jnp.dot(x_ref[:, j, :], cs_ref[...], preferred_element_type=jnp.float32)
        st = jnp.concatenate([u[:, :BRANCH_W], u[:, BRANCH_W:]], 0).astype(jnp.bfloat16)
        a = jnp.dot(m_ref[...], st, preferred_element_type=jnp.float32)
        ar, ai = a[:n_rows], a[n_rows:]
        cp, sp = cphi_ref[:, sl], sphi_ref[:, sl]
        br_ref[j] = ar * cp - ai * sp
        bi_ref[j] = ar * sp + ai * cp


def _fnet_c_kernel(br_ref, bi_ref, m_ref, o_ref):
    for j in range(FNET_SUB):
        st = jnp.concatenate([br_ref[:, j, :], bi_ref[:, j, :]], 0).astype(jnp.bfloat16)
        o_ref[:, j, :] = jnp.dot(m_ref[...], st, preferred_element_type=jnp.float32)


def fourier_latent_call(P_f, *, n_batch, seq):
    rows = seq // GRID_W
    bf16, f32 = jnp.bfloat16, jnp.float32
    c1, s1 = _dft_cs(rows)
    norm = 1.0 / math.sqrt(seq * FNET_GROUP_W)
    m_a = jnp.asarray(np.block([[c1, -s1], [s1, c1]]) * norm, bf16)
    c2, s2 = _dft_cs(GRID_W)
    m_c = jnp.asarray(np.concatenate([c2, -s2], 1), bf16)
    cs = jnp.asarray(_channel_dft(), bf16)
    phi = 2.0 * np.pi * np.outer(np.arange(rows), np.arange(GRID_W)) / seq
    cphi = jnp.repeat(jnp.asarray(np.cos(phi), f32), BRANCH_W, axis=1)
    sphi = jnp.repeat(jnp.asarray(np.sin(phi), f32), BRANCH_W, axis=1)
    xv = P_f.reshape(P_f.shape[0] // GRID_W, GRID_W, BRANCH_W)
    nj = GRID_W // FNET_SUB
    full = lambda a: pl.BlockSpec(a.shape, lambda b, j: (0,) * a.ndim)
    tw = pl.BlockSpec((rows, FNET_SUB * BRANCH_W), lambda b, j: (0, j))
    mid = jax.ShapeDtypeStruct((n_batch * GRID_W, rows, BRANCH_W), f32)
    br, bi = pl.pallas_call(
        functools.partial(_fnet_a_kernel, n_rows=rows),
        grid=(n_batch, nj),
        in_specs=[pl.BlockSpec((rows, FNET_SUB, BRANCH_W), lambda b, j: (b, j, 0)), full(cs), full(m_a), tw, tw],
        out_specs=[pl.BlockSpec((FNET_SUB, rows, BRANCH_W), lambda b, j: (b * nj + j, 0, 0))] * 2,
        out_shape=[mid, mid],
        name="fnet_rows",
    )(xv, cs, m_a, cphi, sphi)
    blk = pl.BlockSpec((GRID_W, FNET_SUB, BRANCH_W), lambda b, i: (b, i, 0))
    y = pl.pallas_call(
        _fnet_c_kernel,
        grid=(n_batch, rows // FNET_SUB),
        in_specs=[blk, blk, pl.BlockSpec(m_c.shape, lambda b, i: (0, 0))],
        out_specs=blk,
        out_shape=mid,
        name="fnet_cols",
    )(br, bi, m_c)
    return y.reshape(n_batch * seq, BRANCH_W)


def _fnet_ctx_kernel(x_ref, cs_ref, m_ref, o_ref):
    u = jnp.dot(x_ref[...], cs_ref[...], preferred_element_type=jnp.float32)
    st = jnp.concatenate([u[:, :BRANCH_W], u[:, BRANCH_W:]], 0).astype(jnp.bfloat16)
    o_ref[...] = jnp.dot(m_ref[...], st, preferred_element_type=jnp.float32).astype(o_ref.dtype)


def fourier_ctx_call(P_f, *, n_batch, seq, ctx_len):
    bf16 = jnp.bfloat16
    c, s = _dft_cs(ctx_len)
    m = jnp.asarray(np.concatenate([c, -s], 1) / math.sqrt(ctx_len * FNET_GROUP_W), bf16)
    cs = jnp.asarray(_channel_dft(), bf16)
    cb = n_batch * seq // ctx_len
    return pl.pallas_call(
        _fnet_ctx_kernel,
        grid=(n_batch,),
        in_specs=[pl.BlockSpec((ctx_len, BRANCH_W), lambda b: (cb + b, 0)),
                  pl.BlockSpec(cs.shape, lambda b: (0, 0)), pl.BlockSpec(m.shape, lambda b: (0, 0))],
        out_specs=pl.BlockSpec((ctx_len, BRANCH_W), lambda b: (b, 0)),
        out_shape=jax.ShapeDtypeStruct((n_batch * ctx_len, BRANCH_W), jnp.float32),
        name="fnet_ctx",
    )(P_f, cs, m)


def kernel(x, c, ctx, c_ctx, w_mod, b_mod, norm1_g, norm2_g, w_in, na_qn_g, na_kn_g, na_rpb,
           da_qn_g, da_kn_g, da_lam_q1, da_lam_k1, da_lam_q2, da_lam_k2, da_subln_g, gdn_conv_w,
           gdn_a_log, gdn_dt_bias, gdn_onorm_g, w_branch, w_out, w_router, router_bias,
           moe_w_gate, moe_w_up, moe_w_down):
    B, S, D = x.shape
    L = ctx.shape[1]
    NL = B * S
    f32 = jnp.float32
    cvec = jnp.zeros((8, D), f32).at[:B].set(c).at[B].set(c_ctx)
    mods = mod_call(cvec, w_mod, b_mod)
    cos, sin = rope_tables(S, 256)
    bf16 = jnp.bfloat16
    Xl, Xc = x.reshape(NL, D), ctx.reshape(B * L, D)
    experts = (moe_w_gate.astype(bf16), moe_w_up.astype(bf16), moe_w_down.astype(bf16))
    w_t = jnp.swapaxes(w_in, 1, 2).astype(bf16)
    for l in range(DEPTH):
        last = l == DEPTH - 1
        lam_init = 0.8 - 0.6 * math.exp(-0.3 * l)
        NAQ, NAK, NAV, DAQ, DAK, DAVX, P_gdn, P_z, P_f, P_gate, P_ab = inproj_call(
            Xl, Xc, 0, mods[l], norm1_g[l], w_t, l, cos, sin, na_qn_g[l], na_kn_g[l], da_qn_g[l], da_kn_g[l],
            n_batch=B, seq=S, ctx_len=L)
        bias = na_bias_tables(na_rpb[l], S // GRID_W)
        lam = (jnp.exp(jnp.sum(da_lam_q1[l] * da_lam_k1[l])) - jnp.exp(jnp.sum(da_lam_q2[l] * da_lam_k2[l]))
               + lam_init).astype(f32)
        Ya = na_call(NAQ, NAK, NAV, bias, n_batch=B, seq=S, ctx_len=L)
        Yd = da_call(lam, DAQ, DAK, DAVX, da_subln_g[l], n_batch=B, seq=S, ctx_len=L, lam_init=lam_init)
        QE, OU, MM, NN = gdn_chunk_call(P_gdn, P_ab, gdn_conv_w[l], gdn_a_log[l], gdn_dt_bias[l], n_batch=B, seq=S)
        Of, Ob = gdn_scan_call(QE, OU, MM, NN, n_batch=B, seq=S)
        Yf = fourier_latent_call(P_f, n_batch=B, seq=S)
        n_tok = NL
        Yac, Ydc, Yfc = Ya, Yd, Yf
        if not last:
            Yac, Ydc = ctx_attn_call(lam, NAQ, NAK, NAV, DAQ, DAK, DAVX, da_subln_g[l],
                                     n_batch=B, seq=S, ctx_len=L, lam_init=lam_init)
            Yfc = fourier_ctx_call(P_f, n_batch=B, seq=S, ctx_len=L)
            n_tok = NL + B * L
        Xmid, H2, RW = merge_call(Xl, Xc, 0, (Ya, Yac), (Yd, Ydc), (Yf, Yfc), Of, Ob, P_z, P_gate, mods[l],
                                  norm2_g[l], gdn_onorm_g[l], w_branch[l].astype(bf16), w_out[l].astype(bf16),
                                  w_router, router_bias, n_tok=n_tok, n_batch=B, seq=S)
        Xl = moe_sorted_call(Xmid, H2, RW, mods[l], *experts, layer=l, row0=0, n_rows=NL, tile=MOE_TILE, win=MOE_WIN,
                             mod_row_fn=lambda i: i // (S // MOE_TILE))
        if not last:
            Xc = moe_sorted_call(Xmid, H2, RW, mods[l], *experts, layer=l, row0=NL, n_rows=B * L, tile=B * L,
                                 win=MOE_WIN_CTX, mod_row_fn=lambda i: B)
    return Xl.reshape(B, S, D)
```

```python
import functools
import math
import jax
import jax.numpy as jnp
from jax import lax
import numpy as np
from jax.experimental import pallas as pl
from jax.experimental.pallas import tpu as pltpu

D_MODEL = 1024
DEPTH = 2
GRID_W = 64
N_BRANCH = 4
BRANCH_W = D_MODEL // N_BRANCH
GDN_HEADS = 4
GDN_DIM = BRANCH_W // GDN_HEADS
GDN_CHUNK = 64
FNET_GROUPS = 4
FNET_GROUP_W = BRANCH_W // FNET_GROUPS
N_EXPERTS = 16
N_GROUPS = 4
RMS_EPS = 1e-6
NEG_INF = -1e30
ROPE_BASE = 10000.0
VMEM_LIMIT = 56 * 1024 * 1024
W_AB0 = 2560
TOKEN_TILE = 512
HI = lax.Precision.HIGHEST
NT_DIMS = (((1,), (1,)), ((), ()))
LOG2E = 1.4426950408889634


def _mod_kernel(c_ref, w_ref, b_ref, o_ref):
    c = c_ref[...]
    a = c * jax.nn.sigmoid(c)
    o_ref[0] = _mm(a, w_ref[0], 3) + b_ref[0]


def mod_call(cvec, w_mod, b_mod, tn=1024):
    depth, D, N = w_mod.shape
    return pl.pallas_call(
        _mod_kernel,
        grid=(depth, N // tn),
        in_specs=[pl.BlockSpec((8, D), lambda l, j: (0, 0)),
                  pl.BlockSpec((1, D, tn), lambda l, j: (l, 0, j)),
                  pl.BlockSpec((1, 1, tn), lambda l, j: (l, 0, j))],
        out_specs=pl.BlockSpec((1, 8, tn), lambda l, j: (l, 0, j)),
        out_shape=jax.ShapeDtypeStruct((depth, 8, N), jnp.float32),
        name="mod",
    )(cvec, w_mod, b_mod.reshape(depth, 1, N))


def _stream_specs(n_lat_tiles, ctx_tile0, tm, width):
    return [pl.BlockSpec((tm, width), lambda i: (jnp.minimum(i, n_lat_tiles - 1), 0)),
            pl.BlockSpec((tm, width), lambda i: (ctx_tile0 + jnp.maximum(i - n_lat_tiles, 0), 0))]


def _inproj_kernel(xl_ref, xc_ref, m_ref, g_ref, wt_ref, cos_ref, sin_ref, g64_ref, g32_ref, gains_ref,
                   naq_ref, nak_ref, nav_ref, daq_ref, dak_ref, davx_ref, gdn_ref, z_ref, f_ref, gate_ref, ab_ref,
                   *, tiles_per_batch, n_lat_tiles, n_batch):
    i = pl.program_id(0)
    r = jnp.where(i < n_lat_tiles, i // tiles_per_batch, n_batch)
    x = jnp.where(i < n_lat_tiles, xl_ref[...], xc_ref[...])
    y = x * lax.rsqrt(jnp.mean(x * x, axis=-1, keepdims=True) + RMS_EPS) * g_ref[...]
    sh = m_ref[pl.ds(r, 1), 0:D_MODEL]
    sc = m_ref[pl.ds(r, 1), D_MODEL:2 * D_MODEL]
    h = (y * (1.0 + sc) + sh).astype(jnp.bfloat16)
    proj = lambda a, b: lax.dot_general(h, wt_ref[0, a:b, :], NT_DIMS, preferred_element_type=jnp.float32)
    proj_b = lambda a, b: proj(W_AB0 + 16 + a, W_AB0 + 16 + b)
    bf16 = jnp.bfloat16

    g64 = g64_ref[...]
    naq_ref[...] = (_group_rms(proj(0, 256), g64, 64, gains_ref[0:1, :]) * (64 ** -0.5)).astype(bf16)
    nak_ref[...] = _group_rms(proj(256, 512), g64, 64, gains_ref[1:2, :]).astype(bf16)
    nav_ref[...] = proj(512, 768).astype(bf16)

    g32 = g32_ref[...]
    cos = cos_ref[...]
    sin = sin_ref[...]
    lane = lax.broadcasted_iota(jnp.int32, (1, 256), 1)
    first = (lane % 16) < 8

    def rope(t):
        swapped = jnp.where(first, pltpu.roll(t, 256 - 8, 1), pltpu.roll(t, 8, 1))
        return t * cos + swapped * sin

    q = rope(_group_rms(proj(768, 1024), g32, 32, gains_ref[2:3, :]))
    k = rope(_group_rms(proj(1024, 1280), g32, 32, gains_ref[3:4, :]))
    daq_ref[...] = (q * (32 ** -0.5 * LOG2E)).astype(bf16)
    dak_ref[...] = k.astype(bf16)
    v = proj(1280, 1536).astype(bf16)
    one_col = (lax.broadcasted_iota(jnp.int32, (v.shape[0], 64), 1) == 0).astype(bf16)
    davx_ref[...] = jnp.concatenate(
        [piece for hd in range(4) for piece in (v[:, hd * 64:(hd + 1) * 64], one_col)], -1)

    gdn_ref[...] = proj(1536, 2304)
    z_ref[...] = proj(2304, 2560).astype(bf16)
    f_ref[...] = proj_b(0, 256).astype(bf16)
    for kk in range(4):
        gate_ref[:, kk * 1024:(kk + 1) * 1024] = jax.nn.sigmoid(
            proj_b(256 + kk * 1024, 256 + (kk + 1) * 1024)).astype(bf16)
    ab_ref[...] = proj(W_AB0, W_AB0 + 128)


def inproj_call(Xl, Xc, ctx_row0, mod_l, g, w_t, layer, cos, sin, na_qg, na_kg, da_qg, da_kg,
                *, n_batch, seq, ctx_len, tm=TOKEN_TILE):
    D = Xl.shape[1]
    NT = n_batch * (seq + ctx_len)
    tpb = seq // tm
    n_lat_tiles = n_batch * tpb
    kern = functools.partial(_inproj_kernel, tiles_per_batch=tpb, n_lat_tiles=n_lat_tiles, n_batch=n_batch)
    bf16, f32 = jnp.bfloat16, jnp.float32
    widths = [(256, bf16)] * 5 + [(512, bf16), (768, f32), (256, bf16), (256, bf16), (4096, bf16), (128, f32)]
    gains = jnp.stack([jnp.tile(na_qg, 4), jnp.tile(na_kg, 4), jnp.tile(da_qg, 8), jnp.tile(da_kg, 8)]
                      + [jnp.zeros((256,), f32)] * 4, 0)
    g64, g32 = _group_ones(256, 64), _group_ones(256, 32)
    tab = pl.BlockSpec((tm, 256), lambda i: (jnp.where(i < n_lat_tiles, i % tpb, tpb), 0))
    full = lambda a: pl.BlockSpec(a.shape, lambda i: (0,) * a.ndim)
    return pl.pallas_call(
        kern,
        grid=(NT // tm,),
        in_specs=_stream_specs(n_lat_tiles, ctx_row0 // tm, tm, D) + [
            full(mod_l), pl.BlockSpec((1, D), lambda i: (0, 0)),
            pl.BlockSpec((1,) + w_t.shape[1:], lambda i: (layer, 0, 0)), tab, tab,
            full(g64), full(g32), full(gains)],
        out_specs=[pl.BlockSpec((tm, w), lambda i: (i, 0)) for w, _ in widths],
        out_shape=[jax.ShapeDtypeStruct((NT, w), dt) for w, dt in widths],
        compiler_params=pltpu.CompilerParams(vmem_limit_bytes=VMEM_LIMIT),
        name="inproj",
    )(Xl, Xc, mod_l, g.reshape(1, D), w_t, cos, sin, g64, g32, gains)


def _group_ones(width, group):
    i = np.arange(width)
    return jnp.asarray((i[:, None] // group == i[None, :] // group).astype(np.float32), jnp.bfloat16)


def _group_sum(xx, gmat):
    hi, lo = _split_bf16(xx)
    return (jnp.dot(hi, gmat, preferred_element_type=jnp.float32)
            + jnp.dot(lo, gmat, preferred_element_type=jnp.float32))


def _group_rms(x, gmat, group, gain):
    return x * lax.rsqrt(_group_sum(x * x, gmat) * (1.0 / group) + RMS_EPS) * gain


def rope_tables(seq, tm):
    nf = 8
    t = jnp.arange(seq)
    rows = (t // GRID_W).astype(jnp.float32)
    cols = (t % GRID_W).astype(jnp.float32)
    freqs = ROPE_BASE ** (-jnp.arange(nf, dtype=jnp.float32) / nf)
    d = np.arange(32)
    f_idx = d % 8
    use_col = (d // 16) == 1
    ang = jnp.where(use_col[None, :], cols[:, None], rows[:, None]) * freqs[f_idx][None, :]
    sign = np.where((d % 16) < 8, -1.0, 1.0).astype(np.float32)
    cos = jnp.tile(jnp.cos(ang), (1, 8))
    sin = jnp.tile(jnp.sin(ang) * sign[None, :], (1, 8))
    cos = jnp.concatenate([cos, jnp.ones((tm, 256), jnp.float32)], 0)
    sin = jnp.concatenate([sin, jnp.zeros((tm, 256), jnp.float32)], 0)
    return cos, sin


NA_RB = 4
NA_UR = 11


def na_bias_tables(rpb, rows):
    reps = [(0, 0), (4, 0), (rows - NA_RB, rows - NA_UR)]
    qc = np.arange(GRID_W)[:, None]
    kc = np.arange(GRID_W)[None, :]
    ws = np.clip(qc - 8, 0, GRID_W - 16)
    vcol = (kc >= ws) & (kc < ws + 16)
    rel_c = np.clip(kc - qc + 15, 0, 30)
    sel_c = (rel_c[..., None] == np.arange(31)).astype(np.float32)
    sel_r, vrow = [], []
    for r0, u0 in reps:
        r = r0 + np.arange(NA_RB)[:, None]
        krow = u0 + np.arange(NA_UR)[None, :]
        start = np.clip(r - 4, 0, rows - 8)
        vrow.append((krow >= start) & (krow < start + 8))
        sel_r.append((np.clip(krow - r + 7, 0, 14)[..., None] == np.arange(15)).astype(np.float32))
    sel_r, vrow = np.stack(sel_r), np.stack(vrow)
    b = jnp.einsum('taui,hij,qkj->thauqk', jnp.asarray(sel_r), rpb.astype(jnp.float32), jnp.asarray(sel_c),
                   precision=HI)
    valid = vrow[:, None, :, :, None, None] & vcol[None, None, None, None, :, :]
    return jnp.where(valid, b, NEG_INF).astype(jnp.bfloat16)


def _na_kernel(q_ref, k_ref, v_ref, kc_ref, vc_ref, bias_ref, o_ref, bias_s, *, rows):
    i = pl.program_id(1)

    @pl.when((i <= 1) | (i == rows // NA_RB - 1))
    def _():
        for h in range(4):
            for a in range(NA_RB):
                for u in range(NA_UR):
                    bias_s[h, a * GRID_W:(a + 1) * GRID_W, u * GRID_W:(u + 1) * GRID_W] = (
                        bias_ref[0, h, a, u].astype(jnp.float32))

    u0 = jnp.clip(i * NA_RB - 4, 0, rows - NA_UR)
    off = pl.multiple_of(u0 * GRID_W, GRID_W)
    nkw = NA_UR * GRID_W
    kwin = k_ref[pl.ds(off, nkw), :]
    vwin = v_ref[pl.ds(off, nkw), :]
    q = q_ref[...]
    kc = kc_ref[...]
    vc = vc_ref[...]
    sls = [slice(h * 64, (h + 1) * 64) for h in range(4)]
    dot_nt = lambda x, y: lax.dot_general(x, y, NT_DIMS, preferred_element_type=jnp.float32)
    s_locs = [dot_nt(q[:, sl], kwin[:, sl]) + bias_s[h] for h, sl in enumerate(sls)]
    s_ctxs = [dot_nt(q[:, sl], kc[:, sl]) for sl in sls]
    ms = [jnp.maximum(jnp.max(a, -1, keepdims=True), jnp.max(c, -1, keepdims=True)) for a, c in zip(s_locs, s_ctxs)]
    p_locs = [jnp.exp(a - m) for a, m in zip(s_locs, ms)]
    p_ctxs = [jnp.exp(c - m) for c, m in zip(s_ctxs, ms)]
    ls = [jnp.sum(a, -1, keepdims=True) + jnp.sum(c, -1, keepdims=True) for a, c in zip(p_locs, p_ctxs)]
    outs = [(jnp.dot(pa.astype(jnp.bfloat16), vwin[:, sl], preferred_element_type=jnp.float32)
             + jnp.dot(pc.astype(jnp.bfloat16), vc[:, sl], preferred_element_type=jnp.float32)) / l
            for pa, pc, l, sl in zip(p_locs, p_ctxs, ls, sls)]
    o_ref[...] = jnp.concatenate(outs, -1).astype(o_ref.dtype)


def na_call(NAQ, NAK, NAV, bias, *, n_batch, seq, ctx_len):
    rows = seq // GRID_W
    nblk = rows // NA_RB
    tq = NA_RB * GRID_W
    cb = n_batch * seq // ctx_len
    kern = functools.partial(_na_kernel, rows=rows)
    tsel = lambda b, i: (jnp.where(i == 0, 0, jnp.where(i == nblk - 1, 2, 1)), 0, 0, 0, 0, 0)
    return pl.pallas_call(
        kern,
        grid=(n_batch, nblk),
        in_specs=[pl.BlockSpec((tq, 256), lambda b, i: (b * nblk + i, 0)),
                  pl.BlockSpec((seq, 256), lambda b, i: (b, 0)),
                  pl.BlockSpec((seq, 256), lambda b, i: (b, 0)),
                  pl.BlockSpec((ctx_len, 256), lambda b, i: (cb + b, 0)),
                  pl.BlockSpec((ctx_len, 256), lambda b, i: (cb + b, 0)),
                  pl.BlockSpec((1,) + bias.shape[1:], tsel)],
        out_specs=pl.BlockSpec((tq, 256), lambda b, i: (b * nblk + i, 0)),
        out_shape=jax.ShapeDtypeStruct((n_batch * seq, 256), jnp.bfloat16),
        scratch_shapes=[pltpu.VMEM((4, tq, NA_UR * GRID_W), jnp.float32)],
        compiler_params=pltpu.CompilerParams(vmem_limit_bytes=VMEM_LIMIT),
        name="na_attn",
    )(NAQ, NAK, NAV, NAK, NAV, bias)


def _stack_masked_q(q):
    lane = lax.broadcasted_iota(jnp.int32, (1, 256), 1)
    return jnp.concatenate([jnp.where((lane // 32) == hm, q, jnp.zeros_like(q)) for hm in range(8)], 0)


def _da_finish(acc, lam, gain, g64, post_scale):
    tq = acc.shape[0] // 8
    norm = acc / jnp.broadcast_to(acc[:, 64:65], acc.shape)
    lane = lax.broadcasted_iota(jnp.int32, (1, 128), 1)
    halves = []
    for hp in range(2):
        d = [norm[(4 * hp + 2 * j) * tq:(4 * hp + 2 * j + 1) * tq]
             - lam * norm[(4 * hp + 2 * j + 1) * tq:(4 * hp + 2 * j + 2) * tq] for j in range(2)]
        halves.append(jnp.where(lane < 64, d[0], pltpu.roll(d[1], 64, 1)))
    o = jnp.concatenate(halves, -1)
    return _group_rms(o, g64, 64, gain) * post_scale


def _da_tile(qs, k, vx, m, acc, tq):
    tk = k.shape[0]
    s = lax.dot_general(qs, k, NT_DIMS, preferred_element_type=jnp.float32)
    m_new = jnp.maximum(m, jnp.max(s, -1, keepdims=True))
    alpha = jnp.exp2(m - m_new)
    p = jnp.exp2(s - jnp.concatenate([m_new] * (tk // 128), axis=1)).astype(jnp.bfloat16)
    pv = jnp.concatenate(
        [jnp.dot(p[2 * h * tq:(2 * h + 2) * tq], vx[:, h * 128:(h + 1) * 128], preferred_element_type=jnp.float32)
         for h in range(4)], 0)
    return m_new, alpha * acc + pv


def _da_kernel(lam_ref, q_ref, k_ref, vx_ref, kc_ref, vxc_ref, gain_ref, g64_ref, o_ref, m_ref, acc_ref,
               *, tk, n_kv, post_scale):
    tq = q_ref.shape[0]
    qs = _stack_masked_q(q_ref[...])
    m_ref[...] = jnp.full(m_ref.shape, -jnp.inf, jnp.float32)
    acc_ref[...] = jnp.zeros(acc_ref.shape, jnp.float32)

    def body(j, carry):
        for u in range(2):
            off = pl.multiple_of((2 * j + u) * tk, tk)
            m, acc = _da_tile(qs, k_ref[pl.ds(off, tk), :], vx_ref[pl.ds(off, tk), :], m_ref[...], acc_ref[...], tq)
            m_ref[...] = m
            acc_ref[...] = acc
        return carry

    lax.fori_loop(0, n_kv // 2 - 1, body, 0, unroll=True)
    off = (n_kv - 2) * tk
    m, acc = _da_tile(qs, k_ref[off:off + tk, :], vx_ref[off:off + tk, :], m_ref[...], acc_ref[...], tq)
    k_last = jnp.concatenate([k_ref[off + tk:off + 2 * tk, :], kc_ref[...]], 0)
    vx_last = jnp.concatenate([vx_ref[off + tk:off + 2 * tk, :], vxc_ref[...]], 0)
    m, acc = _da_tile(qs, k_last, vx_last, m, acc, tq)
    o_ref[...] = _da_finish(acc, lam_ref[0], gain_ref[...], g64_ref[...], post_scale).astype(o_ref.dtype)


def da_call(lam, DAQ, DAK, DAVX, subln_g, *, n_batch, seq, ctx_len, lam_init, tq=256, tk=1024):
    tk = min(tk, seq // 2)
    nq = seq // tq
    cb = n_batch * seq // ctx_len
    g64 = _group_ones(256, 64)
    gain = jnp.tile(subln_g, 4).reshape(1, 256)
    n_kv = seq // tk
    assert n_kv % 2 == 0
    kern = functools.partial(_da_kernel, tk=tk, n_kv=n_kv, post_scale=1.0 - lam_init)
    return pl.pallas_call(
        kern,
        grid=(n_batch, nq),
        in_specs=[pl.BlockSpec(memory_space=pltpu.SMEM),
                  pl.BlockSpec((tq, 256), lambda b, i: (b * nq + i, 0)),
                  pl.BlockSpec((seq, 256), lambda b, i: (b, 0)),
                  pl.BlockSpec((seq, 512), lambda b, i: (b, 0)),
                  pl.BlockSpec((ctx_len, 256), lambda b, i: (cb + b, 0)),
                  pl.BlockSpec((ctx_len, 512), lambda b, i: (cb + b, 0)),
                  pl.BlockSpec((1, 256), lambda b, i: (0, 0)),
                  pl.BlockSpec((256, 256), lambda b, i: (0, 0))],
        out_specs=pl.BlockSpec((tq, 256), lambda b, i: (b * nq + i, 0)),
        out_shape=jax.ShapeDtypeStruct((n_batch * seq, 256), jnp.bfloat16),
        scratch_shapes=[pltpu.VMEM((8 * tq, 128), jnp.float32), pltpu.VMEM((8 * tq, 128), jnp.float32)],
        compiler_params=pltpu.CompilerParams(vmem_limit_bytes=VMEM_LIMIT),
        name="da_attn",
    )(lam.reshape(1), DAQ, DAK, DAVX, DAK, DAVX, gain, g64)


def _ctx_kernel(lam_ref, naq_ref, nak_ref, nav_ref, daq_ref, dak_ref, dav_ref, gain_ref, g64_ref, ya_ref, yd_ref,
                *, post_scale):
    q = naq_ref[...]
    k = nak_ref[...]
    v = nav_ref[...]
    outs = []
    for h in range(4):
        sl = slice(h * 64, (h + 1) * 64)
        s = lax.dot_general(q[:, sl], k[:, sl], NT_DIMS, preferred_element_type=jnp.float32)
        p = jnp.exp(s - jnp.max(s, -1, keepdims=True))
        o = jnp.dot(p.astype(jnp.bfloat16), v[:, sl], preferred_element_type=jnp.float32)
        outs.append(o / jnp.sum(p, -1, keepdims=True))
    ya_ref[...] = jnp.concatenate(outs, -1).astype(ya_ref.dtype)
    tq = daq_ref.shape[0]
    qs = _stack_masked_q(daq_ref[...])
    m0 = jnp.full((8 * tq, 128), -jnp.inf, jnp.float32)
    m, acc = _da_tile(qs, dak_ref[...], dav_ref[...], m0, jnp.zeros((8 * tq, 128), jnp.float32), tq)
    yd_ref[...] = _da_finish(acc, lam_ref[0], gain_ref[...], g64_ref[...], post_scale).astype(yd_ref.dtype)


def ctx_attn_call(lam, NAQ, NAK, NAV, DAQ, DAK, DAVX, subln_g, *, n_batch, seq, ctx_len, lam_init):
    cb = n_batch * seq // ctx_len
    g64 = _group_ones(256, 64)
    gain = jnp.tile(subln_g, 4).reshape(1, 256)
    blk = lambda col: pl.BlockSpec((ctx_len, 256), lambda b: (cb + b, col))
    return pl.pallas_call(
        functools.partial(_ctx_kernel, post_scale=1.0 - lam_init),
        grid=(n_batch,),
        in_specs=[pl.BlockSpec(memory_space=pltpu.SMEM), blk(0), blk(0), blk(0), blk(0), blk(0),
                  pl.BlockSpec((ctx_len, 512), lambda b: (cb + b, 0)),
                  pl.BlockSpec((1, 256), lambda b: (0, 0)), pl.BlockSpec((256, 256), lambda b: (0, 0))],
        out_specs=[pl.BlockSpec((ctx_len, 256), lambda b: (b, 0))] * 2,
        out_shape=[jax.ShapeDtypeStruct((n_batch * ctx_len, 256), jnp.bfloat16)] * 2,
        name="ctx_attn",
    )(lam.reshape(1), NAQ, NAK, NAV, DAQ, DAK, DAVX, gain, g64)


def _split_bf16(a):
    hi = a.astype(jnp.bfloat16)
    return hi, (a - hi.astype(jnp.float32)).astype(jnp.bfloat16)


def _route(logits_t, bias_col):
    per_group = N_EXPERTS // N_GROUPS
    scores = jax.nn.sigmoid(logits_t)
    sel_all = scores + bias_col
    sel = [sel_all[e:e + 1, :] for e in range(N_EXPERTS)]
    top2 = []
    for e in range(N_EXPERTS):
        g0 = e // per_group * per_group
        rank = jnp.zeros_like(sel[e])
        for o in range(g0, g0 + per_group):
            if o != e:
                beats = (sel[o] > sel[e]) | (sel[o] == sel[e]) if o < e else (sel[o] > sel[e])
                rank = rank + beats.astype(jnp.float32)
        top2.append(rank < 1.5)
    gs = []
    for g in range(N_GROUPS):
        acc = jnp.zeros_like(sel[0])
        for e in range(g * per_group, (g + 1) * per_group):
            acc = acc + jnp.where(top2[e], sel[e], 0.0)
        gs.append(acc)
    rows = []
    for g in range(N_GROUPS):
        beaten = jnp.zeros_like(sel[0])
        for o in range(N_GROUPS):
            if o != g:
                b = (gs[o] >= gs[g]) if o < g else (gs[o] > gs[g])
                beaten = beaten + b.astype(jnp.float32)
        best = beaten < 0.5
        for e in range(g * per_group, (g + 1) * per_group):
            rows.append(jnp.where(best & top2[e], scores[e:e + 1, :], 0.0))
    w = jnp.concatenate(rows, 0)
    return w / jnp.sum(w, axis=0, keepdims=True)


def _merge_kernel(xl_ref, xc_ref, yal_ref, yac_ref, ydl_ref, ydc_ref, yfl_ref, yfc_ref, of_ref, ob_ref, z_ref, gate_ref,
                  m_ref, g2_ref, og_ref, g64_ref, wb_ref, wo_ref, wr_ref, rb_ref, xo_ref, h2_ref, rw_ref,
                  *, tiles_per_batch, n_lat_tiles, n_batch):
    i = pl.program_id(0)
    r = jnp.where(i < n_lat_tiles, i // tiles_per_batch, n_batch)
    z = z_ref[...].astype(jnp.float32)
    yg = (_group_rms(of_ref[...] + ob_ref[...], g64_ref[...], GDN_DIM, og_ref[...]) * (z * jax.nn.sigmoid(z)))
    is_lat = i < n_lat_tiles
    pick = lambda lat_ref, ctx_ref: jnp.where(is_lat, lat_ref[...], ctx_ref[...])
    branches = (pick(yal_ref, yac_ref), pick(ydl_ref, ydc_ref), yg.astype(jnp.bfloat16),
                pick(yfl_ref, yfc_ref).astype(jnp.bfloat16))
    mix = None
    for b, yb in enumerate(branches):
        proj = jnp.dot(yb, wb_ref[b], preferred_element_type=jnp.float32)
        term = gate_ref[:, b * D_MODEL:(b + 1) * D_MODEL].astype(jnp.float32) * proj
        mix = term if mix is None else mix + term
    y = jnp.dot(mix.astype(jnp.bfloat16), wo_ref[...], preferred_element_type=jnp.float32)
    mrow = lambda k: m_ref[pl.ds(r, 1), k * D_MODEL:(k + 1) * D_MODEL]
    x = jnp.where(i < n_lat_tiles, xl_ref[...], xc_ref[...]) + mrow(2) * y
    xo_ref[...] = x
    h2 = (x * lax.rsqrt(jnp.mean(x * x, axis=-1, keepdims=True) + RMS_EPS) * g2_ref[...]) * (1.0 + mrow(4)) + mrow(3)
    h2_ref[...] = h2.astype(h2_ref.dtype)
    h_hi, h_lo = _split_bf16(h2)
    w_hi, w_lo = _split_bf16(wr_ref[...])
    nt = lambda a, b: lax.dot_general(a, b, NT_DIMS, preferred_element_type=jnp.float32)
    logits_t = nt(w_hi, h_hi) + nt(w_hi, h_lo) + nt(w_lo, h_hi)
    w_t = _route(logits_t, rb_ref[...])
    pad = jnp.zeros((128 - N_EXPERTS, w_t.shape[1]), jnp.float32)
    rw_ref[...] = jnp.transpose(jnp.concatenate([w_t, pad], 0))


def merge_call(Xl, Xc, ctx_row0, Ya, Yd, Yf, Of, Ob, P_z, P_gate, mod_l, g2n, onorm_g, w_branch, w_out, w_router,
               router_bias, *, n_tok, n_batch, seq, tm=TOKEN_TILE):
    D = Xl.shape[1]
    n_lat_tiles = n_batch * seq // tm
    kern = functools.partial(_merge_kernel, tiles_per_batch=seq // tm, n_lat_tiles=n_lat_tiles, n_batch=n_batch)
    row = lambda w: pl.BlockSpec((tm, w), lambda i: (i, 0))
    full = lambda a: pl.BlockSpec(a.shape, lambda i: (0,) * a.ndim)
    wr_t = jnp.transpose(w_router)
    rb = router_bias.reshape(N_EXPERTS, 1)
    g2 = g2n.reshape(1, D)
    og = jnp.tile(onorm_g, GDN_HEADS).reshape(1, BRANCH_W)
    g64 = _group_ones(BRANCH_W, GDN_DIM)
    return pl.pallas_call(
        kern,
        grid=(n_tok // tm,),
        in_specs=_stream_specs(n_lat_tiles, ctx_row0 // tm, tm, D) + 3 * _stream_specs(n_lat_tiles, 0, tm, 256) + [
                  row(256), row(256), row(256), row(4 * D), full(mod_l),
                  full(g2), full(og), full(g64), full(w_branch), full(w_out), full(wr_t), full(rb)],
        out_specs=[row(D), row(D), row(128)],
        out_shape=[jax.ShapeDtypeStruct((n_tok, D), jnp.float32), jax.ShapeDtypeStruct((n_tok, D), jnp.bfloat16),
                   jax.ShapeDtypeStruct((n_tok, 128), jnp.float32)],
        compiler_params=pltpu.CompilerParams(vmem_limit_bytes=VMEM_LIMIT),
        name="merge",
    )(Xl, Xc, *Ya, *Yd, *Yf, Of, Ob, P_z, P_gate, mod_l, g2, og, g64, w_branch, w_out, wr_t, rb)


MOE_TILE = 1024
MOE_WIN = 320
MOE_WIN_CTX = 176
MOE_EPS = 4


def _moe_sorted_kernel(x_ref, h_ref, rw_ref, m_ref, wg_ref, wu_ref, wd_ref, o_ref, xs_ref, ws_ref, ys_ref, dest_ref,
                       seg_ref, *, win, mod_row_fn):
    T = h_ref.shape[0]
    per_group = N_EXPERTS // N_GROUPS
    i = pl.program_id(0)
    step = pl.program_id(1)
    lane = lax.broadcasted_iota(jnp.int32, (1, 128), 1)
    f32, bf16 = jnp.float32, jnp.bfloat16

    @pl.when(step == 0)
    def _():
        rw = rw_ref[...]
        r128 = lax.broadcasted_iota(jnp.int32, (128, 128), 0)
        c128 = lax.broadcasted_iota(jnp.int32, (128, 128), 1)
        e2g = ((r128 // per_group == c128) & (r128 < N_EXPERTS)).astype(bf16)
        og = (jnp.dot((rw > 0.0).astype(bf16), e2g, preferred_element_type=f32) > 0.5).astype(f32)
        rt = lax.broadcasted_iota(jnp.int32, (T, T), 0)
        ct = lax.broadcasted_iota(jnp.int32, (T, T), 1)
        earlier = jnp.dot((rt > ct).astype(bf16), og.astype(bf16), preferred_element_type=f32)
        cnt_row = jnp.sum(og, axis=0, keepdims=True)
        start_row = jnp.dot(jnp.broadcast_to(cnt_row, (8, 128)), (r128 < c128).astype(f32),
                            preferred_element_type=f32, precision=HI)[0:1]
        dest = jnp.sum(og * (earlier + start_row), axis=-1, keepdims=True)
        dest_ref[...] = jnp.broadcast_to(dest, (T, 128))
        dest_row = jnp.transpose(dest_ref[...])[0:1, :]
        perm = (dest_row == rt.astype(f32)).astype(bf16)
        xs_ref[...] = jnp.dot(perm, h_ref[...], preferred_element_type=f32).astype(bf16)
        ws_ref[...] = sum(jnp.dot(perm, piece, preferred_element_type=f32) for piece in _split_bf16(rw))
        ys_ref[...] = jnp.zeros(ys_ref.shape, f32)
        for g in range(N_GROUPS):
            seg_ref[g] = jnp.sum(jnp.where(lane == g, start_row, 0.0)).astype(jnp.int32)
            seg_ref[N_GROUPS + g] = jnp.sum(jnp.where(lane == g, cnt_row, 0.0)).astype(jnp.int32)

    g = (step * MOE_EPS) // per_group
    start = seg_ref[g]
    end = start + seg_ref[N_GROUPS + g]
    s0 = (start // 16) * 16
    n_win = (end - s0 + win - 1) // win

    def window(w, carry):
        lo = s0 + w * win
        off = pl.multiple_of(jnp.minimum(lo, T - win), 16)
        xw = xs_ref[pl.ds(off, win), :]
        ww = ws_ref[pl.ds(off, win), :]
        row = off + lax.broadcasted_iota(jnp.int32, (win, 1), 0)
        y = None
        for j in range(MOE_EPS):
            w_col = jnp.sum(jnp.where(lane == step * MOE_EPS + j, ww, 0.0), axis=-1, keepdims=True)
            w_col = jnp.where(row >= lo, w_col, 0.0)
            a = jnp.dot(xw, wg_ref[0, j], preferred_element_type=f32)
            u = jnp.dot(xw, wu_ref[0, j], preferred_element_type=f32)
            act = (a * jax.nn.sigmoid(a) * u * w_col).astype(bf16)
            yj = jnp.dot(act, wd_ref[0, j], preferred_element_type=f32)
            y = yj if y is None else y + yj
        ys_ref[pl.ds(off, win), :] += y
        return carry

    lax.fori_loop(0, n_win, window, 0)

    @pl.when(step == N_EXPERTS // MOE_EPS - 1)
    def _():
        ct = lax.broadcasted_iota(jnp.int32, (T, T), 1)
        dest_wide = jnp.concatenate([dest_ref[...]] * (T // 128), axis=1)
        unperm = (dest_wide == ct.astype(f32)).astype(bf16)
        f = jnp.dot(unperm, ys_ref[...].astype(bf16), preferred_element_type=f32)
        g2 = m_ref[pl.ds(mod_row_fn(i), 1), 5 * D_MODEL:6 * D_MODEL]
        o_ref[...] = x_ref[...] + g2 * f


def moe_sorted_call(Xmid, H2, RW, mod_l, w_gate, w_up, w_down, *, layer, row0, n_rows, tile, win, mod_row_fn):
    D = Xmid.shape[1]
    F = w_gate.shape[-1]
    b0 = row0 // tile
    tok = lambda w: pl.BlockSpec((tile, w), lambda i, e: (b0 + i, 0))
    kern = functools.partial(_moe_sorted_kernel, win=win, mod_row_fn=mod_row_fn)
    return pl.pallas_call(
        kern,
        grid=(n_rows // tile, N_EXPERTS // MOE_EPS),
        in_specs=[tok(D), tok(D), tok(128),
                  pl.BlockSpec(mod_l.shape, lambda i, e: (0, 0)),
                  pl.BlockSpec((1, MOE_EPS, D, F), lambda i, e: (layer, e, 0, 0)),
                  pl.BlockSpec((1, MOE_EPS, D, F), lambda i, e: (layer, e, 0, 0)),
                  pl.BlockSpec((1, MOE_EPS, F, D), lambda i, e: (layer, e, 0, 0))],
        out_specs=pl.BlockSpec((tile, D), lambda i, e: (i, 0)),
        out_shape=jax.ShapeDtypeStruct((n_rows, D), jnp.float32),
        scratch_shapes=[pltpu.VMEM((tile, D), jnp.bfloat16), pltpu.VMEM((tile, 128), jnp.float32),
                        pltpu.VMEM((tile, D), jnp.float32), pltpu.VMEM((tile, 128), jnp.float32),
                        pltpu.SMEM((2 * N_GROUPS,), jnp.int32)],
        compiler_params=pltpu.CompilerParams(vmem_limit_bytes=VMEM_LIMIT),
        name="moe_sorted",
    )(Xmid, H2, RW, mod_l, w_gate, w_up, w_down)


GDN_TILE = 256
GDN_CPT = GDN_TILE // GDN_CHUNK
GDN_LOCKSTEP = 2


def _mm(a, b, passes=1, dims=None):
    if dims is None:
        dot = lambda x, y: jnp.dot(x, y, preferred_element_type=jnp.float32)
    else:
        dot = lambda x, y: lax.dot_general(x, y, dims, preferred_element_type=jnp.float32)
    if passes == 1:
        return dot(a.astype(jnp.bfloat16), b.astype(jnp.bfloat16))
    a_hi, a_lo = _split_bf16(a)
    b_hi, b_lo = _split_bf16(b)
    return dot(a_hi, b_hi) + dot(a_hi, b_lo) + dot(a_lo, b_hi)


def _stack_heads(x):
    lane = lax.broadcasted_iota(jnp.int32, (1, 256), 1)
    return jnp.concatenate([jnp.where((lane // GDN_DIM) == h, x, 0.0) for h in range(GDN_HEADS)], 0)


def _slab(x):
    return x[0:64] + x[64:128] + x[128:192] + x[192:256]


def _unit_tri_inverse(mats):
    shape = mats[0].shape
    eye = (lax.broadcasted_iota(jnp.int32, shape, 0) == lax.broadcasted_iota(jnp.int32, shape, 1)).astype(jnp.float32)
    ps = [eye - a for a in mats]
    pws = list(mats)
    for _ in range(5):
        pws = [_mm(pw, pw, 3) for pw in pws]
        ps = [p + _mm(p, pw, 3) for p, pw in zip(ps, pws)]
    return ps


def _gdn_chunk_kernel(x_ref, xp_ref, xn_ref, ab_ref, cw_ref, par_ref, g64_ref, qe_ref, ou_ref, mm_ref, nn_ref,
                      *, tiles_per_batch, n_lat_tiles):
    i = pl.program_id(0)
    is_lat = i < n_lat_tiles
    first = jnp.where(is_lat, (i % tiles_per_batch) == 0, True)
    last = jnp.where(is_lat, (i % tiles_per_batch) == tiles_per_batch - 1, True)
    xp = jnp.concatenate([jnp.where(first, 0.0, xp_ref[...]), x_ref[...], jnp.where(last, 0.0, xn_ref[...])], 0)
    y = xp[6:6 + GDN_TILE] * cw_ref[0:1, :]
    for t in range(1, 5):
        y = y + xp[6 + t:6 + t + GDN_TILE] * cw_ref[t:t + 1, :]
    y = y * jax.nn.sigmoid(y)
    g64 = g64_ref[...]
    q = y[:, 0:256]
    k = y[:, 256:512]
    v = y[:, 512:768]
    q = q * lax.rsqrt(_group_sum(q * q, g64) + RMS_EPS) * (GDN_DIM ** -0.5)
    k = k * lax.rsqrt(_group_sum(k * k, g64) + RMS_EPS)
    ab = ab_ref[...]
    lane128 = lax.broadcasted_iota(jnp.int32, (1, 128), 1)
    g_all = jnp.where(lane128 < 8, par_ref[0:1, :] * jax.nn.softplus(ab + par_ref[1:2, :]), 0.0)
    beta_all = jax.nn.sigmoid(ab)
    r64 = lax.broadcasted_iota(jnp.int32, (64, 64), 0)
    c64 = lax.broadcasted_iota(jnp.int32, (64, 64), 1)
    low = (r64 >= c64).astype(jnp.float32)
    upp = (r64 <= c64).astype(jnp.float32)
    rr = lax.broadcasted_iota(jnp.int32, (256, 256), 0)
    cc = lax.broadcasted_iota(jnp.int32, (256, 256), 1)
    same = (rr // 64) == (cc // 64)
    eye = (rr == cc).astype(jnp.float32)
    for c0 in range(0, GDN_CPT, GDN_LOCKSTEP):
        stage = []
        for c in range(c0, c0 + GDN_LOCKSTEP):
            sl = slice(c * GDN_CHUNK, (c + 1) * GDN_CHUNK)
            g_c = g_all[sl]
            gsum = jnp.where(lane128 < 4, jnp.dot(low, g_c, preferred_element_type=jnp.float32, precision=HI),
                             jnp.dot(upp, g_c, preferred_element_type=jnp.float32, precision=HI))
            gsum_t = jnp.transpose(jnp.concatenate([gsum, jnp.zeros_like(gsum)], 0))[:, 0:64]
            tot = jnp.sum(g_c, axis=0, keepdims=True)
            beta_c = beta_all[sl]
            q_sm = _stack_heads(q[sl])
            k_sm = _stack_heads(k[sl])
            v_sm = _stack_heads(v[sl])
            kk = _mm(k_sm, k_sm, 1, NT_DIMS)
            qk = _mm(q_sm, k_sm, 1, NT_DIMS)
            for d in range(2):
                col = lambda h: 4 * d + h
                cmat = jnp.concatenate([jnp.broadcast_to(gsum[:, col(h):col(h) + 1], (64, 256)) for h in range(4)], 0)
                rrow = jnp.concatenate([gsum_t[col(h):col(h) + 1, :] for h in range(4)], 1)
                bmat = jnp.concatenate([jnp.broadcast_to(beta_c[:, 8 + col(h):9 + col(h)], (64, 256))
                                        for h in range(4)], 0)
                tmat = jnp.concatenate([jnp.broadcast_to(tot[:, col(h):col(h) + 1], (64, 256)) for h in range(4)], 0)
                tri = (rr % 64 >= cc % 64) if d == 0 else (rr % 64 <= cc % 64)
                incl = same & tri
                strict = incl & (rr != cc)
                dec = jnp.where(incl, jnp.exp(jnp.where(incl, cmat - rrow, 0.0)), 0.0)
                stage.append((d, sl, q_sm, k_sm, v_sm, qk, cmat, bmat, tmat, dec,
                              jnp.where(strict, kk * bmat * dec, 0.0)))
        t_invs = _unit_tri_inverse([st[-1] for st in stage])
        n = len(stage)
        q_sms, k_sms, v_sms, qks, cmats, bmats, tmats, decs = (
            [st[f] for st in stage] for f in range(2, 10))
        egs = [jnp.exp(cm) for cm in cmats]
        us = [_mm(t_invs[j], v_sms[j] * bmats[j], 1) for j in range(n)]
        ws = [_mm(t_invs[j], k_sms[j] * bmats[j] * egs[j], 1) for j in range(n)]
        qkms = [decs[j] * qks[j] for j in range(n)]
        qes = [q_sms[j] * egs[j] - _mm(qkms[j], ws[j], 1) for j in range(n)]
        ous = [_mm(qkms[j], us[j], 1) for j in range(n)]
        kd_ts = [jnp.transpose(k_sms[j] * jnp.exp(tmats[j] - cmats[j])) for j in range(n)]
        m_bds = [eye * jnp.exp(tmats[j]) - _mm(kd_ts[j], ws[j], 1) for j in range(n)]
        n_bds = [_mm(kd_ts[j], us[j], 1) for j in range(n)]
        for j in range(n):
            d, sl = stage[j][0], stage[j][1]
            qe_ref[d, sl, :] = _slab(qes[j])
            ou_ref[d, sl, :] = _slab(ous[j])
            mm_ref[d, sl, :] = _slab(m_bds[j])
            nn_ref[d, sl, :] = _slab(n_bds[j])


def gdn_chunk_call(P_gdn, P_ab, conv_w, a_log, dt_bias, *, n_batch, seq):
    NT = P_gdn.shape[0]
    tpb = seq // GDN_TILE
    n_lat = n_batch * tpb
    n_tiles = NT // GDN_TILE
    hb = GDN_TILE // 8
    cw = jnp.zeros((8, 768), jnp.float32).at[:5].set(conv_w)
    par = jnp.zeros((8, 128), jnp.float32)
    par = par.at[0, :8].set(-jnp.exp(a_log.reshape(8))).at[1, :8].set(dt_bias.reshape(8))
    g64 = _group_ones(256, 64)
    kern = functools.partial(_gdn_chunk_kernel, tiles_per_batch=tpb, n_lat_tiles=n_lat)
    full = lambda a: pl.BlockSpec(a.shape, lambda i: (0,) * a.ndim)
    out = pl.BlockSpec((2, GDN_TILE, 256), lambda i: (0, i, 0))
    return pl.pallas_call(
        kern,
        grid=(n_tiles,),
        in_specs=[pl.BlockSpec((GDN_TILE, 768), lambda i: (i, 0)),
                  pl.BlockSpec((8, 768), lambda i: (jnp.maximum(i * hb - 1, 0), 0)),
                  pl.BlockSpec((8, 768), lambda i: (jnp.minimum((i + 1) * hb, n_tiles * hb - 1), 0)),
                  pl.BlockSpec((GDN_TILE, 128), lambda i: (i, 0)),
                  full(cw), full(par), full(g64)],
        out_specs=[out] * 4,
        out_shape=[jax.ShapeDtypeStruct((2, NT, 256), jnp.float32)] * 4,
        compiler_params=pltpu.CompilerParams(vmem_limit_bytes=VMEM_LIMIT),
        name="gdn_chunk",
    )(P_gdn, P_gdn, P_gdn, P_ab, cw, par, g64)


def _gdn_scan_kernel(qef_ref, ouf_ref, mf_ref, nf_ref, qeb_ref, oub_ref, mb_ref, nb_ref, of_ref, ob_ref, s_ref):
    t = pl.program_id(1)

    @pl.when(t == 0)
    def _():
        s_ref[...] = jnp.zeros(s_ref.shape, jnp.float32)

    def step(d, c, qe_ref, ou_ref, m_ref, n_ref, o_ref):
        sl = slice(c * GDN_CHUNK, (c + 1) * GDN_CHUNK)
        s = s_ref[d]
        o_ref[sl, :] = _slab(_mm(_stack_heads(qe_ref[0, sl, :]), s, 1)) + ou_ref[0, sl, :]
        s_ref[d] = _mm(_stack_heads(m_ref[0, sl, :]), s, 3) + _stack_heads(n_ref[0, sl, :])

    for c in range(GDN_CPT):
        step(0, c, qef_ref, ouf_ref, mf_ref, nf_ref, of_ref)
        step(1, GDN_CPT - 1 - c, qeb_ref, oub_ref, mb_ref, nb_ref, ob_ref)


def gdn_scan_call(QE, OU, MM, NN, *, n_batch, seq):
    NT = QE.shape[1]
    tpb = seq // GDN_TILE
    cblk = n_batch * tpb
    fwd = lambda b, t: jnp.where(t == 0, cblk + b, b * tpb + t - 1)
    bwd = lambda b, t: jnp.where(t == 0, cblk + b, b * tpb + tpb - t)
    spec = lambda d, f: pl.BlockSpec((1, GDN_TILE, 256), lambda b, t: (d, f(b, t), 0))
    ospec = lambda f: pl.BlockSpec((GDN_TILE, 256), lambda b, t: (f(b, t), 0))
    return pl.pallas_call(
        _gdn_scan_kernel,
        grid=(n_batch, tpb + 1),
        in_specs=[spec(0, fwd)] * 4 + [spec(1, bwd)] * 4,
        out_specs=[ospec(fwd), ospec(bwd)],
        out_shape=[jax.ShapeDtypeStruct((NT, 256), jnp.float32)] * 2,
        scratch_shapes=[pltpu.VMEM((2, 256, 256), jnp.float32)],
        compiler_params=pltpu.CompilerParams(vmem_limit_bytes=VMEM_LIMIT),
        name="gdn_scan",
    )(QE, OU, MM, NN, QE, OU, MM, NN)


def _dft_cs(n):
    a = 2.0 * np.pi * np.outer(np.arange(n), np.arange(n)) / n
    return np.cos(a), np.sin(a)


def _channel_dft():
    c, s = _dft_cs(FNET_GROUP_W)
    eye = np.eye(FNET_GROUPS)
    return np.concatenate([np.kron(eye, c), np.kron(eye, s)], 1)


FNET_SUB = 16


def _fnet_a_kernel(x_ref, cs_ref, m_ref, cphi_ref, sphi_ref, br_ref, bi_ref, *, n_rows):
    for j in range(FNET_SUB):
        sl = slice(j * BRANCH_W, (j + 1) * BRANCH_W)
        u = jnp.dot(x_ref[:, j, :], cs_ref[...], preferred_element_type=jnp.float32)
        st = jnp.concatenate([u[:, :BRANCH_W], u[:, BRANCH_W:]], 0).astype(jnp.bfloat16)
        a = jnp.dot(m_ref[...], st, preferred_element_type=jnp.float32)
        ar, ai = a[:n_rows], a[n_rows:]
        cp, sp = cphi_ref[:, sl], sphi_ref[:, sl]
        br_ref[j] = ar * cp - ai * sp
        bi_ref[j] = ar * sp + ai * cp


def _fnet_c_kernel(br_ref, bi_ref, m_ref, o_ref):
    for j in range(FNET_SUB):
        st = jnp.concatenate([br_ref[:, j, :], bi_ref[:, j, :]], 0).astype(jnp.bfloat16)
        o_ref[:, j, :] = jnp.dot(m_ref[...], st, preferred_element_type=jnp.float32)


def fourier_latent_call(P_f, *, n_batch, seq):
    rows = seq // GRID_W
    bf16, f32 = jnp.bfloat16, jnp.float32
    c1, s1 = _dft_cs(rows)
    norm = 1.0 / math.sqrt(seq * FNET_GROUP_W)
    m_a = jnp.asarray(np.block([[c1, -s1], [s1, c1]]) * norm, bf16)
    c2, s2 = _dft_cs(GRID_W)
    m_c = jnp.asarray(np.concatenate([c2, -s2], 1), bf16)
    cs = jnp.asarray(_channel_dft(), bf16)
    phi = 2.0 * np.pi * np.outer(np.arange(rows), np.arange(GRID_W)) / seq
    cphi = jnp.repeat(jnp.asarray(np.cos(phi), f32), BRANCH_W, axis=1)
    sphi = jnp.repeat(jnp.asarray(np.sin(phi), f32), BRANCH_W, axis=1)
    xv = P_f.reshape(P_f.shape[0] // GRID_W, GRID_W, BRANCH_W)
    nj = GRID_W // FNET_SUB
    full = lambda a: pl.BlockSpec(a.shape, lambda b, j: (0,) * a.ndim)
    tw = pl.BlockSpec((rows, FNET_SUB * BRANCH_W), lambda b, j: (0, j))
    mid = jax.ShapeDtypeStruct((n_batch * GRID_W, rows, BRANCH_W), f32)
    br, bi = pl.pallas_call(
        functools.partial(_fnet_a_kernel, n_rows=rows),
        grid=(n_batch, nj),
        in_specs=[pl.BlockSpec((rows, FNET_SUB, BRANCH_W), lambda b, j: (b, j, 0)), full(cs), full(m_a), tw, tw],
        out_specs=[pl.BlockSpec((FNET_SUB, rows, BRANCH_W), lambda b, j: (b * nj + j, 0, 0))] * 2,
        out_shape=[mid, mid],
        name="fnet_rows",
    )(xv, cs, m_a, cphi, sphi)
    blk = pl.BlockSpec((GRID_W, FNET_SUB, BRANCH_W), lambda b, i: (b, i, 0))
    y = pl.pallas_call(
        _fnet_c_kernel,
        grid=(n_batch, rows // FNET_SUB),
        in_specs=[blk, blk, pl.BlockSpec(m_c.shape, lambda b, i: (0, 0))],
        out_specs=blk,
        out_shape=mid,
        name="fnet_cols",
    )(br, bi, m_c)
    return y.reshape(n_batch * seq, BRANCH_W)


def _fnet_ctx_kernel(x_ref, cs_ref, m_ref, o_ref):
    u = jnp.dot(x_ref[...], cs_ref[...], preferred_element_type=jnp.float32)
    st = jnp.concatenate([u[:, :BRANCH_W], u[:, BRANCH_W:]], 0).astype(jnp.bfloat16)
    o_ref[...] = jnp.dot(m_ref[...], st, preferred_element_type=jnp.float32).astype(o_ref.dtype)


def fourier_ctx_call(P_f, *, n_batch, seq, ctx_len):
    bf16 = jnp.bfloat16
    c, s = _dft_cs(ctx_len)
    m = jnp.asarray(np.concatenate([c, -s], 1) / math.sqrt(ctx_len * FNET_GROUP_W), bf16)
    cs = jnp.asarray(_channel_dft(), bf16)
    cb = n_batch * seq // ctx_len
    return pl.pallas_call(
        _fnet_ctx_kernel,
        grid=(n_batch,),
        in_specs=[pl.BlockSpec((ctx_len, BRANCH_W), lambda b: (cb + b, 0)),
                  pl.BlockSpec(cs.shape, lambda b: (0, 0)), pl.BlockSpec(m.shape, lambda b: (0, 0))],
        out_specs=pl.BlockSpec((ctx_len, BRANCH_W), lambda b: (b, 0)),
        out_shape=jax.ShapeDtypeStruct((n_batch * ctx_len, BRANCH_W), jnp.float32),
        name="fnet_ctx",
    )(P_f, cs, m)


def kernel(x, c, ctx, c_ctx, w_mod, b_mod, norm1_g, norm2_g, w_in, na_qn_g, na_kn_g, na_rpb,
           da_qn_g, da_kn_g, da_lam_q1, da_lam_k1, da_lam_q2, da_lam_k2, da_subln_g, gdn_conv_w,
           gdn_a_log, gdn_dt_bias, gdn_onorm_g, w_branch, w_out, w_router, router_bias,
           moe_w_gate, moe_w_up, moe_w_down):
    B, S, D = x.shape
    L = ctx.shape[1]
    NL = B * S
    f32 = jnp.float32
    cvec = jnp.zeros((8, D), f32).at[:B].set(c).at[B].set(c_ctx)
    mods = mod_call(cvec, w_mod, b_mod)
    cos, sin = rope_tables(S, TOKEN_TILE)
    bf16 = jnp.bfloat16
    Xl, Xc = x.reshape(NL, D), ctx.reshape(B * L, D)
    experts = (moe_w_gate.astype(bf16), moe_w_up.astype(bf16), moe_w_down.astype(bf16))
    w_t = jnp.swapaxes(w_in, 1, 2).astype(bf16)
    for l in range(DEPTH):
        last = l == DEPTH - 1
        lam_init = 0.8 - 0.6 * math.exp(-0.3 * l)
        NAQ, NAK, NAV, DAQ, DAK, DAVX, P_gdn, P_z, P_f, P_gate, P_ab = inproj_call(
            Xl, Xc, 0, mods[l], norm1_g[l], w_t, l, cos, sin, na_qn_g[l], na_kn_g[l], da_qn_g[l], da_kn_g[l],
            n_batch=B, seq=S, ctx_len=L)
        bias = na_bias_tables(na_rpb[l], S // GRID_W)
        lam = (jnp.exp(jnp.sum(da_lam_q1[l] * da_lam_k1[l])) - jnp.exp(jnp.sum(da_lam_q2[l] * da_lam_k2[l]))
               + lam_init).astype(f32)
        Ya = na_call(NAQ, NAK, NAV, bias, n_batch=B, seq=S, ctx_len=L)
        Yd = da_call(lam, DAQ, DAK, DAVX, da_subln_g[l], n_batch=B, seq=S, ctx_len=L, lam_init=lam_init)
        QE, OU, MM, NN = gdn_chunk_call(P_gdn, P_ab, gdn_conv_w[l], gdn_a_log[l], gdn_dt_bias[l], n_batch=B, seq=S)
        Of, Ob = gdn_scan_call(QE, OU, MM, NN, n_batch=B, seq=S)
        Yf = fourier_latent_call(P_f, n_batch=B, seq=S)
        n_tok = NL
        Yac, Ydc, Yfc = Ya, Yd, Yf
        if not last:
            Yac, Ydc = ctx_attn_call(lam, NAQ, NAK, NAV, DAQ, DAK, DAVX, da_subln_g[l],
                                     n_batch=B, seq=S, ctx_len=L, lam_init=lam_init)
            Yfc = fourier_ctx_call(P_f, n_batch=B, seq=S, ctx_len=L)
            n_tok = NL + B * L
        Xmid, H2, RW = merge_call(Xl, Xc, 0, (Ya, Yac), (Yd, Ydc), (Yf, Yfc), Of, Ob, P_z, P_gate, mods[l],
                                  norm2_g[l], gdn_onorm_g[l], w_branch[l].astype(bf16), w_out[l].astype(bf16),
                                  w_router, router_bias, n_tok=n_tok, n_batch=B, seq=S)
        Xl = moe_sorted_call(Xmid, H2, RW, mods[l], *experts, layer=l, row0=0, n_rows=NL, tile=MOE_TILE, win=MOE_WIN,
                             mod_row_fn=lambda i: i // (S // MOE_TILE))
        if not last:
            Xc = moe_sorted_call(Xmid, H2, RW, mods[l], *experts, layer=l, row0=NL, n_rows=B * L, tile=B * L,
                                 win=MOE_WIN_CTX, mod_row_fn=lambda i: B)
    return Xl.reshape(B, S, D)
```
